```python
import math
import jax, jax.numpy as jnp
from jax import lax
import numpy as np

D_MODEL = 1024
BATCH = 16
SEQ = 2048
DEPTH = 1

HEAD_DIM = 64
N_ATTN_HEADS = 8
N_HGRN_HEADS = 8
ATTN_WIDTH = N_ATTN_HEADS * HEAD_DIM
HGRN_WIDTH = N_HGRN_HEADS * HEAD_DIM
MIX_WIDTH = ATTN_WIDTH + HGRN_WIDTH
IN_PROJ_WIDTH = 3 * ATTN_WIDTH + 4 * HGRN_WIDTH
DILATED_CONFIGS = ((128, 1), (512, 4), (2048, 16))
ATTN_BLOCK = 128
HGRN_CHUNK = 32
N_GROUPS = 4
EXPERTS_PER_GROUP = 8
N_EXPERTS = N_GROUPS * EXPERTS_PER_GROUP
EXPERT_TOP_K = 2
D_EXPERT = 512
MOE_BLOCK = 128
EPS = 1e-6

kernel_name = "hybrid_dilated_attn_hgrn2_hmoe"


def rmsnorm(x, g):
    xf = x.astype(jnp.float32)
    y = xf * lax.rsqrt(jnp.mean(xf * xf, axis=-1, keepdims=True) + EPS)
    return (y * g.astype(jnp.float32)).astype(x.dtype)


def alibi_slopes(n_heads):
    return 2.0 ** (-8.0 * jnp.arange(1, n_heads + 1, dtype=jnp.float32) / n_heads)


def dilated_branch(q, k, v, slopes, window, dilation):
    B, S, H, Dh = q.shape
    n_keys = window // dilation
    assert n_keys <= ATTN_BLOCK
    U = S // dilation
    nb = -(-U // ATTN_BLOCK)
    Up = nb * ATTN_BLOCK

    def to_blocks(t):
        t = t.reshape(B, U, dilation, H, Dh).transpose(0, 2, 1, 3, 4)
        t = jnp.pad(t, ((0, 0), (0, 0), (0, Up - U), (0, 0), (0, 0)))
        return t.reshape(B, dilation, nb, ATTN_BLOCK, H, Dh)

    def from_blocks(t):
        rest = t.shape[4:]
        t = t.reshape((B, dilation, Up) + rest)[:, :, :U]
        return jnp.swapaxes(t, 1, 2).reshape((B, S) + rest)

    def with_prev(t):
        prev = jnp.pad(t, ((0, 0), (0, 0), (1, 0), (0, 0), (0, 0), (0, 0)))[:, :, :-1]
        return jnp.concatenate([prev, t], axis=3)

    qb = to_blocks(q)
    kk = with_prev(to_blocks(k))
    vv = with_prev(to_blocks(v))
    scores = jnp.einsum('bgnqhd,bgnkhd->bgnhqk', qb, kk,
                        preferred_element_type=jnp.float32) / math.sqrt(Dh)
    qi = jnp.arange(ATTN_BLOCK)[:, None]
    ki = jnp.arange(2 * ATTN_BLOCK)[None, :]
    sub_dist = qi + ATTN_BLOCK - ki
    band = (sub_dist >= 0) & (sub_dist <= n_keys)
    key_pos = jnp.arange(nb)[:, None, None] * ATTN_BLOCK + ki[None] - ATTN_BLOCK
    valid = band[None] & (key_pos >= 0)
    bias = -slopes[:, None, None] * (sub_dist * dilation).astype(jnp.float32)[None]
    scores = jnp.where(valid[None, None, :, None], scores + bias[None, None, None], -jnp.inf)
    lse = jax.nn.logsumexp(scores, axis=-1)
    p = jnp.exp(scores - lse[..., None])
    o = jnp.einsum('bgnhqk,bgnkhd->bgnqhd', p, vv.astype(jnp.float32))
    return from_blocks(o), from_blocks(jnp.swapaxes(lse, -1, -2))


def dilated_attention(q, k, v):
    slopes = alibi_slopes(q.shape[2])
    outs, lses = [], []
    for window, dilation in DILATED_CONFIGS:
        o_c, lse_c = dilated_branch(q, k, v, slopes, window, dilation)
        outs.append(o_c)
        lses.append(lse_c)
    w = jax.nn.softmax(jnp.stack(lses, axis=0), axis=0)
    return jnp.sum(w[..., None] * jnp.stack(outs, axis=0), axis=0)


def hgrn2(q, f_raw, i_in, lb):
    B, S, H, Dk = q.shape
    Dv = i_in.shape[-1]
    C = HGRN_CHUNK
    nC = S // C
    f32 = jnp.float32
    q, f_raw, i_in = q.astype(f32), f_raw.astype(f32), i_in.astype(f32)
    lb = lb.reshape(H, Dk)
    log_f = jnp.log(lb + (1.0 - lb) * jax.nn.sigmoid(f_raw))
    key = (1.0 - lb) * jax.nn.sigmoid(-f_raw)
    q = jax.nn.silu(q)
    rs = lambda t: t.reshape(B, nC, C, H, t.shape[-1])
    qc, kc, vc, lfc = rs(q), rs(key), rs(i_in), rs(log_f)
    b = jnp.cumsum(lfc, axis=2)
    b_last = b[:, :, -1:]
    q_in = qc * jnp.exp(b)
    k_in = kc * jnp.exp(-b)
    k_end = kc * jnp.exp(b_last - b)
    causal = jnp.tril(jnp.ones((C, C), dtype=bool))
    A = jnp.where(causal, jnp.einsum('bnthk,bnshk->bnhts', q_in, k_in), 0.0)
    o_intra = jnp.einsum('bnhts,bnshv->bnthv', A, vc)
    dS = jnp.einsum('bnshk,bnshv->bnhkv', k_end, vc)
    decay = jnp.exp(b_last[:, :, 0])

    def step(S_prev, inp):
        dec, ds = inp
        return dec[..., None] * S_prev + ds, S_prev

    S0 = jnp.zeros((B, H, Dk, Dv), f32)
    _, S_prevs = lax.scan(step, S0, (jnp.swapaxes(decay, 0, 1), jnp.swapaxes(dS, 0, 1)))
    S_prevs = jnp.swapaxes(S_prevs, 0, 1)
    o_inter = jnp.einsum('bnthk,bnhkv->bnthv', q_in, S_prevs)
    return (o_intra + o_inter).reshape(B, S, H, Dv)


def hier_moe(h, w_group, b_group, w_router, b_router, w_gate, w_up, w_down):
    B, S, D = h.shape
    N = B * S
    f32 = jnp.float32
    xf = h.reshape(N, D)
    grp_logits = (xf @ w_group).astype(f32) + b_group.astype(f32)
    p_grp = jax.nn.softmax(grp_logits, axis=-1)
    g_sel = jnp.argmax(grp_logits, axis=-1).astype(jnp.int32)
    w_g = jnp.take_along_axis(p_grp, g_sel[:, None], axis=1)
    exp_logits = ((xf @ w_router).astype(f32) + b_router.astype(f32)).reshape(N, N_GROUPS, EXPERTS_PER_GROUP)
    sel_logits = jnp.take_along_axis(exp_logits, g_sel[:, None, None], axis=1)[:, 0]
    top_vals, top_idx = lax.top_k(sel_logits, EXPERT_TOP_K)
    gate = w_g * jax.nn.softmax(top_vals, axis=-1)
    eid = (g_sel[:, None] * EXPERTS_PER_GROUP + top_idx).reshape(-1).astype(jnp.int32)
    tok = jnp.repeat(jnp.arange(N, dtype=jnp.int32), EXPERT_TOP_K)
    A_n = N * EXPERT_TOP_K
    order = jnp.argsort(eid)
    eid_s, tok_s, gate_s = eid[order], tok[order], gate.reshape(-1)[order]
    counts = jnp.bincount(eid, length=N_EXPERTS).astype(jnp.int32)
    starts = jnp.cumsum(counts) - counts
    padded = ((counts + MOE_BLOCK - 1) // MOE_BLOCK) * MOE_BLOCK
    pad_ends = jnp.cumsum(padded)
    pad_starts = pad_ends - padded
    dest = pad_starts[eid_s] + jnp.arange(A_n, dtype=jnp.int32) - starts[eid_s]
    P = (-(-A_n // MOE_BLOCK)) * MOE_BLOCK + N_EXPERTS * MOE_BLOCK
    n_blocks = P // MOE_BLOCK
    buf_tok = jnp.full((P,), N, jnp.int32).at[dest].set(tok_s)
    buf_gate = jnp.zeros((P,), f32).at[dest].set(gate_s)
    block_expert = jnp.minimum(
        jnp.searchsorted(pad_ends, jnp.arange(n_blocks, dtype=jnp.int32) * MOE_BLOCK, side='right'),
        N_EXPERTS - 1).astype(jnp.int32)
    x_pad = jnp.concatenate([xf, jnp.zeros((1, D), xf.dtype)], axis=0)

    def run_block(args):
        e, toks, gts = args
        xb = x_pad[toks]
        hid = jax.nn.silu(xb @ w_gate[e]) * (xb @ w_up[e])
        return ((hid @ w_down[e]) * gts[:, None].astype(xb.dtype)).astype(h.dtype)

    y = lax.map(run_block, (block_expert, buf_tok.reshape(n_blocks, MOE_BLOCK),
                            buf_gate.reshape(n_blocks, MOE_BLOCK)))
    out = jnp.zeros((N + 1, D), h.dtype).at[buf_tok].add(y.reshape(P, D))[:N]
    return out.reshape(B, S, D)


def setup_inputs(seed: int = 0) -> dict:
    key = jax.random.key(seed)
    ks = jax.random.split(key, 16)
    f32 = jnp.float32
    nrm = lambda k, shape, scale: jax.random.normal(k, shape, f32) * scale
    L = DEPTH
    return {
        "x": jax.random.normal(ks[0], (BATCH, SEQ, D_MODEL), f32),
        "norm1_g": 1.0 + nrm(ks[1], (L, D_MODEL), 0.05),
        "w_in": nrm(ks[2], (L, D_MODEL, IN_PROJ_WIDTH), D_MODEL ** -0.5),
        "attn_norm_g": 1.0 + nrm(ks[3], (L, ATTN_WIDTH), 0.05),
        "hgrn_gamma": nrm(ks[4], (L + 1, HGRN_WIDTH), 0.1),
        "hgrn_norm_g": 1.0 + nrm(ks[5], (L, HGRN_WIDTH), 0.05),
        "w_out": nrm(ks[6], (L, MIX_WIDTH, D_MODEL), MIX_WIDTH ** -0.5),
        "norm2_g": 1.0 + nrm(ks[7], (L, D_MODEL), 0.05),
        "w_group": nrm(ks[8], (L, D_MODEL, N_GROUPS), D_MODEL ** -0.5),
        "b_group": nrm(ks[9], (L, N_GROUPS), 0.01),
        "w_router": nrm(ks[10], (L, D_MODEL, N_EXPERTS), D_MODEL ** -0.5),
        "b_router": nrm(ks[11], (L, N_EXPERTS), 0.01),
        "w_gate": nrm(ks[12], (L, N_EXPERTS, D_MODEL, D_EXPERT), D_MODEL ** -0.5),
        "w_up": nrm(ks[13], (L, N_EXPERTS, D_MODEL, D_EXPERT), D_MODEL ** -0.5),
        "w_down": nrm(ks[14], (L, N_EXPERTS, D_EXPERT, D_MODEL), D_EXPERT ** -0.5),
        "norm_f_g": 1.0 + nrm(ks[15], (D_MODEL,), 0.05),
    }


def reference(x, norm1_g, w_in, attn_norm_g, hgrn_gamma, hgrn_norm_g, w_out, norm2_g,
              w_group, b_group, w_router, b_router, w_gate, w_up, w_down, norm_f_g):
    B, S, _ = x.shape
    split_at = [ATTN_WIDTH, 2 * ATTN_WIDTH, 3 * ATTN_WIDTH,
                3 * ATTN_WIDTH + HGRN_WIDTH, 3 * ATTN_WIDTH + 2 * HGRN_WIDTH,
                3 * ATTN_WIDTH + 3 * HGRN_WIDTH]
    lower_bounds = jnp.cumsum(jax.nn.softmax(hgrn_gamma.astype(jnp.float32), axis=0), axis=0)
    for layer in range(DEPTH):
        h = rmsnorm(x, norm1_g[layer])
        proj = h @ w_in[layer]
        qa, ka, va, qh, fh, ih, gh = jnp.split(proj, split_at, axis=-1)
        heads = lambda t, n: t.reshape(B, S, n, HEAD_DIM)
        o_attn = dilated_attention(heads(qa, N_ATTN_HEADS), heads(ka, N_ATTN_HEADS),
                                   heads(va, N_ATTN_HEADS)).reshape(B, S, ATTN_WIDTH)
        y_attn = rmsnorm(o_attn, attn_norm_g[layer])
        o_h = hgrn2(heads(qh, N_HGRN_HEADS), heads(fh, N_HGRN_HEADS), heads(ih, N_HGRN_HEADS),
                    lower_bounds[layer])
        o_h = rmsnorm(o_h, hgrn_norm_g[layer].reshape(N_HGRN_HEADS, HEAD_DIM)).reshape(B, S, HGRN_WIDTH)
        y_hgrn = o_h * jax.nn.silu(gh.astype(jnp.float32))
        mixed = jnp.concatenate([y_attn, y_hgrn], axis=-1).astype(x.dtype)
        x = x + mixed @ w_out[layer]
        h2 = rmsnorm(x, norm2_g[layer])
        x = x + hier_moe(h2, w_group[layer], b_group[layer], w_router[layer], b_router[layer],
                         w_gate[layer], w_up[layer], w_down[layer])
    return rmsnorm(x, norm_f_g)
```

```python
import functools

import jax
import jax.numpy as jnp
from jax import lax
from jax.experimental import pallas as pl
from jax.experimental.pallas import tpu as pltpu

F32 = jnp.float32
BF16 = jnp.bfloat16
I32 = jnp.int32

D_MODEL = 1024
HEAD_DIM = 64
N_HEADS = 8
WIDTH = N_HEADS * HEAD_DIM
QKV_W = 3 * WIDTH
HG_W = 4 * WIDTH
SEQ = 2048
ATTN_BLOCK = 128
DILATIONS = (1, 4, 16)
HGRN_CHUNK = 32
HGRN_SUPER = 256
N_GROUPS = 4
EXPERTS_PER_GROUP = 8
N_EXPERTS = 32
D_EXPERT = 512
EXPERT_LANE0 = 32
FFN_BLOCK = 256
EPS = 1e-6
NEG = -1e30
LANES = 128
VMEM_LIMIT = 56 * 1024 * 1024


def _cparams(*sem):
    return pltpu.CompilerParams(dimension_semantics=sem, vmem_limit_bytes=VMEM_LIMIT)


def _in_proj_kernel(x_ref, g_ref, w_ref, qkv_ref, hg_ref):
    x = x_ref[...]
    ms = jnp.mean(x * x, axis=-1, keepdims=True)
    h = (x * lax.rsqrt(ms + EPS) * g_ref[...]).astype(BF16)
    for j in range(QKV_W // WIDTH):
        sl = slice(j * WIDTH, (j + 1) * WIDTH)
        qkv_ref[:, sl] = jnp.dot(h, w_ref[:, sl], preferred_element_type=F32)
    for j in range(HG_W // WIDTH):
        sl = slice(j * WIDTH, (j + 1) * WIDTH)
        wsl = slice(QKV_W + j * WIDTH, QKV_W + (j + 1) * WIDTH)
        hg_ref[:, sl] = jnp.dot(h, w_ref[:, wsl], preferred_element_type=F32).astype(BF16)


def in_proj(x2, g, w_bf16, tm=512):
    n = x2.shape[0]
    return pl.pallas_call(
        _in_proj_kernel,
        grid=(n // tm,),
        in_specs=[
            pl.BlockSpec((tm, D_MODEL), lambda i: (i, 0)),
            pl.BlockSpec((1, D_MODEL), lambda i: (0, 0)),
            pl.BlockSpec((D_MODEL, QKV_W + HG_W), lambda i: (0, 0)),
        ],
        out_specs=[
            pl.BlockSpec((tm, QKV_W), lambda i: (i, 0)),
            pl.BlockSpec((tm, HG_W), lambda i: (i, 0)),
        ],
        out_shape=[
            jax.ShapeDtypeStruct((n, QKV_W), F32),
            jax.ShapeDtypeStruct((n, HG_W), BF16),
        ],
        compiler_params=_cparams("arbitrary"),
        name="in_proj",
    )(x2, g.reshape(1, D_MODEL), w_bf16)


ATTN_HEADS_PER_STEP = 2
ATTN_W = ATTN_HEADS_PER_STEP * HEAD_DIM


def _attn_bias_tables():
    qi = jnp.arange(ATTN_BLOCK)[:, None]
    ki = jnp.arange(2 * ATTN_BLOCK)[None, :]
    dist = qi + ATTN_BLOCK - ki
    valid = (dist >= 0) & (dist <= ATTN_BLOCK)
    slopes = 2.0 ** (-8.0 * jnp.arange(1, N_HEADS + 1, dtype=F32) / N_HEADS)
    tabs = []
    for dil in DILATIONS:
        b = -slopes[:, None, None] * (dist * dil).astype(F32)[None]
        tabs.append(jnp.where(valid[None], b, NEG))
    return jnp.stack(tabs, axis=0)


def _attn_head(qb, kb, vb, bias):
    s = lax.dot_general(qb, kb, (((1,), (1,)), ((), ())), preferred_element_type=F32) + bias
    m = jnp.max(s, axis=-1, keepdims=True)
    p = jnp.exp(s - m)
    l = jnp.sum(p, axis=-1, keepdims=True)
    o = jnp.dot(p.astype(BF16), vb, preferred_element_type=F32) * (1.0 / l)
    return o, m + jnp.log(l)


def _attn_block(q_ref, k_ref, v_ref, bias_ref, d_idx, q_rows, kv_rows, with_prev):
    q = (q_ref[q_rows, :] * (HEAD_DIM ** -0.5)).astype(BF16)
    k = k_ref[kv_rows, :].astype(BF16)
    v = v_ref[kv_rows, :].astype(BF16)
    outs, lses = [], []
    for h in range(ATTN_HEADS_PER_STEP):
        sl = slice(h * HEAD_DIM, (h + 1) * HEAD_DIM)
        if with_prev:
            bias = bias_ref[d_idx, h]
        else:
            bias = bias_ref[d_idx, h, :, ATTN_BLOCK:]
        o, lse = _attn_head(q[:, sl], k[:, sl], v[:, sl], bias)
        outs.append(o)
        lses.append(lse)
    lane = lax.broadcasted_iota(I32, (ATTN_BLOCK, ATTN_W), 1)
    return jnp.concatenate(outs, axis=-1), jnp.where(lane < HEAD_DIM, lses[0], lses[1])


def _attn_kernel(q_ref, k_ref, v_ref, bias_ref, o_ref, o4_ref, o16_ref, lse4_ref, lse16_ref):
    blk = ATTN_BLOCK

    def store_branch(o_sc, lse_sc, rows, o, lse):
        o_sc[rows, :] = o
        lse_sc[rows, :] = lse

    def body16(r, c):
        rows = pl.ds(r, blk, stride=16)
        o, lses = _attn_block(q_ref, k_ref, v_ref, bias_ref, 2, rows, rows, False)
        store_branch(o16_ref, lse16_ref, rows, o, lses)
        return c
    lax.fori_loop(0, 16, body16, 0)

    def body4_first(r, c):
        rows = pl.ds(r, blk, stride=4)
        o, lses = _attn_block(q_ref, k_ref, v_ref, bias_ref, 1, rows, rows, False)
        store_branch(o4_ref, lse4_ref, rows, o, lses)
        return c
    lax.fori_loop(0, 4, body4_first, 0)

    def body4(i, c):
        r = i // 3
        nblk = i % 3 + 1
        q_rows = pl.ds(r + 4 * blk * nblk, blk, stride=4)
        kv_rows = pl.ds(r + 4 * blk * (nblk - 1), 2 * blk, stride=4)
        o, lses = _attn_block(q_ref, k_ref, v_ref, bias_ref, 1, q_rows, kv_rows, True)
        store_branch(o4_ref, lse4_ref, q_rows, o, lses)
        return c
    lax.fori_loop(0, 12, body4, 0)

    def merge_and_store(rows, o1, l1):
        l4 = lse4_ref[rows, :]
        l16 = lse16_ref[rows, :]
        mx = jnp.maximum(jnp.maximum(l1, l4), l16)
        e1 = jnp.exp(l1 - mx)
        e4 = jnp.exp(l4 - mx)
        e16 = jnp.exp(l16 - mx)
        inv = 1.0 / (e1 + e4 + e16)
        o = (e1 * inv) * o1 + (e4 * inv) * o4_ref[rows, :] + (e16 * inv) * o16_ref[rows, :]
        o_ref[rows, :] = o.astype(o_ref.dtype)

    rows0 = pl.ds(0, blk)
    o, lses = _attn_block(q_ref, k_ref, v_ref, bias_ref, 0, rows0, rows0, False)
    merge_and_store(rows0, o, lses)

    def body1(nblk, c):
        q_rows = pl.ds(pl.multiple_of(nblk * blk, blk), blk)
        kv_rows = pl.ds(pl.multiple_of((nblk - 1) * blk, blk), 2 * blk)
        o, lses = _attn_block(q_ref, k_ref, v_ref, bias_ref, 0, q_rows, kv_rows, True)
        merge_and_store(q_rows, o, lses)
        return c
    lax.fori_loop(1, SEQ // blk, body1, 0)


def attention(qkv, batch):
    qkv3 = qkv.reshape(batch, SEQ, QKV_W)
    bias = _attn_bias_tables()
    n_hg = WIDTH // ATTN_W
    col = lambda which: (lambda b, g: (b, 0, which * n_hg + g))
    return pl.pallas_call(
        _attn_kernel,
        grid=(batch, n_hg),
        in_specs=[
            pl.BlockSpec((None, SEQ, ATTN_W), col(0)),
            pl.BlockSpec((None, SEQ, ATTN_W), col(1)),
            pl.BlockSpec((None, SEQ, ATTN_W), col(2)),
            pl.BlockSpec((len(DILATIONS), ATTN_HEADS_PER_STEP, ATTN_BLOCK, 2 * ATTN_BLOCK),
                         lambda b, g: (0, g, 0, 0)),
        ],
        out_specs=pl.BlockSpec((None, SEQ, ATTN_W), lambda b, g: (b, 0, g)),
        out_shape=jax.ShapeDtypeStruct((batch, SEQ, WIDTH), BF16),
        scratch_shapes=[pltpu.VMEM((SEQ, ATTN_W), F32) for _ in range(4)],
        compiler_params=_cparams("arbitrary", "arbitrary"),
        name="dilated_attention",
    )(qkv3, qkv3, qkv3, bias)


HG_HEADS_PER_STEP = 4
HG_W_STEP = HG_HEADS_PER_STEP * HEAD_DIM


def _split3(x):
    a = x.astype(BF16)
    r = x - a.astype(F32)
    b = r.astype(BF16)
    c = (r - b.astype(F32)).astype(BF16)
    return a, b, c


def _hgrn_kernel(q_ref, f_ref, i_ref, g_ref, gamma_ref, ng_ref, tri_ref, ones_ref, y_ref):
    sup, ch = HGRN_SUPER, HGRN_CHUNK
    n_ch = sup // ch
    gam = gamma_ref[...]
    gmx = jnp.max(gam, axis=0, keepdims=True)
    ge = jnp.exp(gam - gmx)
    lb = ge[0:1] / jnp.sum(ge, axis=0, keepdims=True)
    ng = ng_ref[...]
    tri = tri_ref[...]
    ones_bd = ones_ref[...]
    ri = lax.broadcasted_iota(I32, (sup, sup), 0)
    ci = lax.broadcasted_iota(I32, (sup, sup), 1)
    causal_bd = (ri // ch == ci // ch) & (ri >= ci)

    def body(sb, state):
        rows = pl.ds(pl.multiple_of(sb * sup, sup), sup)
        fr = f_ref[0, rows, :].astype(F32)
        t = jnp.exp(-jnp.abs(fr))
        rcp = 1.0 / (1.0 + t)
        pos = fr >= 0
        sig = jnp.where(pos, rcp, t * rcp)
        nsig = jnp.where(pos, t * rcp, rcp)
        logf = jnp.log(lb + (1.0 - lb) * sig)
        key = (1.0 - lb) * nsig
        la, lbb, lc = _split3(logf)
        cum = lambda m_: (jnp.dot(m_, la, preferred_element_type=F32)
                          + jnp.dot(m_, lbb, preferred_element_type=F32)
                          + jnp.dot(m_, lc, preferred_element_type=F32))
        b = cum(tri)
        b_last = cum(ones_bd)
        qv = q_ref[0, rows, :].astype(F32)
        qs = qv * (1.0 / (1.0 + jnp.exp(-qv)))
        q_in = (qs * jnp.exp(b)).astype(BF16)
        k_in = (key * jnp.exp(-b)).astype(BF16)
        k_end = (key * jnp.exp(b_last - b)).astype(BF16)
        decay = jnp.exp(b_last)
        vv = i_ref[0, rows, :]
        gv = g_ref[0, rows, :].astype(F32)
        gate = gv * (1.0 / (1.0 + jnp.exp(-gv)))
        outs = []
        new_state = []
        for h in range(HG_HEADS_PER_STEP):
            sl = slice(h * HEAD_DIM, (h + 1) * HEAD_DIM)
            a = lax.dot_general(q_in[:, sl], k_in[:, sl], (((1,), (1,)), ((), ())),
                                preferred_element_type=F32)
            a = jnp.where(causal_bd, a, 0.0).astype(BF16)
            o_intra = jnp.dot(a, vv[:, sl], preferred_element_type=F32)
            s_h = state[h]
            inter = []
            for c in range(n_ch):
                cr = slice(c * ch, (c + 1) * ch)
                inter.append(lax.dot_general(q_in[cr, sl], s_h.astype(BF16), (((1,), (1,)), ((), ())),
                                             preferred_element_type=F32))
                ds = lax.dot_general(vv[cr, sl], k_end[cr, sl], (((0,), (0,)), ((), ())),
                                     preferred_element_type=F32)
                dec = decay[c * ch:c * ch + 1, sl]
                s_h = dec * s_h + ds
            new_state.append(s_h)
            o_h = o_intra + jnp.concatenate(inter, axis=0)
            ms = jnp.mean(o_h * o_h, axis=-1, keepdims=True)
            outs.append(o_h * lax.rsqrt(ms + EPS))
        o = jnp.concatenate(outs, axis=-1) * ng * gate
        y_ref[0, rows, :] = o.astype(y_ref.dtype)
        return tuple(new_state)

    s0 = tuple(jnp.zeros((HEAD_DIM, HEAD_DIM), F32) for _ in range(HG_HEADS_PER_STEP))
    lax.fori_loop(0, SEQ // sup, body, s0)


def hgrn(hg, gamma, norm_g, batch):
    hg3 = hg.reshape(batch, SEQ, HG_W)
    n_hg = WIDTH // HG_W_STEP
    r = jnp.arange(HGRN_SUPER)
    same = (r[:, None] // HGRN_CHUNK) == (r[None, :] // HGRN_CHUNK)
    tri = (same & (r[:, None] >= r[None, :])).astype(BF16)
    ones_bd = same.astype(BF16)
    col = lambda which: (lambda b, g: (b, 0, which * n_hg + g))
    const2 = lambda b, g: (0, 0)
    return pl.pallas_call(
        _hgrn_kernel,
        grid=(batch, n_hg),
        in_specs=[
            pl.BlockSpec((1, SEQ, HG_W_STEP), col(0)),
            pl.BlockSpec((1, SEQ, HG_W_STEP), col(1)),
            pl.BlockSpec((1, SEQ, HG_W_STEP), col(2)),
            pl.BlockSpec((1, SEQ, HG_W_STEP), col(3)),
            pl.BlockSpec((2, HG_W_STEP), lambda b, g: (0, g)),
            pl.BlockSpec((1, HG_W_STEP), lambda b, g: (0, g)),
            pl.BlockSpec((HGRN_SUPER, HGRN_SUPER), const2),
            pl.BlockSpec((HGRN_SUPER, HGRN_SUPER), const2),
        ],
        out_specs=pl.BlockSpec((1, SEQ, HG_W_STEP), lambda b, g: (b, 0, g)),
        out_shape=jax.ShapeDtypeStruct((batch, SEQ, WIDTH), BF16),
        compiler_params=_cparams("arbitrary", "arbitrary"),
        name="hgrn2",
    )(hg3, hg3, hg3, hg3, gamma.astype(F32), norm_g.reshape(1, WIDTH).astype(F32), tri, ones_bd)


def _post_mix_kernel(oa_ref, yh_ref, x_ref, ag_ref, wo_ref, g2_ref, wrh_ref, wrl_ref, br_ref,
                     x1_ref, h2_ref, lg_ref):
    oa = oa_ref[...].astype(F32)
    ms = jnp.mean(oa * oa, axis=-1, keepdims=True)
    ya = (oa * lax.rsqrt(ms + EPS) * ag_ref[...]).astype(BF16)
    mix = (jnp.dot(ya, wo_ref[:WIDTH, :], preferred_element_type=F32)
           + jnp.dot(yh_ref[...], wo_ref[WIDTH:, :], preferred_element_type=F32))
    x1 = x_ref[...] + mix
    x1_ref[...] = x1
    ms2 = jnp.mean(x1 * x1, axis=-1, keepdims=True)
    h2 = x1 * lax.rsqrt(ms2 + EPS) * g2_ref[...]
    h2_ref[...] = h2
    hi = h2.astype(BF16)
    lo = (h2 - hi.astype(F32)).astype(BF16)
    wrh = wrh_ref[...]
    lg_ref[...] = (jnp.dot(hi, wrh, preferred_element_type=F32)
                   + jnp.dot(lo, wrh, preferred_element_type=F32)
                   + jnp.dot(hi, wrl_ref[...], preferred_element_type=F32)
                   + br_ref[...])


def post_mix(oa, yh, x2, attn_g, w_out_bf16, g2, wr_hi, wr_lo, br, tm=512):
    n = x2.shape[0]
    row = lambda w: pl.BlockSpec((tm, w), lambda i: (i, 0))
    const = lambda r, c: pl.BlockSpec((r, c), lambda i: (0, 0))
    return pl.pallas_call(
        _post_mix_kernel,
        grid=(n // tm,),
        in_specs=[row(WIDTH), row(WIDTH), row(D_MODEL), const(1, WIDTH), const(2 * WIDTH, D_MODEL),
                  const(1, D_MODEL), const(D_MODEL, LANES), const(D_MODEL, LANES), const(1, LANES)],
        out_specs=[row(D_MODEL), row(D_MODEL), row(LANES)],
        out_shape=[jax.ShapeDtypeStruct((n, D_MODEL), F32),
                   jax.ShapeDtypeStruct((n, D_MODEL), F32),
                   jax.ShapeDtypeStruct((n, LANES), F32)],
        compiler_params=_cparams("arbitrary"),
        name="post_mix",
    )(oa, yh, x2, attn_g.reshape(1, WIDTH), w_out_bf16, g2.reshape(1, D_MODEL), wr_hi, wr_lo, br)


ROUTE_TM = 512
R_E1, R_E2, R_RANK1, R_RANK2, R_G1, R_G2 = 0, 1, 2, 3, 4, 5


def _route_kernel(lg_ref, tri_ref, rec_ref, cnt_ref, carry_ref):
    i = pl.program_id(0)

    @pl.when(i == 0)
    def _():
        carry_ref[...] = jnp.zeros_like(carry_ref)

    lg = lg_ref[...]
    lane = lax.broadcasted_iota(I32, lg.shape, 1)
    lanef = lane.astype(F32)
    big = float(LANES)
    gmask = lane < N_GROUPS
    gl = jnp.where(gmask, lg, NEG)
    gmax = jnp.max(gl, axis=-1, keepdims=True)
    gsel = jnp.min(jnp.where(gmask & (gl == gmax), lanef, big), axis=-1, keepdims=True)
    gsum = jnp.sum(jnp.where(gmask, jnp.exp(gl - gmax), 0.0), axis=-1, keepdims=True)
    w_g = 1.0 / gsum
    lo = EXPERT_LANE0 + EXPERTS_PER_GROUP * gsel
    emask = (lanef >= lo) & (lanef < lo + EXPERTS_PER_GROUP)
    el = jnp.where(emask, lg, NEG)
    v1 = jnp.max(el, axis=-1, keepdims=True)
    i1 = jnp.min(jnp.where(emask & (el == v1), lanef, big), axis=-1, keepdims=True)
    emask2 = emask & (lanef != i1)
    el2 = jnp.where(emask2, lg, NEG)
    v2 = jnp.max(el2, axis=-1, keepdims=True)
    i2 = jnp.min(jnp.where(emask2 & (el2 == v2), lanef, big), axis=-1, keepdims=True)
    e2 = jnp.exp(v2 - v1)
    den = 1.0 / (1.0 + e2)
    g1 = w_g * den
    g2 = w_g * e2 * den
    oh1 = lanef == i1
    oh2 = lanef == i2
    onehot = (oh1 | oh2).astype(BF16)
    before = jnp.dot(tri_ref[...], onehot, preferred_element_type=F32) + carry_ref[...]
    rank1 = jnp.sum(jnp.where(oh1, before, 0.0), axis=-1, keepdims=True)
    rank2 = jnp.sum(jnp.where(oh2, before, 0.0), axis=-1, keepdims=True)
    carry_ref[...] += jnp.sum(onehot.astype(F32), axis=0, keepdims=True)
    cnt_ref[...] = carry_ref[...]
    rec = jnp.zeros_like(lg)
    for slot, val in ((R_E1, i1 - EXPERT_LANE0), (R_E2, i2 - EXPERT_LANE0), (R_RANK1, rank1),
                      (R_RANK2, rank2), (R_G1, g1), (R_G2, g2)):
        rec = jnp.where(lane == slot, val, rec)
    rec_ref[...] = rec


def route(logits):
    n = logits.shape[0]
    tm = ROUTE_TM
    r = jnp.arange(tm)
    tri = (r[:, None] > r[None, :]).astype(BF16)
    return pl.pallas_call(
        _route_kernel,
        grid=(n // tm,),
        in_specs=[pl.BlockSpec((tm, LANES), lambda i: (i, 0)),
                  pl.BlockSpec((tm, tm), lambda i: (0, 0))],
        out_specs=[pl.BlockSpec((tm, LANES), lambda i: (i, 0)),
                   pl.BlockSpec((1, LANES), lambda i: (0, 0))],
        out_shape=[jax.ShapeDtypeStruct((n, LANES), F32),
                   jax.ShapeDtypeStruct((1, LANES), F32)],
        scratch_shapes=[pltpu.VMEM((1, LANES), F32)],
        compiler_params=_cparams("arbitrary"),
        name="route",
    )(logits, tri)


DISPATCH_TM = 256


def _dispatch_kernel(d1_ref, d2_ref, h2_ref, xs_in_ref, xs_ref, sem):
    del xs_in_ref
    i = pl.program_id(0)
    base = i * DISPATCH_TM

    def row_copy(t, dest):
        return pltpu.make_async_copy(h2_ref.at[pl.ds(t, 1)], xs_ref.at[pl.ds(dest, 1)], sem)

    def issue(t, c):
        row_copy(t, d1_ref[base + t]).start()
        row_copy(t, d2_ref[base + t]).start()
        return c
    lax.fori_loop(0, DISPATCH_TM, issue, 0)

    def drain(t, c):
        row_copy(0, 0).wait()
        row_copy(0, 0).wait()
        return c
    lax.fori_loop(0, DISPATCH_TM, drain, 0)


def dispatch(h2, dest1, dest2, n_rows):
    n = h2.shape[0]
    xs0 = jnp.zeros((n_rows, D_MODEL), F32)
    return pl.pallas_call(
        _dispatch_kernel,
        grid_spec=pltpu.PrefetchScalarGridSpec(
            num_scalar_prefetch=2,
            grid=(n // DISPATCH_TM,),
            in_specs=[pl.BlockSpec((DISPATCH_TM, D_MODEL), lambda i, d1, d2: (i, 0)),
                      pl.BlockSpec(memory_space=pl.ANY)],
            out_specs=pl.BlockSpec(memory_space=pl.ANY),
            scratch_shapes=[pltpu.SemaphoreType.DMA(())],
        ),
        out_shape=jax.ShapeDtypeStruct((n_rows, D_MODEL), F32),
        input_output_aliases={3: 0},
        compiler_params=_cparams("arbitrary"),
        name="dispatch",
    )(dest1, dest2, h2, xs0)


def _ffn_kernel(be_ref, nv_ref, xs_ref, wg_ref, wu_ref, wd_ref, ys_ref, wg_sc, wu_sc, wd_sc):
    i = pl.program_id(0)
    prev = be_ref[jnp.maximum(i - 1, 0)]
    fresh = (i == 0) | (be_ref[i] != prev)

    @pl.when(fresh)
    def _():
        wg_sc[...] = wg_ref[0].astype(BF16)
        wu_sc[...] = wu_ref[0].astype(BF16)
        wd_sc[...] = wd_ref[0].astype(BF16)

    @pl.when(i < nv_ref[0])
    def _():
        xb = xs_ref[...].astype(BF16)
        gate = jnp.dot(xb, wg_sc[...], preferred_element_type=F32)
        up = jnp.dot(xb, wu_sc[...], preferred_element_type=F32)
        hid = (gate * (1.0 / (1.0 + jnp.exp(-gate))) * up).astype(BF16)
        ys_ref[...] = jnp.dot(hid, wd_sc[...], preferred_element_type=F32)

    @pl.when(i >= nv_ref[0])
    def _():
        ys_ref[...] = jnp.zeros_like(ys_ref)


def expert_ffn(xs, block_expert, n_valid, w_gate, w_up, w_down):
    n_rows = xs.shape[0]
    n_blocks = n_rows // FFN_BLOCK
    wmap = lambda i, be, nv: (be[i], 0, 0)
    return pl.pallas_call(
        _ffn_kernel,
        grid_spec=pltpu.PrefetchScalarGridSpec(
            num_scalar_prefetch=2,
            grid=(n_blocks,),
            in_specs=[pl.BlockSpec((FFN_BLOCK, D_MODEL), lambda i, be, nv: (i, 0)),
                      pl.BlockSpec((1, D_MODEL, D_EXPERT), wmap),
                      pl.BlockSpec((1, D_MODEL, D_EXPERT), wmap),
                      pl.BlockSpec((1, D_EXPERT, D_MODEL), wmap)],
            out_specs=pl.BlockSpec((FFN_BLOCK, D_MODEL), lambda i, be, nv: (i, 0)),
            scratch_shapes=[pltpu.VMEM((D_MODEL, D_EXPERT), BF16),
                            pltpu.VMEM((D_MODEL, D_EXPERT), BF16),
                            pltpu.VMEM((D_EXPERT, D_MODEL), BF16)],
        ),
        out_shape=jax.ShapeDtypeStruct((n_rows, D_MODEL), F32),
        compiler_params=_cparams("arbitrary"),
        name="expert_ffn",
    )(block_expert, n_valid, xs, w_gate, w_up, w_down)


COMBINE_TM = 256


def _combine_kernel(d1_ref, d2_ref, x1_ref, rec_ref, gf_ref, ys_ref, out_ref, buf1, buf2, sem):
    i = pl.program_id(0)
    base = i * COMBINE_TM

    def row_copy(src_row, buf, t):
        return pltpu.make_async_copy(ys_ref.at[pl.ds(src_row, 1)], buf.at[pl.ds(t, 1)], sem)

    def issue(t, c):
        row_copy(d1_ref[base + t], buf1, t).start()
        row_copy(d2_ref[base + t], buf2, t).start()
        return c
    lax.fori_loop(0, COMBINE_TM, issue, 0)

    def drain(t, c):
        row_copy(0, buf1, 0).wait()
        row_copy(0, buf2, 0).wait()
        return c
    lax.fori_loop(0, COMBINE_TM, drain, 0)

    rec = rec_ref[...]
    g1 = rec[:, R_G1:R_G1 + 1]
    g2 = rec[:, R_G2:R_G2 + 1]
    x = x1_ref[...] + g1 * buf1[...] + g2 * buf2[...]
    ms = jnp.mean(x * x, axis=-1, keepdims=True)
    out_ref[...] = x * lax.rsqrt(ms + EPS) * gf_ref[...]


def combine(x1, rec, norm_f_g, ys, dest1, dest2):
    n = x1.shape[0]
    tm = COMBINE_TM
    return pl.pallas_call(
        _combine_kernel,
        grid_spec=pltpu.PrefetchScalarGridSpec(
            num_scalar_prefetch=2,
            grid=(n // tm,),
            in_specs=[pl.BlockSpec((tm, D_MODEL), lambda i, d1, d2: (i, 0)),
                      pl.BlockSpec((tm, LANES), lambda i, d1, d2: (i, 0)),
                      pl.BlockSpec((1, D_MODEL), lambda i, d1, d2: (0, 0)),
                      pl.BlockSpec(memory_space=pl.ANY)],
            out_specs=pl.BlockSpec((tm, D_MODEL), lambda i, d1, d2: (i, 0)),
            scratch_shapes=[pltpu.VMEM((tm, D_MODEL), F32),
                            pltpu.VMEM((tm, D_MODEL), F32),
                            pltpu.SemaphoreType.DMA(())],
        ),
        out_shape=jax.ShapeDtypeStruct((n, D_MODEL), F32),
        compiler_params=_cparams("arbitrary"),
        name="combine",
    )(dest1, dest2, x1, rec, norm_f_g.reshape(1, D_MODEL), ys)


def _router_weights(w_group, b_group, w_router, b_router):
    w = jnp.zeros((D_MODEL, LANES), F32)
    w = w.at[:, :N_GROUPS].set(w_group).at[:, EXPERT_LANE0:EXPERT_LANE0 + N_EXPERTS].set(w_router)
    b = jnp.zeros((1, LANES), F32)
    b = b.at[0, :N_GROUPS].set(b_group).at[0, EXPERT_LANE0:EXPERT_LANE0 + N_EXPERTS].set(b_router)
    hi = w.astype(BF16)
    lo = (w - hi.astype(F32)).astype(BF16)
    return hi, lo, b


def _dispatch_plan(rec, counts, n_rows):
    cnt = counts[0, EXPERT_LANE0:EXPERT_LANE0 + N_EXPERTS].astype(I32)
    padded = ((cnt + FFN_BLOCK - 1) // FFN_BLOCK) * FFN_BLOCK
    pad_ends = jnp.cumsum(padded)
    pad_starts = pad_ends - padded
    e1 = rec[:, R_E1].astype(I32)
    e2 = rec[:, R_E2].astype(I32)
    dest1 = pad_starts[e1] + rec[:, R_RANK1].astype(I32)
    dest2 = pad_starts[e2] + rec[:, R_RANK2].astype(I32)
    n_blocks = n_rows // FFN_BLOCK
    blk_start = jnp.arange(n_blocks, dtype=I32) * FFN_BLOCK
    block_expert = jnp.minimum(jnp.searchsorted(pad_ends, blk_start, side="right"),
                               N_EXPERTS - 1).astype(I32)
    n_valid = (pad_ends[-1:] // FFN_BLOCK).astype(I32)
    return dest1, dest2, block_expert, n_valid


def kernel(x, norm1_g, w_in, attn_norm_g, hgrn_gamma, hgrn_norm_g, w_out, norm2_g, w_group, b_group,
           w_router, b_router, w_gate, w_up, w_down, norm_f_g):
    batch, seq, d = x.shape
    assert seq == SEQ and d == D_MODEL and norm1_g.shape[0] == 1
    n = batch * seq
    x2 = x.reshape(n, d)
    qkv, hg = in_proj(x2, norm1_g[0], w_in[0].astype(BF16))
    oa = attention(qkv, batch).reshape(n, WIDTH)
    yh = hgrn(hg, hgrn_gamma, hgrn_norm_g[0], batch).reshape(n, WIDTH)
    wr_hi, wr_lo, br = _router_weights(w_group[0], b_group[0], w_router[0], b_router[0])
    x1, h2, logits = post_mix(oa, yh, x2, attn_norm_g[0], w_out[0].astype(BF16), norm2_g[0],
                              wr_hi, wr_lo, br)
    rec, counts = route(logits)
    n_rows = 2 * n + N_EXPERTS * FFN_BLOCK
    dest1, dest2, block_expert, n_valid = _dispatch_plan(rec, counts, n_rows)
    xs = dispatch(h2, dest1, dest2, n_rows)
    ys = expert_ffn(xs, block_expert, n_valid, w_gate[0], w_up[0], w_down[0])
    out = combine(x1, rec, norm_f_g, ys, dest1, dest2)
    return out.reshape(batch, seq, d)
```

```python
import jax
import jax.numpy as jnp
from jax import lax
from jax.experimental import pallas as pl
from jax.experimental.pallas import tpu as pltpu

F32 = jnp.float32
BF16 = jnp.bfloat16
I32 = jnp.int32

D_MODEL = 1024
HEAD_DIM = 64
N_HEADS = 8
WIDTH = N_HEADS * HEAD_DIM
QKV_W = 3 * WIDTH
HG_W = 4 * WIDTH
SEQ = 2048
ATTN_BLOCK = 128
DILATIONS = (1, 4, 16)
N_RES = 16
RES_LEN = SEQ // N_RES
RES_PER_TILE = 4
TILE_TOKENS = RES_PER_TILE * RES_LEN
HGRN_CHUNK = 32
HGRN_SUPER = 256
N_GROUPS = 4
EXPERTS_PER_GROUP = 8
N_EXPERTS = 32
D_EXPERT = 512
EXPERT_LANE0 = 32
FFN_BLOCK = 256
EPS = 1e-6
NEG = -1e30
LOG2E = 1.4426950408889634
LANES = 128
VMEM_LIMIT = 56 * 1024 * 1024


def _cparams(*sem):
    return pltpu.CompilerParams(dimension_semantics=sem, vmem_limit_bytes=VMEM_LIMIT)


def _rows_from_lanes(ref, width):
    return jnp.concatenate([ref[:, rl * width:(rl + 1) * width] for rl in range(RES_PER_TILE)], axis=0)


N_LANE_TILES = WIDTH // LANES


def _in_proj_kernel(x_ref, g_ref, w_ref, qkv_ref, hg_ref):
    x = _rows_from_lanes(x_ref, D_MODEL)
    ms = jnp.mean(x * x, axis=-1, keepdims=True)
    h = (x * lax.rsqrt(ms + EPS) * g_ref[...]).astype(BF16)
    for c in range(QKV_W // WIDTH):
        res = jnp.dot(h, w_ref[:, c * WIDTH:(c + 1) * WIDTH], preferred_element_type=F32)
        for rl in range(RES_PER_TILE):
            for g in range(N_LANE_TILES):
                qkv_ref[rl, g, c] = res[rl * RES_LEN:(rl + 1) * RES_LEN, g * LANES:(g + 1) * LANES]
    for j in range(HG_W // WIDTH):
        wsl = slice(QKV_W + j * WIDTH, QKV_W + (j + 1) * WIDTH)
        res = jnp.dot(h, w_ref[:, wsl], preferred_element_type=F32).astype(BF16)
        for rl in range(RES_PER_TILE):
            hg_ref[:, rl * HG_W + j * WIDTH:rl * HG_W + (j + 1) * WIDTH] = res[rl * RES_LEN:(rl + 1) * RES_LEN]


def in_proj(x, g, w_bf16):
    batch = x.shape[0]
    xv = x.reshape(batch, RES_LEN, N_RES * D_MODEL)
    qkv, hg = pl.pallas_call(
        _in_proj_kernel,
        grid=(batch, N_RES // RES_PER_TILE),
        in_specs=[
            pl.BlockSpec((None, RES_LEN, RES_PER_TILE * D_MODEL), lambda b, q: (b, 0, q)),
            pl.BlockSpec((1, D_MODEL), lambda b, q: (0, 0)),
            pl.BlockSpec((D_MODEL, QKV_W + HG_W), lambda b, q: (0, 0)),
        ],
        out_specs=[
            pl.BlockSpec((None, RES_PER_TILE, N_LANE_TILES, 3, RES_LEN, LANES),
                         lambda b, q: (b, q, 0, 0, 0, 0)),
            pl.BlockSpec((None, RES_LEN, RES_PER_TILE * HG_W), lambda b, q: (b, 0, q)),
        ],
        out_shape=[
            jax.ShapeDtypeStruct((batch, N_RES, N_LANE_TILES, 3, RES_LEN, LANES), F32),
            jax.ShapeDtypeStruct((batch, RES_LEN, N_RES * HG_W), BF16),
        ],
        compiler_params=_cparams("arbitrary", "arbitrary"),
        name="in_proj",
    )(xv, g.reshape(1, D_MODEL), w_bf16)
    return qkv, hg.reshape(batch, SEQ, HG_W)


ATTN_GROUP = 4
ATTN_GROUP1 = 3
HEADS_PER_TILE = LANES // HEAD_DIM


def _attn_bias_tables():
    slopes = 2.0 ** (-8.0 * jnp.arange(1, N_HEADS + 1, dtype=F32) / N_HEADS)

    def table(qpos, kpos, dil):
        sd = qpos[:, None] - kpos[None, :]
        b = -slopes[:, None, None] * (sd * dil).astype(F32)[None] * LOG2E
        return jnp.where(((sd >= 0) & (sd <= ATTN_BLOCK))[None], b, NEG)

    q = jnp.arange(ATTN_BLOCK)
    k = jnp.arange(2 * ATTN_BLOCK)
    q1 = 16 * (q % 8) + q // 8 + ATTN_BLOCK
    k1 = 16 * (k % 16) + k // 16
    q4 = 4 * (q % 32) + q // 32 + ATTN_BLOCK
    k4 = 4 * (k % 64) + k // 64
    q16 = q + ATTN_BLOCK
    none = jnp.full((N_HEADS, ATTN_BLOCK, 2 * ATTN_BLOCK), NEG, F32)
    tabs = [jnp.concatenate([table(q1, k1, 1), table(q1, q1, 1)], axis=-1),
            jnp.concatenate([table(q4, k4, 4), table(q4, q4, 4)], axis=-1),
            jnp.concatenate([none, table(q16, q16, 16)], axis=-1)]
    return jnp.stack(tabs, axis=0)


def _attn_blocks(blocks, bias_ref, d_idx, with_prev):
    lane = lax.broadcasted_iota(I32, (ATTN_BLOCK, LANES), 1)
    head0 = lane < HEAD_DIM
    heads = [(b, h) for b in range(len(blocks)) for h in range(HEADS_PER_TILE)]
    if with_prev:
        bias = [bias_ref[d_idx, h, :, :2 * ATTN_BLOCK] for h in range(HEADS_PER_TILE)]
    else:
        bias = [bias_ref[d_idx, h, :, 2 * ATTN_BLOCK:] for h in range(HEADS_PER_TILE)]
    qs = [q * (HEAD_DIM ** -0.5 * LOG2E) for q, _, _ in blocks]
    ks = [k.astype(BF16) for _, k, _ in blocks]
    vs = [v.astype(BF16) for _, _, v in blocks]
    v_ones = [jnp.concatenate([v, jnp.ones_like(v)], axis=-1) for v in vs]
    qh = {(b, h): jnp.where(head0 if h == 0 else jnp.logical_not(head0), qs[b], 0.0).astype(BF16)
          for b, h in heads}
    s = {(b, h): lax.dot_general(qh[b, h], ks[b], (((1,), (1,)), ((), ())),
                                 preferred_element_type=F32) + bias[h] for b, h in heads}
    m = {bh: jnp.max(s[bh], axis=-1, keepdims=True) for bh in heads}
    p = {bh: jnp.exp2(s[bh] - m[bh]).astype(BF16) for bh in heads}
    acc = {(b, h): jnp.dot(p[b, h], v_ones[b], preferred_element_type=F32) for b, h in heads}
    head0_wide = jnp.concatenate([head0, head0], axis=-1)
    outs = []
    for b in range(len(blocks)):
        ol = jnp.where(head0_wide, acc[b, 0], acc[b, 1])
        l = ol[:, LANES:]
        outs.append((ol[:, :LANES] * (1.0 / l), jnp.where(head0, m[b, 0], m[b, 1]) + jnp.log2(l)))
    return outs


def _attn_kernel(qkv_ref, bias_ref, o_ref, o4_ref, l4_ref, o16_ref, l16_ref, fin_ref):
    grp = ATTN_GROUP

    def body16(j, c):
        rs = [j * grp + a for a in range(grp)]
        res = _attn_blocks([(qkv_ref[r, 0], qkv_ref[r, 1], qkv_ref[r, 2]) for r in rs], bias_ref, 2, False)
        for r, (o, l) in zip(rs, res):
            o16_ref[r] = o
            l16_ref[r] = l
        return c
    lax.fori_loop(0, N_RES // grp, body16, 0)

    def gather4(c, rho, u0, nu):
        return jnp.concatenate([qkv_ref[rho + 4 * a, c, pl.ds(u0, nu), :] for a in range(4)], axis=0)

    def store4(rho, u0, o, l):
        for a in range(4):
            o4_ref[rho + 4 * a, pl.ds(u0, 32), :] = o[a * 32:(a + 1) * 32]
            l4_ref[rho + 4 * a, pl.ds(u0, 32), :] = l[a * 32:(a + 1) * 32]

    res = _attn_blocks([tuple(gather4(c, rho, 0, 32) for c in range(3)) for rho in range(4)],
                       bias_ref, 1, False)
    for rho, (o, l) in enumerate(res):
        store4(rho, 0, o, l)

    def body4(j, c):
        u0 = pl.multiple_of((j + 1) * 32, 32)
        res = _attn_blocks([(gather4(0, rho, u0, 32), gather4(1, rho, u0 - 32, 64),
                             gather4(2, rho, u0 - 32, 64)) for rho in range(4)], bias_ref, 1, True)
        for rho, (o, l) in enumerate(res):
            store4(rho, u0, o, l)
        return c
    lax.fori_loop(0, 3, body4, 0)

    def gather1(ref, u0, nu, *lead):
        return jnp.concatenate([ref[(r,) + lead + (pl.ds(u0, nu), slice(None))] for r in range(N_RES)],
                               axis=0)

    def merge_and_store(u0, o1, l1):
        l4 = gather1(l4_ref, u0, 8)
        l16 = gather1(l16_ref, u0, 8)
        mx = jnp.maximum(jnp.maximum(l1, l4), l16)
        e1 = jnp.exp2(l1 - mx)
        e4 = jnp.exp2(l4 - mx)
        e16 = jnp.exp2(l16 - mx)
        inv = 1.0 / (e1 + e4 + e16)
        o = (e1 * inv) * o1 + (e4 * inv) * gather1(o4_ref, u0, 8) + (e16 * inv) * gather1(o16_ref, u0, 8)
        for r in range(N_RES):
            fin_ref[r, pl.ds(u0, 8), :] = o[r * 8:(r + 1) * 8]

    (o, l), = _attn_blocks([tuple(gather1(qkv_ref, 0, 8, c) for c in range(3))], bias_ref, 0, False)
    merge_and_store(0, o, l)

    n_blocks1 = RES_LEN // 8 - 1

    def body1(j, c):
        u0s = [pl.multiple_of((1 + j * ATTN_GROUP1 + a) * 8, 8) for a in range(ATTN_GROUP1)]
        res = _attn_blocks([(gather1(qkv_ref, u0, 8, 0), gather1(qkv_ref, u0 - 8, 16, 1),
                             gather1(qkv_ref, u0 - 8, 16, 2)) for u0 in u0s], bias_ref, 0, True)
        for u0, (o, l) in zip(u0s, res):
            merge_and_store(u0, o, l)
        return c
    lax.fori_loop(0, n_blocks1 // ATTN_GROUP1, body1, 0)

    def emit(r, c):
        o_ref[r] = fin_ref[r].astype(o_ref.dtype)
        return c
    lax.fori_loop(0, N_RES, emit, 0)


def attention(qkv):
    batch = qkv.shape[0]
    scratch = pltpu.VMEM((N_RES, RES_LEN, LANES), F32)
    return pl.pallas_call(
        _attn_kernel,
        grid=(batch, N_LANE_TILES),
        in_specs=[
            pl.BlockSpec((None, N_RES, None, 3, RES_LEN, LANES), lambda b, g: (b, 0, g, 0, 0, 0)),
            pl.BlockSpec((len(DILATIONS), HEADS_PER_TILE, ATTN_BLOCK, 3 * ATTN_BLOCK),
                         lambda b, g: (0, g, 0, 0)),
        ],
        out_specs=pl.BlockSpec((None, N_RES, None, RES_LEN, LANES), lambda b, g: (b, 0, g, 0, 0)),
        out_shape=jax.ShapeDtypeStruct((batch, N_RES, N_LANE_TILES, RES_LEN, LANES), BF16),
        scratch_shapes=[scratch] * 5,
        compiler_params=_cparams("arbitrary", "arbitrary"),
        name="dilated_attention",
    )(qkv, _attn_bias_tables())


HG_HEADS_PER_STEP = 4
HG_W_STEP = HG_HEADS_PER_STEP * HEAD_DIM


def _split3(x):
    a = x.astype(BF16)
    r = x - a.astype(F32)
    b = r.astype(BF16)
    c = (r - b.astype(F32)).astype(BF16)
    return a, b, c


def _hgrn_kernel(q_ref, f_ref, i_ref, g_ref, gamma_ref, ng_ref, tri_ref, ones_ref, y_ref):
    sup, ch = HGRN_SUPER, HGRN_CHUNK
    n_ch = sup // ch
    gam = gamma_ref[...]
    gmx = jnp.max(gam, axis=0, keepdims=True)
    ge = jnp.exp(gam - gmx)
    lb = ge[0:1] / jnp.sum(ge, axis=0, keepdims=True)
    ng = ng_ref[...]
    tri = tri_ref[...]
    ones_bd = ones_ref[...]
    ri = lax.broadcasted_iota(I32, (sup, sup), 0)
    ci = lax.broadcasted_iota(I32, (sup, sup), 1)
    causal_bd = (ri // ch == ci // ch) & (ri >= ci)

    def body(sb, state):
        rows = pl.ds(pl.multiple_of(sb * sup, sup), sup)
        fr = f_ref[0, rows, :].astype(F32)
        t = jnp.exp(-jnp.abs(fr))
        rcp = 1.0 / (1.0 + t)
        pos = fr >= 0
        sig = jnp.where(pos, rcp, t * rcp)
        nsig = jnp.where(pos, t * rcp, rcp)
        logf = jnp.log(lb + (1.0 - lb) * sig)
        key = (1.0 - lb) * nsig
        la, lbb, lc = _split3(logf)
        cum = lambda m_: (jnp.dot(m_, la, preferred_element_type=F32)
                          + jnp.dot(m_, lbb, preferred_element_type=F32)
                          + jnp.dot(m_, lc, preferred_element_type=F32))
        b = cum(tri)
        b_last = cum(ones_bd)
        qv = q_ref[0, rows, :].astype(F32)
        qs = qv * (1.0 / (1.0 + jnp.exp(-qv)))
        q_in = (qs * jnp.exp(b)).astype(BF16)
        k_in = (key * jnp.exp(-b)).astype(BF16)
        k_end = (key * jnp.exp(b_last - b)).astype(BF16)
        decay = jnp.exp(b_last)
        vv = i_ref[0, rows, :]
        gv = g_ref[0, rows, :].astype(F32)
        gate = gv * (1.0 / (1.0 + jnp.exp(-gv)))
        outs = []
        new_state = []
        for h in range(HG_HEADS_PER_STEP):
            sl = slice(h * HEAD_DIM, (h + 1) * HEAD_DIM)
            a = lax.dot_general(q_in[:, sl], k_in[:, sl], (((1,), (1,)), ((), ())),
                                preferred_element_type=F32)
            a = jnp.where(causal_bd, a, 0.0).astype(BF16)
            o_intra = jnp.dot(a, vv[:, sl], preferred_element_type=F32)
            s_h = state[h]
            inter = []
            for c in range(n_ch):
                cr = slice(c * ch, (c + 1) * ch)
                inter.append(lax.dot_general(q_in[cr, sl], s_h.astype(BF16), (((1,), (1,)), ((), ())),
                                             preferred_element_type=F32))
                ds = lax.dot_general(vv[cr, sl], k_end[cr, sl], (((0,), (0,)), ((), ())),
                                     preferred_element_type=F32)
                dec = decay[c * ch:c * ch + 1, sl]
                s_h = dec * s_h + ds
            new_state.append(s_h)
            o_h = o_intra + jnp.concatenate(inter, axis=0)
            ms = jnp.mean(o_h * o_h, axis=-1, keepdims=True)
            outs.append(o_h * lax.rsqrt(ms + EPS))
        o = jnp.concatenate(outs, axis=-1) * ng * gate
        y_ref[0, rows, :] = o.astype(y_ref.dtype)
        return tuple(new_state)

    s0 = tuple(jnp.zeros((HEAD_DIM, HEAD_DIM), F32) for _ in range(HG_HEADS_PER_STEP))
    lax.fori_loop(0, SEQ // sup, body, s0)


def hgrn(hg3, gamma, norm_g):
    batch = hg3.shape[0]
    n_hg = WIDTH // HG_W_STEP
    r = jnp.arange(HGRN_SUPER)
    same = (r[:, None] // HGRN_CHUNK) == (r[None, :] // HGRN_CHUNK)
    tri = (same & (r[:, None] >= r[None, :])).astype(BF16)
    ones_bd = same.astype(BF16)
    col = lambda which: (lambda b, g: (b, 0, which * n_hg + g))
    const2 = lambda b, g: (0, 0)
    return pl.pallas_call(
        _hgrn_kernel,
        grid=(batch, n_hg),
        in_specs=[
            pl.BlockSpec((1, SEQ, HG_W_STEP), col(0)),
            pl.BlockSpec((1, SEQ, HG_W_STEP), col(1)),
            pl.BlockSpec((1, SEQ, HG_W_STEP), col(2)),
            pl.BlockSpec((1, SEQ, HG_W_STEP), col(3)),
            pl.BlockSpec((2, HG_W_STEP), lambda b, g: (0, g)),
            pl.BlockSpec((1, HG_W_STEP), lambda b, g: (0, g)),
            pl.BlockSpec((HGRN_SUPER, HGRN_SUPER), const2),
            pl.BlockSpec((HGRN_SUPER, HGRN_SUPER), const2),
        ],
        out_specs=pl.BlockSpec((1, SEQ, HG_W_STEP), lambda b, g: (b, 0, g)),
        out_shape=jax.ShapeDtypeStruct((batch, SEQ, WIDTH), BF16),
        compiler_params=_cparams("arbitrary", "arbitrary"),
        name="hgrn2",
    )(hg3, hg3, hg3, hg3, gamma.astype(F32), norm_g.reshape(1, WIDTH).astype(F32), tri, ones_bd)


def _post_mix_kernel(oa_ref, yh_ref, x_ref, ag_ref, wo_ref, g2_ref, wrh_ref, wrl_ref, br_ref,
                     x1_ref, h2_ref, lg_ref):
    oa = jnp.concatenate(
        [jnp.concatenate([oa_ref[rl, g] for g in range(N_LANE_TILES)], axis=1)
         for rl in range(RES_PER_TILE)], axis=0).astype(F32)
    yh = _rows_from_lanes(yh_ref, WIDTH)
    x = _rows_from_lanes(x_ref, D_MODEL)
    ms = jnp.mean(oa * oa, axis=-1, keepdims=True)
    ya = (oa * lax.rsqrt(ms + EPS) * ag_ref[...]).astype(BF16)
    mix = (jnp.dot(ya, wo_ref[:WIDTH, :], preferred_element_type=F32)
           + jnp.dot(yh, wo_ref[WIDTH:, :], preferred_element_type=F32))
    x1 = x + mix
    x1_ref[...] = x1
    ms2 = jnp.mean(x1 * x1, axis=-1, keepdims=True)
    h2 = x1 * lax.rsqrt(ms2 + EPS) * g2_ref[...]
    h2_ref[...] = h2
    hi = h2.astype(BF16)
    lo = (h2 - hi.astype(F32)).astype(BF16)
    wrh = wrh_ref[...]
    lg_ref[...] = (jnp.dot(hi, wrh, preferred_element_type=F32)
                   + jnp.dot(lo, wrh, preferred_element_type=F32)
                   + jnp.dot(hi, wrl_ref[...], preferred_element_type=F32)
                   + br_ref[...])


def post_mix(oa, yh, x, attn_g, w_out_bf16, g2, wr_hi, wr_lo, br):
    batch = x.shape[0]
    n = batch * SEQ
    n_q = N_RES // RES_PER_TILE
    yv = yh.reshape(batch, RES_LEN, N_RES * WIDTH)
    xv = x.reshape(batch, RES_LEN, N_RES * D_MODEL)
    row = lambda w: pl.BlockSpec((TILE_TOKENS, w), lambda b, q: (b * n_q + q, 0))
    const = lambda r, c: pl.BlockSpec((r, c), lambda b, q: (0, 0))
    return pl.pallas_call(
        _post_mix_kernel,
        grid=(batch, n_q),
        in_specs=[pl.BlockSpec((None, RES_PER_TILE, N_LANE_TILES, RES_LEN, LANES),
                               lambda b, q: (b, q, 0, 0, 0)),
                  pl.BlockSpec((None, RES_LEN, RES_PER_TILE * WIDTH), lambda b, q: (b, 0, q)),
                  pl.BlockSpec((None, RES_LEN, RES_PER_TILE * D_MODEL), lambda b, q: (b, 0, q)),
                  const(1, WIDTH), const(2 * WIDTH, D_MODEL), const(1, D_MODEL),
                  const(D_MODEL, LANES), const(D_MODEL, LANES), const(1, LANES)],
        out_specs=[row(D_MODEL), row(D_MODEL), row(LANES)],
        out_shape=[jax.ShapeDtypeStruct((n, D_MODEL), F32),
                   jax.ShapeDtypeStruct((n, D_MODEL), F32),
                   jax.ShapeDtypeStruct((n, LANES), F32)],
        compiler_params=_cparams("arbitrary", "arbitrary"),
        name="post_mix",
    )(oa, yv, xv, attn_g.reshape(1, WIDTH), w_out_bf16, g2.reshape(1, D_MODEL), wr_hi, wr_lo, br)


ROUTE_TM = 512
R_E1, R_E2, R_RANK1, R_RANK2, R_G1, R_G2 = 0, 1, 2, 3, 4, 5


def _route_kernel(lg_ref, tri_ref, rec_ref, cnt_ref, carry_ref):
    i = pl.program_id(0)

    @pl.when(i == 0)
    def _():
        carry_ref[...] = jnp.zeros_like(carry_ref)
        cnt_ref[...] = jnp.zeros_like(cnt_ref)

    lg = lg_ref[...]
    lane = lax.broadcasted_iota(I32, lg.shape, 1)
    lanef = lane.astype(F32)
    big = float(LANES)
    gmask = lane < N_GROUPS
    gl = jnp.where(gmask, lg, NEG)
    gmax = jnp.max(gl, axis=-1, keepdims=True)
    gsel = jnp.min(jnp.where(gmask & (gl == gmax), lanef, big), axis=-1, keepdims=True)
    gsum = jnp.sum(jnp.where(gmask, jnp.exp(gl - gmax), 0.0), axis=-1, keepdims=True)
    w_g = 1.0 / gsum
    lo = EXPERT_LANE0 + EXPERTS_PER_GROUP * gsel
    emask = (lanef >= lo) & (lanef < lo + EXPERTS_PER_GROUP)
    el = jnp.where(emask, lg, NEG)
    v1 = jnp.max(el, axis=-1, keepdims=True)
    i1 = jnp.min(jnp.where(emask & (el == v1), lanef, big), axis=-1, keepdims=True)
    emask2 = emask & (lanef != i1)
    el2 = jnp.where(emask2, lg, NEG)
    v2 = jnp.max(el2, axis=-1, keepdims=True)
    i2 = jnp.min(jnp.where(emask2 & (el2 == v2), lanef, big), axis=-1, keepdims=True)
    e2 = jnp.exp(v2 - v1)
    den = 1.0 / (1.0 + e2)
    g1 = w_g * den
    g2 = w_g * e2 * den
    oh1 = lanef == i1
    oh2 = lanef == i2
    onehot = (oh1 | oh2).astype(BF16)
    before = jnp.dot(tri_ref[...], onehot, preferred_element_type=F32) + carry_ref[...]
    rank1 = jnp.sum(jnp.where(oh1, before, 0.0), axis=-1, keepdims=True)
    rank2 = jnp.sum(jnp.where(oh2, before, 0.0), axis=-1, keepdims=True)
    carry_ref[...] += jnp.sum(onehot.astype(F32), axis=0, keepdims=True)
    cnt_ref[...] += lax.dot_general(onehot, jnp.ones((lg.shape[0], LANES), BF16),
                                    (((0,), (0,)), ((), ())), preferred_element_type=F32)
    rec = jnp.zeros_like(lg)
    for slot, val in ((R_E1, i1), (R_E2, i2), (R_RANK1, rank1),
                      (R_RANK2, rank2), (R_G1, g1), (R_G2, g2)):
        rec = jnp.where(lane == slot, val, rec)
    rec_ref[...] = rec


def route(logits):
    n = logits.shape[0]
    tm = ROUTE_TM
    r = jnp.arange(tm)
    tri = (r[:, None] > r[None, :]).astype(BF16)
    return pl.pallas_call(
        _route_kernel,
        grid=(n // tm,),
        in_specs=[pl.BlockSpec((tm, LANES), lambda i: (i, 0)),
                  pl.BlockSpec((tm, tm), lambda i: (0, 0))],
        out_specs=[pl.BlockSpec((tm, LANES), lambda i: (i, 0)),
                   pl.BlockSpec((LANES, LANES), lambda i: (0, 0))],
        out_shape=[jax.ShapeDtypeStruct((n, LANES), F32),
                   jax.ShapeDtypeStruct((LANES, LANES), F32)],
        scratch_shapes=[pltpu.VMEM((1, LANES), F32)],
        compiler_params=_cparams("arbitrary"),
        name="route",
    )(logits, tri)


def _dest_kernel(rec_ref, cnt_ref, dest_ref, ends_ref):
    cnt = cnt_ref[...]
    padded = jnp.floor((cnt + (FFN_BLOCK - 1)) * (1.0 / FFN_BLOCK)) * FFN_BLOCK
    row = lax.broadcasted_iota(I32, cnt.shape, 0)
    col = lax.broadcasted_iota(I32, cnt.shape, 1)
    starts = jnp.sum(jnp.where(row < col, padded, 0.0), axis=0, keepdims=True)
    ends_ref[...] = jnp.sum(jnp.where(row <= col, padded, 0.0), axis=0, keepdims=True)
    rec = rec_ref[...]
    lanef = lax.broadcasted_iota(I32, rec.shape, 1).astype(F32)
    lane = lax.broadcasted_iota(I32, rec.shape, 1)
    s1 = jnp.sum(jnp.where(lanef == rec[:, R_E1:R_E1 + 1], starts, 0.0), axis=-1, keepdims=True)
    s2 = jnp.sum(jnp.where(lanef == rec[:, R_E2:R_E2 + 1], starts, 0.0), axis=-1, keepdims=True)
    d1 = s1 + rec[:, R_RANK1:R_RANK1 + 1]
    d2 = s2 + rec[:, R_RANK2:R_RANK2 + 1]
    dest_ref[...] = jnp.where(lane == 0, d1, jnp.where(lane == 1, d2, 0.0)).astype(I32)


def dest_rows(rec, counts):
    n = rec.shape[0]
    tm = ROUTE_TM
    return pl.pallas_call(
        _dest_kernel,
        grid=(n // tm,),
        in_specs=[pl.BlockSpec((tm, LANES), lambda i: (i, 0)),
                  pl.BlockSpec((LANES, LANES), lambda i: (0, 0))],
        out_specs=[pl.BlockSpec((tm, LANES), lambda i: (i, 0)),
                   pl.BlockSpec((1, LANES), lambda i: (0, 0))],
        out_shape=[jax.ShapeDtypeStruct((n, LANES), I32),
                   jax.ShapeDtypeStruct((1, LANES), F32)],
        compiler_params=_cparams("arbitrary"),
        name="dest_rows",
    )(rec, counts)


DISPATCH_TM = 256


def _dispatch_kernel(d1_ref, d2_ref, ends_ref, nv_ref, h2_ref, xs_ref, zero_buf, sem, zsem):
    i = pl.program_id(0)
    base = i * DISPATCH_TM
    n_blocks = xs_ref.shape[0] // FFN_BLOCK

    @pl.when(i == 0)
    def _():
        zero_buf[...] = jnp.zeros_like(zero_buf)

        def zero_copy(row0):
            return pltpu.make_async_copy(zero_buf, xs_ref.at[pl.ds(row0, FFN_BLOCK)], zsem)

        def seg_end(e):
            return ends_ref[e], ends_ref[e] > jnp.where(e > 0, ends_ref[jnp.maximum(e - 1, 0)], 0)

        def start_e(e, c):
            end, nonempty = seg_end(e)

            @pl.when(nonempty)
            def _():
                zero_copy(pl.multiple_of(end - FFN_BLOCK, FFN_BLOCK)).start()
            return c
        lax.fori_loop(0, N_EXPERTS, start_e, 0)

        def start_b(blk, c):
            zero_copy(pl.multiple_of(blk * FFN_BLOCK, FFN_BLOCK)).start()
            return c
        lax.fori_loop(nv_ref[0], n_blocks, start_b, 0)

        def wait_e(e, c):
            _, nonempty = seg_end(e)

            @pl.when(nonempty)
            def _():
                zero_copy(0).wait()
            return c
        lax.fori_loop(0, N_EXPERTS, wait_e, 0)

        def wait_b(blk, c):
            zero_copy(0).wait()
            return c
        lax.fori_loop(nv_ref[0], n_blocks, wait_b, 0)

    def row_copy(t, dest):
        return pltpu.make_async_copy(h2_ref.at[pl.ds(t, 1)], xs_ref.at[pl.ds(dest, 1)], sem)

    def issue(t, c):
        row_copy(t, d1_ref[base + t]).start()
        row_copy(t, d2_ref[base + t]).start()
        return c
    lax.fori_loop(0, DISPATCH_TM, issue, 0)

    def drain(t, c):
        row_copy(0, 0).wait()
        row_copy(0, 0).wait()
        return c
    lax.fori_loop(0, DISPATCH_TM, drain, 0)


def dispatch(h2, dest1, dest2, seg_ends, n_valid, n_rows):
    n = h2.shape[0]
    return pl.pallas_call(
        _dispatch_kernel,
        grid_spec=pltpu.PrefetchScalarGridSpec(
            num_scalar_prefetch=4,
            grid=(n // DISPATCH_TM,),
            in_specs=[pl.BlockSpec((DISPATCH_TM, D_MODEL), lambda i, *_: (i, 0))],
            out_specs=pl.BlockSpec(memory_space=pl.ANY),
            scratch_shapes=[pltpu.VMEM((FFN_BLOCK, D_MODEL), F32),
                            pltpu.SemaphoreType.DMA(()),
                            pltpu.SemaphoreType.DMA(())],
        ),
        out_shape=jax.ShapeDtypeStruct((n_rows, D_MODEL), F32),
        compiler_params=_cparams("arbitrary"),
        name="dispatch",
    )(dest1, dest2, seg_ends, n_valid, h2)


def _ffn_kernel(be_ref, nv_ref, xs_ref, wg_ref, wu_ref, wd_ref, ys_ref, wg_sc, wu_sc, wd_sc):
    i = pl.program_id(0)
    prev = be_ref[jnp.maximum(i - 1, 0)]
    fresh = (i == 0) | (be_ref[i] != prev)

    @pl.when(fresh)
    def _():
        wg_sc[...] = wg_ref[0].astype(BF16)
        wu_sc[...] = wu_ref[0].astype(BF16)
        wd_sc[...] = wd_ref[0].astype(BF16)

    @pl.when(i < nv_ref[0])
    def _():
        xb = xs_ref[...].astype(BF16)
        gate = jnp.dot(xb, wg_sc[...], preferred_element_type=F32)
        up = jnp.dot(xb, wu_sc[...], preferred_element_type=F32)
        hid = (gate * (1.0 / (1.0 + jnp.exp(-gate))) * up).astype(BF16)
        ys_ref[...] = jnp.dot(hid, wd_sc[...], preferred_element_type=F32)

    @pl.when(i >= nv_ref[0])
    def _():
        ys_ref[...] = jnp.zeros_like(ys_ref)


def expert_ffn(xs, block_expert, n_valid, w_gate, w_up, w_down):
    n_rows = xs.shape[0]
    n_blocks = n_rows // FFN_BLOCK
    wmap = lambda i, be, nv: (be[i], 0, 0)
    return pl.pallas_call(
        _ffn_kernel,
        grid_spec=pltpu.PrefetchScalarGridSpec(
            num_scalar_prefetch=2,
            grid=(n_blocks,),
            in_specs=[pl.BlockSpec((FFN_BLOCK, D_MODEL), lambda i, be, nv: (jnp.minimum(i, nv[0] - 1), 0)),
                      pl.BlockSpec((1, D_MODEL, D_EXPERT), wmap),
                      pl.BlockSpec((1, D_MODEL, D_EXPERT), wmap),
                      pl.BlockSpec((1, D_EXPERT, D_MODEL), wmap)],
            out_specs=pl.BlockSpec((FFN_BLOCK, D_MODEL), lambda i, be, nv: (i, 0)),
            scratch_shapes=[pltpu.VMEM((D_MODEL, D_EXPERT), BF16),
                            pltpu.VMEM((D_MODEL, D_EXPERT), BF16),
                            pltpu.VMEM((D_EXPERT, D_MODEL), BF16)],
        ),
        out_shape=jax.ShapeDtypeStruct((n_rows, D_MODEL), F32),
        compiler_params=_cparams("arbitrary"),
        name="expert_ffn",
    )(block_expert, n_valid, xs, w_gate, w_up, w_down)


COMBINE_RES = 2
COMBINE_TM = COMBINE_RES * RES_LEN


def _combine_kernel(d1_ref, d2_ref, x1_ref, rec_ref, gf_ref, ys_ref, out_ref, buf1, buf2, sem):
    i = pl.program_id(0)
    base = i * COMBINE_TM

    def row_copy(src_row, buf, t):
        return pltpu.make_async_copy(ys_ref.at[pl.ds(src_row, 1)], buf.at[pl.ds(t, 1)], sem)

    def issue(t, c):
        row_copy(d1_ref[base + t], buf1, t).start()
        row_copy(d2_ref[base + t], buf2, t).start()
        return c
    lax.fori_loop(0, COMBINE_TM, issue, 0)

    def drain(t, c):
        row_copy(0, buf1, 0).wait()
        row_copy(0, buf2, 0).wait()
        return c
    lax.fori_loop(0, COMBINE_TM, drain, 0)

    rec = rec_ref[...]
    g1 = rec[:, R_G1:R_G1 + 1]
    g2 = rec[:, R_G2:R_G2 + 1]
    x = x1_ref[...] + g1 * buf1[...] + g2 * buf2[...]
    ms = jnp.mean(x * x, axis=-1, keepdims=True)
    res = x * lax.rsqrt(ms + EPS) * gf_ref[...]
    for rl in range(COMBINE_RES):
        out_ref[:, rl * D_MODEL:(rl + 1) * D_MODEL] = res[rl * RES_LEN:(rl + 1) * RES_LEN]


def combine(x1, rec, norm_f_g, ys, dest1, dest2, batch):
    n = x1.shape[0]
    tm = COMBINE_TM
    per_b = N_RES // COMBINE_RES
    out = pl.pallas_call(
        _combine_kernel,
        grid_spec=pltpu.PrefetchScalarGridSpec(
            num_scalar_prefetch=2,
            grid=(n // tm,),
            in_specs=[pl.BlockSpec((tm, D_MODEL), lambda i, d1, d2: (i, 0)),
                      pl.BlockSpec((tm, LANES), lambda i, d1, d2: (i, 0)),
                      pl.BlockSpec((1, D_MODEL), lambda i, d1, d2: (0, 0)),
                      pl.BlockSpec(memory_space=pl.ANY)],
            out_specs=pl.BlockSpec((None, RES_LEN, COMBINE_RES * D_MODEL),
                                   lambda i, d1, d2: (i // per_b, 0, i % per_b)),
            scratch_shapes=[pltpu.VMEM((tm, D_MODEL), F32),
                            pltpu.VMEM((tm, D_MODEL), F32),
                            pltpu.SemaphoreType.DMA(())],
        ),
        out_shape=jax.ShapeDtypeStruct((batch, RES_LEN, N_RES * D_MODEL), F32),
        compiler_params=_cparams("arbitrary"),
        name="combine",
    )(dest1, dest2, x1, rec, norm_f_g.reshape(1, D_MODEL), ys)
    return out.reshape(batch, SEQ, D_MODEL)


def _router_weights(w_group, b_group, w_router, b_router):
    w = jnp.zeros((D_MODEL, LANES), F32)
    w = w.at[:, :N_GROUPS].set(w_group).at[:, EXPERT_LANE0:EXPERT_LANE0 + N_EXPERTS].set(w_router)
    b = jnp.zeros((1, LANES), F32)
    b = b.at[0, :N_GROUPS].set(b_group).at[0, EXPERT_LANE0:EXPERT_LANE0 + N_EXPERTS].set(b_router)
    hi = w.astype(BF16)
    lo = (w - hi.astype(F32)).astype(BF16)
    return hi, lo, b


def _block_plan(ends_row, n_rows):
    seg_ends = ends_row[0, EXPERT_LANE0:EXPERT_LANE0 + N_EXPERTS].astype(I32)
    n_blocks = n_rows // FFN_BLOCK
    blk_start = jnp.arange(n_blocks, dtype=I32) * FFN_BLOCK
    block_expert = jnp.minimum(jnp.sum((seg_ends[None, :] <= blk_start[:, None]).astype(I32), axis=1),
                               N_EXPERTS - 1)
    n_valid = seg_ends[-1:] // FFN_BLOCK
    return seg_ends, block_expert, n_valid


def kernel(x, norm1_g, w_in, attn_norm_g, hgrn_gamma, hgrn_norm_g, w_out, norm2_g, w_group, b_group,
           w_router, b_router, w_gate, w_up, w_down, norm_f_g):
    batch, seq, d = x.shape
    assert seq == SEQ and d == D_MODEL and norm1_g.shape[0] == 1
    n = batch * seq
    qkv, hg = in_proj(x, norm1_g[0], w_in[0].astype(BF16))
    oa = attention(qkv)
    yh = hgrn(hg, hgrn_gamma, hgrn_norm_g[0])
    wr_hi, wr_lo, br = _router_weights(w_group[0], b_group[0], w_router[0], b_router[0])
    x1, h2, logits = post_mix(oa, yh, x, attn_norm_g[0], w_out[0].astype(BF16), norm2_g[0],
                              wr_hi, wr_lo, br)
    rec, counts = route(logits)
    dest, ends_row = dest_rows(rec, counts)
    n_rows = 2 * n + N_EXPERTS * FFN_BLOCK
    seg_ends, block_expert, n_valid = _block_plan(ends_row, n_rows)
    dest1, dest2 = dest[:, 0], dest[:, 1]
    xs = dispatch(h2, dest1, dest2, seg_ends, n_valid, n_rows)
    ys = expert_ffn(xs, block_expert, n_valid, w_gate[0], w_up[0], w_down[0])
    return combine(x1, rec, norm_f_g, ys, dest1, dest2, batch)
```

```python
import jax
import jax.numpy as jnp
from jax import lax
from jax.experimental import pallas as pl
from jax.experimental.pallas import tpu as pltpu

F32 = jnp.float32
BF16 = jnp.bfloat16
I32 = jnp.int32

D_MODEL = 1024
HEAD_DIM = 64
N_HEADS = 8
WIDTH = N_HEADS * HEAD_DIM
QKV_W = 3 * WIDTH
HG_W = 4 * WIDTH
SEQ = 2048
ATTN_BLOCK = 128
DILATIONS = (1, 4, 16)
N_RES = 16
RES_LEN = SEQ // N_RES
TILE_TOKENS = 512
U_PER_TILE = TILE_TOKENS // N_RES
HGRN_CHUNK = 32
HGRN_SUPER = 256
N_GROUPS = 4
EXPERTS_PER_GROUP = 8
N_EXPERTS = 32
D_EXPERT = 512
EXPERT_LANE0 = 32
FFN_BLOCK = 256
EPS = 1e-6
NEG = -1e30
LOG2E = 1.4426950408889634
LANES = 128
VMEM_LIMIT = 56 * 1024 * 1024


def _cparams(*sem):
    return pltpu.CompilerParams(dimension_semantics=sem, vmem_limit_bytes=VMEM_LIMIT)


N_LANE_TILES = WIDTH // LANES


def _tile_permutation():
    i = jnp.arange(TILE_TOKENS)
    src = N_RES * (i % U_PER_TILE) + i // U_PER_TILE
    return (src[:, None] == jnp.arange(TILE_TOKENS)[None, :]).astype(BF16)


def _in_proj_kernel(x_ref, g_ref, w_ref, perm_ref, qkv_ref, hg_ref):
    x = x_ref[...]
    ms = jnp.mean(x * x, axis=-1, keepdims=True)
    h = (x * lax.rsqrt(ms + EPS) * g_ref[...]).astype(BF16)
    hp = jnp.dot(perm_ref[...], h, preferred_element_type=F32).astype(BF16)
    for c in range(QKV_W // WIDTH):
        res = jnp.dot(hp, w_ref[:, c * WIDTH:(c + 1) * WIDTH], preferred_element_type=F32)
        for r in range(N_RES):
            for g in range(N_LANE_TILES):
                qkv_ref[r, g, c] = res[r * U_PER_TILE:(r + 1) * U_PER_TILE, g * LANES:(g + 1) * LANES]
    for j in range(HG_W // WIDTH):
        wsl = slice(QKV_W + j * WIDTH, QKV_W + (j + 1) * WIDTH)
        hg_ref[:, j * WIDTH:(j + 1) * WIDTH] = jnp.dot(h, w_ref[:, wsl], preferred_element_type=F32).astype(BF16)


def in_proj(x2, g, w_bf16, batch):
    n = x2.shape[0]
    tiles_per_b = SEQ // TILE_TOKENS
    return pl.pallas_call(
        _in_proj_kernel,
        grid=(n // TILE_TOKENS,),
        in_specs=[
            pl.BlockSpec((TILE_TOKENS, D_MODEL), lambda i: (i, 0)),
            pl.BlockSpec((1, D_MODEL), lambda i: (0, 0)),
            pl.BlockSpec((D_MODEL, QKV_W + HG_W), lambda i: (0, 0)),
            pl.BlockSpec((TILE_TOKENS, TILE_TOKENS), lambda i: (0, 0)),
        ],
        out_specs=[
            pl.BlockSpec((None, N_RES, N_LANE_TILES, 3, U_PER_TILE, LANES),
                         lambda i: (i // tiles_per_b, 0, 0, 0, i % tiles_per_b, 0)),
            pl.BlockSpec((TILE_TOKENS, HG_W), lambda i: (i, 0)),
        ],
        out_shape=[
            jax.ShapeDtypeStruct((batch, N_RES, N_LANE_TILES, 3, RES_LEN, LANES), F32),
            jax.ShapeDtypeStruct((n, HG_W), BF16),
        ],
        compiler_params=_cparams("arbitrary"),
        name="in_proj",
    )(x2, g.reshape(1, D_MODEL), w_bf16, _tile_permutation())


ATTN_GROUP = 4
ATTN_GROUP1 = 3
HEADS_PER_TILE = LANES // HEAD_DIM


def _attn_bias_tables():
    slopes = 2.0 ** (-8.0 * jnp.arange(1, N_HEADS + 1, dtype=F32) / N_HEADS)

    def table(qpos, kpos, dil):
        sd = qpos[:, None] - kpos[None, :]
        b = -slopes[:, None, None] * (sd * dil).astype(F32)[None] * LOG2E
        return jnp.where(((sd >= 0) & (sd <= ATTN_BLOCK))[None], b, NEG)

    q = jnp.arange(ATTN_BLOCK)
    k = jnp.arange(2 * ATTN_BLOCK)
    q1 = 16 * (q % 8) + q // 8 + ATTN_BLOCK
    k1 = 16 * (k % 16) + k // 16
    q4 = 4 * (q % 32) + q // 32 + ATTN_BLOCK
    k4 = 4 * (k % 64) + k // 64
    q16 = q + ATTN_BLOCK
    none = jnp.full((N_HEADS, ATTN_BLOCK, 2 * ATTN_BLOCK), NEG, F32)
    tabs = [jnp.concatenate([table(q1, k1, 1), table(q1, q1, 1)], axis=-1),
            jnp.concatenate([table(q4, k4, 4), table(q4, q4, 4)], axis=-1),
            jnp.concatenate([none, table(q16, q16, 16)], axis=-1)]
    return jnp.stack(tabs, axis=0)


def _attn_blocks(blocks, bias_ref, d_idx, with_prev):
    lane = lax.broadcasted_iota(I32, (ATTN_BLOCK, LANES), 1)
    head0 = lane < HEAD_DIM
    heads = [(b, h) for b in range(len(blocks)) for h in range(HEADS_PER_TILE)]
    if with_prev:
        bias = [bias_ref[d_idx, h, :, :2 * ATTN_BLOCK] for h in range(HEADS_PER_TILE)]
    else:
        bias = [bias_ref[d_idx, h, :, 2 * ATTN_BLOCK:] for h in range(HEADS_PER_TILE)]
    qs = [q * (HEAD_DIM ** -0.5 * LOG2E) for q, _, _ in blocks]
    ks = [k.astype(BF16) for _, k, _ in blocks]
    vs = [v.astype(BF16) for _, _, v in blocks]
    v_ones = [jnp.concatenate([v, jnp.ones_like(v)], axis=-1) for v in vs]
    qh = {(b, h): jnp.where(head0 if h == 0 else jnp.logical_not(head0), qs[b], 0.0).astype(BF16)
          for b, h in heads}
    s = {(b, h): lax.dot_general(qh[b, h], ks[b], (((1,), (1,)), ((), ())),
                                 preferred_element_type=F32) + bias[h] for b, h in heads}
    m = {bh: jnp.max(s[bh], axis=-1, keepdims=True) for bh in heads}
    p = {bh: jnp.exp2(s[bh] - m[bh]).astype(BF16) for bh in heads}
    acc = {(b, h): jnp.dot(p[b, h], v_ones[b], preferred_element_type=F32) for b, h in heads}
    head0_wide = jnp.concatenate([head0, head0], axis=-1)
    outs = []
    for b in range(len(blocks)):
        ol = jnp.where(head0_wide, acc[b, 0], acc[b, 1])
        l = ol[:, LANES:]
        outs.append((ol[:, :LANES] * (1.0 / l), jnp.where(head0, m[b, 0], m[b, 1]) + jnp.log2(l)))
    return outs


def _attn_kernel(qkv_ref, bias_ref, o_ref, o4_ref, l4_ref, o16_ref, l16_ref, fin_ref):
    grp = ATTN_GROUP

    def body16(j, c):
        rs = [j * grp + a for a in range(grp)]
        res = _attn_blocks([(qkv_ref[r, 0], qkv_ref[r, 1], qkv_ref[r, 2]) for r in rs], bias_ref, 2, False)
        for r, (o, l) in zip(rs, res):
            o16_ref[r] = o
            l16_ref[r] = l
        return c
    lax.fori_loop(0, N_RES // grp, body16, 0)

    def gather4(c, rho, u0, nu):
        return jnp.concatenate([qkv_ref[rho + 4 * a, c, pl.ds(u0, nu), :] for a in range(4)], axis=0)

    def store4(rho, u0, o, l):
        for a in range(4):
            o4_ref[rho + 4 * a, pl.ds(u0, 32), :] = o[a * 32:(a + 1) * 32]
            l4_ref[rho + 4 * a, pl.ds(u0, 32), :] = l[a * 32:(a + 1) * 32]

    res = _attn_blocks([tuple(gather4(c, rho, 0, 32) for c in range(3)) for rho in range(4)],
                       bias_ref, 1, False)
    for rho, (o, l) in enumerate(res):
        store4(rho, 0, o, l)

    def body4(j, c):
        u0 = pl.multiple_of((j + 1) * 32, 32)
        res = _attn_blocks([(gather4(0, rho, u0, 32), gather4(1, rho, u0 - 32, 64),
                             gather4(2, rho, u0 - 32, 64)) for rho in range(4)], bias_ref, 1, True)
        for rho, (o, l) in enumerate(res):
            store4(rho, u0, o, l)
        return c
    lax.fori_loop(0, 3, body4, 0)

    def gather1(ref, u0, nu, *lead):
        return jnp.concatenate([ref[(r,) + lead + (pl.ds(u0, nu), slice(None))] for r in range(N_RES)],
                               axis=0)

    def merge_and_store(u0, o1, l1):
        l4 = gather1(l4_ref, u0, 8)
        l16 = gather1(l16_ref, u0, 8)
        mx = jnp.maximum(jnp.maximum(l1, l4), l16)
        e1 = jnp.exp2(l1 - mx)
        e4 = jnp.exp2(l4 - mx)
        e16 = jnp.exp2(l16 - mx)
        inv = 1.0 / (e1 + e4 + e16)
        o = (e1 * inv) * o1 + (e4 * inv) * gather1(o4_ref, u0, 8) + (e16 * inv) * gather1(o16_ref, u0, 8)
        for r in range(N_RES):
            fin_ref[r, pl.ds(u0, 8), :] = o[r * 8:(r + 1) * 8]

    (o, l), = _attn_blocks([tuple(gather1(qkv_ref, 0, 8, c) for c in range(3))], bias_ref, 0, False)
    merge_and_store(0, o, l)

    n_blocks1 = RES_LEN // 8 - 1

    def body1(j, c):
        u0s = [pl.multiple_of((1 + j * ATTN_GROUP1 + a) * 8, 8) for a in range(ATTN_GROUP1)]
        res = _attn_blocks([(gather1(qkv_ref, u0, 8, 0), gather1(qkv_ref, u0 - 8, 16, 1),
                             gather1(qkv_ref, u0 - 8, 16, 2)) for u0 in u0s], bias_ref, 0, True)
        for u0, (o, l) in zip(u0s, res):
            merge_and_store(u0, o, l)
        return c
    lax.fori_loop(0, n_blocks1 // ATTN_GROUP1, body1, 0)

    def emit(r, c):
        o_ref[r] = fin_ref[r].astype(o_ref.dtype)
        return c
    lax.fori_loop(0, N_RES, emit, 0)


def attention(qkv):
    batch = qkv.shape[0]
    scratch = pltpu.VMEM((N_RES, RES_LEN, LANES), F32)
    return pl.pallas_call(
        _attn_kernel,
        grid=(batch, N_LANE_TILES),
        in_specs=[
            pl.BlockSpec((None, N_RES, None, 3, RES_LEN, LANES), lambda b, g: (b, 0, g, 0, 0, 0)),
            pl.BlockSpec((len(DILATIONS), HEADS_PER_TILE, ATTN_BLOCK, 3 * ATTN_BLOCK),
                         lambda b, g: (0, g, 0, 0)),
        ],
        out_specs=pl.BlockSpec((None, N_RES, None, RES_LEN, LANES), lambda b, g: (b, 0, g, 0, 0)),
        out_shape=jax.ShapeDtypeStruct((batch, N_RES, N_LANE_TILES, RES_LEN, LANES), BF16),
        scratch_shapes=[scratch] * 5,
        compiler_params=_cparams("arbitrary", "arbitrary"),
        name="dilated_attention",
    )(qkv, _attn_bias_tables())


def _split3(x):
    a = x.astype(BF16)
    r = x - a.astype(F32)
    b = r.astype(BF16)
    c = (r - b.astype(F32)).astype(BF16)
    return a, b, c


def _hgrn_kernel(q_ref, f_ref, i_ref, g_ref, gamma_ref, ng_ref, tri_ref, ones_ref, ind_ref, y_ref):
    sup, ch = HGRN_SUPER, HGRN_CHUNK
    n_ch = sup // ch
    gam = gamma_ref[...]
    gmx = jnp.max(gam, axis=0, keepdims=True)
    ge = jnp.exp(gam - gmx)
    lb = ge[0:1] / jnp.sum(ge, axis=0, keepdims=True)
    ng = ng_ref[...]
    tri = tri_ref[...]
    ones_bd = ones_ref[...]
    ind = ind_ref[...]
    ri = lax.broadcasted_iota(I32, (sup, sup), 0)
    ci = lax.broadcasted_iota(I32, (sup, sup), 1)
    causal_bd = (ri // ch == ci // ch) & (ri >= ci)
    head0 = lax.broadcasted_iota(I32, (sup, LANES), 1) < HEAD_DIM
    row_chunk = lax.broadcasted_iota(I32, (sup, LANES), 0) // ch
    same_head = (lax.broadcasted_iota(I32, (LANES, LANES), 0) // HEAD_DIM
                 == lax.broadcasted_iota(I32, (LANES, LANES), 1) // HEAD_DIM)

    def body(sb, state):
        rows = pl.ds(pl.multiple_of(sb * sup, sup), sup)
        fr = f_ref[rows, :].astype(F32)
        t = jnp.exp(-jnp.abs(fr))
        rcp = 1.0 / (1.0 + t)
        pos = fr >= 0
        sig = jnp.where(pos, rcp, t * rcp)
        nsig = jnp.where(pos, t * rcp, rcp)
        logf = jnp.log(lb + (1.0 - lb) * sig)
        key = (1.0 - lb) * nsig
        la, lbb, lc = _split3(logf)
        cum = lambda m_: (jnp.dot(m_, la, preferred_element_type=F32)
                          + jnp.dot(m_, lbb, preferred_element_type=F32)
                          + jnp.dot(m_, lc, preferred_element_type=F32))
        b = cum(tri)
        b_last = cum(ones_bd)
        tdot = lambda part: lax.dot_general(part, ind, (((0,), (0,)), ((), ())), preferred_element_type=F32)
        decay_t = jnp.exp(tdot(la) + tdot(lbb) + tdot(lc))
        qv = q_ref[rows, :].astype(F32)
        qs = qv * (1.0 / (1.0 + jnp.exp(-qv)))
        q_in = (qs * jnp.exp(b)).astype(BF16)
        k_in = (key * jnp.exp(-b)).astype(BF16)
        k_end = (key * jnp.exp(b_last - b)).astype(BF16)
        vv = i_ref[rows, :]
        gv = g_ref[rows, :].astype(F32)
        gate = gv * (1.0 / (1.0 + jnp.exp(-gv)))
        zero = jnp.zeros_like(vv)
        q_heads = [jnp.where(head0, q_in, zero), jnp.where(head0, zero, q_in)]
        att = [lax.dot_general(qh, k_in, (((1,), (1,)), ((), ())), preferred_element_type=F32)
               for qh in q_heads]
        att = [jnp.where(causal_bd, a, 0.0).astype(BF16) for a in att]
        intra = [jnp.dot(a, vv, preferred_element_type=F32) for a in att]
        o_intra = jnp.where(head0, intra[0], intra[1])
        v_exp = jnp.concatenate([jnp.where(row_chunk == c, vv, zero) for c in range(n_ch)], axis=1)
        ds_all = lax.dot_general(k_end, v_exp, (((0,), (0,)), ((), ())), preferred_element_type=F32)
        s_prev = []
        for c in range(n_ch):
            s_prev.append(state.astype(BF16))
            ds = jnp.where(same_head, ds_all[:, c * LANES:(c + 1) * LANES], 0.0)
            state = decay_t[:, c:c + 1] * state + ds
        q_exp = jnp.concatenate([jnp.where(row_chunk == c, q_in, zero) for c in range(n_ch)], axis=1)
        o_inter = jnp.dot(q_exp, jnp.concatenate(s_prev, axis=0), preferred_element_type=F32)
        o = o_intra + o_inter
        sq = o * o
        ss0 = jnp.sum(jnp.where(head0, sq, 0.0), axis=-1, keepdims=True)
        ss1 = jnp.sum(jnp.where(head0, 0.0, sq), axis=-1, keepdims=True)
        ms = jnp.where(head0, ss0, ss1) * (1.0 / HEAD_DIM)
        y_ref[rows, :] = (o * lax.rsqrt(ms + EPS) * ng * gate).astype(y_ref.dtype)
        return state

    lax.fori_loop(0, SEQ // sup, body, jnp.zeros((LANES, LANES), F32))


def hgrn(hg3, gamma, norm_g):
    batch = hg3.shape[0]
    r = jnp.arange(HGRN_SUPER)
    same = (r[:, None] // HGRN_CHUNK) == (r[None, :] // HGRN_CHUNK)
    tri = (same & (r[:, None] >= r[None, :])).astype(BF16)
    ones_bd = same.astype(BF16)
    ind = ((r[:, None] // HGRN_CHUNK) == jnp.arange(LANES)[None, :]).astype(BF16)
    col = lambda which: (lambda b, g: (b, 0, which * N_LANE_TILES + g))
    const2 = lambda b, g: (0, 0)
    return pl.pallas_call(
        _hgrn_kernel,
        grid=(batch, N_LANE_TILES),
        in_specs=[
            pl.BlockSpec((None, SEQ, LANES), col(0)),
            pl.BlockSpec((None, SEQ, LANES), col(1)),
            pl.BlockSpec((None, SEQ, LANES), col(2)),
            pl.BlockSpec((None, SEQ, LANES), col(3)),
            pl.BlockSpec((2, LANES), lambda b, g: (0, g)),
            pl.BlockSpec((1, LANES), lambda b, g: (0, g)),
            pl.BlockSpec((HGRN_SUPER, HGRN_SUPER), const2),
            pl.BlockSpec((HGRN_SUPER, HGRN_SUPER), const2),
            pl.BlockSpec((HGRN_SUPER, LANES), const2),
        ],
        out_specs=pl.BlockSpec((None, SEQ, LANES), lambda b, g: (b, 0, g)),
        out_shape=jax.ShapeDtypeStruct((batch, SEQ, WIDTH), BF16),
        compiler_params=_cparams("arbitrary", "arbitrary"),
        name="hgrn2",
    )(hg3, hg3, hg3, hg3, gamma.astype(F32), norm_g.reshape(1, WIDTH).astype(F32), tri, ones_bd, ind)


def _post_mix_kernel(oa_ref, yh_ref, x_ref, ag_ref, wo_ref, g2_ref, wrh_ref, wrl_ref, br_ref, perm_t_ref,
                     x1_ref, h2_ref, lg_ref):
    oa = jnp.concatenate(
        [jnp.concatenate([oa_ref[r, g] for g in range(N_LANE_TILES)], axis=1)
         for r in range(N_RES)], axis=0).astype(F32)
    ms = jnp.mean(oa * oa, axis=-1, keepdims=True)
    ya = (oa * lax.rsqrt(ms + EPS) * ag_ref[...]).astype(BF16)
    ya = jnp.dot(perm_t_ref[...], ya, preferred_element_type=F32).astype(BF16)
    mix = (jnp.dot(ya, wo_ref[:WIDTH, :], preferred_element_type=F32)
           + jnp.dot(yh_ref[...], wo_ref[WIDTH:, :], preferred_element_type=F32))
    x1 = x_ref[...] + mix
    x1_ref[...] = x1
    ms2 = jnp.mean(x1 * x1, axis=-1, keepdims=True)
    h2 = x1 * lax.rsqrt(ms2 + EPS) * g2_ref[...]
    h2_ref[...] = h2
    hi = h2.astype(BF16)
    lo = (h2 - hi.astype(F32)).astype(BF16)
    wrh = wrh_ref[...]
    lg_ref[...] = (jnp.dot(hi, wrh, preferred_element_type=F32)
                   + jnp.dot(lo, wrh, preferred_element_type=F32)
                   + jnp.dot(hi, wrl_ref[...], preferred_element_type=F32)
                   + br_ref[...])


def post_mix(oa, yh, x2, attn_g, w_out_bf16, g2, wr_hi, wr_lo, br):
    n = x2.shape[0]
    tiles_per_b = SEQ // TILE_TOKENS
    row = lambda w: pl.BlockSpec((TILE_TOKENS, w), lambda i: (i, 0))
    const = lambda r, c: pl.BlockSpec((r, c), lambda i: (0, 0))
    return pl.pallas_call(
        _post_mix_kernel,
        grid=(n // TILE_TOKENS,),
        in_specs=[pl.BlockSpec((None, N_RES, N_LANE_TILES, U_PER_TILE, LANES),
                               lambda i: (i // tiles_per_b, 0, 0, i % tiles_per_b, 0)),
                  row(WIDTH), row(D_MODEL),
                  const(1, WIDTH), const(2 * WIDTH, D_MODEL), const(1, D_MODEL),
                  const(D_MODEL, LANES), const(D_MODEL, LANES), const(1, LANES),
                  const(TILE_TOKENS, TILE_TOKENS)],
        out_specs=[row(D_MODEL), row(D_MODEL), row(LANES)],
        out_shape=[jax.ShapeDtypeStruct((n, D_MODEL), F32),
                   jax.ShapeDtypeStruct((n, D_MODEL), F32),
                   jax.ShapeDtypeStruct((n, LANES), F32)],
        compiler_params=_cparams("arbitrary"),
        name="post_mix",
    )(oa, yh, x2, attn_g.reshape(1, WIDTH), w_out_bf16, g2.reshape(1, D_MODEL), wr_hi, wr_lo, br,
      _tile_permutation().T)


ROUTE_TM = 512
R_E1, R_E2, R_RANK1, R_RANK2, R_G1, R_G2 = 0, 1, 2, 3, 4, 5


def _route_kernel(lg_ref, tri_ref, rec_ref, cnt_ref, carry_ref):
    i = pl.program_id(0)

    @pl.when(i == 0)
    def _():
        carry_ref[...] = jnp.zeros_like(carry_ref)
        cnt_ref[...] = jnp.zeros_like(cnt_ref)

    lg = lg_ref[...]
    lane = lax.broadcasted_iota(I32, lg.shape, 1)
    lanef = lane.astype(F32)
    big = float(LANES)
    gmask = lane < N_GROUPS
    gl = jnp.where(gmask, lg, NEG)
    gmax = jnp.max(gl, axis=-1, keepdims=True)
    gsel = jnp.min(jnp.where(gmask & (gl == gmax), lanef, big), axis=-1, keepdims=True)
    gsum = jnp.sum(jnp.where(gmask, jnp.exp(gl - gmax), 0.0), axis=-1, keepdims=True)
    w_g = 1.0 / gsum
    lo = EXPERT_LANE0 + EXPERTS_PER_GROUP * gsel
    emask = (lanef >= lo) & (lanef < lo + EXPERTS_PER_GROUP)
    el = jnp.where(emask, lg, NEG)
    v1 = jnp.max(el, axis=-1, keepdims=True)
    i1 = jnp.min(jnp.where(emask & (el == v1), lanef, big), axis=-1, keepdims=True)
    emask2 = emask & (lanef != i1)
    el2 = jnp.where(emask2, lg, NEG)
    v2 = jnp.max(el2, axis=-1, keepdims=True)
    i2 = jnp.min(jnp.where(emask2 & (el2 == v2), lanef, big), axis=-1, keepdims=True)
    e2 = jnp.exp(v2 - v1)
    den = 1.0 / (1.0 + e2)
    g1 = w_g * den
    g2 = w_g * e2 * den
    oh1 = lanef == i1
    oh2 = lanef == i2
    onehot = (oh1 | oh2).astype(BF16)
    before = jnp.dot(tri_ref[...], onehot, preferred_element_type=F32) + carry_ref[...]
    rank1 = jnp.sum(jnp.where(oh1, before, 0.0), axis=-1, keepdims=True)
    rank2 = jnp.sum(jnp.where(oh2, before, 0.0), axis=-1, keepdims=True)
    carry_ref[...] += jnp.sum(onehot.astype(F32), axis=0, keepdims=True)
    cnt_ref[...] += lax.dot_general(onehot, jnp.ones((lg.shape[0], LANES), BF16),
                                    (((0,), (0,)), ((), ())), preferred_element_type=F32)
    rec = jnp.zeros_like(lg)
    for slot, val in ((R_E1, i1), (R_E2, i2), (R_RANK1, rank1),
                      (R_RANK2, rank2), (R_G1, g1), (R_G2, g2)):
        rec = jnp.where(lane == slot, val, rec)
    rec_ref[...] = rec


def route(logits):
    n = logits.shape[0]
    tm = ROUTE_TM
    r = jnp.arange(tm)
    tri = (r[:, None] > r[None, :]).astype(BF16)
    return pl.pallas_call(
        _route_kernel,
        grid=(n // tm,),
        in_specs=[pl.BlockSpec((tm, LANES), lambda i: (i, 0)),
                  pl.BlockSpec((tm, tm), lambda i: (0, 0))],
        out_specs=[pl.BlockSpec((tm, LANES), lambda i: (i, 0)),
                   pl.BlockSpec((LANES, LANES), lambda i: (0, 0))],
        out_shape=[jax.ShapeDtypeStruct((n, LANES), F32),
                   jax.ShapeDtypeStruct((LANES, LANES), F32)],
        scratch_shapes=[pltpu.VMEM((1, LANES), F32)],
        compiler_params=_cparams("arbitrary"),
        name="route",
    )(logits, tri)


def _dest_kernel(rec_ref, cnt_ref, dest_ref, ends_ref):
    cnt = cnt_ref[...]
    padded = jnp.floor((cnt + (FFN_BLOCK - 1)) * (1.0 / FFN_BLOCK)) * FFN_BLOCK
    row = lax.broadcasted_iota(I32, cnt.shape, 0)
    col = lax.broadcasted_iota(I32, cnt.shape, 1)
    starts = jnp.sum(jnp.where(row < col, padded, 0.0), axis=0, keepdims=True)
    ends_ref[...] = jnp.sum(jnp.where(row <= col, padded, 0.0), axis=0, keepdims=True)
    rec = rec_ref[...]
    lanef = lax.broadcasted_iota(I32, rec.shape, 1).astype(F32)
    lane = lax.broadcasted_iota(I32, rec.shape, 1)
    s1 = jnp.sum(jnp.where(lanef == rec[:, R_E1:R_E1 + 1], starts, 0.0), axis=-1, keepdims=True)
    s2 = jnp.sum(jnp.where(lanef == rec[:, R_E2:R_E2 + 1], starts, 0.0), axis=-1, keepdims=True)
    d1 = s1 + rec[:, R_RANK1:R_RANK1 + 1]
    d2 = s2 + rec[:, R_RANK2:R_RANK2 + 1]
    dest_ref[...] = jnp.where(lane == 0, d1, jnp.where(lane == 1, d2, 0.0)).astype(I32)


def dest_rows(rec, counts):
    n = rec.shape[0]
    tm = ROUTE_TM
    return pl.pallas_call(
        _dest_kernel,
        grid=(n // tm,),
        in_specs=[pl.BlockSpec((tm, LANES), lambda i: (i, 0)),
                  pl.BlockSpec((LANES, LANES), lambda i: (0, 0))],
        out_specs=[pl.BlockSpec((tm, LANES), lambda i: (i, 0)),
                   pl.BlockSpec((1, LANES), lambda i: (0, 0))],
        out_shape=[jax.ShapeDtypeStruct((n, LANES), I32),
                   jax.ShapeDtypeStruct((1, LANES), F32)],
        compiler_params=_cparams("arbitrary"),
        name="dest_rows",
    )(rec, counts)


DISPATCH_TM = 256
ROW_DMA_UNROLL = 8


def _dispatch_kernel(d1_ref, d2_ref, ends_ref, nv_ref, h2_ref, xs_ref, zero_buf, sem, zsem):
    i = pl.program_id(0)
    base = i * DISPATCH_TM
    n_blocks = xs_ref.shape[0] // FFN_BLOCK

    @pl.when(i == 0)
    def _():
        zero_buf[...] = jnp.zeros_like(zero_buf)

        def zero_copy(row0):
            return pltpu.make_async_copy(zero_buf, xs_ref.at[pl.ds(row0, FFN_BLOCK)], zsem)

        def seg_end(e):
            return ends_ref[e], ends_ref[e] > jnp.where(e > 0, ends_ref[jnp.maximum(e - 1, 0)], 0)

        def start_e(e, c):
            end, nonempty = seg_end(e)

            @pl.when(nonempty)
            def _():
                zero_copy(pl.multiple_of(end - FFN_BLOCK, FFN_BLOCK)).start()
            return c
        lax.fori_loop(0, N_EXPERTS, start_e, 0)

        def start_b(blk, c):
            zero_copy(pl.multiple_of(blk * FFN_BLOCK, FFN_BLOCK)).start()
            return c
        lax.fori_loop(nv_ref[0], n_blocks, start_b, 0)

        def wait_e(e, c):
            _, nonempty = seg_end(e)

            @pl.when(nonempty)
            def _():
                zero_copy(0).wait()
            return c
        lax.fori_loop(0, N_EXPERTS, wait_e, 0)

        def wait_b(blk, c):
            zero_copy(0).wait()
            return c
        lax.fori_loop(nv_ref[0], n_blocks, wait_b, 0)

    def row_copy(t, dest):
        return pltpu.make_async_copy(h2_ref.at[pl.ds(t, 1)], xs_ref.at[pl.ds(dest, 1)], sem)

    def issue(t, c):
        row_copy(t, d1_ref[base + t]).start()
        row_copy(t, d2_ref[base + t]).start()
        return c
    lax.fori_loop(0, DISPATCH_TM, issue, 0, unroll=ROW_DMA_UNROLL)

    def drain(t, c):
        row_copy(0, 0).wait()
        row_copy(0, 0).wait()
        return c
    lax.fori_loop(0, DISPATCH_TM, drain, 0, unroll=ROW_DMA_UNROLL)


def dispatch(h2, dest1, dest2, seg_ends, n_valid, n_rows):
    n = h2.shape[0]
    return pl.pallas_call(
        _dispatch_kernel,
        grid_spec=pltpu.PrefetchScalarGridSpec(
            num_scalar_prefetch=4,
            grid=(n // DISPATCH_TM,),
            in_specs=[pl.BlockSpec((DISPATCH_TM, D_MODEL), lambda i, *_: (i, 0))],
            out_specs=pl.BlockSpec(memory_space=pl.ANY),
            scratch_shapes=[pltpu.VMEM((FFN_BLOCK, D_MODEL), F32),
                            pltpu.SemaphoreType.DMA(()),
                            pltpu.SemaphoreType.DMA(())],
        ),
        out_shape=jax.ShapeDtypeStruct((n_rows, D_MODEL), F32),
        compiler_params=_cparams("arbitrary"),
        name="dispatch",
    )(dest1, dest2, seg_ends, n_valid, h2)


def _ffn_kernel(be_ref, nv_ref, xs_ref, wg_ref, wu_ref, wd_ref, ys_ref, wg_sc, wu_sc, wd_sc):
    i = pl.program_id(0)
    prev = be_ref[jnp.maximum(i - 1, 0)]
    fresh = (i == 0) | (be_ref[i] != prev)

    @pl.when(fresh)
    def _():
        wg_sc[...] = wg_ref[0].astype(BF16)
        wu_sc[...] = wu_ref[0].astype(BF16)
        wd_sc[...] = wd_ref[0].astype(BF16)

    @pl.when(i < nv_ref[0])
    def _():
        xb = xs_ref[...].astype(BF16)
        gate = jnp.dot(xb, wg_sc[...], preferred_element_type=F32)
        up = jnp.dot(xb, wu_sc[...], preferred_element_type=F32)
        hid = (gate * (1.0 / (1.0 + jnp.exp(-gate))) * up).astype(BF16)
        ys_ref[...] = jnp.dot(hid, wd_sc[...], preferred_element_type=F32)

    @pl.when(i >= nv_ref[0])
    def _():
        ys_ref[...] = jnp.zeros_like(ys_ref)


def expert_ffn(xs, block_expert, n_valid, w_gate, w_up, w_down):
    n_rows = xs.shape[0]
    n_blocks = n_rows // FFN_BLOCK
    wmap = lambda i, be, nv: (be[i], 0, 0)
    return pl.pallas_call(
        _ffn_kernel,
        grid_spec=pltpu.PrefetchScalarGridSpec(
            num_scalar_prefetch=2,
            grid=(n_blocks,),
            in_specs=[pl.BlockSpec((FFN_BLOCK, D_MODEL), lambda i, be, nv: (jnp.minimum(i, nv[0] - 1), 0)),
                      pl.BlockSpec((1, D_MODEL, D_EXPERT), wmap),
                      pl.BlockSpec((1, D_MODEL, D_EXPERT), wmap),
                      pl.BlockSpec((1, D_EXPERT, D_MODEL), wmap)],
            out_specs=pl.BlockSpec((FFN_BLOCK, D_MODEL), lambda i, be, nv: (i, 0)),
            scratch_shapes=[pltpu.VMEM((D_MODEL, D_EXPERT), BF16),
                            pltpu.VMEM((D_MODEL, D_EXPERT), BF16),
                            pltpu.VMEM((D_EXPERT, D_MODEL), BF16)],
        ),
        out_shape=jax.ShapeDtypeStruct((n_rows, D_MODEL), F32),
        compiler_params=_cparams("arbitrary"),
        name="expert_ffn",
    )(block_expert, n_valid, xs, w_gate, w_up, w_down)


COMBINE_TM = 256


def _combine_kernel(d1_ref, d2_ref, x1_ref, rec_ref, gf_ref, ys_ref, out_ref, buf1, buf2, sem):
    i = pl.program_id(0)
    base = i * COMBINE_TM

    def row_copy(src_row, buf, t):
        return pltpu.make_async_copy(ys_ref.at[pl.ds(src_row, 1)], buf.at[pl.ds(t, 1)], sem)

    def issue(t, c):
        row_copy(d1_ref[base + t], buf1, t).start()
        row_copy(d2_ref[base + t], buf2, t).start()
        return c
    lax.fori_loop(0, COMBINE_TM, issue, 0, unroll=ROW_DMA_UNROLL)

    def drain(t, c):
        row_copy(0, buf1, 0).wait()
        row_copy(0, buf2, 0).wait()
        return c
    lax.fori_loop(0, COMBINE_TM, drain, 0, unroll=ROW_DMA_UNROLL)

    rec = rec_ref[...]
    g1 = rec[:, R_G1:R_G1 + 1]
    g2 = rec[:, R_G2:R_G2 + 1]
    x = x1_ref[...] + g1 * buf1[...] + g2 * buf2[...]
    ms = jnp.mean(x * x, axis=-1, keepdims=True)
    out_ref[...] = x * lax.rsqrt(ms + EPS) * gf_ref[...]


def combine(x1, rec, norm_f_g, ys, dest1, dest2):
    n = x1.shape[0]
    tm = COMBINE_TM
    return pl.pallas_call(
        _combine_kernel,
        grid_spec=pltpu.PrefetchScalarGridSpec(
            num_scalar_prefetch=2,
            grid=(n // tm,),
            in_specs=[pl.BlockSpec((tm, D_MODEL), lambda i, d1, d2: (i, 0)),
                      pl.BlockSpec((tm, LANES), lambda i, d1, d2: (i, 0)),
                      pl.BlockSpec((1, D_MODEL), lambda i, d1, d2: (0, 0)),
                      pl.BlockSpec(memory_space=pl.ANY)],
            out_specs=pl.BlockSpec((tm, D_MODEL), lambda i, d1, d2: (i, 0)),
            scratch_shapes=[pltpu.VMEM((tm, D_MODEL), F32),
                            pltpu.VMEM((tm, D_MODEL), F32),
                            pltpu.SemaphoreType.DMA(())],
        ),
        out_shape=jax.ShapeDtypeStruct((n, D_MODEL), F32),
        compiler_params=_cparams("arbitrary"),
        name="combine",
    )(dest1, dest2, x1, rec, norm_f_g.reshape(1, D_MODEL), ys)


def _router_weights(w_group, b_group, w_router, b_router):
    w = jnp.zeros((D_MODEL, LANES), F32)
    w = w.at[:, :N_GROUPS].set(w_group).at[:, EXPERT_LANE0:EXPERT_LANE0 + N_EXPERTS].set(w_router)
    b = jnp.zeros((1, LANES), F32)
    b = b.at[0, :N_GROUPS].set(b_group).at[0, EXPERT_LANE0:EXPERT_LANE0 + N_EXPERTS].set(b_router)
    hi = w.astype(BF16)
    lo = (w - hi.astype(F32)).astype(BF16)
    return hi, lo, b


def _block_plan(ends_row, n_rows):
    seg_ends = ends_row[0, EXPERT_LANE0:EXPERT_LANE0 + N_EXPERTS].astype(I32)
    n_blocks = n_rows // FFN_BLOCK
    blk_start = jnp.arange(n_blocks, dtype=I32) * FFN_BLOCK
    block_expert = jnp.minimum(jnp.sum((seg_ends[None, :] <= blk_start[:, None]).astype(I32), axis=1),
                               N_EXPERTS - 1)
    n_valid = seg_ends[-1:] // FFN_BLOCK
    return seg_ends, block_expert, n_valid


def kernel(x, norm1_g, w_in, attn_norm_g, hgrn_gamma, hgrn_norm_g, w_out, norm2_g, w_group, b_group,
           w_router, b_router, w_gate, w_up, w_down, norm_f_g):
    batch, seq, d = x.shape
    assert seq == SEQ and d == D_MODEL and norm1_g.shape[0] == 1
    n = batch * seq
    x2 = x.reshape(n, d)
    qkv, hg = in_proj(x2, norm1_g[0], w_in[0].astype(BF16), batch)
    oa = attention(qkv)
    yh = hgrn(hg.reshape(batch, seq, HG_W), hgrn_gamma, hgrn_norm_g[0]).reshape(n, WIDTH)
    wr_hi, wr_lo, br = _router_weights(w_group[0], b_group[0], w_router[0], b_router[0])
    x1, h2, logits = post_mix(oa, yh, x2, attn_norm_g[0], w_out[0].astype(BF16), norm2_g[0],
                              wr_hi, wr_lo, br)
    rec, counts = route(logits)
    dest, ends_row = dest_rows(rec, counts)
    n_rows = 2 * n + N_EXPERTS * FFN_BLOCK
    seg_ends, block_expert, n_valid = _block_plan(ends_row, n_rows)
    dest1, dest2 = dest[:, 0], dest[:, 1]
    xs = dispatch(h2, dest1, dest2, seg_ends, n_valid, n_rows)
    ys = expert_ffn(xs, block_expert, n_valid, w_gate[0], w_up[0], w_down[0])
    return combine(x1, rec, norm_f_g, ys, dest1, dest2).reshape(batch, seq, d)
```

```python
import jax
import jax.numpy as jnp
from jax import lax
from jax.experimental import pallas as pl
from jax.experimental.pallas import tpu as pltpu

F32 = jnp.float32
BF16 = jnp.bfloat16
I32 = jnp.int32
U32 = jnp.uint32

D_MODEL = 1024
HEAD_DIM = 64
N_HEADS = 8
WIDTH = N_HEADS * HEAD_DIM
QKV_W = 3 * WIDTH
HG_W = 4 * WIDTH
SEQ = 2048
ATTN_BLOCK = 128
DILATIONS = (1, 4, 16)
N_RES = 16
RES_LEN = SEQ // N_RES
TILE_TOKENS = 512
U_PER_TILE = TILE_TOKENS // N_RES
HGRN_CHUNK = 32
HGRN_SUPER = 256
N_GROUPS = 4
EXPERTS_PER_GROUP = 8
N_EXPERTS = 32
D_EXPERT = 512
EXPERT_LANE0 = 32
FFN_BLOCK = 256
PACKED_W = D_MODEL // 2
EPS = 1e-6
NEG = -1e30
LOG2E = 1.4426950408889634
LANES = 128
VMEM_LIMIT = 56 * 1024 * 1024


def _cparams(*sem):
    return pltpu.CompilerParams(dimension_semantics=sem, vmem_limit_bytes=VMEM_LIMIT)


def _pack_halves(x):
    w = x.shape[1] // 2
    hi = pltpu.bitcast(x[:, :w].astype(BF16).astype(F32), U32)
    lo = pltpu.bitcast(x[:, w:].astype(BF16).astype(F32), U32)
    return hi | (lo >> 16)


def _unpack_halves(p):
    return pltpu.bitcast(p & jnp.uint32(0xFFFF0000), F32), pltpu.bitcast(p << 16, F32)


N_LANE_TILES = WIDTH // LANES


def _tile_permutation():
    i = jnp.arange(TILE_TOKENS)
    src = N_RES * (i % U_PER_TILE) + i // U_PER_TILE
    return (src[:, None] == jnp.arange(TILE_TOKENS)[None, :]).astype(BF16)


def _in_proj_kernel(x_ref, g_ref, w_ref, perm_ref, qkv_ref, hg_ref):
    x = x_ref[...]
    ms = jnp.mean(x * x, axis=-1, keepdims=True)
    h = (x * lax.rsqrt(ms + EPS) * g_ref[...]).astype(BF16)
    hp = jnp.dot(perm_ref[...], h, preferred_element_type=F32).astype(BF16)
    for c in range(QKV_W // WIDTH):
        res = jnp.dot(hp, w_ref[:, c * WIDTH:(c + 1) * WIDTH], preferred_element_type=F32)
        for r in range(N_RES):
            for g in range(N_LANE_TILES):
                qkv_ref[r, g, c] = res[r * U_PER_TILE:(r + 1) * U_PER_TILE, g * LANES:(g + 1) * LANES]
    for j in range(HG_W // WIDTH):
        wsl = slice(QKV_W + j * WIDTH, QKV_W + (j + 1) * WIDTH)
        hg_ref[:, j * WIDTH:(j + 1) * WIDTH] = jnp.dot(h, w_ref[:, wsl], preferred_element_type=F32).astype(BF16)


def in_proj(x2, g, w_bf16, batch):
    n = x2.shape[0]
    tiles_per_b = SEQ // TILE_TOKENS
    return pl.pallas_call(
        _in_proj_kernel,
        grid=(n // TILE_TOKENS,),
        in_specs=[
            pl.BlockSpec((TILE_TOKENS, D_MODEL), lambda i: (i, 0)),
            pl.BlockSpec((1, D_MODEL), lambda i: (0, 0)),
            pl.BlockSpec((D_MODEL, QKV_W + HG_W), lambda i: (0, 0)),
            pl.BlockSpec((TILE_TOKENS, TILE_TOKENS), lambda i: (0, 0)),
        ],
        out_specs=[
            pl.BlockSpec((None, N_RES, N_LANE_TILES, 3, U_PER_TILE, LANES),
                         lambda i: (i // tiles_per_b, 0, 0, 0, i % tiles_per_b, 0)),
            pl.BlockSpec((TILE_TOKENS, HG_W), lambda i: (i, 0)),
        ],
        out_shape=[
            jax.ShapeDtypeStruct((batch, N_RES, N_LANE_TILES, 3, RES_LEN, LANES), F32),
            jax.ShapeDtypeStruct((n, HG_W), BF16),
        ],
        compiler_params=_cparams("arbitrary"),
        name="in_proj",
    )(x2, g.reshape(1, D_MODEL), w_bf16, _tile_permutation())


ATTN_GROUP = 4
ATTN_GROUP1 = 3
HEADS_PER_TILE = LANES // HEAD_DIM


def _attn_bias_tables():
    slopes = 2.0 ** (-8.0 * jnp.arange(1, N_HEADS + 1, dtype=F32) / N_HEADS)

    def table(qpos, kpos, dil):
        sd = qpos[:, None] - kpos[None, :]
        b = -slopes[:, None, None] * (sd * dil).astype(F32)[None] * LOG2E
        return jnp.where(((sd >= 0) & (sd <= ATTN_BLOCK))[None], b, NEG)

    q = jnp.arange(ATTN_BLOCK)
    k = jnp.arange(2 * ATTN_BLOCK)
    q1 = 16 * (q % 8) + q // 8 + ATTN_BLOCK
    k1 = 16 * (k % 16) + k // 16
    q4 = 4 * (q % 32) + q // 32 + ATTN_BLOCK
    k4 = 4 * (k % 64) + k // 64
    q16 = q + ATTN_BLOCK
    none = jnp.full((N_HEADS, ATTN_BLOCK, 2 * ATTN_BLOCK), NEG, F32)
    tabs = [jnp.concatenate([table(q1, k1, 1), table(q1, q1, 1)], axis=-1),
            jnp.concatenate([table(q4, k4, 4), table(q4, q4, 4)], axis=-1),
            jnp.concatenate([none, table(q16, q16, 16)], axis=-1)]
    return jnp.stack(tabs, axis=0)


def _attn_blocks(blocks, bias_ref, d_idx, with_prev):
    nb = ATTN_BLOCK
    lane = lax.broadcasted_iota(I32, (nb, LANES), 1)
    head0 = lane < HEAD_DIM
    ksl = slice(0, 2 * nb) if with_prev else slice(2 * nb, 3 * nb)
    bias = jnp.concatenate([bias_ref[d_idx, h, :, ksl] for h in range(HEADS_PER_TILE)], axis=0)
    qs = [q * (HEAD_DIM ** -0.5 * LOG2E) for q, _, _ in blocks]
    qq = [jnp.concatenate([jnp.where(head0, q, 0.0), jnp.where(head0, 0.0, q)], axis=0).astype(BF16)
          for q in qs]
    ks = [k.astype(BF16) for _, k, _ in blocks]
    vs = [v.astype(BF16) for _, _, v in blocks]
    v_ones = [jnp.concatenate([v, jnp.ones_like(v)], axis=-1) for v in vs]
    s = [lax.dot_general(q2, k, (((1,), (1,)), ((), ())), preferred_element_type=F32) + bias
         for q2, k in zip(qq, ks)]
    m = [jnp.max(x, axis=-1, keepdims=True) for x in s]
    p = [jnp.exp2(x - mx).astype(BF16) for x, mx in zip(s, m)]
    acc = [jnp.dot(px, vo, preferred_element_type=F32) for px, vo in zip(p, v_ones)]
    head0_wide = jnp.concatenate([head0, head0], axis=-1)
    outs = []
    for a, mx in zip(acc, m):
        ol = jnp.where(head0_wide, a[:nb], a[nb:])
        l = ol[:, LANES:]
        outs.append((ol[:, :LANES] * (1.0 / l), jnp.where(head0, mx[:nb], mx[nb:]) + jnp.log2(l)))
    return outs


def _attn_kernel(qkv_ref, bias_ref, o_ref, o4_ref, l4_ref, o16_ref, l16_ref, fin_ref):
    grp = ATTN_GROUP

    def body16(j, c):
        rs = [j * grp + a for a in range(grp)]
        res = _attn_blocks([(qkv_ref[r, 0], qkv_ref[r, 1], qkv_ref[r, 2]) for r in rs], bias_ref, 2, False)
        for r, (o, l) in zip(rs, res):
            o16_ref[r] = o
            l16_ref[r] = l
        return c
    lax.fori_loop(0, N_RES // grp, body16, 0)

    def gather4(c, rho, u0, nu):
        return jnp.concatenate([qkv_ref[rho + 4 * a, c, pl.ds(u0, nu), :] for a in range(4)], axis=0)

    def store4(rho, u0, o, l):
        for a in range(4):
            o4_ref[rho + 4 * a, pl.ds(u0, 32), :] = o[a * 32:(a + 1) * 32]
            l4_ref[rho + 4 * a, pl.ds(u0, 32), :] = l[a * 32:(a + 1) * 32]

    res = _attn_blocks([tuple(gather4(c, rho, 0, 32) for c in range(3)) for rho in range(4)],
                       bias_ref, 1, False)
    for rho, (o, l) in enumerate(res):
        store4(rho, 0, o, l)

    def body4(j, c):
        u0 = pl.multiple_of((j + 1) * 32, 32)
        res = _attn_blocks([(gather4(0, rho, u0, 32), gather4(1, rho, u0 - 32, 64),
                             gather4(2, rho, u0 - 32, 64)) for rho in range(4)], bias_ref, 1, True)
        for rho, (o, l) in enumerate(res):
            store4(rho, u0, o, l)
        return c
    lax.fori_loop(0, 3, body4, 0)

    def gather1(ref, u0, nu, *lead):
        return jnp.concatenate([ref[(r,) + lead + (pl.ds(u0, nu), slice(None))] for r in range(N_RES)],
                               axis=0)

    def merge_and_store(u0, o1, l1):
        l4 = gather1(l4_ref, u0, 8)
        l16 = gather1(l16_ref, u0, 8)
        mx = jnp.maximum(jnp.maximum(l1, l4), l16)
        e1 = jnp.exp2(l1 - mx)
        e4 = jnp.exp2(l4 - mx)
        e16 = jnp.exp2(l16 - mx)
        inv = 1.0 / (e1 + e4 + e16)
        o = (e1 * inv) * o1 + (e4 * inv) * gather1(o4_ref, u0, 8) + (e16 * inv) * gather1(o16_ref, u0, 8)
        for r in range(N_RES):
            fin_ref[r, pl.ds(u0, 8), :] = o[r * 8:(r + 1) * 8]

    (o, l), = _attn_blocks([tuple(gather1(qkv_ref, 0, 8, c) for c in range(3))], bias_ref, 0, False)
    merge_and_store(0, o, l)

    n_blocks1 = RES_LEN // 8 - 1

    def body1(j, c):
        u0s = [pl.multiple_of((1 + j * ATTN_GROUP1 + a) * 8, 8) for a in range(ATTN_GROUP1)]
        res = _attn_blocks([(gather1(qkv_ref, u0, 8, 0), gather1(qkv_ref, u0 - 8, 16, 1),
                             gather1(qkv_ref, u0 - 8, 16, 2)) for u0 in u0s], bias_ref, 0, True)
        for u0, (o, l) in zip(u0s, res):
            merge_and_store(u0, o, l)
        return c
    lax.fori_loop(0, n_blocks1 // ATTN_GROUP1, body1, 0)

    def emit(r, c):
        o_ref[r] = fin_ref[r].astype(o_ref.dtype)
        return c
    lax.fori_loop(0, N_RES, emit, 0)


def attention(qkv):
    batch = qkv.shape[0]
    scratch = pltpu.VMEM((N_RES, RES_LEN, LANES), F32)
    return pl.pallas_call(
        _attn_kernel,
        grid=(batch, N_LANE_TILES),
        in_specs=[
            pl.BlockSpec((None, N_RES, None, 3, RES_LEN, LANES), lambda b, g: (b, 0, g, 0, 0, 0)),
            pl.BlockSpec((len(DILATIONS), HEADS_PER_TILE, ATTN_BLOCK, 3 * ATTN_BLOCK),
                         lambda b, g: (0, g, 0, 0)),
        ],
        out_specs=pl.BlockSpec((None, N_RES, None, RES_LEN, LANES), lambda b, g: (b, 0, g, 0, 0)),
        out_shape=jax.ShapeDtypeStruct((batch, N_RES, N_LANE_TILES, RES_LEN, LANES), BF16),
        scratch_shapes=[scratch] * 5,
        compiler_params=_cparams("arbitrary", "arbitrary"),
        name="dilated_attention",
    )(qkv, _attn_bias_tables())


def _split3(x):
    a = x.astype(BF16)
    r = x - a.astype(F32)
    b = r.astype(BF16)
    c = (r - b.astype(F32)).astype(BF16)
    return a, b, c


def _hgrn_kernel(q_ref, f_ref, i_ref, g_ref, gamma_ref, ng_ref, tri_ref, y_ref):
    sup, ch = HGRN_SUPER, HGRN_CHUNK
    n_ch = sup // ch
    gam = gamma_ref[...]
    gmx = jnp.max(gam, axis=0, keepdims=True)
    ge = jnp.exp(gam - gmx)
    lb = ge[0:1] / jnp.sum(ge, axis=0, keepdims=True)
    ng = ng_ref[...]
    tri = tri_ref[...]
    ri = lax.broadcasted_iota(I32, (2 * sup, sup), 0) % sup
    ci = lax.broadcasted_iota(I32, (2 * sup, sup), 1)
    causal_bd2 = (ri // ch == ci // ch) & (ri >= ci)
    head0 = lax.broadcasted_iota(I32, (sup, LANES), 1) < HEAD_DIM
    row_chunk = lax.broadcasted_iota(I32, (sup, LANES), 0) // ch
    same_head = (lax.broadcasted_iota(I32, (LANES, LANES), 0) // HEAD_DIM
                 == lax.broadcasted_iota(I32, (LANES, LANES), 1) // HEAD_DIM)

    def body(sb, state):
        rows = pl.ds(pl.multiple_of(sb * sup, sup), sup)
        fr = f_ref[rows, :].astype(F32)
        t = jnp.exp(-jnp.abs(fr))
        rcp = 1.0 / (1.0 + t)
        pos = fr >= 0
        sig = jnp.where(pos, rcp, t * rcp)
        nsig = jnp.where(pos, t * rcp, rcp)
        logf = jnp.log(lb + (1.0 - lb) * sig)
        key = (1.0 - lb) * nsig
        la, lbb, lc = _split3(logf)
        b = (jnp.dot(tri, la, preferred_element_type=F32) + jnp.dot(tri, lbb, preferred_element_type=F32)
             + jnp.dot(tri, lc, preferred_element_type=F32))
        totals = [b[(c + 1) * ch - 1:(c + 1) * ch, :] for c in range(n_ch)]
        b_last = jnp.concatenate([jnp.broadcast_to(t_, (ch, LANES)) for t_ in totals], axis=0)
        tot_tile = jnp.concatenate(totals + [jnp.zeros((LANES - n_ch, LANES), F32)], axis=0)
        decay_t = jnp.exp(jnp.transpose(tot_tile))
        qv = q_ref[rows, :].astype(F32)
        qs = qv * (1.0 / (1.0 + jnp.exp(-qv)))
        q_in = (qs * jnp.exp(b)).astype(BF16)
        k_in = (key * jnp.exp(-b)).astype(BF16)
        k_end = (key * jnp.exp(b_last - b)).astype(BF16)
        vv = i_ref[rows, :]
        gv = g_ref[rows, :].astype(F32)
        gate = gv * (1.0 / (1.0 + jnp.exp(-gv)))
        zero = jnp.zeros_like(vv)
        q2 = jnp.concatenate([jnp.where(head0, q_in, zero), jnp.where(head0, zero, q_in)], axis=0)
        att = lax.dot_general(q2, k_in, (((1,), (1,)), ((), ())), preferred_element_type=F32)
        att = jnp.where(causal_bd2, att, 0.0).astype(BF16)
        intra = jnp.dot(att, vv, preferred_element_type=F32)
        o_intra = jnp.where(head0, intra[:sup], intra[sup:])
        v_exp = jnp.concatenate([jnp.where(row_chunk == c, vv, zero) for c in range(n_ch)], axis=1)
        ds_all = lax.dot_general(k_end, v_exp, (((0,), (0,)), ((), ())), preferred_element_type=F32)
        s_prev = []
        for c in range(n_ch):
            s_prev.append(state.astype(BF16))
            ds = jnp.where(same_head, ds_all[:, c * LANES:(c + 1) * LANES], 0.0)
            state = decay_t[:, c:c + 1] * state + ds
        q_exp = jnp.concatenate([jnp.where(row_chunk == c, q_in, zero) for c in range(n_ch)], axis=1)
        o_inter = jnp.dot(q_exp, jnp.concatenate(s_prev, axis=0), preferred_element_type=F32)
        o = o_intra + o_inter
        sq = o * o
        ss0 = jnp.sum(jnp.where(head0, sq, 0.0), axis=-1, keepdims=True)
        ss1 = jnp.sum(jnp.where(head0, 0.0, sq), axis=-1, keepdims=True)
        ms = jnp.where(head0, ss0, ss1) * (1.0 / HEAD_DIM)
        y_ref[rows, :] = (o * lax.rsqrt(ms + EPS) * ng * gate).astype(y_ref.dtype)
        return state

    lax.fori_loop(0, SEQ // sup, body, jnp.zeros((LANES, LANES), F32))


def hgrn(hg3, gamma, norm_g):
    batch = hg3.shape[0]
    r = jnp.arange(HGRN_SUPER)
    same = (r[:, None] // HGRN_CHUNK) == (r[None, :] // HGRN_CHUNK)
    tri = (same & (r[:, None] >= r[None, :])).astype(BF16)
    col = lambda which: (lambda b, g: (b, 0, which * N_LANE_TILES + g))
    const2 = lambda b, g: (0, 0)
    return pl.pallas_call(
        _hgrn_kernel,
        grid=(batch, N_LANE_TILES),
        in_specs=[
            pl.BlockSpec((None, SEQ, LANES), col(0)),
            pl.BlockSpec((None, SEQ, LANES), col(1)),
            pl.BlockSpec((None, SEQ, LANES), col(2)),
            pl.BlockSpec((None, SEQ, LANES), col(3)),
            pl.BlockSpec((2, LANES), lambda b, g: (0, g)),
            pl.BlockSpec((1, LANES), lambda b, g: (0, g)),
            pl.BlockSpec((HGRN_SUPER, HGRN_SUPER), const2),
        ],
        out_specs=pl.BlockSpec((None, SEQ, LANES), lambda b, g: (b, 0, g)),
        out_shape=jax.ShapeDtypeStruct((batch, SEQ, WIDTH), BF16),
        compiler_params=_cparams("arbitrary", "arbitrary"),
        name="hgrn2",
    )(hg3, hg3, hg3, hg3, gamma.astype(F32), norm_g.reshape(1, WIDTH).astype(F32), tri)


def _post_mix_kernel(oa_ref, yh_ref, x_ref, ag_ref, wo_ref, g2_ref, wrh_ref, wrl_ref, br_ref, perm_t_ref,
                     x1_ref, h2_ref, lg_ref):
    oa = jnp.concatenate(
        [jnp.concatenate([oa_ref[r, g] for g in range(N_LANE_TILES)], axis=1)
         for r in range(N_RES)], axis=0).astype(F32)
    ms = jnp.mean(oa * oa, axis=-1, keepdims=True)
    ya = (oa * lax.rsqrt(ms + EPS) * ag_ref[...]).astype(BF16)
    ya = jnp.dot(perm_t_ref[...], ya, preferred_element_type=F32).astype(BF16)
    mix = (jnp.dot(ya, wo_ref[:WIDTH, :], preferred_element_type=F32)
           + jnp.dot(yh_ref[...], wo_ref[WIDTH:, :], preferred_element_type=F32))
    x1 = x_ref[...] + mix
    x1_ref[...] = x1
    ms2 = jnp.mean(x1 * x1, axis=-1, keepdims=True)
    h2 = x1 * lax.rsqrt(ms2 + EPS) * g2_ref[...]
    h2_ref[...] = _pack_halves(h2)
    hi = h2.astype(BF16)
    lo = (h2 - hi.astype(F32)).astype(BF16)
    wrh = wrh_ref[...]
    lg_ref[...] = (jnp.dot(hi, wrh, preferred_element_type=F32)
                   + jnp.dot(lo, wrh, preferred_element_type=F32)
                   + jnp.dot(hi, wrl_ref[...], preferred_element_type=F32)
                   + br_ref[...])


def post_mix(oa, yh, x2, attn_g, w_out_bf16, g2, wr_hi, wr_lo, br):
    n = x2.shape[0]
    tiles_per_b = SEQ // TILE_TOKENS
    row = lambda w: pl.BlockSpec((TILE_TOKENS, w), lambda i: (i, 0))
    const = lambda r, c: pl.BlockSpec((r, c), lambda i: (0, 0))
    return pl.pallas_call(
        _post_mix_kernel,
        grid=(n // TILE_TOKENS,),
        in_specs=[pl.BlockSpec((None, N_RES, N_LANE_TILES, U_PER_TILE, LANES),
                               lambda i: (i // tiles_per_b, 0, 0, i % tiles_per_b, 0)),
                  row(WIDTH), row(D_MODEL),
                  const(1, WIDTH), const(2 * WIDTH, D_MODEL), const(1, D_MODEL),
                  const(D_MODEL, LANES), const(D_MODEL, LANES), const(1, LANES),
                  const(TILE_TOKENS, TILE_TOKENS)],
        out_specs=[row(D_MODEL), row(PACKED_W), row(LANES)],
        out_shape=[jax.ShapeDtypeStruct((n, D_MODEL), F32),
                   jax.ShapeDtypeStruct((n, PACKED_W), U32),
                   jax.ShapeDtypeStruct((n, LANES), F32)],
        compiler_params=_cparams("arbitrary"),
        name="post_mix",
    )(oa, yh, x2, attn_g.reshape(1, WIDTH), w_out_bf16, g2.reshape(1, D_MODEL), wr_hi, wr_lo, br,
      _tile_permutation().T)


ROUTE_TM = 512
R_E1, R_E2, R_RANK1, R_RANK2, R_G1, R_G2 = 0, 1, 2, 3, 4, 5


def _route_kernel(lg_ref, tri_ref, rec_ref, cnt_ref, carry_ref):
    i = pl.program_id(0)

    @pl.when(i == 0)
    def _():
        carry_ref[...] = jnp.zeros_like(carry_ref)
        cnt_ref[...] = jnp.zeros_like(cnt_ref)

    lg = lg_ref[...]
    lane = lax.broadcasted_iota(I32, lg.shape, 1)
    lanef = lane.astype(F32)
    big = float(LANES)
    gmask = lane < N_GROUPS
    gl = jnp.where(gmask, lg, NEG)
    gmax = jnp.max(gl, axis=-1, keepdims=True)
    gsel = jnp.min(jnp.where(gmask & (gl == gmax), lanef, big), axis=-1, keepdims=True)
    gsum = jnp.sum(jnp.where(gmask, jnp.exp(gl - gmax), 0.0), axis=-1, keepdims=True)
    w_g = 1.0 / gsum
    lo = EXPERT_LANE0 + EXPERTS_PER_GROUP * gsel
    emask = (lanef >= lo) & (lanef < lo + EXPERTS_PER_GROUP)
    el = jnp.where(emask, lg, NEG)
    v1 = jnp.max(el, axis=-1, keepdims=True)
    i1 = jnp.min(jnp.where(emask & (el == v1), lanef, big), axis=-1, keepdims=True)
    emask2 = emask & (lanef != i1)
    el2 = jnp.where(emask2, lg, NEG)
    v2 = jnp.max(el2, axis=-1, keepdims=True)
    i2 = jnp.min(jnp.where(emask2 & (el2 == v2), lanef, big), axis=-1, keepdims=True)
    e2 = jnp.exp(v2 - v1)
    den = 1.0 / (1.0 + e2)
    g1 = w_g * den
    g2 = w_g * e2 * den
    oh1 = lanef == i1
    oh2 = lanef == i2
    onehot = (oh1 | oh2).astype(BF16)
    before = jnp.dot(tri_ref[...], onehot, preferred_element_type=F32) + carry_ref[...]
    rank1 = jnp.sum(jnp.where(oh1, before, 0.0), axis=-1, keepdims=True)
    rank2 = jnp.sum(jnp.where(oh2, before, 0.0), axis=-1, keepdims=True)
    carry_ref[...] += jnp.sum(onehot.astype(F32), axis=0, keepdims=True)
    cnt_ref[...] += lax.dot_general(onehot, jnp.ones((lg.shape[0], LANES), BF16),
                                    (((0,), (0,)), ((), ())), preferred_element_type=F32)
    rec = jnp.zeros_like(lg)
    for slot, val in ((R_E1, i1), (R_E2, i2), (R_RANK1, rank1),
                      (R_RANK2, rank2), (R_G1, g1), (R_G2, g2)):
        rec = jnp.where(lane == slot, val, rec)
    rec_ref[...] = rec


def route(logits):
    n = logits.shape[0]
    tm = ROUTE_TM
    r = jnp.arange(tm)
    tri = (r[:, None] > r[None, :]).astype(BF16)
    return pl.pallas_call(
        _route_kernel,
        grid=(n // tm,),
        in_specs=[pl.BlockSpec((tm, LANES), lambda i: (i, 0)),
                  pl.BlockSpec((tm, tm), lambda i: (0, 0))],
        out_specs=[pl.BlockSpec((tm, LANES), lambda i: (i, 0)),
                   pl.BlockSpec((LANES, LANES), lambda i: (0, 0))],
        out_shape=[jax.ShapeDtypeStruct((n, LANES), F32),
                   jax.ShapeDtypeStruct((LANES, LANES), F32)],
        scratch_shapes=[pltpu.VMEM((1, LANES), F32)],
        compiler_params=_cparams("arbitrary"),
        name="route",
    )(logits, tri)


def _dest_kernel(rec_ref, cnt_ref, dest_ref, ends_ref):
    cnt = cnt_ref[...]
    padded = jnp.floor((cnt + (FFN_BLOCK - 1)) * (1.0 / FFN_BLOCK)) * FFN_BLOCK
    row = lax.broadcasted_iota(I32, cnt.shape, 0)
    col = lax.broadcasted_iota(I32, cnt.shape, 1)
    starts = jnp.sum(jnp.where(row < col, padded, 0.0), axis=0, keepdims=True)
    ends_ref[...] = jnp.sum(jnp.where(row <= col, padded, 0.0), axis=0, keepdims=True)
    rec = rec_ref[...]
    lanef = lax.broadcasted_iota(I32, rec.shape, 1).astype(F32)
    lane = lax.broadcasted_iota(I32, rec.shape, 1)
    s1 = jnp.sum(jnp.where(lanef == rec[:, R_E1:R_E1 + 1], starts, 0.0), axis=-1, keepdims=True)
    s2 = jnp.sum(jnp.where(lanef == rec[:, R_E2:R_E2 + 1], starts, 0.0), axis=-1, keepdims=True)
    d1 = s1 + rec[:, R_RANK1:R_RANK1 + 1]
    d2 = s2 + rec[:, R_RANK2:R_RANK2 + 1]
    dest_ref[...] = jnp.where(lane == 0, d1, jnp.where(lane == 1, d2, 0.0)).astype(I32)


def dest_rows(rec, counts):
    n = rec.shape[0]
    tm = ROUTE_TM
    return pl.pallas_call(
        _dest_kernel,
        grid=(n // tm,),
        in_specs=[pl.BlockSpec((tm, LANES), lambda i: (i, 0)),
                  pl.BlockSpec((LANES, LANES), lambda i: (0, 0))],
        out_specs=[pl.BlockSpec((tm, LANES), lambda i: (i, 0)),
                   pl.BlockSpec((1, LANES), lambda i: (0, 0))],
        out_shape=[jax.ShapeDtypeStruct((n, LANES), I32),
                   jax.ShapeDtypeStruct((1, LANES), F32)],
        compiler_params=_cparams("arbitrary"),
        name="dest_rows",
    )(rec, counts)


DISPATCH_TM = 256
ROW_DMA_UNROLL = 8


def _dispatch_kernel(d1_ref, d2_ref, ends_ref, nv_ref, h2_ref, xs_ref, zero_buf, sem, zsem):
    i = pl.program_id(0)
    base = i * DISPATCH_TM
    n_blocks = xs_ref.shape[0] // FFN_BLOCK

    @pl.when(i == 0)
    def _():
        zero_buf[...] = jnp.zeros_like(zero_buf)

        def zero_copy(row0):
            return pltpu.make_async_copy(zero_buf, xs_ref.at[pl.ds(row0, FFN_BLOCK)], zsem)

        def seg_end(e):
            return ends_ref[e], ends_ref[e] > jnp.where(e > 0, ends_ref[jnp.maximum(e - 1, 0)], 0)

        def start_e(e, c):
            end, nonempty = seg_end(e)

            @pl.when(nonempty)
            def _():
                zero_copy(pl.multiple_of(end - FFN_BLOCK, FFN_BLOCK)).start()
            return c
        lax.fori_loop(0, N_EXPERTS, start_e, 0)

        def start_b(blk, c):
            zero_copy(pl.multiple_of(blk * FFN_BLOCK, FFN_BLOCK)).start()
            return c
        lax.fori_loop(nv_ref[0], n_blocks, start_b, 0)

        def wait_e(e, c):
            _, nonempty = seg_end(e)

            @pl.when(nonempty)
            def _():
                zero_copy(0).wait()
            return c
        lax.fori_loop(0, N_EXPERTS, wait_e, 0)

        def wait_b(blk, c):
            zero_copy(0).wait()
            return c
        lax.fori_loop(nv_ref[0], n_blocks, wait_b, 0)

    slot = i % 2

    def row_copy(src, dest, s):
        return pltpu.make_async_copy(h2_ref.at[pl.ds(src, 1)], xs_ref.at[pl.ds(dest, 1)], sem.at[s])

    def issue(t, c):
        row_copy(base + t, d1_ref[base + t], slot).start()
        row_copy(base + t, d2_ref[base + t], slot).start()
        return c
    lax.fori_loop(0, DISPATCH_TM, issue, 0, unroll=ROW_DMA_UNROLL)

    def drain_slot(s):
        def drain(t, c):
            row_copy(0, 0, s).wait()
            row_copy(0, 0, s).wait()
            return c
        lax.fori_loop(0, DISPATCH_TM, drain, 0, unroll=ROW_DMA_UNROLL)

    @pl.when(i > 0)
    def _():
        drain_slot(1 - slot)

    @pl.when(i + 1 == pl.num_programs(0))
    def _():
        drain_slot(slot)


def dispatch(h2, dest1, dest2, seg_ends, n_valid, n_rows):
    n = h2.shape[0]
    return pl.pallas_call(
        _dispatch_kernel,
        grid_spec=pltpu.PrefetchScalarGridSpec(
            num_scalar_prefetch=4,
            grid=(n // DISPATCH_TM,),
            in_specs=[pl.BlockSpec(memory_space=pl.ANY)],
            out_specs=pl.BlockSpec(memory_space=pl.ANY),
            scratch_shapes=[pltpu.VMEM((FFN_BLOCK, PACKED_W), U32),
                            pltpu.SemaphoreType.DMA((2,)),
                            pltpu.SemaphoreType.DMA(())],
        ),
        out_shape=jax.ShapeDtypeStruct((n_rows, PACKED_W), U32),
        compiler_params=_cparams("arbitrary"),
        name="dispatch",
    )(dest1, dest2, seg_ends, n_valid, h2)


def _ffn_kernel(be_ref, nv_ref, xs_ref, wg_ref, wu_ref, wd_ref, ys_ref, wg_sc, wu_sc, wd_sc):
    i = pl.program_id(0)
    prev = be_ref[jnp.maximum(i - 1, 0)]
    fresh = (i == 0) | (be_ref[i] != prev)

    @pl.when(fresh)
    def _():
        wg_sc[...] = wg_ref[0].astype(BF16)
        wu_sc[...] = wu_ref[0].astype(BF16)
        wd_sc[...] = wd_ref[0].astype(BF16)

    @pl.when(i < nv_ref[0])
    def _():
        xa, xb = (t.astype(BF16) for t in _unpack_halves(xs_ref[...]))
        half = PACKED_W
        gate = (jnp.dot(xa, wg_sc[:half, :], preferred_element_type=F32)
                + jnp.dot(xb, wg_sc[half:, :], preferred_element_type=F32))
        up = (jnp.dot(xa, wu_sc[:half, :], preferred_element_type=F32)
              + jnp.dot(xb, wu_sc[half:, :], preferred_element_type=F32))
        hid = (gate * (1.0 / (1.0 + jnp.exp(-gate))) * up).astype(BF16)
        ys_ref[...] = _pack_halves(jnp.dot(hid, wd_sc[...], preferred_element_type=F32))

    @pl.when(i >= nv_ref[0])
    def _():
        ys_ref[...] = jnp.zeros_like(ys_ref)


def expert_ffn(xs, block_expert, n_valid, w_gate, w_up, w_down):
    n_rows = xs.shape[0]
    n_blocks = n_rows // FFN_BLOCK
    wmap = lambda i, be, nv: (be[i], 0, 0)
    return pl.pallas_call(
        _ffn_kernel,
        grid_spec=pltpu.PrefetchScalarGridSpec(
            num_scalar_prefetch=2,
            grid=(n_blocks,),
            in_specs=[pl.BlockSpec((FFN_BLOCK, PACKED_W), lambda i, be, nv: (jnp.minimum(i, nv[0] - 1), 0)),
                      pl.BlockSpec((1, D_MODEL, D_EXPERT), wmap),
                      pl.BlockSpec((1, D_MODEL, D_EXPERT), wmap),
                      pl.BlockSpec((1, D_EXPERT, D_MODEL), wmap)],
            out_specs=pl.BlockSpec((FFN_BLOCK, PACKED_W), lambda i, be, nv: (i, 0)),
            scratch_shapes=[pltpu.VMEM((D_MODEL, D_EXPERT), BF16),
                            pltpu.VMEM((D_MODEL, D_EXPERT), BF16),
                            pltpu.VMEM((D_EXPERT, D_MODEL), BF16)],
        ),
        out_shape=jax.ShapeDtypeStruct((n_rows, PACKED_W), U32),
        compiler_params=_cparams("arbitrary"),
        name="expert_ffn",
    )(block_expert, n_valid, xs, w_gate, w_up, w_down)


COMBINE_TM = 256


def _combine_kernel(d1_ref, d2_ref, x1_ref, rec_ref, gf_ref, ys_ref, out_ref, buf1, buf2, sem):
    i = pl.program_id(0)
    slot = i % 2

    def row_copy(src_row, buf, s, t):
        return pltpu.make_async_copy(ys_ref.at[pl.ds(src_row, 1)], buf.at[s, pl.ds(t, 1)], sem.at[s])

    def gather_tile(tile, s):
        base = tile * COMBINE_TM

        def issue(t, c):
            row_copy(d1_ref[base + t], buf1, s, t).start()
            row_copy(d2_ref[base + t], buf2, s, t).start()
            return c
        lax.fori_loop(0, COMBINE_TM, issue, 0, unroll=ROW_DMA_UNROLL)

    @pl.when(i == 0)
    def _():
        gather_tile(0, 0)

    @pl.when(i + 1 < pl.num_programs(0))
    def _():
        gather_tile(i + 1, 1 - slot)

    def drain(t, c):
        row_copy(0, buf1, slot, 0).wait()
        row_copy(0, buf2, slot, 0).wait()
        return c
    lax.fori_loop(0, COMBINE_TM, drain, 0, unroll=ROW_DMA_UNROLL)

    rec = rec_ref[...]
    g1 = rec[:, R_G1:R_G1 + 1]
    g2 = rec[:, R_G2:R_G2 + 1]
    y1a, y1b = _unpack_halves(buf1[slot])
    y2a, y2b = _unpack_halves(buf2[slot])
    half = PACKED_W
    xa = x1_ref[:, :half] + g1 * y1a + g2 * y2a
    xb = x1_ref[:, half:] + g1 * y1b + g2 * y2b
    ms = (jnp.sum(xa * xa, axis=-1, keepdims=True) + jnp.sum(xb * xb, axis=-1, keepdims=True)) * (1.0 / D_MODEL)
    scale = lax.rsqrt(ms + EPS)
    out_ref[:, :half] = xa * scale * gf_ref[:, :half]
    out_ref[:, half:] = xb * scale * gf_ref[:, half:]


def combine(x1, rec, norm_f_g, ys, dest1, dest2):
    n = x1.shape[0]
    tm = COMBINE_TM
    return pl.pallas_call(
        _combine_kernel,
        grid_spec=pltpu.PrefetchScalarGridSpec(
            num_scalar_prefetch=2,
            grid=(n // tm,),
            in_specs=[pl.BlockSpec((tm, D_MODEL), lambda i, d1, d2: (i, 0)),
                      pl.BlockSpec((tm, LANES), lambda i, d1, d2: (i, 0)),
                      pl.BlockSpec((1, D_MODEL), lambda i, d1, d2: (0, 0)),
                      pl.BlockSpec(memory_space=pl.ANY)],
            out_specs=pl.BlockSpec((tm, D_MODEL), lambda i, d1, d2: (i, 0)),
            scratch_shapes=[pltpu.VMEM((2, tm, PACKED_W), U32),
                            pltpu.VMEM((2, tm, PACKED_W), U32),
                            pltpu.SemaphoreType.DMA((2,))],
        ),
        out_shape=jax.ShapeDtypeStruct((n, D_MODEL), F32),
        compiler_params=_cparams("arbitrary"),
        name="combine",
    )(dest1, dest2, x1, rec, norm_f_g.reshape(1, D_MODEL), ys)


def _router_weights(w_group, b_group, w_router, b_router):
    w = jnp.zeros((D_MODEL, LANES), F32)
    w = w.at[:, :N_GROUPS].set(w_group).at[:, EXPERT_LANE0:EXPERT_LANE0 + N_EXPERTS].set(w_router)
    b = jnp.zeros((1, LANES), F32)
    b = b.at[0, :N_GROUPS].set(b_group).at[0, EXPERT_LANE0:EXPERT_LANE0 + N_EXPERTS].set(b_router)
    hi = w.astype(BF16)
    lo = (w - hi.astype(F32)).astype(BF16)
    return hi, lo, b


def _block_plan(ends_row, n_rows):
    seg_ends = ends_row[0, EXPERT_LANE0:EXPERT_LANE0 + N_EXPERTS].astype(I32)
    n_blocks = n_rows // FFN_BLOCK
    blk_start = jnp.arange(n_blocks, dtype=I32) * FFN_BLOCK
    block_expert = jnp.minimum(jnp.sum((seg_ends[None, :] <= blk_start[:, None]).astype(I32), axis=1),
                               N_EXPERTS - 1)
    n_valid = seg_ends[-1:] // FFN_BLOCK
    return seg_ends, block_expert, n_valid


def kernel(x, norm1_g, w_in, attn_norm_g, hgrn_gamma, hgrn_norm_g, w_out, norm2_g, w_group, b_group,
           w_router, b_router, w_gate, w_up, w_down, norm_f_g):
    batch, seq, d = x.shape
    assert seq == SEQ and d == D_MODEL and norm1_g.shape[0] == 1
    n = batch * seq
    x2 = x.reshape(n, d)
    qkv, hg = in_proj(x2, norm1_g[0], w_in[0].astype(BF16), batch)
    oa = attention(qkv)
    yh = hgrn(hg.reshape(batch, seq, HG_W), hgrn_gamma, hgrn_norm_g[0]).reshape(n, WIDTH)
    wr_hi, wr_lo, br = _router_weights(w_group[0], b_group[0], w_router[0], b_router[0])
    x1, h2, logits = post_mix(oa, yh, x2, attn_norm_g[0], w_out[0].astype(BF16), norm2_g[0],
                              wr_hi, wr_lo, br)
    rec, counts = route(logits)
    dest, ends_row = dest_rows(rec, counts)
    n_rows = 2 * n + N_EXPERTS * FFN_BLOCK
    seg_ends, block_expert, n_valid = _block_plan(ends_row, n_rows)
    dest1, dest2 = dest[:, 0], dest[:, 1]
    xs = dispatch(h2, dest1, dest2, seg_ends, n_valid, n_rows)
    ys = expert_ffn(xs, block_expert, n_valid, w_gate[0], w_up[0], w_down[0])
    return combine(x1, rec, norm_f_g, ys, dest1, dest2).reshape(batch, seq, d)
```

```python
import jax
import jax.numpy as jnp
from jax import lax
from jax.experimental import pallas as pl
from jax.experimental.pallas import tpu as pltpu

F32 = jnp.float32
BF16 = jnp.bfloat16
I32 = jnp.int32
U32 = jnp.uint32

D_MODEL = 1024
HEAD_DIM = 64
N_HEADS = 8
WIDTH = N_HEADS * HEAD_DIM
QKV_W = 3 * WIDTH
HG_W = 4 * WIDTH
SEQ = 2048
ATTN_BLOCK = 128
DILATIONS = (1, 4, 16)
N_RES = 16
RES_LEN = SEQ // N_RES
TILE_TOKENS = 512
U_PER_TILE = TILE_TOKENS // N_RES
HGRN_CHUNK = 32
HGRN_SUPER = 256
N_GROUPS = 4
EXPERTS_PER_GROUP = 8
N_EXPERTS = 32
D_EXPERT = 512
EXPERT_LANE0 = 32
FFN_BLOCK = 256
PACKED_W = D_MODEL // 2
EPS = 1e-6
NEG = -1e30
LOG2E = 1.4426950408889634
LANES = 128
VMEM_LIMIT = 56 * 1024 * 1024


def _cparams(*sem):
    return pltpu.CompilerParams(dimension_semantics=sem, vmem_limit_bytes=VMEM_LIMIT)


def _pack_halves(x):
    w = x.shape[1] // 2
    hi = pltpu.bitcast(x[:, :w].astype(BF16).astype(F32), U32)
    lo = pltpu.bitcast(x[:, w:].astype(BF16).astype(F32), U32)
    return hi | (lo >> 16)


def _unpack_halves(p):
    return pltpu.bitcast(p & jnp.uint32(0xFFFF0000), F32), pltpu.bitcast(p << 16, F32)


N_LANE_TILES = WIDTH // LANES


def _tile_permutation():
    i = jnp.arange(TILE_TOKENS)
    src = N_RES * (i % U_PER_TILE) + i // U_PER_TILE
    return (src[:, None] == jnp.arange(TILE_TOKENS)[None, :]).astype(BF16)


def _in_proj_kernel(x_ref, g_ref, w_ref, perm_ref, qkv_ref, hg_ref):
    x = x_ref[...]
    ms = jnp.mean(x * x, axis=-1, keepdims=True)
    h = (x * lax.rsqrt(ms + EPS) * g_ref[...]).astype(BF16)
    hp = jnp.dot(perm_ref[...], h, preferred_element_type=F32).astype(BF16)
    for c in range(QKV_W // WIDTH):
        res = jnp.dot(hp, w_ref[:, c * WIDTH:(c + 1) * WIDTH], preferred_element_type=F32)
        for r in range(N_RES):
            for g in range(N_LANE_TILES):
                qkv_ref[r, g, c] = res[r * U_PER_TILE:(r + 1) * U_PER_TILE, g * LANES:(g + 1) * LANES]
    for j in range(HG_W // WIDTH):
        wsl = slice(QKV_W + j * WIDTH, QKV_W + (j + 1) * WIDTH)
        hg_ref[:, j * WIDTH:(j + 1) * WIDTH] = jnp.dot(h, w_ref[:, wsl], preferred_element_type=F32).astype(BF16)


def in_proj(x2, g, w_bf16, batch):
    n = x2.shape[0]
    tiles_per_b = SEQ // TILE_TOKENS
    return pl.pallas_call(
        _in_proj_kernel,
        grid=(n // TILE_TOKENS,),
        in_specs=[
            pl.BlockSpec((TILE_TOKENS, D_MODEL), lambda i: (i, 0)),
            pl.BlockSpec((1, D_MODEL), lambda i: (0, 0)),
            pl.BlockSpec((D_MODEL, QKV_W + HG_W), lambda i: (0, 0)),
            pl.BlockSpec((TILE_TOKENS, TILE_TOKENS), lambda i: (0, 0)),
        ],
        out_specs=[
            pl.BlockSpec((None, N_RES, N_LANE_TILES, 3, U_PER_TILE, LANES),
                         lambda i: (i // tiles_per_b, 0, 0, 0, i % tiles_per_b, 0)),
            pl.BlockSpec((TILE_TOKENS, HG_W), lambda i: (i, 0)),
        ],
        out_shape=[
            jax.ShapeDtypeStruct((batch, N_RES, N_LANE_TILES, 3, RES_LEN, LANES), F32),
            jax.ShapeDtypeStruct((n, HG_W), BF16),
        ],
        compiler_params=_cparams("arbitrary"),
        name="in_proj",
    )(x2, g.reshape(1, D_MODEL), w_bf16, _tile_permutation())


ATTN_GROUP = 4
ATTN_GROUP1 = 3
HEADS_PER_TILE = LANES // HEAD_DIM


def _attn_bias_tables():
    slopes = 2.0 ** (-8.0 * jnp.arange(1, N_HEADS + 1, dtype=F32) / N_HEADS)

    def table(qpos, kpos, dil):
        sd = qpos[:, None] - kpos[None, :]
        b = -slopes[:, None, None] * (sd * dil).astype(F32)[None] * LOG2E
        return jnp.where(((sd >= 0) & (sd <= ATTN_BLOCK))[None], b, NEG)

    q = jnp.arange(ATTN_BLOCK)
    k = jnp.arange(2 * ATTN_BLOCK)
    q1 = 16 * (q % 8) + q // 8 + ATTN_BLOCK
    k1 = 16 * (k % 16) + k // 16
    q4 = 4 * (q % 32) + q // 32 + ATTN_BLOCK
    k4 = 4 * (k % 64) + k // 64
    q16 = q + ATTN_BLOCK
    none = jnp.full((N_HEADS, ATTN_BLOCK, 2 * ATTN_BLOCK), NEG, F32)
    tabs = [jnp.concatenate([table(q1, k1, 1), table(q1, q1, 1)], axis=-1),
            jnp.concatenate([table(q4, k4, 4), table(q4, q4, 4)], axis=-1),
            jnp.concatenate([none, table(q16, q16, 16)], axis=-1)]
    return jnp.stack(tabs, axis=0)


def _attn_blocks(blocks, bias_ref, d_idx, with_prev):
    nb = ATTN_BLOCK
    lane = lax.broadcasted_iota(I32, (nb, LANES), 1)
    head0 = lane < HEAD_DIM
    ksl = slice(0, 2 * nb) if with_prev else slice(2 * nb, 3 * nb)
    bias = jnp.concatenate([bias_ref[d_idx, h, :, ksl] for h in range(HEADS_PER_TILE)], axis=0)
    qs = [q * (HEAD_DIM ** -0.5 * LOG2E) for q, _, _ in blocks]
    qq = [jnp.concatenate([jnp.where(head0, q, 0.0), jnp.where(head0, 0.0, q)], axis=0).astype(BF16)
          for q in qs]
    ks = [k.astype(BF16) for _, k, _ in blocks]
    vs = [v.astype(BF16) for _, _, v in blocks]
    v_ones = [jnp.concatenate([v, jnp.ones_like(v)], axis=-1) for v in vs]
    s = [lax.dot_general(q2, k, (((1,), (1,)), ((), ())), preferred_element_type=F32) + bias
         for q2, k in zip(qq, ks)]
    m = [jnp.max(x, axis=-1, keepdims=True) for x in s]
    p = [jnp.exp2(x - mx).astype(BF16) for x, mx in zip(s, m)]
    acc = [jnp.dot(px, vo, preferred_element_type=F32) for px, vo in zip(p, v_ones)]
    head0_wide = jnp.concatenate([head0, head0], axis=-1)
    outs = []
    for a, mx in zip(acc, m):
        ol = jnp.where(head0_wide, a[:nb], a[nb:])
        l = ol[:, LANES:]
        outs.append((ol[:, :LANES] * (1.0 / l), jnp.where(head0, mx[:nb], mx[nb:]) + jnp.log2(l)))
    return outs


def _attn_kernel(qkv_ref, bias_ref, o_ref, o4_ref, l4_ref, o16_ref, l16_ref, fin_ref):
    grp = ATTN_GROUP

    def body16(j, c):
        rs = [j * grp + a for a in range(grp)]
        res = _attn_blocks([(qkv_ref[r, 0], qkv_ref[r, 1], qkv_ref[r, 2]) for r in rs], bias_ref, 2, False)
        for r, (o, l) in zip(rs, res):
            o16_ref[r] = o
            l16_ref[r] = l
        return c
    lax.fori_loop(0, N_RES // grp, body16, 0)

    def gather4(c, rho, u0, nu):
        return jnp.concatenate([qkv_ref[rho + 4 * a, c, pl.ds(u0, nu), :] for a in range(4)], axis=0)

    def store4(rho, u0, o, l):
        for a in range(4):
            o4_ref[rho + 4 * a, pl.ds(u0, 32), :] = o[a * 32:(a + 1) * 32]
            l4_ref[rho + 4 * a, pl.ds(u0, 32), :] = l[a * 32:(a + 1) * 32]

    res = _attn_blocks([tuple(gather4(c, rho, 0, 32) for c in range(3)) for rho in range(4)],
                       bias_ref, 1, False)
    for rho, (o, l) in enumerate(res):
        store4(rho, 0, o, l)

    def body4(j, c):
        u0 = pl.multiple_of((j + 1) * 32, 32)
        res = _attn_blocks([(gather4(0, rho, u0, 32), gather4(1, rho, u0 - 32, 64),
                             gather4(2, rho, u0 - 32, 64)) for rho in range(4)], bias_ref, 1, True)
        for rho, (o, l) in enumerate(res):
            store4(rho, u0, o, l)
        return c
    lax.fori_loop(0, 3, body4, 0)

    def gather1(ref, u0, nu, *lead):
        return jnp.concatenate([ref[(r,) + lead + (pl.ds(u0, nu), slice(None))] for r in range(N_RES)],
                               axis=0)

    def merge_and_store(u0, o1, l1):
        l4 = gather1(l4_ref, u0, 8)
        l16 = gather1(l16_ref, u0, 8)
        mx = jnp.maximum(jnp.maximum(l1, l4), l16)
        e1 = jnp.exp2(l1 - mx)
        e4 = jnp.exp2(l4 - mx)
        e16 = jnp.exp2(l16 - mx)
        inv = 1.0 / (e1 + e4 + e16)
        o = (e1 * inv) * o1 + (e4 * inv) * gather1(o4_ref, u0, 8) + (e16 * inv) * gather1(o16_ref, u0, 8)
        for r in range(N_RES):
            fin_ref[r, pl.ds(u0, 8), :] = o[r * 8:(r + 1) * 8]

    (o, l), = _attn_blocks([tuple(gather1(qkv_ref, 0, 8, c) for c in range(3))], bias_ref, 0, False)
    merge_and_store(0, o, l)

    n_blocks1 = RES_LEN // 8 - 1

    def body1(j, c):
        u0s = [pl.multiple_of((1 + j * ATTN_GROUP1 + a) * 8, 8) for a in range(ATTN_GROUP1)]
        res = _attn_blocks([(gather1(qkv_ref, u0, 8, 0), gather1(qkv_ref, u0 - 8, 16, 1),
                             gather1(qkv_ref, u0 - 8, 16, 2)) for u0 in u0s], bias_ref, 0, True)
        for u0, (o, l) in zip(u0s, res):
            merge_and_store(u0, o, l)
        return c
    lax.fori_loop(0, n_blocks1 // ATTN_GROUP1, body1, 0)

    def emit(r, c):
        o_ref[r] = fin_ref[r].astype(o_ref.dtype)
        return c
    lax.fori_loop(0, N_RES, emit, 0)


def attention(qkv):
    batch = qkv.shape[0]
    scratch = pltpu.VMEM((N_RES, RES_LEN, LANES), F32)
    return pl.pallas_call(
        _attn_kernel,
        grid=(batch, N_LANE_TILES),
        in_specs=[
            pl.BlockSpec((None, N_RES, None, 3, RES_LEN, LANES), lambda b, g: (b, 0, g, 0, 0, 0)),
            pl.BlockSpec((len(DILATIONS), HEADS_PER_TILE, ATTN_BLOCK, 3 * ATTN_BLOCK),
                         lambda b, g: (0, g, 0, 0)),
        ],
        out_specs=pl.BlockSpec((None, N_RES, None, RES_LEN, LANES), lambda b, g: (b, 0, g, 0, 0)),
        out_shape=jax.ShapeDtypeStruct((batch, N_RES, N_LANE_TILES, RES_LEN, LANES), BF16),
        scratch_shapes=[scratch] * 5,
        compiler_params=_cparams("arbitrary", "arbitrary"),
        name="dilated_attention",
    )(qkv, _attn_bias_tables())


def _split3(x):
    a = x.astype(BF16)
    r = x - a.astype(F32)
    b = r.astype(BF16)
    c = (r - b.astype(F32)).astype(BF16)
    return a, b, c


def _hgrn_kernel(q_ref, f_ref, i_ref, g_ref, gamma_ref, ng_ref, tri_ref, y_ref):
    sup, ch = HGRN_SUPER, HGRN_CHUNK
    n_ch = sup // ch
    gam = gamma_ref[...]
    gmx = jnp.max(gam, axis=0, keepdims=True)
    ge = jnp.exp(gam - gmx)
    lb = ge[0:1] / jnp.sum(ge, axis=0, keepdims=True)
    ng = ng_ref[...]
    tri = tri_ref[...]
    ri = lax.broadcasted_iota(I32, (2 * sup, sup), 0) % sup
    ci = lax.broadcasted_iota(I32, (2 * sup, sup), 1)
    causal_bd2 = (ri // ch == ci // ch) & (ri >= ci)
    head0 = lax.broadcasted_iota(I32, (sup, LANES), 1) < HEAD_DIM
    row_chunk = lax.broadcasted_iota(I32, (sup, LANES), 0) // ch
    same_head = (lax.broadcasted_iota(I32, (LANES, LANES), 0) // HEAD_DIM
                 == lax.broadcasted_iota(I32, (LANES, LANES), 1) // HEAD_DIM)

    def body(sb, state):
        rows = pl.ds(pl.multiple_of(sb * sup, sup), sup)
        fr = f_ref[rows, :].astype(F32)
        t = jnp.exp(-jnp.abs(fr))
        rcp = 1.0 / (1.0 + t)
        pos = fr >= 0
        sig = jnp.where(pos, rcp, t * rcp)
        nsig = jnp.where(pos, t * rcp, rcp)
        logf = jnp.log(lb + (1.0 - lb) * sig)
        key = (1.0 - lb) * nsig
        la, lbb, lc = _split3(logf)
        b = (jnp.dot(tri, la, preferred_element_type=F32) + jnp.dot(tri, lbb, preferred_element_type=F32)
             + jnp.dot(tri, lc, preferred_element_type=F32))
        totals = [b[(c + 1) * ch - 1:(c + 1) * ch, :] for c in range(n_ch)]
        b_last = jnp.concatenate([jnp.broadcast_to(t_, (ch, LANES)) for t_ in totals], axis=0)
        tot_tile = jnp.concatenate(totals + [jnp.zeros((LANES - n_ch, LANES), F32)], axis=0)
        decay_t = jnp.exp(jnp.transpose(tot_tile))
        qv = q_ref[rows, :].astype(F32)
        qs = qv * (1.0 / (1.0 + jnp.exp(-qv)))
        q_in = (qs * jnp.exp(b)).astype(BF16)
        k_in = (key * jnp.exp(-b)).astype(BF16)
        k_end = (key * jnp.exp(b_last - b)).astype(BF16)
        vv = i_ref[rows, :]
        gv = g_ref[rows, :].astype(F32)
        gate = gv * (1.0 / (1.0 + jnp.exp(-gv)))
        zero = jnp.zeros_like(vv)
        q2 = jnp.concatenate([jnp.where(head0, q_in, zero), jnp.where(head0, zero, q_in)], axis=0)
        att = lax.dot_general(q2, k_in, (((1,), (1,)), ((), ())), preferred_element_type=F32)
        att = jnp.where(causal_bd2, att, 0.0).astype(BF16)
        intra = jnp.dot(att, vv, preferred_element_type=F32)
        o_intra = jnp.where(head0, intra[:sup], intra[sup:])
        v_exp = jnp.concatenate([jnp.where(row_chunk == c, vv, zero) for c in range(n_ch)], axis=1)
        ds_all = lax.dot_general(k_end, v_exp, (((0,), (0,)), ((), ())), preferred_element_type=F32)
        s_prev = []
        for c in range(n_ch):
            s_prev.append(state.astype(BF16))
            ds = jnp.where(same_head, ds_all[:, c * LANES:(c + 1) * LANES], 0.0)
            state = decay_t[:, c:c + 1] * state + ds
        q_exp = jnp.concatenate([jnp.where(row_chunk == c, q_in, zero) for c in range(n_ch)], axis=1)
        o_inter = jnp.dot(q_exp, jnp.concatenate(s_prev, axis=0), preferred_element_type=F32)
        o = o_intra + o_inter
        sq = o * o
        ss0 = jnp.sum(jnp.where(head0, sq, 0.0), axis=-1, keepdims=True)
        ss1 = jnp.sum(jnp.where(head0, 0.0, sq), axis=-1, keepdims=True)
        ms = jnp.where(head0, ss0, ss1) * (1.0 / HEAD_DIM)
        y_ref[rows, :] = (o * lax.rsqrt(ms + EPS) * ng * gate).astype(y_ref.dtype)
        return state

    lax.fori_loop(0, SEQ // sup, body, jnp.zeros((LANES, LANES), F32))


def hgrn(hg3, gamma, norm_g):
    batch = hg3.shape[0]
    r = jnp.arange(HGRN_SUPER)
    same = (r[:, None] // HGRN_CHUNK) == (r[None, :] // HGRN_CHUNK)
    tri = (same & (r[:, None] >= r[None, :])).astype(BF16)
    col = lambda which: (lambda b, g: (b, 0, which * N_LANE_TILES + g))
    const2 = lambda b, g: (0, 0)
    return pl.pallas_call(
        _hgrn_kernel,
        grid=(batch, N_LANE_TILES),
        in_specs=[
            pl.BlockSpec((None, SEQ, LANES), col(0)),
            pl.BlockSpec((None, SEQ, LANES), col(1)),
            pl.BlockSpec((None, SEQ, LANES), col(2)),
            pl.BlockSpec((None, SEQ, LANES), col(3)),
            pl.BlockSpec((2, LANES), lambda b, g: (0, g)),
            pl.BlockSpec((1, LANES), lambda b, g: (0, g)),
            pl.BlockSpec((HGRN_SUPER, HGRN_SUPER), const2),
        ],
        out_specs=pl.BlockSpec((None, SEQ, LANES), lambda b, g: (b, 0, g)),
        out_shape=jax.ShapeDtypeStruct((batch, SEQ, WIDTH), BF16),
        compiler_params=_cparams("arbitrary", "arbitrary"),
        name="hgrn2",
    )(hg3, hg3, hg3, hg3, gamma.astype(F32), norm_g.reshape(1, WIDTH).astype(F32), tri)


def _post_mix_kernel(oa_ref, yh_ref, x_ref, ag_ref, wo_ref, g2_ref, wrh_ref, wrl_ref, br_ref, perm_t_ref,
                     x1_ref, h2_ref, lg_ref):
    oa = jnp.concatenate(
        [jnp.concatenate([oa_ref[r, g] for g in range(N_LANE_TILES)], axis=1)
         for r in range(N_RES)], axis=0).astype(F32)
    ms = jnp.mean(oa * oa, axis=-1, keepdims=True)
    ya = (oa * lax.rsqrt(ms + EPS) * ag_ref[...]).astype(BF16)
    ya = jnp.dot(perm_t_ref[...], ya, preferred_element_type=F32).astype(BF16)
    mix = (jnp.dot(ya, wo_ref[:WIDTH, :], preferred_element_type=F32)
           + jnp.dot(yh_ref[...], wo_ref[WIDTH:, :], preferred_element_type=F32))
    x1 = x_ref[...] + mix
    x1_ref[...] = x1
    ms2 = jnp.mean(x1 * x1, axis=-1, keepdims=True)
    h2 = x1 * lax.rsqrt(ms2 + EPS) * g2_ref[...]
    h2_ref[...] = _pack_halves(h2)
    hi = h2.astype(BF16)
    lo = (h2 - hi.astype(F32)).astype(BF16)
    wrh = wrh_ref[...]
    lg_ref[...] = (jnp.dot(hi, wrh, preferred_element_type=F32)
                   + jnp.dot(lo, wrh, preferred_element_type=F32)
                   + jnp.dot(hi, wrl_ref[...], preferred_element_type=F32)
                   + br_ref[...])


def post_mix(oa, yh, x2, attn_g, w_out_bf16, g2, wr_hi, wr_lo, br):
    n = x2.shape[0]
    tiles_per_b = SEQ // TILE_TOKENS
    row = lambda w: pl.BlockSpec((TILE_TOKENS, w), lambda i: (i, 0))
    const = lambda r, c: pl.BlockSpec((r, c), lambda i: (0, 0))
    return pl.pallas_call(
        _post_mix_kernel,
        grid=(n // TILE_TOKENS,),
        in_specs=[pl.BlockSpec((None, N_RES, N_LANE_TILES, U_PER_TILE, LANES),
                               lambda i: (i // tiles_per_b, 0, 0, i % tiles_per_b, 0)),
                  row(WIDTH), row(D_MODEL),
                  const(1, WIDTH), const(2 * WIDTH, D_MODEL), const(1, D_MODEL),
                  const(D_MODEL, LANES), const(D_MODEL, LANES), const(1, LANES),
                  const(TILE_TOKENS, TILE_TOKENS)],
        out_specs=[row(D_MODEL), row(PACKED_W), row(LANES)],
        out_shape=[jax.ShapeDtypeStruct((n, D_MODEL), F32),
                   jax.ShapeDtypeStruct((n, PACKED_W), U32),
                   jax.ShapeDtypeStruct((n, LANES), F32)],
        compiler_params=_cparams("arbitrary"),
        name="post_mix",
    )(oa, yh, x2, attn_g.reshape(1, WIDTH), w_out_bf16, g2.reshape(1, D_MODEL), wr_hi, wr_lo, br,
      _tile_permutation().T)


ROUTE_TM = 512
R_E1, R_E2, R_RANK1, R_RANK2, R_G1, R_G2 = 0, 1, 2, 3, 4, 5


def _route_kernel(lg_ref, tri_ref, rec_ref, cnt_ref, carry_ref):
    i = pl.program_id(0)

    @pl.when(i == 0)
    def _():
        carry_ref[...] = jnp.zeros_like(carry_ref)
        cnt_ref[...] = jnp.zeros_like(cnt_ref)

    lg = lg_ref[...]
    lane = lax.broadcasted_iota(I32, lg.shape, 1)
    lanef = lane.astype(F32)
    big = float(LANES)
    gmask = lane < N_GROUPS
    gl = jnp.where(gmask, lg, NEG)
    gmax = jnp.max(gl, axis=-1, keepdims=True)
    gsel = jnp.min(jnp.where(gmask & (gl == gmax), lanef, big), axis=-1, keepdims=True)
    gsum = jnp.sum(jnp.where(gmask, jnp.exp(gl - gmax), 0.0), axis=-1, keepdims=True)
    w_g = 1.0 / gsum
    lo = EXPERT_LANE0 + EXPERTS_PER_GROUP * gsel
    emask = (lanef >= lo) & (lanef < lo + EXPERTS_PER_GROUP)
    el = jnp.where(emask, lg, NEG)
    v1 = jnp.max(el, axis=-1, keepdims=True)
    i1 = jnp.min(jnp.where(emask & (el == v1), lanef, big), axis=-1, keepdims=True)
    emask2 = emask & (lanef != i1)
    el2 = jnp.where(emask2, lg, NEG)
    v2 = jnp.max(el2, axis=-1, keepdims=True)
    i2 = jnp.min(jnp.where(emask2 & (el2 == v2), lanef, big), axis=-1, keepdims=True)
    e2 = jnp.exp(v2 - v1)
    den = 1.0 / (1.0 + e2)
    g1 = w_g * den
    g2 = w_g * e2 * den
    oh1 = lanef == i1
    oh2 = lanef == i2
    onehot = (oh1 | oh2).astype(BF16)
    before = jnp.dot(tri_ref[...], onehot, preferred_element_type=F32) + carry_ref[...]
    rank1 = jnp.sum(jnp.where(oh1, before, 0.0), axis=-1, keepdims=True)
    rank2 = jnp.sum(jnp.where(oh2, before, 0.0), axis=-1, keepdims=True)
    carry_ref[...] += jnp.sum(onehot.astype(F32), axis=0, keepdims=True)
    cnt_ref[...] += lax.dot_general(onehot, jnp.ones((lg.shape[0], LANES), BF16),
                                    (((0,), (0,)), ((), ())), preferred_element_type=F32)
    rec = jnp.zeros_like(lg)
    for slot, val in ((R_E1, i1), (R_E2, i2), (R_RANK1, rank1),
                      (R_RANK2, rank2), (R_G1, g1), (R_G2, g2)):
        rec = jnp.where(lane == slot, val, rec)
    rec_ref[...] = rec


def route(logits):
    n = logits.shape[0]
    tm = ROUTE_TM
    r = jnp.arange(tm)
    tri = (r[:, None] > r[None, :]).astype(BF16)
    return pl.pallas_call(
        _route_kernel,
        grid=(n // tm,),
        in_specs=[pl.BlockSpec((tm, LANES), lambda i: (i, 0)),
                  pl.BlockSpec((tm, tm), lambda i: (0, 0))],
        out_specs=[pl.BlockSpec((tm, LANES), lambda i: (i, 0)),
                   pl.BlockSpec((LANES, LANES), lambda i: (0, 0))],
        out_shape=[jax.ShapeDtypeStruct((n, LANES), F32),
                   jax.ShapeDtypeStruct((LANES, LANES), F32)],
        scratch_shapes=[pltpu.VMEM((1, LANES), F32)],
        compiler_params=_cparams("arbitrary"),
        name="route",
    )(logits, tri)


def _dest_kernel(rec_ref, cnt_ref, dest_ref, ends_ref):
    cnt = cnt_ref[...]
    padded = jnp.floor((cnt + (FFN_BLOCK - 1)) * (1.0 / FFN_BLOCK)) * FFN_BLOCK
    row = lax.broadcasted_iota(I32, cnt.shape, 0)
    col = lax.broadcasted_iota(I32, cnt.shape, 1)
    starts = jnp.sum(jnp.where(row < col, padded, 0.0), axis=0, keepdims=True)
    ends_ref[...] = jnp.sum(jnp.where(row <= col, padded, 0.0), axis=0, keepdims=True)
    rec = rec_ref[...]
    lanef = lax.broadcasted_iota(I32, rec.shape, 1).astype(F32)
    lane = lax.broadcasted_iota(I32, rec.shape, 1)
    s1 = jnp.sum(jnp.where(lanef == rec[:, R_E1:R_E1 + 1], starts, 0.0), axis=-1, keepdims=True)
    s2 = jnp.sum(jnp.where(lanef == rec[:, R_E2:R_E2 + 1], starts, 0.0), axis=-1, keepdims=True)
    d1 = s1 + rec[:, R_RANK1:R_RANK1 + 1]
    d2 = s2 + rec[:, R_RANK2:R_RANK2 + 1]
    dest_ref[...] = jnp.where(lane == 0, d1, jnp.where(lane == 1, d2, 0.0)).astype(I32)


def dest_rows(rec, counts):
    n = rec.shape[0]
    tm = ROUTE_TM
    return pl.pallas_call(
        _dest_kernel,
        grid=(n // tm,),
        in_specs=[pl.BlockSpec((tm, LANES), lambda i: (i, 0)),
                  pl.BlockSpec((LANES, LANES), lambda i: (0, 0))],
        out_specs=[pl.BlockSpec((tm, LANES), lambda i: (i, 0)),
                   pl.BlockSpec((1, LANES), lambda i: (0, 0))],
        out_shape=[jax.ShapeDtypeStruct((n, LANES), I32),
                   jax.ShapeDtypeStruct((1, LANES), F32)],
        compiler_params=_cparams("arbitrary"),
        name="dest_rows",
    )(rec, counts)


DISPATCH_TM = 256
ROW_DMA_UNROLL = 8


def _dispatch_kernel(d1_ref, d2_ref, ends_ref, nv_ref, h2_ref, xs_ref, zero_buf, stage, sem, zsem, ld_sem):
    i = pl.program_id(0)
    base = i * DISPATCH_TM
    n_blocks = xs_ref.shape[0] // FFN_BLOCK

    @pl.when(i == 0)
    def _():
        zero_buf[...] = jnp.zeros_like(zero_buf)

        def zero_copy(row0):
            return pltpu.make_async_copy(zero_buf, xs_ref.at[pl.ds(row0, FFN_BLOCK)], zsem)

        def seg_end(e):
            return ends_ref[e], ends_ref[e] > jnp.where(e > 0, ends_ref[jnp.maximum(e - 1, 0)], 0)

        def start_e(e, c):
            end, nonempty = seg_end(e)

            @pl.when(nonempty)
            def _():
                zero_copy(pl.multiple_of(end - FFN_BLOCK, FFN_BLOCK)).start()
            return c
        lax.fori_loop(0, N_EXPERTS, start_e, 0)

        def start_b(blk, c):
            zero_copy(pl.multiple_of(blk * FFN_BLOCK, FFN_BLOCK)).start()
            return c
        lax.fori_loop(nv_ref[0], n_blocks, start_b, 0)

        def wait_e(e, c):
            _, nonempty = seg_end(e)

            @pl.when(nonempty)
            def _():
                zero_copy(0).wait()
            return c
        lax.fori_loop(0, N_EXPERTS, wait_e, 0)

        def wait_b(blk, c):
            zero_copy(0).wait()
            return c
        lax.fori_loop(nv_ref[0], n_blocks, wait_b, 0)

    n_steps = pl.num_programs(0)
    slot = i % 2
    stage_slot = i % 3

    def tile_load(tile, s):
        return pltpu.make_async_copy(h2_ref.at[pl.ds(tile * DISPATCH_TM, DISPATCH_TM)], stage.at[s],
                                     ld_sem.at[s])

    @pl.when(i == 0)
    def _():
        tile_load(0, 0).start()

    @pl.when(i + 1 < n_steps)
    def _():
        tile_load(i + 1, (i + 1) % 3).start()

    tile_load(i, stage_slot).wait()

    def row_copy(t, dest, s):
        return pltpu.make_async_copy(stage.at[stage_slot, pl.ds(t, 1)], xs_ref.at[pl.ds(dest, 1)], sem.at[s])

    def issue(t, c):
        row_copy(t, d1_ref[base + t], slot).start()
        row_copy(t, d2_ref[base + t], slot).start()
        return c
    lax.fori_loop(0, DISPATCH_TM, issue, 0, unroll=ROW_DMA_UNROLL)

    def drain_slot(s):
        def drain(t, c):
            row_copy(0, 0, s).wait()
            row_copy(0, 0, s).wait()
            return c
        lax.fori_loop(0, DISPATCH_TM, drain, 0, unroll=ROW_DMA_UNROLL)

    @pl.when(i > 0)
    def _():
        drain_slot(1 - slot)

    @pl.when(i + 1 == pl.num_programs(0))
    def _():
        drain_slot(slot)


def dispatch(h2, dest1, dest2, seg_ends, n_valid, n_rows):
    n = h2.shape[0]
    return pl.pallas_call(
        _dispatch_kernel,
        grid_spec=pltpu.PrefetchScalarGridSpec(
            num_scalar_prefetch=4,
            grid=(n // DISPATCH_TM,),
            in_specs=[pl.BlockSpec(memory_space=pl.ANY)],
            out_specs=pl.BlockSpec(memory_space=pl.ANY),
            scratch_shapes=[pltpu.VMEM((FFN_BLOCK, PACKED_W), U32),
                            pltpu.VMEM((3, DISPATCH_TM, PACKED_W), U32),
                            pltpu.SemaphoreType.DMA((2,)),
                            pltpu.SemaphoreType.DMA(()),
                            pltpu.SemaphoreType.DMA((3,))],
        ),
        out_shape=jax.ShapeDtypeStruct((n_rows, PACKED_W), U32),
        compiler_params=_cparams("arbitrary"),
        name="dispatch",
    )(dest1, dest2, seg_ends, n_valid, h2)


def _ffn_kernel(be_ref, nv_ref, xs_ref, wg_ref, wu_ref, wd_ref, ys_ref, wg_sc, wu_sc, wd_sc):
    i = pl.program_id(0)
    prev = be_ref[jnp.maximum(i - 1, 0)]
    fresh = (i == 0) | (be_ref[i] != prev)

    @pl.when(fresh)
    def _():
        wg_sc[...] = wg_ref[0].astype(BF16)
        wu_sc[...] = wu_ref[0].astype(BF16)
        wd_sc[...] = wd_ref[0].astype(BF16)

    @pl.when(i < nv_ref[0])
    def _():
        xa, xb = (t.astype(BF16) for t in _unpack_halves(xs_ref[...]))
        half = PACKED_W
        gate = (jnp.dot(xa, wg_sc[:half, :], preferred_element_type=F32)
                + jnp.dot(xb, wg_sc[half:, :], preferred_element_type=F32))
        up = (jnp.dot(xa, wu_sc[:half, :], preferred_element_type=F32)
              + jnp.dot(xb, wu_sc[half:, :], preferred_element_type=F32))
        hid = (gate * (1.0 / (1.0 + jnp.exp(-gate))) * up).astype(BF16)
        ys_ref[...] = _pack_halves(jnp.dot(hid, wd_sc[...], preferred_element_type=F32))

    @pl.when(i >= nv_ref[0])
    def _():
        ys_ref[...] = jnp.zeros_like(ys_ref)


def expert_ffn(xs, block_expert, n_valid, w_gate, w_up, w_down):
    n_rows = xs.shape[0]
    n_blocks = n_rows // FFN_BLOCK
    wmap = lambda i, be, nv: (be[i], 0, 0)
    return pl.pallas_call(
        _ffn_kernel,
        grid_spec=pltpu.PrefetchScalarGridSpec(
            num_scalar_prefetch=2,
            grid=(n_blocks,),
            in_specs=[pl.BlockSpec((FFN_BLOCK, PACKED_W), lambda i, be, nv: (jnp.minimum(i, nv[0] - 1), 0)),
                      pl.BlockSpec((1, D_MODEL, D_EXPERT), wmap),
                      pl.BlockSpec((1, D_MODEL, D_EXPERT), wmap),
                      pl.BlockSpec((1, D_EXPERT, D_MODEL), wmap)],
            out_specs=pl.BlockSpec((FFN_BLOCK, PACKED_W), lambda i, be, nv: (i, 0)),
            scratch_shapes=[pltpu.VMEM((D_MODEL, D_EXPERT), BF16),
                            pltpu.VMEM((D_MODEL, D_EXPERT), BF16),
                            pltpu.VMEM((D_EXPERT, D_MODEL), BF16)],
        ),
        out_shape=jax.ShapeDtypeStruct((n_rows, PACKED_W), U32),
        compiler_params=_cparams("arbitrary"),
        name="expert_ffn",
    )(block_expert, n_valid, xs, w_gate, w_up, w_down)


COMBINE_TM = 256


def _combine_kernel(d1_ref, d2_ref, x1_ref, rec_ref, gf_ref, ys_ref, out_ref, buf1, buf2, sem):
    i = pl.program_id(0)
    slot = i % 2

    def row_copy(src_row, buf, s, t):
        return pltpu.make_async_copy(ys_ref.at[pl.ds(src_row, 1)], buf.at[s, pl.ds(t, 1)], sem.at[s])

    def gather_tile(tile, s):
        base = tile * COMBINE_TM

        def issue(t, c):
            row_copy(d1_ref[base + t], buf1, s, t).start()
            row_copy(d2_ref[base + t], buf2, s, t).start()
            return c
        lax.fori_loop(0, COMBINE_TM, issue, 0, unroll=ROW_DMA_UNROLL)

    @pl.when(i == 0)
    def _():
        gather_tile(0, 0)

    @pl.when(i + 1 < pl.num_programs(0))
    def _():
        gather_tile(i + 1, 1 - slot)

    def drain(t, c):
        row_copy(0, buf1, slot, 0).wait()
        row_copy(0, buf2, slot, 0).wait()
        return c
    lax.fori_loop(0, COMBINE_TM, drain, 0, unroll=ROW_DMA_UNROLL)

    rec = rec_ref[...]
    g1 = rec[:, R_G1:R_G1 + 1]
    g2 = rec[:, R_G2:R_G2 + 1]
    y1a, y1b = _unpack_halves(buf1[slot])
    y2a, y2b = _unpack_halves(buf2[slot])
    half = PACKED_W
    xa = x1_ref[:, :half] + g1 * y1a + g2 * y2a
    xb = x1_ref[:, half:] + g1 * y1b + g2 * y2b
    ms = (jnp.sum(xa * xa, axis=-1, keepdims=True) + jnp.sum(xb * xb, axis=-1, keepdims=True)) * (1.0 / D_MODEL)
    scale = lax.rsqrt(ms + EPS)
    out_ref[:, :half] = xa * scale * gf_ref[:, :half]
    out_ref[:, half:] = xb * scale * gf_ref[:, half:]


def combine(x1, rec, norm_f_g, ys, dest1, dest2):
    n = x1.shape[0]
    tm = COMBINE_TM
    return pl.pallas_call(
        _combine_kernel,
        grid_spec=pltpu.PrefetchScalarGridSpec(
            num_scalar_prefetch=2,
            grid=(n // tm,),
            in_specs=[pl.BlockSpec((tm, D_MODEL), lambda i, d1, d2: (i, 0)),
                      pl.BlockSpec((tm, LANES), lambda i, d1, d2: (i, 0)),
                      pl.BlockSpec((1, D_MODEL), lambda i, d1, d2: (0, 0)),
                      pl.BlockSpec(memory_space=pl.ANY)],
            out_specs=pl.BlockSpec((tm, D_MODEL), lambda i, d1, d2: (i, 0)),
            scratch_shapes=[pltpu.VMEM((2, tm, PACKED_W), U32),
                            pltpu.VMEM((2, tm, PACKED_W), U32),
                            pltpu.SemaphoreType.DMA((2,))],
        ),
        out_shape=jax.ShapeDtypeStruct((n, D_MODEL), F32),
        compiler_params=_cparams("arbitrary"),
        name="combine",
    )(dest1, dest2, x1, rec, norm_f_g.reshape(1, D_MODEL), ys)


def _router_weights(w_group, b_group, w_router, b_router):
    w = jnp.zeros((D_MODEL, LANES), F32)
    w = w.at[:, :N_GROUPS].set(w_group).at[:, EXPERT_LANE0:EXPERT_LANE0 + N_EXPERTS].set(w_router)
    b = jnp.zeros((1, LANES), F32)
    b = b.at[0, :N_GROUPS].set(b_group).at[0, EXPERT_LANE0:EXPERT_LANE0 + N_EXPERTS].set(b_router)
    hi = w.astype(BF16)
    lo = (w - hi.astype(F32)).astype(BF16)
    return hi, lo, b


def _block_plan(ends_row, n_rows):
    seg_ends = ends_row[0, EXPERT_LANE0:EXPERT_LANE0 + N_EXPERTS].astype(I32)
    n_blocks = n_rows // FFN_BLOCK
    blk_start = jnp.arange(n_blocks, dtype=I32) * FFN_BLOCK
    block_expert = jnp.minimum(jnp.sum((seg_ends[None, :] <= blk_start[:, None]).astype(I32), axis=1),
                               N_EXPERTS - 1)
    n_valid = seg_ends[-1:] // FFN_BLOCK
    return seg_ends, block_expert, n_valid


def kernel(x, norm1_g, w_in, attn_norm_g, hgrn_gamma, hgrn_norm_g, w_out, norm2_g, w_group, b_group,
           w_router, b_router, w_gate, w_up, w_down, norm_f_g):
    batch, seq, d = x.shape
    assert seq == SEQ and d == D_MODEL and norm1_g.shape[0] == 1
    n = batch * seq
    x2 = x.reshape(n, d)
    qkv, hg = in_proj(x2, norm1_g[0], w_in[0].astype(BF16), batch)
    oa = attention(qkv)
    yh = hgrn(hg.reshape(batch, seq, HG_W), hgrn_gamma, hgrn_norm_g[0]).reshape(n, WIDTH)
    wr_hi, wr_lo, br = _router_weights(w_group[0], b_group[0], w_router[0], b_router[0])
    x1, h2, logits = post_mix(oa, yh, x2, attn_norm_g[0], w_out[0].astype(BF16), norm2_g[0],
                              wr_hi, wr_lo, br)
    rec, counts = route(logits)
    dest, ends_row = dest_rows(rec, counts)
    n_rows = 2 * n + N_EXPERTS * FFN_BLOCK
    seg_ends, block_expert, n_valid = _block_plan(ends_row, n_rows)
    dest1, dest2 = dest[:, 0], dest[:, 1]
    xs = dispatch(h2, dest1, dest2, seg_ends, n_valid, n_rows)
    ys = expert_ffn(xs, block_expert, n_valid, w_gate[0], w_up[0], w_down[0])
    return combine(x1, rec, norm_f_g, ys, dest1, dest2).reshape(batch, seq, d)
```

```python
import jax
import jax.numpy as jnp
from jax import lax
from jax.experimental import pallas as pl
from jax.experimental.pallas import tpu as pltpu

F32 = jnp.float32
BF16 = jnp.bfloat16
I32 = jnp.int32
U32 = jnp.uint32

D_MODEL = 1024
HEAD_DIM = 64
N_HEADS = 8
WIDTH = N_HEADS * HEAD_DIM
QKV_W = 3 * WIDTH
HG_W = 4 * WIDTH
SEQ = 2048
ATTN_BLOCK = 128
DILATIONS = (1, 4, 16)
N_RES = 16
RES_LEN = SEQ // N_RES
TILE_TOKENS = 512
U_PER_TILE = TILE_TOKENS // N_RES
HGRN_CHUNK = 32
HGRN_SUPER = 256
HGRN_GROUP = 4
N_GROUPS = 4
EXPERTS_PER_GROUP = 8
N_EXPERTS = 32
D_EXPERT = 512
EXPERT_LANE0 = 32
FFN_BLOCK = 256
PACKED_W = D_MODEL // 2
EPS = 1e-6
NEG = -1e30
LOG2E = 1.4426950408889634
LANES = 128
VMEM_LIMIT = 56 * 1024 * 1024


def _cparams(*sem):
    return pltpu.CompilerParams(dimension_semantics=sem, vmem_limit_bytes=VMEM_LIMIT)


def _pack_halves(x):
    w = x.shape[1] // 2
    hi = pltpu.bitcast(x[:, :w].astype(BF16).astype(F32), U32)
    lo = pltpu.bitcast(x[:, w:].astype(BF16).astype(F32), U32)
    return hi | (lo >> 16)


def _unpack_halves(p):
    return pltpu.bitcast(p & jnp.uint32(0xFFFF0000), F32), pltpu.bitcast(p << 16, F32)


N_LANE_TILES = WIDTH // LANES


def _tile_permutation():
    i = jnp.arange(TILE_TOKENS)
    src = N_RES * (i % U_PER_TILE) + i // U_PER_TILE
    return (src[:, None] == jnp.arange(TILE_TOKENS)[None, :]).astype(BF16)


def _in_proj_kernel(x_ref, g_ref, w_ref, perm_ref, qkv_ref, hg_ref):
    x = x_ref[...]
    ms = jnp.mean(x * x, axis=-1, keepdims=True)
    h = (x * lax.rsqrt(ms + EPS) * g_ref[...]).astype(BF16)
    hp = jnp.dot(perm_ref[...], h, preferred_element_type=F32).astype(BF16)
    for c in range(QKV_W // WIDTH):
        res = jnp.dot(hp, w_ref[:, c * WIDTH:(c + 1) * WIDTH], preferred_element_type=F32)
        for r in range(N_RES):
            for g in range(N_LANE_TILES):
                qkv_ref[r, g, c] = res[r * U_PER_TILE:(r + 1) * U_PER_TILE, g * LANES:(g + 1) * LANES]
    for j in range(HG_W // WIDTH):
        wsl = slice(QKV_W + j * WIDTH, QKV_W + (j + 1) * WIDTH)
        hg_ref[:, j * WIDTH:(j + 1) * WIDTH] = jnp.dot(h, w_ref[:, wsl], preferred_element_type=F32).astype(BF16)


def in_proj(x2, g, w_bf16, batch):
    n = x2.shape[0]
    tiles_per_b = SEQ // TILE_TOKENS
    return pl.pallas_call(
        _in_proj_kernel,
        grid=(n // TILE_TOKENS,),
        in_specs=[
            pl.BlockSpec((TILE_TOKENS, D_MODEL), lambda i: (i, 0)),
            pl.BlockSpec((1, D_MODEL), lambda i: (0, 0)),
            pl.BlockSpec((D_MODEL, QKV_W + HG_W), lambda i: (0, 0)),
            pl.BlockSpec((TILE_TOKENS, TILE_TOKENS), lambda i: (0, 0)),
        ],
        out_specs=[
            pl.BlockSpec((None, N_RES, N_LANE_TILES, 3, U_PER_TILE, LANES),
                         lambda i: (i // tiles_per_b, 0, 0, 0, i % tiles_per_b, 0)),
            pl.BlockSpec((TILE_TOKENS, HG_W), lambda i: (i, 0)),
        ],
        out_shape=[
            jax.ShapeDtypeStruct((batch, N_RES, N_LANE_TILES, 3, RES_LEN, LANES), F32),
            jax.ShapeDtypeStruct((n, HG_W), BF16),
        ],
        compiler_params=_cparams("arbitrary"),
        name="in_proj",
    )(x2, g.reshape(1, D_MODEL), w_bf16, _tile_permutation())


ATTN_GROUP16 = 8
ATTN_GROUP1 = 5
HEADS_PER_TILE = LANES // HEAD_DIM


def _attn_bias_tables():
    slopes = 2.0 ** (-8.0 * jnp.arange(1, N_HEADS + 1, dtype=F32) / N_HEADS)

    def table(qpos, kpos, dil):
        sd = qpos[:, None] - kpos[None, :]
        b = -slopes[:, None, None] * (sd * dil).astype(F32)[None] * LOG2E
        return jnp.where(((sd >= 0) & (sd <= ATTN_BLOCK))[None], b, NEG)

    q = jnp.arange(ATTN_BLOCK)
    k = jnp.arange(2 * ATTN_BLOCK)
    q1 = 16 * (q % 8) + q // 8 + ATTN_BLOCK
    k1 = 16 * (k % 16) + k // 16
    q4 = 4 * (q % 32) + q // 32 + ATTN_BLOCK
    k4 = 4 * (k % 64) + k // 64
    q16 = q + ATTN_BLOCK
    none = jnp.full((N_HEADS, ATTN_BLOCK, 2 * ATTN_BLOCK), NEG, F32)
    tabs = [jnp.concatenate([table(q1, k1, 1), table(q1, q1, 1)], axis=-1),
            jnp.concatenate([table(q4, k4, 4), table(q4, q4, 4)], axis=-1),
            jnp.concatenate([none, table(q16, q16, 16)], axis=-1)]
    return jnp.stack(tabs, axis=0)


def _attn_bias(bias_ref, d_idx, with_prev):
    ksl = slice(0, 2 * ATTN_BLOCK) if with_prev else slice(2 * ATTN_BLOCK, 3 * ATTN_BLOCK)
    return jnp.concatenate([bias_ref[d_idx, h, :, ksl] for h in range(HEADS_PER_TILE)], axis=0)


def _attn_blocks(blocks):
    nb = ATTN_BLOCK
    lane = lax.broadcasted_iota(I32, (nb, LANES), 1)
    head0 = lane < HEAD_DIM
    biases = [blk[3] for blk in blocks]
    blocks = [blk[:3] for blk in blocks]
    qs = [q * (HEAD_DIM ** -0.5 * LOG2E) for q, _, _ in blocks]
    qq = [jnp.concatenate([jnp.where(head0, q, 0.0), jnp.where(head0, 0.0, q)], axis=0).astype(BF16)
          for q in qs]
    ks = [k.astype(BF16) for _, k, _ in blocks]
    vs = [v.astype(BF16) for _, _, v in blocks]
    v_ones = [jnp.concatenate([v, jnp.ones_like(v)], axis=-1) for v in vs]
    s = [lax.dot_general(q2, k, (((1,), (1,)), ((), ())), preferred_element_type=F32) + bias
         for q2, k, bias in zip(qq, ks, biases)]
    m = [jnp.max(x, axis=-1, keepdims=True) for x in s]
    p = [jnp.exp2(x - mx).astype(BF16) for x, mx in zip(s, m)]
    acc = [jnp.dot(px, vo, preferred_element_type=F32) for px, vo in zip(p, v_ones)]
    head0_wide = jnp.concatenate([head0, head0], axis=-1)
    outs = []
    for a, mx in zip(acc, m):
        ol = jnp.where(head0_wide, a[:nb], a[nb:])
        l = ol[:, LANES:]
        outs.append((ol[:, :LANES] * (1.0 / l), jnp.where(head0, mx[:nb], mx[nb:]) + jnp.log2(l)))
    return outs


def _attn_kernel(qkv_ref, bias_ref, o_ref, o4_ref, l4_ref, o16_ref, l16_ref, fin_ref):
    grp = ATTN_GROUP16

    def body16(j, c):
        rs = [j * grp + a for a in range(grp)]
        bias = _attn_bias(bias_ref, 2, False)
        res = _attn_blocks([(qkv_ref[r, 0], qkv_ref[r, 1], qkv_ref[r, 2], bias) for r in rs])
        for r, (o, l) in zip(rs, res):
            o16_ref[r] = o
            l16_ref[r] = l
        return c
    lax.fori_loop(0, N_RES // grp, body16, 0)

    def gather4(c, rho, u0, nu):
        return jnp.concatenate([qkv_ref[rho + 4 * a, c, pl.ds(u0, nu), :] for a in range(4)], axis=0)

    def store4(rho, u0, o, l):
        for a in range(4):
            o4_ref[rho + 4 * a, pl.ds(u0, 32), :] = o[a * 32:(a + 1) * 32]
            l4_ref[rho + 4 * a, pl.ds(u0, 32), :] = l[a * 32:(a + 1) * 32]

    def body4(j, c):
        bias = _attn_bias(bias_ref, 1, True)
        todo = [(2 * j + a, 32 * n) for a in range(2) for n in range(1, 4)]
        res = _attn_blocks([(gather4(0, rho, u0, 32), gather4(1, rho, u0 - 32, 64),
                             gather4(2, rho, u0 - 32, 64), bias) for rho, u0 in todo])
        for (rho, u0), (o, l) in zip(todo, res):
            store4(rho, u0, o, l)
        return c
    lax.fori_loop(0, 2, body4, 0)

    def gather1(ref, u0, nu, *lead):
        return jnp.concatenate([ref[(r,) + lead + (pl.ds(u0, nu), slice(None))] for r in range(N_RES)],
                               axis=0)

    def merge_and_store(u0, o1, l1):
        l4 = gather1(l4_ref, u0, 8)
        l16 = gather1(l16_ref, u0, 8)
        mx = jnp.maximum(jnp.maximum(l1, l4), l16)
        e1 = jnp.exp2(l1 - mx)
        e4 = jnp.exp2(l4 - mx)
        e16 = jnp.exp2(l16 - mx)
        inv = 1.0 / (e1 + e4 + e16)
        o = (e1 * inv) * o1 + (e4 * inv) * gather1(o4_ref, u0, 8) + (e16 * inv) * gather1(o16_ref, u0, 8)
        for r in range(N_RES):
            fin_ref[r, pl.ds(u0, 8), :] = o[r * 8:(r + 1) * 8]

    bias4 = _attn_bias(bias_ref, 1, False)
    first = [tuple(gather4(c, rho, 0, 32) for c in range(3)) + (bias4,) for rho in range(4)]
    first.append(tuple(gather1(qkv_ref, 0, 8, c) for c in range(3)) + (_attn_bias(bias_ref, 0, False),))
    res = _attn_blocks(first)
    for rho in range(4):
        store4(rho, 0, *res[rho])
    merge_and_store(0, *res[4])

    n_blocks1 = RES_LEN // 8 - 1

    def body1(j, c):
        u0s = [pl.multiple_of((1 + j * ATTN_GROUP1 + a) * 8, 8) for a in range(ATTN_GROUP1)]
        bias = _attn_bias(bias_ref, 0, True)
        res = _attn_blocks([(gather1(qkv_ref, u0, 8, 0), gather1(qkv_ref, u0 - 8, 16, 1),
                             gather1(qkv_ref, u0 - 8, 16, 2), bias) for u0 in u0s])
        for u0, (o, l) in zip(u0s, res):
            merge_and_store(u0, o, l)
        return c
    lax.fori_loop(0, n_blocks1 // ATTN_GROUP1, body1, 0)

    def emit(r, c):
        o_ref[r] = fin_ref[r].astype(o_ref.dtype)
        return c
    lax.fori_loop(0, N_RES, emit, 0)


def attention(qkv):
    batch = qkv.shape[0]
    scratch = pltpu.VMEM((N_RES, RES_LEN, LANES), F32)
    return pl.pallas_call(
        _attn_kernel,
        grid=(batch, N_LANE_TILES),
        in_specs=[
            pl.BlockSpec((None, N_RES, None, 3, RES_LEN, LANES), lambda b, g: (b, 0, g, 0, 0, 0)),
            pl.BlockSpec((len(DILATIONS), HEADS_PER_TILE, ATTN_BLOCK, 3 * ATTN_BLOCK),
                         lambda b, g: (0, g, 0, 0)),
        ],
        out_specs=pl.BlockSpec((None, N_RES, None, RES_LEN, LANES), lambda b, g: (b, 0, g, 0, 0)),
        out_shape=jax.ShapeDtypeStruct((batch, N_RES, N_LANE_TILES, RES_LEN, LANES), BF16),
        scratch_shapes=[scratch] * 5,
        compiler_params=_cparams("arbitrary", "arbitrary"),
        name="dilated_attention",
    )(qkv, _attn_bias_tables())


def _split3(x):
    a = x.astype(BF16)
    r = x - a.astype(F32)
    b = r.astype(BF16)
    c = (r - b.astype(F32)).astype(BF16)
    return a, b, c


def _hgrn_kernel(q_ref, f_ref, i_ref, g_ref, gamma_ref, ng_ref, tri_ref, y_ref):
    sup, ch = HGRN_SUPER, HGRN_CHUNK
    n_ch = sup // ch
    gam = gamma_ref[...]
    gmx = jnp.max(gam, axis=0, keepdims=True)
    ge = jnp.exp(gam - gmx)
    lb = ge[0:1] / jnp.sum(ge, axis=0, keepdims=True)
    ng = ng_ref[...]
    tri = tri_ref[...]
    ri = lax.broadcasted_iota(I32, (2 * sup, sup), 0) % sup
    ci = lax.broadcasted_iota(I32, (2 * sup, sup), 1)
    causal_bd2 = (ri // ch == ci // ch) & (ri >= ci)
    head0 = lax.broadcasted_iota(I32, (sup, LANES), 1) < HEAD_DIM
    row_chunk = lax.broadcasted_iota(I32, (sup, LANES), 0) // ch
    same_head = (lax.broadcasted_iota(I32, (LANES, LANES), 0) // HEAD_DIM
                 == lax.broadcasted_iota(I32, (LANES, LANES), 1) // HEAD_DIM)

    def sigmoid(x):
        return 1.0 / (1.0 + jnp.exp(-x))

    def dot_nt(a, b):
        return lax.dot_general(a, b, (((1,), (1,)), ((), ())), preferred_element_type=F32)

    def dot_tn(a, b):
        return lax.dot_general(a, b, (((0,), (0,)), ((), ())), preferred_element_type=F32)

    def expand(x):
        zero = jnp.zeros_like(x)
        return jnp.concatenate([jnp.where(row_chunk == c, x, zero) for c in range(n_ch)], axis=1)

    def body(j, state):
        grp = range(HGRN_GROUP)
        rows = [pl.ds(pl.multiple_of((j * HGRN_GROUP + a) * sup, sup), sup) for a in grp]
        fr = [f_ref[r, :].astype(F32) for r in rows]
        t = [jnp.exp(-jnp.abs(x)) for x in fr]
        rcp = [1.0 / (1.0 + x) for x in t]
        sig = [jnp.where(f >= 0, r, x * r) for f, r, x in zip(fr, rcp, t)]
        nsig = [jnp.where(f >= 0, x * r, r) for f, r, x in zip(fr, rcp, t)]
        logf = [jnp.log(lb + (1.0 - lb) * x) for x in sig]
        key = [(1.0 - lb) * x for x in nsig]
        parts = [_split3(x) for x in logf]
        b = [sum(jnp.dot(tri, p, preferred_element_type=F32) for p in ps) for ps in parts]
        totals = [[x[(c + 1) * ch - 1:(c + 1) * ch, :] for c in range(n_ch)] for x in b]
        b_last = [jnp.concatenate([jnp.broadcast_to(t_, (ch, LANES)) for t_ in ts], axis=0) for ts in totals]
        pad = jnp.zeros((LANES - n_ch, LANES), F32)
        decay_t = [jnp.exp(jnp.transpose(jnp.concatenate(ts + [pad], axis=0))) for ts in totals]
        qv = [q_ref[r, :].astype(F32) for r in rows]
        q_in = [(x * sigmoid(x) * jnp.exp(bb)).astype(BF16) for x, bb in zip(qv, b)]
        k_in = [(k * jnp.exp(-bb)).astype(BF16) for k, bb in zip(key, b)]
        k_end = [(k * jnp.exp(bl - bb)).astype(BF16) for k, bl, bb in zip(key, b_last, b)]
        vv = [i_ref[r, :] for r in rows]
        q2 = [jnp.concatenate([jnp.where(head0, x, jnp.zeros_like(x)), jnp.where(head0, jnp.zeros_like(x), x)],
                              axis=0) for x in q_in]
        att = [jnp.where(causal_bd2, dot_nt(x, k), 0.0).astype(BF16) for x, k in zip(q2, k_in)]
        intra = [jnp.dot(a_, v, preferred_element_type=F32) for a_, v in zip(att, vv)]
        o_intra = [jnp.where(head0, x[:sup], x[sup:]) for x in intra]
        ds_all = [dot_tn(k, expand(v)) for k, v in zip(k_end, vv)]
        s_stack = []
        for a in grp:
            s_prev = []
            for c in range(n_ch):
                s_prev.append(state.astype(BF16))
                ds = jnp.where(same_head, ds_all[a][:, c * LANES:(c + 1) * LANES], 0.0)
                state = decay_t[a][:, c:c + 1] * state + ds
            s_stack.append(jnp.concatenate(s_prev, axis=0))
        o_inter = [jnp.dot(expand(x), s_, preferred_element_type=F32) for x, s_ in zip(q_in, s_stack)]
        o = [x + y for x, y in zip(o_intra, o_inter)]
        sq = [x * x for x in o]
        ss0 = [jnp.sum(jnp.where(head0, x, 0.0), axis=-1, keepdims=True) for x in sq]
        ss1 = [jnp.sum(jnp.where(head0, 0.0, x), axis=-1, keepdims=True) for x in sq]
        ms = [jnp.where(head0, x, y) * (1.0 / HEAD_DIM) for x, y in zip(ss0, ss1)]
        gv = [g_ref[r, :].astype(F32) for r in rows]
        for r, x, m_, g_ in zip(rows, o, ms, gv):
            y_ref[r, :] = (x * lax.rsqrt(m_ + EPS) * ng * (g_ * sigmoid(g_))).astype(y_ref.dtype)
        return state

    lax.fori_loop(0, SEQ // (sup * HGRN_GROUP), body, jnp.zeros((LANES, LANES), F32))


def hgrn(hg3, gamma, norm_g):
    batch = hg3.shape[0]
    r = jnp.arange(HGRN_SUPER)
    same = (r[:, None] // HGRN_CHUNK) == (r[None, :] // HGRN_CHUNK)
    tri = (same & (r[:, None] >= r[None, :])).astype(BF16)
    col = lambda which: (lambda b, g: (b, 0, which * N_LANE_TILES + g))
    const2 = lambda b, g: (0, 0)
    return pl.pallas_call(
        _hgrn_kernel,
        grid=(batch, N_LANE_TILES),
        in_specs=[
            pl.BlockSpec((None, SEQ, LANES), col(0)),
            pl.BlockSpec((None, SEQ, LANES), col(1)),
            pl.BlockSpec((None, SEQ, LANES), col(2)),
            pl.BlockSpec((None, SEQ, LANES), col(3)),
            pl.BlockSpec((2, LANES), lambda b, g: (0, g)),
            pl.BlockSpec((1, LANES), lambda b, g: (0, g)),
            pl.BlockSpec((HGRN_SUPER, HGRN_SUPER), const2),
        ],
        out_specs=pl.BlockSpec((None, SEQ, LANES), lambda b, g: (b, 0, g)),
        out_shape=jax.ShapeDtypeStruct((batch, SEQ, WIDTH), BF16),
        compiler_params=_cparams("arbitrary", "arbitrary"),
        name="hgrn2",
    )(hg3, hg3, hg3, hg3, gamma.astype(F32), norm_g.reshape(1, WIDTH).astype(F32), tri)


def _post_mix_kernel(oa_ref, yh_ref, x_ref, ag_ref, wo_ref, g2_ref, wrh_ref, wrl_ref, br_ref, perm_t_ref,
                     x1_ref, h2_ref, lg_ref):
    oa = jnp.concatenate(
        [jnp.concatenate([oa_ref[r, g] for g in range(N_LANE_TILES)], axis=1)
         for r in range(N_RES)], axis=0).astype(F32)
    ms = jnp.mean(oa * oa, axis=-1, keepdims=True)
    ya = (oa * lax.rsqrt(ms + EPS) * ag_ref[...]).astype(BF16)
    ya = jnp.dot(perm_t_ref[...], ya, preferred_element_type=F32).astype(BF16)
    mix = (jnp.dot(ya, wo_ref[:WIDTH, :], preferred_element_type=F32)
           + jnp.dot(yh_ref[...], wo_ref[WIDTH:, :], preferred_element_type=F32))
    x1 = x_ref[...] + mix
    x1_ref[...] = x1
    ms2 = jnp.mean(x1 * x1, axis=-1, keepdims=True)
    h2 = x1 * lax.rsqrt(ms2 + EPS) * g2_ref[...]
    h2_ref[...] = _pack_halves(h2)
    hi = h2.astype(BF16)
    lo = (h2 - hi.astype(F32)).astype(BF16)
    wrh = wrh_ref[...]
    lg_ref[...] = (jnp.dot(hi, wrh, preferred_element_type=F32)
                   + jnp.dot(lo, wrh, preferred_element_type=F32)
                   + jnp.dot(hi, wrl_ref[...], preferred_element_type=F32)
                   + br_ref[...])


def post_mix(oa, yh, x2, attn_g, w_out_bf16, g2, wr_hi, wr_lo, br):
    n = x2.shape[0]
    tiles_per_b = SEQ // TILE_TOKENS
    row = lambda w: pl.BlockSpec((TILE_TOKENS, w), lambda i: (i, 0))
    const = lambda r, c: pl.BlockSpec((r, c), lambda i: (0, 0))
    return pl.pallas_call(
        _post_mix_kernel,
        grid=(n // TILE_TOKENS,),
        in_specs=[pl.BlockSpec((None, N_RES, N_LANE_TILES, U_PER_TILE, LANES),
                               lambda i: (i // tiles_per_b, 0, 0, i % tiles_per_b, 0)),
                  row(WIDTH), row(D_MODEL),
                  const(1, WIDTH), const(2 * WIDTH, D_MODEL), const(1, D_MODEL),
                  const(D_MODEL, LANES), const(D_MODEL, LANES), const(1, LANES),
                  const(TILE_TOKENS, TILE_TOKENS)],
        out_specs=[row(D_MODEL), row(PACKED_W), row(LANES)],
        out_shape=[jax.ShapeDtypeStruct((n, D_MODEL), F32),
                   jax.ShapeDtypeStruct((n, PACKED_W), U32),
                   jax.ShapeDtypeStruct((n, LANES), F32)],
        compiler_params=_cparams("arbitrary"),
        name="post_mix",
    )(oa, yh, x2, attn_g.reshape(1, WIDTH), w_out_bf16, g2.reshape(1, D_MODEL), wr_hi, wr_lo, br,
      _tile_permutation().T)


ROUTE_TM = 512
R_E1, R_E2, R_RANK1, R_RANK2, R_G1, R_G2 = 0, 1, 2, 3, 4, 5


def _route_kernel(lg_ref, tri_ref, rec_ref, cnt_ref, carry_ref):
    i = pl.program_id(0)

    @pl.when(i == 0)
    def _():
        carry_ref[...] = jnp.zeros_like(carry_ref)
        cnt_ref[...] = jnp.zeros_like(cnt_ref)

    lg = lg_ref[...]
    lane = lax.broadcasted_iota(I32, lg.shape, 1)
    lanef = lane.astype(F32)
    big = float(LANES)
    gmask = lane < N_GROUPS
    gl = jnp.where(gmask, lg, NEG)
    gmax = jnp.max(gl, axis=-1, keepdims=True)
    gsel = jnp.min(jnp.where(gmask & (gl == gmax), lanef, big), axis=-1, keepdims=True)
    gsum = jnp.sum(jnp.where(gmask, jnp.exp(gl - gmax), 0.0), axis=-1, keepdims=True)
    w_g = 1.0 / gsum
    lo = EXPERT_LANE0 + EXPERTS_PER_GROUP * gsel
    emask = (lanef >= lo) & (lanef < lo + EXPERTS_PER_GROUP)
    el = jnp.where(emask, lg, NEG)
    v1 = jnp.max(el, axis=-1, keepdims=True)
    i1 = jnp.min(jnp.where(emask & (el == v1), lanef, big), axis=-1, keepdims=True)
    emask2 = emask & (lanef != i1)
    el2 = jnp.where(emask2, lg, NEG)
    v2 = jnp.max(el2, axis=-1, keepdims=True)
    i2 = jnp.min(jnp.where(emask2 & (el2 == v2), lanef, big), axis=-1, keepdims=True)
    e2 = jnp.exp(v2 - v1)
    den = 1.0 / (1.0 + e2)
    g1 = w_g * den
    g2 = w_g * e2 * den
    oh1 = lanef == i1
    oh2 = lanef == i2
    onehot = (oh1 | oh2).astype(BF16)
    before = jnp.dot(tri_ref[...], onehot, preferred_element_type=F32) + carry_ref[...]
    rank1 = jnp.sum(jnp.where(oh1, before, 0.0), axis=-1, keepdims=True)
    rank2 = jnp.sum(jnp.where(oh2, before, 0.0), axis=-1, keepdims=True)
    carry_ref[...] += jnp.sum(onehot.astype(F32), axis=0, keepdims=True)
    cnt_ref[...] += lax.dot_general(onehot, jnp.ones((lg.shape[0], LANES), BF16),
                                    (((0,), (0,)), ((), ())), preferred_element_type=F32)
    rec = jnp.zeros_like(lg)
    for slot, val in ((R_E1, i1), (R_E2, i2), (R_RANK1, rank1),
                      (R_RANK2, rank2), (R_G1, g1), (R_G2, g2)):
        rec = jnp.where(lane == slot, val, rec)
    rec_ref[...] = rec


def route(logits):
    n = logits.shape[0]
    tm = ROUTE_TM
    r = jnp.arange(tm)
    tri = (r[:, None] > r[None, :]).astype(BF16)
    return pl.pallas_call(
        _route_kernel,
        grid=(n // tm,),
        in_specs=[pl.BlockSpec((tm, LANES), lambda i: (i, 0)),
                  pl.BlockSpec((tm, tm), lambda i: (0, 0))],
        out_specs=[pl.BlockSpec((tm, LANES), lambda i: (i, 0)),
                   pl.BlockSpec((LANES, LANES), lambda i: (0, 0))],
        out_shape=[jax.ShapeDtypeStruct((n, LANES), F32),
                   jax.ShapeDtypeStruct((LANES, LANES), F32)],
        scratch_shapes=[pltpu.VMEM((1, LANES), F32)],
        compiler_params=_cparams("arbitrary"),
        name="route",
    )(logits, tri)


def _dest_kernel(rec_ref, cnt_ref, dest_ref, ends_ref):
    cnt = cnt_ref[...]
    padded = jnp.floor((cnt + (FFN_BLOCK - 1)) * (1.0 / FFN_BLOCK)) * FFN_BLOCK
    row = lax.broadcasted_iota(I32, cnt.shape, 0)
    col = lax.broadcasted_iota(I32, cnt.shape, 1)
    starts = jnp.sum(jnp.where(row < col, padded, 0.0), axis=0, keepdims=True)
    ends_ref[...] = jnp.sum(jnp.where(row <= col, padded, 0.0), axis=0, keepdims=True)
    rec = rec_ref[...]
    lanef = lax.broadcasted_iota(I32, rec.shape, 1).astype(F32)
    lane = lax.broadcasted_iota(I32, rec.shape, 1)
    s1 = jnp.sum(jnp.where(lanef == rec[:, R_E1:R_E1 + 1], starts, 0.0), axis=-1, keepdims=True)
    s2 = jnp.sum(jnp.where(lanef == rec[:, R_E2:R_E2 + 1], starts, 0.0), axis=-1, keepdims=True)
    d1 = s1 + rec[:, R_RANK1:R_RANK1 + 1]
    d2 = s2 + rec[:, R_RANK2:R_RANK2 + 1]
    dest_ref[...] = jnp.where(lane == 0, d1, jnp.where(lane == 1, d2, 0.0)).astype(I32)


def dest_rows(rec, counts):
    n = rec.shape[0]
    tm = ROUTE_TM
    return pl.pallas_call(
        _dest_kernel,
        grid=(n // tm,),
        in_specs=[pl.BlockSpec((tm, LANES), lambda i: (i, 0)),
                  pl.BlockSpec((LANES, LANES), lambda i: (0, 0))],
        out_specs=[pl.BlockSpec((tm, LANES), lambda i: (i, 0)),
                   pl.BlockSpec((1, LANES), lambda i: (0, 0))],
        out_shape=[jax.ShapeDtypeStruct((n, LANES), I32),
                   jax.ShapeDtypeStruct((1, LANES), F32)],
        compiler_params=_cparams("arbitrary"),
        name="dest_rows",
    )(rec, counts)


DISPATCH_TM = 256
ROW_DMA_UNROLL = 8


def _dispatch_kernel(d1_ref, d2_ref, ends_ref, nv_ref, h2_ref, xs_ref, zero_buf, stage, sem, zsem, ld_sem):
    i = pl.program_id(0)
    base = i * DISPATCH_TM
    n_blocks = xs_ref.shape[0] // FFN_BLOCK

    @pl.when(i == 0)
    def _():
        zero_buf[...] = jnp.zeros_like(zero_buf)

        def zero_copy(row0):
            return pltpu.make_async_copy(zero_buf, xs_ref.at[pl.ds(row0, FFN_BLOCK)], zsem)

        def seg_end(e):
            return ends_ref[e], ends_ref[e] > jnp.where(e > 0, ends_ref[jnp.maximum(e - 1, 0)], 0)

        def start_e(e, c):
            end, nonempty = seg_end(e)

            @pl.when(nonempty)
            def _():
                zero_copy(pl.multiple_of(end - FFN_BLOCK, FFN_BLOCK)).start()
            return c
        lax.fori_loop(0, N_EXPERTS, start_e, 0)

        def start_b(blk, c):
            zero_copy(pl.multiple_of(blk * FFN_BLOCK, FFN_BLOCK)).start()
            return c
        lax.fori_loop(nv_ref[0], n_blocks, start_b, 0)

        def wait_e(e, c):
            _, nonempty = seg_end(e)

            @pl.when(nonempty)
            def _():
                zero_copy(0).wait()
            return c
        lax.fori_loop(0, N_EXPERTS, wait_e, 0)

        def wait_b(blk, c):
            zero_copy(0).wait()
            return c
        lax.fori_loop(nv_ref[0], n_blocks, wait_b, 0)

    n_steps = pl.num_programs(0)
    slot = i % 2
    stage_slot = i % 3

    def tile_load(tile, s):
        return pltpu.make_async_copy(h2_ref.at[pl.ds(tile * DISPATCH_TM, DISPATCH_TM)], stage.at[s],
                                     ld_sem.at[s])

    @pl.when(i == 0)
    def _():
        tile_load(0, 0).start()

    @pl.when(i + 1 < n_steps)
    def _():
        tile_load(i + 1, (i + 1) % 3).start()

    tile_load(i, stage_slot).wait()

    def row_copy(t, dest, s):
        return pltpu.make_async_copy(stage.at[stage_slot, pl.ds(t, 1)], xs_ref.at[pl.ds(dest, 1)], sem.at[s])

    def issue(t, c):
        row_copy(t, d1_ref[base + t], slot).start()
        row_copy(t, d2_ref[base + t], slot).start()
        return c
    lax.fori_loop(0, DISPATCH_TM, issue, 0, unroll=ROW_DMA_UNROLL)

    def drain_slot(s):
        def drain(t, c):
            row_copy(0, 0, s).wait()
            row_copy(0, 0, s).wait()
            return c
        lax.fori_loop(0, DISPATCH_TM, drain, 0, unroll=ROW_DMA_UNROLL)

    @pl.when(i > 0)
    def _():
        drain_slot(1 - slot)

    @pl.when(i + 1 == pl.num_programs(0))
    def _():
        drain_slot(slot)


def dispatch(h2, dest1, dest2, seg_ends, n_valid, n_rows):
    n = h2.shape[0]
    return pl.pallas_call(
        _dispatch_kernel,
        grid_spec=pltpu.PrefetchScalarGridSpec(
            num_scalar_prefetch=4,
            grid=(n // DISPATCH_TM,),
            in_specs=[pl.BlockSpec(memory_space=pl.ANY)],
            out_specs=pl.BlockSpec(memory_space=pl.ANY),
            scratch_shapes=[pltpu.VMEM((FFN_BLOCK, PACKED_W), U32),
                            pltpu.VMEM((3, DISPATCH_TM, PACKED_W), U32),
                            pltpu.SemaphoreType.DMA((2,)),
                            pltpu.SemaphoreType.DMA(()),
                            pltpu.SemaphoreType.DMA((3,))],
        ),
        out_shape=jax.ShapeDtypeStruct((n_rows, PACKED_W), U32),
        compiler_params=_cparams("arbitrary"),
        name="dispatch",
    )(dest1, dest2, seg_ends, n_valid, h2)


def _ffn_kernel(be_ref, nv_ref, xs_ref, wg_ref, wu_ref, wd_ref, ys_ref, wg_sc, wu_sc, wd_sc):
    i = pl.program_id(0)
    prev = be_ref[jnp.maximum(i - 1, 0)]
    fresh = (i == 0) | (be_ref[i] != prev)

    @pl.when(fresh)
    def _():
        wg_sc[...] = wg_ref[0].astype(BF16)
        wu_sc[...] = wu_ref[0].astype(BF16)
        wd_sc[...] = wd_ref[0].astype(BF16)

    @pl.when(i < nv_ref[0])
    def _():
        xa, xb = (t.astype(BF16) for t in _unpack_halves(xs_ref[...]))
        half = PACKED_W
        gate = (jnp.dot(xa, wg_sc[:half, :], preferred_element_type=F32)
                + jnp.dot(xb, wg_sc[half:, :], preferred_element_type=F32))
        up = (jnp.dot(xa, wu_sc[:half, :], preferred_element_type=F32)
              + jnp.dot(xb, wu_sc[half:, :], preferred_element_type=F32))
        hid = (gate * (1.0 / (1.0 + jnp.exp(-gate))) * up).astype(BF16)
        ys_ref[...] = _pack_halves(jnp.dot(hid, wd_sc[...], preferred_element_type=F32))

    @pl.when(i >= nv_ref[0])
    def _():
        ys_ref[...] = jnp.zeros_like(ys_ref)


def expert_ffn(xs, block_expert, n_valid, w_gate, w_up, w_down):
    n_rows = xs.shape[0]
    n_blocks = n_rows // FFN_BLOCK
    wmap = lambda i, be, nv: (be[i], 0, 0)
    return pl.pallas_call(
        _ffn_kernel,
        grid_spec=pltpu.PrefetchScalarGridSpec(
            num_scalar_prefetch=2,
            grid=(n_blocks,),
            in_specs=[pl.BlockSpec((FFN_BLOCK, PACKED_W), lambda i, be, nv: (jnp.minimum(i, nv[0] - 1), 0)),
                      pl.BlockSpec((1, D_MODEL, D_EXPERT), wmap),
                      pl.BlockSpec((1, D_MODEL, D_EXPERT), wmap),
                      pl.BlockSpec((1, D_EXPERT, D_MODEL), wmap)],
            out_specs=pl.BlockSpec((FFN_BLOCK, PACKED_W), lambda i, be, nv: (i, 0)),
            scratch_shapes=[pltpu.VMEM((D_MODEL, D_EXPERT), BF16),
                            pltpu.VMEM((D_MODEL, D_EXPERT), BF16),
                            pltpu.VMEM((D_EXPERT, D_MODEL), BF16)],
        ),
        out_shape=jax.ShapeDtypeStruct((n_rows, PACKED_W), U32),
        compiler_params=_cparams("arbitrary"),
        name="expert_ffn",
    )(block_expert, n_valid, xs, w_gate, w_up, w_down)


COMBINE_TM = 256


def _combine_kernel(d1_ref, d2_ref, x1_ref, rec_ref, gf_ref, ys_ref, out_ref, buf1, buf2, sem):
    i = pl.program_id(0)
    slot = i % 2

    def row_copy(src_row, buf, s, t):
        return pltpu.make_async_copy(ys_ref.at[pl.ds(src_row, 1)], buf.at[s, pl.ds(t, 1)], sem.at[s])

    def gather_tile(tile, s):
        base = tile * COMBINE_TM

        def issue(t, c):
            row_copy(d1_ref[base + t], buf1, s, t).start()
            row_copy(d2_ref[base + t], buf2, s, t).start()
            return c
        lax.fori_loop(0, COMBINE_TM, issue, 0, unroll=ROW_DMA_UNROLL)

    @pl.when(i == 0)
    def _():
        gather_tile(0, 0)

    @pl.when(i + 1 < pl.num_programs(0))
    def _():
        gather_tile(i + 1, 1 - slot)

    def drain(t, c):
        row_copy(0, buf1, slot, 0).wait()
        row_copy(0, buf2, slot, 0).wait()
        return c
    lax.fori_loop(0, COMBINE_TM, drain, 0, unroll=ROW_DMA_UNROLL)

    rec = rec_ref[...]
    g1 = rec[:, R_G1:R_G1 + 1]
    g2 = rec[:, R_G2:R_G2 + 1]
    y1a, y1b = _unpack_halves(buf1[slot])
    y2a, y2b = _unpack_halves(buf2[slot])
    half = PACKED_W
    xa = x1_ref[:, :half] + g1 * y1a + g2 * y2a
    xb = x1_ref[:, half:] + g1 * y1b + g2 * y2b
    ms = (jnp.sum(xa * xa, axis=-1, keepdims=True) + jnp.sum(xb * xb, axis=-1, keepdims=True)) * (1.0 / D_MODEL)
    scale = lax.rsqrt(ms + EPS)
    out_ref[:, :half] = xa * scale * gf_ref[:, :half]
    out_ref[:, half:] = xb * scale * gf_ref[:, half:]


def combine(x1, rec, norm_f_g, ys, dest1, dest2):
    n = x1.shape[0]
    tm = COMBINE_TM
    return pl.pallas_call(
        _combine_kernel,
        grid_spec=pltpu.PrefetchScalarGridSpec(
            num_scalar_prefetch=2,
            grid=(n // tm,),
            in_specs=[pl.BlockSpec((tm, D_MODEL), lambda i, d1, d2: (i, 0)),
                      pl.BlockSpec((tm, LANES), lambda i, d1, d2: (i, 0)),
                      pl.BlockSpec((1, D_MODEL), lambda i, d1, d2: (0, 0)),
                      pl.BlockSpec(memory_space=pl.ANY)],
            out_specs=pl.BlockSpec((tm, D_MODEL), lambda i, d1, d2: (i, 0)),
            scratch_shapes=[pltpu.VMEM((2, tm, PACKED_W), U32),
                            pltpu.VMEM((2, tm, PACKED_W), U32),
                            pltpu.SemaphoreType.DMA((2,))],
        ),
        out_shape=jax.ShapeDtypeStruct((n, D_MODEL), F32),
        compiler_params=_cparams("arbitrary"),
        name="combine",
    )(dest1, dest2, x1, rec, norm_f_g.reshape(1, D_MODEL), ys)


def _router_weights(w_group, b_group, w_router, b_router):
    w = jnp.zeros((D_MODEL, LANES), F32)
    w = w.at[:, :N_GROUPS].set(w_group).at[:, EXPERT_LANE0:EXPERT_LANE0 + N_EXPERTS].set(w_router)
    b = jnp.zeros((1, LANES), F32)
    b = b.at[0, :N_GROUPS].set(b_group).at[0, EXPERT_LANE0:EXPERT_LANE0 + N_EXPERTS].set(b_router)
    hi = w.astype(BF16)
    lo = (w - hi.astype(F32)).astype(BF16)
    return hi, lo, b


def _block_plan(ends_row, n_rows):
    seg_ends = ends_row[0, EXPERT_LANE0:EXPERT_LANE0 + N_EXPERTS].astype(I32)
    n_blocks = n_rows // FFN_BLOCK
    blk_start = jnp.arange(n_blocks, dtype=I32) * FFN_BLOCK
    block_expert = jnp.minimum(jnp.sum((seg_ends[None, :] <= blk_start[:, None]).astype(I32), axis=1),
                               N_EXPERTS - 1)
    n_valid = seg_ends[-1:] // FFN_BLOCK
    return seg_ends, block_expert, n_valid


def kernel(x, norm1_g, w_in, attn_norm_g, hgrn_gamma, hgrn_norm_g, w_out, norm2_g, w_group, b_group,
           w_router, b_router, w_gate, w_up, w_down, norm_f_g):
    batch, seq, d = x.shape
    assert seq == SEQ and d == D_MODEL and norm1_g.shape[0] == 1
    n = batch * seq
    x2 = x.reshape(n, d)
    qkv, hg = in_proj(x2, norm1_g[0], w_in[0].astype(BF16), batch)
    oa = attention(qkv)
    yh = hgrn(hg.reshape(batch, seq, HG_W), hgrn_gamma, hgrn_norm_g[0]).reshape(n, WIDTH)
    wr_hi, wr_lo, br = _router_weights(w_group[0], b_group[0], w_router[0], b_router[0])
    x1, h2, logits = post_mix(oa, yh, x2, attn_norm_g[0], w_out[0].astype(BF16), norm2_g[0],
                              wr_hi, wr_lo, br)
    rec, counts = route(logits)
    dest, ends_row = dest_rows(rec, counts)
    n_rows = 2 * n + N_EXPERTS * FFN_BLOCK
    seg_ends, block_expert, n_valid = _block_plan(ends_row, n_rows)
    dest1, dest2 = dest[:, 0], dest[:, 1]
    xs = dispatch(h2, dest1, dest2, seg_ends, n_valid, n_rows)
    ys = expert_ffn(xs, block_expert, n_valid, w_gate[0], w_up[0], w_down[0])
    return combine(x1, rec, norm_f_g, ys, dest1, dest2).reshape(batch, seq, d)
```

```python
import jax
import jax.numpy as jnp
from jax import lax
from jax.experimental import pallas as pl
from jax.experimental.pallas import tpu as pltpu

F32 = jnp.float32
BF16 = jnp.bfloat16
I32 = jnp.int32
U32 = jnp.uint32

D_MODEL = 1024
HEAD_DIM = 64
N_HEADS = 8
WIDTH = N_HEADS * HEAD_DIM
QKV_W = 3 * WIDTH
HG_W = 4 * WIDTH
SEQ = 2048
ATTN_BLOCK = 128
DILATIONS = (1, 4, 16)
N_RES = 16
RES_LEN = SEQ // N_RES
TILE_TOKENS = 512
U_PER_TILE = TILE_TOKENS // N_RES
HGRN_CHUNK = 32
HGRN_SUPER = 256
HGRN_GROUP = 4
N_GROUPS = 4
EXPERTS_PER_GROUP = 8
N_EXPERTS = 32
D_EXPERT = 512
EXPERT_LANE0 = 32
FFN_BLOCK = 256
PACKED_W = D_MODEL // 2
EPS = 1e-6
NEG = -1e30
LOG2E = 1.4426950408889634
LANES = 128
VMEM_LIMIT = 56 * 1024 * 1024


def _cparams(*sem):
    return pltpu.CompilerParams(dimension_semantics=sem, vmem_limit_bytes=VMEM_LIMIT)


def _pack_pair(a, b):
    hi = pltpu.bitcast(a.astype(BF16).astype(F32), U32)
    lo = pltpu.bitcast(b.astype(BF16).astype(F32), U32)
    return hi | (lo >> 16)


def _pack_halves(x):
    w = x.shape[1] // 2
    return _pack_pair(x[:, :w], x[:, w:])


def _unpack_halves(p):
    return pltpu.bitcast(p & jnp.uint32(0xFFFF0000), F32), pltpu.bitcast(p << 16, F32)


N_LANE_TILES = WIDTH // LANES


def _tile_permutation():
    i = jnp.arange(TILE_TOKENS)
    src = N_RES * (i % U_PER_TILE) + i // U_PER_TILE
    return (src[:, None] == jnp.arange(TILE_TOKENS)[None, :]).astype(BF16)


def _in_proj_kernel(x_ref, g_ref, w_ref, perm_ref, qkv_ref, hg_ref):
    x = x_ref[...]
    ms = jnp.mean(x * x, axis=-1, keepdims=True)
    h = (x * lax.rsqrt(ms + EPS) * g_ref[...]).astype(BF16)
    hp = jnp.dot(perm_ref[...], h, preferred_element_type=F32).astype(BF16)
    for c in range(QKV_W // WIDTH):
        res = jnp.dot(hp, w_ref[:, c * WIDTH:(c + 1) * WIDTH], preferred_element_type=F32)
        for r in range(N_RES):
            for g in range(N_LANE_TILES):
                qkv_ref[r, g, c] = res[r * U_PER_TILE:(r + 1) * U_PER_TILE, g * LANES:(g + 1) * LANES]
    for j in range(HG_W // WIDTH):
        wsl = slice(QKV_W + j * WIDTH, QKV_W + (j + 1) * WIDTH)
        hg_ref[:, j * WIDTH:(j + 1) * WIDTH] = jnp.dot(h, w_ref[:, wsl], preferred_element_type=F32).astype(BF16)


def in_proj(x2, g, w_bf16, batch):
    n = x2.shape[0]
    tiles_per_b = SEQ // TILE_TOKENS
    return pl.pallas_call(
        _in_proj_kernel,
        grid=(n // TILE_TOKENS,),
        in_specs=[
            pl.BlockSpec((TILE_TOKENS, D_MODEL), lambda i: (i, 0)),
            pl.BlockSpec((1, D_MODEL), lambda i: (0, 0)),
            pl.BlockSpec((D_MODEL, QKV_W + HG_W), lambda i: (0, 0)),
            pl.BlockSpec((TILE_TOKENS, TILE_TOKENS), lambda i: (0, 0)),
        ],
        out_specs=[
            pl.BlockSpec((None, N_RES, N_LANE_TILES, 3, U_PER_TILE, LANES),
                         lambda i: (i // tiles_per_b, 0, 0, 0, i % tiles_per_b, 0)),
            pl.BlockSpec((TILE_TOKENS, HG_W), lambda i: (i, 0)),
        ],
        out_shape=[
            jax.ShapeDtypeStruct((batch, N_RES, N_LANE_TILES, 3, RES_LEN, LANES), F32),
            jax.ShapeDtypeStruct((n, HG_W), BF16),
        ],
        compiler_params=_cparams("arbitrary"),
        name="in_proj",
    )(x2, g.reshape(1, D_MODEL), w_bf16, _tile_permutation())


ATTN_GROUP16 = 8
ATTN_GROUP1 = 5
HEADS_PER_TILE = LANES // HEAD_DIM


def _attn_bias_tables():
    slopes = 2.0 ** (-8.0 * jnp.arange(1, N_HEADS + 1, dtype=F32) / N_HEADS)

    def table(qpos, kpos, dil):
        sd = qpos[:, None] - kpos[None, :]
        b = -slopes[:, None, None] * (sd * dil).astype(F32)[None] * LOG2E
        return jnp.where(((sd >= 0) & (sd <= ATTN_BLOCK))[None], b, NEG)

    q = jnp.arange(ATTN_BLOCK)
    k = jnp.arange(2 * ATTN_BLOCK)
    q1 = 16 * (q % 8) + q // 8 + ATTN_BLOCK
    k1 = 16 * (k % 16) + k // 16
    q4 = 4 * (q % 32) + q // 32 + ATTN_BLOCK
    k4 = 4 * (k % 64) + k // 64
    q16 = q + ATTN_BLOCK
    none = jnp.full((N_HEADS, ATTN_BLOCK, 2 * ATTN_BLOCK), NEG, F32)
    tabs = [jnp.concatenate([table(q1, k1, 1), table(q1, q1, 1)], axis=-1),
            jnp.concatenate([table(q4, k4, 4), table(q4, q4, 4)], axis=-1),
            jnp.concatenate([none, table(q16, q16, 16)], axis=-1)]
    return jnp.stack(tabs, axis=0)


def _attn_bias(bias_ref, d_idx, with_prev):
    ksl = slice(0, 2 * ATTN_BLOCK) if with_prev else slice(2 * ATTN_BLOCK, 3 * ATTN_BLOCK)
    return jnp.concatenate([bias_ref[d_idx, h, :, ksl] for h in range(HEADS_PER_TILE)], axis=0)


def _attn_blocks(blocks):
    nb = ATTN_BLOCK
    lane = lax.broadcasted_iota(I32, (nb, LANES), 1)
    head0 = lane < HEAD_DIM
    biases = [blk[3] for blk in blocks]
    blocks = [blk[:3] for blk in blocks]
    qs = [q * (HEAD_DIM ** -0.5 * LOG2E) for q, _, _ in blocks]
    qq = [jnp.concatenate([jnp.where(head0, q, 0.0), jnp.where(head0, 0.0, q)], axis=0).astype(BF16)
          for q in qs]
    ks = [k.astype(BF16) for _, k, _ in blocks]
    vs = [v.astype(BF16) for _, _, v in blocks]
    v_ones = [jnp.concatenate([v, jnp.ones_like(v)], axis=-1) for v in vs]
    s = [lax.dot_general(q2, k, (((1,), (1,)), ((), ())), preferred_element_type=F32) + bias
         for q2, k, bias in zip(qq, ks, biases)]
    m = [jnp.max(x, axis=-1, keepdims=True) for x in s]
    p = [jnp.exp2(x - mx).astype(BF16) for x, mx in zip(s, m)]
    acc = [jnp.dot(px, vo, preferred_element_type=F32) for px, vo in zip(p, v_ones)]
    head0_wide = jnp.concatenate([head0, head0], axis=-1)
    outs = []
    for a, mx in zip(acc, m):
        ol = jnp.where(head0_wide, a[:nb], a[nb:])
        l = ol[:, LANES:]
        outs.append((ol[:, :LANES] * (1.0 / l), jnp.where(head0, mx[:nb], mx[nb:]) + jnp.log2(l)))
    return outs


def _attn_kernel(qkv_ref, bias_ref, o_ref, o4_ref, l4_ref, o16_ref, l16_ref, fin_ref):
    grp = ATTN_GROUP16

    def body16(j, c):
        rs = [j * grp + a for a in range(grp)]
        bias = _attn_bias(bias_ref, 2, False)
        res = _attn_blocks([(qkv_ref[r, 0], qkv_ref[r, 1], qkv_ref[r, 2], bias) for r in rs])
        for r, (o, l) in zip(rs, res):
            o16_ref[r] = o
            l16_ref[r] = l
        return c
    lax.fori_loop(0, N_RES // grp, body16, 0)

    def gather4(c, rho, u0, nu):
        return jnp.concatenate([qkv_ref[rho + 4 * a, c, pl.ds(u0, nu), :] for a in range(4)], axis=0)

    def store4(rho, u0, o, l):
        for a in range(4):
            o4_ref[rho + 4 * a, pl.ds(u0, 32), :] = o[a * 32:(a + 1) * 32]
            l4_ref[rho + 4 * a, pl.ds(u0, 32), :] = l[a * 32:(a + 1) * 32]

    def body4(j, c):
        bias = _attn_bias(bias_ref, 1, True)
        todo = [(2 * j + a, 32 * n) for a in range(2) for n in range(1, 4)]
        res = _attn_blocks([(gather4(0, rho, u0, 32), gather4(1, rho, u0 - 32, 64),
                             gather4(2, rho, u0 - 32, 64), bias) for rho, u0 in todo])
        for (rho, u0), (o, l) in zip(todo, res):
            store4(rho, u0, o, l)
        return c
    lax.fori_loop(0, 2, body4, 0)

    def gather1(ref, u0, nu, *lead):
        return jnp.concatenate([ref[(r,) + lead + (pl.ds(u0, nu), slice(None))] for r in range(N_RES)],
                               axis=0)

    def merge_and_store(u0, o1, l1):
        l4 = gather1(l4_ref, u0, 8)
        l16 = gather1(l16_ref, u0, 8)
        mx = jnp.maximum(jnp.maximum(l1, l4), l16)
        e1 = jnp.exp2(l1 - mx)
        e4 = jnp.exp2(l4 - mx)
        e16 = jnp.exp2(l16 - mx)
        inv = 1.0 / (e1 + e4 + e16)
        o = (e1 * inv) * o1 + (e4 * inv) * gather1(o4_ref, u0, 8) + (e16 * inv) * gather1(o16_ref, u0, 8)
        for r in range(N_RES):
            fin_ref[r, pl.ds(u0, 8), :] = o[r * 8:(r + 1) * 8]

    bias4 = _attn_bias(bias_ref, 1, False)
    first = [tuple(gather4(c, rho, 0, 32) for c in range(3)) + (bias4,) for rho in range(4)]
    first.append(tuple(gather1(qkv_ref, 0, 8, c) for c in range(3)) + (_attn_bias(bias_ref, 0, False),))
    res = _attn_blocks(first)
    for rho in range(4):
        store4(rho, 0, *res[rho])
    merge_and_store(0, *res[4])

    n_blocks1 = RES_LEN // 8 - 1

    def body1(j, c):
        u0s = [pl.multiple_of((1 + j * ATTN_GROUP1 + a) * 8, 8) for a in range(ATTN_GROUP1)]
        bias = _attn_bias(bias_ref, 0, True)
        res = _attn_blocks([(gather1(qkv_ref, u0, 8, 0), gather1(qkv_ref, u0 - 8, 16, 1),
                             gather1(qkv_ref, u0 - 8, 16, 2), bias) for u0 in u0s])
        for u0, (o, l) in zip(u0s, res):
            merge_and_store(u0, o, l)
        return c
    lax.fori_loop(0, n_blocks1 // ATTN_GROUP1, body1, 0)

    def emit(r, c):
        o_ref[r] = fin_ref[r].astype(o_ref.dtype)
        return c
    lax.fori_loop(0, N_RES, emit, 0)


def attention(qkv):
    batch = qkv.shape[0]
    scratch = pltpu.VMEM((N_RES, RES_LEN, LANES), F32)
    return pl.pallas_call(
        _attn_kernel,
        grid=(batch, N_LANE_TILES),
        in_specs=[
            pl.BlockSpec((None, N_RES, None, 3, RES_LEN, LANES), lambda b, g: (b, 0, g, 0, 0, 0)),
            pl.BlockSpec((len(DILATIONS), HEADS_PER_TILE, ATTN_BLOCK, 3 * ATTN_BLOCK),
                         lambda b, g: (0, g, 0, 0)),
        ],
        out_specs=pl.BlockSpec((None, N_RES, None, RES_LEN, LANES), lambda b, g: (b, 0, g, 0, 0)),
        out_shape=jax.ShapeDtypeStruct((batch, N_RES, N_LANE_TILES, RES_LEN, LANES), BF16),
        scratch_shapes=[scratch] * 5,
        compiler_params=_cparams("arbitrary", "arbitrary"),
        name="dilated_attention",
    )(qkv, _attn_bias_tables())


def _split3(x):
    a = x.astype(BF16)
    r = x - a.astype(F32)
    b = r.astype(BF16)
    c = (r - b.astype(F32)).astype(BF16)
    return a, b, c


def _hgrn_kernel(q_ref, f_ref, i_ref, g_ref, gamma_ref, ng_ref, tri_ref, y_ref):
    sup, ch = HGRN_SUPER, HGRN_CHUNK
    n_ch = sup // ch
    gam = gamma_ref[...]
    gmx = jnp.max(gam, axis=0, keepdims=True)
    ge = jnp.exp(gam - gmx)
    lb = ge[0:1] / jnp.sum(ge, axis=0, keepdims=True)
    ng = ng_ref[...]
    tri = tri_ref[...]
    ri = lax.broadcasted_iota(I32, (2 * sup, sup), 0) % sup
    ci = lax.broadcasted_iota(I32, (2 * sup, sup), 1)
    causal_bd2 = (ri // ch == ci // ch) & (ri >= ci)
    head0 = lax.broadcasted_iota(I32, (sup, LANES), 1) < HEAD_DIM
    row_chunk = lax.broadcasted_iota(I32, (sup, LANES), 0) // ch
    same_head = (lax.broadcasted_iota(I32, (LANES, LANES), 0) // HEAD_DIM
                 == lax.broadcasted_iota(I32, (LANES, LANES), 1) // HEAD_DIM)

    def sigmoid(x):
        return 1.0 / (1.0 + jnp.exp(-x))

    def dot_nt(a, b):
        return lax.dot_general(a, b, (((1,), (1,)), ((), ())), preferred_element_type=F32)

    def dot_tn(a, b):
        return lax.dot_general(a, b, (((0,), (0,)), ((), ())), preferred_element_type=F32)

    def expand(x):
        zero = jnp.zeros_like(x)
        return jnp.concatenate([jnp.where(row_chunk == c, x, zero) for c in range(n_ch)], axis=1)

    def body(j, state):
        grp = range(HGRN_GROUP)
        rows = [pl.ds(pl.multiple_of((j * HGRN_GROUP + a) * sup, sup), sup) for a in grp]
        fr = [f_ref[r, :].astype(F32) for r in rows]
        t = [jnp.exp(-jnp.abs(x)) for x in fr]
        rcp = [1.0 / (1.0 + x) for x in t]
        sig = [jnp.where(f >= 0, r, x * r) for f, r, x in zip(fr, rcp, t)]
        nsig = [jnp.where(f >= 0, x * r, r) for f, r, x in zip(fr, rcp, t)]
        logf = [jnp.log(lb + (1.0 - lb) * x) for x in sig]
        key = [(1.0 - lb) * x for x in nsig]
        parts = [_split3(x) for x in logf]
        b = [sum(jnp.dot(tri, p, preferred_element_type=F32) for p in ps) for ps in parts]
        totals = [[x[(c + 1) * ch - 1:(c + 1) * ch, :] for c in range(n_ch)] for x in b]
        b_last = [jnp.concatenate([jnp.broadcast_to(t_, (ch, LANES)) for t_ in ts], axis=0) for ts in totals]
        pad = jnp.zeros((LANES - n_ch, LANES), F32)
        decay_t = [jnp.exp(jnp.transpose(jnp.concatenate(ts + [pad], axis=0))) for ts in totals]
        qv = [q_ref[r, :].astype(F32) for r in rows]
        q_in = [(x * sigmoid(x) * jnp.exp(bb)).astype(BF16) for x, bb in zip(qv, b)]
        k_in = [(k * jnp.exp(-bb)).astype(BF16) for k, bb in zip(key, b)]
        k_end = [(k * jnp.exp(bl - bb)).astype(BF16) for k, bl, bb in zip(key, b_last, b)]
        vv = [i_ref[r, :] for r in rows]
        q2 = [jnp.concatenate([jnp.where(head0, x, jnp.zeros_like(x)), jnp.where(head0, jnp.zeros_like(x), x)],
                              axis=0) for x in q_in]
        att = [jnp.where(causal_bd2, dot_nt(x, k), 0.0).astype(BF16) for x, k in zip(q2, k_in)]
        intra = [jnp.dot(a_, v, preferred_element_type=F32) for a_, v in zip(att, vv)]
        o_intra = [jnp.where(head0, x[:sup], x[sup:]) for x in intra]
        ds_all = [dot_tn(k, expand(v)) for k, v in zip(k_end, vv)]
        s_stack = []
        for a in grp:
            s_prev = []
            for c in range(n_ch):
                s_prev.append(state.astype(BF16))
                ds = jnp.where(same_head, ds_all[a][:, c * LANES:(c + 1) * LANES], 0.0)
                state = decay_t[a][:, c:c + 1] * state + ds
            s_stack.append(jnp.concatenate(s_prev, axis=0))
        o_inter = [jnp.dot(expand(x), s_, preferred_element_type=F32) for x, s_ in zip(q_in, s_stack)]
        o = [x + y for x, y in zip(o_intra, o_inter)]
        sq = [x * x for x in o]
        ss0 = [jnp.sum(jnp.where(head0, x, 0.0), axis=-1, keepdims=True) for x in sq]
        ss1 = [jnp.sum(jnp.where(head0, 0.0, x), axis=-1, keepdims=True) for x in sq]
        ms = [jnp.where(head0, x, y) * (1.0 / HEAD_DIM) for x, y in zip(ss0, ss1)]
        gv = [g_ref[r, :].astype(F32) for r in rows]
        for r, x, m_, g_ in zip(rows, o, ms, gv):
            y_ref[r, :] = (x * lax.rsqrt(m_ + EPS) * ng * (g_ * sigmoid(g_))).astype(y_ref.dtype)
        return state

    lax.fori_loop(0, SEQ // (sup * HGRN_GROUP), body, jnp.zeros((LANES, LANES), F32))


def hgrn(hg3, gamma, norm_g):
    batch = hg3.shape[0]
    r = jnp.arange(HGRN_SUPER)
    same = (r[:, None] // HGRN_CHUNK) == (r[None, :] // HGRN_CHUNK)
    tri = (same & (r[:, None] >= r[None, :])).astype(BF16)
    col = lambda which: (lambda b, g: (b, 0, which * N_LANE_TILES + g))
    const2 = lambda b, g: (0, 0)
    return pl.pallas_call(
        _hgrn_kernel,
        grid=(batch, N_LANE_TILES),
        in_specs=[
            pl.BlockSpec((None, SEQ, LANES), col(0)),
            pl.BlockSpec((None, SEQ, LANES), col(1)),
            pl.BlockSpec((None, SEQ, LANES), col(2)),
            pl.BlockSpec((None, SEQ, LANES), col(3)),
            pl.BlockSpec((2, LANES), lambda b, g: (0, g)),
            pl.BlockSpec((1, LANES), lambda b, g: (0, g)),
            pl.BlockSpec((HGRN_SUPER, HGRN_SUPER), const2),
        ],
        out_specs=pl.BlockSpec((None, SEQ, LANES), lambda b, g: (b, 0, g)),
        out_shape=jax.ShapeDtypeStruct((batch, SEQ, WIDTH), BF16),
        compiler_params=_cparams("arbitrary", "arbitrary"),
        name="hgrn2",
    )(hg3, hg3, hg3, hg3, gamma.astype(F32), norm_g.reshape(1, WIDTH).astype(F32), tri)


def _post_mix_kernel(oa_ref, yh_ref, x_ref, ag_ref, wo_ref, g2_ref, wrh_ref, wrl_ref, br_ref, perm_t_ref,
                     x1_ref, h2_ref, lg_ref):
    oa = jnp.concatenate(
        [jnp.concatenate([oa_ref[r, g] for g in range(N_LANE_TILES)], axis=1)
         for r in range(N_RES)], axis=0).astype(F32)
    ms = jnp.mean(oa * oa, axis=-1, keepdims=True)
    ya = (oa * lax.rsqrt(ms + EPS) * ag_ref[...]).astype(BF16)
    ya = jnp.dot(perm_t_ref[...], ya, preferred_element_type=F32).astype(BF16)
    mix = (jnp.dot(ya, wo_ref[:WIDTH, :], preferred_element_type=F32)
           + jnp.dot(yh_ref[...], wo_ref[WIDTH:, :], preferred_element_type=F32))
    x1 = x_ref[...] + mix
    x1_ref[...] = x1
    ms2 = jnp.mean(x1 * x1, axis=-1, keepdims=True)
    h2 = x1 * lax.rsqrt(ms2 + EPS) * g2_ref[...]
    hi = h2.astype(BF16)
    h2_ref[...] = hi
    lo = (h2 - hi.astype(F32)).astype(BF16)
    wrh = wrh_ref[...]
    lg_ref[...] = (jnp.dot(hi, wrh, preferred_element_type=F32)
                   + jnp.dot(lo, wrh, preferred_element_type=F32)
                   + jnp.dot(hi, wrl_ref[...], preferred_element_type=F32)
                   + br_ref[...])


def post_mix(oa, yh, x2, attn_g, w_out_bf16, g2, wr_hi, wr_lo, br):
    n = x2.shape[0]
    tiles_per_b = SEQ // TILE_TOKENS
    row = lambda w: pl.BlockSpec((TILE_TOKENS, w), lambda i: (i, 0))
    const = lambda r, c: pl.BlockSpec((r, c), lambda i: (0, 0))
    return pl.pallas_call(
        _post_mix_kernel,
        grid=(n // TILE_TOKENS,),
        in_specs=[pl.BlockSpec((None, N_RES, N_LANE_TILES, U_PER_TILE, LANES),
                               lambda i: (i // tiles_per_b, 0, 0, i % tiles_per_b, 0)),
                  row(WIDTH), row(D_MODEL),
                  const(1, WIDTH), const(2 * WIDTH, D_MODEL), const(1, D_MODEL),
                  const(D_MODEL, LANES), const(D_MODEL, LANES), const(1, LANES),
                  const(TILE_TOKENS, TILE_TOKENS)],
        out_specs=[row(D_MODEL), row(D_MODEL), row(LANES)],
        out_shape=[jax.ShapeDtypeStruct((n, D_MODEL), F32),
                   jax.ShapeDtypeStruct((n, D_MODEL), BF16),
                   jax.ShapeDtypeStruct((n, LANES), F32)],
        compiler_params=_cparams("arbitrary"),
        name="post_mix",
    )(oa, yh, x2, attn_g.reshape(1, WIDTH), w_out_bf16, g2.reshape(1, D_MODEL), wr_hi, wr_lo, br,
      _tile_permutation().T)


ROUTE_TM = 512
R_E1, R_E2, R_SLOT1, R_SLOT2, R_G1, R_G2 = 0, 1, 2, 3, 4, 5
RUN_ALIGN = 8
LOCAL_ROWS = 2 * ROUTE_TM + N_EXPERTS * RUN_ALIGN


def _route_kernel(lg_ref, tri_ref, upper_ref, rec_ref, tcarry_ref, trows_ref, toff_ref, cnt_ref, carry_ref):
    i = pl.program_id(0)

    @pl.when(i == 0)
    def _():
        carry_ref[...] = jnp.zeros_like(carry_ref)
        cnt_ref[...] = jnp.zeros_like(cnt_ref)

    lg = lg_ref[...]
    lane = lax.broadcasted_iota(I32, lg.shape, 1)
    lanef = lane.astype(F32)
    big = float(LANES)
    gmask = lane < N_GROUPS
    gl = jnp.where(gmask, lg, NEG)
    gmax = jnp.max(gl, axis=-1, keepdims=True)
    gsel = jnp.min(jnp.where(gmask & (gl == gmax), lanef, big), axis=-1, keepdims=True)
    gsum = jnp.sum(jnp.where(gmask, jnp.exp(gl - gmax), 0.0), axis=-1, keepdims=True)
    w_g = 1.0 / gsum
    lo = EXPERT_LANE0 + EXPERTS_PER_GROUP * gsel
    emask = (lanef >= lo) & (lanef < lo + EXPERTS_PER_GROUP)
    el = jnp.where(emask, lg, NEG)
    v1 = jnp.max(el, axis=-1, keepdims=True)
    i1 = jnp.min(jnp.where(emask & (el == v1), lanef, big), axis=-1, keepdims=True)
    emask2 = emask & (lanef != i1)
    el2 = jnp.where(emask2, lg, NEG)
    v2 = jnp.max(el2, axis=-1, keepdims=True)
    i2 = jnp.min(jnp.where(emask2 & (el2 == v2), lanef, big), axis=-1, keepdims=True)
    e2 = jnp.exp(v2 - v1)
    den = 1.0 / (1.0 + e2)
    g1 = w_g * den
    g2 = w_g * e2 * den
    oh1 = lanef == i1
    oh2 = lanef == i2
    onehot = (oh1 | oh2).astype(BF16)
    before = jnp.dot(tri_ref[...], onehot, preferred_element_type=F32)
    count = jnp.sum(onehot.astype(F32), axis=0, keepdims=True)
    units = jnp.floor((count + (RUN_ALIGN - 1)) * (1.0 / RUN_ALIGN))
    rows = units * RUN_ALIGN
    off = RUN_ALIGN * jnp.dot(jnp.broadcast_to(units, (8, LANES)).astype(BF16), upper_ref[...],
                              preferred_element_type=F32)[0:1]
    slot1 = jnp.sum(jnp.where(oh1, off + before, 0.0), axis=-1, keepdims=True)
    slot2 = jnp.sum(jnp.where(oh2, off + before, 0.0), axis=-1, keepdims=True)
    tcarry_ref[0] = carry_ref[...]
    trows_ref[0] = rows
    toff_ref[0] = off
    carry_ref[...] += rows
    eye = lax.broadcasted_iota(I32, (LANES, LANES), 0) == lax.broadcasted_iota(I32, (LANES, LANES), 1)
    cnt_ref[...] += jnp.sum(jnp.where(eye, rows, 0.0), axis=1, keepdims=True)
    rec = jnp.zeros_like(lg)
    for slot, val in ((R_E1, i1), (R_E2, i2), (R_SLOT1, slot1),
                      (R_SLOT2, slot2), (R_G1, g1), (R_G2, g2)):
        rec = jnp.where(lane == slot, val, rec)
    rec_ref[...] = rec


def route(logits):
    n = logits.shape[0]
    tm = ROUTE_TM
    r = jnp.arange(tm)
    tri = (r[:, None] > r[None, :]).astype(BF16)
    ln = jnp.arange(LANES)
    upper = (ln[:, None] < ln[None, :]).astype(BF16)
    tile_row = pl.BlockSpec((1, 1, LANES), lambda i: (i, 0, 0))
    return pl.pallas_call(
        _route_kernel,
        grid=(n // tm,),
        in_specs=[pl.BlockSpec((tm, LANES), lambda i: (i, 0)),
                  pl.BlockSpec((tm, tm), lambda i: (0, 0)),
                  pl.BlockSpec((LANES, LANES), lambda i: (0, 0))],
        out_specs=[pl.BlockSpec((tm, LANES), lambda i: (i, 0))] + [tile_row] * 3
                  + [pl.BlockSpec((LANES, LANES), lambda i: (0, 0))],
        out_shape=[jax.ShapeDtypeStruct((n, LANES), F32)]
                  + [jax.ShapeDtypeStruct((n // tm, 1, LANES), F32)] * 3
                  + [jax.ShapeDtypeStruct((LANES, LANES), F32)],
        scratch_shapes=[pltpu.VMEM((1, LANES), F32)],
        compiler_params=_cparams("arbitrary"),
        name="route",
    )(logits, tri, upper)


P_START, P_ROWS, P_OFF = 0, 1, 2


def _plan_kernel(carry_ref, rows_ref, off_ref, cnt_ref, plan_ref, ends_ref):
    cnt = cnt_ref[...]
    padded = jnp.floor((cnt + (FFN_BLOCK - 1)) * (1.0 / FFN_BLOCK)) * FFN_BLOCK
    row = lax.broadcasted_iota(I32, cnt.shape, 0)
    col = lax.broadcasted_iota(I32, cnt.shape, 1)
    starts = jnp.sum(jnp.where(row < col, padded, 0.0), axis=0, keepdims=True)
    ends_ref[...] = jnp.sum(jnp.where(row <= col, padded, 0.0), axis=0, keepdims=True)
    plan_ref[P_START] = (carry_ref[...] + starts).astype(I32)
    plan_ref[P_ROWS] = rows_ref[...].astype(I32)
    plan_ref[P_OFF] = off_ref[...].astype(I32)


def run_plan(tabs, counts):
    n_tiles = tabs[0].shape[0]
    flat = [t.reshape(n_tiles, LANES) for t in tabs]
    return pl.pallas_call(
        _plan_kernel,
        out_shape=[jax.ShapeDtypeStruct((3, n_tiles, LANES), I32),
                   jax.ShapeDtypeStruct((1, LANES), F32)],
        compiler_params=pltpu.CompilerParams(vmem_limit_bytes=VMEM_LIMIT),
        name="run_plan",
    )(*flat, counts)


RUN_PIECES = (512, 256, 128, 64, 32, 16, 8)
SORT_CHUNK = 256


def _for_each_run_piece(plan_ref, tile, n_tiles, fn):
    plane = n_tiles * LANES

    def per_expert(e, c):
        idx = tile * LANES + EXPERT_LANE0 + e
        start = plan_ref[P_START * plane + idx]
        rows = plan_ref[P_ROWS * plane + idx]
        off = plan_ref[P_OFF * plane + idx]
        for size in RUN_PIECES:
            done = jnp.bitwise_and(rows, -2 * size)

            @pl.when(jnp.bitwise_and(rows, size) != 0)
            def _():
                fn(pl.multiple_of(off + done, RUN_ALIGN), pl.multiple_of(start + done, RUN_ALIGN), size)
        return c
    lax.fori_loop(0, N_EXPERTS, per_expert, 0)


def _dispatch_kernel(plan_ref, ends_ref, nv_ref, h2_ref, rec_ref, xs_ref, zero_buf, stage, sem, zsem):
    i = pl.program_id(0)
    n_steps = pl.num_programs(0)
    n_blocks = xs_ref.shape[0] // FFN_BLOCK

    @pl.when(i == 0)
    def _():
        zero_buf[...] = jnp.zeros_like(zero_buf)

        def zero_copy(row0):
            return pltpu.make_async_copy(zero_buf, xs_ref.at[pl.ds(row0, FFN_BLOCK)], zsem)

        def seg_end(e):
            return ends_ref[e], ends_ref[e] > jnp.where(e > 0, ends_ref[jnp.maximum(e - 1, 0)], 0)

        def start_e(e, c):
            end, nonempty = seg_end(e)

            @pl.when(nonempty)
            def _():
                zero_copy(pl.multiple_of(end - FFN_BLOCK, FFN_BLOCK)).start()
            return c
        lax.fori_loop(0, N_EXPERTS, start_e, 0)

        def start_b(blk, c):
            zero_copy(pl.multiple_of(blk * FFN_BLOCK, FFN_BLOCK)).start()
            return c
        lax.fori_loop(nv_ref[0], n_blocks, start_b, 0)

        def wait_e(e, c):
            _, nonempty = seg_end(e)

            @pl.when(nonempty)
            def _():
                zero_copy(0).wait()
            return c
        lax.fori_loop(0, N_EXPERTS, wait_e, 0)

        def wait_b(blk, c):
            zero_copy(0).wait()
            return c
        lax.fori_loop(nv_ref[0], n_blocks, wait_b, 0)

    slot = i % 2
    rec = rec_ref[...]
    s1 = rec[:, R_SLOT1:R_SLOT1 + 1]
    s2 = rec[:, R_SLOT2:R_SLOT2 + 1]
    half = PACKED_W
    for c in range(LOCAL_ROWS // SORT_CHUNK):
        lanef = (lax.broadcasted_iota(I32, (ROUTE_TM, SORT_CHUNK), 1) + c * SORT_CHUNK).astype(F32)
        sel_t = ((lanef == s1) | (lanef == s2)).astype(BF16)
        pick = lambda cols: lax.dot_general(sel_t, h2_ref[:, cols], (((0,), (0,)), ((), ())),
                                            preferred_element_type=F32)
        stage[slot, c * SORT_CHUNK:(c + 1) * SORT_CHUNK, :] = _pack_pair(pick(slice(0, half)),
                                                                         pick(slice(half, 2 * half)))

    def start_piece(lrow, grow, size):
        pltpu.make_async_copy(stage.at[slot, pl.ds(lrow, size)], xs_ref.at[pl.ds(grow, size)],
                              sem.at[slot]).start()
    _for_each_run_piece(plan_ref, i, n_steps, start_piece)

    def wait_pieces(tile, s):
        def wait_piece(lrow, grow, size):
            pltpu.make_async_copy(stage.at[s, pl.ds(0, size)], xs_ref.at[pl.ds(0, size)], sem.at[s]).wait()
        _for_each_run_piece(plan_ref, tile, n_steps, wait_piece)

    @pl.when(i > 0)
    def _():
        wait_pieces(i - 1, 1 - slot)

    @pl.when(i + 1 == n_steps)
    def _():
        wait_pieces(i, slot)


def dispatch(h2, rec, plan, seg_ends, n_valid, n_rows):
    n = h2.shape[0]
    return pl.pallas_call(
        _dispatch_kernel,
        grid_spec=pltpu.PrefetchScalarGridSpec(
            num_scalar_prefetch=3,
            grid=(n // ROUTE_TM,),
            in_specs=[pl.BlockSpec((ROUTE_TM, D_MODEL), lambda i, *_: (i, 0)),
                      pl.BlockSpec((ROUTE_TM, LANES), lambda i, *_: (i, 0))],
            out_specs=pl.BlockSpec(memory_space=pl.ANY),
            scratch_shapes=[pltpu.VMEM((FFN_BLOCK, PACKED_W), U32),
                            pltpu.VMEM((2, LOCAL_ROWS, PACKED_W), U32),
                            pltpu.SemaphoreType.DMA((2,)),
                            pltpu.SemaphoreType.DMA(())],
        ),
        out_shape=jax.ShapeDtypeStruct((n_rows, PACKED_W), U32),
        compiler_params=_cparams("arbitrary"),
        name="dispatch",
    )(plan, seg_ends, n_valid, h2, rec)


def _ffn_kernel(be_ref, nv_ref, xs_ref, wg_ref, wu_ref, wd_ref, ys_ref, wg_sc, wu_sc, wd_sc):
    i = pl.program_id(0)
    prev = be_ref[jnp.maximum(i - 1, 0)]
    fresh = (i == 0) | (be_ref[i] != prev)

    @pl.when(fresh)
    def _():
        wg_sc[...] = wg_ref[0].astype(BF16)
        wu_sc[...] = wu_ref[0].astype(BF16)
        wd_sc[...] = wd_ref[0].astype(BF16)

    @pl.when(i < nv_ref[0])
    def _():
        xa, xb = (t.astype(BF16) for t in _unpack_halves(xs_ref[...]))
        half = PACKED_W
        gate = (jnp.dot(xa, wg_sc[:half, :], preferred_element_type=F32)
                + jnp.dot(xb, wg_sc[half:, :], preferred_element_type=F32))
        up = (jnp.dot(xa, wu_sc[:half, :], preferred_element_type=F32)
              + jnp.dot(xb, wu_sc[half:, :], preferred_element_type=F32))
        hid = (gate * (1.0 / (1.0 + jnp.exp(-gate))) * up).astype(BF16)
        ys_ref[...] = _pack_halves(jnp.dot(hid, wd_sc[...], preferred_element_type=F32))

    @pl.when(i >= nv_ref[0])
    def _():
        ys_ref[...] = jnp.zeros_like(ys_ref)


def expert_ffn(xs, block_expert, n_valid, w_gate, w_up, w_down):
    n_rows = xs.shape[0]
    n_blocks = n_rows // FFN_BLOCK
    wmap = lambda i, be, nv: (be[i], 0, 0)
    return pl.pallas_call(
        _ffn_kernel,
        grid_spec=pltpu.PrefetchScalarGridSpec(
            num_scalar_prefetch=2,
            grid=(n_blocks,),
            in_specs=[pl.BlockSpec((FFN_BLOCK, PACKED_W), lambda i, be, nv: (jnp.minimum(i, nv[0] - 1), 0)),
                      pl.BlockSpec((1, D_MODEL, D_EXPERT), wmap),
                      pl.BlockSpec((1, D_MODEL, D_EXPERT), wmap),
                      pl.BlockSpec((1, D_EXPERT, D_MODEL), wmap)],
            out_specs=pl.BlockSpec((FFN_BLOCK, PACKED_W), lambda i, be, nv: (i, 0)),
            scratch_shapes=[pltpu.VMEM((D_MODEL, D_EXPERT), BF16),
                            pltpu.VMEM((D_MODEL, D_EXPERT), BF16),
                            pltpu.VMEM((D_EXPERT, D_MODEL), BF16)],
        ),
        out_shape=jax.ShapeDtypeStruct((n_rows, PACKED_W), U32),
        compiler_params=_cparams("arbitrary"),
        name="expert_ffn",
    )(block_expert, n_valid, xs, w_gate, w_up, w_down)


def _combine_kernel(plan_ref, x1_ref, rec_ref, gf_ref, ys_ref, out_ref, buf, sem):
    i = pl.program_id(0)
    n_steps = pl.num_programs(0)
    slot = i % 2

    def fetch_tile(tile, s):
        def start_piece(lrow, grow, size):
            pltpu.make_async_copy(ys_ref.at[pl.ds(grow, size)], buf.at[s, pl.ds(lrow, size)], sem.at[s]).start()
        _for_each_run_piece(plan_ref, tile, n_steps, start_piece)

    @pl.when(i == 0)
    def _():
        buf[...] = jnp.zeros_like(buf)
        fetch_tile(0, 0)

    @pl.when(i + 1 < n_steps)
    def _():
        fetch_tile(i + 1, 1 - slot)

    def wait_piece(lrow, grow, size):
        pltpu.make_async_copy(ys_ref.at[pl.ds(0, size)], buf.at[slot, pl.ds(0, size)], sem.at[slot]).wait()
    _for_each_run_piece(plan_ref, i, n_steps, wait_piece)

    rec = rec_ref[...]
    g1 = rec[:, R_G1:R_G1 + 1]
    g2 = rec[:, R_G2:R_G2 + 1]
    lanef = lax.broadcasted_iota(I32, (ROUTE_TM, LOCAL_ROWS), 1).astype(F32)
    sel1 = (lanef == rec[:, R_SLOT1:R_SLOT1 + 1]).astype(BF16)
    sel2 = (lanef == rec[:, R_SLOT2:R_SLOT2 + 1]).astype(BF16)
    ya, yb = (t.astype(BF16) for t in _unpack_halves(buf[slot]))
    y1a = jnp.dot(sel1, ya, preferred_element_type=F32)
    y2a = jnp.dot(sel2, ya, preferred_element_type=F32)
    y1b = jnp.dot(sel1, yb, preferred_element_type=F32)
    y2b = jnp.dot(sel2, yb, preferred_element_type=F32)
    half = PACKED_W
    xa = x1_ref[:, :half] + g1 * y1a + g2 * y2a
    xb = x1_ref[:, half:] + g1 * y1b + g2 * y2b
    ms = (jnp.sum(xa * xa, axis=-1, keepdims=True) + jnp.sum(xb * xb, axis=-1, keepdims=True)) * (1.0 / D_MODEL)
    scale = lax.rsqrt(ms + EPS)
    out_ref[:, :half] = xa * scale * gf_ref[:, :half]
    out_ref[:, half:] = xb * scale * gf_ref[:, half:]


def combine(x1, rec, norm_f_g, ys, plan):
    n = x1.shape[0]
    tm = ROUTE_TM
    return pl.pallas_call(
        _combine_kernel,
        grid_spec=pltpu.PrefetchScalarGridSpec(
            num_scalar_prefetch=1,
            grid=(n // tm,),
            in_specs=[pl.BlockSpec((tm, D_MODEL), lambda i, p: (i, 0)),
                      pl.BlockSpec((tm, LANES), lambda i, p: (i, 0)),
                      pl.BlockSpec((1, D_MODEL), lambda i, p: (0, 0)),
                      pl.BlockSpec(memory_space=pl.ANY)],
            out_specs=pl.BlockSpec((tm, D_MODEL), lambda i, p: (i, 0)),
            scratch_shapes=[pltpu.VMEM((2, LOCAL_ROWS, PACKED_W), U32),
                            pltpu.SemaphoreType.DMA((2,))],
        ),
        out_shape=jax.ShapeDtypeStruct((n, D_MODEL), F32),
        compiler_params=_cparams("arbitrary"),
        name="combine",
    )(plan, x1, rec, norm_f_g.reshape(1, D_MODEL), ys)


def _router_weights(w_group, b_group, w_router, b_router):
    w = jnp.zeros((D_MODEL, LANES), F32)
    w = w.at[:, :N_GROUPS].set(w_group).at[:, EXPERT_LANE0:EXPERT_LANE0 + N_EXPERTS].set(w_router)
    b = jnp.zeros((1, LANES), F32)
    b = b.at[0, :N_GROUPS].set(b_group).at[0, EXPERT_LANE0:EXPERT_LANE0 + N_EXPERTS].set(b_router)
    hi = w.astype(BF16)
    lo = (w - hi.astype(F32)).astype(BF16)
    return hi, lo, b


def _sorted_rows_bound(n_tokens):
    worst = (2 * n_tokens + (n_tokens // ROUTE_TM) * N_EXPERTS * (RUN_ALIGN - 1)
             + N_EXPERTS * (FFN_BLOCK - 1))
    return -(-worst // FFN_BLOCK) * FFN_BLOCK


def _block_plan(ends_row, n_rows):
    seg_ends = ends_row[0, EXPERT_LANE0:EXPERT_LANE0 + N_EXPERTS].astype(I32)
    n_blocks = n_rows // FFN_BLOCK
    blk_start = jnp.arange(n_blocks, dtype=I32) * FFN_BLOCK
    block_expert = jnp.minimum(jnp.sum((seg_ends[None, :] <= blk_start[:, None]).astype(I32), axis=1),
                               N_EXPERTS - 1)
    n_valid = seg_ends[-1:] // FFN_BLOCK
    return seg_ends, block_expert, n_valid


def kernel(x, norm1_g, w_in, attn_norm_g, hgrn_gamma, hgrn_norm_g, w_out, norm2_g, w_group, b_group,
           w_router, b_router, w_gate, w_up, w_down, norm_f_g):
    batch, seq, d = x.shape
    assert seq == SEQ and d == D_MODEL and norm1_g.shape[0] == 1
    n = batch * seq
    x2 = x.reshape(n, d)
    qkv, hg = in_proj(x2, norm1_g[0], w_in[0].astype(BF16), batch)
    oa = attention(qkv)
    yh = hgrn(hg.reshape(batch, seq, HG_W), hgrn_gamma, hgrn_norm_g[0]).reshape(n, WIDTH)
    wr_hi, wr_lo, br = _router_weights(w_group[0], b_group[0], w_router[0], b_router[0])
    x1, h2, logits = post_mix(oa, yh, x2, attn_norm_g[0], w_out[0].astype(BF16), norm2_g[0],
                              wr_hi, wr_lo, br)
    rec, tcarry, trows, toff, counts = route(logits)
    plan, ends_row = run_plan((tcarry, trows, toff), counts)
    plan = plan.reshape(-1)
    n_rows = _sorted_rows_bound(n)
    seg_ends, block_expert, n_valid = _block_plan(ends_row, n_rows)
    xs = dispatch(h2, rec, plan, seg_ends, n_valid, n_rows)
    ys = expert_ffn(xs, block_expert, n_valid, w_gate[0], w_up[0], w_down[0])
    return combine(x1, rec, norm_f_g, ys, plan).reshape(batch, seq, d)
```

```python
import jax
import jax.numpy as jnp
from jax import lax
from jax.experimental import pallas as pl
from jax.experimental.pallas import tpu as pltpu

F32 = jnp.float32
BF16 = jnp.bfloat16
I32 = jnp.int32
U32 = jnp.uint32

D_MODEL = 1024
HEAD_DIM = 64
N_HEADS = 8
WIDTH = N_HEADS * HEAD_DIM
QKV_W = 3 * WIDTH
HG_W = 4 * WIDTH
SEQ = 2048
ATTN_BLOCK = 128
DILATIONS = (1, 4, 16)
N_RES = 16
RES_LEN = SEQ // N_RES
TILE_TOKENS = 512
U_PER_TILE = TILE_TOKENS // N_RES
HGRN_CHUNK = 32
HGRN_SUPER = 256
HGRN_GROUP = 4
N_GROUPS = 4
EXPERTS_PER_GROUP = 8
N_EXPERTS = 32
D_EXPERT = 512
EXPERT_LANE0 = 32
FFN_BLOCK = 256
PACKED_W = D_MODEL // 2
EPS = 1e-6
NEG = -1e30
LOG2E = 1.4426950408889634
LANES = 128
VMEM_LIMIT = 56 * 1024 * 1024


def _cparams(*sem):
    return pltpu.CompilerParams(dimension_semantics=sem, vmem_limit_bytes=VMEM_LIMIT)


def _pack_pair(a, b):
    hi = pltpu.bitcast(a.astype(BF16).astype(F32), U32)
    lo = pltpu.bitcast(b.astype(BF16).astype(F32), U32)
    return hi | (lo >> 16)


def _pack_halves(x):
    w = x.shape[1] // 2
    return _pack_pair(x[:, :w], x[:, w:])


def _unpack_halves(p):
    return pltpu.bitcast(p & jnp.uint32(0xFFFF0000), F32), pltpu.bitcast(p << 16, F32)


N_LANE_TILES = WIDTH // LANES


def _tile_permutation():
    i = jnp.arange(TILE_TOKENS)
    src = N_RES * (i % U_PER_TILE) + i // U_PER_TILE
    return (src[:, None] == jnp.arange(TILE_TOKENS)[None, :]).astype(BF16)


def _in_proj_kernel(x_ref, g_ref, w_ref, perm_ref, qkv_ref, hg_ref):
    x = x_ref[...]
    ms = jnp.mean(x * x, axis=-1, keepdims=True)
    h = (x * lax.rsqrt(ms + EPS) * g_ref[...]).astype(BF16)
    hp = jnp.dot(perm_ref[...], h, preferred_element_type=F32).astype(BF16)
    for c in range(QKV_W // WIDTH):
        res = jnp.dot(hp, w_ref[:, c * WIDTH:(c + 1) * WIDTH], preferred_element_type=F32)
        for r in range(N_RES):
            for g in range(N_LANE_TILES):
                qkv_ref[r, g, c] = res[r * U_PER_TILE:(r + 1) * U_PER_TILE, g * LANES:(g + 1) * LANES]
    for j in range(HG_W // WIDTH):
        wsl = slice(QKV_W + j * WIDTH, QKV_W + (j + 1) * WIDTH)
        hg_ref[:, j * WIDTH:(j + 1) * WIDTH] = jnp.dot(h, w_ref[:, wsl], preferred_element_type=F32).astype(BF16)


def in_proj(x2, g, w_bf16, batch):
    n = x2.shape[0]
    tiles_per_b = SEQ // TILE_TOKENS
    return pl.pallas_call(
        _in_proj_kernel,
        grid=(n // TILE_TOKENS,),
        in_specs=[
            pl.BlockSpec((TILE_TOKENS, D_MODEL), lambda i: (i, 0)),
            pl.BlockSpec((1, D_MODEL), lambda i: (0, 0)),
            pl.BlockSpec((D_MODEL, QKV_W + HG_W), lambda i: (0, 0)),
            pl.BlockSpec((TILE_TOKENS, TILE_TOKENS), lambda i: (0, 0)),
        ],
        out_specs=[
            pl.BlockSpec((None, N_RES, N_LANE_TILES, 3, U_PER_TILE, LANES),
                         lambda i: (i // tiles_per_b, 0, 0, 0, i % tiles_per_b, 0)),
            pl.BlockSpec((TILE_TOKENS, HG_W), lambda i: (i, 0)),
        ],
        out_shape=[
            jax.ShapeDtypeStruct((batch, N_RES, N_LANE_TILES, 3, RES_LEN, LANES), F32),
            jax.ShapeDtypeStruct((n, HG_W), BF16),
        ],
        compiler_params=_cparams("arbitrary"),
        name="in_proj",
    )(x2, g.reshape(1, D_MODEL), w_bf16, _tile_permutation())


ATTN_GROUP16 = 8
ATTN_GROUP1 = 5
HEADS_PER_TILE = LANES // HEAD_DIM


def _attn_bias_tables():
    slopes = 2.0 ** (-8.0 * jnp.arange(1, N_HEADS + 1, dtype=F32) / N_HEADS)

    def table(qpos, kpos, dil):
        sd = qpos[:, None] - kpos[None, :]
        b = -slopes[:, None, None] * (sd * dil).astype(F32)[None] * LOG2E
        return jnp.where(((sd >= 0) & (sd <= ATTN_BLOCK))[None], b, NEG)

    q = jnp.arange(ATTN_BLOCK)
    k = jnp.arange(2 * ATTN_BLOCK)
    q1 = 16 * (q % 8) + q // 8 + ATTN_BLOCK
    k1 = 16 * (k % 16) + k // 16
    q4 = 4 * (q % 32) + q // 32 + ATTN_BLOCK
    k4 = 4 * (k % 64) + k // 64
    q16 = q + ATTN_BLOCK
    none = jnp.full((N_HEADS, ATTN_BLOCK, 2 * ATTN_BLOCK), NEG, F32)
    tabs = [jnp.concatenate([table(q1, k1, 1), table(q1, q1, 1)], axis=-1),
            jnp.concatenate([table(q4, k4, 4), table(q4, q4, 4)], axis=-1),
            jnp.concatenate([none, table(q16, q16, 16)], axis=-1)]
    return jnp.stack(tabs, axis=0)


def _attn_bias(bias_ref, d_idx, with_prev):
    ksl = slice(0, 2 * ATTN_BLOCK) if with_prev else slice(2 * ATTN_BLOCK, 3 * ATTN_BLOCK)
    return jnp.concatenate([bias_ref[d_idx, h, :, ksl] for h in range(HEADS_PER_TILE)], axis=0)


def _attn_blocks(blocks):
    nb = ATTN_BLOCK
    lane = lax.broadcasted_iota(I32, (nb, LANES), 1)
    head0 = lane < HEAD_DIM
    biases = [blk[3] for blk in blocks]
    blocks = [blk[:3] for blk in blocks]
    qs = [q * (HEAD_DIM ** -0.5 * LOG2E) for q, _, _ in blocks]
    qq = [jnp.concatenate([jnp.where(head0, q, 0.0), jnp.where(head0, 0.0, q)], axis=0).astype(BF16)
          for q in qs]
    ks = [k.astype(BF16) for _, k, _ in blocks]
    vs = [v.astype(BF16) for _, _, v in blocks]
    v_ones = [jnp.concatenate([v, jnp.ones_like(v)], axis=-1) for v in vs]
    s = [lax.dot_general(q2, k, (((1,), (1,)), ((), ())), preferred_element_type=F32) + bias
         for q2, k, bias in zip(qq, ks, biases)]
    m = [jnp.max(x, axis=-1, keepdims=True) for x in s]
    p = [jnp.exp2(x - mx).astype(BF16) for x, mx in zip(s, m)]
    acc = [jnp.dot(px, vo, preferred_element_type=F32) for px, vo in zip(p, v_ones)]
    head0_wide = jnp.concatenate([head0, head0], axis=-1)
    outs = []
    for a, mx in zip(acc, m):
        ol = jnp.where(head0_wide, a[:nb], a[nb:])
        l = ol[:, LANES:]
        outs.append((ol[:, :LANES] * (1.0 / l), jnp.where(head0, mx[:nb], mx[nb:]) + jnp.log2(l)))
    return outs


def _attn_kernel(qkv_ref, bias_ref, o_ref, o4_ref, l4_ref, o16_ref, l16_ref, fin_ref):
    grp = ATTN_GROUP16

    def body16(j, c):
        rs = [j * grp + a for a in range(grp)]
        bias = _attn_bias(bias_ref, 2, False)
        res = _attn_blocks([(qkv_ref[r, 0], qkv_ref[r, 1], qkv_ref[r, 2], bias) for r in rs])
        for r, (o, l) in zip(rs, res):
            o16_ref[r] = o
            l16_ref[r] = l
        return c
    lax.fori_loop(0, N_RES // grp, body16, 0)

    def gather4(c, rho, u0, nu):
        return jnp.concatenate([qkv_ref[rho + 4 * a, c, pl.ds(u0, nu), :] for a in range(4)], axis=0)

    def store4(rho, u0, o, l):
        for a in range(4):
            o4_ref[rho + 4 * a, pl.ds(u0, 32), :] = o[a * 32:(a + 1) * 32]
            l4_ref[rho + 4 * a, pl.ds(u0, 32), :] = l[a * 32:(a + 1) * 32]

    def body4(j, c):
        bias = _attn_bias(bias_ref, 1, True)
        todo = [(2 * j + a, 32 * n) for a in range(2) for n in range(1, 4)]
        res = _attn_blocks([(gather4(0, rho, u0, 32), gather4(1, rho, u0 - 32, 64),
                             gather4(2, rho, u0 - 32, 64), bias) for rho, u0 in todo])
        for (rho, u0), (o, l) in zip(todo, res):
            store4(rho, u0, o, l)
        return c
    lax.fori_loop(0, 2, body4, 0)

    def gather1(ref, u0, nu, *lead):
        return jnp.concatenate([ref[(r,) + lead + (pl.ds(u0, nu), slice(None))] for r in range(N_RES)],
                               axis=0)

    def merge_and_store(u0, o1, l1):
        l4 = gather1(l4_ref, u0, 8)
        l16 = gather1(l16_ref, u0, 8)
        mx = jnp.maximum(jnp.maximum(l1, l4), l16)
        e1 = jnp.exp2(l1 - mx)
        e4 = jnp.exp2(l4 - mx)
        e16 = jnp.exp2(l16 - mx)
        inv = 1.0 / (e1 + e4 + e16)
        o = (e1 * inv) * o1 + (e4 * inv) * gather1(o4_ref, u0, 8) + (e16 * inv) * gather1(o16_ref, u0, 8)
        for r in range(N_RES):
            fin_ref[r, pl.ds(u0, 8), :] = o[r * 8:(r + 1) * 8]

    bias4 = _attn_bias(bias_ref, 1, False)
    first = [tuple(gather4(c, rho, 0, 32) for c in range(3)) + (bias4,) for rho in range(4)]
    first.append(tuple(gather1(qkv_ref, 0, 8, c) for c in range(3)) + (_attn_bias(bias_ref, 0, False),))
    res = _attn_blocks(first)
    for rho in range(4):
        store4(rho, 0, *res[rho])
    merge_and_store(0, *res[4])

    n_blocks1 = RES_LEN // 8 - 1

    def body1(j, c):
        u0s = [pl.multiple_of((1 + j * ATTN_GROUP1 + a) * 8, 8) for a in range(ATTN_GROUP1)]
        bias = _attn_bias(bias_ref, 0, True)
        res = _attn_blocks([(gather1(qkv_ref, u0, 8, 0), gather1(qkv_ref, u0 - 8, 16, 1),
                             gather1(qkv_ref, u0 - 8, 16, 2), bias) for u0 in u0s])
        for u0, (o, l) in zip(u0s, res):
            merge_and_store(u0, o, l)
        return c
    lax.fori_loop(0, n_blocks1 // ATTN_GROUP1, body1, 0)

    def emit(r, c):
        o_ref[r] = fin_ref[r].astype(o_ref.dtype)
        return c
    lax.fori_loop(0, N_RES, emit, 0)


def attention(qkv):
    batch = qkv.shape[0]
    scratch = pltpu.VMEM((N_RES, RES_LEN, LANES), F32)
    return pl.pallas_call(
        _attn_kernel,
        grid=(batch, N_LANE_TILES),
        in_specs=[
            pl.BlockSpec((None, N_RES, None, 3, RES_LEN, LANES), lambda b, g: (b, 0, g, 0, 0, 0)),
            pl.BlockSpec((len(DILATIONS), HEADS_PER_TILE, ATTN_BLOCK, 3 * ATTN_BLOCK),
                         lambda b, g: (0, g, 0, 0)),
        ],
        out_specs=pl.BlockSpec((None, N_RES, None, RES_LEN, LANES), lambda b, g: (b, 0, g, 0, 0)),
        out_shape=jax.ShapeDtypeStruct((batch, N_RES, N_LANE_TILES, RES_LEN, LANES), BF16),
        scratch_shapes=[scratch] * 5,
        compiler_params=_cparams("arbitrary", "arbitrary"),
        name="dilated_attention",
    )(qkv, _attn_bias_tables())


def _split3(x):
    a = x.astype(BF16)
    r = x - a.astype(F32)
    b = r.astype(BF16)
    c = (r - b.astype(F32)).astype(BF16)
    return a, b, c


def _hgrn_kernel(q_ref, f_ref, i_ref, g_ref, gamma_ref, ng_ref, tri_ref, y_ref):
    sup, ch = HGRN_SUPER, HGRN_CHUNK
    n_ch = sup // ch
    gam = gamma_ref[...]
    gmx = jnp.max(gam, axis=0, keepdims=True)
    ge = jnp.exp(gam - gmx)
    lb = ge[0:1] / jnp.sum(ge, axis=0, keepdims=True)
    ng = ng_ref[...]
    tri = tri_ref[...]
    ri = lax.broadcasted_iota(I32, (2 * sup, sup), 0) % sup
    ci = lax.broadcasted_iota(I32, (2 * sup, sup), 1)
    causal_bd2 = (ri // ch == ci // ch) & (ri >= ci)
    head0 = lax.broadcasted_iota(I32, (sup, LANES), 1) < HEAD_DIM
    row_chunk = lax.broadcasted_iota(I32, (sup, LANES), 0) // ch
    same_head = (lax.broadcasted_iota(I32, (LANES, LANES), 0) // HEAD_DIM
                 == lax.broadcasted_iota(I32, (LANES, LANES), 1) // HEAD_DIM)

    def sigmoid(x):
        return 1.0 / (1.0 + jnp.exp(-x))

    def dot_nt(a, b):
        return lax.dot_general(a, b, (((1,), (1,)), ((), ())), preferred_element_type=F32)

    def dot_tn(a, b):
        return lax.dot_general(a, b, (((0,), (0,)), ((), ())), preferred_element_type=F32)

    def expand(x):
        zero = jnp.zeros_like(x)
        return jnp.concatenate([jnp.where(row_chunk == c, x, zero) for c in range(n_ch)], axis=1)

    def body(j, state):
        grp = range(HGRN_GROUP)
        rows = [pl.ds(pl.multiple_of((j * HGRN_GROUP + a) * sup, sup), sup) for a in grp]
        fr = [f_ref[r, :].astype(F32) for r in rows]
        t = [jnp.exp(-jnp.abs(x)) for x in fr]
        rcp = [1.0 / (1.0 + x) for x in t]
        sig = [jnp.where(f >= 0, r, x * r) for f, r, x in zip(fr, rcp, t)]
        nsig = [jnp.where(f >= 0, x * r, r) for f, r, x in zip(fr, rcp, t)]
        logf = [jnp.log(lb + (1.0 - lb) * x) for x in sig]
        key = [(1.0 - lb) * x for x in nsig]
        parts = [_split3(x) for x in logf]
        b = [sum(jnp.dot(tri, p, preferred_element_type=F32) for p in ps) for ps in parts]
        totals = [[x[(c + 1) * ch - 1:(c + 1) * ch, :] for c in range(n_ch)] for x in b]
        b_last = [jnp.concatenate([jnp.broadcast_to(t_, (ch, LANES)) for t_ in ts], axis=0) for ts in totals]
        pad = jnp.zeros((LANES - n_ch, LANES), F32)
        decay_t = [jnp.exp(jnp.transpose(jnp.concatenate(ts + [pad], axis=0))) for ts in totals]
        qv = [q_ref[r, :].astype(F32) for r in rows]
        q_in = [(x * sigmoid(x) * jnp.exp(bb)).astype(BF16) for x, bb in zip(qv, b)]
        k_in = [(k * jnp.exp(-bb)).astype(BF16) for k, bb in zip(key, b)]
        k_end = [(k * jnp.exp(bl - bb)).astype(BF16) for k, bl, bb in zip(key, b_last, b)]
        vv = [i_ref[r, :] for r in rows]
        q2 = [jnp.concatenate([jnp.where(head0, x, jnp.zeros_like(x)), jnp.where(head0, jnp.zeros_like(x), x)],
                              axis=0) for x in q_in]
        att = [jnp.where(causal_bd2, dot_nt(x, k), 0.0).astype(BF16) for x, k in zip(q2, k_in)]
        intra = [jnp.dot(a_, v, preferred_element_type=F32) for a_, v in zip(att, vv)]
        o_intra = [jnp.where(head0, x[:sup], x[sup:]) for x in intra]
        ds_all = [dot_tn(k, expand(v)) for k, v in zip(k_end, vv)]
        s_stack = []
        for a in grp:
            s_prev = []
            for c in range(n_ch):
                s_prev.append(state.astype(BF16))
                ds = jnp.where(same_head, ds_all[a][:, c * LANES:(c + 1) * LANES], 0.0)
                state = decay_t[a][:, c:c + 1] * state + ds
            s_stack.append(jnp.concatenate(s_prev, axis=0))
        o_inter = [jnp.dot(expand(x), s_, preferred_element_type=F32) for x, s_ in zip(q_in, s_stack)]
        o = [x + y for x, y in zip(o_intra, o_inter)]
        sq = [x * x for x in o]
        ss0 = [jnp.sum(jnp.where(head0, x, 0.0), axis=-1, keepdims=True) for x in sq]
        ss1 = [jnp.sum(jnp.where(head0, 0.0, x), axis=-1, keepdims=True) for x in sq]
        ms = [jnp.where(head0, x, y) * (1.0 / HEAD_DIM) for x, y in zip(ss0, ss1)]
        gv = [g_ref[r, :].astype(F32) for r in rows]
        for r, x, m_, g_ in zip(rows, o, ms, gv):
            y_ref[r, :] = (x * lax.rsqrt(m_ + EPS) * ng * (g_ * sigmoid(g_))).astype(y_ref.dtype)
        return state

    lax.fori_loop(0, SEQ // (sup * HGRN_GROUP), body, jnp.zeros((LANES, LANES), F32))


def hgrn(hg3, gamma, norm_g):
    batch = hg3.shape[0]
    r = jnp.arange(HGRN_SUPER)
    same = (r[:, None] // HGRN_CHUNK) == (r[None, :] // HGRN_CHUNK)
    tri = (same & (r[:, None] >= r[None, :])).astype(BF16)
    col = lambda which: (lambda b, g: (b, 0, which * N_LANE_TILES + g))
    const2 = lambda b, g: (0, 0)
    return pl.pallas_call(
        _hgrn_kernel,
        grid=(batch, N_LANE_TILES),
        in_specs=[
            pl.BlockSpec((None, SEQ, LANES), col(0)),
            pl.BlockSpec((None, SEQ, LANES), col(1)),
            pl.BlockSpec((None, SEQ, LANES), col(2)),
            pl.BlockSpec((None, SEQ, LANES), col(3)),
            pl.BlockSpec((2, LANES), lambda b, g: (0, g)),
            pl.BlockSpec((1, LANES), lambda b, g: (0, g)),
            pl.BlockSpec((HGRN_SUPER, HGRN_SUPER), const2),
        ],
        out_specs=pl.BlockSpec((None, SEQ, LANES), lambda b, g: (b, 0, g)),
        out_shape=jax.ShapeDtypeStruct((batch, SEQ, WIDTH), BF16),
        compiler_params=_cparams("arbitrary", "arbitrary"),
        name="hgrn2",
    )(hg3, hg3, hg3, hg3, gamma.astype(F32), norm_g.reshape(1, WIDTH).astype(F32), tri)


def _post_mix_kernel(oa_ref, yh_ref, x_ref, ag_ref, wo_ref, g2_ref, wrh_ref, wrl_ref, br_ref, perm_t_ref,
                     x1_ref, h2_ref, lg_ref):
    oa = jnp.concatenate(
        [jnp.concatenate([oa_ref[r, g] for g in range(N_LANE_TILES)], axis=1)
         for r in range(N_RES)], axis=0).astype(F32)
    ms = jnp.mean(oa * oa, axis=-1, keepdims=True)
    ya = (oa * lax.rsqrt(ms + EPS) * ag_ref[...]).astype(BF16)
    ya = jnp.dot(perm_t_ref[...], ya, preferred_element_type=F32).astype(BF16)
    mix = (jnp.dot(ya, wo_ref[:WIDTH, :], preferred_element_type=F32)
           + jnp.dot(yh_ref[...], wo_ref[WIDTH:, :], preferred_element_type=F32))
    x1 = x_ref[...] + mix
    x1_ref[...] = x1
    ms2 = jnp.mean(x1 * x1, axis=-1, keepdims=True)
    h2 = x1 * lax.rsqrt(ms2 + EPS) * g2_ref[...]
    hi = h2.astype(BF16)
    h2_ref[...] = hi
    lo = (h2 - hi.astype(F32)).astype(BF16)
    wrh = wrh_ref[...]
    lg_ref[...] = (jnp.dot(hi, wrh, preferred_element_type=F32)
                   + jnp.dot(lo, wrh, preferred_element_type=F32)
                   + jnp.dot(hi, wrl_ref[...], preferred_element_type=F32)
                   + br_ref[...])


def post_mix(oa, yh, x2, attn_g, w_out_bf16, g2, wr_hi, wr_lo, br):
    n = x2.shape[0]
    tiles_per_b = SEQ // TILE_TOKENS
    row = lambda w: pl.BlockSpec((TILE_TOKENS, w), lambda i: (i, 0))
    const = lambda r, c: pl.BlockSpec((r, c), lambda i: (0, 0))
    return pl.pallas_call(
        _post_mix_kernel,
        grid=(n // TILE_TOKENS,),
        in_specs=[pl.BlockSpec((None, N_RES, N_LANE_TILES, U_PER_TILE, LANES),
                               lambda i: (i // tiles_per_b, 0, 0, i % tiles_per_b, 0)),
                  row(WIDTH), row(D_MODEL),
                  const(1, WIDTH), const(2 * WIDTH, D_MODEL), const(1, D_MODEL),
                  const(D_MODEL, LANES), const(D_MODEL, LANES), const(1, LANES),
                  const(TILE_TOKENS, TILE_TOKENS)],
        out_specs=[row(D_MODEL), row(D_MODEL), row(LANES)],
        out_shape=[jax.ShapeDtypeStruct((n, D_MODEL), F32),
                   jax.ShapeDtypeStruct((n, D_MODEL), BF16),
                   jax.ShapeDtypeStruct((n, LANES), F32)],
        compiler_params=_cparams("arbitrary"),
        name="post_mix",
    )(oa, yh, x2, attn_g.reshape(1, WIDTH), w_out_bf16, g2.reshape(1, D_MODEL), wr_hi, wr_lo, br,
      _tile_permutation().T)


ROUTE_TM = 512
R_E1, R_E2, R_SLOT1, R_SLOT2, R_G1, R_G2 = 0, 1, 2, 3, 4, 5
TOTAL_LANE = 0
RUN_ALIGN = 8
LOCAL_ROWS = 2 * ROUTE_TM + N_EXPERTS * RUN_ALIGN


def _route_kernel(lg_ref, tri_ref, upper_ref, rec_ref, tcarry_ref, trows_ref, toff_ref, cnt_ref, carry_ref):
    i = pl.program_id(0)

    @pl.when(i == 0)
    def _():
        carry_ref[...] = jnp.zeros_like(carry_ref)
        cnt_ref[...] = jnp.zeros_like(cnt_ref)

    lg = lg_ref[...]
    lane = lax.broadcasted_iota(I32, lg.shape, 1)
    lanef = lane.astype(F32)
    big = float(LANES)
    gmask = lane < N_GROUPS
    gl = jnp.where(gmask, lg, NEG)
    gmax = jnp.max(gl, axis=-1, keepdims=True)
    gsel = jnp.min(jnp.where(gmask & (gl == gmax), lanef, big), axis=-1, keepdims=True)
    gsum = jnp.sum(jnp.where(gmask, jnp.exp(gl - gmax), 0.0), axis=-1, keepdims=True)
    w_g = 1.0 / gsum
    lo = EXPERT_LANE0 + EXPERTS_PER_GROUP * gsel
    emask = (lanef >= lo) & (lanef < lo + EXPERTS_PER_GROUP)
    el = jnp.where(emask, lg, NEG)
    v1 = jnp.max(el, axis=-1, keepdims=True)
    i1 = jnp.min(jnp.where(emask & (el == v1), lanef, big), axis=-1, keepdims=True)
    emask2 = emask & (lanef != i1)
    el2 = jnp.where(emask2, lg, NEG)
    v2 = jnp.max(el2, axis=-1, keepdims=True)
    i2 = jnp.min(jnp.where(emask2 & (el2 == v2), lanef, big), axis=-1, keepdims=True)
    e2 = jnp.exp(v2 - v1)
    den = 1.0 / (1.0 + e2)
    g1 = w_g * den
    g2 = w_g * e2 * den
    oh1 = lanef == i1
    oh2 = lanef == i2
    onehot = (oh1 | oh2).astype(BF16)
    before = jnp.dot(tri_ref[...], onehot, preferred_element_type=F32)
    count = jnp.sum(onehot.astype(F32), axis=0, keepdims=True)
    units = jnp.floor((count + (RUN_ALIGN - 1)) * (1.0 / RUN_ALIGN))
    rows = units * RUN_ALIGN
    off = RUN_ALIGN * jnp.dot(jnp.broadcast_to(units, (8, LANES)).astype(BF16), upper_ref[...],
                              preferred_element_type=F32)[0:1]
    slot1 = jnp.sum(jnp.where(oh1, off + before, 0.0), axis=-1, keepdims=True)
    slot2 = jnp.sum(jnp.where(oh2, off + before, 0.0), axis=-1, keepdims=True)
    tcarry_ref[0] = carry_ref[...]
    trows_ref[0] = rows
    total = jnp.sum(rows, axis=-1, keepdims=True)
    toff_ref[0] = jnp.where(lax.broadcasted_iota(I32, (1, LANES), 1) == TOTAL_LANE, total, off)
    carry_ref[...] += rows
    eye = lax.broadcasted_iota(I32, (LANES, LANES), 0) == lax.broadcasted_iota(I32, (LANES, LANES), 1)
    cnt_ref[...] += jnp.sum(jnp.where(eye, rows, 0.0), axis=1, keepdims=True)
    rec = jnp.zeros_like(lg)
    for slot, val in ((R_E1, i1), (R_E2, i2), (R_SLOT1, slot1),
                      (R_SLOT2, slot2), (R_G1, g1), (R_G2, g2)):
        rec = jnp.where(lane == slot, val, rec)
    rec_ref[...] = rec


def route(logits):
    n = logits.shape[0]
    tm = ROUTE_TM
    r = jnp.arange(tm)
    tri = (r[:, None] > r[None, :]).astype(BF16)
    ln = jnp.arange(LANES)
    upper = (ln[:, None] < ln[None, :]).astype(BF16)
    tile_row = pl.BlockSpec((1, 1, LANES), lambda i: (i, 0, 0))
    return pl.pallas_call(
        _route_kernel,
        grid=(n // tm,),
        in_specs=[pl.BlockSpec((tm, LANES), lambda i: (i, 0)),
                  pl.BlockSpec((tm, tm), lambda i: (0, 0)),
                  pl.BlockSpec((LANES, LANES), lambda i: (0, 0))],
        out_specs=[pl.BlockSpec((tm, LANES), lambda i: (i, 0))] + [tile_row] * 3
                  + [pl.BlockSpec((LANES, LANES), lambda i: (0, 0))],
        out_shape=[jax.ShapeDtypeStruct((n, LANES), F32)]
                  + [jax.ShapeDtypeStruct((n // tm, 1, LANES), F32)] * 3
                  + [jax.ShapeDtypeStruct((LANES, LANES), F32)],
        scratch_shapes=[pltpu.VMEM((1, LANES), F32)],
        compiler_params=_cparams("arbitrary"),
        name="route",
    )(logits, tri, upper)


P_START, P_ROWS, P_OFF = 0, 1, 2


def _plan_kernel(carry_ref, rows_ref, off_ref, cnt_ref, plan_ref, ends_ref):
    cnt = cnt_ref[...]
    padded = jnp.floor((cnt + (FFN_BLOCK - 1)) * (1.0 / FFN_BLOCK)) * FFN_BLOCK
    row = lax.broadcasted_iota(I32, cnt.shape, 0)
    col = lax.broadcasted_iota(I32, cnt.shape, 1)
    starts = jnp.sum(jnp.where(row < col, padded, 0.0), axis=0, keepdims=True)
    ends_ref[...] = jnp.sum(jnp.where(row <= col, padded, 0.0), axis=0, keepdims=True)
    plan_ref[P_START] = (carry_ref[...] + starts).astype(I32)
    plan_ref[P_ROWS] = rows_ref[...].astype(I32)
    plan_ref[P_OFF] = off_ref[...].astype(I32)


def run_plan(tabs, counts):
    n_tiles = tabs[0].shape[0]
    flat = [t.reshape(n_tiles, LANES) for t in tabs]
    return pl.pallas_call(
        _plan_kernel,
        out_shape=[jax.ShapeDtypeStruct((3, n_tiles, LANES), I32),
                   jax.ShapeDtypeStruct((1, LANES), F32)],
        compiler_params=pltpu.CompilerParams(vmem_limit_bytes=VMEM_LIMIT),
        name="run_plan",
    )(*flat, counts)


RUN_PIECE = 64
SMALL_PIECES = (32, 16, 8)
TOTAL_PIECES = (1024, 512, 256, 128, 64, 32, 16, 8)
SORT_CHUNK = 256


def _for_each_run_piece(plan_ref, tile, n_tiles, fn):
    plane = n_tiles * LANES

    def per_expert(e, c):
        idx = tile * LANES + EXPERT_LANE0 + e
        start = plan_ref[P_START * plane + idx]
        rows = plan_ref[P_ROWS * plane + idx]
        off = plan_ref[P_OFF * plane + idx]

        def big(j, c2):
            fn(pl.multiple_of(off + j * RUN_PIECE, RUN_ALIGN), pl.multiple_of(start + j * RUN_PIECE, RUN_ALIGN),
               RUN_PIECE)
            return c2
        lax.fori_loop(0, lax.shift_right_logical(rows, 6), big, 0)
        for size in SMALL_PIECES:
            done = jnp.bitwise_and(rows, -2 * size)

            @pl.when(jnp.bitwise_and(rows, size) != 0)
            def _():
                fn(pl.multiple_of(off + done, RUN_ALIGN), pl.multiple_of(start + done, RUN_ALIGN), size)
        return c
    lax.fori_loop(0, N_EXPERTS, per_expert, 0)


def _for_each_total_piece(plan_ref, tile, n_tiles, fn):
    total = plan_ref[P_OFF * n_tiles * LANES + tile * LANES + TOTAL_LANE]
    for size in TOTAL_PIECES:
        @pl.when(jnp.bitwise_and(total, size) != 0)
        def _():
            fn(size)


def _dispatch_kernel(plan_ref, ends_ref, nv_ref, h2_ref, rec_ref, xs_ref, zero_buf, stage, sem, zsem):
    i = pl.program_id(0)
    n_steps = pl.num_programs(0)
    n_blocks = xs_ref.shape[0] // FFN_BLOCK

    @pl.when(i == 0)
    def _():
        zero_buf[...] = jnp.zeros_like(zero_buf)

        def zero_copy(row0):
            return pltpu.make_async_copy(zero_buf, xs_ref.at[pl.ds(row0, FFN_BLOCK)], zsem)

        def seg_end(e):
            return ends_ref[e], ends_ref[e] > jnp.where(e > 0, ends_ref[jnp.maximum(e - 1, 0)], 0)

        def start_e(e, c):
            end, nonempty = seg_end(e)

            @pl.when(nonempty)
            def _():
                zero_copy(pl.multiple_of(end - FFN_BLOCK, FFN_BLOCK)).start()
            return c
        lax.fori_loop(0, N_EXPERTS, start_e, 0)

        def start_b(blk, c):
            zero_copy(pl.multiple_of(blk * FFN_BLOCK, FFN_BLOCK)).start()
            return c
        lax.fori_loop(nv_ref[0], n_blocks, start_b, 0)

        def wait_e(e, c):
            _, nonempty = seg_end(e)

            @pl.when(nonempty)
            def _():
                zero_copy(0).wait()
            return c
        lax.fori_loop(0, N_EXPERTS, wait_e, 0)

        def wait_b(blk, c):
            zero_copy(0).wait()
            return c
        lax.fori_loop(nv_ref[0], n_blocks, wait_b, 0)

    slot = i % 2
    rec = rec_ref[...]
    s1 = rec[:, R_SLOT1:R_SLOT1 + 1]
    s2 = rec[:, R_SLOT2:R_SLOT2 + 1]
    half = PACKED_W
    for c in range(LOCAL_ROWS // SORT_CHUNK):
        lanef = (lax.broadcasted_iota(I32, (ROUTE_TM, SORT_CHUNK), 1) + c * SORT_CHUNK).astype(F32)
        sel_t = ((lanef == s1) | (lanef == s2)).astype(BF16)
        pick = lambda cols: lax.dot_general(sel_t, h2_ref[:, cols], (((0,), (0,)), ((), ())),
                                            preferred_element_type=F32)
        stage[slot, c * SORT_CHUNK:(c + 1) * SORT_CHUNK, :] = _pack_pair(pick(slice(0, half)),
                                                                         pick(slice(half, 2 * half)))

    def start_piece(lrow, grow, size):
        pltpu.make_async_copy(stage.at[slot, pl.ds(lrow, size)], xs_ref.at[pl.ds(grow, size)],
                              sem.at[slot]).start()
    _for_each_run_piece(plan_ref, i, n_steps, start_piece)

    def wait_pieces(tile, s):
        def wait_piece(size):
            pltpu.make_async_copy(stage.at[s, pl.ds(0, size)], xs_ref.at[pl.ds(0, size)], sem.at[s]).wait()
        _for_each_total_piece(plan_ref, tile, n_steps, wait_piece)

    @pl.when(i > 0)
    def _():
        wait_pieces(i - 1, 1 - slot)

    @pl.when(i + 1 == n_steps)
    def _():
        wait_pieces(i, slot)


def dispatch(h2, rec, plan, seg_ends, n_valid, n_rows):
    n = h2.shape[0]
    return pl.pallas_call(
        _dispatch_kernel,
        grid_spec=pltpu.PrefetchScalarGridSpec(
            num_scalar_prefetch=3,
            grid=(n // ROUTE_TM,),
            in_specs=[pl.BlockSpec((ROUTE_TM, D_MODEL), lambda i, *_: (i, 0)),
                      pl.BlockSpec((ROUTE_TM, LANES), lambda i, *_: (i, 0))],
            out_specs=pl.BlockSpec(memory_space=pl.ANY),
            scratch_shapes=[pltpu.VMEM((FFN_BLOCK, PACKED_W), U32),
                            pltpu.VMEM((2, LOCAL_ROWS, PACKED_W), U32),
                            pltpu.SemaphoreType.DMA((2,)),
                            pltpu.SemaphoreType.DMA(())],
        ),
        out_shape=jax.ShapeDtypeStruct((n_rows, PACKED_W), U32),
        compiler_params=_cparams("arbitrary"),
        name="dispatch",
    )(plan, seg_ends, n_valid, h2, rec)


def _ffn_kernel(be_ref, nv_ref, xs_ref, wg_ref, wu_ref, wd_ref, ys_ref, wg_sc, wu_sc, wd_sc):
    i = pl.program_id(0)
    prev = be_ref[jnp.maximum(i - 1, 0)]
    fresh = (i == 0) | (be_ref[i] != prev)

    @pl.when(fresh)
    def _():
        wg_sc[...] = wg_ref[0].astype(BF16)
        wu_sc[...] = wu_ref[0].astype(BF16)
        wd_sc[...] = wd_ref[0].astype(BF16)

    @pl.when(i < nv_ref[0])
    def _():
        xa, xb = (t.astype(BF16) for t in _unpack_halves(xs_ref[...]))
        half = PACKED_W
        gate = (jnp.dot(xa, wg_sc[:half, :], preferred_element_type=F32)
                + jnp.dot(xb, wg_sc[half:, :], preferred_element_type=F32))
        up = (jnp.dot(xa, wu_sc[:half, :], preferred_element_type=F32)
              + jnp.dot(xb, wu_sc[half:, :], preferred_element_type=F32))
        hid = (gate * (1.0 / (1.0 + jnp.exp(-gate))) * up).astype(BF16)
        ys_ref[...] = _pack_halves(jnp.dot(hid, wd_sc[...], preferred_element_type=F32))

    @pl.when(i >= nv_ref[0])
    def _():
        ys_ref[...] = jnp.zeros_like(ys_ref)


def expert_ffn(xs, block_expert, n_valid, w_gate, w_up, w_down):
    n_rows = xs.shape[0]
    n_blocks = n_rows // FFN_BLOCK
    wmap = lambda i, be, nv: (be[i], 0, 0)
    return pl.pallas_call(
        _ffn_kernel,
        grid_spec=pltpu.PrefetchScalarGridSpec(
            num_scalar_prefetch=2,
            grid=(n_blocks,),
            in_specs=[pl.BlockSpec((FFN_BLOCK, PACKED_W), lambda i, be, nv: (jnp.minimum(i, nv[0] - 1), 0)),
                      pl.BlockSpec((1, D_MODEL, D_EXPERT), wmap),
                      pl.BlockSpec((1, D_MODEL, D_EXPERT), wmap),
                      pl.BlockSpec((1, D_EXPERT, D_MODEL), wmap)],
            out_specs=pl.BlockSpec((FFN_BLOCK, PACKED_W), lambda i, be, nv: (i, 0)),
            scratch_shapes=[pltpu.VMEM((D_MODEL, D_EXPERT), BF16),
                            pltpu.VMEM((D_MODEL, D_EXPERT), BF16),
                            pltpu.VMEM((D_EXPERT, D_MODEL), BF16)],
        ),
        out_shape=jax.ShapeDtypeStruct((n_rows, PACKED_W), U32),
        compiler_params=_cparams("arbitrary"),
        name="expert_ffn",
    )(block_expert, n_valid, xs, w_gate, w_up, w_down)


def _combine_kernel(plan_ref, x1_ref, rec_ref, gf_ref, ys_ref, out_ref, buf, sem):
    i = pl.program_id(0)
    n_steps = pl.num_programs(0)
    slot = i % 2

    def fetch_tile(tile, s):
        def start_piece(lrow, grow, size):
            pltpu.make_async_copy(ys_ref.at[pl.ds(grow, size)], buf.at[s, pl.ds(lrow, size)], sem.at[s]).start()
        _for_each_run_piece(plan_ref, tile, n_steps, start_piece)

    @pl.when(i == 0)
    def _():
        buf[...] = jnp.zeros_like(buf)
        fetch_tile(0, 0)

    @pl.when(i + 1 < n_steps)
    def _():
        fetch_tile(i + 1, 1 - slot)

    def wait_piece(size):
        pltpu.make_async_copy(ys_ref.at[pl.ds(0, size)], buf.at[slot, pl.ds(0, size)], sem.at[slot]).wait()
    _for_each_total_piece(plan_ref, i, n_steps, wait_piece)

    rec = rec_ref[...]
    lanef = lax.broadcasted_iota(I32, (ROUTE_TM, LOCAL_ROWS), 1).astype(F32)
    sel = (jnp.where(lanef == rec[:, R_SLOT1:R_SLOT1 + 1], rec[:, R_G1:R_G1 + 1], 0.0)
           + jnp.where(lanef == rec[:, R_SLOT2:R_SLOT2 + 1], rec[:, R_G2:R_G2 + 1], 0.0)).astype(BF16)
    ya, yb = (t.astype(BF16) for t in _unpack_halves(buf[slot]))
    half = PACKED_W
    xa = x1_ref[:, :half] + jnp.dot(sel, ya, preferred_element_type=F32)
    xb = x1_ref[:, half:] + jnp.dot(sel, yb, preferred_element_type=F32)
    ms = (jnp.sum(xa * xa, axis=-1, keepdims=True) + jnp.sum(xb * xb, axis=-1, keepdims=True)) * (1.0 / D_MODEL)
    scale = lax.rsqrt(ms + EPS)
    out_ref[:, :half] = xa * scale * gf_ref[:, :half]
    out_ref[:, half:] = xb * scale * gf_ref[:, half:]


def combine(x1, rec, norm_f_g, ys, plan):
    n = x1.shape[0]
    tm = ROUTE_TM
    return pl.pallas_call(
        _combine_kernel,
        grid_spec=pltpu.PrefetchScalarGridSpec(
            num_scalar_prefetch=1,
            grid=(n // tm,),
            in_specs=[pl.BlockSpec((tm, D_MODEL), lambda i, p: (i, 0)),
                      pl.BlockSpec((tm, LANES), lambda i, p: (i, 0)),
                      pl.BlockSpec((1, D_MODEL), lambda i, p: (0, 0)),
                      pl.BlockSpec(memory_space=pl.ANY)],
            out_specs=pl.BlockSpec((tm, D_MODEL), lambda i, p: (i, 0)),
            scratch_shapes=[pltpu.VMEM((2, LOCAL_ROWS, PACKED_W), U32),
                            pltpu.SemaphoreType.DMA((2,))],
        ),
        out_shape=jax.ShapeDtypeStruct((n, D_MODEL), F32),
        compiler_params=_cparams("arbitrary"),
        name="combine",
    )(plan, x1, rec, norm_f_g.reshape(1, D_MODEL), ys)


def _router_weights(w_group, b_group, w_router, b_router):
    w = jnp.zeros((D_MODEL, LANES), F32)
    w = w.at[:, :N_GROUPS].set(w_group).at[:, EXPERT_LANE0:EXPERT_LANE0 + N_EXPERTS].set(w_router)
    b = jnp.zeros((1, LANES), F32)
    b = b.at[0, :N_GROUPS].set(b_group).at[0, EXPERT_LANE0:EXPERT_LANE0 + N_EXPERTS].set(b_router)
    hi = w.astype(BF16)
    lo = (w - hi.astype(F32)).astype(BF16)
    return hi, lo, b


def _sorted_rows_bound(n_tokens):
    worst = (2 * n_tokens + (n_tokens // ROUTE_TM) * N_EXPERTS * (RUN_ALIGN - 1)
             + N_EXPERTS * (FFN_BLOCK - 1))
    return -(-worst // FFN_BLOCK) * FFN_BLOCK


def _block_plan(ends_row, n_rows):
    seg_ends = ends_row[0, EXPERT_LANE0:EXPERT_LANE0 + N_EXPERTS].astype(I32)
    n_blocks = n_rows // FFN_BLOCK
    blk_start = jnp.arange(n_blocks, dtype=I32) * FFN_BLOCK
    block_expert = jnp.minimum(jnp.sum((seg_ends[None, :] <= blk_start[:, None]).astype(I32), axis=1),
                               N_EXPERTS - 1)
    n_valid = seg_ends[-1:] // FFN_BLOCK
    return seg_ends, block_expert, n_valid


def kernel(x, norm1_g, w_in, attn_norm_g, hgrn_gamma, hgrn_norm_g, w_out, norm2_g, w_group, b_group,
           w_router, b_router, w_gate, w_up, w_down, norm_f_g):
    batch, seq, d = x.shape
    assert seq == SEQ and d == D_MODEL and norm1_g.shape[0] == 1
    n = batch * seq
    x2 = x.reshape(n, d)
    qkv, hg = in_proj(x2, norm1_g[0], w_in[0].astype(BF16), batch)
    oa = attention(qkv)
    yh = hgrn(hg.reshape(batch, seq, HG_W), hgrn_gamma, hgrn_norm_g[0]).reshape(n, WIDTH)
    wr_hi, wr_lo, br = _router_weights(w_group[0], b_group[0], w_router[0], b_router[0])
    x1, h2, logits = post_mix(oa, yh, x2, attn_norm_g[0], w_out[0].astype(BF16), norm2_g[0],
                              wr_hi, wr_lo, br)
    rec, tcarry, trows, toff, counts = route(logits)
    plan, ends_row = run_plan((tcarry, trows, toff), counts)
    plan = plan.reshape(-1)
    n_rows = _sorted_rows_bound(n)
    seg_ends, block_expert, n_valid = _block_plan(ends_row, n_rows)
    xs = dispatch(h2, rec, plan, seg_ends, n_valid, n_rows)
    ys = expert_ffn(xs, block_expert, n_valid, w_gate[0], w_up[0], w_down[0])
    return combine(x1, rec, norm_f_g, ys, plan).reshape(batch, seq, d)
```

```python
import jax
import jax.numpy as jnp
from jax import lax
from jax.experimental import pallas as pl
from jax.experimental.pallas import tpu as pltpu

F32 = jnp.float32
BF16 = jnp.bfloat16
I32 = jnp.int32
U32 = jnp.uint32

D_MODEL = 1024
HEAD_DIM = 64
N_HEADS = 8
WIDTH = N_HEADS * HEAD_DIM
QKV_W = 3 * WIDTH
HG_W = 4 * WIDTH
SEQ = 2048
ATTN_BLOCK = 128
DILATIONS = (1, 4, 16)
N_RES = 16
RES_LEN = SEQ // N_RES
TILE_TOKENS = 512
U_PER_TILE = TILE_TOKENS // N_RES
HGRN_CHUNK = 32
HGRN_SUPER = 256
HGRN_GROUP = 4
N_GROUPS = 4
EXPERTS_PER_GROUP = 8
N_EXPERTS = 32
D_EXPERT = 512
EXPERT_LANE0 = 32
FFN_BLOCK = 512
PACKED_W = D_MODEL // 2
EPS = 1e-6
NEG = -1e30
LOG2E = 1.4426950408889634
LANES = 128
VMEM_LIMIT = 56 * 1024 * 1024


def _cparams(*sem):
    return pltpu.CompilerParams(dimension_semantics=sem, vmem_limit_bytes=VMEM_LIMIT)


def _pack_pair(a, b):
    hi = pltpu.bitcast(a.astype(BF16).astype(F32), U32)
    lo = pltpu.bitcast(b.astype(BF16).astype(F32), U32)
    return hi | (lo >> 16)


def _pack_halves(x):
    w = x.shape[1] // 2
    return _pack_pair(x[:, :w], x[:, w:])


def _unpack_halves(p):
    return pltpu.bitcast(p & jnp.uint32(0xFFFF0000), F32), pltpu.bitcast(p << 16, F32)


N_LANE_TILES = WIDTH // LANES


def _tile_permutation():
    i = jnp.arange(TILE_TOKENS)
    src = N_RES * (i % U_PER_TILE) + i // U_PER_TILE
    return (src[:, None] == jnp.arange(TILE_TOKENS)[None, :]).astype(BF16)


def _in_proj_kernel(x_ref, g_ref, w_ref, perm_ref, qkv_ref, hg_ref):
    x = x_ref[...]
    ms = jnp.mean(x * x, axis=-1, keepdims=True)
    h = (x * lax.rsqrt(ms + EPS) * g_ref[...]).astype(BF16)
    hp = jnp.dot(perm_ref[...], h, preferred_element_type=F32).astype(BF16)
    for c in range(QKV_W // WIDTH):
        res = jnp.dot(hp, w_ref[:, c * WIDTH:(c + 1) * WIDTH], preferred_element_type=F32)
        for r in range(N_RES):
            for g in range(N_LANE_TILES):
                qkv_ref[r, g, c] = res[r * U_PER_TILE:(r + 1) * U_PER_TILE, g * LANES:(g + 1) * LANES]
    for j in range(HG_W // WIDTH):
        wsl = slice(QKV_W + j * WIDTH, QKV_W + (j + 1) * WIDTH)
        hg_ref[:, j * WIDTH:(j + 1) * WIDTH] = jnp.dot(h, w_ref[:, wsl], preferred_element_type=F32).astype(BF16)


def in_proj(x2, g, w_bf16, batch):
    n = x2.shape[0]
    tiles_per_b = SEQ // TILE_TOKENS
    return pl.pallas_call(
        _in_proj_kernel,
        grid=(n // TILE_TOKENS,),
        in_specs=[
            pl.BlockSpec((TILE_TOKENS, D_MODEL), lambda i: (i, 0)),
            pl.BlockSpec((1, D_MODEL), lambda i: (0, 0)),
            pl.BlockSpec((D_MODEL, QKV_W + HG_W), lambda i: (0, 0)),
            pl.BlockSpec((TILE_TOKENS, TILE_TOKENS), lambda i: (0, 0)),
        ],
        out_specs=[
            pl.BlockSpec((None, N_RES, N_LANE_TILES, 3, U_PER_TILE, LANES),
                         lambda i: (i // tiles_per_b, 0, 0, 0, i % tiles_per_b, 0)),
            pl.BlockSpec((TILE_TOKENS, HG_W), lambda i: (i, 0)),
        ],
        out_shape=[
            jax.ShapeDtypeStruct((batch, N_RES, N_LANE_TILES, 3, RES_LEN, LANES), F32),
            jax.ShapeDtypeStruct((n, HG_W), BF16),
        ],
        compiler_params=_cparams("arbitrary"),
        name="in_proj",
    )(x2, g.reshape(1, D_MODEL), w_bf16, _tile_permutation())


ATTN_GROUP16 = 8
ATTN_GROUP1 = 5
HEADS_PER_TILE = LANES // HEAD_DIM


def _attn_bias_tables():
    slopes = 2.0 ** (-8.0 * jnp.arange(1, N_HEADS + 1, dtype=F32) / N_HEADS)

    def table(qpos, kpos, dil):
        sd = qpos[:, None] - kpos[None, :]
        b = -slopes[:, None, None] * (sd * dil).astype(F32)[None] * LOG2E
        return jnp.where(((sd >= 0) & (sd <= ATTN_BLOCK))[None], b, NEG)

    q = jnp.arange(ATTN_BLOCK)
    k = jnp.arange(2 * ATTN_BLOCK)
    q1 = 16 * (q % 8) + q // 8 + ATTN_BLOCK
    k1 = 16 * (k % 16) + k // 16
    q4 = 4 * (q % 32) + q // 32 + ATTN_BLOCK
    k4 = 4 * (k % 64) + k // 64
    q16 = q + ATTN_BLOCK
    none = jnp.full((N_HEADS, ATTN_BLOCK, 2 * ATTN_BLOCK), NEG, F32)
    tabs = [jnp.concatenate([table(q1, k1, 1), table(q1, q1, 1)], axis=-1),
            jnp.concatenate([table(q4, k4, 4), table(q4, q4, 4)], axis=-1),
            jnp.concatenate([none, table(q16, q16, 16)], axis=-1)]
    return jnp.stack(tabs, axis=0)


def _attn_bias(bias_ref, d_idx, with_prev):
    ksl = slice(0, 2 * ATTN_BLOCK) if with_prev else slice(2 * ATTN_BLOCK, 3 * ATTN_BLOCK)
    return jnp.concatenate([bias_ref[d_idx, h, :, ksl] for h in range(HEADS_PER_TILE)], axis=0)


def _attn_blocks(blocks):
    nb = ATTN_BLOCK
    lane = lax.broadcasted_iota(I32, (nb, LANES), 1)
    head0 = lane < HEAD_DIM
    biases = [blk[3] for blk in blocks]
    blocks = [blk[:3] for blk in blocks]
    qs = [q * (HEAD_DIM ** -0.5 * LOG2E) for q, _, _ in blocks]
    qq = [jnp.concatenate([jnp.where(head0, q, 0.0), jnp.where(head0, 0.0, q)], axis=0).astype(BF16)
          for q in qs]
    ks = [k.astype(BF16) for _, k, _ in blocks]
    vs = [v.astype(BF16) for _, _, v in blocks]
    v_ones = [jnp.concatenate([v, jnp.ones_like(v)], axis=-1) for v in vs]
    s = [lax.dot_general(q2, k, (((1,), (1,)), ((), ())), preferred_element_type=F32) + bias
         for q2, k, bias in zip(qq, ks, biases)]
    m = [jnp.max(x, axis=-1, keepdims=True) for x in s]
    p = [jnp.exp2(x - mx).astype(BF16) for x, mx in zip(s, m)]
    acc = [jnp.dot(px, vo, preferred_element_type=F32) for px, vo in zip(p, v_ones)]
    head0_wide = jnp.concatenate([head0, head0], axis=-1)
    outs = []
    for a, mx in zip(acc, m):
        ol = jnp.where(head0_wide, a[:nb], a[nb:])
        l = ol[:, LANES:]
        outs.append((ol[:, :LANES] * (1.0 / l), jnp.where(head0, mx[:nb], mx[nb:]) + jnp.log2(l)))
    return outs


def _attn_kernel(qkv_ref, bias_ref, o_ref, o4_ref, l4_ref, o16_ref, l16_ref, fin_ref):
    grp = ATTN_GROUP16

    def body16(j, c):
        rs = [j * grp + a for a in range(grp)]
        bias = _attn_bias(bias_ref, 2, False)
        res = _attn_blocks([(qkv_ref[r, 0], qkv_ref[r, 1], qkv_ref[r, 2], bias) for r in rs])
        for r, (o, l) in zip(rs, res):
            o16_ref[r] = o
            l16_ref[r] = l
        return c
    lax.fori_loop(0, N_RES // grp, body16, 0)

    def gather4(c, rho, u0, nu):
        return jnp.concatenate([qkv_ref[rho + 4 * a, c, pl.ds(u0, nu), :] for a in range(4)], axis=0)

    def store4(rho, u0, o, l):
        for a in range(4):
            o4_ref[rho + 4 * a, pl.ds(u0, 32), :] = o[a * 32:(a + 1) * 32]
            l4_ref[rho + 4 * a, pl.ds(u0, 32), :] = l[a * 32:(a + 1) * 32]

    def body4(j, c):
        bias = _attn_bias(bias_ref, 1, True)
        todo = [(2 * j + a, 32 * n) for a in range(2) for n in range(1, 4)]
        res = _attn_blocks([(gather4(0, rho, u0, 32), gather4(1, rho, u0 - 32, 64),
                             gather4(2, rho, u0 - 32, 64), bias) for rho, u0 in todo])
        for (rho, u0), (o, l) in zip(todo, res):
            store4(rho, u0, o, l)
        return c
    lax.fori_loop(0, 2, body4, 0)

    def gather1(ref, u0, nu, *lead):
        return jnp.concatenate([ref[(r,) + lead + (pl.ds(u0, nu), slice(None))] for r in range(N_RES)],
                               axis=0)

    def merge_and_store(u0, o1, l1):
        l4 = gather1(l4_ref, u0, 8)
        l16 = gather1(l16_ref, u0, 8)
        mx = jnp.maximum(jnp.maximum(l1, l4), l16)
        e1 = jnp.exp2(l1 - mx)
        e4 = jnp.exp2(l4 - mx)
        e16 = jnp.exp2(l16 - mx)
        inv = 1.0 / (e1 + e4 + e16)
        o = (e1 * inv) * o1 + (e4 * inv) * gather1(o4_ref, u0, 8) + (e16 * inv) * gather1(o16_ref, u0, 8)
        for r in range(N_RES):
            fin_ref[r, pl.ds(u0, 8), :] = o[r * 8:(r + 1) * 8]

    bias4 = _attn_bias(bias_ref, 1, False)
    first = [tuple(gather4(c, rho, 0, 32) for c in range(3)) + (bias4,) for rho in range(4)]
    first.append(tuple(gather1(qkv_ref, 0, 8, c) for c in range(3)) + (_attn_bias(bias_ref, 0, False),))
    res = _attn_blocks(first)
    for rho in range(4):
        store4(rho, 0, *res[rho])
    merge_and_store(0, *res[4])

    n_blocks1 = RES_LEN // 8 - 1

    def body1(j, c):
        u0s = [pl.multiple_of((1 + j * ATTN_GROUP1 + a) * 8, 8) for a in range(ATTN_GROUP1)]
        bias = _attn_bias(bias_ref, 0, True)
        res = _attn_blocks([(gather1(qkv_ref, u0, 8, 0), gather1(qkv_ref, u0 - 8, 16, 1),
                             gather1(qkv_ref, u0 - 8, 16, 2), bias) for u0 in u0s])
        for u0, (o, l) in zip(u0s, res):
            merge_and_store(u0, o, l)
        return c
    lax.fori_loop(0, n_blocks1 // ATTN_GROUP1, body1, 0)

    def emit(r, c):
        o_ref[r] = fin_ref[r].astype(o_ref.dtype)
        return c
    lax.fori_loop(0, N_RES, emit, 0)


def attention(qkv):
    batch = qkv.shape[0]
    scratch = pltpu.VMEM((N_RES, RES_LEN, LANES), F32)
    return pl.pallas_call(
        _attn_kernel,
        grid=(batch, N_LANE_TILES),
        in_specs=[
            pl.BlockSpec((None, N_RES, None, 3, RES_LEN, LANES), lambda b, g: (b, 0, g, 0, 0, 0)),
            pl.BlockSpec((len(DILATIONS), HEADS_PER_TILE, ATTN_BLOCK, 3 * ATTN_BLOCK),
                         lambda b, g: (0, g, 0, 0)),
        ],
        out_specs=pl.BlockSpec((None, N_RES, None, RES_LEN, LANES), lambda b, g: (b, 0, g, 0, 0)),
        out_shape=jax.ShapeDtypeStruct((batch, N_RES, N_LANE_TILES, RES_LEN, LANES), BF16),
        scratch_shapes=[scratch] * 5,
        compiler_params=_cparams("arbitrary", "arbitrary"),
        name="dilated_attention",
    )(qkv, _attn_bias_tables())


def _split3(x):
    a = x.astype(BF16)
    r = x - a.astype(F32)
    b = r.astype(BF16)
    c = (r - b.astype(F32)).astype(BF16)
    return a, b, c


def _hgrn_kernel(q_ref, f_ref, i_ref, g_ref, gamma_ref, ng_ref, tri_ref, y_ref):
    sup, ch = HGRN_SUPER, HGRN_CHUNK
    n_ch = sup // ch
    gam = gamma_ref[...]
    gmx = jnp.max(gam, axis=0, keepdims=True)
    ge = jnp.exp(gam - gmx)
    lb = ge[0:1] / jnp.sum(ge, axis=0, keepdims=True)
    ng = ng_ref[...]
    tri = tri_ref[...]
    ri = lax.broadcasted_iota(I32, (2 * sup, sup), 0) % sup
    ci = lax.broadcasted_iota(I32, (2 * sup, sup), 1)
    causal_bd2 = (ri // ch == ci // ch) & (ri >= ci)
    head0 = lax.broadcasted_iota(I32, (sup, LANES), 1) < HEAD_DIM
    row_chunk = lax.broadcasted_iota(I32, (sup, LANES), 0) // ch
    same_head = (lax.broadcasted_iota(I32, (LANES, LANES), 0) // HEAD_DIM
                 == lax.broadcasted_iota(I32, (LANES, LANES), 1) // HEAD_DIM)

    def sigmoid(x):
        return 1.0 / (1.0 + jnp.exp(-x))

    def dot_nt(a, b):
        return lax.dot_general(a, b, (((1,), (1,)), ((), ())), preferred_element_type=F32)

    def dot_tn(a, b):
        return lax.dot_general(a, b, (((0,), (0,)), ((), ())), preferred_element_type=F32)

    def expand(x):
        zero = jnp.zeros_like(x)
        return jnp.concatenate([jnp.where(row_chunk == c, x, zero) for c in range(n_ch)], axis=1)

    def body(j, state):
        grp = range(HGRN_GROUP)
        rows = [pl.ds(pl.multiple_of((j * HGRN_GROUP + a) * sup, sup), sup) for a in grp]
        fr = [f_ref[r, :].astype(F32) for r in rows]
        t = [jnp.exp(-jnp.abs(x)) for x in fr]
        rcp = [1.0 / (1.0 + x) for x in t]
        sig = [jnp.where(f >= 0, r, x * r) for f, r, x in zip(fr, rcp, t)]
        nsig = [jnp.where(f >= 0, x * r, r) for f, r, x in zip(fr, rcp, t)]
        logf = [jnp.log(lb + (1.0 - lb) * x) for x in sig]
        key = [(1.0 - lb) * x for x in nsig]
        parts = [_split3(x) for x in logf]
        b = [sum(jnp.dot(tri, p, preferred_element_type=F32) for p in ps) for ps in parts]
        totals = [[x[(c + 1) * ch - 1:(c + 1) * ch, :] for c in range(n_ch)] for x in b]
        b_last = [jnp.concatenate([jnp.broadcast_to(t_, (ch, LANES)) for t_ in ts], axis=0) for ts in totals]
        pad = jnp.zeros((LANES - n_ch, LANES), F32)
        decay_t = [jnp.exp(jnp.transpose(jnp.concatenate(ts + [pad], axis=0))) for ts in totals]
        qv = [q_ref[r, :].astype(F32) for r in rows]
        q_in = [(x * sigmoid(x) * jnp.exp(bb)).astype(BF16) for x, bb in zip(qv, b)]
        k_in = [(k * jnp.exp(-bb)).astype(BF16) for k, bb in zip(key, b)]
        k_end = [(k * jnp.exp(bl - bb)).astype(BF16) for k, bl, bb in zip(key, b_last, b)]
        vv = [i_ref[r, :] for r in rows]
        q2 = [jnp.concatenate([jnp.where(head0, x, jnp.zeros_like(x)), jnp.where(head0, jnp.zeros_like(x), x)],
                              axis=0) for x in q_in]
        att = [jnp.where(causal_bd2, dot_nt(x, k), 0.0).astype(BF16) for x, k in zip(q2, k_in)]
        intra = [jnp.dot(a_, v, preferred_element_type=F32) for a_, v in zip(att, vv)]
        o_intra = [jnp.where(head0, x[:sup], x[sup:]) for x in intra]
        ds_all = [dot_tn(k, expand(v)) for k, v in zip(k_end, vv)]
        s_stack = []
        for a in grp:
            s_prev = []
            for c in range(n_ch):
                s_prev.append(state.astype(BF16))
                ds = jnp.where(same_head, ds_all[a][:, c * LANES:(c + 1) * LANES], 0.0)
                state = decay_t[a][:, c:c + 1] * state + ds
            s_stack.append(jnp.concatenate(s_prev, axis=0))
        o_inter = [jnp.dot(expand(x), s_, preferred_element_type=F32) for x, s_ in zip(q_in, s_stack)]
        o = [x + y for x, y in zip(o_intra, o_inter)]
        sq = [x * x for x in o]
        ss0 = [jnp.sum(jnp.where(head0, x, 0.0), axis=-1, keepdims=True) for x in sq]
        ss1 = [jnp.sum(jnp.where(head0, 0.0, x), axis=-1, keepdims=True) for x in sq]
        ms = [jnp.where(head0, x, y) * (1.0 / HEAD_DIM) for x, y in zip(ss0, ss1)]
        gv = [g_ref[r, :].astype(F32) for r in rows]
        for r, x, m_, g_ in zip(rows, o, ms, gv):
            y_ref[r, :] = (x * lax.rsqrt(m_ + EPS) * ng * (g_ * sigmoid(g_))).astype(y_ref.dtype)
        return state

    lax.fori_loop(0, SEQ // (sup * HGRN_GROUP), body, jnp.zeros((LANES, LANES), F32))


def hgrn(hg3, gamma, norm_g):
    batch = hg3.shape[0]
    r = jnp.arange(HGRN_SUPER)
    same = (r[:, None] // HGRN_CHUNK) == (r[None, :] // HGRN_CHUNK)
    tri = (same & (r[:, None] >= r[None, :])).astype(BF16)
    col = lambda which: (lambda b, g: (b, 0, which * N_LANE_TILES + g))
    const2 = lambda b, g: (0, 0)
    return pl.pallas_call(
        _hgrn_kernel,
        grid=(batch, N_LANE_TILES),
        in_specs=[
            pl.BlockSpec((None, SEQ, LANES), col(0)),
            pl.BlockSpec((None, SEQ, LANES), col(1)),
            pl.BlockSpec((None, SEQ, LANES), col(2)),
            pl.BlockSpec((None, SEQ, LANES), col(3)),
            pl.BlockSpec((2, LANES), lambda b, g: (0, g)),
            pl.BlockSpec((1, LANES), lambda b, g: (0, g)),
            pl.BlockSpec((HGRN_SUPER, HGRN_SUPER), const2),
        ],
        out_specs=pl.BlockSpec((None, SEQ, LANES), lambda b, g: (b, 0, g)),
        out_shape=jax.ShapeDtypeStruct((batch, SEQ, WIDTH), BF16),
        compiler_params=_cparams("arbitrary", "arbitrary"),
        name="hgrn2",
    )(hg3, hg3, hg3, hg3, gamma.astype(F32), norm_g.reshape(1, WIDTH).astype(F32), tri)


def _post_mix_kernel(oa_ref, yh_ref, x_ref, ag_ref, wo_ref, g2_ref, wrh_ref, wrl_ref, br_ref, perm_t_ref,
                     x1_ref, h2_ref, lg_ref):
    oa = jnp.concatenate(
        [jnp.concatenate([oa_ref[r, g] for g in range(N_LANE_TILES)], axis=1)
         for r in range(N_RES)], axis=0).astype(F32)
    ms = jnp.mean(oa * oa, axis=-1, keepdims=True)
    ya = (oa * lax.rsqrt(ms + EPS) * ag_ref[...]).astype(BF16)
    ya = jnp.dot(perm_t_ref[...], ya, preferred_element_type=F32).astype(BF16)
    mix = (jnp.dot(ya, wo_ref[:WIDTH, :], preferred_element_type=F32)
           + jnp.dot(yh_ref[...], wo_ref[WIDTH:, :], preferred_element_type=F32))
    x1 = x_ref[...] + mix
    x1_ref[...] = x1
    ms2 = jnp.mean(x1 * x1, axis=-1, keepdims=True)
    h2 = x1 * lax.rsqrt(ms2 + EPS) * g2_ref[...]
    hi = h2.astype(BF16)
    h2_ref[...] = hi
    lo = (h2 - hi.astype(F32)).astype(BF16)
    wrh = wrh_ref[...]
    lg_ref[...] = (jnp.dot(hi, wrh, preferred_element_type=F32)
                   + jnp.dot(lo, wrh, preferred_element_type=F32)
                   + jnp.dot(hi, wrl_ref[...], preferred_element_type=F32)
                   + br_ref[...])


def post_mix(oa, yh, x2, attn_g, w_out_bf16, g2, wr_hi, wr_lo, br):
    n = x2.shape[0]
    tiles_per_b = SEQ // TILE_TOKENS
    row = lambda w: pl.BlockSpec((TILE_TOKENS, w), lambda i: (i, 0))
    const = lambda r, c: pl.BlockSpec((r, c), lambda i: (0, 0))
    return pl.pallas_call(
        _post_mix_kernel,
        grid=(n // TILE_TOKENS,),
        in_specs=[pl.BlockSpec((None, N_RES, N_LANE_TILES, U_PER_TILE, LANES),
                               lambda i: (i // tiles_per_b, 0, 0, i % tiles_per_b, 0)),
                  row(WIDTH), row(D_MODEL),
                  const(1, WIDTH), const(2 * WIDTH, D_MODEL), const(1, D_MODEL),
                  const(D_MODEL, LANES), const(D_MODEL, LANES), const(1, LANES),
                  const(TILE_TOKENS, TILE_TOKENS)],
        out_specs=[row(D_MODEL), row(D_MODEL), row(LANES)],
        out_shape=[jax.ShapeDtypeStruct((n, D_MODEL), F32),
                   jax.ShapeDtypeStruct((n, D_MODEL), BF16),
                   jax.ShapeDtypeStruct((n, LANES), F32)],
        compiler_params=_cparams("arbitrary"),
        name="post_mix",
    )(oa, yh, x2, attn_g.reshape(1, WIDTH), w_out_bf16, g2.reshape(1, D_MODEL), wr_hi, wr_lo, br,
      _tile_permutation().T)


ROUTE_TM = 512
R_E1, R_E2, R_SLOT1, R_SLOT2, R_G1, R_G2 = 0, 1, 2, 3, 4, 5
TOTAL_LANE = 0
RUN_ALIGN = 8
LOCAL_ROWS = 2 * ROUTE_TM + N_EXPERTS * RUN_ALIGN


def _route_kernel(lg_ref, tri_ref, upper_ref, rec_ref, tcarry_ref, trows_ref, toff_ref, cnt_ref, carry_ref):
    i = pl.program_id(0)

    @pl.when(i == 0)
    def _():
        carry_ref[...] = jnp.zeros_like(carry_ref)
        cnt_ref[...] = jnp.zeros_like(cnt_ref)

    lg = lg_ref[...]
    lane = lax.broadcasted_iota(I32, lg.shape, 1)
    lanef = lane.astype(F32)
    big = float(LANES)
    gmask = lane < N_GROUPS
    gl = jnp.where(gmask, lg, NEG)
    gmax = jnp.max(gl, axis=-1, keepdims=True)
    gsel = jnp.min(jnp.where(gmask & (gl == gmax), lanef, big), axis=-1, keepdims=True)
    gsum = jnp.sum(jnp.where(gmask, jnp.exp(gl - gmax), 0.0), axis=-1, keepdims=True)
    w_g = 1.0 / gsum
    lo = EXPERT_LANE0 + EXPERTS_PER_GROUP * gsel
    emask = (lanef >= lo) & (lanef < lo + EXPERTS_PER_GROUP)
    el = jnp.where(emask, lg, NEG)
    v1 = jnp.max(el, axis=-1, keepdims=True)
    i1 = jnp.min(jnp.where(emask & (el == v1), lanef, big), axis=-1, keepdims=True)
    emask2 = emask & (lanef != i1)
    el2 = jnp.where(emask2, lg, NEG)
    v2 = jnp.max(el2, axis=-1, keepdims=True)
    i2 = jnp.min(jnp.where(emask2 & (el2 == v2), lanef, big), axis=-1, keepdims=True)
    e2 = jnp.exp(v2 - v1)
    den = 1.0 / (1.0 + e2)
    g1 = w_g * den
    g2 = w_g * e2 * den
    oh1 = lanef == i1
    oh2 = lanef == i2
    onehot = (oh1 | oh2).astype(BF16)
    before = jnp.dot(tri_ref[...], onehot, preferred_element_type=F32)
    count = jnp.sum(onehot.astype(F32), axis=0, keepdims=True)
    units = jnp.floor((count + (RUN_ALIGN - 1)) * (1.0 / RUN_ALIGN))
    rows = units * RUN_ALIGN
    off = RUN_ALIGN * jnp.dot(jnp.broadcast_to(units, (8, LANES)).astype(BF16), upper_ref[...],
                              preferred_element_type=F32)[0:1]
    slot1 = jnp.sum(jnp.where(oh1, off + before, 0.0), axis=-1, keepdims=True)
    slot2 = jnp.sum(jnp.where(oh2, off + before, 0.0), axis=-1, keepdims=True)
    tcarry_ref[0] = carry_ref[...]
    trows_ref[0] = rows
    total = jnp.sum(rows, axis=-1, keepdims=True)
    toff_ref[0] = jnp.where(lax.broadcasted_iota(I32, (1, LANES), 1) == TOTAL_LANE, total, off)
    carry_ref[...] += rows
    eye = lax.broadcasted_iota(I32, (LANES, LANES), 0) == lax.broadcasted_iota(I32, (LANES, LANES), 1)
    cnt_ref[...] += jnp.sum(jnp.where(eye, rows, 0.0), axis=1, keepdims=True)
    rec = jnp.zeros_like(lg)
    for slot, val in ((R_E1, i1), (R_E2, i2), (R_SLOT1, slot1),
                      (R_SLOT2, slot2), (R_G1, g1), (R_G2, g2)):
        rec = jnp.where(lane == slot, val, rec)
    rec_ref[...] = rec


def route(logits):
    n = logits.shape[0]
    tm = ROUTE_TM
    r = jnp.arange(tm)
    tri = (r[:, None] > r[None, :]).astype(BF16)
    ln = jnp.arange(LANES)
    upper = (ln[:, None] < ln[None, :]).astype(BF16)
    tile_row = pl.BlockSpec((1, 1, LANES), lambda i: (i, 0, 0))
    return pl.pallas_call(
        _route_kernel,
        grid=(n // tm,),
        in_specs=[pl.BlockSpec((tm, LANES), lambda i: (i, 0)),
                  pl.BlockSpec((tm, tm), lambda i: (0, 0)),
                  pl.BlockSpec((LANES, LANES), lambda i: (0, 0))],
        out_specs=[pl.BlockSpec((tm, LANES), lambda i: (i, 0))] + [tile_row] * 3
                  + [pl.BlockSpec((LANES, LANES), lambda i: (0, 0))],
        out_shape=[jax.ShapeDtypeStruct((n, LANES), F32)]
                  + [jax.ShapeDtypeStruct((n // tm, 1, LANES), F32)] * 3
                  + [jax.ShapeDtypeStruct((LANES, LANES), F32)],
        scratch_shapes=[pltpu.VMEM((1, LANES), F32)],
        compiler_params=_cparams("arbitrary"),
        name="route",
    )(logits, tri, upper)


P_START, P_ROWS, P_OFF = 0, 1, 2


def _plan_kernel(carry_ref, rows_ref, off_ref, cnt_ref, plan_ref, ends_ref):
    cnt = cnt_ref[...]
    padded = jnp.floor((cnt + (FFN_BLOCK - 1)) * (1.0 / FFN_BLOCK)) * FFN_BLOCK
    row = lax.broadcasted_iota(I32, cnt.shape, 0)
    col = lax.broadcasted_iota(I32, cnt.shape, 1)
    starts = jnp.sum(jnp.where(row < col, padded, 0.0), axis=0, keepdims=True)
    ends_ref[...] = jnp.sum(jnp.where(row <= col, padded, 0.0), axis=0, keepdims=True)
    plan_ref[P_START] = (carry_ref[...] + starts).astype(I32)
    plan_ref[P_ROWS] = rows_ref[...].astype(I32)
    plan_ref[P_OFF] = off_ref[...].astype(I32)


def run_plan(tabs, counts):
    n_tiles = tabs[0].shape[0]
    flat = [t.reshape(n_tiles, LANES) for t in tabs]
    return pl.pallas_call(
        _plan_kernel,
        out_shape=[jax.ShapeDtypeStruct((3, n_tiles, LANES), I32),
                   jax.ShapeDtypeStruct((1, LANES), F32)],
        compiler_params=pltpu.CompilerParams(vmem_limit_bytes=VMEM_LIMIT),
        name="run_plan",
    )(*flat, counts)


RUN_PIECE = 64
SMALL_PIECES = (32, 16, 8)
TOTAL_PIECES = (1024, 512, 256, 128, 64, 32, 16, 8)
SORT_CHUNK = 256


def _for_each_run_piece(plan_ref, tile, n_tiles, fn):
    plane = n_tiles * LANES

    def per_expert(e, c):
        idx = tile * LANES + EXPERT_LANE0 + e
        start = plan_ref[P_START * plane + idx]
        rows = plan_ref[P_ROWS * plane + idx]
        off = plan_ref[P_OFF * plane + idx]

        def big(j, c2):
            fn(pl.multiple_of(off + j * RUN_PIECE, RUN_ALIGN), pl.multiple_of(start + j * RUN_PIECE, RUN_ALIGN),
               RUN_PIECE)
            return c2
        lax.fori_loop(0, lax.shift_right_logical(rows, 6), big, 0)
        for size in SMALL_PIECES:
            done = jnp.bitwise_and(rows, -2 * size)

            @pl.when(jnp.bitwise_and(rows, size) != 0)
            def _():
                fn(pl.multiple_of(off + done, RUN_ALIGN), pl.multiple_of(start + done, RUN_ALIGN), size)
        return c
    lax.fori_loop(0, N_EXPERTS, per_expert, 0)


def _for_each_total_piece(plan_ref, tile, n_tiles, fn):
    total = plan_ref[P_OFF * n_tiles * LANES + tile * LANES + TOTAL_LANE]
    for size in TOTAL_PIECES:
        @pl.when(jnp.bitwise_and(total, size) != 0)
        def _():
            fn(size)


def _dispatch_kernel(plan_ref, ends_ref, nv_ref, h2_ref, rec_ref, xs_ref, zero_buf, stage, sem, zsem):
    i = pl.program_id(0)
    n_steps = pl.num_programs(0)
    n_blocks = xs_ref.shape[0] // FFN_BLOCK

    @pl.when(i == 0)
    def _():
        zero_buf[...] = jnp.zeros_like(zero_buf)

        def zero_copy(row0):
            return pltpu.make_async_copy(zero_buf, xs_ref.at[pl.ds(row0, FFN_BLOCK)], zsem)

        def seg_end(e):
            return ends_ref[e], ends_ref[e] > jnp.where(e > 0, ends_ref[jnp.maximum(e - 1, 0)], 0)

        def start_e(e, c):
            end, nonempty = seg_end(e)

            @pl.when(nonempty)
            def _():
                zero_copy(pl.multiple_of(end - FFN_BLOCK, FFN_BLOCK)).start()
            return c
        lax.fori_loop(0, N_EXPERTS, start_e, 0)

        def start_b(blk, c):
            zero_copy(pl.multiple_of(blk * FFN_BLOCK, FFN_BLOCK)).start()
            return c
        lax.fori_loop(nv_ref[0], n_blocks, start_b, 0)

        def wait_e(e, c):
            _, nonempty = seg_end(e)

            @pl.when(nonempty)
            def _():
                zero_copy(0).wait()
            return c
        lax.fori_loop(0, N_EXPERTS, wait_e, 0)

        def wait_b(blk, c):
            zero_copy(0).wait()
            return c
        lax.fori_loop(nv_ref[0], n_blocks, wait_b, 0)

    slot = i % 2
    rec = rec_ref[...]
    s1 = rec[:, R_SLOT1:R_SLOT1 + 1]
    s2 = rec[:, R_SLOT2:R_SLOT2 + 1]
    half = PACKED_W
    for c in range(LOCAL_ROWS // SORT_CHUNK):
        lanef = (lax.broadcasted_iota(I32, (ROUTE_TM, SORT_CHUNK), 1) + c * SORT_CHUNK).astype(F32)
        sel_t = ((lanef == s1) | (lanef == s2)).astype(BF16)
        pick = lambda cols: lax.dot_general(sel_t, h2_ref[:, cols], (((0,), (0,)), ((), ())),
                                            preferred_element_type=F32)
        stage[slot, c * SORT_CHUNK:(c + 1) * SORT_CHUNK, :] = _pack_pair(pick(slice(0, half)),
                                                                         pick(slice(half, 2 * half)))

    def start_piece(lrow, grow, size):
        pltpu.make_async_copy(stage.at[slot, pl.ds(lrow, size)], xs_ref.at[pl.ds(grow, size)],
                              sem.at[slot]).start()
    _for_each_run_piece(plan_ref, i, n_steps, start_piece)

    def wait_pieces(tile, s):
        def wait_piece(size):
            pltpu.make_async_copy(stage.at[s, pl.ds(0, size)], xs_ref.at[pl.ds(0, size)], sem.at[s]).wait()
        _for_each_total_piece(plan_ref, tile, n_steps, wait_piece)

    @pl.when(i > 0)
    def _():
        wait_pieces(i - 1, 1 - slot)

    @pl.when(i + 1 == n_steps)
    def _():
        wait_pieces(i, slot)


def dispatch(h2, rec, plan, seg_ends, n_valid, n_rows):
    n = h2.shape[0]
    return pl.pallas_call(
        _dispatch_kernel,
        grid_spec=pltpu.PrefetchScalarGridSpec(
            num_scalar_prefetch=3,
            grid=(n // ROUTE_TM,),
            in_specs=[pl.BlockSpec((ROUTE_TM, D_MODEL), lambda i, *_: (i, 0)),
                      pl.BlockSpec((ROUTE_TM, LANES), lambda i, *_: (i, 0))],
            out_specs=pl.BlockSpec(memory_space=pl.ANY),
            scratch_shapes=[pltpu.VMEM((FFN_BLOCK, PACKED_W), U32),
                            pltpu.VMEM((2, LOCAL_ROWS, PACKED_W), U32),
                            pltpu.SemaphoreType.DMA((2,)),
                            pltpu.SemaphoreType.DMA(())],
        ),
        out_shape=jax.ShapeDtypeStruct((n_rows, PACKED_W), U32),
        compiler_params=_cparams("arbitrary"),
        name="dispatch",
    )(plan, seg_ends, n_valid, h2, rec)


def _ffn_kernel(be_ref, nv_ref, xs_ref, wg_ref, wu_ref, wd_ref, ys_ref, wg_sc, wu_sc, wd_sc):
    i = pl.program_id(0)
    prev = be_ref[jnp.maximum(i - 1, 0)]
    fresh = (i == 0) | (be_ref[i] != prev)

    @pl.when(fresh)
    def _():
        wg_sc[...] = wg_ref[0].astype(BF16)
        wu_sc[...] = wu_ref[0].astype(BF16)
        wd_sc[...] = wd_ref[0].astype(BF16)

    @pl.when(i < nv_ref[0])
    def _():
        xa, xb = (t.astype(BF16) for t in _unpack_halves(xs_ref[...]))
        half = PACKED_W
        gate = (jnp.dot(xa, wg_sc[:half, :], preferred_element_type=F32)
                + jnp.dot(xb, wg_sc[half:, :], preferred_element_type=F32))
        up = (jnp.dot(xa, wu_sc[:half, :], preferred_element_type=F32)
              + jnp.dot(xb, wu_sc[half:, :], preferred_element_type=F32))
        hid = (gate * (1.0 / (1.0 + jnp.exp(-gate))) * up).astype(BF16)
        ys_ref[...] = _pack_halves(jnp.dot(hid, wd_sc[...], preferred_element_type=F32))

    @pl.when(i >= nv_ref[0])
    def _():
        ys_ref[...] = jnp.zeros_like(ys_ref)


def expert_ffn(xs, block_expert, n_valid, w_gate, w_up, w_down):
    n_rows = xs.shape[0]
    n_blocks = n_rows // FFN_BLOCK
    wmap = lambda i, be, nv: (be[i], 0, 0)
    return pl.pallas_call(
        _ffn_kernel,
        grid_spec=pltpu.PrefetchScalarGridSpec(
            num_scalar_prefetch=2,
            grid=(n_blocks,),
            in_specs=[pl.BlockSpec((FFN_BLOCK, PACKED_W), lambda i, be, nv: (jnp.minimum(i, nv[0] - 1), 0)),
                      pl.BlockSpec((1, D_MODEL, D_EXPERT), wmap),
                      pl.BlockSpec((1, D_MODEL, D_EXPERT), wmap),
                      pl.BlockSpec((1, D_EXPERT, D_MODEL), wmap)],
            out_specs=pl.BlockSpec((FFN_BLOCK, PACKED_W), lambda i, be, nv: (i, 0)),
            scratch_shapes=[pltpu.VMEM((D_MODEL, D_EXPERT), BF16),
                            pltpu.VMEM((D_MODEL, D_EXPERT), BF16),
                            pltpu.VMEM((D_EXPERT, D_MODEL), BF16)],
        ),
        out_shape=jax.ShapeDtypeStruct((n_rows, PACKED_W), U32),
        compiler_params=_cparams("arbitrary"),
        name="expert_ffn",
    )(block_expert, n_valid, xs, w_gate, w_up, w_down)


def _combine_kernel(plan_ref, x1_ref, rec_ref, gf_ref, ys_ref, out_ref, buf, sem):
    i = pl.program_id(0)
    n_steps = pl.num_programs(0)
    slot = i % 2

    def fetch_tile(tile, s):
        def start_piece(lrow, grow, size):
            pltpu.make_async_copy(ys_ref.at[pl.ds(grow, size)], buf.at[s, pl.ds(lrow, size)], sem.at[s]).start()
        _for_each_run_piece(plan_ref, tile, n_steps, start_piece)

    @pl.when(i == 0)
    def _():
        buf[...] = jnp.zeros_like(buf)
        fetch_tile(0, 0)

    @pl.when(i + 1 < n_steps)
    def _():
        fetch_tile(i + 1, 1 - slot)

    def wait_piece(size):
        pltpu.make_async_copy(ys_ref.at[pl.ds(0, size)], buf.at[slot, pl.ds(0, size)], sem.at[slot]).wait()
    _for_each_total_piece(plan_ref, i, n_steps, wait_piece)

    rec = rec_ref[...]
    lanef = lax.broadcasted_iota(I32, (ROUTE_TM, LOCAL_ROWS), 1).astype(F32)
    sel = (jnp.where(lanef == rec[:, R_SLOT1:R_SLOT1 + 1], rec[:, R_G1:R_G1 + 1], 0.0)
           + jnp.where(lanef == rec[:, R_SLOT2:R_SLOT2 + 1], rec[:, R_G2:R_G2 + 1], 0.0)).astype(BF16)
    ya, yb = (t.astype(BF16) for t in _unpack_halves(buf[slot]))
    half = PACKED_W
    xa = x1_ref[:, :half] + jnp.dot(sel, ya, preferred_element_type=F32)
    xb = x1_ref[:, half:] + jnp.dot(sel, yb, preferred_element_type=F32)
    ms = (jnp.sum(xa * xa, axis=-1, keepdims=True) + jnp.sum(xb * xb, axis=-1, keepdims=True)) * (1.0 / D_MODEL)
    scale = lax.rsqrt(ms + EPS)
    out_ref[:, :half] = xa * scale * gf_ref[:, :half]
    out_ref[:, half:] = xb * scale * gf_ref[:, half:]


def combine(x1, rec, norm_f_g, ys, plan):
    n = x1.shape[0]
    tm = ROUTE_TM
    return pl.pallas_call(
        _combine_kernel,
        grid_spec=pltpu.PrefetchScalarGridSpec(
            num_scalar_prefetch=1,
            grid=(n // tm,),
            in_specs=[pl.BlockSpec((tm, D_MODEL), lambda i, p: (i, 0)),
                      pl.BlockSpec((tm, LANES), lambda i, p: (i, 0)),
                      pl.BlockSpec((1, D_MODEL), lambda i, p: (0, 0)),
                      pl.BlockSpec(memory_space=pl.ANY)],
            out_specs=pl.BlockSpec((tm, D_MODEL), lambda i, p: (i, 0)),
            scratch_shapes=[pltpu.VMEM((2, LOCAL_ROWS, PACKED_W), U32),
                            pltpu.SemaphoreType.DMA((2,))],
        ),
        out_shape=jax.ShapeDtypeStruct((n, D_MODEL), F32),
        compiler_params=_cparams("arbitrary"),
        name="combine",
    )(plan, x1, rec, norm_f_g.reshape(1, D_MODEL), ys)


def _router_weights(w_group, b_group, w_router, b_router):
    w = jnp.zeros((D_MODEL, LANES), F32)
    w = w.at[:, :N_GROUPS].set(w_group).at[:, EXPERT_LANE0:EXPERT_LANE0 + N_EXPERTS].set(w_router)
    b = jnp.zeros((1, LANES), F32)
    b = b.at[0, :N_GROUPS].set(b_group).at[0, EXPERT_LANE0:EXPERT_LANE0 + N_EXPERTS].set(b_router)
    hi = w.astype(BF16)
    lo = (w - hi.astype(F32)).astype(BF16)
    return hi, lo, b


def _sorted_rows_bound(n_tokens):
    worst = (2 * n_tokens + (n_tokens // ROUTE_TM) * N_EXPERTS * (RUN_ALIGN - 1)
             + N_EXPERTS * (FFN_BLOCK - 1))
    return -(-worst // FFN_BLOCK) * FFN_BLOCK


def _block_plan(ends_row, n_rows):
    seg_ends = ends_row[0, EXPERT_LANE0:EXPERT_LANE0 + N_EXPERTS].astype(I32)
    n_blocks = n_rows // FFN_BLOCK
    blk_start = jnp.arange(n_blocks, dtype=I32) * FFN_BLOCK
    block_expert = jnp.minimum(jnp.sum((seg_ends[None, :] <= blk_start[:, None]).astype(I32), axis=1),
                               N_EXPERTS - 1)
    n_valid = seg_ends[-1:] // FFN_BLOCK
    return seg_ends, block_expert, n_valid


def kernel(x, norm1_g, w_in, attn_norm_g, hgrn_gamma, hgrn_norm_g, w_out, norm2_g, w_group, b_group,
           w_router, b_router, w_gate, w_up, w_down, norm_f_g):
    batch, seq, d = x.shape
    assert seq == SEQ and d == D_MODEL and norm1_g.shape[0] == 1
    n = batch * seq
    x2 = x.reshape(n, d)
    qkv, hg = in_proj(x2, norm1_g[0], w_in[0].astype(BF16), batch)
    oa = attention(qkv)
    yh = hgrn(hg.reshape(batch, seq, HG_W), hgrn_gamma, hgrn_norm_g[0]).reshape(n, WIDTH)
    wr_hi, wr_lo, br = _router_weights(w_group[0], b_group[0], w_router[0], b_router[0])
    x1, h2, logits = post_mix(oa, yh, x2, attn_norm_g[0], w_out[0].astype(BF16), norm2_g[0],
                              wr_hi, wr_lo, br)
    rec, tcarry, trows, toff, counts = route(logits)
    plan, ends_row = run_plan((tcarry, trows, toff), counts)
    plan = plan.reshape(-1)
    n_rows = _sorted_rows_bound(n)
    seg_ends, block_expert, n_valid = _block_plan(ends_row, n_rows)
    xs = dispatch(h2, rec, plan, seg_ends, n_valid, n_rows)
    ys = expert_ffn(xs, block_expert, n_valid, w_gate[0], w_up[0], w_down[0])
    return combine(x1, rec, norm_f_g, ys, plan).reshape(batch, seq, d)
```

```python
import jax
import jax.numpy as jnp
from jax import lax
from jax.experimental import pallas as pl
from jax.experimental.pallas import tpu as pltpu

F32 = jnp.float32
BF16 = jnp.bfloat16
I32 = jnp.int32
U32 = jnp.uint32

D_MODEL = 1024
HEAD_DIM = 64
N_HEADS = 8
WIDTH = N_HEADS * HEAD_DIM
QKV_W = 3 * WIDTH
HG_W = 4 * WIDTH
SEQ = 2048
ATTN_BLOCK = 128
DILATIONS = (1, 4, 16)
N_RES = 16
RES_LEN = SEQ // N_RES
TILE_TOKENS = 512
U_PER_TILE = TILE_TOKENS // N_RES
HGRN_CHUNK = 32
HGRN_SUPER = 256
HGRN_GROUP = 4
N_GROUPS = 4
EXPERTS_PER_GROUP = 8
N_EXPERTS = 32
D_EXPERT = 512
EXPERT_LANE0 = 32
FFN_BLOCK = 512
PACKED_W = D_MODEL // 2
EPS = 1e-6
NEG = -1e30
LOG2E = 1.4426950408889634
LANES = 128
VMEM_LIMIT = 56 * 1024 * 1024


def _cparams(*sem):
    return pltpu.CompilerParams(dimension_semantics=sem, vmem_limit_bytes=VMEM_LIMIT)


def _pack_pair(a, b):
    hi = pltpu.bitcast(a.astype(BF16).astype(F32), U32)
    lo = pltpu.bitcast(b.astype(BF16).astype(F32), U32)
    return hi | (lo >> 16)


def _pack_halves(x):
    w = x.shape[1] // 2
    return _pack_pair(x[:, :w], x[:, w:])


def _unpack_halves(p):
    return pltpu.bitcast(p & jnp.uint32(0xFFFF0000), F32), pltpu.bitcast(p << 16, F32)


N_LANE_TILES = WIDTH // LANES


def _tile_permutation():
    i = jnp.arange(TILE_TOKENS)
    src = N_RES * (i % U_PER_TILE) + i // U_PER_TILE
    return (src[:, None] == jnp.arange(TILE_TOKENS)[None, :]).astype(BF16)


def _in_proj_kernel(x_ref, g_ref, w_ref, perm_ref, qkv_ref, hg_ref):
    x = x_ref[...]
    ms = jnp.mean(x * x, axis=-1, keepdims=True)
    h = (x * lax.rsqrt(ms + EPS) * g_ref[...]).astype(BF16)
    hp = jnp.dot(perm_ref[...], h, preferred_element_type=F32).astype(BF16)
    for c in range(QKV_W // WIDTH):
        res = jnp.dot(hp, w_ref[:, c * WIDTH:(c + 1) * WIDTH], preferred_element_type=F32)
        for r in range(N_RES):
            for g in range(N_LANE_TILES):
                qkv_ref[r, g, c] = res[r * U_PER_TILE:(r + 1) * U_PER_TILE, g * LANES:(g + 1) * LANES]
    for j in range(HG_W // WIDTH):
        wsl = slice(QKV_W + j * WIDTH, QKV_W + (j + 1) * WIDTH)
        hg_ref[:, j * WIDTH:(j + 1) * WIDTH] = jnp.dot(h, w_ref[:, wsl], preferred_element_type=F32).astype(BF16)


def in_proj(x2, g, w_bf16, batch):
    n = x2.shape[0]
    tiles_per_b = SEQ // TILE_TOKENS
    return pl.pallas_call(
        _in_proj_kernel,
        grid=(n // TILE_TOKENS,),
        in_specs=[
            pl.BlockSpec((TILE_TOKENS, D_MODEL), lambda i: (i, 0)),
            pl.BlockSpec((1, D_MODEL), lambda i: (0, 0)),
            pl.BlockSpec((D_MODEL, QKV_W + HG_W), lambda i: (0, 0)),
            pl.BlockSpec((TILE_TOKENS, TILE_TOKENS), lambda i: (0, 0)),
        ],
        out_specs=[
            pl.BlockSpec((None, N_RES, N_LANE_TILES, 3, U_PER_TILE, LANES),
                         lambda i: (i // tiles_per_b, 0, 0, 0, i % tiles_per_b, 0)),
            pl.BlockSpec((TILE_TOKENS, HG_W), lambda i: (i, 0)),
        ],
        out_shape=[
            jax.ShapeDtypeStruct((batch, N_RES, N_LANE_TILES, 3, RES_LEN, LANES), F32),
            jax.ShapeDtypeStruct((n, HG_W), BF16),
        ],
        compiler_params=_cparams("arbitrary"),
        name="in_proj",
    )(x2, g.reshape(1, D_MODEL), w_bf16, _tile_permutation())


ATTN_GROUP16 = 8
ATTN_GROUP1 = 5
HEADS_PER_TILE = LANES // HEAD_DIM


def _attn_bias_tables():
    slopes = 2.0 ** (-8.0 * jnp.arange(1, N_HEADS + 1, dtype=F32) / N_HEADS)

    def table(qpos, kpos, dil):
        sd = qpos[:, None] - kpos[None, :]
        b = -slopes[:, None, None] * (sd * dil).astype(F32)[None] * LOG2E
        return jnp.where(((sd >= 0) & (sd <= ATTN_BLOCK))[None], b, NEG)

    q = jnp.arange(ATTN_BLOCK)
    k = jnp.arange(2 * ATTN_BLOCK)
    q1 = 16 * (q % 8) + q // 8 + ATTN_BLOCK
    k1 = 16 * (k % 16) + k // 16
    q4 = 4 * (q % 32) + q // 32 + ATTN_BLOCK
    k4 = 4 * (k % 64) + k // 64
    q16 = q + ATTN_BLOCK
    none = jnp.full((N_HEADS, ATTN_BLOCK, 2 * ATTN_BLOCK), NEG, F32)
    tabs = [jnp.concatenate([table(q1, k1, 1), table(q1, q1, 1)], axis=-1),
            jnp.concatenate([table(q4, k4, 4), table(q4, q4, 4)], axis=-1),
            jnp.concatenate([none, table(q16, q16, 16)], axis=-1)]
    return jnp.stack(tabs, axis=0)


def _attn_bias(bias_ref, d_idx, with_prev):
    ksl = slice(0, 2 * ATTN_BLOCK) if with_prev else slice(2 * ATTN_BLOCK, 3 * ATTN_BLOCK)
    return jnp.concatenate([bias_ref[d_idx, h, :, ksl] for h in range(HEADS_PER_TILE)], axis=0)


def _attn_blocks(blocks):
    nb = ATTN_BLOCK
    lane = lax.broadcasted_iota(I32, (nb, LANES), 1)
    head0 = lane < HEAD_DIM
    biases = [blk[3] for blk in blocks]
    blocks = [blk[:3] for blk in blocks]
    qs = [q * (HEAD_DIM ** -0.5 * LOG2E) for q, _, _ in blocks]
    qq = [jnp.concatenate([jnp.where(head0, q, 0.0), jnp.where(head0, 0.0, q)], axis=0).astype(BF16)
          for q in qs]
    ks = [k.astype(BF16) for _, k, _ in blocks]
    vs = [v.astype(BF16) for _, _, v in blocks]
    v_ones = [jnp.concatenate([v, jnp.ones_like(v)], axis=-1) for v in vs]
    s = [lax.dot_general(q2, k, (((1,), (1,)), ((), ())), preferred_element_type=F32) + bias
         for q2, k, bias in zip(qq, ks, biases)]
    m = [jnp.max(x, axis=-1, keepdims=True) for x in s]
    p = [jnp.exp2(x - mx).astype(BF16) for x, mx in zip(s, m)]
    acc = [jnp.dot(px, vo, preferred_element_type=F32) for px, vo in zip(p, v_ones)]
    head0_wide = jnp.concatenate([head0, head0], axis=-1)
    outs = []
    for a, mx in zip(acc, m):
        ol = jnp.where(head0_wide, a[:nb], a[nb:])
        l = ol[:, LANES:]
        outs.append((ol[:, :LANES] * (1.0 / l), jnp.where(head0, mx[:nb], mx[nb:]) + jnp.log2(l)))
    return outs


def _attn_kernel(qkv_ref, bias_ref, o_ref, o4_ref, l4_ref, o16_ref, l16_ref, fin_ref):
    grp = ATTN_GROUP16

    def body16(j, c):
        rs = [j * grp + a for a in range(grp)]
        bias = _attn_bias(bias_ref, 2, False)
        res = _attn_blocks([(qkv_ref[r, 0], qkv_ref[r, 1], qkv_ref[r, 2], bias) for r in rs])
        for r, (o, l) in zip(rs, res):
            o16_ref[r] = o
            l16_ref[r] = l
        return c
    lax.fori_loop(0, N_RES // grp, body16, 0)

    def gather4(c, rho, u0, nu):
        return jnp.concatenate([qkv_ref[rho + 4 * a, c, pl.ds(u0, nu), :] for a in range(4)], axis=0)

    def store4(rho, u0, o, l):
        for a in range(4):
            o4_ref[rho + 4 * a, pl.ds(u0, 32), :] = o[a * 32:(a + 1) * 32]
            l4_ref[rho + 4 * a, pl.ds(u0, 32), :] = l[a * 32:(a + 1) * 32]

    def body4(j, c):
        bias = _attn_bias(bias_ref, 1, True)
        todo = [(2 * j + a, 32 * n) for a in range(2) for n in range(1, 4)]
        res = _attn_blocks([(gather4(0, rho, u0, 32), gather4(1, rho, u0 - 32, 64),
                             gather4(2, rho, u0 - 32, 64), bias) for rho, u0 in todo])
        for (rho, u0), (o, l) in zip(todo, res):
            store4(rho, u0, o, l)
        return c
    lax.fori_loop(0, 2, body4, 0)

    def gather1(ref, u0, nu, *lead):
        return jnp.concatenate([ref[(r,) + lead + (pl.ds(u0, nu), slice(None))] for r in range(N_RES)],
                               axis=0)

    def merge_and_store(u0, o1, l1):
        l4 = gather1(l4_ref, u0, 8)
        l16 = gather1(l16_ref, u0, 8)
        mx = jnp.maximum(jnp.maximum(l1, l4), l16)
        e1 = jnp.exp2(l1 - mx)
        e4 = jnp.exp2(l4 - mx)
        e16 = jnp.exp2(l16 - mx)
        inv = 1.0 / (e1 + e4 + e16)
        o = (e1 * inv) * o1 + (e4 * inv) * gather1(o4_ref, u0, 8) + (e16 * inv) * gather1(o16_ref, u0, 8)
        for r in range(N_RES):
            fin_ref[r, pl.ds(u0, 8), :] = o[r * 8:(r + 1) * 8]

    bias4 = _attn_bias(bias_ref, 1, False)
    first = [tuple(gather4(c, rho, 0, 32) for c in range(3)) + (bias4,) for rho in range(4)]
    first.append(tuple(gather1(qkv_ref, 0, 8, c) for c in range(3)) + (_attn_bias(bias_ref, 0, False),))
    res = _attn_blocks(first)
    for rho in range(4):
        store4(rho, 0, *res[rho])
    merge_and_store(0, *res[4])

    n_blocks1 = RES_LEN // 8 - 1

    def body1(j, c):
        u0s = [pl.multiple_of((1 + j * ATTN_GROUP1 + a) * 8, 8) for a in range(ATTN_GROUP1)]
        bias = _attn_bias(bias_ref, 0, True)
        res = _attn_blocks([(gather1(qkv_ref, u0, 8, 0), gather1(qkv_ref, u0 - 8, 16, 1),
                             gather1(qkv_ref, u0 - 8, 16, 2), bias) for u0 in u0s])
        for u0, (o, l) in zip(u0s, res):
            merge_and_store(u0, o, l)
        return c
    lax.fori_loop(0, n_blocks1 // ATTN_GROUP1, body1, 0)

    def emit(r, c):
        o_ref[r] = fin_ref[r].astype(o_ref.dtype)
        return c
    lax.fori_loop(0, N_RES, emit, 0)


def attention(qkv):
    batch = qkv.shape[0]
    scratch = pltpu.VMEM((N_RES, RES_LEN, LANES), F32)
    return pl.pallas_call(
        _attn_kernel,
        grid=(batch, N_LANE_TILES),
        in_specs=[
            pl.BlockSpec((None, N_RES, None, 3, RES_LEN, LANES), lambda b, g: (b, 0, g, 0, 0, 0)),
            pl.BlockSpec((len(DILATIONS), HEADS_PER_TILE, ATTN_BLOCK, 3 * ATTN_BLOCK),
                         lambda b, g: (0, g, 0, 0)),
        ],
        out_specs=pl.BlockSpec((None, N_RES, None, RES_LEN, LANES), lambda b, g: (b, 0, g, 0, 0)),
        out_shape=jax.ShapeDtypeStruct((batch, N_RES, N_LANE_TILES, RES_LEN, LANES), BF16),
        scratch_shapes=[scratch] * 5,
        compiler_params=_cparams("arbitrary", "arbitrary"),
        name="dilated_attention",
    )(qkv, _attn_bias_tables())


def _split3(x):
    a = x.astype(BF16)
    r = x - a.astype(F32)
    b = r.astype(BF16)
    c = (r - b.astype(F32)).astype(BF16)
    return a, b, c


def _hgrn_kernel(q_ref, f_ref, i_ref, g_ref, gamma_ref, ng_ref, tri_ref, y_ref):
    sup, ch = HGRN_SUPER, HGRN_CHUNK
    n_ch = sup // ch
    gam = gamma_ref[...]
    gmx = jnp.max(gam, axis=0, keepdims=True)
    ge = jnp.exp(gam - gmx)
    lb = ge[0:1] / jnp.sum(ge, axis=0, keepdims=True)
    ng = ng_ref[...]
    tri = tri_ref[...]
    ri = lax.broadcasted_iota(I32, (2 * sup, sup), 0) % sup
    ci = lax.broadcasted_iota(I32, (2 * sup, sup), 1)
    causal_bd2 = (ri // ch == ci // ch) & (ri >= ci)
    head0 = lax.broadcasted_iota(I32, (sup, LANES), 1) < HEAD_DIM
    row_chunk = lax.broadcasted_iota(I32, (sup, LANES), 0) // ch
    same_head = (lax.broadcasted_iota(I32, (LANES, LANES), 0) // HEAD_DIM
                 == lax.broadcasted_iota(I32, (LANES, LANES), 1) // HEAD_DIM)

    def sigmoid(x):
        return 1.0 / (1.0 + jnp.exp(-x))

    def dot_nt(a, b):
        return lax.dot_general(a, b, (((1,), (1,)), ((), ())), preferred_element_type=F32)

    def dot_tn(a, b):
        return lax.dot_general(a, b, (((0,), (0,)), ((), ())), preferred_element_type=F32)

    def expand(x):
        zero = jnp.zeros_like(x)
        return jnp.concatenate([jnp.where(row_chunk == c, x, zero) for c in range(n_ch)], axis=1)

    def body(j, state):
        grp = range(HGRN_GROUP)
        rows = [pl.ds(pl.multiple_of((j * HGRN_GROUP + a) * sup, sup), sup) for a in grp]
        fr = [f_ref[r, :].astype(F32) for r in rows]
        t = [jnp.exp(-jnp.abs(x)) for x in fr]
        rcp = [1.0 / (1.0 + x) for x in t]
        sig = [jnp.where(f >= 0, r, x * r) for f, r, x in zip(fr, rcp, t)]
        nsig = [jnp.where(f >= 0, x * r, r) for f, r, x in zip(fr, rcp, t)]
        logf = [jnp.log(lb + (1.0 - lb) * x) for x in sig]
        key = [(1.0 - lb) * x for x in nsig]
        parts = [_split3(x) for x in logf]
        b = [sum(jnp.dot(tri, p, preferred_element_type=F32) for p in ps) for ps in parts]
        totals = [[x[(c + 1) * ch - 1:(c + 1) * ch, :] for c in range(n_ch)] for x in b]
        b_last = [jnp.concatenate([jnp.broadcast_to(t_, (ch, LANES)) for t_ in ts], axis=0) for ts in totals]
        pad = jnp.zeros((LANES - n_ch, LANES), F32)
        decay_t = [jnp.exp(jnp.transpose(jnp.concatenate(ts + [pad], axis=0))) for ts in totals]
        qv = [q_ref[r, :].astype(F32) for r in rows]
        q_in = [(x * sigmoid(x) * jnp.exp(bb)).astype(BF16) for x, bb in zip(qv, b)]
        k_in = [(k * jnp.exp(-bb)).astype(BF16) for k, bb in zip(key, b)]
        k_end = [(k * jnp.exp(bl - bb)).astype(BF16) for k, bl, bb in zip(key, b_last, b)]
        vv = [i_ref[r, :] for r in rows]
        q2 = [jnp.concatenate([jnp.where(head0, x, jnp.zeros_like(x)), jnp.where(head0, jnp.zeros_like(x), x)],
                              axis=0) for x in q_in]
        att = [jnp.where(causal_bd2, dot_nt(x, k), 0.0).astype(BF16) for x, k in zip(q2, k_in)]
        intra = [jnp.dot(a_, v, preferred_element_type=F32) for a_, v in zip(att, vv)]
        o_intra = [jnp.where(head0, x[:sup], x[sup:]) for x in intra]
        ds_all = [dot_tn(k, expand(v)) for k, v in zip(k_end, vv)]
        s_stack = []
        for a in grp:
            s_prev = []
            for c in range(n_ch):
                s_prev.append(state.astype(BF16))
                ds = jnp.where(same_head, ds_all[a][:, c * LANES:(c + 1) * LANES], 0.0)
                state = decay_t[a][:, c:c + 1] * state + ds
            s_stack.append(jnp.concatenate(s_prev, axis=0))
        o_inter = [jnp.dot(expand(x), s_, preferred_element_type=F32) for x, s_ in zip(q_in, s_stack)]
        o = [x + y for x, y in zip(o_intra, o_inter)]
        sq = [x * x for x in o]
        ss0 = [jnp.sum(jnp.where(head0, x, 0.0), axis=-1, keepdims=True) for x in sq]
        ss1 = [jnp.sum(jnp.where(head0, 0.0, x), axis=-1, keepdims=True) for x in sq]
        ms = [jnp.where(head0, x, y) * (1.0 / HEAD_DIM) for x, y in zip(ss0, ss1)]
        gv = [g_ref[r, :].astype(F32) for r in rows]
        for r, x, m_, g_ in zip(rows, o, ms, gv):
            y_ref[r, :] = (x * lax.rsqrt(m_ + EPS) * ng * (g_ * sigmoid(g_))).astype(y_ref.dtype)
        return state

    lax.fori_loop(0, SEQ // (sup * HGRN_GROUP), body, jnp.zeros((LANES, LANES), F32))


def hgrn(hg3, gamma, norm_g):
    batch = hg3.shape[0]
    r = jnp.arange(HGRN_SUPER)
    same = (r[:, None] // HGRN_CHUNK) == (r[None, :] // HGRN_CHUNK)
    tri = (same & (r[:, None] >= r[None, :])).astype(BF16)
    col = lambda which: (lambda b, g: (b, 0, which * N_LANE_TILES + g))
    const2 = lambda b, g: (0, 0)
    return pl.pallas_call(
        _hgrn_kernel,
        grid=(batch, N_LANE_TILES),
        in_specs=[
            pl.BlockSpec((None, SEQ, LANES), col(0)),
            pl.BlockSpec((None, SEQ, LANES), col(1)),
            pl.BlockSpec((None, SEQ, LANES), col(2)),
            pl.BlockSpec((None, SEQ, LANES), col(3)),
            pl.BlockSpec((2, LANES), lambda b, g: (0, g)),
            pl.BlockSpec((1, LANES), lambda b, g: (0, g)),
            pl.BlockSpec((HGRN_SUPER, HGRN_SUPER), const2),
        ],
        out_specs=pl.BlockSpec((None, SEQ, LANES), lambda b, g: (b, 0, g)),
        out_shape=jax.ShapeDtypeStruct((batch, SEQ, WIDTH), BF16),
        compiler_params=_cparams("arbitrary", "arbitrary"),
        name="hgrn2",
    )(hg3, hg3, hg3, hg3, gamma.astype(F32), norm_g.reshape(1, WIDTH).astype(F32), tri)


def _post_mix_kernel(oa_ref, yh_ref, x_ref, ag_ref, wo_ref, g2_ref, wrh_ref, wrl_ref, br_ref, perm_t_ref,
                     tri_ref, upper_ref,
                     x1_ref, h2_ref, rec_ref, tcarry_ref, trows_ref, toff_ref, cnt_ref, carry_ref):
    oa = jnp.concatenate(
        [jnp.concatenate([oa_ref[r, g] for g in range(N_LANE_TILES)], axis=1)
         for r in range(N_RES)], axis=0).astype(F32)
    ms = jnp.mean(oa * oa, axis=-1, keepdims=True)
    ya = (oa * lax.rsqrt(ms + EPS) * ag_ref[...]).astype(BF16)
    ya = jnp.dot(perm_t_ref[...], ya, preferred_element_type=F32).astype(BF16)
    mix = (jnp.dot(ya, wo_ref[:WIDTH, :], preferred_element_type=F32)
           + jnp.dot(yh_ref[...], wo_ref[WIDTH:, :], preferred_element_type=F32))
    x1 = x_ref[...] + mix
    x1_ref[...] = x1
    ms2 = jnp.mean(x1 * x1, axis=-1, keepdims=True)
    h2 = x1 * lax.rsqrt(ms2 + EPS) * g2_ref[...]
    hi = h2.astype(BF16)
    h2_ref[...] = hi
    lo = (h2 - hi.astype(F32)).astype(BF16)
    wrh = wrh_ref[...]
    logits = (jnp.dot(hi, wrh, preferred_element_type=F32)
              + jnp.dot(lo, wrh, preferred_element_type=F32)
              + jnp.dot(hi, wrl_ref[...], preferred_element_type=F32)
              + br_ref[...])
    _route_tile(logits, tri_ref, upper_ref, rec_ref, tcarry_ref, trows_ref, toff_ref, cnt_ref, carry_ref)


def post_mix(oa, yh, x2, attn_g, w_out_bf16, g2, wr_hi, wr_lo, br):
    n = x2.shape[0]
    tiles_per_b = SEQ // TILE_TOKENS
    assert ROUTE_TM == TILE_TOKENS
    row = lambda w: pl.BlockSpec((TILE_TOKENS, w), lambda i: (i, 0))
    const = lambda r, c: pl.BlockSpec((r, c), lambda i: (0, 0))
    tile_row = pl.BlockSpec((1, 1, LANES), lambda i: (i, 0, 0))
    return pl.pallas_call(
        _post_mix_kernel,
        grid=(n // TILE_TOKENS,),
        in_specs=[pl.BlockSpec((None, N_RES, N_LANE_TILES, U_PER_TILE, LANES),
                               lambda i: (i // tiles_per_b, 0, 0, i % tiles_per_b, 0)),
                  row(WIDTH), row(D_MODEL),
                  const(1, WIDTH), const(2 * WIDTH, D_MODEL), const(1, D_MODEL),
                  const(D_MODEL, LANES), const(D_MODEL, LANES), const(1, LANES),
                  const(TILE_TOKENS, TILE_TOKENS), const(ROUTE_TM, ROUTE_TM), const(LANES, LANES)],
        out_specs=[row(D_MODEL), row(D_MODEL), row(LANES)] + [tile_row] * 3 + [const(LANES, LANES)],
        out_shape=[jax.ShapeDtypeStruct((n, D_MODEL), F32),
                   jax.ShapeDtypeStruct((n, D_MODEL), BF16),
                   jax.ShapeDtypeStruct((n, LANES), F32)]
                  + [jax.ShapeDtypeStruct((n // ROUTE_TM, 1, LANES), F32)] * 3
                  + [jax.ShapeDtypeStruct((LANES, LANES), F32)],
        scratch_shapes=[pltpu.VMEM((1, LANES), F32)],
        compiler_params=_cparams("arbitrary"),
        name="post_mix",
    )(oa, yh, x2, attn_g.reshape(1, WIDTH), w_out_bf16, g2.reshape(1, D_MODEL), wr_hi, wr_lo, br,
      _tile_permutation().T, *_route_constants())


ROUTE_TM = 512
R_E1, R_E2, R_SLOT1, R_SLOT2, R_G1, R_G2 = 0, 1, 2, 3, 4, 5
TOTAL_LANE = 0
RUN_ALIGN = 8
LOCAL_ROWS = 2 * ROUTE_TM + N_EXPERTS * RUN_ALIGN


def _route_tile(lg, tri_ref, upper_ref, rec_ref, tcarry_ref, trows_ref, toff_ref, cnt_ref, carry_ref):
    i = pl.program_id(0)

    @pl.when(i == 0)
    def _():
        carry_ref[...] = jnp.zeros_like(carry_ref)
        cnt_ref[...] = jnp.zeros_like(cnt_ref)

    lane = lax.broadcasted_iota(I32, lg.shape, 1)
    lanef = lane.astype(F32)
    big = float(LANES)
    gmask = lane < N_GROUPS
    gl = jnp.where(gmask, lg, NEG)
    gmax = jnp.max(gl, axis=-1, keepdims=True)
    gsel = jnp.min(jnp.where(gmask & (gl == gmax), lanef, big), axis=-1, keepdims=True)
    gsum = jnp.sum(jnp.where(gmask, jnp.exp(gl - gmax), 0.0), axis=-1, keepdims=True)
    w_g = 1.0 / gsum
    lo = EXPERT_LANE0 + EXPERTS_PER_GROUP * gsel
    emask = (lanef >= lo) & (lanef < lo + EXPERTS_PER_GROUP)
    el = jnp.where(emask, lg, NEG)
    v1 = jnp.max(el, axis=-1, keepdims=True)
    i1 = jnp.min(jnp.where(emask & (el == v1), lanef, big), axis=-1, keepdims=True)
    emask2 = emask & (lanef != i1)
    el2 = jnp.where(emask2, lg, NEG)
    v2 = jnp.max(el2, axis=-1, keepdims=True)
    i2 = jnp.min(jnp.where(emask2 & (el2 == v2), lanef, big), axis=-1, keepdims=True)
    e2 = jnp.exp(v2 - v1)
    den = 1.0 / (1.0 + e2)
    g1 = w_g * den
    g2 = w_g * e2 * den
    oh1 = lanef == i1
    oh2 = lanef == i2
    onehot = (oh1 | oh2).astype(BF16)
    before = jnp.dot(tri_ref[...], onehot, preferred_element_type=F32)
    count = jnp.sum(onehot.astype(F32), axis=0, keepdims=True)
    units = jnp.floor((count + (RUN_ALIGN - 1)) * (1.0 / RUN_ALIGN))
    rows = units * RUN_ALIGN
    off = RUN_ALIGN * jnp.dot(jnp.broadcast_to(units, (8, LANES)).astype(BF16), upper_ref[...],
                              preferred_element_type=F32)[0:1]
    slot1 = jnp.sum(jnp.where(oh1, off + before, 0.0), axis=-1, keepdims=True)
    slot2 = jnp.sum(jnp.where(oh2, off + before, 0.0), axis=-1, keepdims=True)
    tcarry_ref[0] = carry_ref[...]
    trows_ref[0] = rows
    total = jnp.sum(rows, axis=-1, keepdims=True)
    toff_ref[0] = jnp.where(lax.broadcasted_iota(I32, (1, LANES), 1) == TOTAL_LANE, total, off)
    carry_ref[...] += rows
    eye = lax.broadcasted_iota(I32, (LANES, LANES), 0) == lax.broadcasted_iota(I32, (LANES, LANES), 1)
    cnt_ref[...] += jnp.sum(jnp.where(eye, rows, 0.0), axis=1, keepdims=True)
    rec = jnp.zeros_like(lg)
    for slot, val in ((R_E1, i1), (R_E2, i2), (R_SLOT1, slot1),
                      (R_SLOT2, slot2), (R_G1, g1), (R_G2, g2)):
        rec = jnp.where(lane == slot, val, rec)
    rec_ref[...] = rec


def _route_constants():
    r = jnp.arange(ROUTE_TM)
    tri = (r[:, None] > r[None, :]).astype(BF16)
    ln = jnp.arange(LANES)
    upper = (ln[:, None] < ln[None, :]).astype(BF16)
    return tri, upper


P_START, P_ROWS, P_OFF = 0, 1, 2


def _plan_kernel(carry_ref, rows_ref, off_ref, cnt_ref, plan_ref, ends_ref):
    cnt = cnt_ref[...]
    padded = jnp.floor((cnt + (FFN_BLOCK - 1)) * (1.0 / FFN_BLOCK)) * FFN_BLOCK
    row = lax.broadcasted_iota(I32, cnt.shape, 0)
    col = lax.broadcasted_iota(I32, cnt.shape, 1)
    starts = jnp.sum(jnp.where(row < col, padded, 0.0), axis=0, keepdims=True)
    ends_ref[...] = jnp.sum(jnp.where(row <= col, padded, 0.0), axis=0, keepdims=True)
    plan_ref[P_START] = (carry_ref[...] + starts).astype(I32)
    plan_ref[P_ROWS] = rows_ref[...].astype(I32)
    plan_ref[P_OFF] = off_ref[...].astype(I32)


def run_plan(tabs, counts):
    n_tiles = tabs[0].shape[0]
    flat = [t.reshape(n_tiles, LANES) for t in tabs]
    return pl.pallas_call(
        _plan_kernel,
        out_shape=[jax.ShapeDtypeStruct((3, n_tiles, LANES), I32),
                   jax.ShapeDtypeStruct((1, LANES), F32)],
        compiler_params=pltpu.CompilerParams(vmem_limit_bytes=VMEM_LIMIT),
        name="run_plan",
    )(*flat, counts)


RUN_PIECE = 64
SMALL_PIECES = (32, 16, 8)
TOTAL_PIECES = (1024, 512, 256, 128, 64, 32, 16, 8)
SORT_CHUNK = 256


def _for_each_run_piece(plan_ref, tile, n_tiles, fn):
    plane = n_tiles * LANES

    def per_expert(e, c):
        idx = tile * LANES + EXPERT_LANE0 + e
        start = plan_ref[P_START * plane + idx]
        rows = plan_ref[P_ROWS * plane + idx]
        off = plan_ref[P_OFF * plane + idx]

        def big(j, c2):
            fn(pl.multiple_of(off + j * RUN_PIECE, RUN_ALIGN), pl.multiple_of(start + j * RUN_PIECE, RUN_ALIGN),
               RUN_PIECE)
            return c2
        lax.fori_loop(0, lax.shift_right_logical(rows, 6), big, 0)
        for size in SMALL_PIECES:
            done = jnp.bitwise_and(rows, -2 * size)

            @pl.when(jnp.bitwise_and(rows, size) != 0)
            def _():
                fn(pl.multiple_of(off + done, RUN_ALIGN), pl.multiple_of(start + done, RUN_ALIGN), size)
        return c
    lax.fori_loop(0, N_EXPERTS, per_expert, 0)


def _for_each_total_piece(plan_ref, tile, n_tiles, fn):
    total = plan_ref[P_OFF * n_tiles * LANES + tile * LANES + TOTAL_LANE]
    for size in TOTAL_PIECES:
        @pl.when(jnp.bitwise_and(total, size) != 0)
        def _():
            fn(size)


def _dispatch_kernel(plan_ref, ends_ref, nv_ref, h2_ref, rec_ref, xs_ref, zero_buf, stage, sem, zsem):
    i = pl.program_id(0)
    n_steps = pl.num_programs(0)
    n_blocks = xs_ref.shape[0] // FFN_BLOCK

    @pl.when(i == 0)
    def _():
        zero_buf[...] = jnp.zeros_like(zero_buf)

        def zero_copy(row0):
            return pltpu.make_async_copy(zero_buf, xs_ref.at[pl.ds(row0, FFN_BLOCK)], zsem)

        def seg_end(e):
            return ends_ref[e], ends_ref[e] > jnp.where(e > 0, ends_ref[jnp.maximum(e - 1, 0)], 0)

        def start_e(e, c):
            end, nonempty = seg_end(e)

            @pl.when(nonempty)
            def _():
                zero_copy(pl.multiple_of(end - FFN_BLOCK, FFN_BLOCK)).start()
            return c
        lax.fori_loop(0, N_EXPERTS, start_e, 0)

        def start_b(blk, c):
            zero_copy(pl.multiple_of(blk * FFN_BLOCK, FFN_BLOCK)).start()
            return c
        lax.fori_loop(nv_ref[0], n_blocks, start_b, 0)

        def wait_e(e, c):
            _, nonempty = seg_end(e)

            @pl.when(nonempty)
            def _():
                zero_copy(0).wait()
            return c
        lax.fori_loop(0, N_EXPERTS, wait_e, 0)

        def wait_b(blk, c):
            zero_copy(0).wait()
            return c
        lax.fori_loop(nv_ref[0], n_blocks, wait_b, 0)

    slot = i % 2
    rec = rec_ref[...]
    s1 = rec[:, R_SLOT1:R_SLOT1 + 1]
    s2 = rec[:, R_SLOT2:R_SLOT2 + 1]
    half = PACKED_W
    for c in range(LOCAL_ROWS // SORT_CHUNK):
        lanef = (lax.broadcasted_iota(I32, (ROUTE_TM, SORT_CHUNK), 1) + c * SORT_CHUNK).astype(F32)
        sel_t = ((lanef == s1) | (lanef == s2)).astype(BF16)
        pick = lambda cols: lax.dot_general(sel_t, h2_ref[:, cols], (((0,), (0,)), ((), ())),
                                            preferred_element_type=F32)
        stage[slot, c * SORT_CHUNK:(c + 1) * SORT_CHUNK, :] = _pack_pair(pick(slice(0, half)),
                                                                         pick(slice(half, 2 * half)))

    def start_piece(lrow, grow, size):
        pltpu.make_async_copy(stage.at[slot, pl.ds(lrow, size)], xs_ref.at[pl.ds(grow, size)],
                              sem.at[slot]).start()
    _for_each_run_piece(plan_ref, i, n_steps, start_piece)

    def wait_pieces(tile, s):
        def wait_piece(size):
            pltpu.make_async_copy(stage.at[s, pl.ds(0, size)], xs_ref.at[pl.ds(0, size)], sem.at[s]).wait()
        _for_each_total_piece(plan_ref, tile, n_steps, wait_piece)

    @pl.when(i > 0)
    def _():
        wait_pieces(i - 1, 1 - slot)

    @pl.when(i + 1 == n_steps)
    def _():
        wait_pieces(i, slot)


def dispatch(h2, rec, plan, seg_ends, n_valid, n_rows):
    n = h2.shape[0]
    return pl.pallas_call(
        _dispatch_kernel,
        grid_spec=pltpu.PrefetchScalarGridSpec(
            num_scalar_prefetch=3,
            grid=(n // ROUTE_TM,),
            in_specs=[pl.BlockSpec((ROUTE_TM, D_MODEL), lambda i, *_: (i, 0)),
                      pl.BlockSpec((ROUTE_TM, LANES), lambda i, *_: (i, 0))],
            out_specs=pl.BlockSpec(memory_space=pl.ANY),
            scratch_shapes=[pltpu.VMEM((FFN_BLOCK, PACKED_W), U32),
                            pltpu.VMEM((2, LOCAL_ROWS, PACKED_W), U32),
                            pltpu.SemaphoreType.DMA((2,)),
                            pltpu.SemaphoreType.DMA(())],
        ),
        out_shape=jax.ShapeDtypeStruct((n_rows, PACKED_W), U32),
        compiler_params=_cparams("arbitrary"),
        name="dispatch",
    )(plan, seg_ends, n_valid, h2, rec)


def _ffn_kernel(be_ref, nv_ref, xs_ref, wg_ref, wu_ref, wd_ref, ys_ref, wg_sc, wu_sc, wd_sc):
    i = pl.program_id(0)
    prev = be_ref[jnp.maximum(i - 1, 0)]
    fresh = (i == 0) | (be_ref[i] != prev)

    @pl.when(fresh)
    def _():
        wg_sc[...] = wg_ref[0].astype(BF16)
        wu_sc[...] = wu_ref[0].astype(BF16)
        wd_sc[...] = wd_ref[0].astype(BF16)

    @pl.when(i < nv_ref[0])
    def _():
        xa, xb = (t.astype(BF16) for t in _unpack_halves(xs_ref[...]))
        half = PACKED_W
        gate = (jnp.dot(xa, wg_sc[:half, :], preferred_element_type=F32)
                + jnp.dot(xb, wg_sc[half:, :], preferred_element_type=F32))
        up = (jnp.dot(xa, wu_sc[:half, :], preferred_element_type=F32)
              + jnp.dot(xb, wu_sc[half:, :], preferred_element_type=F32))
        hid = (gate * (1.0 / (1.0 + jnp.exp(-gate))) * up).astype(BF16)
        ys_ref[...] = _pack_halves(jnp.dot(hid, wd_sc[...], preferred_element_type=F32))

    @pl.when(i >= nv_ref[0])
    def _():
        ys_ref[...] = jnp.zeros_like(ys_ref)


def expert_ffn(xs, block_expert, n_valid, w_gate, w_up, w_down):
    n_rows = xs.shape[0]
    n_blocks = n_rows // FFN_BLOCK
    wmap = lambda i, be, nv: (be[i], 0, 0)
    return pl.pallas_call(
        _ffn_kernel,
        grid_spec=pltpu.PrefetchScalarGridSpec(
            num_scalar_prefetch=2,
            grid=(n_blocks,),
            in_specs=[pl.BlockSpec((FFN_BLOCK, PACKED_W), lambda i, be, nv: (jnp.minimum(i, nv[0] - 1), 0)),
                      pl.BlockSpec((1, D_MODEL, D_EXPERT), wmap),
                      pl.BlockSpec((1, D_MODEL, D_EXPERT), wmap),
                      pl.BlockSpec((1, D_EXPERT, D_MODEL), wmap)],
            out_specs=pl.BlockSpec((FFN_BLOCK, PACKED_W), lambda i, be, nv: (i, 0)),
            scratch_shapes=[pltpu.VMEM((D_MODEL, D_EXPERT), BF16),
                            pltpu.VMEM((D_MODEL, D_EXPERT), BF16),
                            pltpu.VMEM((D_EXPERT, D_MODEL), BF16)],
        ),
        out_shape=jax.ShapeDtypeStruct((n_rows, PACKED_W), U32),
        compiler_params=_cparams("arbitrary"),
        name="expert_ffn",
    )(block_expert, n_valid, xs, w_gate, w_up, w_down)


def _combine_kernel(plan_ref, x1_ref, rec_ref, gf_ref, ys_ref, out_ref, buf, sem):
    i = pl.program_id(0)
    n_steps = pl.num_programs(0)
    slot = i % 2

    def fetch_tile(tile, s):
        def start_piece(lrow, grow, size):
            pltpu.make_async_copy(ys_ref.at[pl.ds(grow, size)], buf.at[s, pl.ds(lrow, size)], sem.at[s]).start()
        _for_each_run_piece(plan_ref, tile, n_steps, start_piece)

    @pl.when(i == 0)
    def _():
        buf[...] = jnp.zeros_like(buf)
        fetch_tile(0, 0)

    @pl.when(i + 1 < n_steps)
    def _():
        fetch_tile(i + 1, 1 - slot)

    def wait_piece(size):
        pltpu.make_async_copy(ys_ref.at[pl.ds(0, size)], buf.at[slot, pl.ds(0, size)], sem.at[slot]).wait()
    _for_each_total_piece(plan_ref, i, n_steps, wait_piece)

    rec = rec_ref[...]
    lanef = lax.broadcasted_iota(I32, (ROUTE_TM, LOCAL_ROWS), 1).astype(F32)
    sel = (jnp.where(lanef == rec[:, R_SLOT1:R_SLOT1 + 1], rec[:, R_G1:R_G1 + 1], 0.0)
           + jnp.where(lanef == rec[:, R_SLOT2:R_SLOT2 + 1], rec[:, R_G2:R_G2 + 1], 0.0)).astype(BF16)
    ya, yb = (t.astype(BF16) for t in _unpack_halves(buf[slot]))
    half = PACKED_W
    xa = x1_ref[:, :half] + jnp.dot(sel, ya, preferred_element_type=F32)
    xb = x1_ref[:, half:] + jnp.dot(sel, yb, preferred_element_type=F32)
    ms = (jnp.sum(xa * xa, axis=-1, keepdims=True) + jnp.sum(xb * xb, axis=-1, keepdims=True)) * (1.0 / D_MODEL)
    scale = lax.rsqrt(ms + EPS)
    out_ref[:, :half] = xa * scale * gf_ref[:, :half]
    out_ref[:, half:] = xb * scale * gf_ref[:, half:]


def combine(x1, rec, norm_f_g, ys, plan):
    n = x1.shape[0]
    tm = ROUTE_TM
    return pl.pallas_call(
        _combine_kernel,
        grid_spec=pltpu.PrefetchScalarGridSpec(
            num_scalar_prefetch=1,
            grid=(n // tm,),
            in_specs=[pl.BlockSpec((tm, D_MODEL), lambda i, p: (i, 0)),
                      pl.BlockSpec((tm, LANES), lambda i, p: (i, 0)),
                      pl.BlockSpec((1, D_MODEL), lambda i, p: (0, 0)),
                      pl.BlockSpec(memory_space=pl.ANY)],
            out_specs=pl.BlockSpec((tm, D_MODEL), lambda i, p: (i, 0)),
            scratch_shapes=[pltpu.VMEM((2, LOCAL_ROWS, PACKED_W), U32),
                            pltpu.SemaphoreType.DMA((2,))],
        ),
        out_shape=jax.ShapeDtypeStruct((n, D_MODEL), F32),
        compiler_params=_cparams("arbitrary"),
        name="combine",
    )(plan, x1, rec, norm_f_g.reshape(1, D_MODEL), ys)


def _router_weights(w_group, b_group, w_router, b_router):
    w = jnp.zeros((D_MODEL, LANES), F32)
    w = w.at[:, :N_GROUPS].set(w_group).at[:, EXPERT_LANE0:EXPERT_LANE0 + N_EXPERTS].set(w_router)
    b = jnp.zeros((1, LANES), F32)
    b = b.at[0, :N_GROUPS].set(b_group).at[0, EXPERT_LANE0:EXPERT_LANE0 + N_EXPERTS].set(b_router)
    hi = w.astype(BF16)
    lo = (w - hi.astype(F32)).astype(BF16)
    return hi, lo, b


def _sorted_rows_bound(n_tokens):
    worst = (2 * n_tokens + (n_tokens // ROUTE_TM) * N_EXPERTS * (RUN_ALIGN - 1)
             + N_EXPERTS * (FFN_BLOCK - 1))
    return -(-worst // FFN_BLOCK) * FFN_BLOCK


def _block_plan(ends_row, n_rows):
    seg_ends = ends_row[0, EXPERT_LANE0:EXPERT_LANE0 + N_EXPERTS].astype(I32)
    n_blocks = n_rows // FFN_BLOCK
    blk_start = jnp.arange(n_blocks, dtype=I32) * FFN_BLOCK
    block_expert = jnp.minimum(jnp.sum((seg_ends[None, :] <= blk_start[:, None]).astype(I32), axis=1),
                               N_EXPERTS - 1)
    n_valid = seg_ends[-1:] // FFN_BLOCK
    return seg_ends, block_expert, n_valid


def kernel(x, norm1_g, w_in, attn_norm_g, hgrn_gamma, hgrn_norm_g, w_out, norm2_g, w_group, b_group,
           w_router, b_router, w_gate, w_up, w_down, norm_f_g):
    batch, seq, d = x.shape
    assert seq == SEQ and d == D_MODEL and norm1_g.shape[0] == 1
    n = batch * seq
    x2 = x.reshape(n, d)
    qkv, hg = in_proj(x2, norm1_g[0], w_in[0].astype(BF16), batch)
    oa = attention(qkv)
    yh = hgrn(hg.reshape(batch, seq, HG_W), hgrn_gamma, hgrn_norm_g[0]).reshape(n, WIDTH)
    wr_hi, wr_lo, br = _router_weights(w_group[0], b_group[0], w_router[0], b_router[0])
    x1, h2, rec, tcarry, trows, toff, counts = post_mix(oa, yh, x2, attn_norm_g[0], w_out[0].astype(BF16),
                                                        norm2_g[0], wr_hi, wr_lo, br)
    plan, ends_row = run_plan((tcarry, trows, toff), counts)
    plan = plan.reshape(-1)
    n_rows = _sorted_rows_bound(n)
    seg_ends, block_expert, n_valid = _block_plan(ends_row, n_rows)
    xs = dispatch(h2, rec, plan, seg_ends, n_valid, n_rows)
    ys = expert_ffn(xs, block_expert, n_valid, w_gate[0], w_up[0], w_down[0])
    return combine(x1, rec, norm_f_g, ys, plan).reshape(batch, seq, d)
```

```python
import jax
import jax.numpy as jnp
from jax import lax
from jax.experimental import pallas as pl
from jax.experimental.pallas import tpu as pltpu

F32 = jnp.float32
BF16 = jnp.bfloat16
I32 = jnp.int32
U32 = jnp.uint32

D_MODEL = 1024
HEAD_DIM = 64
N_HEADS = 8
WIDTH = N_HEADS * HEAD_DIM
QKV_W = 3 * WIDTH
HG_W = 4 * WIDTH
SEQ = 2048
ATTN_BLOCK = 128
DILATIONS = (1, 4, 16)
N_RES = 16
RES_LEN = SEQ // N_RES
TILE_TOKENS = 512
U_PER_TILE = TILE_TOKENS // N_RES
HGRN_CHUNK = 32
HGRN_SUPER = 256
HGRN_GROUP = 4
N_GROUPS = 4
EXPERTS_PER_GROUP = 8
N_EXPERTS = 32
D_EXPERT = 512
FFN_BLOCK = 512
PACKED_W = D_MODEL // 2
EPS = 1e-6
NEG = -1e30
LOG2E = 1.4426950408889634
LANES = 128
VMEM_LIMIT = 56 * 1024 * 1024


def _cparams(*sem):
    return pltpu.CompilerParams(dimension_semantics=sem, vmem_limit_bytes=VMEM_LIMIT)


def _pack_pair(a, b):
    hi = pltpu.bitcast(a.astype(BF16).astype(F32), U32)
    lo = pltpu.bitcast(b.astype(BF16).astype(F32), U32)
    return hi | (lo >> 16)


def _pack_halves(x):
    w = x.shape[1] // 2
    return _pack_pair(x[:, :w], x[:, w:])


def _unpack_halves(p):
    return pltpu.bitcast(p & jnp.uint32(0xFFFF0000), F32), pltpu.bitcast(p << 16, F32)


N_LANE_TILES = WIDTH // LANES


def _tile_permutation():
    i = jnp.arange(TILE_TOKENS)
    src = N_RES * (i % U_PER_TILE) + i // U_PER_TILE
    return (src[:, None] == jnp.arange(TILE_TOKENS)[None, :]).astype(BF16)


def _in_proj_kernel(x_ref, g_ref, w_ref, perm_ref, qkv_ref, hg_ref):
    x = x_ref[...]
    ms = jnp.mean(x * x, axis=-1, keepdims=True)
    h = (x * lax.rsqrt(ms + EPS) * g_ref[...]).astype(BF16)
    hp = jnp.dot(perm_ref[...], h, preferred_element_type=F32).astype(BF16)
    for c in range(QKV_W // WIDTH):
        res = jnp.dot(hp, w_ref[:, c * WIDTH:(c + 1) * WIDTH], preferred_element_type=F32)
        for r in range(N_RES):
            for g in range(N_LANE_TILES):
                qkv_ref[r, g, c] = res[r * U_PER_TILE:(r + 1) * U_PER_TILE, g * LANES:(g + 1) * LANES]
    for j in range(HG_W // WIDTH):
        wsl = slice(QKV_W + j * WIDTH, QKV_W + (j + 1) * WIDTH)
        hg_ref[:, j * WIDTH:(j + 1) * WIDTH] = jnp.dot(h, w_ref[:, wsl], preferred_element_type=F32).astype(BF16)


def in_proj(x2, g, w_bf16, batch):
    n = x2.shape[0]
    tiles_per_b = SEQ // TILE_TOKENS
    return pl.pallas_call(
        _in_proj_kernel,
        grid=(n // TILE_TOKENS,),
        in_specs=[
            pl.BlockSpec((TILE_TOKENS, D_MODEL), lambda i: (i, 0)),
            pl.BlockSpec((1, D_MODEL), lambda i: (0, 0)),
            pl.BlockSpec((D_MODEL, QKV_W + HG_W), lambda i: (0, 0)),
            pl.BlockSpec((TILE_TOKENS, TILE_TOKENS), lambda i: (0, 0)),
        ],
        out_specs=[
            pl.BlockSpec((None, N_RES, N_LANE_TILES, 3, U_PER_TILE, LANES),
                         lambda i: (i // tiles_per_b, 0, 0, 0, i % tiles_per_b, 0)),
            pl.BlockSpec((TILE_TOKENS, HG_W), lambda i: (i, 0)),
        ],
        out_shape=[
            jax.ShapeDtypeStruct((batch, N_RES, N_LANE_TILES, 3, RES_LEN, LANES), F32),
            jax.ShapeDtypeStruct((n, HG_W), BF16),
        ],
        compiler_params=_cparams("arbitrary"),
        name="in_proj",
    )(x2, g.reshape(1, D_MODEL), w_bf16, _tile_permutation())


ATTN_GROUP16 = 8
ATTN_GROUP1 = 5
HEADS_PER_TILE = LANES // HEAD_DIM


def _attn_bias_tables():
    slopes = 2.0 ** (-8.0 * jnp.arange(1, N_HEADS + 1, dtype=F32) / N_HEADS)

    def table(qpos, kpos, dil):
        sd = qpos[:, None] - kpos[None, :]
        b = -slopes[:, None, None] * (sd * dil).astype(F32)[None] * LOG2E
        return jnp.where(((sd >= 0) & (sd <= ATTN_BLOCK))[None], b, NEG)

    q = jnp.arange(ATTN_BLOCK)
    k = jnp.arange(2 * ATTN_BLOCK)
    q1 = 16 * (q % 8) + q // 8 + ATTN_BLOCK
    k1 = 16 * (k % 16) + k // 16
    q4 = 4 * (q % 32) + q // 32 + ATTN_BLOCK
    k4 = 4 * (k % 64) + k // 64
    q16 = q + ATTN_BLOCK
    none = jnp.full((N_HEADS, ATTN_BLOCK, 2 * ATTN_BLOCK), NEG, F32)
    tabs = [jnp.concatenate([table(q1, k1, 1), table(q1, q1, 1)], axis=-1),
            jnp.concatenate([table(q4, k4, 4), table(q4, q4, 4)], axis=-1),
            jnp.concatenate([none, table(q16, q16, 16)], axis=-1)]
    return jnp.stack(tabs, axis=0)


def _attn_bias(bias_ref, d_idx, with_prev):
    ksl = slice(0, 2 * ATTN_BLOCK) if with_prev else slice(2 * ATTN_BLOCK, 3 * ATTN_BLOCK)
    return jnp.concatenate([bias_ref[d_idx, h, :, ksl] for h in range(HEADS_PER_TILE)], axis=0)


def _attn_blocks(blocks):
    nb = ATTN_BLOCK
    lane = lax.broadcasted_iota(I32, (nb, LANES), 1)
    head0 = lane < HEAD_DIM
    biases = [blk[3] for blk in blocks]
    blocks = [blk[:3] for blk in blocks]
    qs = [q * (HEAD_DIM ** -0.5 * LOG2E) for q, _, _ in blocks]
    qq = [jnp.concatenate([jnp.where(head0, q, 0.0), jnp.where(head0, 0.0, q)], axis=0).astype(BF16)
          for q in qs]
    ks = [k.astype(BF16) for _, k, _ in blocks]
    vs = [v.astype(BF16) for _, _, v in blocks]
    v_ones = [jnp.concatenate([v, jnp.ones_like(v)], axis=-1) for v in vs]
    s = [lax.dot_general(q2, k, (((1,), (1,)), ((), ())), preferred_element_type=F32) + bias
         for q2, k, bias in zip(qq, ks, biases)]
    m = [jnp.max(x, axis=-1, keepdims=True) for x in s]
    p = [jnp.exp2(x - mx).astype(BF16) for x, mx in zip(s, m)]
    acc = [jnp.dot(px, vo, preferred_element_type=F32) for px, vo in zip(p, v_ones)]
    head0_wide = jnp.concatenate([head0, head0], axis=-1)
    outs = []
    for a, mx in zip(acc, m):
        ol = jnp.where(head0_wide, a[:nb], a[nb:])
        l = ol[:, LANES:]
        outs.append((ol[:, :LANES] * (1.0 / l), jnp.where(head0, mx[:nb], mx[nb:]) + jnp.log2(l)))
    return outs


def _attn_kernel(qkv_ref, bias_ref, o_ref, o4_ref, l4_ref, o16_ref, l16_ref, fin_ref):
    grp = ATTN_GROUP16

    def body16(j, c):
        rs = [j * grp + a for a in range(grp)]
        bias = _attn_bias(bias_ref, 2, False)
        res = _attn_blocks([(qkv_ref[r, 0], qkv_ref[r, 1], qkv_ref[r, 2], bias) for r in rs])
        for r, (o, l) in zip(rs, res):
            o16_ref[r] = o
            l16_ref[r] = l
        return c
    lax.fori_loop(0, N_RES // grp, body16, 0)

    def gather4(c, rho, u0, nu):
        return jnp.concatenate([qkv_ref[rho + 4 * a, c, pl.ds(u0, nu), :] for a in range(4)], axis=0)

    def store4(rho, u0, o, l):
        for a in range(4):
            o4_ref[rho + 4 * a, pl.ds(u0, 32), :] = o[a * 32:(a + 1) * 32]
            l4_ref[rho + 4 * a, pl.ds(u0, 32), :] = l[a * 32:(a + 1) * 32]

    def body4(j, c):
        bias = _attn_bias(bias_ref, 1, True)
        todo = [(2 * j + a, 32 * n) for a in range(2) for n in range(1, 4)]
        res = _attn_blocks([(gather4(0, rho, u0, 32), gather4(1, rho, u0 - 32, 64),
                             gather4(2, rho, u0 - 32, 64), bias) for rho, u0 in todo])
        for (rho, u0), (o, l) in zip(todo, res):
            store4(rho, u0, o, l)
        return c
    lax.fori_loop(0, 2, body4, 0)

    def gather1(ref, u0, nu, *lead):
        return jnp.concatenate([ref[(r,) + lead + (pl.ds(u0, nu), slice(None))] for r in range(N_RES)],
                               axis=0)

    def merge_and_store(u0, o1, l1):
        l4 = gather1(l4_ref, u0, 8)
        l16 = gather1(l16_ref, u0, 8)
        mx = jnp.maximum(jnp.maximum(l1, l4), l16)
        e1 = jnp.exp2(l1 - mx)
        e4 = jnp.exp2(l4 - mx)
        e16 = jnp.exp2(l16 - mx)
        inv = 1.0 / (e1 + e4 + e16)
        o = (e1 * inv) * o1 + (e4 * inv) * gather1(o4_ref, u0, 8) + (e16 * inv) * gather1(o16_ref, u0, 8)
        for r in range(N_RES):
            fin_ref[r, pl.ds(u0, 8), :] = o[r * 8:(r + 1) * 8]

    bias4 = _attn_bias(bias_ref, 1, False)
    first = [tuple(gather4(c, rho, 0, 32) for c in range(3)) + (bias4,) for rho in range(4)]
    first.append(tuple(gather1(qkv_ref, 0, 8, c) for c in range(3)) + (_attn_bias(bias_ref, 0, False),))
    res = _attn_blocks(first)
    for rho in range(4):
        store4(rho, 0, *res[rho])
    merge_and_store(0, *res[4])

    n_blocks1 = RES_LEN // 8 - 1

    def body1(j, c):
        u0s = [pl.multiple_of((1 + j * ATTN_GROUP1 + a) * 8, 8) for a in range(ATTN_GROUP1)]
        bias = _attn_bias(bias_ref, 0, True)
        res = _attn_blocks([(gather1(qkv_ref, u0, 8, 0), gather1(qkv_ref, u0 - 8, 16, 1),
                             gather1(qkv_ref, u0 - 8, 16, 2), bias) for u0 in u0s])
        for u0, (o, l) in zip(u0s, res):
            merge_and_store(u0, o, l)
        return c
    lax.fori_loop(0, n_blocks1 // ATTN_GROUP1, body1, 0)

    def emit(r, c):
        o_ref[r] = fin_ref[r].astype(o_ref.dtype)
        return c
    lax.fori_loop(0, N_RES, emit, 0)


def attention(qkv):
    batch = qkv.shape[0]
    scratch = pltpu.VMEM((N_RES, RES_LEN, LANES), F32)
    return pl.pallas_call(
        _attn_kernel,
        grid=(batch, N_LANE_TILES),
        in_specs=[
            pl.BlockSpec((None, N_RES, None, 3, RES_LEN, LANES), lambda b, g: (b, 0, g, 0, 0, 0)),
            pl.BlockSpec((len(DILATIONS), HEADS_PER_TILE, ATTN_BLOCK, 3 * ATTN_BLOCK),
                         lambda b, g: (0, g, 0, 0)),
        ],
        out_specs=pl.BlockSpec((None, N_RES, None, RES_LEN, LANES), lambda b, g: (b, 0, g, 0, 0)),
        out_shape=jax.ShapeDtypeStruct((batch, N_RES, N_LANE_TILES, RES_LEN, LANES), BF16),
        scratch_shapes=[scratch] * 5,
        compiler_params=_cparams("arbitrary", "arbitrary"),
        name="dilated_attention",
    )(qkv, _attn_bias_tables())


def _split3(x):
    a = x.astype(BF16)
    r = x - a.astype(F32)
    b = r.astype(BF16)
    c = (r - b.astype(F32)).astype(BF16)
    return a, b, c


def _hgrn_kernel(q_ref, f_ref, i_ref, g_ref, gamma_ref, ng_ref, tri_ref, y_ref):
    sup, ch = HGRN_SUPER, HGRN_CHUNK
    n_ch = sup // ch
    gam = gamma_ref[...]
    gmx = jnp.max(gam, axis=0, keepdims=True)
    ge = jnp.exp(gam - gmx)
    lb = ge[0:1] / jnp.sum(ge, axis=0, keepdims=True)
    ng = ng_ref[...]
    tri = tri_ref[...]
    ri = lax.broadcasted_iota(I32, (2 * sup, sup), 0) % sup
    ci = lax.broadcasted_iota(I32, (2 * sup, sup), 1)
    causal_bd2 = (ri // ch == ci // ch) & (ri >= ci)
    head0 = lax.broadcasted_iota(I32, (sup, LANES), 1) < HEAD_DIM
    row_chunk = lax.broadcasted_iota(I32, (sup, LANES), 0) // ch
    same_head = (lax.broadcasted_iota(I32, (LANES, LANES), 0) // HEAD_DIM
                 == lax.broadcasted_iota(I32, (LANES, LANES), 1) // HEAD_DIM)

    def sigmoid(x):
        return 1.0 / (1.0 + jnp.exp(-x))

    def dot_nt(a, b):
        return lax.dot_general(a, b, (((1,), (1,)), ((), ())), preferred_element_type=F32)

    def dot_tn(a, b):
        return lax.dot_general(a, b, (((0,), (0,)), ((), ())), preferred_element_type=F32)

    def expand(x):
        zero = jnp.zeros_like(x)
        return jnp.concatenate([jnp.where(row_chunk == c, x, zero) for c in range(n_ch)], axis=1)

    def body(j, state):
        grp = range(HGRN_GROUP)
        rows = [pl.ds(pl.multiple_of((j * HGRN_GROUP + a) * sup, sup), sup) for a in grp]
        fr = [f_ref[r, :].astype(F32) for r in rows]
        t = [jnp.exp(-jnp.abs(x)) for x in fr]
        rcp = [1.0 / (1.0 + x) for x in t]
        sig = [jnp.where(f >= 0, r, x * r) for f, r, x in zip(fr, rcp, t)]
        nsig = [jnp.where(f >= 0, x * r, r) for f, r, x in zip(fr, rcp, t)]
        logf = [jnp.log(lb + (1.0 - lb) * x) for x in sig]
        key = [(1.0 - lb) * x for x in nsig]
        parts = [_split3(x) for x in logf]
        b = [sum(jnp.dot(tri, p, preferred_element_type=F32) for p in ps) for ps in parts]
        totals = [[x[(c + 1) * ch - 1:(c + 1) * ch, :] for c in range(n_ch)] for x in b]
        b_last = [jnp.concatenate([jnp.broadcast_to(t_, (ch, LANES)) for t_ in ts], axis=0) for ts in totals]
        pad = jnp.zeros((LANES - n_ch, LANES), F32)
        decay_t = [jnp.exp(jnp.transpose(jnp.concatenate(ts + [pad], axis=0))) for ts in totals]
        qv = [q_ref[r, :].astype(F32) for r in rows]
        q_in = [(x * sigmoid(x) * jnp.exp(bb)).astype(BF16) for x, bb in zip(qv, b)]
        k_in = [(k * jnp.exp(-bb)).astype(BF16) for k, bb in zip(key, b)]
        k_end = [(k * jnp.exp(bl - bb)).astype(BF16) for k, bl, bb in zip(key, b_last, b)]
        vv = [i_ref[r, :] for r in rows]
        q2 = [jnp.concatenate([jnp.where(head0, x, jnp.zeros_like(x)), jnp.where(head0, jnp.zeros_like(x), x)],
                              axis=0) for x in q_in]
        att = [jnp.where(causal_bd2, dot_nt(x, k), 0.0).astype(BF16) for x, k in zip(q2, k_in)]
        intra = [jnp.dot(a_, v, preferred_element_type=F32) for a_, v in zip(att, vv)]
        o_intra = [jnp.where(head0, x[:sup], x[sup:]) for x in intra]
        ds_all = [dot_tn(k, expand(v)) for k, v in zip(k_end, vv)]
        s_stack = []
        for a in grp:
            s_prev = []
            for c in range(n_ch):
                s_prev.append(state.astype(BF16))
                ds = jnp.where(same_head, ds_all[a][:, c * LANES:(c + 1) * LANES], 0.0)
                state = decay_t[a][:, c:c + 1] * state + ds
            s_stack.append(jnp.concatenate(s_prev, axis=0))
        o_inter = [jnp.dot(expand(x), s_, preferred_element_type=F32) for x, s_ in zip(q_in, s_stack)]
        o = [x + y for x, y in zip(o_intra, o_inter)]
        sq = [x * x for x in o]
        ss0 = [jnp.sum(jnp.where(head0, x, 0.0), axis=-1, keepdims=True) for x in sq]
        ss1 = [jnp.sum(jnp.where(head0, 0.0, x), axis=-1, keepdims=True) for x in sq]
        ms = [jnp.where(head0, x, y) * (1.0 / HEAD_DIM) for x, y in zip(ss0, ss1)]
        gv = [g_ref[r, :].astype(F32) for r in rows]
        for r, x, m_, g_ in zip(rows, o, ms, gv):
            y_ref[r, :] = (x * lax.rsqrt(m_ + EPS) * ng * (g_ * sigmoid(g_))).astype(y_ref.dtype)
        return state

    lax.fori_loop(0, SEQ // (sup * HGRN_GROUP), body, jnp.zeros((LANES, LANES), F32))


def hgrn(hg3, gamma, norm_g):
    batch = hg3.shape[0]
    r = jnp.arange(HGRN_SUPER)
    same = (r[:, None] // HGRN_CHUNK) == (r[None, :] // HGRN_CHUNK)
    tri = (same & (r[:, None] >= r[None, :])).astype(BF16)
    col = lambda which: (lambda b, g: (b, 0, which * N_LANE_TILES + g))
    const2 = lambda b, g: (0, 0)
    return pl.pallas_call(
        _hgrn_kernel,
        grid=(batch, N_LANE_TILES),
        in_specs=[
            pl.BlockSpec((None, SEQ, LANES), col(0)),
            pl.BlockSpec((None, SEQ, LANES), col(1)),
            pl.BlockSpec((None, SEQ, LANES), col(2)),
            pl.BlockSpec((None, SEQ, LANES), col(3)),
            pl.BlockSpec((2, LANES), lambda b, g: (0, g)),
            pl.BlockSpec((1, LANES), lambda b, g: (0, g)),
            pl.BlockSpec((HGRN_SUPER, HGRN_SUPER), const2),
        ],
        out_specs=pl.BlockSpec((None, SEQ, LANES), lambda b, g: (b, 0, g)),
        out_shape=jax.ShapeDtypeStruct((batch, SEQ, WIDTH), BF16),
        compiler_params=_cparams("arbitrary", "arbitrary"),
        name="hgrn2",
    )(hg3, hg3, hg3, hg3, gamma.astype(F32), norm_g.reshape(1, WIDTH).astype(F32), tri)


def _post_mix_kernel(oa_ref, yh_ref, x_ref, ag_ref, wo_ref, g2_ref, wrh_ref, wrl_ref, br_ref, perm_t_ref,
                     tri_ref, lower_ref,
                     x1_ref, h2_ref, rec_ref, tab_ref, carry_ref, lg_sc):
    i = pl.program_id(0)

    @pl.when(i == 0)
    def _():
        lg_sc[...] = jnp.zeros_like(lg_sc)
    _route_tile(lg_sc[...], (i > 0).astype(F32), tri_ref, lower_ref, rec_ref, tab_ref, carry_ref)

    oa = jnp.concatenate(
        [jnp.concatenate([oa_ref[r, g] for g in range(N_LANE_TILES)], axis=1)
         for r in range(N_RES)], axis=0).astype(F32)
    ms = jnp.mean(oa * oa, axis=-1, keepdims=True)
    ya = (oa * lax.rsqrt(ms + EPS) * ag_ref[...]).astype(BF16)
    ya = jnp.dot(perm_t_ref[...], ya, preferred_element_type=F32).astype(BF16)
    mix = (jnp.dot(ya, wo_ref[:WIDTH, :], preferred_element_type=F32)
           + jnp.dot(yh_ref[...], wo_ref[WIDTH:, :], preferred_element_type=F32))
    x1 = x_ref[...] + mix
    x1_ref[...] = x1
    ms2 = jnp.mean(x1 * x1, axis=-1, keepdims=True)
    h2 = x1 * lax.rsqrt(ms2 + EPS) * g2_ref[...]
    hi = h2.astype(BF16)
    h2_ref[...] = hi
    lo = (h2 - hi.astype(F32)).astype(BF16)
    nt = (((1,), (1,)), ((), ()))
    wrh = wrh_ref[...]
    lg_sc[...] = (lax.dot_general(wrh, hi, nt, preferred_element_type=F32)
                  + lax.dot_general(wrh, lo, nt, preferred_element_type=F32)
                  + lax.dot_general(wrl_ref[...], hi, nt, preferred_element_type=F32)
                  + br_ref[...])


def post_mix(oa, yh, x2, attn_g, w_out_bf16, g2, wr_hi, wr_lo, br):
    n = x2.shape[0]
    tiles_per_b = SEQ // TILE_TOKENS
    assert ROUTE_TM == TILE_TOKENS
    n_tiles = n // TILE_TOKENS
    cur = lambda i: jnp.minimum(i, n_tiles - 1)
    prev = lambda i: jnp.maximum(i - 1, 0)
    row = lambda w: pl.BlockSpec((TILE_TOKENS, w), lambda i: (cur(i), 0))
    const = lambda r, c: pl.BlockSpec((r, c), lambda i: (0, 0))
    return pl.pallas_call(
        _post_mix_kernel,
        grid=(n_tiles + 1,),
        in_specs=[pl.BlockSpec((None, N_RES, N_LANE_TILES, U_PER_TILE, LANES),
                               lambda i: (cur(i) // tiles_per_b, 0, 0, cur(i) % tiles_per_b, 0)),
                  row(WIDTH), row(D_MODEL),
                  const(1, WIDTH), const(2 * WIDTH, D_MODEL), const(1, D_MODEL),
                  const(LANES, D_MODEL), const(LANES, D_MODEL), const(LANES, 1),
                  const(TILE_TOKENS, TILE_TOKENS), const(ROUTE_TM, ROUTE_TM), const(N_EXPERTS, N_EXPERTS)],
        out_specs=[row(D_MODEL), row(D_MODEL),
                   pl.BlockSpec((8, ROUTE_TM), lambda i: (0, prev(i))),
                   pl.BlockSpec((1, 3, TABLE_ROWS, LANES), lambda i: (prev(i), 0, 0, 0))],
        out_shape=[jax.ShapeDtypeStruct((n, D_MODEL), F32),
                   jax.ShapeDtypeStruct((n, D_MODEL), BF16),
                   jax.ShapeDtypeStruct((8, n), F32),
                   jax.ShapeDtypeStruct((n_tiles, 3, TABLE_ROWS, LANES), F32)],
        scratch_shapes=[pltpu.VMEM((N_EXPERTS, LANES), F32), pltpu.VMEM((LANES, TILE_TOKENS), F32)],
        compiler_params=_cparams("arbitrary"),
        name="post_mix",
    )(oa, yh, x2, attn_g.reshape(1, WIDTH), w_out_bf16, g2.reshape(1, D_MODEL), wr_hi, wr_lo, br,
      _tile_permutation().T, *_route_constants())


ROUTE_TM = 512
EXPERT_ROW0 = 32
R_E1, R_E2, R_SLOT1, R_SLOT2, R_G1, R_G2 = 0, 1, 2, 3, 4, 5
T_CARRY, T_ROWS, T_OFF = 0, 1, 2
TABLE_ROWS = 40
TOTAL_LANE = N_EXPERTS
RUN_ALIGN = 8
LOCAL_ROWS = 2 * ROUTE_TM + N_EXPERTS * RUN_ALIGN


def _route_tile(lg, live, tri_ref, lower_ref, rec_ref, tab_ref, carry_ref):
    i = pl.program_id(0)

    @pl.when(i == 0)
    def _():
        carry_ref[...] = jnp.zeros_like(carry_ref)

    tm = lg.shape[1]
    sub8 = lax.broadcasted_iota(I32, (8, tm), 0).astype(F32)
    big = 8.0
    gmask = sub8 < N_GROUPS
    gl = jnp.where(gmask, lg[0:8], NEG)
    gmax = jnp.max(gl, axis=0, keepdims=True)
    gsel = jnp.min(jnp.where(gmask & (gl == gmax), sub8, big), axis=0, keepdims=True)
    gsum = jnp.sum(jnp.where(gmask, jnp.exp(gl - gmax), 0.0), axis=0, keepdims=True)
    w_g = 1.0 / gsum
    el = jnp.zeros((EXPERTS_PER_GROUP, tm), F32)
    for g in range(N_GROUPS):
        lo = EXPERT_ROW0 + g * EXPERTS_PER_GROUP
        el = jnp.where(gsel == g, lg[lo:lo + EXPERTS_PER_GROUP], el)
    v1 = jnp.max(el, axis=0, keepdims=True)
    i1 = jnp.min(jnp.where(el == v1, sub8, big), axis=0, keepdims=True)
    el2 = jnp.where(sub8 == i1, NEG, el)
    v2 = jnp.max(el2, axis=0, keepdims=True)
    i2 = jnp.min(jnp.where((el2 == v2) & (sub8 != i1), sub8, big), axis=0, keepdims=True)
    ex = jnp.exp(v2 - v1)
    den = 1.0 / (1.0 + ex)
    g1 = w_g * den
    g2 = w_g * ex * den
    e1 = gsel * EXPERTS_PER_GROUP + i1
    e2 = gsel * EXPERTS_PER_GROUP + i2
    sub_e = lax.broadcasted_iota(I32, (N_EXPERTS, tm), 0).astype(F32)
    oh1 = sub_e == e1
    oh2 = sub_e == e2
    onehot = (oh1 | oh2).astype(BF16)
    before = jnp.dot(onehot, tri_ref[...], preferred_element_type=F32)
    count = jnp.sum(onehot.astype(F32), axis=1, keepdims=True)
    units = jnp.floor((count + (RUN_ALIGN - 1)) * (1.0 / RUN_ALIGN)) * live
    units = jnp.broadcast_to(units, (N_EXPERTS, LANES))
    rows = units * RUN_ALIGN
    off = RUN_ALIGN * jnp.dot(lower_ref[...], units.astype(BF16), preferred_element_type=F32)
    place = off[:, 0:1] + before
    slot1 = jnp.sum(jnp.where(oh1, place, 0.0), axis=0, keepdims=True)
    slot2 = jnp.sum(jnp.where(oh2, place, 0.0), axis=0, keepdims=True)
    total = jnp.sum(rows, axis=0, keepdims=True)
    pad = jnp.zeros((TABLE_ROWS - N_EXPERTS - 1, LANES), F32)
    tab_ref[0, T_CARRY] = jnp.concatenate([carry_ref[...], total, pad], axis=0)
    tab_ref[0, T_ROWS] = jnp.concatenate([rows, total, pad], axis=0)
    tab_ref[0, T_OFF] = jnp.concatenate([off, total, pad], axis=0)
    carry_ref[...] += rows
    rec = jnp.zeros((8, tm), F32)
    for slot, val in ((R_E1, e1), (R_E2, e2), (R_SLOT1, slot1),
                      (R_SLOT2, slot2), (R_G1, g1), (R_G2, g2)):
        rec = jnp.where(sub8 == slot, val, rec)
    rec_ref[...] = rec


def _route_constants():
    r = jnp.arange(ROUTE_TM)
    tri = (r[:, None] < r[None, :]).astype(BF16)
    e = jnp.arange(N_EXPERTS)
    lower = (e[:, None] > e[None, :]).astype(BF16)
    return tri, lower


P_START, P_ROWS, P_OFF = 0, 1, 2


def _plan_kernel(carry_ref, rows_ref, off_ref, upper_ref, plan_ref, ends_ref):
    n_tiles = carry_ref.shape[0]
    lane = lax.broadcasted_iota(I32, (1, LANES), 1)
    last = pl.ds(n_tiles - 1, 1)
    totals = jnp.where(lane < N_EXPERTS, carry_ref[last, :] + rows_ref[last, :], 0.0)
    blocks = jnp.floor((totals + (FFN_BLOCK - 1)) * (1.0 / FFN_BLOCK))
    starts = FFN_BLOCK * jnp.dot(jnp.broadcast_to(blocks, (8, LANES)).astype(BF16), upper_ref[...],
                                 preferred_element_type=F32)[0:1]
    ends_ref[...] = starts + blocks * FFN_BLOCK
    plan_ref[P_START] = (carry_ref[...] + starts).astype(I32)
    plan_ref[P_ROWS] = rows_ref[...].astype(I32)
    plan_ref[P_OFF] = off_ref[...].astype(I32)


def run_plan(tab):
    n_tiles = tab.shape[0]
    rows = jnp.pad(tab[:, :, :, 0], ((0, 0), (0, 0), (0, LANES - TABLE_ROWS)))
    ln = jnp.arange(LANES)
    upper = (ln[:, None] < ln[None, :]).astype(BF16)
    return pl.pallas_call(
        _plan_kernel,
        out_shape=[jax.ShapeDtypeStruct((3, n_tiles, LANES), I32),
                   jax.ShapeDtypeStruct((1, LANES), F32)],
        compiler_params=pltpu.CompilerParams(vmem_limit_bytes=VMEM_LIMIT),
        name="run_plan",
    )(rows[:, T_CARRY], rows[:, T_ROWS], rows[:, T_OFF], upper)


RUN_PIECE = 64
SMALL_PIECES = (32, 16, 8)
TOTAL_PIECES = (1024, 512, 256, 128, 64, 32, 16, 8)
SORT_CHUNK = 256


def _for_each_run_piece(plan_ref, tile, n_tiles, fn):
    plane = n_tiles * LANES

    def per_expert(e, c):
        idx = tile * LANES + e
        start = plan_ref[P_START * plane + idx]
        rows = plan_ref[P_ROWS * plane + idx]
        off = plan_ref[P_OFF * plane + idx]

        def big(j, c2):
            fn(pl.multiple_of(off + j * RUN_PIECE, RUN_ALIGN), pl.multiple_of(start + j * RUN_PIECE, RUN_ALIGN),
               RUN_PIECE)
            return c2
        lax.fori_loop(0, lax.shift_right_logical(rows, 6), big, 0)
        for size in SMALL_PIECES:
            done = jnp.bitwise_and(rows, -2 * size)

            @pl.when(jnp.bitwise_and(rows, size) != 0)
            def _():
                fn(pl.multiple_of(off + done, RUN_ALIGN), pl.multiple_of(start + done, RUN_ALIGN), size)
        return c
    lax.fori_loop(0, N_EXPERTS, per_expert, 0)


def _for_each_total_piece(plan_ref, tile, n_tiles, fn):
    total = plan_ref[P_OFF * n_tiles * LANES + tile * LANES + TOTAL_LANE]
    for size in TOTAL_PIECES:
        @pl.when(jnp.bitwise_and(total, size) != 0)
        def _():
            fn(size)


def _dispatch_kernel(plan_ref, ends_ref, nv_ref, h2_ref, rec_ref, xs_ref, zero_buf, stage, sem, zsem):
    i = pl.program_id(0)
    n_steps = pl.num_programs(0)
    n_blocks = xs_ref.shape[0] // FFN_BLOCK

    @pl.when(i == 0)
    def _():
        zero_buf[...] = jnp.zeros_like(zero_buf)

        def zero_copy(row0):
            return pltpu.make_async_copy(zero_buf, xs_ref.at[pl.ds(row0, FFN_BLOCK)], zsem)

        def seg_end(e):
            return ends_ref[e], ends_ref[e] > jnp.where(e > 0, ends_ref[jnp.maximum(e - 1, 0)], 0)

        def start_e(e, c):
            end, nonempty = seg_end(e)

            @pl.when(nonempty)
            def _():
                zero_copy(pl.multiple_of(end - FFN_BLOCK, FFN_BLOCK)).start()
            return c
        lax.fori_loop(0, N_EXPERTS, start_e, 0)

        def start_b(blk, c):
            zero_copy(pl.multiple_of(blk * FFN_BLOCK, FFN_BLOCK)).start()
            return c
        lax.fori_loop(nv_ref[0], n_blocks, start_b, 0)

        def wait_e(e, c):
            _, nonempty = seg_end(e)

            @pl.when(nonempty)
            def _():
                zero_copy(0).wait()
            return c
        lax.fori_loop(0, N_EXPERTS, wait_e, 0)

        def wait_b(blk, c):
            zero_copy(0).wait()
            return c
        lax.fori_loop(nv_ref[0], n_blocks, wait_b, 0)

    slot = i % 2
    s1 = rec_ref[R_SLOT1:R_SLOT1 + 1, :]
    s2 = rec_ref[R_SLOT2:R_SLOT2 + 1, :]
    half = PACKED_W
    for c in range(LOCAL_ROWS // SORT_CHUNK):
        rowf = (lax.broadcasted_iota(I32, (SORT_CHUNK, ROUTE_TM), 0) + c * SORT_CHUNK).astype(F32)
        sel = ((rowf == s1) | (rowf == s2)).astype(BF16)
        pick = lambda cols: jnp.dot(sel, h2_ref[:, cols], preferred_element_type=F32)
        stage[slot, c * SORT_CHUNK:(c + 1) * SORT_CHUNK, :] = _pack_pair(pick(slice(0, half)),
                                                                         pick(slice(half, 2 * half)))

    def start_piece(lrow, grow, size):
        pltpu.make_async_copy(stage.at[slot, pl.ds(lrow, size)], xs_ref.at[pl.ds(grow, size)],
                              sem.at[slot]).start()
    _for_each_run_piece(plan_ref, i, n_steps, start_piece)

    def wait_pieces(tile, s):
        def wait_piece(size):
            pltpu.make_async_copy(stage.at[s, pl.ds(0, size)], xs_ref.at[pl.ds(0, size)], sem.at[s]).wait()
        _for_each_total_piece(plan_ref, tile, n_steps, wait_piece)

    @pl.when(i > 0)
    def _():
        wait_pieces(i - 1, 1 - slot)

    @pl.when(i + 1 == n_steps)
    def _():
        wait_pieces(i, slot)


def dispatch(h2, rec, plan, seg_ends, n_valid, n_rows):
    n = h2.shape[0]
    return pl.pallas_call(
        _dispatch_kernel,
        grid_spec=pltpu.PrefetchScalarGridSpec(
            num_scalar_prefetch=3,
            grid=(n // ROUTE_TM,),
            in_specs=[pl.BlockSpec((ROUTE_TM, D_MODEL), lambda i, *_: (i, 0)),
                      pl.BlockSpec((8, ROUTE_TM), lambda i, *_: (0, i))],
            out_specs=pl.BlockSpec(memory_space=pl.ANY),
            scratch_shapes=[pltpu.VMEM((FFN_BLOCK, PACKED_W), U32),
                            pltpu.VMEM((2, LOCAL_ROWS, PACKED_W), U32),
                            pltpu.SemaphoreType.DMA((2,)),
                            pltpu.SemaphoreType.DMA(())],
        ),
        out_shape=jax.ShapeDtypeStruct((n_rows, PACKED_W), U32),
        compiler_params=_cparams("arbitrary"),
        name="dispatch",
    )(plan, seg_ends, n_valid, h2, rec)


def _ffn_kernel(be_ref, nv_ref, xs_ref, wg_ref, wu_ref, wd_ref, ys_ref, wg_sc, wu_sc, wd_sc):
    i = pl.program_id(0)
    prev = be_ref[jnp.maximum(i - 1, 0)]
    fresh = (i == 0) | (be_ref[i] != prev)

    @pl.when(fresh)
    def _():
        wg_sc[...] = wg_ref[0].astype(BF16)
        wu_sc[...] = wu_ref[0].astype(BF16)
        wd_sc[...] = wd_ref[0].astype(BF16)

    @pl.when(i < nv_ref[0])
    def _():
        xa, xb = (t.astype(BF16) for t in _unpack_halves(xs_ref[...]))
        half = PACKED_W
        gate = (jnp.dot(xa, wg_sc[:half, :], preferred_element_type=F32)
                + jnp.dot(xb, wg_sc[half:, :], preferred_element_type=F32))
        up = (jnp.dot(xa, wu_sc[:half, :], preferred_element_type=F32)
              + jnp.dot(xb, wu_sc[half:, :], preferred_element_type=F32))
        hid = (gate * (1.0 / (1.0 + jnp.exp(-gate))) * up).astype(BF16)
        ys_ref[...] = _pack_halves(jnp.dot(hid, wd_sc[...], preferred_element_type=F32))

    @pl.when(i >= nv_ref[0])
    def _():
        ys_ref[...] = jnp.zeros_like(ys_ref)


def expert_ffn(xs, block_expert, n_valid, w_gate, w_up, w_down):
    n_rows = xs.shape[0]
    n_blocks = n_rows // FFN_BLOCK
    wmap = lambda i, be, nv: (be[i], 0, 0)
    return pl.pallas_call(
        _ffn_kernel,
        grid_spec=pltpu.PrefetchScalarGridSpec(
            num_scalar_prefetch=2,
            grid=(n_blocks,),
            in_specs=[pl.BlockSpec((FFN_BLOCK, PACKED_W), lambda i, be, nv: (jnp.minimum(i, nv[0] - 1), 0)),
                      pl.BlockSpec((1, D_MODEL, D_EXPERT), wmap),
                      pl.BlockSpec((1, D_MODEL, D_EXPERT), wmap),
                      pl.BlockSpec((1, D_EXPERT, D_MODEL), wmap)],
            out_specs=pl.BlockSpec((FFN_BLOCK, PACKED_W), lambda i, be, nv: (i, 0)),
            scratch_shapes=[pltpu.VMEM((D_MODEL, D_EXPERT), BF16),
                            pltpu.VMEM((D_MODEL, D_EXPERT), BF16),
                            pltpu.VMEM((D_EXPERT, D_MODEL), BF16)],
        ),
        out_shape=jax.ShapeDtypeStruct((n_rows, PACKED_W), U32),
        compiler_params=_cparams("arbitrary"),
        name="expert_ffn",
    )(block_expert, n_valid, xs, w_gate, w_up, w_down)


def _combine_kernel(plan_ref, x1_ref, rec_ref, gf_ref, ys_ref, out_ref, buf, sem):
    i = pl.program_id(0)
    n_steps = pl.num_programs(0)
    slot = i % 2

    def fetch_tile(tile, s):
        def start_piece(lrow, grow, size):
            pltpu.make_async_copy(ys_ref.at[pl.ds(grow, size)], buf.at[s, pl.ds(lrow, size)], sem.at[s]).start()
        _for_each_run_piece(plan_ref, tile, n_steps, start_piece)

    @pl.when(i == 0)
    def _():
        buf[...] = jnp.zeros_like(buf)
        fetch_tile(0, 0)

    @pl.when(i + 1 < n_steps)
    def _():
        fetch_tile(i + 1, 1 - slot)

    def wait_piece(size):
        pltpu.make_async_copy(ys_ref.at[pl.ds(0, size)], buf.at[slot, pl.ds(0, size)], sem.at[slot]).wait()
    _for_each_total_piece(plan_ref, i, n_steps, wait_piece)

    rowf = lax.broadcasted_iota(I32, (LOCAL_ROWS, ROUTE_TM), 0).astype(F32)
    sel = (jnp.where(rowf == rec_ref[R_SLOT1:R_SLOT1 + 1, :], rec_ref[R_G1:R_G1 + 1, :], 0.0)
           + jnp.where(rowf == rec_ref[R_SLOT2:R_SLOT2 + 1, :], rec_ref[R_G2:R_G2 + 1, :], 0.0)).astype(BF16)
    ya, yb = (t.astype(BF16) for t in _unpack_halves(buf[slot]))
    half = PACKED_W
    tn = (((0,), (0,)), ((), ()))
    xa = x1_ref[:, :half] + lax.dot_general(sel, ya, tn, preferred_element_type=F32)
    xb = x1_ref[:, half:] + lax.dot_general(sel, yb, tn, preferred_element_type=F32)
    ms = (jnp.sum(xa * xa, axis=-1, keepdims=True) + jnp.sum(xb * xb, axis=-1, keepdims=True)) * (1.0 / D_MODEL)
    scale = lax.rsqrt(ms + EPS)
    out_ref[:, :half] = xa * scale * gf_ref[:, :half]
    out_ref[:, half:] = xb * scale * gf_ref[:, half:]


def combine(x1, rec, norm_f_g, ys, plan):
    n = x1.shape[0]
    tm = ROUTE_TM
    return pl.pallas_call(
        _combine_kernel,
        grid_spec=pltpu.PrefetchScalarGridSpec(
            num_scalar_prefetch=1,
            grid=(n // tm,),
            in_specs=[pl.BlockSpec((tm, D_MODEL), lambda i, p: (i, 0)),
                      pl.BlockSpec((8, tm), lambda i, p: (0, i)),
                      pl.BlockSpec((1, D_MODEL), lambda i, p: (0, 0)),
                      pl.BlockSpec(memory_space=pl.ANY)],
            out_specs=pl.BlockSpec((tm, D_MODEL), lambda i, p: (i, 0)),
            scratch_shapes=[pltpu.VMEM((2, LOCAL_ROWS, PACKED_W), U32),
                            pltpu.SemaphoreType.DMA((2,))],
        ),
        out_shape=jax.ShapeDtypeStruct((n, D_MODEL), F32),
        compiler_params=_cparams("arbitrary"),
        name="combine",
    )(plan, x1, rec, norm_f_g.reshape(1, D_MODEL), ys)


def _router_weights(w_group, b_group, w_router, b_router):
    w = jnp.zeros((LANES, D_MODEL), F32)
    w = w.at[:N_GROUPS].set(w_group.T).at[EXPERT_ROW0:EXPERT_ROW0 + N_EXPERTS].set(w_router.T)
    b = jnp.zeros((LANES, 1), F32)
    b = b.at[:N_GROUPS, 0].set(b_group).at[EXPERT_ROW0:EXPERT_ROW0 + N_EXPERTS, 0].set(b_router)
    hi = w.astype(BF16)
    lo = (w - hi.astype(F32)).astype(BF16)
    return hi, lo, b


def _sorted_rows_bound(n_tokens):
    worst = (2 * n_tokens + (n_tokens // ROUTE_TM) * N_EXPERTS * (RUN_ALIGN - 1)
             + N_EXPERTS * (FFN_BLOCK - 1))
    return -(-worst // FFN_BLOCK) * FFN_BLOCK


def _block_plan(ends_row, n_rows):
    seg_ends = ends_row[0, :N_EXPERTS].astype(I32)
    n_blocks = n_rows // FFN_BLOCK
    blk_start = jnp.arange(n_blocks, dtype=I32) * FFN_BLOCK
    block_expert = jnp.minimum(jnp.sum((seg_ends[None, :] <= blk_start[:, None]).astype(I32), axis=1),
                               N_EXPERTS - 1)
    n_valid = seg_ends[-1:] // FFN_BLOCK
    return seg_ends, block_expert, n_valid


def kernel(x, norm1_g, w_in, attn_norm_g, hgrn_gamma, hgrn_norm_g, w_out, norm2_g, w_group, b_group,
           w_router, b_router, w_gate, w_up, w_down, norm_f_g):
    batch, seq, d = x.shape
    assert seq == SEQ and d == D_MODEL and norm1_g.shape[0] == 1
    n = batch * seq
    x2 = x.reshape(n, d)
    qkv, hg = in_proj(x2, norm1_g[0], w_in[0].astype(BF16), batch)
    oa = attention(qkv)
    yh = hgrn(hg.reshape(batch, seq, HG_W), hgrn_gamma, hgrn_norm_g[0]).reshape(n, WIDTH)
    wr_hi, wr_lo, br = _router_weights(w_group[0], b_group[0], w_router[0], b_router[0])
    x1, h2, rec, tab = post_mix(oa, yh, x2, attn_norm_g[0], w_out[0].astype(BF16), norm2_g[0],
                                wr_hi, wr_lo, br)
    plan, ends_row = run_plan(tab)
    plan = plan.reshape(-1)
    n_rows = _sorted_rows_bound(n)
    seg_ends, block_expert, n_valid = _block_plan(ends_row, n_rows)
    xs = dispatch(h2, rec, plan, seg_ends, n_valid, n_rows)
    ys = expert_ffn(xs, block_expert, n_valid, w_gate[0], w_up[0], w_down[0])
    return combine(x1, rec, norm_f_g, ys, plan).reshape(batch, seq, d)
```

```python
import jax
import jax.numpy as jnp
from jax import lax
from jax.experimental import pallas as pl
from jax.experimental.pallas import tpu as pltpu

F32 = jnp.float32
BF16 = jnp.bfloat16
I32 = jnp.int32
U32 = jnp.uint32

D_MODEL = 1024
HEAD_DIM = 64
N_HEADS = 8
WIDTH = N_HEADS * HEAD_DIM
QKV_W = 3 * WIDTH
HG_W = 4 * WIDTH
SEQ = 2048
ATTN_BLOCK = 128
DILATIONS = (1, 4, 16)
N_RES = 16
RES_LEN = SEQ // N_RES
TILE_TOKENS = 512
U_PER_TILE = TILE_TOKENS // N_RES
HGRN_CHUNK = 32
HGRN_SUPER = 256
HGRN_GROUP = 4
N_GROUPS = 4
EXPERTS_PER_GROUP = 8
N_EXPERTS = 32
D_EXPERT = 512
FFN_BLOCK = 512
PACKED_W = D_MODEL // 2
EPS = 1e-6
NEG = -1e30
LOG2E = 1.4426950408889634
LANES = 128
VMEM_LIMIT = 56 * 1024 * 1024


def _cparams(*sem):
    return pltpu.CompilerParams(dimension_semantics=sem, vmem_limit_bytes=VMEM_LIMIT)


def _pack_pair(a, b):
    hi = pltpu.bitcast(a.astype(BF16).astype(F32), U32)
    lo = pltpu.bitcast(b.astype(BF16).astype(F32), U32)
    return hi | (lo >> 16)


def _pack_halves(x):
    w = x.shape[1] // 2
    return _pack_pair(x[:, :w], x[:, w:])


def _unpack_halves(p):
    return pltpu.bitcast(p & jnp.uint32(0xFFFF0000), F32), pltpu.bitcast(p << 16, F32)


N_LANE_TILES = WIDTH // LANES


def _tile_permutation():
    i = jnp.arange(TILE_TOKENS)
    src = N_RES * (i % U_PER_TILE) + i // U_PER_TILE
    return (src[:, None] == jnp.arange(TILE_TOKENS)[None, :]).astype(BF16)


def _in_proj_kernel(x_ref, g_ref, w_ref, perm_ref, qkv_ref, hg_ref):
    x = x_ref[...]
    ms = jnp.mean(x * x, axis=-1, keepdims=True)
    h = (x * lax.rsqrt(ms + EPS) * g_ref[...]).astype(BF16)
    hp = jnp.dot(perm_ref[...], h, preferred_element_type=F32).astype(BF16)
    for c in range(QKV_W // WIDTH):
        res = jnp.dot(hp, w_ref[:, c * WIDTH:(c + 1) * WIDTH], preferred_element_type=F32)
        for r in range(N_RES):
            for g in range(N_LANE_TILES):
                qkv_ref[r, g, c] = res[r * U_PER_TILE:(r + 1) * U_PER_TILE, g * LANES:(g + 1) * LANES]
    for j in range(HG_W // WIDTH):
        wsl = slice(QKV_W + j * WIDTH, QKV_W + (j + 1) * WIDTH)
        hg_ref[:, j * WIDTH:(j + 1) * WIDTH] = jnp.dot(h, w_ref[:, wsl], preferred_element_type=F32).astype(BF16)


def in_proj(x2, g, w_bf16, batch):
    n = x2.shape[0]
    tiles_per_b = SEQ // TILE_TOKENS
    return pl.pallas_call(
        _in_proj_kernel,
        grid=(n // TILE_TOKENS,),
        in_specs=[
            pl.BlockSpec((TILE_TOKENS, D_MODEL), lambda i: (i, 0)),
            pl.BlockSpec((1, D_MODEL), lambda i: (0, 0)),
            pl.BlockSpec((D_MODEL, QKV_W + HG_W), lambda i: (0, 0)),
            pl.BlockSpec((TILE_TOKENS, TILE_TOKENS), lambda i: (0, 0)),
        ],
        out_specs=[
            pl.BlockSpec((None, N_RES, N_LANE_TILES, 3, U_PER_TILE, LANES),
                         lambda i: (i // tiles_per_b, 0, 0, 0, i % tiles_per_b, 0)),
            pl.BlockSpec((TILE_TOKENS, HG_W), lambda i: (i, 0)),
        ],
        out_shape=[
            jax.ShapeDtypeStruct((batch, N_RES, N_LANE_TILES, 3, RES_LEN, LANES), F32),
            jax.ShapeDtypeStruct((n, HG_W), BF16),
        ],
        compiler_params=_cparams("arbitrary"),
        name="in_proj",
    )(x2, g.reshape(1, D_MODEL), w_bf16, _tile_permutation())


ATTN_GROUP16 = 8
ATTN_GROUP1 = 5
HEADS_PER_TILE = LANES // HEAD_DIM


def _attn_bias_tables():
    slopes = 2.0 ** (-8.0 * jnp.arange(1, N_HEADS + 1, dtype=F32) / N_HEADS)

    def table(qpos, kpos, dil):
        sd = qpos[:, None] - kpos[None, :]
        b = -slopes[:, None, None] * (sd * dil).astype(F32)[None] * LOG2E
        return jnp.where(((sd >= 0) & (sd <= ATTN_BLOCK))[None], b, NEG)

    q = jnp.arange(ATTN_BLOCK)
    k = jnp.arange(2 * ATTN_BLOCK)
    q1 = 16 * (q % 8) + q // 8 + ATTN_BLOCK
    k1 = 16 * (k % 16) + k // 16
    q4 = 4 * (q % 32) + q // 32 + ATTN_BLOCK
    k4 = 4 * (k % 64) + k // 64
    q16 = q + ATTN_BLOCK
    none = jnp.full((N_HEADS, ATTN_BLOCK, 2 * ATTN_BLOCK), NEG, F32)
    tabs = [jnp.concatenate([table(q1, k1, 1), table(q1, q1, 1)], axis=-1),
            jnp.concatenate([table(q4, k4, 4), table(q4, q4, 4)], axis=-1),
            jnp.concatenate([none, table(q16, q16, 16)], axis=-1)]
    return jnp.stack(tabs, axis=0)


def _attn_bias(bias_ref, d_idx, with_prev):
    ksl = slice(0, 2 * ATTN_BLOCK) if with_prev else slice(2 * ATTN_BLOCK, 3 * ATTN_BLOCK)
    return jnp.concatenate([bias_ref[d_idx, h, :, ksl] for h in range(HEADS_PER_TILE)], axis=0)


def _attn_blocks(blocks):
    nb = ATTN_BLOCK
    lane = lax.broadcasted_iota(I32, (nb, LANES), 1)
    head0 = lane < HEAD_DIM
    biases = [blk[3] for blk in blocks]
    blocks = [blk[:3] for blk in blocks]
    qs = [q * (HEAD_DIM ** -0.5 * LOG2E) for q, _, _ in blocks]
    qq = [jnp.concatenate([jnp.where(head0, q, 0.0), jnp.where(head0, 0.0, q)], axis=0).astype(BF16)
          for q in qs]
    ks = [k.astype(BF16) for _, k, _ in blocks]
    vs = [v.astype(BF16) for _, _, v in blocks]
    v_ones = [jnp.concatenate([v, jnp.ones_like(v)], axis=-1) for v in vs]
    s = [lax.dot_general(q2, k, (((1,), (1,)), ((), ())), preferred_element_type=F32) + bias
         for q2, k, bias in zip(qq, ks, biases)]
    m = [jnp.max(x, axis=-1, keepdims=True) for x in s]
    p = [jnp.exp2(x - mx).astype(BF16) for x, mx in zip(s, m)]
    acc = [jnp.dot(px, vo, preferred_element_type=F32) for px, vo in zip(p, v_ones)]
    head0_wide = jnp.concatenate([head0, head0], axis=-1)
    outs = []
    for a, mx in zip(acc, m):
        ol = jnp.where(head0_wide, a[:nb], a[nb:])
        l = ol[:, LANES:]
        outs.append((ol[:, :LANES] * (1.0 / l), jnp.where(head0, mx[:nb], mx[nb:]) + jnp.log2(l)))
    return outs


def _attn_kernel(qkv_ref, bias_ref, o_ref, o4_ref, l4_ref, o16_ref, l16_ref, fin_ref):
    grp = ATTN_GROUP16

    def body16(j, c):
        rs = [j * grp + a for a in range(grp)]
        bias = _attn_bias(bias_ref, 2, False)
        res = _attn_blocks([(qkv_ref[r, 0], qkv_ref[r, 1], qkv_ref[r, 2], bias) for r in rs])
        for r, (o, l) in zip(rs, res):
            o16_ref[r] = o
            l16_ref[r] = l
        return c
    for j in range(N_RES // grp):
        body16(j, 0)

    def gather4(c, rho, u0, nu):
        return jnp.concatenate([qkv_ref[rho + 4 * a, c, pl.ds(u0, nu), :] for a in range(4)], axis=0)

    def store4(rho, u0, o, l):
        for a in range(4):
            o4_ref[rho + 4 * a, pl.ds(u0, 32), :] = o[a * 32:(a + 1) * 32]
            l4_ref[rho + 4 * a, pl.ds(u0, 32), :] = l[a * 32:(a + 1) * 32]

    def body4(j, c):
        bias = _attn_bias(bias_ref, 1, True)
        todo = [(2 * j + a, 32 * n) for a in range(2) for n in range(1, 4)]
        res = _attn_blocks([(gather4(0, rho, u0, 32), gather4(1, rho, u0 - 32, 64),
                             gather4(2, rho, u0 - 32, 64), bias) for rho, u0 in todo])
        for (rho, u0), (o, l) in zip(todo, res):
            store4(rho, u0, o, l)
        return c
    for j in range(2):
        body4(j, 0)

    def gather1(ref, u0, nu, *lead):
        return jnp.concatenate([ref[(r,) + lead + (pl.ds(u0, nu), slice(None))] for r in range(N_RES)],
                               axis=0)

    def merge_and_store(u0, o1, l1):
        l4 = gather1(l4_ref, u0, 8)
        l16 = gather1(l16_ref, u0, 8)
        mx = jnp.maximum(jnp.maximum(l1, l4), l16)
        e1 = jnp.exp2(l1 - mx)
        e4 = jnp.exp2(l4 - mx)
        e16 = jnp.exp2(l16 - mx)
        inv = 1.0 / (e1 + e4 + e16)
        o = (e1 * inv) * o1 + (e4 * inv) * gather1(o4_ref, u0, 8) + (e16 * inv) * gather1(o16_ref, u0, 8)
        for r in range(N_RES):
            fin_ref[r, pl.ds(u0, 8), :] = o[r * 8:(r + 1) * 8]

    bias4 = _attn_bias(bias_ref, 1, False)
    first = [tuple(gather4(c, rho, 0, 32) for c in range(3)) + (bias4,) for rho in range(4)]
    first.append(tuple(gather1(qkv_ref, 0, 8, c) for c in range(3)) + (_attn_bias(bias_ref, 0, False),))
    res = _attn_blocks(first)
    for rho in range(4):
        store4(rho, 0, *res[rho])
    merge_and_store(0, *res[4])

    n_blocks1 = RES_LEN // 8 - 1

    def body1(j, c):
        u0s = [(1 + j * ATTN_GROUP1 + a) * 8 for a in range(ATTN_GROUP1)]
        bias = _attn_bias(bias_ref, 0, True)
        res = _attn_blocks([(gather1(qkv_ref, u0, 8, 0), gather1(qkv_ref, u0 - 8, 16, 1),
                             gather1(qkv_ref, u0 - 8, 16, 2), bias) for u0 in u0s])
        for u0, (o, l) in zip(u0s, res):
            merge_and_store(u0, o, l)
        return c
    for j in range(n_blocks1 // ATTN_GROUP1):
        body1(j, 0)

    def emit(r, c):
        o_ref[r] = fin_ref[r].astype(o_ref.dtype)
        return c
    lax.fori_loop(0, N_RES, emit, 0)


def attention(qkv):
    batch = qkv.shape[0]
    scratch = pltpu.VMEM((N_RES, RES_LEN, LANES), F32)
    return pl.pallas_call(
        _attn_kernel,
        grid=(batch, N_LANE_TILES),
        in_specs=[
            pl.BlockSpec((None, N_RES, None, 3, RES_LEN, LANES), lambda b, g: (b, 0, g, 0, 0, 0)),
            pl.BlockSpec((len(DILATIONS), HEADS_PER_TILE, ATTN_BLOCK, 3 * ATTN_BLOCK),
                         lambda b, g: (0, g, 0, 0)),
        ],
        out_specs=pl.BlockSpec((None, N_RES, None, RES_LEN, LANES), lambda b, g: (b, 0, g, 0, 0)),
        out_shape=jax.ShapeDtypeStruct((batch, N_RES, N_LANE_TILES, RES_LEN, LANES), BF16),
        scratch_shapes=[scratch] * 5,
        compiler_params=_cparams("arbitrary", "arbitrary"),
        name="dilated_attention",
    )(qkv, _attn_bias_tables())


def _split2(x):
    a = x.astype(BF16)
    b = (x - a.astype(F32)).astype(BF16)
    return a, b


def _hgrn_kernel(q_ref, f_ref, i_ref, g_ref, gamma_ref, ng_ref, tri_ref, y_ref):
    sup, ch = HGRN_SUPER, HGRN_CHUNK
    n_ch = sup // ch
    gam = gamma_ref[...]
    gmx = jnp.max(gam, axis=0, keepdims=True)
    ge = jnp.exp(gam - gmx)
    lb = ge[0:1] / jnp.sum(ge, axis=0, keepdims=True)
    ng = ng_ref[...]
    tri = tri_ref[...]
    ri = lax.broadcasted_iota(I32, (2 * sup, sup), 0) % sup
    ci = lax.broadcasted_iota(I32, (2 * sup, sup), 1)
    causal_bd2 = (ri // ch == ci // ch) & (ri >= ci)
    head0 = lax.broadcasted_iota(I32, (sup, LANES), 1) < HEAD_DIM
    row_chunk = lax.broadcasted_iota(I32, (sup, LANES), 0) // ch
    same_head = (lax.broadcasted_iota(I32, (LANES, LANES), 0) // HEAD_DIM
                 == lax.broadcasted_iota(I32, (LANES, LANES), 1) // HEAD_DIM)

    def sigmoid(x):
        return 1.0 / (1.0 + jnp.exp(-x))

    def dot_nt(a, b):
        return lax.dot_general(a, b, (((1,), (1,)), ((), ())), preferred_element_type=F32)

    def dot_tn(a, b):
        return lax.dot_general(a, b, (((0,), (0,)), ((), ())), preferred_element_type=F32)

    def expand(x):
        zero = jnp.zeros_like(x)
        return jnp.concatenate([jnp.where(row_chunk == c, x, zero) for c in range(n_ch)], axis=1)

    def body(j, state):
        grp = range(HGRN_GROUP)
        rows = [pl.ds((j * HGRN_GROUP + a) * sup, sup) for a in grp]
        fr = [f_ref[r, :].astype(F32) for r in rows]
        t = [jnp.exp(-jnp.abs(x)) for x in fr]
        rcp = [1.0 / (1.0 + x) for x in t]
        sig = [jnp.where(f >= 0, r, x * r) for f, r, x in zip(fr, rcp, t)]
        nsig = [jnp.where(f >= 0, x * r, r) for f, r, x in zip(fr, rcp, t)]
        logf = [jnp.log(lb + (1.0 - lb) * x) for x in sig]
        key = [(1.0 - lb) * x for x in nsig]
        parts = [_split2(x) for x in logf]
        b = [sum(jnp.dot(tri, p, preferred_element_type=F32) for p in ps) for ps in parts]
        totals = [[x[(c + 1) * ch - 1:(c + 1) * ch, :] for c in range(n_ch)] for x in b]
        b_last = [jnp.concatenate([jnp.broadcast_to(t_, (ch, LANES)) for t_ in ts], axis=0) for ts in totals]
        pad = jnp.zeros((LANES - n_ch, LANES), F32)
        decay_t = [jnp.exp(jnp.transpose(jnp.concatenate(ts + [pad], axis=0))) for ts in totals]
        qv = [q_ref[r, :].astype(F32) for r in rows]
        q_in = [(x * sigmoid(x) * jnp.exp(bb)).astype(BF16) for x, bb in zip(qv, b)]
        k_in = [(k * jnp.exp(-bb)).astype(BF16) for k, bb in zip(key, b)]
        k_end = [(k * jnp.exp(bl - bb)).astype(BF16) for k, bl, bb in zip(key, b_last, b)]
        vv = [i_ref[r, :] for r in rows]
        q2 = [jnp.concatenate([jnp.where(head0, x, jnp.zeros_like(x)), jnp.where(head0, jnp.zeros_like(x), x)],
                              axis=0) for x in q_in]
        att = [jnp.where(causal_bd2, dot_nt(x, k), 0.0).astype(BF16) for x, k in zip(q2, k_in)]
        intra = [jnp.dot(a_, v, preferred_element_type=F32) for a_, v in zip(att, vv)]
        o_intra = [jnp.where(head0, x[:sup], x[sup:]) for x in intra]
        ds_all = [dot_tn(k, expand(v)) for k, v in zip(k_end, vv)]
        s_stack = []
        for a in grp:
            s_prev = []
            for c in range(n_ch):
                s_prev.append(state.astype(BF16))
                ds = jnp.where(same_head, ds_all[a][:, c * LANES:(c + 1) * LANES], 0.0)
                state = decay_t[a][:, c:c + 1] * state + ds
            s_stack.append(jnp.concatenate(s_prev, axis=0))
        o_inter = [jnp.dot(expand(x), s_, preferred_element_type=F32) for x, s_ in zip(q_in, s_stack)]
        o = [x + y for x, y in zip(o_intra, o_inter)]
        sq = [x * x for x in o]
        ss0 = [jnp.sum(jnp.where(head0, x, 0.0), axis=-1, keepdims=True) for x in sq]
        ss1 = [jnp.sum(jnp.where(head0, 0.0, x), axis=-1, keepdims=True) for x in sq]
        ms = [jnp.where(head0, x, y) * (1.0 / HEAD_DIM) for x, y in zip(ss0, ss1)]
        gv = [g_ref[r, :].astype(F32) for r in rows]
        for r, x, m_, g_ in zip(rows, o, ms, gv):
            y_ref[r, :] = (x * lax.rsqrt(m_ + EPS) * ng * (g_ * sigmoid(g_))).astype(y_ref.dtype)
        return state

    state = jnp.zeros((LANES, LANES), F32)
    for j in range(SEQ // (sup * HGRN_GROUP)):
        state = body(j, state)


def hgrn(hg3, gamma, norm_g):
    batch = hg3.shape[0]
    r = jnp.arange(HGRN_SUPER)
    same = (r[:, None] // HGRN_CHUNK) == (r[None, :] // HGRN_CHUNK)
    tri = (same & (r[:, None] >= r[None, :])).astype(BF16)
    col = lambda which: (lambda b, g: (b, 0, which * N_LANE_TILES + g))
    const2 = lambda b, g: (0, 0)
    return pl.pallas_call(
        _hgrn_kernel,
        grid=(batch, N_LANE_TILES),
        in_specs=[
            pl.BlockSpec((None, SEQ, LANES), col(0)),
            pl.BlockSpec((None, SEQ, LANES), col(1)),
            pl.BlockSpec((None, SEQ, LANES), col(2)),
            pl.BlockSpec((None, SEQ, LANES), col(3)),
            pl.BlockSpec((2, LANES), lambda b, g: (0, g)),
            pl.BlockSpec((1, LANES), lambda b, g: (0, g)),
            pl.BlockSpec((HGRN_SUPER, HGRN_SUPER), const2),
        ],
        out_specs=pl.BlockSpec((None, SEQ, LANES), lambda b, g: (b, 0, g)),
        out_shape=jax.ShapeDtypeStruct((batch, SEQ, WIDTH), BF16),
        compiler_params=_cparams("arbitrary", "arbitrary"),
        name="hgrn2",
    )(hg3, hg3, hg3, hg3, gamma.astype(F32), norm_g.reshape(1, WIDTH).astype(F32), tri)


def _post_mix_kernel(oa_ref, yh_ref, x_ref, ag_ref, wo_ref, g2_ref, wrh_ref, wrl_ref, br_ref, perm_t_ref,
                     tri_ref, lower_ref,
                     x1_ref, h2_ref, rec_ref, tab_ref, carry_ref, lg_sc):
    i = pl.program_id(0)

    @pl.when(i == 0)
    def _():
        lg_sc[...] = jnp.zeros_like(lg_sc)
    _route_tile(lg_sc[...], (i > 0).astype(F32), tri_ref, lower_ref, rec_ref, tab_ref, carry_ref)

    oa = jnp.concatenate(
        [jnp.concatenate([oa_ref[r, g] for g in range(N_LANE_TILES)], axis=1)
         for r in range(N_RES)], axis=0).astype(F32)
    ms = jnp.mean(oa * oa, axis=-1, keepdims=True)
    ya = (oa * lax.rsqrt(ms + EPS) * ag_ref[...]).astype(BF16)
    ya = jnp.dot(perm_t_ref[...], ya, preferred_element_type=F32).astype(BF16)
    mix = (jnp.dot(ya, wo_ref[:WIDTH, :], preferred_element_type=F32)
           + jnp.dot(yh_ref[...], wo_ref[WIDTH:, :], preferred_element_type=F32))
    x1 = x_ref[...] + mix
    x1_ref[...] = x1
    ms2 = jnp.mean(x1 * x1, axis=-1, keepdims=True)
    h2 = x1 * lax.rsqrt(ms2 + EPS) * g2_ref[...]
    hi = h2.astype(BF16)
    h2_ref[...] = hi
    lo = (h2 - hi.astype(F32)).astype(BF16)
    nt = (((1,), (1,)), ((), ()))
    wrh = wrh_ref[...]
    lg_sc[...] = (lax.dot_general(wrh, hi, nt, preferred_element_type=F32)
                  + lax.dot_general(wrh, lo, nt, preferred_element_type=F32)
                  + lax.dot_general(wrl_ref[...], hi, nt, preferred_element_type=F32)
                  + br_ref[...])


def post_mix(oa, yh, x2, attn_g, w_out_bf16, g2, wr_hi, wr_lo, br):
    n = x2.shape[0]
    tiles_per_b = SEQ // TILE_TOKENS
    assert ROUTE_TM == TILE_TOKENS
    n_tiles = n // TILE_TOKENS
    cur = lambda i: jnp.minimum(i, n_tiles - 1)
    prev = lambda i: jnp.maximum(i - 1, 0)
    row = lambda w: pl.BlockSpec((TILE_TOKENS, w), lambda i: (cur(i), 0))
    const = lambda r, c: pl.BlockSpec((r, c), lambda i: (0, 0))
    return pl.pallas_call(
        _post_mix_kernel,
        grid=(n_tiles + 1,),
        in_specs=[pl.BlockSpec((None, N_RES, N_LANE_TILES, U_PER_TILE, LANES),
                               lambda i: (cur(i) // tiles_per_b, 0, 0, cur(i) % tiles_per_b, 0)),
                  row(WIDTH), row(D_MODEL),
                  const(1, WIDTH), const(2 * WIDTH, D_MODEL), const(1, D_MODEL),
                  const(LANES, D_MODEL), const(LANES, D_MODEL), const(LANES, 1),
                  const(TILE_TOKENS, TILE_TOKENS), const(ROUTE_TM, ROUTE_TM), const(N_EXPERTS, N_EXPERTS)],
        out_specs=[row(D_MODEL), row(D_MODEL),
                   pl.BlockSpec((8, ROUTE_TM), lambda i: (0, prev(i))),
                   pl.BlockSpec((1, 3, TABLE_ROWS, LANES), lambda i: (prev(i), 0, 0, 0))],
        out_shape=[jax.ShapeDtypeStruct((n, D_MODEL), F32),
                   jax.ShapeDtypeStruct((n, D_MODEL), BF16),
                   jax.ShapeDtypeStruct((8, n), F32),
                   jax.ShapeDtypeStruct((n_tiles, 3, TABLE_ROWS, LANES), F32)],
        scratch_shapes=[pltpu.VMEM((N_EXPERTS, LANES), F32), pltpu.VMEM((LANES, TILE_TOKENS), F32)],
        compiler_params=_cparams("arbitrary"),
        name="post_mix",
    )(oa, yh, x2, attn_g.reshape(1, WIDTH), w_out_bf16, g2.reshape(1, D_MODEL), wr_hi, wr_lo, br,
      _tile_permutation().T, *_route_constants())


ROUTE_TM = 512
EXPERT_ROW0 = 32
R_E1, R_E2, R_SLOT1, R_SLOT2, R_G1, R_G2 = 0, 1, 2, 3, 4, 5
T_CARRY, T_ROWS, T_OFF = 0, 1, 2
TABLE_ROWS = 40
TOTAL_LANE = N_EXPERTS
RUN_ALIGN = 8
LOCAL_ROWS = 2 * ROUTE_TM + N_EXPERTS * RUN_ALIGN


def _route_tile(lg, live, tri_ref, lower_ref, rec_ref, tab_ref, carry_ref):
    i = pl.program_id(0)

    @pl.when(i == 0)
    def _():
        carry_ref[...] = jnp.zeros_like(carry_ref)

    tm = lg.shape[1]
    sub8 = lax.broadcasted_iota(I32, (8, tm), 0).astype(F32)
    big = 8.0
    gmask = sub8 < N_GROUPS
    gl = jnp.where(gmask, lg[0:8], NEG)
    gmax = jnp.max(gl, axis=0, keepdims=True)
    gsel = jnp.min(jnp.where(gmask & (gl == gmax), sub8, big), axis=0, keepdims=True)
    gsum = jnp.sum(jnp.where(gmask, jnp.exp(gl - gmax), 0.0), axis=0, keepdims=True)
    w_g = 1.0 / gsum
    el = jnp.zeros((EXPERTS_PER_GROUP, tm), F32)
    for g in range(N_GROUPS):
        lo = EXPERT_ROW0 + g * EXPERTS_PER_GROUP
        el = jnp.where(gsel == g, lg[lo:lo + EXPERTS_PER_GROUP], el)
    v1 = jnp.max(el, axis=0, keepdims=True)
    i1 = jnp.min(jnp.where(el == v1, sub8, big), axis=0, keepdims=True)
    el2 = jnp.where(sub8 == i1, NEG, el)
    v2 = jnp.max(el2, axis=0, keepdims=True)
    i2 = jnp.min(jnp.where((el2 == v2) & (sub8 != i1), sub8, big), axis=0, keepdims=True)
    ex = jnp.exp(v2 - v1)
    den = 1.0 / (1.0 + ex)
    g1 = w_g * den
    g2 = w_g * ex * den
    e1 = gsel * EXPERTS_PER_GROUP + i1
    e2 = gsel * EXPERTS_PER_GROUP + i2
    sub_e = lax.broadcasted_iota(I32, (N_EXPERTS, tm), 0).astype(F32)
    oh1 = sub_e == e1
    oh2 = sub_e == e2
    onehot = (oh1 | oh2).astype(BF16)
    before = jnp.dot(onehot, tri_ref[...], preferred_element_type=F32)
    count = jnp.sum(onehot.astype(F32), axis=1, keepdims=True)
    units = jnp.floor((count + (RUN_ALIGN - 1)) * (1.0 / RUN_ALIGN)) * live
    units = jnp.broadcast_to(units, (N_EXPERTS, LANES))
    rows = units * RUN_ALIGN
    off = RUN_ALIGN * jnp.dot(lower_ref[...], units.astype(BF16), preferred_element_type=F32)
    place = off[:, 0:1] + before
    slot1 = jnp.sum(jnp.where(oh1, place, 0.0), axis=0, keepdims=True)
    slot2 = jnp.sum(jnp.where(oh2, place, 0.0), axis=0, keepdims=True)
    total = jnp.sum(rows, axis=0, keepdims=True)
    pad = jnp.zeros((TABLE_ROWS - N_EXPERTS - 1, LANES), F32)
    tab_ref[0, T_CARRY] = jnp.concatenate([carry_ref[...], total, pad], axis=0)
    tab_ref[0, T_ROWS] = jnp.concatenate([rows, total, pad], axis=0)
    tab_ref[0, T_OFF] = jnp.concatenate([off, total, pad], axis=0)
    carry_ref[...] += rows
    rec = jnp.zeros((8, tm), F32)
    for slot, val in ((R_E1, e1), (R_E2, e2), (R_SLOT1, slot1),
                      (R_SLOT2, slot2), (R_G1, g1), (R_G2, g2)):
        rec = jnp.where(sub8 == slot, val, rec)
    rec_ref[...] = rec


def _route_constants():
    r = jnp.arange(ROUTE_TM)
    tri = (r[:, None] < r[None, :]).astype(BF16)
    e = jnp.arange(N_EXPERTS)
    lower = (e[:, None] > e[None, :]).astype(BF16)
    return tri, lower


P_START, P_ROWS, P_OFF = 0, 1, 2


def _plan_kernel(carry_ref, rows_ref, off_ref, upper_ref, plan_ref, ends_ref):
    n_tiles = carry_ref.shape[0]
    lane = lax.broadcasted_iota(I32, (1, LANES), 1)
    last = pl.ds(n_tiles - 1, 1)
    totals = jnp.where(lane < N_EXPERTS, carry_ref[last, :] + rows_ref[last, :], 0.0)
    blocks = jnp.floor((totals + (FFN_BLOCK - 1)) * (1.0 / FFN_BLOCK))
    starts = FFN_BLOCK * jnp.dot(jnp.broadcast_to(blocks, (8, LANES)).astype(BF16), upper_ref[...],
                                 preferred_element_type=F32)[0:1]
    ends_ref[...] = starts + blocks * FFN_BLOCK
    plan_ref[P_START] = (carry_ref[...] + starts).astype(I32)
    plan_ref[P_ROWS] = rows_ref[...].astype(I32)
    plan_ref[P_OFF] = off_ref[...].astype(I32)


def run_plan(tab):
    n_tiles = tab.shape[0]
    rows = jnp.pad(tab[:, :, :, 0], ((0, 0), (0, 0), (0, LANES - TABLE_ROWS)))
    ln = jnp.arange(LANES)
    upper = (ln[:, None] < ln[None, :]).astype(BF16)
    return pl.pallas_call(
        _plan_kernel,
        out_shape=[jax.ShapeDtypeStruct((3, n_tiles, LANES), I32),
                   jax.ShapeDtypeStruct((1, LANES), F32)],
        compiler_params=pltpu.CompilerParams(vmem_limit_bytes=VMEM_LIMIT),
        name="run_plan",
    )(rows[:, T_CARRY], rows[:, T_ROWS], rows[:, T_OFF], upper)


RUN_PIECE = 64
SMALL_PIECES = (32, 16, 8)
TOTAL_PIECES = (1024, 512, 256, 128, 64, 32, 16, 8)
SORT_CHUNK = 256


def _for_each_run_piece(plan_ref, tile, n_tiles, fn):
    plane = n_tiles * LANES

    def per_expert(e, c):
        idx = tile * LANES + e
        start = plan_ref[P_START * plane + idx]
        rows = plan_ref[P_ROWS * plane + idx]
        off = plan_ref[P_OFF * plane + idx]

        def big(j, c2):
            fn(pl.multiple_of(off + j * RUN_PIECE, RUN_ALIGN), pl.multiple_of(start + j * RUN_PIECE, RUN_ALIGN),
               RUN_PIECE)
            return c2
        lax.fori_loop(0, lax.shift_right_logical(rows, 6), big, 0)
        for size in SMALL_PIECES:
            done = jnp.bitwise_and(rows, -2 * size)

            @pl.when(jnp.bitwise_and(rows, size) != 0)
            def _():
                fn(pl.multiple_of(off + done, RUN_ALIGN), pl.multiple_of(start + done, RUN_ALIGN), size)
        return c
    lax.fori_loop(0, N_EXPERTS, per_expert, 0)


def _for_each_total_piece(plan_ref, tile, n_tiles, fn):
    total = plan_ref[P_OFF * n_tiles * LANES + tile * LANES + TOTAL_LANE]
    for size in TOTAL_PIECES:
        @pl.when(jnp.bitwise_and(total, size) != 0)
        def _():
            fn(size)


def _dispatch_kernel(plan_ref, ends_ref, nv_ref, h2_ref, rec_ref, xs_ref, zero_buf, stage, sem, zsem):
    i = pl.program_id(0)
    n_steps = pl.num_programs(0)
    n_blocks = xs_ref.shape[0] // FFN_BLOCK

    @pl.when(i == 0)
    def _():
        zero_buf[...] = jnp.zeros_like(zero_buf)

        def zero_copy(row0):
            return pltpu.make_async_copy(zero_buf, xs_ref.at[pl.ds(row0, FFN_BLOCK)], zsem)

        def seg_end(e):
            return ends_ref[e], ends_ref[e] > jnp.where(e > 0, ends_ref[jnp.maximum(e - 1, 0)], 0)

        def start_e(e, c):
            end, nonempty = seg_end(e)

            @pl.when(nonempty)
            def _():
                zero_copy(pl.multiple_of(end - FFN_BLOCK, FFN_BLOCK)).start()
            return c
        lax.fori_loop(0, N_EXPERTS, start_e, 0)

        def start_b(blk, c):
            zero_copy(pl.multiple_of(blk * FFN_BLOCK, FFN_BLOCK)).start()
            return c
        lax.fori_loop(nv_ref[0], n_blocks, start_b, 0)

        def wait_e(e, c):
            _, nonempty = seg_end(e)

            @pl.when(nonempty)
            def _():
                zero_copy(0).wait()
            return c
        lax.fori_loop(0, N_EXPERTS, wait_e, 0)

        def wait_b(blk, c):
            zero_copy(0).wait()
            return c
        lax.fori_loop(nv_ref[0], n_blocks, wait_b, 0)

    slot = i % 2
    s1 = rec_ref[R_SLOT1:R_SLOT1 + 1, :]
    s2 = rec_ref[R_SLOT2:R_SLOT2 + 1, :]
    half = PACKED_W
    for c in range(LOCAL_ROWS // SORT_CHUNK):
        rowf = (lax.broadcasted_iota(I32, (SORT_CHUNK, ROUTE_TM), 0) + c * SORT_CHUNK).astype(F32)
        sel = ((rowf == s1) | (rowf == s2)).astype(BF16)
        pick = lambda cols: jnp.dot(sel, h2_ref[:, cols], preferred_element_type=F32)
        stage[slot, c * SORT_CHUNK:(c + 1) * SORT_CHUNK, :] = _pack_pair(pick(slice(0, half)),
                                                                         pick(slice(half, 2 * half)))

    def start_piece(lrow, grow, size):
        pltpu.make_async_copy(stage.at[slot, pl.ds(lrow, size)], xs_ref.at[pl.ds(grow, size)],
                              sem.at[slot]).start()
    _for_each_run_piece(plan_ref, i, n_steps, start_piece)

    def wait_pieces(tile, s):
        def wait_piece(size):
            pltpu.make_async_copy(stage.at[s, pl.ds(0, size)], xs_ref.at[pl.ds(0, size)], sem.at[s]).wait()
        _for_each_total_piece(plan_ref, tile, n_steps, wait_piece)

    @pl.when(i > 0)
    def _():
        wait_pieces(i - 1, 1 - slot)

    @pl.when(i + 1 == n_steps)
    def _():
        wait_pieces(i, slot)


def dispatch(h2, rec, plan, seg_ends, n_valid, n_rows):
    n = h2.shape[0]
    return pl.pallas_call(
        _dispatch_kernel,
        grid_spec=pltpu.PrefetchScalarGridSpec(
            num_scalar_prefetch=3,
            grid=(n // ROUTE_TM,),
            in_specs=[pl.BlockSpec((ROUTE_TM, D_MODEL), lambda i, *_: (i, 0)),
                      pl.BlockSpec((8, ROUTE_TM), lambda i, *_: (0, i))],
            out_specs=pl.BlockSpec(memory_space=pl.ANY),
            scratch_shapes=[pltpu.VMEM((FFN_BLOCK, PACKED_W), U32),
                            pltpu.VMEM((2, LOCAL_ROWS, PACKED_W), U32),
                            pltpu.SemaphoreType.DMA((2,)),
                            pltpu.SemaphoreType.DMA(())],
        ),
        out_shape=jax.ShapeDtypeStruct((n_rows, PACKED_W), U32),
        compiler_params=_cparams("arbitrary"),
        name="dispatch",
    )(plan, seg_ends, n_valid, h2, rec)


def _ffn_kernel(be_ref, nv_ref, xs_ref, wg_ref, wu_ref, wd_ref, ys_ref, wg_sc, wu_sc, wd_sc):
    i = pl.program_id(0)
    prev = be_ref[jnp.maximum(i - 1, 0)]
    fresh = (i == 0) | (be_ref[i] != prev)

    @pl.when(fresh)
    def _():
        wg_sc[...] = wg_ref[0].astype(BF16)
        wu_sc[...] = wu_ref[0].astype(BF16)
        wd_sc[...] = wd_ref[0].astype(BF16)

    @pl.when(i < nv_ref[0])
    def _():
        xa, xb = (t.astype(BF16) for t in _unpack_halves(xs_ref[...]))
        half = PACKED_W
        gate = (jnp.dot(xa, wg_sc[:half, :], preferred_element_type=F32)
                + jnp.dot(xb, wg_sc[half:, :], preferred_element_type=F32))
        up = (jnp.dot(xa, wu_sc[:half, :], preferred_element_type=F32)
              + jnp.dot(xb, wu_sc[half:, :], preferred_element_type=F32))
        hid = (gate * (1.0 / (1.0 + jnp.exp(-gate))) * up).astype(BF16)
        ys_ref[...] = _pack_halves(jnp.dot(hid, wd_sc[...], preferred_element_type=F32))

    @pl.when(i >= nv_ref[0])
    def _():
        ys_ref[...] = jnp.zeros_like(ys_ref)


def expert_ffn(xs, block_expert, n_valid, w_gate, w_up, w_down):
    n_rows = xs.shape[0]
    n_blocks = n_rows // FFN_BLOCK
    wmap = lambda i, be, nv: (be[i], 0, 0)
    return pl.pallas_call(
        _ffn_kernel,
        grid_spec=pltpu.PrefetchScalarGridSpec(
            num_scalar_prefetch=2,
            grid=(n_blocks,),
            in_specs=[pl.BlockSpec((FFN_BLOCK, PACKED_W), lambda i, be, nv: (jnp.minimum(i, nv[0] - 1), 0)),
                      pl.BlockSpec((1, D_MODEL, D_EXPERT), wmap),
                      pl.BlockSpec((1, D_MODEL, D_EXPERT), wmap),
                      pl.BlockSpec((1, D_EXPERT, D_MODEL), wmap)],
            out_specs=pl.BlockSpec((FFN_BLOCK, PACKED_W), lambda i, be, nv: (i, 0)),
            scratch_shapes=[pltpu.VMEM((D_MODEL, D_EXPERT), BF16),
                            pltpu.VMEM((D_MODEL, D_EXPERT), BF16),
                            pltpu.VMEM((D_EXPERT, D_MODEL), BF16)],
        ),
        out_shape=jax.ShapeDtypeStruct((n_rows, PACKED_W), U32),
        compiler_params=_cparams("arbitrary"),
        name="expert_ffn",
    )(block_expert, n_valid, xs, w_gate, w_up, w_down)


def _combine_kernel(plan_ref, x1_ref, rec_ref, gf_ref, ys_ref, out_ref, buf, sem):
    i = pl.program_id(0)
    n_steps = pl.num_programs(0)
    slot = i % 2

    def fetch_tile(tile, s):
        def start_piece(lrow, grow, size):
            pltpu.make_async_copy(ys_ref.at[pl.ds(grow, size)], buf.at[s, pl.ds(lrow, size)], sem.at[s]).start()
        _for_each_run_piece(plan_ref, tile, n_steps, start_piece)

    @pl.when(i == 0)
    def _():
        buf[...] = jnp.zeros_like(buf)
        fetch_tile(0, 0)

    @pl.when(i + 1 < n_steps)
    def _():
        fetch_tile(i + 1, 1 - slot)

    def wait_piece(size):
        pltpu.make_async_copy(ys_ref.at[pl.ds(0, size)], buf.at[slot, pl.ds(0, size)], sem.at[slot]).wait()
    _for_each_total_piece(plan_ref, i, n_steps, wait_piece)

    rowf = lax.broadcasted_iota(I32, (LOCAL_ROWS, ROUTE_TM), 0).astype(F32)
    sel = (jnp.where(rowf == rec_ref[R_SLOT1:R_SLOT1 + 1, :], rec_ref[R_G1:R_G1 + 1, :], 0.0)
           + jnp.where(rowf == rec_ref[R_SLOT2:R_SLOT2 + 1, :], rec_ref[R_G2:R_G2 + 1, :], 0.0)).astype(BF16)
    ya, yb = (t.astype(BF16) for t in _unpack_halves(buf[slot]))
    half = PACKED_W
    tn = (((0,), (0,)), ((), ()))
    xa = x1_ref[:, :half] + lax.dot_general(sel, ya, tn, preferred_element_type=F32)
    xb = x1_ref[:, half:] + lax.dot_general(sel, yb, tn, preferred_element_type=F32)
    ms = (jnp.sum(xa * xa, axis=-1, keepdims=True) + jnp.sum(xb * xb, axis=-1, keepdims=True)) * (1.0 / D_MODEL)
    scale = lax.rsqrt(ms + EPS)
    out_ref[:, :half] = xa * scale * gf_ref[:, :half]
    out_ref[:, half:] = xb * scale * gf_ref[:, half:]


def combine(x1, rec, norm_f_g, ys, plan):
    n = x1.shape[0]
    tm = ROUTE_TM
    return pl.pallas_call(
        _combine_kernel,
        grid_spec=pltpu.PrefetchScalarGridSpec(
            num_scalar_prefetch=1,
            grid=(n // tm,),
            in_specs=[pl.BlockSpec((tm, D_MODEL), lambda i, p: (i, 0)),
                      pl.BlockSpec((8, tm), lambda i, p: (0, i)),
                      pl.BlockSpec((1, D_MODEL), lambda i, p: (0, 0)),
                      pl.BlockSpec(memory_space=pl.ANY)],
            out_specs=pl.BlockSpec((tm, D_MODEL), lambda i, p: (i, 0)),
            scratch_shapes=[pltpu.VMEM((2, LOCAL_ROWS, PACKED_W), U32),
                            pltpu.SemaphoreType.DMA((2,))],
        ),
        out_shape=jax.ShapeDtypeStruct((n, D_MODEL), F32),
        compiler_params=_cparams("arbitrary"),
        name="combine",
    )(plan, x1, rec, norm_f_g.reshape(1, D_MODEL), ys)


def _router_weights(w_group, b_group, w_router, b_router):
    w = jnp.zeros((LANES, D_MODEL), F32)
    w = w.at[:N_GROUPS].set(w_group.T).at[EXPERT_ROW0:EXPERT_ROW0 + N_EXPERTS].set(w_router.T)
    b = jnp.zeros((LANES, 1), F32)
    b = b.at[:N_GROUPS, 0].set(b_group).at[EXPERT_ROW0:EXPERT_ROW0 + N_EXPERTS, 0].set(b_router)
    hi = w.astype(BF16)
    lo = (w - hi.astype(F32)).astype(BF16)
    return hi, lo, b


def _sorted_rows_bound(n_tokens):
    worst = (2 * n_tokens + (n_tokens // ROUTE_TM) * N_EXPERTS * (RUN_ALIGN - 1)
             + N_EXPERTS * (FFN_BLOCK - 1))
    return -(-worst // FFN_BLOCK) * FFN_BLOCK


def _block_plan(ends_row, n_rows):
    seg_ends = ends_row[0, :N_EXPERTS].astype(I32)
    n_blocks = n_rows // FFN_BLOCK
    blk_start = jnp.arange(n_blocks, dtype=I32) * FFN_BLOCK
    block_expert = jnp.minimum(jnp.sum((seg_ends[None, :] <= blk_start[:, None]).astype(I32), axis=1),
                               N_EXPERTS - 1)
    n_valid = seg_ends[-1:] // FFN_BLOCK
    return seg_ends, block_expert, n_valid


def kernel(x, norm1_g, w_in, attn_norm_g, hgrn_gamma, hgrn_norm_g, w_out, norm2_g, w_group, b_group,
           w_router, b_router, w_gate, w_up, w_down, norm_f_g):
    batch, seq, d = x.shape
    assert seq == SEQ and d == D_MODEL and norm1_g.shape[0] == 1
    n = batch * seq
    x2 = x.reshape(n, d)
    qkv, hg = in_proj(x2, norm1_g[0], w_in[0].astype(BF16), batch)
    oa = attention(qkv)
    yh = hgrn(hg.reshape(batch, seq, HG_W), hgrn_gamma, hgrn_norm_g[0]).reshape(n, WIDTH)
    wr_hi, wr_lo, br = _router_weights(w_group[0], b_group[0], w_router[0], b_router[0])
    x1, h2, rec, tab = post_mix(oa, yh, x2, attn_norm_g[0], w_out[0].astype(BF16), norm2_g[0],
                                wr_hi, wr_lo, br)
    plan, ends_row = run_plan(tab)
    plan = plan.reshape(-1)
    n_rows = _sorted_rows_bound(n)
    seg_ends, block_expert, n_valid = _block_plan(ends_row, n_rows)
    xs = dispatch(h2, rec, plan, seg_ends, n_valid, n_rows)
    ys = expert_ffn(xs, block_expert, n_valid, w_gate[0], w_up[0], w_down[0])
    return combine(x1, rec, norm_f_g, ys, plan).reshape(batch, seq, d)
```

```python
import jax
import jax.numpy as jnp
from jax import lax
from jax.experimental import pallas as pl
from jax.experimental.pallas import tpu as pltpu

F32 = jnp.float32
BF16 = jnp.bfloat16
I32 = jnp.int32
U32 = jnp.uint32

D_MODEL = 1024
HEAD_DIM = 64
N_HEADS = 8
WIDTH = N_HEADS * HEAD_DIM
QKV_W = 3 * WIDTH
HG_W = 4 * WIDTH
SEQ = 2048
ATTN_BLOCK = 128
DILATIONS = (1, 4, 16)
N_RES = 16
RES_LEN = SEQ // N_RES
TILE_TOKENS = 512
U_PER_TILE = TILE_TOKENS // N_RES
HGRN_CHUNK = 32
HGRN_SUPER = 256
HGRN_GROUP = 4
N_GROUPS = 4
EXPERTS_PER_GROUP = 8
N_EXPERTS = 32
D_EXPERT = 512
FFN_BLOCK = 512
PACKED_W = D_MODEL // 2
EPS = 1e-6
NEG = -1e30
LOG2E = 1.4426950408889634
LANES = 128
VMEM_LIMIT = 56 * 1024 * 1024


def _cparams(*sem):
    return pltpu.CompilerParams(dimension_semantics=sem, vmem_limit_bytes=VMEM_LIMIT)


def _pack_pair(a, b):
    hi = pltpu.bitcast(a.astype(BF16).astype(F32), U32)
    lo = pltpu.bitcast(b.astype(BF16).astype(F32), U32)
    return hi | (lo >> 16)


def _pack_halves(x):
    w = x.shape[1] // 2
    return _pack_pair(x[:, :w], x[:, w:])


def _unpack_halves(p):
    return pltpu.bitcast(p & jnp.uint32(0xFFFF0000), F32), pltpu.bitcast(p << 16, F32)


N_LANE_TILES = WIDTH // LANES


def _tile_permutation():
    i = jnp.arange(TILE_TOKENS)
    src = N_RES * (i % U_PER_TILE) + i // U_PER_TILE
    return (src[:, None] == jnp.arange(TILE_TOKENS)[None, :]).astype(BF16)


def _in_proj_kernel(x_ref, g_ref, w_ref, perm_ref, qkv_ref, hg_ref):
    x = x_ref[...]
    ms = jnp.mean(x * x, axis=-1, keepdims=True)
    h = (x * lax.rsqrt(ms + EPS) * g_ref[...]).astype(BF16)
    hp = jnp.dot(perm_ref[...], h, preferred_element_type=F32).astype(BF16)
    for c in range(QKV_W // WIDTH):
        res = jnp.dot(hp, w_ref[:, c * WIDTH:(c + 1) * WIDTH], preferred_element_type=F32)
        for r in range(N_RES):
            for g in range(N_LANE_TILES):
                qkv_ref[r, g, c] = res[r * U_PER_TILE:(r + 1) * U_PER_TILE, g * LANES:(g + 1) * LANES]
    for j in range(HG_W // WIDTH):
        wsl = slice(QKV_W + j * WIDTH, QKV_W + (j + 1) * WIDTH)
        hg_ref[:, j * WIDTH:(j + 1) * WIDTH] = jnp.dot(h, w_ref[:, wsl], preferred_element_type=F32).astype(BF16)


def in_proj(x2, g, w_bf16, batch):
    n = x2.shape[0]
    tiles_per_b = SEQ // TILE_TOKENS
    return pl.pallas_call(
        _in_proj_kernel,
        grid=(n // TILE_TOKENS,),
        in_specs=[
            pl.BlockSpec((TILE_TOKENS, D_MODEL), lambda i: (i, 0)),
            pl.BlockSpec((1, D_MODEL), lambda i: (0, 0)),
            pl.BlockSpec((D_MODEL, QKV_W + HG_W), lambda i: (0, 0)),
            pl.BlockSpec((TILE_TOKENS, TILE_TOKENS), lambda i: (0, 0)),
        ],
        out_specs=[
            pl.BlockSpec((None, N_RES, N_LANE_TILES, 3, U_PER_TILE, LANES),
                         lambda i: (i // tiles_per_b, 0, 0, 0, i % tiles_per_b, 0)),
            pl.BlockSpec((TILE_TOKENS, HG_W), lambda i: (i, 0)),
        ],
        out_shape=[
            jax.ShapeDtypeStruct((batch, N_RES, N_LANE_TILES, 3, RES_LEN, LANES), F32),
            jax.ShapeDtypeStruct((n, HG_W), BF16),
        ],
        compiler_params=_cparams("arbitrary"),
        name="in_proj",
    )(x2, g.reshape(1, D_MODEL), w_bf16, _tile_permutation())


ATTN_GROUP16 = 8
ATTN_GROUP1 = 5
HEADS_PER_TILE = LANES // HEAD_DIM


def _attn_bias_tables():
    slopes = 2.0 ** (-8.0 * jnp.arange(1, N_HEADS + 1, dtype=F32) / N_HEADS)

    def table(qpos, kpos, dil):
        sd = qpos[:, None] - kpos[None, :]
        b = -slopes[:, None, None] * (sd * dil).astype(F32)[None] * LOG2E
        return jnp.where(((sd >= 0) & (sd <= ATTN_BLOCK))[None], b, NEG)

    q = jnp.arange(ATTN_BLOCK)
    k = jnp.arange(2 * ATTN_BLOCK)
    q1 = 16 * (q % 8) + q // 8 + ATTN_BLOCK
    k1 = 16 * (k % 16) + k // 16
    q4 = 4 * (q % 32) + q // 32 + ATTN_BLOCK
    k4 = 4 * (k % 64) + k // 64
    q16 = q + ATTN_BLOCK
    none = jnp.full((N_HEADS, ATTN_BLOCK, 2 * ATTN_BLOCK), NEG, F32)
    tabs = [jnp.concatenate([table(q1, k1, 1), table(q1, q1, 1)], axis=-1),
            jnp.concatenate([table(q4, k4, 4), table(q4, q4, 4)], axis=-1),
            jnp.concatenate([none, table(q16, q16, 16)], axis=-1)]
    return jnp.stack(tabs, axis=0)


def _attn_bias(bias_ref, d_idx, with_prev):
    ksl = slice(0, 2 * ATTN_BLOCK) if with_prev else slice(2 * ATTN_BLOCK, 3 * ATTN_BLOCK)
    return jnp.concatenate([bias_ref[d_idx, h, :, ksl] for h in range(HEADS_PER_TILE)], axis=0)


def _attn_blocks(blocks):
    nb = ATTN_BLOCK
    lane = lax.broadcasted_iota(I32, (nb, LANES), 1)
    head0 = lane < HEAD_DIM
    biases = [blk[3] for blk in blocks]
    blocks = [blk[:3] for blk in blocks]
    qs = [q * (HEAD_DIM ** -0.5 * LOG2E) for q, _, _ in blocks]
    qq = [jnp.concatenate([jnp.where(head0, q, 0.0), jnp.where(head0, 0.0, q)], axis=0).astype(BF16)
          for q in qs]
    ks = [k.astype(BF16) for _, k, _ in blocks]
    vs = [v.astype(BF16) for _, _, v in blocks]
    v_ones = [jnp.concatenate([v, jnp.ones_like(v)], axis=-1) for v in vs]
    s = [lax.dot_general(q2, k, (((1,), (1,)), ((), ())), preferred_element_type=F32) + bias
         for q2, k, bias in zip(qq, ks, biases)]
    m = [jnp.max(x, axis=-1, keepdims=True) for x in s]
    p = [jnp.exp2(x - mx).astype(BF16) for x, mx in zip(s, m)]
    acc = [jnp.dot(px, vo, preferred_element_type=F32) for px, vo in zip(p, v_ones)]
    head0_wide = jnp.concatenate([head0, head0], axis=-1)
    outs = []
    for a, mx in zip(acc, m):
        ol = jnp.where(head0_wide, a[:nb], a[nb:])
        l = ol[:, LANES:]
        outs.append((ol[:, :LANES] * (1.0 / l), jnp.where(head0, mx[:nb], mx[nb:]) + jnp.log2(l)))
    return outs


def _attn_kernel(qkv_ref, bias_ref, o_ref, o4_ref, l4_ref, o16_ref, l16_ref, fin_ref):
    grp = ATTN_GROUP16

    def body16(j, c):
        rs = [j * grp + a for a in range(grp)]
        bias = _attn_bias(bias_ref, 2, False)
        res = _attn_blocks([(qkv_ref[r, 0], qkv_ref[r, 1], qkv_ref[r, 2], bias) for r in rs])
        for r, (o, l) in zip(rs, res):
            o16_ref[r] = o
            l16_ref[r] = l
        return c
    for j in range(N_RES // grp):
        body16(j, 0)

    def gather4(c, rho, u0, nu):
        return jnp.concatenate([qkv_ref[rho + 4 * a, c, pl.ds(u0, nu), :] for a in range(4)], axis=0)

    def store4(rho, u0, o, l):
        for a in range(4):
            o4_ref[rho + 4 * a, pl.ds(u0, 32), :] = o[a * 32:(a + 1) * 32]
            l4_ref[rho + 4 * a, pl.ds(u0, 32), :] = l[a * 32:(a + 1) * 32]

    def body4(j, c):
        bias = _attn_bias(bias_ref, 1, True)
        todo = [(2 * j + a, 32 * n) for a in range(2) for n in range(1, 4)]
        res = _attn_blocks([(gather4(0, rho, u0, 32), gather4(1, rho, u0 - 32, 64),
                             gather4(2, rho, u0 - 32, 64), bias) for rho, u0 in todo])
        for (rho, u0), (o, l) in zip(todo, res):
            store4(rho, u0, o, l)
        return c
    for j in range(2):
        body4(j, 0)

    def gather1(ref, u0, nu, *lead):
        return jnp.concatenate([ref[(r,) + lead + (pl.ds(u0, nu), slice(None))] for r in range(N_RES)],
                               axis=0)

    def merge_and_store(u0, o1, l1):
        l4 = gather1(l4_ref, u0, 8)
        l16 = gather1(l16_ref, u0, 8)
        mx = jnp.maximum(jnp.maximum(l1, l4), l16)
        e1 = jnp.exp2(l1 - mx)
        e4 = jnp.exp2(l4 - mx)
        e16 = jnp.exp2(l16 - mx)
        inv = 1.0 / (e1 + e4 + e16)
        o = (e1 * inv) * o1 + (e4 * inv) * gather1(o4_ref, u0, 8) + (e16 * inv) * gather1(o16_ref, u0, 8)
        for r in range(N_RES):
            fin_ref[r, pl.ds(u0, 8), :] = o[r * 8:(r + 1) * 8]

    bias4 = _attn_bias(bias_ref, 1, False)
    first = [tuple(gather4(c, rho, 0, 32) for c in range(3)) + (bias4,) for rho in range(4)]
    first.append(tuple(gather1(qkv_ref, 0, 8, c) for c in range(3)) + (_attn_bias(bias_ref, 0, False),))
    res = _attn_blocks(first)
    for rho in range(4):
        store4(rho, 0, *res[rho])
    merge_and_store(0, *res[4])

    n_blocks1 = RES_LEN // 8 - 1

    def body1(j, c):
        u0s = [(1 + j * ATTN_GROUP1 + a) * 8 for a in range(ATTN_GROUP1)]
        bias = _attn_bias(bias_ref, 0, True)
        res = _attn_blocks([(gather1(qkv_ref, u0, 8, 0), gather1(qkv_ref, u0 - 8, 16, 1),
                             gather1(qkv_ref, u0 - 8, 16, 2), bias) for u0 in u0s])
        for u0, (o, l) in zip(u0s, res):
            merge_and_store(u0, o, l)
        return c
    for j in range(n_blocks1 // ATTN_GROUP1):
        body1(j, 0)

    def emit(r, c):
        o_ref[r] = fin_ref[r].astype(o_ref.dtype)
        return c
    lax.fori_loop(0, N_RES, emit, 0)


def attention(qkv):
    batch = qkv.shape[0]
    scratch = pltpu.VMEM((N_RES, RES_LEN, LANES), F32)
    return pl.pallas_call(
        _attn_kernel,
        grid=(batch, N_LANE_TILES),
        in_specs=[
            pl.BlockSpec((None, N_RES, None, 3, RES_LEN, LANES), lambda b, g: (b, 0, g, 0, 0, 0)),
            pl.BlockSpec((len(DILATIONS), HEADS_PER_TILE, ATTN_BLOCK, 3 * ATTN_BLOCK),
                         lambda b, g: (0, g, 0, 0)),
        ],
        out_specs=pl.BlockSpec((None, N_RES, None, RES_LEN, LANES), lambda b, g: (b, 0, g, 0, 0)),
        out_shape=jax.ShapeDtypeStruct((batch, N_RES, N_LANE_TILES, RES_LEN, LANES), BF16),
        scratch_shapes=[scratch] * 5,
        compiler_params=_cparams("arbitrary", "arbitrary"),
        name="dilated_attention",
    )(qkv, _attn_bias_tables())


def _split2(x):
    a = x.astype(BF16)
    b = (x - a.astype(F32)).astype(BF16)
    return a, b


def _hgrn_kernel(q_ref, f_ref, i_ref, g_ref, gamma_ref, ng_ref, tri_ref, y_ref):
    sup, ch = HGRN_SUPER, HGRN_CHUNK
    n_ch = sup // ch
    gam = gamma_ref[...]
    gmx = jnp.max(gam, axis=0, keepdims=True)
    ge = jnp.exp(gam - gmx)
    lb = ge[0:1] / jnp.sum(ge, axis=0, keepdims=True)
    ng = ng_ref[...]
    tri = tri_ref[...]
    ri = lax.broadcasted_iota(I32, (2 * sup, sup), 0) % sup
    ci = lax.broadcasted_iota(I32, (2 * sup, sup), 1)
    causal_bd2 = (ri // ch == ci // ch) & (ri >= ci)
    head0 = lax.broadcasted_iota(I32, (sup, LANES), 1) < HEAD_DIM
    row_chunk = lax.broadcasted_iota(I32, (sup, LANES), 0) // ch
    same_head = (lax.broadcasted_iota(I32, (LANES, LANES), 0) // HEAD_DIM
                 == lax.broadcasted_iota(I32, (LANES, LANES), 1) // HEAD_DIM)

    def sigmoid(x):
        return 1.0 / (1.0 + jnp.exp(-x))

    def dot_nt(a, b):
        return lax.dot_general(a, b, (((1,), (1,)), ((), ())), preferred_element_type=F32)

    def dot_tn(a, b):
        return lax.dot_general(a, b, (((0,), (0,)), ((), ())), preferred_element_type=F32)

    def expand(x):
        zero = jnp.zeros_like(x)
        return jnp.concatenate([jnp.where(row_chunk == c, x, zero) for c in range(n_ch)], axis=1)

    def body(j, state):
        grp = range(HGRN_GROUP)
        rows = [pl.ds((j * HGRN_GROUP + a) * sup, sup) for a in grp]
        fr = [f_ref[r, :].astype(F32) for r in rows]
        t = [jnp.exp(-jnp.abs(x)) for x in fr]
        rcp = [1.0 / (1.0 + x) for x in t]
        sig = [jnp.where(f >= 0, r, x * r) for f, r, x in zip(fr, rcp, t)]
        nsig = [jnp.where(f >= 0, x * r, r) for f, r, x in zip(fr, rcp, t)]
        logf = [jnp.log(lb + (1.0 - lb) * x) for x in sig]
        key = [(1.0 - lb) * x for x in nsig]
        parts = [_split2(x) for x in logf]
        b = [sum(jnp.dot(tri, p, preferred_element_type=F32) for p in ps) for ps in parts]
        totals = [[x[(c + 1) * ch - 1:(c + 1) * ch, :] for c in range(n_ch)] for x in b]
        b_last = [jnp.concatenate([jnp.broadcast_to(t_, (ch, LANES)) for t_ in ts], axis=0) for ts in totals]
        pad = jnp.zeros((LANES - n_ch, LANES), F32)
        decay_t = [jnp.exp(jnp.transpose(jnp.concatenate(ts + [pad], axis=0))) for ts in totals]
        qv = [q_ref[r, :].astype(F32) for r in rows]
        q_in = [(x * sigmoid(x) * jnp.exp(bb)).astype(BF16) for x, bb in zip(qv, b)]
        k_in = [(k * jnp.exp(-bb)).astype(BF16) for k, bb in zip(key, b)]
        k_end = [(k * jnp.exp(bl - bb)).astype(BF16) for k, bl, bb in zip(key, b_last, b)]
        vv = [i_ref[r, :] for r in rows]
        q2 = [jnp.concatenate([jnp.where(head0, x, jnp.zeros_like(x)), jnp.where(head0, jnp.zeros_like(x), x)],
                              axis=0) for x in q_in]
        att = [jnp.where(causal_bd2, dot_nt(x, k), 0.0).astype(BF16) for x, k in zip(q2, k_in)]
        intra = [jnp.dot(a_, v, preferred_element_type=F32) for a_, v in zip(att, vv)]
        o_intra = [jnp.where(head0, x[:sup], x[sup:]) for x in intra]
        ds_all = [dot_tn(k, expand(v)) for k, v in zip(k_end, vv)]
        s_prev = []
        for a in grp:
            for c in range(n_ch):
                s_prev.append(state.astype(BF16))
                ds = jnp.where(same_head, ds_all[a][:, c * LANES:(c + 1) * LANES], 0.0)
                state = decay_t[a][:, c:c + 1] * state + ds
        o_inter = [jnp.concatenate(
            [jnp.dot(q_in[a][c * ch:(c + 1) * ch], s_prev[a * n_ch + c], preferred_element_type=F32)
             for c in range(n_ch)], axis=0) for a in grp]
        o = [x + y for x, y in zip(o_intra, o_inter)]
        sq = [x * x for x in o]
        ss0 = [jnp.sum(jnp.where(head0, x, 0.0), axis=-1, keepdims=True) for x in sq]
        ss1 = [jnp.sum(jnp.where(head0, 0.0, x), axis=-1, keepdims=True) for x in sq]
        ms = [jnp.where(head0, x, y) * (1.0 / HEAD_DIM) for x, y in zip(ss0, ss1)]
        gv = [g_ref[r, :].astype(F32) for r in rows]
        for r, x, m_, g_ in zip(rows, o, ms, gv):
            y_ref[r, :] = (x * lax.rsqrt(m_ + EPS) * ng * (g_ * sigmoid(g_))).astype(y_ref.dtype)
        return state

    state = jnp.zeros((LANES, LANES), F32)
    for j in range(SEQ // (sup * HGRN_GROUP)):
        state = body(j, state)


def hgrn(hg3, gamma, norm_g):
    batch = hg3.shape[0]
    r = jnp.arange(HGRN_SUPER)
    same = (r[:, None] // HGRN_CHUNK) == (r[None, :] // HGRN_CHUNK)
    tri = (same & (r[:, None] >= r[None, :])).astype(BF16)
    col = lambda which: (lambda b, g: (b, 0, which * N_LANE_TILES + g))
    const2 = lambda b, g: (0, 0)
    return pl.pallas_call(
        _hgrn_kernel,
        grid=(batch, N_LANE_TILES),
        in_specs=[
            pl.BlockSpec((None, SEQ, LANES), col(0)),
            pl.BlockSpec((None, SEQ, LANES), col(1)),
            pl.BlockSpec((None, SEQ, LANES), col(2)),
            pl.BlockSpec((None, SEQ, LANES), col(3)),
            pl.BlockSpec((2, LANES), lambda b, g: (0, g)),
            pl.BlockSpec((1, LANES), lambda b, g: (0, g)),
            pl.BlockSpec((HGRN_SUPER, HGRN_SUPER), const2),
        ],
        out_specs=pl.BlockSpec((None, SEQ, LANES), lambda b, g: (b, 0, g)),
        out_shape=jax.ShapeDtypeStruct((batch, SEQ, WIDTH), BF16),
        compiler_params=_cparams("arbitrary", "arbitrary"),
        name="hgrn2",
    )(hg3, hg3, hg3, hg3, gamma.astype(F32), norm_g.reshape(1, WIDTH).astype(F32), tri)


def _post_mix_kernel(oa_ref, yh_ref, x_ref, ag_ref, wo_ref, g2_ref, wrh_ref, wrl_ref, br_ref, perm_t_ref,
                     tri_ref, lower_ref,
                     x1_ref, h2_ref, rec_ref, tab_ref, carry_ref, lg_sc):
    i = pl.program_id(0)

    @pl.when(i == 0)
    def _():
        lg_sc[...] = jnp.zeros_like(lg_sc)
    _route_tile(lg_sc[...], (i > 0).astype(F32), tri_ref, lower_ref, rec_ref, tab_ref, carry_ref)

    oa = jnp.concatenate(
        [jnp.concatenate([oa_ref[r, g] for g in range(N_LANE_TILES)], axis=1)
         for r in range(N_RES)], axis=0).astype(F32)
    ms = jnp.mean(oa * oa, axis=-1, keepdims=True)
    ya = (oa * lax.rsqrt(ms + EPS) * ag_ref[...]).astype(BF16)
    ya = jnp.dot(perm_t_ref[...], ya, preferred_element_type=F32).astype(BF16)
    mix = (jnp.dot(ya, wo_ref[:WIDTH, :], preferred_element_type=F32)
           + jnp.dot(yh_ref[...], wo_ref[WIDTH:, :], preferred_element_type=F32))
    x1 = x_ref[...] + mix
    x1_ref[...] = x1
    ms2 = jnp.mean(x1 * x1, axis=-1, keepdims=True)
    h2 = x1 * lax.rsqrt(ms2 + EPS) * g2_ref[...]
    hi = h2.astype(BF16)
    h2_ref[...] = hi
    lo = (h2 - hi.astype(F32)).astype(BF16)
    nt = (((1,), (1,)), ((), ()))
    wrh = wrh_ref[...]
    lg_sc[...] = (lax.dot_general(wrh, hi, nt, preferred_element_type=F32)
                  + lax.dot_general(wrh, lo, nt, preferred_element_type=F32)
                  + lax.dot_general(wrl_ref[...], hi, nt, preferred_element_type=F32)
                  + br_ref[...])


def post_mix(oa, yh, x2, attn_g, w_out_bf16, g2, wr_hi, wr_lo, br):
    n = x2.shape[0]
    tiles_per_b = SEQ // TILE_TOKENS
    assert ROUTE_TM == TILE_TOKENS
    n_tiles = n // TILE_TOKENS
    cur = lambda i: jnp.minimum(i, n_tiles - 1)
    prev = lambda i: jnp.maximum(i - 1, 0)
    row = lambda w: pl.BlockSpec((TILE_TOKENS, w), lambda i: (cur(i), 0))
    const = lambda r, c: pl.BlockSpec((r, c), lambda i: (0, 0))
    return pl.pallas_call(
        _post_mix_kernel,
        grid=(n_tiles + 1,),
        in_specs=[pl.BlockSpec((None, N_RES, N_LANE_TILES, U_PER_TILE, LANES),
                               lambda i: (cur(i) // tiles_per_b, 0, 0, cur(i) % tiles_per_b, 0)),
                  row(WIDTH), row(D_MODEL),
                  const(1, WIDTH), const(2 * WIDTH, D_MODEL), const(1, D_MODEL),
                  const(LANES, D_MODEL), const(LANES, D_MODEL), const(LANES, 1),
                  const(TILE_TOKENS, TILE_TOKENS), const(ROUTE_TM, ROUTE_TM), const(N_EXPERTS, N_EXPERTS)],
        out_specs=[row(D_MODEL), row(D_MODEL),
                   pl.BlockSpec((8, ROUTE_TM), lambda i: (0, prev(i))),
                   pl.BlockSpec((1, 8, LANES), lambda i: (prev(i), 0, 0))],
        out_shape=[jax.ShapeDtypeStruct((n, D_MODEL), F32),
                   jax.ShapeDtypeStruct((n, D_MODEL), BF16),
                   jax.ShapeDtypeStruct((8, n), F32),
                   jax.ShapeDtypeStruct((n_tiles, 8, LANES), F32)],
        scratch_shapes=[pltpu.VMEM((N_EXPERTS, LANES), F32), pltpu.VMEM((LANES, TILE_TOKENS), F32)],
        compiler_params=_cparams("arbitrary"),
        name="post_mix",
    )(oa, yh, x2, attn_g.reshape(1, WIDTH), w_out_bf16, g2.reshape(1, D_MODEL), wr_hi, wr_lo, br,
      _tile_permutation().T, *_route_constants())


ROUTE_TM = 512
EXPERT_ROW0 = 32
R_E1, R_E2, R_SLOT1, R_SLOT2, R_G1, R_G2 = 0, 1, 2, 3, 4, 5
T_CARRY, T_ROWS, T_OFF = 0, 1, 2
SEG_END, SEG_USED_END = 0, 1
TOTAL_LANE = N_EXPERTS
RUN_ALIGN = 8
LOCAL_ROWS = 2 * ROUTE_TM + N_EXPERTS * RUN_ALIGN


def _route_tile(lg, live, tri_ref, lower_ref, rec_ref, tab_ref, carry_ref):
    i = pl.program_id(0)

    @pl.when(i == 0)
    def _():
        carry_ref[...] = jnp.zeros_like(carry_ref)

    tm = lg.shape[1]
    sub8 = lax.broadcasted_iota(I32, (8, tm), 0).astype(F32)
    big = 8.0
    gmask = sub8 < N_GROUPS
    gl = jnp.where(gmask, lg[0:8], NEG)
    gmax = jnp.max(gl, axis=0, keepdims=True)
    gsel = jnp.min(jnp.where(gmask & (gl == gmax), sub8, big), axis=0, keepdims=True)
    gsum = jnp.sum(jnp.where(gmask, jnp.exp(gl - gmax), 0.0), axis=0, keepdims=True)
    w_g = 1.0 / gsum
    el = jnp.zeros((EXPERTS_PER_GROUP, tm), F32)
    for g in range(N_GROUPS):
        lo = EXPERT_ROW0 + g * EXPERTS_PER_GROUP
        el = jnp.where(gsel == g, lg[lo:lo + EXPERTS_PER_GROUP], el)
    v1 = jnp.max(el, axis=0, keepdims=True)
    i1 = jnp.min(jnp.where(el == v1, sub8, big), axis=0, keepdims=True)
    el2 = jnp.where(sub8 == i1, NEG, el)
    v2 = jnp.max(el2, axis=0, keepdims=True)
    i2 = jnp.min(jnp.where((el2 == v2) & (sub8 != i1), sub8, big), axis=0, keepdims=True)
    ex = jnp.exp(v2 - v1)
    den = 1.0 / (1.0 + ex)
    g1 = w_g * den
    g2 = w_g * ex * den
    e1 = gsel * EXPERTS_PER_GROUP + i1
    e2 = gsel * EXPERTS_PER_GROUP + i2
    sub_e = lax.broadcasted_iota(I32, (N_EXPERTS, tm), 0).astype(F32)
    oh1 = sub_e == e1
    oh2 = sub_e == e2
    onehot = (oh1 | oh2).astype(BF16)
    before = jnp.dot(onehot, tri_ref[...], preferred_element_type=F32)
    count = jnp.sum(onehot.astype(F32), axis=1, keepdims=True)
    units = jnp.floor((count + (RUN_ALIGN - 1)) * (1.0 / RUN_ALIGN)) * live
    units = jnp.broadcast_to(units, (N_EXPERTS, LANES))
    rows = units * RUN_ALIGN
    off = RUN_ALIGN * jnp.dot(lower_ref[...], units.astype(BF16), preferred_element_type=F32)
    place = off[:, 0:1] + before
    slot1 = jnp.sum(jnp.where(oh1, place, 0.0), axis=0, keepdims=True)
    slot2 = jnp.sum(jnp.where(oh2, place, 0.0), axis=0, keepdims=True)
    total = jnp.sum(rows, axis=0, keepdims=True)
    eye = (lax.broadcasted_iota(I32, (N_EXPERTS, LANES), 0) == lax.broadcasted_iota(I32, (N_EXPERTS, LANES), 1))
    lane1 = lax.broadcasted_iota(I32, (1, LANES), 1)
    as_row = lambda col: jnp.where(lane1 == TOTAL_LANE, total,
                                   jnp.sum(jnp.where(eye, col, 0.0), axis=0, keepdims=True))
    sub = lax.broadcasted_iota(I32, (8, LANES), 0)
    tab_ref[0] = jnp.where(sub == T_CARRY, as_row(carry_ref[...]),
                           jnp.where(sub == T_ROWS, as_row(rows), jnp.where(sub == T_OFF, as_row(off), 0.0)))
    carry_ref[...] += rows
    rec = jnp.zeros((8, tm), F32)
    for slot, val in ((R_E1, e1), (R_E2, e2), (R_SLOT1, slot1),
                      (R_SLOT2, slot2), (R_G1, g1), (R_G2, g2)):
        rec = jnp.where(sub8 == slot, val, rec)
    rec_ref[...] = rec


def _route_constants():
    r = jnp.arange(ROUTE_TM)
    tri = (r[:, None] < r[None, :]).astype(BF16)
    e = jnp.arange(N_EXPERTS)
    lower = (e[:, None] > e[None, :]).astype(BF16)
    return tri, lower


P_START, P_ROWS, P_OFF = 0, 1, 2


def _plan_kernel(carry_ref, rows_ref, off_ref, upper_ref, plan_ref, seg_ref):
    n_tiles = carry_ref.shape[0]
    lane = lax.broadcasted_iota(I32, (1, LANES), 1)
    last = pl.ds(n_tiles - 1, 1)
    totals = jnp.where(lane < N_EXPERTS, carry_ref[last, :] + rows_ref[last, :], 0.0)
    blocks = jnp.floor((totals + (FFN_BLOCK - 1)) * (1.0 / FFN_BLOCK))
    starts = FFN_BLOCK * jnp.dot(jnp.broadcast_to(blocks, (8, LANES)).astype(BF16), upper_ref[...],
                                 preferred_element_type=F32)[0:1]
    sub = lax.broadcasted_iota(I32, (8, LANES), 0)
    seg_ref[...] = jnp.where(sub == SEG_END, starts + blocks * FFN_BLOCK,
                             jnp.where(sub == SEG_USED_END, starts + totals, 0.0))
    plan_ref[P_START] = (carry_ref[...] + starts).astype(I32)
    plan_ref[P_ROWS] = rows_ref[...].astype(I32)
    plan_ref[P_OFF] = off_ref[...].astype(I32)


def run_plan(tab):
    n_tiles = tab.shape[0]
    ln = jnp.arange(LANES)
    upper = (ln[:, None] < ln[None, :]).astype(BF16)
    return pl.pallas_call(
        _plan_kernel,
        out_shape=[jax.ShapeDtypeStruct((3, n_tiles, LANES), I32),
                   jax.ShapeDtypeStruct((8, LANES), F32)],
        compiler_params=pltpu.CompilerParams(vmem_limit_bytes=VMEM_LIMIT),
        name="run_plan",
    )(tab[:, T_CARRY], tab[:, T_ROWS], tab[:, T_OFF], upper)


RUN_PIECE = 64
SMALL_PIECES = (32, 16, 8)
TOTAL_PIECES = (1024, 512, 256, 128, 64, 32, 16, 8)
SORT_CHUNK = 256


def _for_each_run_piece(plan_ref, tile, n_tiles, fn):
    plane = n_tiles * LANES

    def per_expert(e, c):
        idx = tile * LANES + e
        start = plan_ref[P_START * plane + idx]
        rows = plan_ref[P_ROWS * plane + idx]
        off = plan_ref[P_OFF * plane + idx]

        def big(j, c2):
            fn(pl.multiple_of(off + j * RUN_PIECE, RUN_ALIGN), pl.multiple_of(start + j * RUN_PIECE, RUN_ALIGN),
               RUN_PIECE)
            return c2
        lax.fori_loop(0, lax.shift_right_logical(rows, 6), big, 0)
        for size in SMALL_PIECES:
            done = jnp.bitwise_and(rows, -2 * size)

            @pl.when(jnp.bitwise_and(rows, size) != 0)
            def _():
                fn(pl.multiple_of(off + done, RUN_ALIGN), pl.multiple_of(start + done, RUN_ALIGN), size)
        return c
    lax.fori_loop(0, N_EXPERTS, per_expert, 0)


def _for_each_total_piece(plan_ref, tile, n_tiles, fn):
    total = plan_ref[P_OFF * n_tiles * LANES + tile * LANES + TOTAL_LANE]
    for size in TOTAL_PIECES:
        @pl.when(jnp.bitwise_and(total, size) != 0)
        def _():
            fn(size)


def _dispatch_kernel(plan_ref, ends_ref, nv_ref, h2_ref, rec_ref, xs_ref, zero_buf, stage, sem, zsem):
    i = pl.program_id(0)
    n_steps = pl.num_programs(0)
    n_blocks = xs_ref.shape[0] // FFN_BLOCK

    @pl.when(i == 0)
    def _():
        zero_buf[...] = jnp.zeros_like(zero_buf)

        def zero_copy(row0):
            return pltpu.make_async_copy(zero_buf, xs_ref.at[pl.ds(row0, FFN_BLOCK)], zsem)

        def seg_end(e):
            return ends_ref[e], ends_ref[e] > jnp.where(e > 0, ends_ref[jnp.maximum(e - 1, 0)], 0)

        def start_e(e, c):
            end, nonempty = seg_end(e)

            @pl.when(nonempty)
            def _():
                zero_copy(pl.multiple_of(end - FFN_BLOCK, FFN_BLOCK)).start()
            return c
        lax.fori_loop(0, N_EXPERTS, start_e, 0)

        def start_b(blk, c):
            zero_copy(pl.multiple_of(blk * FFN_BLOCK, FFN_BLOCK)).start()
            return c
        lax.fori_loop(nv_ref[0], n_blocks, start_b, 0)

        def wait_e(e, c):
            _, nonempty = seg_end(e)

            @pl.when(nonempty)
            def _():
                zero_copy(0).wait()
            return c
        lax.fori_loop(0, N_EXPERTS, wait_e, 0)

        def wait_b(blk, c):
            zero_copy(0).wait()
            return c
        lax.fori_loop(nv_ref[0], n_blocks, wait_b, 0)

    slot = i % 2
    s1 = rec_ref[R_SLOT1:R_SLOT1 + 1, :]
    s2 = rec_ref[R_SLOT2:R_SLOT2 + 1, :]
    half = PACKED_W
    for c in range(LOCAL_ROWS // SORT_CHUNK):
        rowf = (lax.broadcasted_iota(I32, (SORT_CHUNK, ROUTE_TM), 0) + c * SORT_CHUNK).astype(F32)
        sel = ((rowf == s1) | (rowf == s2)).astype(BF16)
        pick = lambda cols: jnp.dot(sel, h2_ref[:, cols], preferred_element_type=F32)
        stage[slot, c * SORT_CHUNK:(c + 1) * SORT_CHUNK, :] = _pack_pair(pick(slice(0, half)),
                                                                         pick(slice(half, 2 * half)))

    def start_piece(lrow, grow, size):
        pltpu.make_async_copy(stage.at[slot, pl.ds(lrow, size)], xs_ref.at[pl.ds(grow, size)],
                              sem.at[slot]).start()
    _for_each_run_piece(plan_ref, i, n_steps, start_piece)

    def wait_pieces(tile, s):
        def wait_piece(size):
            pltpu.make_async_copy(stage.at[s, pl.ds(0, size)], xs_ref.at[pl.ds(0, size)], sem.at[s]).wait()
        _for_each_total_piece(plan_ref, tile, n_steps, wait_piece)

    @pl.when(i > 0)
    def _():
        wait_pieces(i - 1, 1 - slot)

    @pl.when(i + 1 == n_steps)
    def _():
        wait_pieces(i, slot)


def dispatch(h2, rec, plan, seg_ends, n_valid, n_rows):
    n = h2.shape[0]
    return pl.pallas_call(
        _dispatch_kernel,
        grid_spec=pltpu.PrefetchScalarGridSpec(
            num_scalar_prefetch=3,
            grid=(n // ROUTE_TM,),
            in_specs=[pl.BlockSpec((ROUTE_TM, D_MODEL), lambda i, *_: (i, 0)),
                      pl.BlockSpec((8, ROUTE_TM), lambda i, *_: (0, i))],
            out_specs=pl.BlockSpec(memory_space=pl.ANY),
            scratch_shapes=[pltpu.VMEM((FFN_BLOCK, PACKED_W), U32),
                            pltpu.VMEM((2, LOCAL_ROWS, PACKED_W), U32),
                            pltpu.SemaphoreType.DMA((2,)),
                            pltpu.SemaphoreType.DMA(())],
        ),
        out_shape=jax.ShapeDtypeStruct((n_rows, PACKED_W), U32),
        compiler_params=_cparams("arbitrary"),
        name="dispatch",
    )(plan, seg_ends, n_valid, h2, rec)


def _ffn_kernel(be_ref, nv_ref, br_ref, xs_ref, wg_ref, wu_ref, wd_ref, ys_ref, wg_sc, wu_sc, wd_sc):
    i = pl.program_id(0)
    prev = be_ref[jnp.maximum(i - 1, 0)]
    fresh = (i == 0) | (be_ref[i] != prev)

    @pl.when(fresh)
    def _():
        wg_sc[...] = wg_ref[0].astype(BF16)
        wu_sc[...] = wu_ref[0].astype(BF16)
        wd_sc[...] = wd_ref[0].astype(BF16)

    def rows_ffn(rows):
        xa, xb = (t.astype(BF16) for t in _unpack_halves(xs_ref[rows, :]))
        half = PACKED_W
        gate = (jnp.dot(xa, wg_sc[:half, :], preferred_element_type=F32)
                + jnp.dot(xb, wg_sc[half:, :], preferred_element_type=F32))
        up = (jnp.dot(xa, wu_sc[:half, :], preferred_element_type=F32)
              + jnp.dot(xb, wu_sc[half:, :], preferred_element_type=F32))
        hid = (gate * (1.0 / (1.0 + jnp.exp(-gate))) * up).astype(BF16)
        ys_ref[rows, :] = _pack_halves(jnp.dot(hid, wd_sc[...], preferred_element_type=F32))

    used = br_ref[i]
    lo_half = pl.ds(0, FFN_BLOCK // 2)
    hi_half = pl.ds(FFN_BLOCK // 2, FFN_BLOCK // 2)

    @pl.when(used > FFN_BLOCK // 2)
    def _():
        rows_ffn(pl.ds(0, FFN_BLOCK))

    @pl.when((used > 0) & (used <= FFN_BLOCK // 2))
    def _():
        rows_ffn(lo_half)
        ys_ref[hi_half, :] = jnp.zeros((FFN_BLOCK // 2, PACKED_W), U32)

    @pl.when(used == 0)
    def _():
        ys_ref[...] = jnp.zeros_like(ys_ref)


def expert_ffn(xs, block_expert, n_valid, block_rows, w_gate, w_up, w_down):
    n_rows = xs.shape[0]
    n_blocks = n_rows // FFN_BLOCK
    wmap = lambda i, be, nv, br: (be[i], 0, 0)
    return pl.pallas_call(
        _ffn_kernel,
        grid_spec=pltpu.PrefetchScalarGridSpec(
            num_scalar_prefetch=3,
            grid=(n_blocks,),
            in_specs=[pl.BlockSpec((FFN_BLOCK, PACKED_W), lambda i, be, nv, br: (jnp.minimum(i, nv[0] - 1), 0)),
                      pl.BlockSpec((1, D_MODEL, D_EXPERT), wmap),
                      pl.BlockSpec((1, D_MODEL, D_EXPERT), wmap),
                      pl.BlockSpec((1, D_EXPERT, D_MODEL), wmap)],
            out_specs=pl.BlockSpec((FFN_BLOCK, PACKED_W), lambda i, be, nv, br: (i, 0)),
            scratch_shapes=[pltpu.VMEM((D_MODEL, D_EXPERT), BF16),
                            pltpu.VMEM((D_MODEL, D_EXPERT), BF16),
                            pltpu.VMEM((D_EXPERT, D_MODEL), BF16)],
        ),
        out_shape=jax.ShapeDtypeStruct((n_rows, PACKED_W), U32),
        compiler_params=_cparams("arbitrary"),
        name="expert_ffn",
    )(block_expert, n_valid, block_rows, xs, w_gate, w_up, w_down)


def _combine_kernel(plan_ref, x1_ref, rec_ref, gf_ref, ys_ref, out_ref, buf, sem):
    i = pl.program_id(0)
    n_steps = pl.num_programs(0)
    slot = i % 2

    def fetch_tile(tile, s):
        def start_piece(lrow, grow, size):
            pltpu.make_async_copy(ys_ref.at[pl.ds(grow, size)], buf.at[s, pl.ds(lrow, size)], sem.at[s]).start()
        _for_each_run_piece(plan_ref, tile, n_steps, start_piece)

    @pl.when(i == 0)
    def _():
        buf[...] = jnp.zeros_like(buf)
        fetch_tile(0, 0)

    @pl.when(i + 1 < n_steps)
    def _():
        fetch_tile(i + 1, 1 - slot)

    def wait_piece(size):
        pltpu.make_async_copy(ys_ref.at[pl.ds(0, size)], buf.at[slot, pl.ds(0, size)], sem.at[slot]).wait()
    _for_each_total_piece(plan_ref, i, n_steps, wait_piece)

    rowf = lax.broadcasted_iota(I32, (LOCAL_ROWS, ROUTE_TM), 0).astype(F32)
    sel = (jnp.where(rowf == rec_ref[R_SLOT1:R_SLOT1 + 1, :], rec_ref[R_G1:R_G1 + 1, :], 0.0)
           + jnp.where(rowf == rec_ref[R_SLOT2:R_SLOT2 + 1, :], rec_ref[R_G2:R_G2 + 1, :], 0.0)).astype(BF16)
    ya, yb = (t.astype(BF16) for t in _unpack_halves(buf[slot]))
    half = PACKED_W
    tn = (((0,), (0,)), ((), ()))
    xa = x1_ref[:, :half] + lax.dot_general(sel, ya, tn, preferred_element_type=F32)
    xb = x1_ref[:, half:] + lax.dot_general(sel, yb, tn, preferred_element_type=F32)
    ms = (jnp.sum(xa * xa, axis=-1, keepdims=True) + jnp.sum(xb * xb, axis=-1, keepdims=True)) * (1.0 / D_MODEL)
    scale = lax.rsqrt(ms + EPS)
    out_ref[:, :half] = xa * scale * gf_ref[:, :half]
    out_ref[:, half:] = xb * scale * gf_ref[:, half:]


def combine(x1, rec, norm_f_g, ys, plan):
    n = x1.shape[0]
    tm = ROUTE_TM
    return pl.pallas_call(
        _combine_kernel,
        grid_spec=pltpu.PrefetchScalarGridSpec(
            num_scalar_prefetch=1,
            grid=(n // tm,),
            in_specs=[pl.BlockSpec((tm, D_MODEL), lambda i, p: (i, 0)),
                      pl.BlockSpec((8, tm), lambda i, p: (0, i)),
                      pl.BlockSpec((1, D_MODEL), lambda i, p: (0, 0)),
                      pl.BlockSpec(memory_space=pl.ANY)],
            out_specs=pl.BlockSpec((tm, D_MODEL), lambda i, p: (i, 0)),
            scratch_shapes=[pltpu.VMEM((2, LOCAL_ROWS, PACKED_W), U32),
                            pltpu.SemaphoreType.DMA((2,))],
        ),
        out_shape=jax.ShapeDtypeStruct((n, D_MODEL), F32),
        compiler_params=_cparams("arbitrary"),
        name="combine",
    )(plan, x1, rec, norm_f_g.reshape(1, D_MODEL), ys)


def _router_weights(w_group, b_group, w_router, b_router):
    w = jnp.zeros((LANES, D_MODEL), F32)
    w = w.at[:N_GROUPS].set(w_group.T).at[EXPERT_ROW0:EXPERT_ROW0 + N_EXPERTS].set(w_router.T)
    b = jnp.zeros((LANES, 1), F32)
    b = b.at[:N_GROUPS, 0].set(b_group).at[EXPERT_ROW0:EXPERT_ROW0 + N_EXPERTS, 0].set(b_router)
    hi = w.astype(BF16)
    lo = (w - hi.astype(F32)).astype(BF16)
    return hi, lo, b


def _sorted_rows_bound(n_tokens):
    worst = (2 * n_tokens + (n_tokens // ROUTE_TM) * N_EXPERTS * (RUN_ALIGN - 1)
             + N_EXPERTS * (FFN_BLOCK - 1))
    return -(-worst // FFN_BLOCK) * FFN_BLOCK


def _block_plan(seg, n_rows):
    seg_ends = seg[SEG_END, :N_EXPERTS].astype(I32)
    used_ends = seg[SEG_USED_END, :N_EXPERTS].astype(I32)
    n_blocks = n_rows // FFN_BLOCK
    blk_start = jnp.arange(n_blocks, dtype=I32) * FFN_BLOCK
    block_expert = jnp.minimum(jnp.sum((seg_ends[None, :] <= blk_start[:, None]).astype(I32), axis=1),
                               N_EXPERTS - 1)
    n_valid = seg_ends[-1:] // FFN_BLOCK
    block_rows = jnp.clip(used_ends[block_expert] - blk_start, 0, FFN_BLOCK)
    block_rows = jnp.where(blk_start < seg_ends[-1], block_rows, 0)
    return seg_ends, block_expert, n_valid, block_rows


def kernel(x, norm1_g, w_in, attn_norm_g, hgrn_gamma, hgrn_norm_g, w_out, norm2_g, w_group, b_group,
           w_router, b_router, w_gate, w_up, w_down, norm_f_g):
    batch, seq, d = x.shape
    assert seq == SEQ and d == D_MODEL and norm1_g.shape[0] == 1
    n = batch * seq
    x2 = x.reshape(n, d)
    qkv, hg = in_proj(x2, norm1_g[0], w_in[0].astype(BF16), batch)
    oa = attention(qkv)
    yh = hgrn(hg.reshape(batch, seq, HG_W), hgrn_gamma, hgrn_norm_g[0]).reshape(n, WIDTH)
    wr_hi, wr_lo, br = _router_weights(w_group[0], b_group[0], w_router[0], b_router[0])
    x1, h2, rec, tab = post_mix(oa, yh, x2, attn_norm_g[0], w_out[0].astype(BF16), norm2_g[0],
                                wr_hi, wr_lo, br)
    plan, seg = run_plan(tab)
    plan = plan.reshape(-1)
    n_rows = _sorted_rows_bound(n)
    seg_ends, block_expert, n_valid, block_rows = _block_plan(seg, n_rows)
    xs = dispatch(h2, rec, plan, seg_ends, n_valid, n_rows)
    ys = expert_ffn(xs, block_expert, n_valid, block_rows, w_gate[0], w_up[0], w_down[0])
    return combine(x1, rec, norm_f_g, ys, plan).reshape(batch, seq, d)
```

```python
import jax
import jax.numpy as jnp
from jax import lax
from jax.experimental import pallas as pl
from jax.experimental.pallas import tpu as pltpu

F32 = jnp.float32
BF16 = jnp.bfloat16
I32 = jnp.int32
U32 = jnp.uint32

D_MODEL = 1024
HEAD_DIM = 64
N_HEADS = 8
WIDTH = N_HEADS * HEAD_DIM
QKV_W = 3 * WIDTH
HG_W = 4 * WIDTH
SEQ = 2048
ATTN_BLOCK = 128
DILATIONS = (1, 4, 16)
N_RES = 16
RES_LEN = SEQ // N_RES
TILE_TOKENS = 512
U_PER_TILE = TILE_TOKENS // N_RES
HGRN_CHUNK = 32
HGRN_SUPER = 256
HGRN_GROUP = 4
N_GROUPS = 4
EXPERTS_PER_GROUP = 8
N_EXPERTS = 32
D_EXPERT = 512
FFN_BLOCK = 512
PACKED_W = D_MODEL // 2
EPS = 1e-6
NEG = -1e30
LOG2E = 1.4426950408889634
LANES = 128
VMEM_LIMIT = 56 * 1024 * 1024


def _cparams(*sem):
    return pltpu.CompilerParams(dimension_semantics=sem, vmem_limit_bytes=VMEM_LIMIT)


def _pack_pair(a, b):
    hi = pltpu.bitcast(a.astype(BF16).astype(F32), U32)
    lo = pltpu.bitcast(b.astype(BF16).astype(F32), U32)
    return hi | (lo >> 16)


def _pack_halves(x):
    w = x.shape[1] // 2
    return _pack_pair(x[:, :w], x[:, w:])


def _unpack_halves(p):
    return pltpu.bitcast(p & jnp.uint32(0xFFFF0000), F32), pltpu.bitcast(p << 16, F32)


N_LANE_TILES = WIDTH // LANES


def _tile_permutation():
    i = jnp.arange(TILE_TOKENS)
    src = N_RES * (i % U_PER_TILE) + i // U_PER_TILE
    return (src[:, None] == jnp.arange(TILE_TOKENS)[None, :]).astype(BF16)


def _in_proj_kernel(x_ref, g_ref, w_ref, perm_ref, qkv_ref, hg_ref):
    x = x_ref[...]
    ms = jnp.mean(x * x, axis=-1, keepdims=True)
    h = (x * lax.rsqrt(ms + EPS) * g_ref[...]).astype(BF16)
    hp = jnp.dot(perm_ref[...], h, preferred_element_type=F32).astype(BF16)
    for c in range(QKV_W // WIDTH):
        res = jnp.dot(hp, w_ref[:, c * WIDTH:(c + 1) * WIDTH], preferred_element_type=F32)
        for r in range(N_RES):
            for g in range(N_LANE_TILES):
                qkv_ref[r, g, c] = res[r * U_PER_TILE:(r + 1) * U_PER_TILE, g * LANES:(g + 1) * LANES]
    for j in range(HG_W // WIDTH):
        wsl = slice(QKV_W + j * WIDTH, QKV_W + (j + 1) * WIDTH)
        hg_ref[:, j * WIDTH:(j + 1) * WIDTH] = jnp.dot(h, w_ref[:, wsl], preferred_element_type=F32).astype(BF16)


def in_proj(x2, g, w_bf16, batch):
    n = x2.shape[0]
    tiles_per_b = SEQ // TILE_TOKENS
    return pl.pallas_call(
        _in_proj_kernel,
        grid=(n // TILE_TOKENS,),
        in_specs=[
            pl.BlockSpec((TILE_TOKENS, D_MODEL), lambda i: (i, 0)),
            pl.BlockSpec((1, D_MODEL), lambda i: (0, 0)),
            pl.BlockSpec((D_MODEL, QKV_W + HG_W), lambda i: (0, 0)),
            pl.BlockSpec((TILE_TOKENS, TILE_TOKENS), lambda i: (0, 0)),
        ],
        out_specs=[
            pl.BlockSpec((None, N_RES, N_LANE_TILES, 3, U_PER_TILE, LANES),
                         lambda i: (i // tiles_per_b, 0, 0, 0, i % tiles_per_b, 0)),
            pl.BlockSpec((TILE_TOKENS, HG_W), lambda i: (i, 0)),
        ],
        out_shape=[
            jax.ShapeDtypeStruct((batch, N_RES, N_LANE_TILES, 3, RES_LEN, LANES), F32),
            jax.ShapeDtypeStruct((n, HG_W), BF16),
        ],
        compiler_params=_cparams("arbitrary"),
        name="in_proj",
    )(x2, g.reshape(1, D_MODEL), w_bf16, _tile_permutation())


ATTN_GROUP16 = 16
ATTN_GROUP1 = 5
HEADS_PER_TILE = LANES // HEAD_DIM


def _attn_bias_tables():
    slopes = 2.0 ** (-8.0 * jnp.arange(1, N_HEADS + 1, dtype=F32) / N_HEADS)

    def table(qpos, kpos, dil):
        sd = qpos[:, None] - kpos[None, :]
        b = -slopes[:, None, None] * (sd * dil).astype(F32)[None] * LOG2E
        return jnp.where(((sd >= 0) & (sd <= ATTN_BLOCK))[None], b, NEG)

    q = jnp.arange(ATTN_BLOCK)
    k = jnp.arange(2 * ATTN_BLOCK)
    q1 = 16 * (q % 8) + q // 8 + ATTN_BLOCK
    k1 = 16 * (k % 16) + k // 16
    q4 = 4 * (q % 32) + q // 32 + ATTN_BLOCK
    k4 = 4 * (k % 64) + k // 64
    q16 = q + ATTN_BLOCK
    none = jnp.full((N_HEADS, ATTN_BLOCK, 2 * ATTN_BLOCK), NEG, F32)
    tabs = [jnp.concatenate([table(q1, k1, 1), table(q1, q1, 1)], axis=-1),
            jnp.concatenate([table(q4, k4, 4), table(q4, q4, 4)], axis=-1),
            jnp.concatenate([none, table(q16, q16, 16)], axis=-1)]
    return jnp.stack(tabs, axis=0)


def _attn_bias(bias_ref, d_idx, with_prev):
    ksl = slice(0, 2 * ATTN_BLOCK) if with_prev else slice(2 * ATTN_BLOCK, 3 * ATTN_BLOCK)
    return jnp.concatenate([bias_ref[d_idx, h, :, ksl] for h in range(HEADS_PER_TILE)], axis=0)


def _attn_blocks(blocks):
    nb = ATTN_BLOCK
    lane = lax.broadcasted_iota(I32, (nb, LANES), 1)
    head0 = lane < HEAD_DIM
    biases = [blk[3] for blk in blocks]
    blocks = [blk[:3] for blk in blocks]
    qs = [q * (HEAD_DIM ** -0.5 * LOG2E) for q, _, _ in blocks]
    qq = [jnp.concatenate([jnp.where(head0, q, 0.0), jnp.where(head0, 0.0, q)], axis=0).astype(BF16)
          for q in qs]
    ks = [k.astype(BF16) for _, k, _ in blocks]
    vs = [v.astype(BF16) for _, _, v in blocks]
    v_ones = [jnp.concatenate([v, jnp.ones_like(v)], axis=-1) for v in vs]
    s = [lax.dot_general(q2, k, (((1,), (1,)), ((), ())), preferred_element_type=F32) + bias
         for q2, k, bias in zip(qq, ks, biases)]
    m = [jnp.max(x, axis=-1, keepdims=True) for x in s]
    p = [jnp.exp2(x - mx).astype(BF16) for x, mx in zip(s, m)]
    acc = [jnp.dot(px, vo, preferred_element_type=F32) for px, vo in zip(p, v_ones)]
    head0_wide = jnp.concatenate([head0, head0], axis=-1)
    outs = []
    for a, mx in zip(acc, m):
        ol = jnp.where(head0_wide, a[:nb], a[nb:])
        l = ol[:, LANES:]
        outs.append((ol[:, :LANES] * (1.0 / l), jnp.where(head0, mx[:nb], mx[nb:]) + jnp.log2(l)))
    return outs


def _attn_kernel(qkv_ref, bias_ref, o_ref, o4_ref, l4_ref, o16_ref, l16_ref, fin_ref):
    grp = ATTN_GROUP16

    def body16(j, c):
        rs = [j * grp + a for a in range(grp)]
        bias = _attn_bias(bias_ref, 2, False)
        res = _attn_blocks([(qkv_ref[r, 0], qkv_ref[r, 1], qkv_ref[r, 2], bias) for r in rs])
        for r, (o, l) in zip(rs, res):
            o16_ref[r] = o
            l16_ref[r] = l
        return c
    for j in range(N_RES // grp):
        body16(j, 0)

    def gather4(c, rho, u0, nu):
        return jnp.concatenate([qkv_ref[rho + 4 * a, c, pl.ds(u0, nu), :] for a in range(4)], axis=0)

    def store4(rho, u0, o, l):
        for a in range(4):
            o4_ref[rho + 4 * a, pl.ds(u0, 32), :] = o[a * 32:(a + 1) * 32]
            l4_ref[rho + 4 * a, pl.ds(u0, 32), :] = l[a * 32:(a + 1) * 32]

    def body4(j, c):
        bias = _attn_bias(bias_ref, 1, True)
        todo = [(2 * j + a, 32 * n) for a in range(2) for n in range(1, 4)]
        res = _attn_blocks([(gather4(0, rho, u0, 32), gather4(1, rho, u0 - 32, 64),
                             gather4(2, rho, u0 - 32, 64), bias) for rho, u0 in todo])
        for (rho, u0), (o, l) in zip(todo, res):
            store4(rho, u0, o, l)
        return c
    for j in range(2):
        body4(j, 0)

    def gather1(ref, u0, nu, *lead):
        return jnp.concatenate([ref[(r,) + lead + (pl.ds(u0, nu), slice(None))] for r in range(N_RES)],
                               axis=0)

    def merge_and_store(u0, o1, l1):
        l4 = gather1(l4_ref, u0, 8)
        l16 = gather1(l16_ref, u0, 8)
        mx = jnp.maximum(jnp.maximum(l1, l4), l16)
        e1 = jnp.exp2(l1 - mx)
        e4 = jnp.exp2(l4 - mx)
        e16 = jnp.exp2(l16 - mx)
        inv = 1.0 / (e1 + e4 + e16)
        o = (e1 * inv) * o1 + (e4 * inv) * gather1(o4_ref, u0, 8) + (e16 * inv) * gather1(o16_ref, u0, 8)
        for r in range(N_RES):
            fin_ref[r, pl.ds(u0, 8), :] = o[r * 8:(r + 1) * 8]

    bias4 = _attn_bias(bias_ref, 1, False)
    first = [tuple(gather4(c, rho, 0, 32) for c in range(3)) + (bias4,) for rho in range(4)]
    first.append(tuple(gather1(qkv_ref, 0, 8, c) for c in range(3)) + (_attn_bias(bias_ref, 0, False),))
    res = _attn_blocks(first)
    for rho in range(4):
        store4(rho, 0, *res[rho])
    merge_and_store(0, *res[4])

    n_blocks1 = RES_LEN // 8 - 1

    def body1(j, c):
        u0s = [(1 + j * ATTN_GROUP1 + a) * 8 for a in range(ATTN_GROUP1)]
        bias = _attn_bias(bias_ref, 0, True)
        res = _attn_blocks([(gather1(qkv_ref, u0, 8, 0), gather1(qkv_ref, u0 - 8, 16, 1),
                             gather1(qkv_ref, u0 - 8, 16, 2), bias) for u0 in u0s])
        for u0, (o, l) in zip(u0s, res):
            merge_and_store(u0, o, l)
        return c
    for j in range(n_blocks1 // ATTN_GROUP1):
        body1(j, 0)

    def emit(r, c):
        o_ref[r] = fin_ref[r].astype(o_ref.dtype)
        return c
    lax.fori_loop(0, N_RES, emit, 0)


def attention(qkv):
    batch = qkv.shape[0]
    scratch = pltpu.VMEM((N_RES, RES_LEN, LANES), F32)
    return pl.pallas_call(
        _attn_kernel,
        grid=(batch, N_LANE_TILES),
        in_specs=[
            pl.BlockSpec((None, N_RES, None, 3, RES_LEN, LANES), lambda b, g: (b, 0, g, 0, 0, 0)),
            pl.BlockSpec((len(DILATIONS), HEADS_PER_TILE, ATTN_BLOCK, 3 * ATTN_BLOCK),
                         lambda b, g: (0, g, 0, 0)),
        ],
        out_specs=pl.BlockSpec((None, N_RES, None, RES_LEN, LANES), lambda b, g: (b, 0, g, 0, 0)),
        out_shape=jax.ShapeDtypeStruct((batch, N_RES, N_LANE_TILES, RES_LEN, LANES), BF16),
        scratch_shapes=[scratch] * 5,
        compiler_params=_cparams("arbitrary", "arbitrary"),
        name="dilated_attention",
    )(qkv, _attn_bias_tables())


def _split2(x):
    a = x.astype(BF16)
    b = (x - a.astype(F32)).astype(BF16)
    return a, b


def _hgrn_kernel(q_ref, f_ref, i_ref, g_ref, gamma_ref, ng_ref, tri_ref, y_ref):
    sup, ch = HGRN_SUPER, HGRN_CHUNK
    n_ch = sup // ch
    gam = gamma_ref[...]
    gmx = jnp.max(gam, axis=0, keepdims=True)
    ge = jnp.exp(gam - gmx)
    lb = ge[0:1] / jnp.sum(ge, axis=0, keepdims=True)
    ng = ng_ref[...]
    tri = tri_ref[...]
    ri = lax.broadcasted_iota(I32, (2 * sup, sup), 0) % sup
    ci = lax.broadcasted_iota(I32, (2 * sup, sup), 1)
    causal_bd2 = (ri // ch == ci // ch) & (ri >= ci)
    head0 = lax.broadcasted_iota(I32, (sup, LANES), 1) < HEAD_DIM
    row_chunk = lax.broadcasted_iota(I32, (sup, LANES), 0) // ch
    same_head = (lax.broadcasted_iota(I32, (LANES, LANES), 0) // HEAD_DIM
                 == lax.broadcasted_iota(I32, (LANES, LANES), 1) // HEAD_DIM)

    def sigmoid(x):
        return 1.0 / (1.0 + jnp.exp(-x))

    def dot_nt(a, b):
        return lax.dot_general(a, b, (((1,), (1,)), ((), ())), preferred_element_type=F32)

    def dot_tn(a, b):
        return lax.dot_general(a, b, (((0,), (0,)), ((), ())), preferred_element_type=F32)

    def expand(x):
        zero = jnp.zeros_like(x)
        return jnp.concatenate([jnp.where(row_chunk == c, x, zero) for c in range(n_ch)], axis=1)

    def body(j, state):
        grp = range(HGRN_GROUP)
        rows = [pl.ds((j * HGRN_GROUP + a) * sup, sup) for a in grp]
        fr = [f_ref[r, :].astype(F32) for r in rows]
        t = [jnp.exp(-jnp.abs(x)) for x in fr]
        rcp = [1.0 / (1.0 + x) for x in t]
        sig = [jnp.where(f >= 0, r, x * r) for f, r, x in zip(fr, rcp, t)]
        nsig = [jnp.where(f >= 0, x * r, r) for f, r, x in zip(fr, rcp, t)]
        logf = [jnp.log(lb + (1.0 - lb) * x) for x in sig]
        key = [(1.0 - lb) * x for x in nsig]
        parts = [_split2(x) for x in logf]
        b = [sum(jnp.dot(tri, p, preferred_element_type=F32) for p in ps) for ps in parts]
        totals = [[x[(c + 1) * ch - 1:(c + 1) * ch, :] for c in range(n_ch)] for x in b]
        b_last = [jnp.concatenate([jnp.broadcast_to(t_, (ch, LANES)) for t_ in ts], axis=0) for ts in totals]
        pad = jnp.zeros((LANES - n_ch, LANES), F32)
        decay_t = [jnp.exp(jnp.transpose(jnp.concatenate(ts + [pad], axis=0))) for ts in totals]
        qv = [q_ref[r, :].astype(F32) for r in rows]
        q_in = [(x * sigmoid(x) * jnp.exp(bb)).astype(BF16) for x, bb in zip(qv, b)]
        k_in = [(k * jnp.exp(-bb)).astype(BF16) for k, bb in zip(key, b)]
        k_end = [(k * jnp.exp(bl - bb)).astype(BF16) for k, bl, bb in zip(key, b_last, b)]
        vv = [i_ref[r, :] for r in rows]
        q2 = [jnp.concatenate([jnp.where(head0, x, jnp.zeros_like(x)), jnp.where(head0, jnp.zeros_like(x), x)],
                              axis=0) for x in q_in]
        att = [jnp.where(causal_bd2, dot_nt(x, k), 0.0).astype(BF16) for x, k in zip(q2, k_in)]
        intra = [jnp.dot(a_, v, preferred_element_type=F32) for a_, v in zip(att, vv)]
        o_intra = [jnp.where(head0, x[:sup], x[sup:]) for x in intra]
        ds_all = [dot_tn(k, expand(v)) for k, v in zip(k_end, vv)]
        s_prev = []
        for a in grp:
            for c in range(n_ch):
                s_prev.append(state.astype(BF16))
                ds = jnp.where(same_head, ds_all[a][:, c * LANES:(c + 1) * LANES], 0.0)
                state = decay_t[a][:, c:c + 1] * state + ds
        o_inter = [jnp.concatenate(
            [jnp.dot(q_in[a][c * ch:(c + 1) * ch], s_prev[a * n_ch + c], preferred_element_type=F32)
             for c in range(n_ch)], axis=0) for a in grp]
        o = [x + y for x, y in zip(o_intra, o_inter)]
        sq = [x * x for x in o]
        ss0 = [jnp.sum(jnp.where(head0, x, 0.0), axis=-1, keepdims=True) for x in sq]
        ss1 = [jnp.sum(jnp.where(head0, 0.0, x), axis=-1, keepdims=True) for x in sq]
        ms = [jnp.where(head0, x, y) * (1.0 / HEAD_DIM) for x, y in zip(ss0, ss1)]
        gv = [g_ref[r, :].astype(F32) for r in rows]
        for r, x, m_, g_ in zip(rows, o, ms, gv):
            y_ref[r, :] = (x * lax.rsqrt(m_ + EPS) * ng * (g_ * sigmoid(g_))).astype(y_ref.dtype)
        return state

    state = jnp.zeros((LANES, LANES), F32)
    for j in range(SEQ // (sup * HGRN_GROUP)):
        state = body(j, state)


def hgrn(hg3, gamma, norm_g):
    batch = hg3.shape[0]
    r = jnp.arange(HGRN_SUPER)
    same = (r[:, None] // HGRN_CHUNK) == (r[None, :] // HGRN_CHUNK)
    tri = (same & (r[:, None] >= r[None, :])).astype(BF16)
    col = lambda which: (lambda b, g: (b, 0, which * N_LANE_TILES + g))
    const2 = lambda b, g: (0, 0)
    return pl.pallas_call(
        _hgrn_kernel,
        grid=(batch, N_LANE_TILES),
        in_specs=[
            pl.BlockSpec((None, SEQ, LANES), col(0)),
            pl.BlockSpec((None, SEQ, LANES), col(1)),
            pl.BlockSpec((None, SEQ, LANES), col(2)),
            pl.BlockSpec((None, SEQ, LANES), col(3)),
            pl.BlockSpec((2, LANES), lambda b, g: (0, g)),
            pl.BlockSpec((1, LANES), lambda b, g: (0, g)),
            pl.BlockSpec((HGRN_SUPER, HGRN_SUPER), const2),
        ],
        out_specs=pl.BlockSpec((None, SEQ, LANES), lambda b, g: (b, 0, g)),
        out_shape=jax.ShapeDtypeStruct((batch, SEQ, WIDTH), BF16),
        compiler_params=_cparams("arbitrary", "arbitrary"),
        name="hgrn2",
    )(hg3, hg3, hg3, hg3, gamma.astype(F32), norm_g.reshape(1, WIDTH).astype(F32), tri)


def _post_mix_kernel(oa_ref, yh_ref, x_ref, ag_ref, wo_ref, g2_ref, wrh_ref, wrl_ref, br_ref, perm_t_ref,
                     tri_ref, lower_ref,
                     x1_ref, h2_ref, rec_ref, tab_ref, carry_ref, lg_sc):
    i = pl.program_id(0)

    @pl.when(i == 0)
    def _():
        lg_sc[...] = jnp.zeros_like(lg_sc)
    _route_tile(lg_sc[...], (i > 0).astype(F32), tri_ref, lower_ref, rec_ref, tab_ref, carry_ref)

    oa = jnp.concatenate(
        [jnp.concatenate([oa_ref[r, g] for g in range(N_LANE_TILES)], axis=1)
         for r in range(N_RES)], axis=0).astype(F32)
    ms = jnp.mean(oa * oa, axis=-1, keepdims=True)
    ya = (oa * lax.rsqrt(ms + EPS) * ag_ref[...]).astype(BF16)
    ya = jnp.dot(perm_t_ref[...], ya, preferred_element_type=F32).astype(BF16)
    mix = (jnp.dot(ya, wo_ref[:WIDTH, :], preferred_element_type=F32)
           + jnp.dot(yh_ref[...], wo_ref[WIDTH:, :], preferred_element_type=F32))
    x1 = x_ref[...] + mix
    x1_ref[...] = x1
    ms2 = jnp.mean(x1 * x1, axis=-1, keepdims=True)
    h2 = x1 * lax.rsqrt(ms2 + EPS) * g2_ref[...]
    hi = h2.astype(BF16)
    h2_ref[...] = hi
    lo = (h2 - hi.astype(F32)).astype(BF16)
    nt = (((1,), (1,)), ((), ()))
    wrh = wrh_ref[...]
    lg_sc[...] = (lax.dot_general(wrh, hi, nt, preferred_element_type=F32)
                  + lax.dot_general(wrh, lo, nt, preferred_element_type=F32)
                  + lax.dot_general(wrl_ref[...], hi, nt, preferred_element_type=F32)
                  + br_ref[...])


def post_mix(oa, yh, x2, attn_g, w_out_bf16, g2, wr_hi, wr_lo, br):
    n = x2.shape[0]
    tiles_per_b = SEQ // TILE_TOKENS
    assert ROUTE_TM == TILE_TOKENS
    n_tiles = n // TILE_TOKENS
    cur = lambda i: jnp.minimum(i, n_tiles - 1)
    prev = lambda i: jnp.maximum(i - 1, 0)
    row = lambda w: pl.BlockSpec((TILE_TOKENS, w), lambda i: (cur(i), 0))
    const = lambda r, c: pl.BlockSpec((r, c), lambda i: (0, 0))
    return pl.pallas_call(
        _post_mix_kernel,
        grid=(n_tiles + 1,),
        in_specs=[pl.BlockSpec((None, N_RES, N_LANE_TILES, U_PER_TILE, LANES),
                               lambda i: (cur(i) // tiles_per_b, 0, 0, cur(i) % tiles_per_b, 0)),
                  row(WIDTH), row(D_MODEL),
                  const(1, WIDTH), const(2 * WIDTH, D_MODEL), const(1, D_MODEL),
                  const(LANES, D_MODEL), const(LANES, D_MODEL), const(LANES, 1),
                  const(TILE_TOKENS, TILE_TOKENS), const(ROUTE_TM, ROUTE_TM), const(N_EXPERTS, N_EXPERTS)],
        out_specs=[row(D_MODEL), row(D_MODEL),
                   pl.BlockSpec((8, ROUTE_TM), lambda i: (0, prev(i))),
                   pl.BlockSpec((1, 8, LANES), lambda i: (prev(i), 0, 0))],
        out_shape=[jax.ShapeDtypeStruct((n, D_MODEL), F32),
                   jax.ShapeDtypeStruct((n, D_MODEL), BF16),
                   jax.ShapeDtypeStruct((8, n), F32),
                   jax.ShapeDtypeStruct((n_tiles, 8, LANES), F32)],
        scratch_shapes=[pltpu.VMEM((N_EXPERTS, LANES), F32), pltpu.VMEM((LANES, TILE_TOKENS), F32)],
        compiler_params=_cparams("arbitrary"),
        name="post_mix",
    )(oa, yh, x2, attn_g.reshape(1, WIDTH), w_out_bf16, g2.reshape(1, D_MODEL), wr_hi, wr_lo, br,
      _tile_permutation().T, *_route_constants())


ROUTE_TM = 512
EXPERT_ROW0 = 32
R_E1, R_E2, R_SLOT1, R_SLOT2, R_G1, R_G2 = 0, 1, 2, 3, 4, 5
T_CARRY, T_ROWS, T_OFF = 0, 1, 2
SEG_END = 0
TOTAL_LANE = N_EXPERTS
RUN_ALIGN = 8
LOCAL_ROWS = 2 * ROUTE_TM + N_EXPERTS * RUN_ALIGN


def _route_tile(lg, live, tri_ref, lower_ref, rec_ref, tab_ref, carry_ref):
    i = pl.program_id(0)

    @pl.when(i == 0)
    def _():
        carry_ref[...] = jnp.zeros_like(carry_ref)

    tm = lg.shape[1]
    sub8 = lax.broadcasted_iota(I32, (8, tm), 0).astype(F32)
    big = 8.0
    gmask = sub8 < N_GROUPS
    gl = jnp.where(gmask, lg[0:8], NEG)
    gmax = jnp.max(gl, axis=0, keepdims=True)
    gsel = jnp.min(jnp.where(gmask & (gl == gmax), sub8, big), axis=0, keepdims=True)
    gsum = jnp.sum(jnp.where(gmask, jnp.exp(gl - gmax), 0.0), axis=0, keepdims=True)
    w_g = 1.0 / gsum
    el = jnp.zeros((EXPERTS_PER_GROUP, tm), F32)
    for g in range(N_GROUPS):
        lo = EXPERT_ROW0 + g * EXPERTS_PER_GROUP
        el = jnp.where(gsel == g, lg[lo:lo + EXPERTS_PER_GROUP], el)
    v1 = jnp.max(el, axis=0, keepdims=True)
    i1 = jnp.min(jnp.where(el == v1, sub8, big), axis=0, keepdims=True)
    el2 = jnp.where(sub8 == i1, NEG, el)
    v2 = jnp.max(el2, axis=0, keepdims=True)
    i2 = jnp.min(jnp.where((el2 == v2) & (sub8 != i1), sub8, big), axis=0, keepdims=True)
    ex = jnp.exp(v2 - v1)
    den = 1.0 / (1.0 + ex)
    g1 = w_g * den
    g2 = w_g * ex * den
    e1 = gsel * EXPERTS_PER_GROUP + i1
    e2 = gsel * EXPERTS_PER_GROUP + i2
    sub_e = lax.broadcasted_iota(I32, (N_EXPERTS, tm), 0).astype(F32)
    oh1 = sub_e == e1
    oh2 = sub_e == e2
    onehot = (oh1 | oh2).astype(BF16)
    before = jnp.dot(onehot, tri_ref[...], preferred_element_type=F32)
    count = jnp.sum(onehot.astype(F32), axis=1, keepdims=True)
    units = jnp.floor((count + (RUN_ALIGN - 1)) * (1.0 / RUN_ALIGN)) * live
    units = jnp.broadcast_to(units, (N_EXPERTS, LANES))
    rows = units * RUN_ALIGN
    off = RUN_ALIGN * jnp.dot(lower_ref[...], units.astype(BF16), preferred_element_type=F32)
    place = off[:, 0:1] + before
    slot1 = jnp.sum(jnp.where(oh1, place, 0.0), axis=0, keepdims=True)
    slot2 = jnp.sum(jnp.where(oh2, place, 0.0), axis=0, keepdims=True)
    total = jnp.sum(rows, axis=0, keepdims=True)
    eye = (lax.broadcasted_iota(I32, (N_EXPERTS, LANES), 0) == lax.broadcasted_iota(I32, (N_EXPERTS, LANES), 1))
    lane1 = lax.broadcasted_iota(I32, (1, LANES), 1)
    as_row = lambda col: jnp.where(lane1 == TOTAL_LANE, total,
                                   jnp.sum(jnp.where(eye, col, 0.0), axis=0, keepdims=True))
    sub = lax.broadcasted_iota(I32, (8, LANES), 0)
    tab_ref[0] = jnp.where(sub == T_CARRY, as_row(carry_ref[...]),
                           jnp.where(sub == T_ROWS, as_row(rows), jnp.where(sub == T_OFF, as_row(off), 0.0)))
    carry_ref[...] += rows
    rec = jnp.zeros((8, tm), F32)
    for slot, val in ((R_E1, e1), (R_E2, e2), (R_SLOT1, slot1),
                      (R_SLOT2, slot2), (R_G1, g1), (R_G2, g2)):
        rec = jnp.where(sub8 == slot, val, rec)
    rec_ref[...] = rec


def _route_constants():
    r = jnp.arange(ROUTE_TM)
    tri = (r[:, None] < r[None, :]).astype(BF16)
    e = jnp.arange(N_EXPERTS)
    lower = (e[:, None] > e[None, :]).astype(BF16)
    return tri, lower


P_START, P_ROWS, P_OFF = 0, 1, 2


def _plan_kernel(carry_ref, rows_ref, off_ref, upper_ref, plan_ref, seg_ref):
    n_tiles = carry_ref.shape[0]
    lane = lax.broadcasted_iota(I32, (1, LANES), 1)
    last = pl.ds(n_tiles - 1, 1)
    totals = jnp.where(lane < N_EXPERTS, carry_ref[last, :] + rows_ref[last, :], 0.0)
    blocks = jnp.floor((totals + (FFN_BLOCK - 1)) * (1.0 / FFN_BLOCK))
    starts = FFN_BLOCK * jnp.dot(jnp.broadcast_to(blocks, (8, LANES)).astype(BF16), upper_ref[...],
                                 preferred_element_type=F32)[0:1]
    seg_ref[...] = jnp.broadcast_to(starts + blocks * FFN_BLOCK, seg_ref.shape)
    plan_ref[P_START] = (carry_ref[...] + starts).astype(I32)
    plan_ref[P_ROWS] = rows_ref[...].astype(I32)
    plan_ref[P_OFF] = off_ref[...].astype(I32)


def run_plan(tab):
    n_tiles = tab.shape[0]
    ln = jnp.arange(LANES)
    upper = (ln[:, None] < ln[None, :]).astype(BF16)
    return pl.pallas_call(
        _plan_kernel,
        out_shape=[jax.ShapeDtypeStruct((3, n_tiles, LANES), I32),
                   jax.ShapeDtypeStruct((8, LANES), F32)],
        compiler_params=pltpu.CompilerParams(vmem_limit_bytes=VMEM_LIMIT),
        name="run_plan",
    )(tab[:, T_CARRY], tab[:, T_ROWS], tab[:, T_OFF], upper)


RUN_PIECE = 64
SMALL_PIECES = (32, 16, 8)
TOTAL_PIECES = (1024, 512, 256, 128, 64, 32, 16, 8)
SORT_CHUNK = 256


def _for_each_run_piece(plan_ref, tile, n_tiles, fn):
    plane = n_tiles * LANES

    def per_expert(e, c):
        idx = tile * LANES + e
        start = plan_ref[P_START * plane + idx]
        rows = plan_ref[P_ROWS * plane + idx]
        off = plan_ref[P_OFF * plane + idx]

        def big(j, c2):
            fn(pl.multiple_of(off + j * RUN_PIECE, RUN_ALIGN), pl.multiple_of(start + j * RUN_PIECE, RUN_ALIGN),
               RUN_PIECE)
            return c2
        lax.fori_loop(0, lax.shift_right_logical(rows, 6), big, 0)
        for size in SMALL_PIECES:
            done = jnp.bitwise_and(rows, -2 * size)

            @pl.when(jnp.bitwise_and(rows, size) != 0)
            def _():
                fn(pl.multiple_of(off + done, RUN_ALIGN), pl.multiple_of(start + done, RUN_ALIGN), size)
        return c
    lax.fori_loop(0, N_EXPERTS, per_expert, 0)


def _for_each_total_piece(plan_ref, tile, n_tiles, fn):
    total = plan_ref[P_OFF * n_tiles * LANES + tile * LANES + TOTAL_LANE]
    for size in TOTAL_PIECES:
        @pl.when(jnp.bitwise_and(total, size) != 0)
        def _():
            fn(size)


def _dispatch_kernel(plan_ref, ends_ref, nv_ref, h2_ref, rec_ref, xs_ref, zero_buf, stage, sem, zsem):
    i = pl.program_id(0)
    n_steps = pl.num_programs(0)
    n_blocks = xs_ref.shape[0] // FFN_BLOCK

    @pl.when(i == 0)
    def _():
        zero_buf[...] = jnp.zeros_like(zero_buf)

        def zero_copy(row0):
            return pltpu.make_async_copy(zero_buf, xs_ref.at[pl.ds(row0, FFN_BLOCK)], zsem)

        def seg_end(e):
            return ends_ref[e], ends_ref[e] > jnp.where(e > 0, ends_ref[jnp.maximum(e - 1, 0)], 0)

        def start_e(e, c):
            end, nonempty = seg_end(e)

            @pl.when(nonempty)
            def _():
                zero_copy(pl.multiple_of(end - FFN_BLOCK, FFN_BLOCK)).start()
            return c
        lax.fori_loop(0, N_EXPERTS, start_e, 0)

        def start_b(blk, c):
            zero_copy(pl.multiple_of(blk * FFN_BLOCK, FFN_BLOCK)).start()
            return c
        lax.fori_loop(nv_ref[0], n_blocks, start_b, 0)

        def wait_e(e, c):
            _, nonempty = seg_end(e)

            @pl.when(nonempty)
            def _():
                zero_copy(0).wait()
            return c
        lax.fori_loop(0, N_EXPERTS, wait_e, 0)

        def wait_b(blk, c):
            zero_copy(0).wait()
            return c
        lax.fori_loop(nv_ref[0], n_blocks, wait_b, 0)

    slot = i % 2
    s1 = rec_ref[R_SLOT1:R_SLOT1 + 1, :]
    s2 = rec_ref[R_SLOT2:R_SLOT2 + 1, :]
    half = PACKED_W
    for c in range(LOCAL_ROWS // SORT_CHUNK):
        rowf = (lax.broadcasted_iota(I32, (SORT_CHUNK, ROUTE_TM), 0) + c * SORT_CHUNK).astype(F32)
        sel = ((rowf == s1) | (rowf == s2)).astype(BF16)
        pick = lambda cols: jnp.dot(sel, h2_ref[:, cols], preferred_element_type=F32)
        stage[slot, c * SORT_CHUNK:(c + 1) * SORT_CHUNK, :] = _pack_pair(pick(slice(0, half)),
                                                                         pick(slice(half, 2 * half)))

    def start_piece(lrow, grow, size):
        pltpu.make_async_copy(stage.at[slot, pl.ds(lrow, size)], xs_ref.at[pl.ds(grow, size)],
                              sem.at[slot]).start()
    _for_each_run_piece(plan_ref, i, n_steps, start_piece)

    def wait_pieces(tile, s):
        def wait_piece(size):
            pltpu.make_async_copy(stage.at[s, pl.ds(0, size)], xs_ref.at[pl.ds(0, size)], sem.at[s]).wait()
        _for_each_total_piece(plan_ref, tile, n_steps, wait_piece)

    @pl.when(i > 0)
    def _():
        wait_pieces(i - 1, 1 - slot)

    @pl.when(i + 1 == n_steps)
    def _():
        wait_pieces(i, slot)


def dispatch(h2, rec, plan, seg_ends, n_valid, n_rows):
    n = h2.shape[0]
    return pl.pallas_call(
        _dispatch_kernel,
        grid_spec=pltpu.PrefetchScalarGridSpec(
            num_scalar_prefetch=3,
            grid=(n // ROUTE_TM,),
            in_specs=[pl.BlockSpec((ROUTE_TM, D_MODEL), lambda i, *_: (i, 0)),
                      pl.BlockSpec((8, ROUTE_TM), lambda i, *_: (0, i))],
            out_specs=pl.BlockSpec(memory_space=pl.ANY),
            scratch_shapes=[pltpu.VMEM((FFN_BLOCK, PACKED_W), U32),
                            pltpu.VMEM((2, LOCAL_ROWS, PACKED_W), U32),
                            pltpu.SemaphoreType.DMA((2,)),
                            pltpu.SemaphoreType.DMA(())],
        ),
        out_shape=jax.ShapeDtypeStruct((n_rows, PACKED_W), U32),
        compiler_params=_cparams("arbitrary"),
        name="dispatch",
    )(plan, seg_ends, n_valid, h2, rec)


def _ffn_kernel(be_ref, nv_ref, xs_ref, wg_ref, wu_ref, wd_ref, ys_ref, wg_sc, wu_sc, wd_sc):
    i = pl.program_id(0)
    prev = be_ref[jnp.maximum(i - 1, 0)]
    fresh = (i == 0) | (be_ref[i] != prev)

    @pl.when(fresh)
    def _():
        wg_sc[...] = wg_ref[0].astype(BF16)
        wu_sc[...] = wu_ref[0].astype(BF16)
        wd_sc[...] = wd_ref[0].astype(BF16)

    @pl.when(i < nv_ref[0])
    def _():
        xa, xb = (t.astype(BF16) for t in _unpack_halves(xs_ref[...]))
        half = PACKED_W
        gate = (jnp.dot(xa, wg_sc[:half, :], preferred_element_type=F32)
                + jnp.dot(xb, wg_sc[half:, :], preferred_element_type=F32))
        up = (jnp.dot(xa, wu_sc[:half, :], preferred_element_type=F32)
              + jnp.dot(xb, wu_sc[half:, :], preferred_element_type=F32))
        hid = (gate * (1.0 / (1.0 + jnp.exp(-gate))) * up).astype(BF16)
        ys_ref[...] = _pack_halves(jnp.dot(hid, wd_sc[...], preferred_element_type=F32))

    @pl.when(i >= nv_ref[0])
    def _():
        ys_ref[...] = jnp.zeros_like(ys_ref)


def expert_ffn(xs, block_expert, n_valid, w_gate, w_up, w_down):
    n_rows = xs.shape[0]
    n_blocks = n_rows // FFN_BLOCK
    wmap = lambda i, be, nv: (be[i], 0, 0)
    return pl.pallas_call(
        _ffn_kernel,
        grid_spec=pltpu.PrefetchScalarGridSpec(
            num_scalar_prefetch=2,
            grid=(n_blocks,),
            in_specs=[pl.BlockSpec((FFN_BLOCK, PACKED_W), lambda i, be, nv: (jnp.minimum(i, nv[0] - 1), 0)),
                      pl.BlockSpec((1, D_MODEL, D_EXPERT), wmap),
                      pl.BlockSpec((1, D_MODEL, D_EXPERT), wmap),
                      pl.BlockSpec((1, D_EXPERT, D_MODEL), wmap)],
            out_specs=pl.BlockSpec((FFN_BLOCK, PACKED_W), lambda i, be, nv: (i, 0)),
            scratch_shapes=[pltpu.VMEM((D_MODEL, D_EXPERT), BF16),
                            pltpu.VMEM((D_MODEL, D_EXPERT), BF16),
                            pltpu.VMEM((D_EXPERT, D_MODEL), BF16)],
        ),
        out_shape=jax.ShapeDtypeStruct((n_rows, PACKED_W), U32),
        compiler_params=_cparams("arbitrary"),
        name="expert_ffn",
    )(block_expert, n_valid, xs, w_gate, w_up, w_down)


def _combine_kernel(plan_ref, x1_ref, rec_ref, gf_ref, ys_ref, out_ref, buf, sem):
    i = pl.program_id(0)
    n_steps = pl.num_programs(0)
    slot = i % 2

    def fetch_tile(tile, s):
        def start_piece(lrow, grow, size):
            pltpu.make_async_copy(ys_ref.at[pl.ds(grow, size)], buf.at[s, pl.ds(lrow, size)], sem.at[s]).start()
        _for_each_run_piece(plan_ref, tile, n_steps, start_piece)

    @pl.when(i == 0)
    def _():
        buf[...] = jnp.zeros_like(buf)
        fetch_tile(0, 0)

    @pl.when(i + 1 < n_steps)
    def _():
        fetch_tile(i + 1, 1 - slot)

    def wait_piece(size):
        pltpu.make_async_copy(ys_ref.at[pl.ds(0, size)], buf.at[slot, pl.ds(0, size)], sem.at[slot]).wait()
    _for_each_total_piece(plan_ref, i, n_steps, wait_piece)

    rowf = lax.broadcasted_iota(I32, (LOCAL_ROWS, ROUTE_TM), 0).astype(F32)
    sel = (jnp.where(rowf == rec_ref[R_SLOT1:R_SLOT1 + 1, :], rec_ref[R_G1:R_G1 + 1, :], 0.0)
           + jnp.where(rowf == rec_ref[R_SLOT2:R_SLOT2 + 1, :], rec_ref[R_G2:R_G2 + 1, :], 0.0)).astype(BF16)
    ya, yb = (t.astype(BF16) for t in _unpack_halves(buf[slot]))
    half = PACKED_W
    tn = (((0,), (0,)), ((), ()))
    xa = x1_ref[:, :half] + lax.dot_general(sel, ya, tn, preferred_element_type=F32)
    xb = x1_ref[:, half:] + lax.dot_general(sel, yb, tn, preferred_element_type=F32)
    ms = (jnp.sum(xa * xa, axis=-1, keepdims=True) + jnp.sum(xb * xb, axis=-1, keepdims=True)) * (1.0 / D_MODEL)
    scale = lax.rsqrt(ms + EPS)
    out_ref[:, :half] = xa * scale * gf_ref[:, :half]
    out_ref[:, half:] = xb * scale * gf_ref[:, half:]


def combine(x1, rec, norm_f_g, ys, plan):
    n = x1.shape[0]
    tm = ROUTE_TM
    return pl.pallas_call(
        _combine_kernel,
        grid_spec=pltpu.PrefetchScalarGridSpec(
            num_scalar_prefetch=1,
            grid=(n // tm,),
            in_specs=[pl.BlockSpec((tm, D_MODEL), lambda i, p: (i, 0)),
                      pl.BlockSpec((8, tm), lambda i, p: (0, i)),
                      pl.BlockSpec((1, D_MODEL), lambda i, p: (0, 0)),
                      pl.BlockSpec(memory_space=pl.ANY)],
            out_specs=pl.BlockSpec((tm, D_MODEL), lambda i, p: (i, 0)),
            scratch_shapes=[pltpu.VMEM((2, LOCAL_ROWS, PACKED_W), U32),
                            pltpu.SemaphoreType.DMA((2,))],
        ),
        out_shape=jax.ShapeDtypeStruct((n, D_MODEL), F32),
        compiler_params=_cparams("arbitrary"),
        name="combine",
    )(plan, x1, rec, norm_f_g.reshape(1, D_MODEL), ys)


def _router_weights(w_group, b_group, w_router, b_router):
    w = jnp.zeros((LANES, D_MODEL), F32)
    w = w.at[:N_GROUPS].set(w_group.T).at[EXPERT_ROW0:EXPERT_ROW0 + N_EXPERTS].set(w_router.T)
    b = jnp.zeros((LANES, 1), F32)
    b = b.at[:N_GROUPS, 0].set(b_group).at[EXPERT_ROW0:EXPERT_ROW0 + N_EXPERTS, 0].set(b_router)
    hi = w.astype(BF16)
    lo = (w - hi.astype(F32)).astype(BF16)
    return hi, lo, b


def _sorted_rows_bound(n_tokens):
    worst = (2 * n_tokens + (n_tokens // ROUTE_TM) * N_EXPERTS * (RUN_ALIGN - 1)
             + N_EXPERTS * (FFN_BLOCK - 1))
    return -(-worst // FFN_BLOCK) * FFN_BLOCK


def _block_plan(seg, n_rows):
    seg_ends = seg[SEG_END, :N_EXPERTS].astype(I32)
    n_blocks = n_rows // FFN_BLOCK
    blk_start = jnp.arange(n_blocks, dtype=I32) * FFN_BLOCK
    block_expert = jnp.minimum(jnp.sum((seg_ends[None, :] <= blk_start[:, None]).astype(I32), axis=1),
                               N_EXPERTS - 1)
    n_valid = seg_ends[-1:] // FFN_BLOCK
    return seg_ends, block_expert, n_valid


def kernel(x, norm1_g, w_in, attn_norm_g, hgrn_gamma, hgrn_norm_g, w_out, norm2_g, w_group, b_group,
           w_router, b_router, w_gate, w_up, w_down, norm_f_g):
    batch, seq, d = x.shape
    assert seq == SEQ and d == D_MODEL and norm1_g.shape[0] == 1
    n = batch * seq
    x2 = x.reshape(n, d)
    qkv, hg = in_proj(x2, norm1_g[0], w_in[0].astype(BF16), batch)
    oa = attention(qkv)
    yh = hgrn(hg.reshape(batch, seq, HG_W), hgrn_gamma, hgrn_norm_g[0]).reshape(n, WIDTH)
    wr_hi, wr_lo, br = _router_weights(w_group[0], b_group[0], w_router[0], b_router[0])
    x1, h2, rec, tab = post_mix(oa, yh, x2, attn_norm_g[0], w_out[0].astype(BF16), norm2_g[0],
                                wr_hi, wr_lo, br)
    plan, seg = run_plan(tab)
    plan = plan.reshape(-1)
    n_rows = _sorted_rows_bound(n)
    seg_ends, block_expert, n_valid = _block_plan(seg, n_rows)
    xs = dispatch(h2, rec, plan, seg_ends, n_valid, n_rows)
    ys = expert_ffn(xs, block_expert, n_valid, w_gate[0], w_up[0], w_down[0])
    return combine(x1, rec, norm_f_g, ys, plan).reshape(batch, seq, d)
```

```python
import jax
import jax.numpy as jnp
from jax import lax
from jax.experimental import pallas as pl
from jax.experimental.pallas import tpu as pltpu

F32 = jnp.float32
BF16 = jnp.bfloat16
I32 = jnp.int32
U32 = jnp.uint32

D_MODEL = 1024
HEAD_DIM = 64
N_HEADS = 8
WIDTH = N_HEADS * HEAD_DIM
QKV_W = 3 * WIDTH
HG_W = 4 * WIDTH
SEQ = 2048
ATTN_BLOCK = 128
DILATIONS = (1, 4, 16)
N_RES = 16
RES_LEN = SEQ // N_RES
TILE_TOKENS = 512
U_PER_TILE = TILE_TOKENS // N_RES
HGRN_CHUNK = 32
HGRN_SUPER = 256
HGRN_GROUP = 4
N_GROUPS = 4
EXPERTS_PER_GROUP = 8
N_EXPERTS = 32
D_EXPERT = 512
FFN_BLOCK = 512
PACKED_W = D_MODEL // 2
EPS = 1e-6
NEG = -1e30
LOG2E = 1.4426950408889634
LANES = 128
VMEM_LIMIT = 56 * 1024 * 1024


def _cparams(*sem):
    return pltpu.CompilerParams(dimension_semantics=sem, vmem_limit_bytes=VMEM_LIMIT)


def _pack_pair(a, b):
    hi = pltpu.bitcast(a.astype(BF16).astype(F32), U32)
    lo = pltpu.bitcast(b.astype(BF16).astype(F32), U32)
    return hi | (lo >> 16)


def _pack_halves(x):
    w = x.shape[1] // 2
    return _pack_pair(x[:, :w], x[:, w:])


def _unpack_halves(p):
    return pltpu.bitcast(p & jnp.uint32(0xFFFF0000), F32), pltpu.bitcast(p << 16, F32)


N_LANE_TILES = WIDTH // LANES


def _tile_permutation():
    i = jnp.arange(TILE_TOKENS)
    src = N_RES * (i % U_PER_TILE) + i // U_PER_TILE
    return (src[:, None] == jnp.arange(TILE_TOKENS)[None, :]).astype(BF16)


def _in_proj_kernel(x_ref, g_ref, w_ref, perm_ref, qkv_ref, hg_ref):
    x = x_ref[...]
    ms = jnp.mean(x * x, axis=-1, keepdims=True)
    h = (x * lax.rsqrt(ms + EPS) * g_ref[...]).astype(BF16)
    hp = jnp.dot(perm_ref[...], h, preferred_element_type=F32).astype(BF16)
    for c in range(QKV_W // WIDTH):
        res = jnp.dot(hp, w_ref[:, c * WIDTH:(c + 1) * WIDTH], preferred_element_type=F32)
        for r in range(N_RES):
            for g in range(N_LANE_TILES):
                qkv_ref[r, g, c] = res[r * U_PER_TILE:(r + 1) * U_PER_TILE, g * LANES:(g + 1) * LANES]
    for j in range(HG_W // WIDTH):
        wsl = slice(QKV_W + j * WIDTH, QKV_W + (j + 1) * WIDTH)
        hg_ref[:, j * WIDTH:(j + 1) * WIDTH] = jnp.dot(h, w_ref[:, wsl], preferred_element_type=F32).astype(BF16)


def in_proj(x2, g, w_bf16, batch):
    n = x2.shape[0]
    tiles_per_b = SEQ // TILE_TOKENS
    return pl.pallas_call(
        _in_proj_kernel,
        grid=(n // TILE_TOKENS,),
        in_specs=[
            pl.BlockSpec((TILE_TOKENS, D_MODEL), lambda i: (i, 0)),
            pl.BlockSpec((1, D_MODEL), lambda i: (0, 0)),
            pl.BlockSpec((D_MODEL, QKV_W + HG_W), lambda i: (0, 0)),
            pl.BlockSpec((TILE_TOKENS, TILE_TOKENS), lambda i: (0, 0)),
        ],
        out_specs=[
            pl.BlockSpec((None, N_RES, N_LANE_TILES, 3, U_PER_TILE, LANES),
                         lambda i: (i // tiles_per_b, 0, 0, 0, i % tiles_per_b, 0)),
            pl.BlockSpec((TILE_TOKENS, HG_W), lambda i: (i, 0)),
        ],
        out_shape=[
            jax.ShapeDtypeStruct((batch, N_RES, N_LANE_TILES, 3, RES_LEN, LANES), F32),
            jax.ShapeDtypeStruct((n, HG_W), BF16),
        ],
        compiler_params=_cparams("arbitrary"),
        name="in_proj",
    )(x2, g.reshape(1, D_MODEL), w_bf16, _tile_permutation())


ATTN_GROUP16 = 16
ATTN_GROUP1 = 5
HEADS_PER_TILE = LANES // HEAD_DIM


def _attn_bias_tables():
    slopes = 2.0 ** (-8.0 * jnp.arange(1, N_HEADS + 1, dtype=F32) / N_HEADS)

    def table(qpos, kpos, dil):
        sd = qpos[:, None] - kpos[None, :]
        b = -slopes[:, None, None] * (sd * dil).astype(F32)[None] * LOG2E
        return jnp.where(((sd >= 0) & (sd <= ATTN_BLOCK))[None], b, NEG)

    q = jnp.arange(ATTN_BLOCK)
    k = jnp.arange(2 * ATTN_BLOCK)
    q1 = 16 * (q % 8) + q // 8 + ATTN_BLOCK
    k1 = 16 * (k % 16) + k // 16
    q4 = 4 * (q % 32) + q // 32 + ATTN_BLOCK
    k4 = 4 * (k % 64) + k // 64
    q16 = q + ATTN_BLOCK
    none = jnp.full((N_HEADS, ATTN_BLOCK, 2 * ATTN_BLOCK), NEG, F32)
    tabs = [jnp.concatenate([table(q1, k1, 1), table(q1, q1, 1)], axis=-1),
            jnp.concatenate([table(q4, k4, 4), table(q4, q4, 4)], axis=-1),
            jnp.concatenate([none, table(q16, q16, 16)], axis=-1)]
    return jnp.stack(tabs, axis=0)


def _attn_bias(bias_ref, d_idx, with_prev):
    ksl = slice(0, 2 * ATTN_BLOCK) if with_prev else slice(2 * ATTN_BLOCK, 3 * ATTN_BLOCK)
    return jnp.concatenate([bias_ref[d_idx, h, :, ksl] for h in range(HEADS_PER_TILE)], axis=0)


def _attn_blocks(blocks):
    nb = ATTN_BLOCK
    lane = lax.broadcasted_iota(I32, (nb, LANES), 1)
    head0 = lane < HEAD_DIM
    biases = [blk[3] for blk in blocks]
    blocks = [blk[:3] for blk in blocks]
    qs = [q * (HEAD_DIM ** -0.5 * LOG2E) for q, _, _ in blocks]
    qq = [jnp.concatenate([jnp.where(head0, q, 0.0), jnp.where(head0, 0.0, q)], axis=0).astype(BF16)
          for q in qs]
    ks = [k.astype(BF16) for _, k, _ in blocks]
    vs = [v.astype(BF16) for _, _, v in blocks]
    v_ones = [jnp.concatenate([v, jnp.ones_like(v)], axis=-1) for v in vs]
    s = [lax.dot_general(q2, k, (((1,), (1,)), ((), ())), preferred_element_type=F32) + bias
         for q2, k, bias in zip(qq, ks, biases)]
    m = [jnp.max(x, axis=-1, keepdims=True) for x in s]
    p = [jnp.exp2(x - mx).astype(BF16) for x, mx in zip(s, m)]
    acc = [jnp.dot(px, vo, preferred_element_type=F32) for px, vo in zip(p, v_ones)]
    head0_wide = jnp.concatenate([head0, head0], axis=-1)
    outs = []
    for a, mx in zip(acc, m):
        ol = jnp.where(head0_wide, a[:nb], a[nb:])
        l = ol[:, LANES:]
        outs.append((ol[:, :LANES] * (1.0 / l), jnp.where(head0, mx[:nb], mx[nb:]) + jnp.log2(l)))
    return outs


def _attn_kernel(qkv_ref, bias_ref, o_ref, o4_ref, l4_ref, o16_ref, l16_ref, fin_ref):
    grp = ATTN_GROUP16

    def body16(j, c):
        rs = [j * grp + a for a in range(grp)]
        bias = _attn_bias(bias_ref, 2, False)
        res = _attn_blocks([(qkv_ref[r, 0], qkv_ref[r, 1], qkv_ref[r, 2], bias) for r in rs])
        for r, (o, l) in zip(rs, res):
            o16_ref[r] = o
            l16_ref[r] = l
        return c
    for j in range(N_RES // grp):
        body16(j, 0)

    def gather4(c, rho, u0, nu):
        return jnp.concatenate([qkv_ref[rho + 4 * a, c, pl.ds(u0, nu), :] for a in range(4)], axis=0)

    def store4(rho, u0, o, l):
        for a in range(4):
            o4_ref[rho + 4 * a, pl.ds(u0, 32), :] = o[a * 32:(a + 1) * 32]
            l4_ref[rho + 4 * a, pl.ds(u0, 32), :] = l[a * 32:(a + 1) * 32]

    def body4(j, c):
        bias = _attn_bias(bias_ref, 1, True)
        todo = [(2 * j + a, 32 * n) for a in range(2) for n in range(1, 4)]
        res = _attn_blocks([(gather4(0, rho, u0, 32), gather4(1, rho, u0 - 32, 64),
                             gather4(2, rho, u0 - 32, 64), bias) for rho, u0 in todo])
        for (rho, u0), (o, l) in zip(todo, res):
            store4(rho, u0, o, l)
        return c
    for j in range(2):
        body4(j, 0)

    def gather1(ref, u0, nu, *lead):
        return jnp.concatenate([ref[(r,) + lead + (pl.ds(u0, nu), slice(None))] for r in range(N_RES)],
                               axis=0)

    def merge_and_store(u0, o1, l1):
        l4 = gather1(l4_ref, u0, 8)
        l16 = gather1(l16_ref, u0, 8)
        mx = jnp.maximum(jnp.maximum(l1, l4), l16)
        e1 = jnp.exp2(l1 - mx)
        e4 = jnp.exp2(l4 - mx)
        e16 = jnp.exp2(l16 - mx)
        inv = 1.0 / (e1 + e4 + e16)
        o = (e1 * inv) * o1 + (e4 * inv) * gather1(o4_ref, u0, 8) + (e16 * inv) * gather1(o16_ref, u0, 8)
        for r in range(N_RES):
            fin_ref[r, pl.ds(u0, 8), :] = o[r * 8:(r + 1) * 8]

    bias4 = _attn_bias(bias_ref, 1, False)
    first = [tuple(gather4(c, rho, 0, 32) for c in range(3)) + (bias4,) for rho in range(4)]
    first.append(tuple(gather1(qkv_ref, 0, 8, c) for c in range(3)) + (_attn_bias(bias_ref, 0, False),))
    res = _attn_blocks(first)
    for rho in range(4):
        store4(rho, 0, *res[rho])
    merge_and_store(0, *res[4])

    n_blocks1 = RES_LEN // 8 - 1

    def body1(j, c):
        u0s = [(1 + j * ATTN_GROUP1 + a) * 8 for a in range(ATTN_GROUP1)]
        bias = _attn_bias(bias_ref, 0, True)
        res = _attn_blocks([(gather1(qkv_ref, u0, 8, 0), gather1(qkv_ref, u0 - 8, 16, 1),
                             gather1(qkv_ref, u0 - 8, 16, 2), bias) for u0 in u0s])
        for u0, (o, l) in zip(u0s, res):
            merge_and_store(u0, o, l)
        return c
    for j in range(n_blocks1 // ATTN_GROUP1):
        body1(j, 0)

    def emit(r, c):
        o_ref[r] = fin_ref[r].astype(o_ref.dtype)
        return c
    lax.fori_loop(0, N_RES, emit, 0)


def attention(qkv):
    batch = qkv.shape[0]
    scratch = pltpu.VMEM((N_RES, RES_LEN, LANES), F32)
    return pl.pallas_call(
        _attn_kernel,
        grid=(batch, N_LANE_TILES),
        in_specs=[
            pl.BlockSpec((None, N_RES, None, 3, RES_LEN, LANES), lambda b, g: (b, 0, g, 0, 0, 0)),
            pl.BlockSpec((len(DILATIONS), HEADS_PER_TILE, ATTN_BLOCK, 3 * ATTN_BLOCK),
                         lambda b, g: (0, g, 0, 0)),
        ],
        out_specs=pl.BlockSpec((None, N_RES, None, RES_LEN, LANES), lambda b, g: (b, 0, g, 0, 0)),
        out_shape=jax.ShapeDtypeStruct((batch, N_RES, N_LANE_TILES, RES_LEN, LANES), BF16),
        scratch_shapes=[scratch] * 5,
        compiler_params=_cparams("arbitrary", "arbitrary"),
        name="dilated_attention",
    )(qkv, _attn_bias_tables())


def _split2(x):
    a = x.astype(BF16)
    b = (x - a.astype(F32)).astype(BF16)
    return a, b


def _hgrn_kernel(q_ref, f_ref, i_ref, g_ref, gamma_ref, ng_ref, tri_ref, y_ref):
    sup, ch = HGRN_SUPER, HGRN_CHUNK
    n_ch = sup // ch
    gam = gamma_ref[...]
    gmx = jnp.max(gam, axis=0, keepdims=True)
    ge = jnp.exp(gam - gmx)
    lb = ge[0:1] / jnp.sum(ge, axis=0, keepdims=True)
    ng = ng_ref[...]
    tri = tri_ref[...]
    ri = lax.broadcasted_iota(I32, (2 * sup, sup), 0) % sup
    ci = lax.broadcasted_iota(I32, (2 * sup, sup), 1)
    causal_bd2 = (ri // ch == ci // ch) & (ri >= ci)
    head0 = lax.broadcasted_iota(I32, (sup, LANES), 1) < HEAD_DIM
    row_chunk = lax.broadcasted_iota(I32, (sup, LANES), 0) // ch
    same_head = (lax.broadcasted_iota(I32, (LANES, LANES), 0) // HEAD_DIM
                 == lax.broadcasted_iota(I32, (LANES, LANES), 1) // HEAD_DIM)

    def sigmoid(x):
        return 1.0 / (1.0 + jnp.exp(-x))

    def dot_nt(a, b):
        return lax.dot_general(a, b, (((1,), (1,)), ((), ())), preferred_element_type=F32)

    def dot_tn(a, b):
        return lax.dot_general(a, b, (((0,), (0,)), ((), ())), preferred_element_type=F32)

    def expand(x):
        zero = jnp.zeros_like(x)
        return jnp.concatenate([jnp.where(row_chunk == c, x, zero) for c in range(n_ch)], axis=1)

    def body(j, state):
        grp = range(HGRN_GROUP)
        rows = [pl.ds((j * HGRN_GROUP + a) * sup, sup) for a in grp]
        fr = [f_ref[r, :].astype(F32) for r in rows]
        t = [jnp.exp(-jnp.abs(x)) for x in fr]
        rcp = [1.0 / (1.0 + x) for x in t]
        sig = [jnp.where(f >= 0, r, x * r) for f, r, x in zip(fr, rcp, t)]
        nsig = [jnp.where(f >= 0, x * r, r) for f, r, x in zip(fr, rcp, t)]
        logf = [jnp.log(lb + (1.0 - lb) * x) for x in sig]
        key = [(1.0 - lb) * x for x in nsig]
        parts = [_split2(x) for x in logf]
        b = [sum(jnp.dot(tri, p, preferred_element_type=F32) for p in ps) for ps in parts]
        totals = [[x[(c + 1) * ch - 1:(c + 1) * ch, :] for c in range(n_ch)] for x in b]
        b_last = [jnp.concatenate([jnp.broadcast_to(t_, (ch, LANES)) for t_ in ts], axis=0) for ts in totals]
        pad = jnp.zeros((LANES - n_ch, LANES), F32)
        decay_t = [jnp.exp(jnp.transpose(jnp.concatenate(ts + [pad], axis=0))) for ts in totals]
        qv = [q_ref[r, :].astype(F32) for r in rows]
        q_in = [(x * sigmoid(x) * jnp.exp(bb)).astype(BF16) for x, bb in zip(qv, b)]
        k_in = [(k * jnp.exp(-bb)).astype(BF16) for k, bb in zip(key, b)]
        k_end = [(k * jnp.exp(bl - bb)).astype(BF16) for k, bl, bb in zip(key, b_last, b)]
        vv = [i_ref[r, :] for r in rows]
        q2 = [jnp.concatenate([jnp.where(head0, x, jnp.zeros_like(x)), jnp.where(head0, jnp.zeros_like(x), x)],
                              axis=0) for x in q_in]
        att = [jnp.where(causal_bd2, dot_nt(x, k), 0.0).astype(BF16) for x, k in zip(q2, k_in)]
        intra = [jnp.dot(a_, v, preferred_element_type=F32) for a_, v in zip(att, vv)]
        o_intra = [jnp.where(head0, x[:sup], x[sup:]) for x in intra]
        ds_all = [dot_tn(k, expand(v)) for k, v in zip(k_end, vv)]
        s_prev = []
        for a in grp:
            for c in range(n_ch):
                s_prev.append(state.astype(BF16))
                ds = jnp.where(same_head, ds_all[a][:, c * LANES:(c + 1) * LANES], 0.0)
                state = decay_t[a][:, c:c + 1] * state + ds
        o_inter = [jnp.concatenate(
            [jnp.dot(q_in[a][c * ch:(c + 1) * ch], s_prev[a * n_ch + c], preferred_element_type=F32)
             for c in range(n_ch)], axis=0) for a in grp]
        o = [x + y for x, y in zip(o_intra, o_inter)]
        sq = [x * x for x in o]
        ss0 = [jnp.sum(jnp.where(head0, x, 0.0), axis=-1, keepdims=True) for x in sq]
        ss1 = [jnp.sum(jnp.where(head0, 0.0, x), axis=-1, keepdims=True) for x in sq]
        ms = [jnp.where(head0, x, y) * (1.0 / HEAD_DIM) for x, y in zip(ss0, ss1)]
        gv = [g_ref[r, :].astype(F32) for r in rows]
        for r, x, m_, g_ in zip(rows, o, ms, gv):
            y_ref[r, :] = (x * lax.rsqrt(m_ + EPS) * ng * (g_ * sigmoid(g_))).astype(y_ref.dtype)
        return state

    state = jnp.zeros((LANES, LANES), F32)
    for j in range(SEQ // (sup * HGRN_GROUP)):
        state = body(j, state)


def hgrn(hg3, gamma, norm_g):
    batch = hg3.shape[0]
    r = jnp.arange(HGRN_SUPER)
    same = (r[:, None] // HGRN_CHUNK) == (r[None, :] // HGRN_CHUNK)
    tri = (same & (r[:, None] >= r[None, :])).astype(BF16)
    col = lambda which: (lambda b, g: (b, 0, which * N_LANE_TILES + g))
    const2 = lambda b, g: (0, 0)
    return pl.pallas_call(
        _hgrn_kernel,
        grid=(batch, N_LANE_TILES),
        in_specs=[
            pl.BlockSpec((None, SEQ, LANES), col(0)),
            pl.BlockSpec((None, SEQ, LANES), col(1)),
            pl.BlockSpec((None, SEQ, LANES), col(2)),
            pl.BlockSpec((None, SEQ, LANES), col(3)),
            pl.BlockSpec((2, LANES), lambda b, g: (0, g)),
            pl.BlockSpec((1, LANES), lambda b, g: (0, g)),
            pl.BlockSpec((HGRN_SUPER, HGRN_SUPER), const2),
        ],
        out_specs=pl.BlockSpec((None, SEQ, LANES), lambda b, g: (b, 0, g)),
        out_shape=jax.ShapeDtypeStruct((batch, SEQ, WIDTH), BF16),
        compiler_params=_cparams("arbitrary", "arbitrary"),
        name="hgrn2",
    )(hg3, hg3, hg3, hg3, gamma.astype(F32), norm_g.reshape(1, WIDTH).astype(F32), tri)


def _post_mix_kernel(oa_ref, yh_ref, x_ref, ag_ref, wo_ref, g2_ref, wrh_ref, wrl_ref, br_ref, perm_t_ref,
                     tri_ref, lower_ref,
                     x1_ref, h2_ref, rec_ref, tab_ref, carry_ref, lg_sc):
    i = pl.program_id(0)

    @pl.when(i == 0)
    def _():
        lg_sc[...] = jnp.zeros_like(lg_sc)
    _route_tile(lg_sc[...], (i > 0).astype(F32), tri_ref, lower_ref, rec_ref, tab_ref, carry_ref)

    oa = jnp.concatenate(
        [jnp.concatenate([oa_ref[r, g] for g in range(N_LANE_TILES)], axis=1)
         for r in range(N_RES)], axis=0).astype(F32)
    ms = jnp.mean(oa * oa, axis=-1, keepdims=True)
    ya = (oa * lax.rsqrt(ms + EPS) * ag_ref[...]).astype(BF16)
    ya = jnp.dot(perm_t_ref[...], ya, preferred_element_type=F32).astype(BF16)
    mix = (jnp.dot(ya, wo_ref[:WIDTH, :], preferred_element_type=F32)
           + jnp.dot(yh_ref[...], wo_ref[WIDTH:, :], preferred_element_type=F32))
    x1 = x_ref[...] + mix
    x1_ref[...] = x1
    ms2 = jnp.mean(x1 * x1, axis=-1, keepdims=True)
    h2 = x1 * lax.rsqrt(ms2 + EPS) * g2_ref[...]
    hi = h2.astype(BF16)
    h2_ref[...] = hi
    lo = (h2 - hi.astype(F32)).astype(BF16)
    nt = (((1,), (1,)), ((), ()))
    wrh = wrh_ref[...]
    lg_sc[...] = (lax.dot_general(wrh, hi, nt, preferred_element_type=F32)
                  + lax.dot_general(wrh, lo, nt, preferred_element_type=F32)
                  + lax.dot_general(wrl_ref[...], hi, nt, preferred_element_type=F32)
                  + br_ref[...])


def post_mix(oa, yh, x2, attn_g, w_out_bf16, g2, wr_hi, wr_lo, br):
    n = x2.shape[0]
    tiles_per_b = SEQ // TILE_TOKENS
    assert ROUTE_TM == TILE_TOKENS
    n_tiles = n // TILE_TOKENS
    cur = lambda i: jnp.minimum(i, n_tiles - 1)
    prev = lambda i: jnp.maximum(i - 1, 0)
    row = lambda w: pl.BlockSpec((TILE_TOKENS, w), lambda i: (cur(i), 0))
    const = lambda r, c: pl.BlockSpec((r, c), lambda i: (0, 0))
    return pl.pallas_call(
        _post_mix_kernel,
        grid=(n_tiles + 1,),
        in_specs=[pl.BlockSpec((None, N_RES, N_LANE_TILES, U_PER_TILE, LANES),
                               lambda i: (cur(i) // tiles_per_b, 0, 0, cur(i) % tiles_per_b, 0)),
                  row(WIDTH), row(D_MODEL),
                  const(1, WIDTH), const(2 * WIDTH, D_MODEL), const(1, D_MODEL),
                  const(LANES, D_MODEL), const(LANES, D_MODEL), const(LANES, 1),
                  const(TILE_TOKENS, TILE_TOKENS), const(ROUTE_TM, ROUTE_TM), const(N_EXPERTS, N_EXPERTS)],
        out_specs=[row(D_MODEL), row(D_MODEL),
                   pl.BlockSpec((8, ROUTE_TM), lambda i: (0, prev(i))),
                   pl.BlockSpec((1, 8, LANES), lambda i: (prev(i), 0, 0))],
        out_shape=[jax.ShapeDtypeStruct((n, D_MODEL), F32),
                   jax.ShapeDtypeStruct((n, D_MODEL), BF16),
                   jax.ShapeDtypeStruct((8, n), F32),
                   jax.ShapeDtypeStruct((n_tiles, 8, LANES), F32)],
        scratch_shapes=[pltpu.VMEM((N_EXPERTS, LANES), F32), pltpu.VMEM((LANES, TILE_TOKENS), F32)],
        compiler_params=_cparams("arbitrary"),
        name="post_mix",
    )(oa, yh, x2, attn_g.reshape(1, WIDTH), w_out_bf16, g2.reshape(1, D_MODEL), wr_hi, wr_lo, br,
      _tile_permutation().T, *_route_constants())


ROUTE_TM = 512
EXPERT_ROW0 = 32
R_E1, R_E2, R_SLOT1, R_SLOT2, R_G1, R_G2 = 0, 1, 2, 3, 4, 5
T_CARRY, T_ROWS, T_OFF = 0, 1, 2
SEG_END = 0
TOTAL_LANE = N_EXPERTS
RUN_ALIGN = 8
LOCAL_ROWS = 2 * ROUTE_TM + N_EXPERTS * RUN_ALIGN


def _route_tile(lg, live, tri_ref, lower_ref, rec_ref, tab_ref, carry_ref):
    i = pl.program_id(0)

    @pl.when(i == 0)
    def _():
        carry_ref[...] = jnp.zeros_like(carry_ref)

    tm = lg.shape[1]
    sub8 = lax.broadcasted_iota(I32, (8, tm), 0).astype(F32)
    big = 8.0
    gmask = sub8 < N_GROUPS
    gl = jnp.where(gmask, lg[0:8], NEG)
    gmax = jnp.max(gl, axis=0, keepdims=True)
    gsel = jnp.min(jnp.where(gmask & (gl == gmax), sub8, big), axis=0, keepdims=True)
    gsum = jnp.sum(jnp.where(gmask, jnp.exp(gl - gmax), 0.0), axis=0, keepdims=True)
    w_g = 1.0 / gsum
    el = jnp.zeros((EXPERTS_PER_GROUP, tm), F32)
    for g in range(N_GROUPS):
        lo = EXPERT_ROW0 + g * EXPERTS_PER_GROUP
        el = jnp.where(gsel == g, lg[lo:lo + EXPERTS_PER_GROUP], el)
    v1 = jnp.max(el, axis=0, keepdims=True)
    i1 = jnp.min(jnp.where(el == v1, sub8, big), axis=0, keepdims=True)
    el2 = jnp.where(sub8 == i1, NEG, el)
    v2 = jnp.max(el2, axis=0, keepdims=True)
    i2 = jnp.min(jnp.where((el2 == v2) & (sub8 != i1), sub8, big), axis=0, keepdims=True)
    ex = jnp.exp(v2 - v1)
    den = 1.0 / (1.0 + ex)
    g1 = w_g * den
    g2 = w_g * ex * den
    e1 = gsel * EXPERTS_PER_GROUP + i1
    e2 = gsel * EXPERTS_PER_GROUP + i2
    sub_e = lax.broadcasted_iota(I32, (N_EXPERTS, tm), 0).astype(F32)
    oh1 = sub_e == e1
    oh2 = sub_e == e2
    onehot = (oh1 | oh2).astype(BF16)
    before = jnp.dot(onehot, tri_ref[...], preferred_element_type=F32)
    count = jnp.sum(onehot.astype(F32), axis=1, keepdims=True)
    units = jnp.floor((count + (RUN_ALIGN - 1)) * (1.0 / RUN_ALIGN)) * live
    units = jnp.broadcast_to(units, (N_EXPERTS, LANES))
    rows = units * RUN_ALIGN
    off = RUN_ALIGN * jnp.dot(lower_ref[...], units.astype(BF16), preferred_element_type=F32)
    place = off[:, 0:1] + before
    slot1 = jnp.sum(jnp.where(oh1, place, 0.0), axis=0, keepdims=True)
    slot2 = jnp.sum(jnp.where(oh2, place, 0.0), axis=0, keepdims=True)
    total = jnp.sum(rows, axis=0, keepdims=True)
    eye = (lax.broadcasted_iota(I32, (N_EXPERTS, LANES), 0) == lax.broadcasted_iota(I32, (N_EXPERTS, LANES), 1))
    lane1 = lax.broadcasted_iota(I32, (1, LANES), 1)
    as_row = lambda col: jnp.where(lane1 == TOTAL_LANE, total,
                                   jnp.sum(jnp.where(eye, col, 0.0), axis=0, keepdims=True))
    sub = lax.broadcasted_iota(I32, (8, LANES), 0)
    tab_ref[0] = jnp.where(sub == T_CARRY, as_row(carry_ref[...]),
                           jnp.where(sub == T_ROWS, as_row(rows), jnp.where(sub == T_OFF, as_row(off), 0.0)))
    carry_ref[...] += rows
    rec = jnp.zeros((8, tm), F32)
    for slot, val in ((R_E1, e1), (R_E2, e2), (R_SLOT1, slot1),
                      (R_SLOT2, slot2), (R_G1, g1), (R_G2, g2)):
        rec = jnp.where(sub8 == slot, val, rec)
    rec_ref[...] = rec


def _route_constants():
    r = jnp.arange(ROUTE_TM)
    tri = (r[:, None] < r[None, :]).astype(BF16)
    e = jnp.arange(N_EXPERTS)
    lower = (e[:, None] > e[None, :]).astype(BF16)
    return tri, lower


P_START, P_ROWS, P_OFF = 0, 1, 2


def _plan_kernel(carry_ref, rows_ref, off_ref, upper_ref, plan_ref, seg_ref):
    n_tiles = carry_ref.shape[0]
    lane = lax.broadcasted_iota(I32, (1, LANES), 1)
    last = pl.ds(n_tiles - 1, 1)
    totals = jnp.where(lane < N_EXPERTS, carry_ref[last, :] + rows_ref[last, :], 0.0)
    blocks = jnp.floor((totals + (FFN_BLOCK - 1)) * (1.0 / FFN_BLOCK))
    starts = FFN_BLOCK * jnp.dot(jnp.broadcast_to(blocks, (8, LANES)).astype(BF16), upper_ref[...],
                                 preferred_element_type=F32)[0:1]
    seg_ref[...] = jnp.broadcast_to(starts + blocks * FFN_BLOCK, seg_ref.shape)
    plan_ref[P_START] = (carry_ref[...] + starts).astype(I32)
    plan_ref[P_ROWS] = rows_ref[...].astype(I32)
    plan_ref[P_OFF] = off_ref[...].astype(I32)


def run_plan(tab):
    n_tiles = tab.shape[0]
    ln = jnp.arange(LANES)
    upper = (ln[:, None] < ln[None, :]).astype(BF16)
    return pl.pallas_call(
        _plan_kernel,
        out_shape=[jax.ShapeDtypeStruct((3, n_tiles, LANES), I32),
                   jax.ShapeDtypeStruct((8, LANES), F32)],
        compiler_params=pltpu.CompilerParams(vmem_limit_bytes=VMEM_LIMIT),
        name="run_plan",
    )(tab[:, T_CARRY], tab[:, T_ROWS], tab[:, T_OFF], upper)


RUN_PIECE = 64
SMALL_PIECES = (32, 16, 8)
TOTAL_PIECES = (1024, 512, 256, 128, 64, 32, 16, 8)
SORT_CHUNK = 256


def _for_each_run_piece(plan_ref, tile, n_tiles, fn):
    plane = n_tiles * LANES

    def per_expert(e, c):
        idx = tile * LANES + e
        start = plan_ref[P_START * plane + idx]
        rows = plan_ref[P_ROWS * plane + idx]
        off = plan_ref[P_OFF * plane + idx]

        def big(j, c2):
            fn(pl.multiple_of(off + j * RUN_PIECE, RUN_ALIGN), pl.multiple_of(start + j * RUN_PIECE, RUN_ALIGN),
               RUN_PIECE)
            return c2
        lax.fori_loop(0, lax.shift_right_logical(rows, 6), big, 0)
        for size in SMALL_PIECES:
            done = jnp.bitwise_and(rows, -2 * size)

            @pl.when(jnp.bitwise_and(rows, size) != 0)
            def _():
                fn(pl.multiple_of(off + done, RUN_ALIGN), pl.multiple_of(start + done, RUN_ALIGN), size)
        return c
    lax.fori_loop(0, N_EXPERTS, per_expert, 0)


def _for_each_total_piece(plan_ref, tile, n_tiles, fn):
    total = plan_ref[P_OFF * n_tiles * LANES + tile * LANES + TOTAL_LANE]
    for size in TOTAL_PIECES:
        @pl.when(jnp.bitwise_and(total, size) != 0)
        def _():
            fn(size)


def _dispatch_kernel(plan_ref, ends_ref, nv_ref, h2_ref, rec_ref, xs_ref, zero_buf, stage, sem, zsem):
    i = pl.program_id(0)
    n_steps = pl.num_programs(0)
    n_blocks = xs_ref.shape[0] // FFN_BLOCK

    @pl.when(i == 0)
    def _():
        zero_buf[...] = jnp.zeros_like(zero_buf)

        def zero_copy(row0):
            return pltpu.make_async_copy(zero_buf, xs_ref.at[pl.ds(row0, FFN_BLOCK)], zsem)

        def seg_end(e):
            return ends_ref[e], ends_ref[e] > jnp.where(e > 0, ends_ref[jnp.maximum(e - 1, 0)], 0)

        def start_e(e, c):
            end, nonempty = seg_end(e)

            @pl.when(nonempty)
            def _():
                zero_copy(pl.multiple_of(end - FFN_BLOCK, FFN_BLOCK)).start()
            return c
        lax.fori_loop(0, N_EXPERTS, start_e, 0)

        def start_b(blk, c):
            zero_copy(pl.multiple_of(blk * FFN_BLOCK, FFN_BLOCK)).start()
            return c
        lax.fori_loop(nv_ref[0], n_blocks, start_b, 0)

        def wait_e(e, c):
            _, nonempty = seg_end(e)

            @pl.when(nonempty)
            def _():
                zero_copy(0).wait()
            return c
        lax.fori_loop(0, N_EXPERTS, wait_e, 0)

        def wait_b(blk, c):
            zero_copy(0).wait()
            return c
        lax.fori_loop(nv_ref[0], n_blocks, wait_b, 0)

    slot = i % 2
    s1 = rec_ref[R_SLOT1:R_SLOT1 + 1, :]
    s2 = rec_ref[R_SLOT2:R_SLOT2 + 1, :]
    half = PACKED_W
    for c in range(LOCAL_ROWS // SORT_CHUNK):
        rowf = (lax.broadcasted_iota(I32, (SORT_CHUNK, ROUTE_TM), 0) + c * SORT_CHUNK).astype(F32)
        sel = ((rowf == s1) | (rowf == s2)).astype(BF16)
        pick = lambda cols: jnp.dot(sel, h2_ref[:, cols], preferred_element_type=F32)
        stage[slot, c * SORT_CHUNK:(c + 1) * SORT_CHUNK, :] = _pack_pair(pick(slice(0, half)),
                                                                         pick(slice(half, 2 * half)))

    def start_piece(lrow, grow, size):
        pltpu.make_async_copy(stage.at[slot, pl.ds(lrow, size)], xs_ref.at[pl.ds(grow, size)],
                              sem.at[slot]).start()
    _for_each_run_piece(plan_ref, i, n_steps, start_piece)

    def wait_pieces(tile, s):
        def wait_piece(size):
            pltpu.make_async_copy(stage.at[s, pl.ds(0, size)], xs_ref.at[pl.ds(0, size)], sem.at[s]).wait()
        _for_each_total_piece(plan_ref, tile, n_steps, wait_piece)

    @pl.when(i > 0)
    def _():
        wait_pieces(i - 1, 1 - slot)

    @pl.when(i + 1 == n_steps)
    def _():
        wait_pieces(i, slot)


def dispatch(h2, rec, plan, seg_ends, n_valid, n_rows):
    n = h2.shape[0]
    return pl.pallas_call(
        _dispatch_kernel,
        grid_spec=pltpu.PrefetchScalarGridSpec(
            num_scalar_prefetch=3,
            grid=(n // ROUTE_TM,),
            in_specs=[pl.BlockSpec((ROUTE_TM, D_MODEL), lambda i, *_: (i, 0)),
                      pl.BlockSpec((8, ROUTE_TM), lambda i, *_: (0, i))],
            out_specs=pl.BlockSpec(memory_space=pl.ANY),
            scratch_shapes=[pltpu.VMEM((FFN_BLOCK, PACKED_W), U32),
                            pltpu.VMEM((2, LOCAL_ROWS, PACKED_W), U32),
                            pltpu.SemaphoreType.DMA((2,)),
                            pltpu.SemaphoreType.DMA(())],
        ),
        out_shape=jax.ShapeDtypeStruct((n_rows, PACKED_W), U32),
        compiler_params=_cparams("arbitrary"),
        name="dispatch",
    )(plan, seg_ends, n_valid, h2, rec)


def _ffn_kernel(ends_ref, xs_ref, wg_ref, wu_ref, wd_ref, ys_ref,
                wg_sc, wu_sc, wd_sc, xbuf, ybuf, in_sem, out_sem, pend):
    e = pl.program_id(0)
    last = pl.num_programs(0) - 1
    blk = FFN_BLOCK
    base = jnp.where(e > 0, ends_ref[jnp.maximum(e - 1, 0)], 0)
    n_blk = (ends_ref[e] - base) // blk

    def load(row, s):
        return pltpu.make_async_copy(xs_ref.at[pl.ds(row, blk)], xbuf.at[s], in_sem.at[s])

    def store(row, s):
        return pltpu.make_async_copy(ybuf.at[s], ys_ref.at[pl.ds(row, blk)], out_sem.at[s])

    @pl.when(e == 0)
    def _():
        pend[0] = 0
        pend[1] = 0

        @pl.when(n_blk > 0)
        def _():
            load(pl.multiple_of(base, blk), 0).start()

    wg_sc[...] = wg_ref[0].astype(BF16)
    wu_sc[...] = wu_ref[0].astype(BF16)
    wd_sc[...] = wd_ref[0].astype(BF16)

    def body(j, c):
        s = j % 2
        load(0, s).wait()

        @pl.when(j + 1 < n_blk)
        def _():
            load(pl.multiple_of(base + (j + 1) * blk, blk), 1 - s).start()

        @pl.when(pend[s] == 1)
        def _():
            store(0, s).wait()
        xa, xb = (t.astype(BF16) for t in _unpack_halves(xbuf[s]))
        half = PACKED_W
        gate = (jnp.dot(xa, wg_sc[:half, :], preferred_element_type=F32)
                + jnp.dot(xb, wg_sc[half:, :], preferred_element_type=F32))
        up = (jnp.dot(xa, wu_sc[:half, :], preferred_element_type=F32)
              + jnp.dot(xb, wu_sc[half:, :], preferred_element_type=F32))
        hid = (gate * (1.0 / (1.0 + jnp.exp(-gate))) * up).astype(BF16)
        ybuf[s] = _pack_halves(jnp.dot(hid, wd_sc[...], preferred_element_type=F32))
        store(pl.multiple_of(base + j * blk, blk), s).start()
        pend[s] = 1
        return c
    lax.fori_loop(0, n_blk, body, 0)

    @pl.when(e < last)
    def _():
        nxt_base = ends_ref[e]

        @pl.when(ends_ref[jnp.minimum(e + 1, last)] > nxt_base)
        def _():
            load(pl.multiple_of(nxt_base, blk), 0).start()

    @pl.when(e == last)
    def _():
        for s in range(2):
            @pl.when(pend[s] == 1)
            def _():
                store(0, s).wait()
        ybuf[0] = jnp.zeros((blk, PACKED_W), U32)

        def zero_block(b, c):
            cp = store(pl.multiple_of(b * blk, blk), 0)
            cp.start()
            cp.wait()
            return c
        lax.fori_loop(ends_ref[last] // blk, ys_ref.shape[0] // blk, zero_block, 0)


def expert_ffn(xs, seg_ends, w_gate, w_up, w_down):
    n_rows = xs.shape[0]
    wspec = lambda r, c: pl.BlockSpec((1, r, c), lambda e, ends: (e, 0, 0))
    return pl.pallas_call(
        _ffn_kernel,
        grid_spec=pltpu.PrefetchScalarGridSpec(
            num_scalar_prefetch=1,
            grid=(N_EXPERTS,),
            in_specs=[pl.BlockSpec(memory_space=pl.ANY),
                      wspec(D_MODEL, D_EXPERT), wspec(D_MODEL, D_EXPERT), wspec(D_EXPERT, D_MODEL)],
            out_specs=pl.BlockSpec(memory_space=pl.ANY),
            scratch_shapes=[pltpu.VMEM((D_MODEL, D_EXPERT), BF16),
                            pltpu.VMEM((D_MODEL, D_EXPERT), BF16),
                            pltpu.VMEM((D_EXPERT, D_MODEL), BF16),
                            pltpu.VMEM((2, FFN_BLOCK, PACKED_W), U32),
                            pltpu.VMEM((2, FFN_BLOCK, PACKED_W), U32),
                            pltpu.SemaphoreType.DMA((2,)),
                            pltpu.SemaphoreType.DMA((2,)),
                            pltpu.SMEM((2,), I32)],
        ),
        out_shape=jax.ShapeDtypeStruct((n_rows, PACKED_W), U32),
        compiler_params=_cparams("arbitrary"),
        name="expert_ffn",
    )(seg_ends, xs, w_gate, w_up, w_down)


def _combine_kernel(plan_ref, x1_ref, rec_ref, gf_ref, ys_ref, out_ref, buf, sem):
    i = pl.program_id(0)
    n_steps = pl.num_programs(0)
    slot = i % 2

    def fetch_tile(tile, s):
        def start_piece(lrow, grow, size):
            pltpu.make_async_copy(ys_ref.at[pl.ds(grow, size)], buf.at[s, pl.ds(lrow, size)], sem.at[s]).start()
        _for_each_run_piece(plan_ref, tile, n_steps, start_piece)

    @pl.when(i == 0)
    def _():
        buf[...] = jnp.zeros_like(buf)
        fetch_tile(0, 0)

    @pl.when(i + 1 < n_steps)
    def _():
        fetch_tile(i + 1, 1 - slot)

    def wait_piece(size):
        pltpu.make_async_copy(ys_ref.at[pl.ds(0, size)], buf.at[slot, pl.ds(0, size)], sem.at[slot]).wait()
    _for_each_total_piece(plan_ref, i, n_steps, wait_piece)

    rowf = lax.broadcasted_iota(I32, (LOCAL_ROWS, ROUTE_TM), 0).astype(F32)
    sel = (jnp.where(rowf == rec_ref[R_SLOT1:R_SLOT1 + 1, :], rec_ref[R_G1:R_G1 + 1, :], 0.0)
           + jnp.where(rowf == rec_ref[R_SLOT2:R_SLOT2 + 1, :], rec_ref[R_G2:R_G2 + 1, :], 0.0)).astype(BF16)
    ya, yb = (t.astype(BF16) for t in _unpack_halves(buf[slot]))
    half = PACKED_W
    tn = (((0,), (0,)), ((), ()))
    xa = x1_ref[:, :half] + lax.dot_general(sel, ya, tn, preferred_element_type=F32)
    xb = x1_ref[:, half:] + lax.dot_general(sel, yb, tn, preferred_element_type=F32)
    ms = (jnp.sum(xa * xa, axis=-1, keepdims=True) + jnp.sum(xb * xb, axis=-1, keepdims=True)) * (1.0 / D_MODEL)
    scale = lax.rsqrt(ms + EPS)
    out_ref[:, :half] = xa * scale * gf_ref[:, :half]
    out_ref[:, half:] = xb * scale * gf_ref[:, half:]


def combine(x1, rec, norm_f_g, ys, plan):
    n = x1.shape[0]
    tm = ROUTE_TM
    return pl.pallas_call(
        _combine_kernel,
        grid_spec=pltpu.PrefetchScalarGridSpec(
            num_scalar_prefetch=1,
            grid=(n // tm,),
            in_specs=[pl.BlockSpec((tm, D_MODEL), lambda i, p: (i, 0)),
                      pl.BlockSpec((8, tm), lambda i, p: (0, i)),
                      pl.BlockSpec((1, D_MODEL), lambda i, p: (0, 0)),
                      pl.BlockSpec(memory_space=pl.ANY)],
            out_specs=pl.BlockSpec((tm, D_MODEL), lambda i, p: (i, 0)),
            scratch_shapes=[pltpu.VMEM((2, LOCAL_ROWS, PACKED_W), U32),
                            pltpu.SemaphoreType.DMA((2,))],
        ),
        out_shape=jax.ShapeDtypeStruct((n, D_MODEL), F32),
        compiler_params=_cparams("arbitrary"),
        name="combine",
    )(plan, x1, rec, norm_f_g.reshape(1, D_MODEL), ys)


def _router_weights(w_group, b_group, w_router, b_router):
    w = jnp.zeros((LANES, D_MODEL), F32)
    w = w.at[:N_GROUPS].set(w_group.T).at[EXPERT_ROW0:EXPERT_ROW0 + N_EXPERTS].set(w_router.T)
    b = jnp.zeros((LANES, 1), F32)
    b = b.at[:N_GROUPS, 0].set(b_group).at[EXPERT_ROW0:EXPERT_ROW0 + N_EXPERTS, 0].set(b_router)
    hi = w.astype(BF16)
    lo = (w - hi.astype(F32)).astype(BF16)
    return hi, lo, b


def _sorted_rows_bound(n_tokens):
    worst = (2 * n_tokens + (n_tokens // ROUTE_TM) * N_EXPERTS * (RUN_ALIGN - 1)
             + N_EXPERTS * (FFN_BLOCK - 1))
    return -(-worst // FFN_BLOCK) * FFN_BLOCK


def _block_plan(seg):
    seg_ends = seg[SEG_END, :N_EXPERTS].astype(I32)
    return seg_ends, seg_ends[-1:] // FFN_BLOCK


def kernel(x, norm1_g, w_in, attn_norm_g, hgrn_gamma, hgrn_norm_g, w_out, norm2_g, w_group, b_group,
           w_router, b_router, w_gate, w_up, w_down, norm_f_g):
    batch, seq, d = x.shape
    assert seq == SEQ and d == D_MODEL and norm1_g.shape[0] == 1
    n = batch * seq
    x2 = x.reshape(n, d)
    qkv, hg = in_proj(x2, norm1_g[0], w_in[0].astype(BF16), batch)
    oa = attention(qkv)
    yh = hgrn(hg.reshape(batch, seq, HG_W), hgrn_gamma, hgrn_norm_g[0]).reshape(n, WIDTH)
    wr_hi, wr_lo, br = _router_weights(w_group[0], b_group[0], w_router[0], b_router[0])
    x1, h2, rec, tab = post_mix(oa, yh, x2, attn_norm_g[0], w_out[0].astype(BF16), norm2_g[0],
                                wr_hi, wr_lo, br)
    plan, seg = run_plan(tab)
    plan = plan.reshape(-1)
    n_rows = _sorted_rows_bound(n)
    seg_ends, n_valid = _block_plan(seg)
    xs = dispatch(h2, rec, plan, seg_ends, n_valid, n_rows)
    ys = expert_ffn(xs, seg_ends, w_gate[0], w_up[0], w_down[0])
    return combine(x1, rec, norm_f_g, ys, plan).reshape(batch, seq, d)
```

```python
import jax
import jax.numpy as jnp
from jax import lax
from jax.experimental import pallas as pl
from jax.experimental.pallas import tpu as pltpu

F32 = jnp.float32
BF16 = jnp.bfloat16
I32 = jnp.int32
U32 = jnp.uint32

D_MODEL = 1024
HEAD_DIM = 64
N_HEADS = 8
WIDTH = N_HEADS * HEAD_DIM
QKV_W = 3 * WIDTH
HG_W = 4 * WIDTH
SEQ = 2048
ATTN_BLOCK = 128
DILATIONS = (1, 4, 16)
N_RES = 16
RES_LEN = SEQ // N_RES
TILE_TOKENS = 512
U_PER_TILE = TILE_TOKENS // N_RES
HGRN_CHUNK = 32
HGRN_SUPER = 256
HGRN_GROUP = 4
N_GROUPS = 4
EXPERTS_PER_GROUP = 8
N_EXPERTS = 32
D_EXPERT = 512
FFN_BLOCK = 512
PACKED_W = D_MODEL // 2
EPS = 1e-6
NEG = -1e30
LOG2E = 1.4426950408889634
LANES = 128
VMEM_LIMIT = 56 * 1024 * 1024


def _cparams(*sem):
    return pltpu.CompilerParams(dimension_semantics=sem, vmem_limit_bytes=VMEM_LIMIT)


def _pack_pair(a, b):
    hi = pltpu.bitcast(a.astype(BF16).astype(F32), U32)
    lo = pltpu.bitcast(b.astype(BF16).astype(F32), U32)
    return hi | (lo >> 16)


def _pack_halves(x):
    w = x.shape[1] // 2
    return _pack_pair(x[:, :w], x[:, w:])


def _unpack_halves(p):
    return pltpu.bitcast(p & jnp.uint32(0xFFFF0000), F32), pltpu.bitcast(p << 16, F32)


N_LANE_TILES = WIDTH // LANES


def _tile_permutation():
    i = jnp.arange(TILE_TOKENS)
    src = N_RES * (i % U_PER_TILE) + i // U_PER_TILE
    return (src[:, None] == jnp.arange(TILE_TOKENS)[None, :]).astype(BF16)


def _in_proj_kernel(x_ref, g_ref, w_ref, perm_ref, qkv_ref, hg_ref):
    x = x_ref[...]
    ms = jnp.mean(x * x, axis=-1, keepdims=True)
    h = (x * lax.rsqrt(ms + EPS) * g_ref[...]).astype(BF16)
    hp = jnp.dot(perm_ref[...], h, preferred_element_type=F32).astype(BF16)
    for c in range(QKV_W // WIDTH):
        res = jnp.dot(hp, w_ref[:, c * WIDTH:(c + 1) * WIDTH], preferred_element_type=F32)
        for r in range(N_RES):
            for g in range(N_LANE_TILES):
                qkv_ref[r, g, c] = res[r * U_PER_TILE:(r + 1) * U_PER_TILE, g * LANES:(g + 1) * LANES]
    for j in range(HG_W // WIDTH):
        wsl = slice(QKV_W + j * WIDTH, QKV_W + (j + 1) * WIDTH)
        hg_ref[:, j * WIDTH:(j + 1) * WIDTH] = jnp.dot(h, w_ref[:, wsl], preferred_element_type=F32).astype(BF16)


def in_proj(x2, g, w_bf16, batch):
    n = x2.shape[0]
    tiles_per_b = SEQ // TILE_TOKENS
    return pl.pallas_call(
        _in_proj_kernel,
        grid=(n // TILE_TOKENS,),
        in_specs=[
            pl.BlockSpec((TILE_TOKENS, D_MODEL), lambda i: (i, 0)),
            pl.BlockSpec((1, D_MODEL), lambda i: (0, 0)),
            pl.BlockSpec((D_MODEL, QKV_W + HG_W), lambda i: (0, 0)),
            pl.BlockSpec((TILE_TOKENS, TILE_TOKENS), lambda i: (0, 0)),
        ],
        out_specs=[
            pl.BlockSpec((None, N_RES, N_LANE_TILES, 3, U_PER_TILE, LANES),
                         lambda i: (i // tiles_per_b, 0, 0, 0, i % tiles_per_b, 0)),
            pl.BlockSpec((TILE_TOKENS, HG_W), lambda i: (i, 0)),
        ],
        out_shape=[
            jax.ShapeDtypeStruct((batch, N_RES, N_LANE_TILES, 3, RES_LEN, LANES), F32),
            jax.ShapeDtypeStruct((n, HG_W), BF16),
        ],
        compiler_params=_cparams("arbitrary"),
        name="in_proj",
    )(x2, g.reshape(1, D_MODEL), w_bf16, _tile_permutation())


ATTN_GROUP16 = 16
ATTN_GROUP1 = 5
HEADS_PER_TILE = LANES // HEAD_DIM


def _attn_bias_tables():
    slopes = 2.0 ** (-8.0 * jnp.arange(1, N_HEADS + 1, dtype=F32) / N_HEADS)

    def table(qpos, kpos, dil):
        sd = qpos[:, None] - kpos[None, :]
        b = -slopes[:, None, None] * (sd * dil).astype(F32)[None] * LOG2E
        return jnp.where(((sd >= 0) & (sd <= ATTN_BLOCK))[None], b, NEG)

    q = jnp.arange(ATTN_BLOCK)
    k = jnp.arange(2 * ATTN_BLOCK)
    q1 = 16 * (q % 8) + q // 8 + ATTN_BLOCK
    k1 = 16 * (k % 16) + k // 16
    q4 = 4 * (q % 32) + q // 32 + ATTN_BLOCK
    k4 = 4 * (k % 64) + k // 64
    q16 = q + ATTN_BLOCK
    none = jnp.full((N_HEADS, ATTN_BLOCK, 2 * ATTN_BLOCK), NEG, F32)
    tabs = [jnp.concatenate([table(q1, k1, 1), table(q1, q1, 1)], axis=-1),
            jnp.concatenate([table(q4, k4, 4), table(q4, q4, 4)], axis=-1),
            jnp.concatenate([none, table(q16, q16, 16)], axis=-1)]
    return jnp.stack(tabs, axis=0)


def _attn_bias(bias_ref, d_idx, with_prev):
    ksl = slice(0, 2 * ATTN_BLOCK) if with_prev else slice(2 * ATTN_BLOCK, 3 * ATTN_BLOCK)
    return jnp.concatenate([bias_ref[d_idx, h, :, ksl] for h in range(HEADS_PER_TILE)], axis=0)


def _attn_blocks(blocks):
    nb = ATTN_BLOCK
    lane = lax.broadcasted_iota(I32, (nb, LANES), 1)
    head0 = lane < HEAD_DIM
    biases = [blk[3] for blk in blocks]
    blocks = [blk[:3] for blk in blocks]
    qs = [q * (HEAD_DIM ** -0.5 * LOG2E) for q, _, _ in blocks]
    qq = [jnp.concatenate([jnp.where(head0, q, 0.0), jnp.where(head0, 0.0, q)], axis=0).astype(BF16)
          for q in qs]
    ks = [k.astype(BF16) for _, k, _ in blocks]
    vs = [v.astype(BF16) for _, _, v in blocks]
    v_ones = [jnp.concatenate([v, jnp.ones_like(v)], axis=-1) for v in vs]
    s = [lax.dot_general(q2, k, (((1,), (1,)), ((), ())), preferred_element_type=F32) + bias
         for q2, k, bias in zip(qq, ks, biases)]
    m = [jnp.max(x, axis=-1, keepdims=True) for x in s]
    p = [jnp.exp2(x - mx).astype(BF16) for x, mx in zip(s, m)]
    acc = [jnp.dot(px, vo, preferred_element_type=F32) for px, vo in zip(p, v_ones)]
    head0_wide = jnp.concatenate([head0, head0], axis=-1)
    outs = []
    for a, mx in zip(acc, m):
        ol = jnp.where(head0_wide, a[:nb], a[nb:])
        l = ol[:, LANES:]
        outs.append((ol[:, :LANES] * (1.0 / l), jnp.where(head0, mx[:nb], mx[nb:]) + jnp.log2(l)))
    return outs


def _attn_kernel(qkv_ref, bias_ref, o_ref, o4_ref, l4_ref, o16_ref, l16_ref, fin_ref):
    grp = ATTN_GROUP16

    def body16(j, c):
        rs = [j * grp + a for a in range(grp)]
        bias = _attn_bias(bias_ref, 2, False)
        res = _attn_blocks([(qkv_ref[r, 0], qkv_ref[r, 1], qkv_ref[r, 2], bias) for r in rs])
        for r, (o, l) in zip(rs, res):
            o16_ref[r] = o
            l16_ref[r] = l
        return c
    for j in range(N_RES // grp):
        body16(j, 0)

    def gather4(c, rho, u0, nu):
        return jnp.concatenate([qkv_ref[rho + 4 * a, c, pl.ds(u0, nu), :] for a in range(4)], axis=0)

    def store4(rho, u0, o, l):
        for a in range(4):
            o4_ref[rho + 4 * a, pl.ds(u0, 32), :] = o[a * 32:(a + 1) * 32]
            l4_ref[rho + 4 * a, pl.ds(u0, 32), :] = l[a * 32:(a + 1) * 32]

    def body4(j, c):
        bias = _attn_bias(bias_ref, 1, True)
        todo = [(2 * j + a, 32 * n) for a in range(2) for n in range(1, 4)]
        res = _attn_blocks([(gather4(0, rho, u0, 32), gather4(1, rho, u0 - 32, 64),
                             gather4(2, rho, u0 - 32, 64), bias) for rho, u0 in todo])
        for (rho, u0), (o, l) in zip(todo, res):
            store4(rho, u0, o, l)
        return c
    for j in range(2):
        body4(j, 0)

    def gather1(ref, u0, nu, *lead):
        return jnp.concatenate([ref[(r,) + lead + (pl.ds(u0, nu), slice(None))] for r in range(N_RES)],
                               axis=0)

    def merge_and_store(u0, o1, l1):
        l4 = gather1(l4_ref, u0, 8)
        l16 = gather1(l16_ref, u0, 8)
        mx = jnp.maximum(jnp.maximum(l1, l4), l16)
        e1 = jnp.exp2(l1 - mx)
        e4 = jnp.exp2(l4 - mx)
        e16 = jnp.exp2(l16 - mx)
        inv = 1.0 / (e1 + e4 + e16)
        o = (e1 * inv) * o1 + (e4 * inv) * gather1(o4_ref, u0, 8) + (e16 * inv) * gather1(o16_ref, u0, 8)
        for r in range(N_RES):
            fin_ref[r, pl.ds(u0, 8), :] = o[r * 8:(r + 1) * 8]

    bias4 = _attn_bias(bias_ref, 1, False)
    first = [tuple(gather4(c, rho, 0, 32) for c in range(3)) + (bias4,) for rho in range(4)]
    first.append(tuple(gather1(qkv_ref, 0, 8, c) for c in range(3)) + (_attn_bias(bias_ref, 0, False),))
    res = _attn_blocks(first)
    for rho in range(4):
        store4(rho, 0, *res[rho])
    merge_and_store(0, *res[4])

    n_blocks1 = RES_LEN // 8 - 1

    def body1(j, c):
        u0s = [(1 + j * ATTN_GROUP1 + a) * 8 for a in range(ATTN_GROUP1)]
        bias = _attn_bias(bias_ref, 0, True)
        res = _attn_blocks([(gather1(qkv_ref, u0, 8, 0), gather1(qkv_ref, u0 - 8, 16, 1),
                             gather1(qkv_ref, u0 - 8, 16, 2), bias) for u0 in u0s])
        for u0, (o, l) in zip(u0s, res):
            merge_and_store(u0, o, l)
        return c
    for j in range(n_blocks1 // ATTN_GROUP1):
        body1(j, 0)

    def emit(r, c):
        o_ref[r] = fin_ref[r].astype(o_ref.dtype)
        return c
    lax.fori_loop(0, N_RES, emit, 0)


def attention(qkv):
    batch = qkv.shape[0]
    scratch = pltpu.VMEM((N_RES, RES_LEN, LANES), F32)
    return pl.pallas_call(
        _attn_kernel,
        grid=(batch, N_LANE_TILES),
        in_specs=[
            pl.BlockSpec((None, N_RES, None, 3, RES_LEN, LANES), lambda b, g: (b, 0, g, 0, 0, 0)),
            pl.BlockSpec((len(DILATIONS), HEADS_PER_TILE, ATTN_BLOCK, 3 * ATTN_BLOCK),
                         lambda b, g: (0, g, 0, 0)),
        ],
        out_specs=pl.BlockSpec((None, N_RES, None, RES_LEN, LANES), lambda b, g: (b, 0, g, 0, 0)),
        out_shape=jax.ShapeDtypeStruct((batch, N_RES, N_LANE_TILES, RES_LEN, LANES), BF16),
        scratch_shapes=[scratch] * 5,
        compiler_params=_cparams("arbitrary", "arbitrary"),
        name="dilated_attention",
    )(qkv, _attn_bias_tables())


def _split2(x):
    a = x.astype(BF16)
    b = (x - a.astype(F32)).astype(BF16)
    return a, b


def _hgrn_kernel(q_ref, f_ref, i_ref, g_ref, gamma_ref, ng_ref, tri_ref, y_ref):
    sup, ch = HGRN_SUPER, HGRN_CHUNK
    n_ch = sup // ch
    gam = gamma_ref[...]
    gmx = jnp.max(gam, axis=0, keepdims=True)
    ge = jnp.exp(gam - gmx)
    lb = ge[0:1] / jnp.sum(ge, axis=0, keepdims=True)
    ng = ng_ref[...]
    tri = tri_ref[...]
    ri = lax.broadcasted_iota(I32, (2 * sup, sup), 0) % sup
    ci = lax.broadcasted_iota(I32, (2 * sup, sup), 1)
    causal_bd2 = (ri // ch == ci // ch) & (ri >= ci)
    head0 = lax.broadcasted_iota(I32, (sup, LANES), 1) < HEAD_DIM
    row_chunk = lax.broadcasted_iota(I32, (sup, LANES), 0) // ch
    same_head = (lax.broadcasted_iota(I32, (LANES, LANES), 0) // HEAD_DIM
                 == lax.broadcasted_iota(I32, (LANES, LANES), 1) // HEAD_DIM)

    def sigmoid(x):
        return 1.0 / (1.0 + jnp.exp(-x))

    def dot_nt(a, b):
        return lax.dot_general(a, b, (((1,), (1,)), ((), ())), preferred_element_type=F32)

    def dot_tn(a, b):
        return lax.dot_general(a, b, (((0,), (0,)), ((), ())), preferred_element_type=F32)

    def expand(x):
        zero = jnp.zeros_like(x)
        return jnp.concatenate([jnp.where(row_chunk == c, x, zero) for c in range(n_ch)], axis=1)

    def body(j, state):
        grp = range(HGRN_GROUP)
        rows = [pl.ds((j * HGRN_GROUP + a) * sup, sup) for a in grp]
        fr = [f_ref[r, :].astype(F32) for r in rows]
        t = [jnp.exp(-jnp.abs(x)) for x in fr]
        rcp = [1.0 / (1.0 + x) for x in t]
        sig = [jnp.where(f >= 0, r, x * r) for f, r, x in zip(fr, rcp, t)]
        nsig = [jnp.where(f >= 0, x * r, r) for f, r, x in zip(fr, rcp, t)]
        logf = [jnp.log(lb + (1.0 - lb) * x) for x in sig]
        key = [(1.0 - lb) * x for x in nsig]
        parts = [_split2(x) for x in logf]
        b = [sum(jnp.dot(tri, p, preferred_element_type=F32) for p in ps) for ps in parts]
        totals = [[x[(c + 1) * ch - 1:(c + 1) * ch, :] for c in range(n_ch)] for x in b]
        b_last = [jnp.concatenate([jnp.broadcast_to(t_, (ch, LANES)) for t_ in ts], axis=0) for ts in totals]
        pad = jnp.zeros((LANES - n_ch, LANES), F32)
        decay_t = [jnp.exp(jnp.transpose(jnp.concatenate(ts + [pad], axis=0))) for ts in totals]
        qv = [q_ref[r, :].astype(F32) for r in rows]
        q_in = [(x * sigmoid(x) * jnp.exp(bb)).astype(BF16) for x, bb in zip(qv, b)]
        k_in = [(k * jnp.exp(-bb)).astype(BF16) for k, bb in zip(key, b)]
        k_end = [(k * jnp.exp(bl - bb)).astype(BF16) for k, bl, bb in zip(key, b_last, b)]
        vv = [i_ref[r, :] for r in rows]
        q2 = [jnp.concatenate([jnp.where(head0, x, jnp.zeros_like(x)), jnp.where(head0, jnp.zeros_like(x), x)],
                              axis=0) for x in q_in]
        att = [jnp.where(causal_bd2, dot_nt(x, k), 0.0).astype(BF16) for x, k in zip(q2, k_in)]
        intra = [jnp.dot(a_, v, preferred_element_type=F32) for a_, v in zip(att, vv)]
        o_intra = [jnp.where(head0, x[:sup], x[sup:]) for x in intra]
        ds_all = [dot_tn(k, expand(v)) for k, v in zip(k_end, vv)]
        s_prev = []
        for a in grp:
            for c in range(n_ch):
                s_prev.append(state.astype(BF16))
                ds = jnp.where(same_head, ds_all[a][:, c * LANES:(c + 1) * LANES], 0.0)
                state = decay_t[a][:, c:c + 1] * state + ds
        o_inter = [jnp.concatenate(
            [jnp.dot(q_in[a][c * ch:(c + 1) * ch], s_prev[a * n_ch + c], preferred_element_type=F32)
             for c in range(n_ch)], axis=0) for a in grp]
        o = [x + y for x, y in zip(o_intra, o_inter)]
        sq = [x * x for x in o]
        ss0 = [jnp.sum(jnp.where(head0, x, 0.0), axis=-1, keepdims=True) for x in sq]
        ss1 = [jnp.sum(jnp.where(head0, 0.0, x), axis=-1, keepdims=True) for x in sq]
        ms = [jnp.where(head0, x, y) * (1.0 / HEAD_DIM) for x, y in zip(ss0, ss1)]
        gv = [g_ref[r, :].astype(F32) for r in rows]
        for r, x, m_, g_ in zip(rows, o, ms, gv):
            y_ref[r, :] = (x * lax.rsqrt(m_ + EPS) * ng * (g_ * sigmoid(g_))).astype(y_ref.dtype)
        return state

    state = jnp.zeros((LANES, LANES), F32)
    for j in range(SEQ // (sup * HGRN_GROUP)):
        state = body(j, state)


def hgrn(hg3, gamma, norm_g):
    batch = hg3.shape[0]
    r = jnp.arange(HGRN_SUPER)
    same = (r[:, None] // HGRN_CHUNK) == (r[None, :] // HGRN_CHUNK)
    tri = (same & (r[:, None] >= r[None, :])).astype(BF16)
    col = lambda which: (lambda b, g: (b, 0, which * N_LANE_TILES + g))
    const2 = lambda b, g: (0, 0)
    return pl.pallas_call(
        _hgrn_kernel,
        grid=(batch, N_LANE_TILES),
        in_specs=[
            pl.BlockSpec((None, SEQ, LANES), col(0)),
            pl.BlockSpec((None, SEQ, LANES), col(1)),
            pl.BlockSpec((None, SEQ, LANES), col(2)),
            pl.BlockSpec((None, SEQ, LANES), col(3)),
            pl.BlockSpec((2, LANES), lambda b, g: (0, g)),
            pl.BlockSpec((1, LANES), lambda b, g: (0, g)),
            pl.BlockSpec((HGRN_SUPER, HGRN_SUPER), const2),
        ],
        out_specs=pl.BlockSpec((None, SEQ, LANES), lambda b, g: (b, 0, g)),
        out_shape=jax.ShapeDtypeStruct((batch, SEQ, WIDTH), BF16),
        compiler_params=_cparams("arbitrary", "arbitrary"),
        name="hgrn2",
    )(hg3, hg3, hg3, hg3, gamma.astype(F32), norm_g.reshape(1, WIDTH).astype(F32), tri)


def _post_mix_kernel(oa_ref, yh_ref, x_ref, ag_ref, wo_ref, g2_ref, wrh_ref, wrl_ref, br_ref, perm_t_ref,
                     tri_ref, lower_ref,
                     x1_ref, h2_ref, rec_ref, tab_ref, carry_ref, lg_sc):
    i = pl.program_id(0)

    @pl.when(i == 0)
    def _():
        lg_sc[...] = jnp.zeros_like(lg_sc)
    _route_tile(lg_sc[...], (i > 0).astype(F32), tri_ref, lower_ref, rec_ref, tab_ref, carry_ref)

    oa = jnp.concatenate(
        [jnp.concatenate([oa_ref[r, g] for g in range(N_LANE_TILES)], axis=1)
         for r in range(N_RES)], axis=0).astype(F32)
    ms = jnp.mean(oa * oa, axis=-1, keepdims=True)
    ya = (oa * lax.rsqrt(ms + EPS) * ag_ref[...]).astype(BF16)
    ya = jnp.dot(perm_t_ref[...], ya, preferred_element_type=F32).astype(BF16)
    mix = (jnp.dot(ya, wo_ref[:WIDTH, :], preferred_element_type=F32)
           + jnp.dot(yh_ref[...], wo_ref[WIDTH:, :], preferred_element_type=F32))
    x1 = x_ref[...] + mix
    x1_ref[...] = x1
    ms2 = jnp.mean(x1 * x1, axis=-1, keepdims=True)
    h2 = x1 * lax.rsqrt(ms2 + EPS) * g2_ref[...]
    hi = h2.astype(BF16)
    h2_ref[...] = hi
    lo = (h2 - hi.astype(F32)).astype(BF16)
    nt = (((1,), (1,)), ((), ()))
    wrh = wrh_ref[...]
    lg_sc[...] = (lax.dot_general(wrh, hi, nt, preferred_element_type=F32)
                  + lax.dot_general(wrh, lo, nt, preferred_element_type=F32)
                  + lax.dot_general(wrl_ref[...], hi, nt, preferred_element_type=F32)
                  + br_ref[...])


def post_mix(oa, yh, x2, attn_g, w_out_bf16, g2, wr_hi, wr_lo, br):
    n = x2.shape[0]
    tiles_per_b = SEQ // TILE_TOKENS
    assert ROUTE_TM == TILE_TOKENS
    n_tiles = n // TILE_TOKENS
    cur = lambda i: jnp.minimum(i, n_tiles - 1)
    prev = lambda i: jnp.maximum(i - 1, 0)
    row = lambda w: pl.BlockSpec((TILE_TOKENS, w), lambda i: (cur(i), 0))
    const = lambda r, c: pl.BlockSpec((r, c), lambda i: (0, 0))
    return pl.pallas_call(
        _post_mix_kernel,
        grid=(n_tiles + 1,),
        in_specs=[pl.BlockSpec((None, N_RES, N_LANE_TILES, U_PER_TILE, LANES),
                               lambda i: (cur(i) // tiles_per_b, 0, 0, cur(i) % tiles_per_b, 0)),
                  row(WIDTH), row(D_MODEL),
                  const(1, WIDTH), const(2 * WIDTH, D_MODEL), const(1, D_MODEL),
                  const(LANES, D_MODEL), const(LANES, D_MODEL), const(LANES, 1),
                  const(TILE_TOKENS, TILE_TOKENS), const(ROUTE_TM, ROUTE_TM), const(N_EXPERTS, N_EXPERTS)],
        out_specs=[row(D_MODEL), row(D_MODEL),
                   pl.BlockSpec((8, ROUTE_TM), lambda i: (0, prev(i))),
                   pl.BlockSpec((1, 8, LANES), lambda i: (prev(i), 0, 0))],
        out_shape=[jax.ShapeDtypeStruct((n, D_MODEL), F32),
                   jax.ShapeDtypeStruct((n, D_MODEL), BF16),
                   jax.ShapeDtypeStruct((8, n), F32),
                   jax.ShapeDtypeStruct((n_tiles, 8, LANES), F32)],
        scratch_shapes=[pltpu.VMEM((N_EXPERTS, LANES), F32), pltpu.VMEM((LANES, TILE_TOKENS), F32)],
        compiler_params=_cparams("arbitrary"),
        name="post_mix",
    )(oa, yh, x2, attn_g.reshape(1, WIDTH), w_out_bf16, g2.reshape(1, D_MODEL), wr_hi, wr_lo, br,
      _tile_permutation().T, *_route_constants())


ROUTE_TM = 512
EXPERT_ROW0 = 32
R_E1, R_E2, R_SLOT1, R_SLOT2, R_G1, R_G2 = 0, 1, 2, 3, 4, 5
T_CARRY, T_ROWS, T_OFF = 0, 1, 2
SEG_END = 0
TOTAL_LANE = N_EXPERTS
RUN_ALIGN = 8
LOCAL_ROWS = 2 * ROUTE_TM + N_EXPERTS * RUN_ALIGN


def _route_tile(lg, live, tri_ref, lower_ref, rec_ref, tab_ref, carry_ref):
    i = pl.program_id(0)

    @pl.when(i == 0)
    def _():
        carry_ref[...] = jnp.zeros_like(carry_ref)

    tm = lg.shape[1]
    sub8 = lax.broadcasted_iota(I32, (8, tm), 0).astype(F32)
    big = 8.0
    gmask = sub8 < N_GROUPS
    gl = jnp.where(gmask, lg[0:8], NEG)
    gmax = jnp.max(gl, axis=0, keepdims=True)
    gsel = jnp.min(jnp.where(gmask & (gl == gmax), sub8, big), axis=0, keepdims=True)
    gsum = jnp.sum(jnp.where(gmask, jnp.exp(gl - gmax), 0.0), axis=0, keepdims=True)
    w_g = 1.0 / gsum
    el = jnp.zeros((EXPERTS_PER_GROUP, tm), F32)
    for g in range(N_GROUPS):
        lo = EXPERT_ROW0 + g * EXPERTS_PER_GROUP
        el = jnp.where(gsel == g, lg[lo:lo + EXPERTS_PER_GROUP], el)
    v1 = jnp.max(el, axis=0, keepdims=True)
    i1 = jnp.min(jnp.where(el == v1, sub8, big), axis=0, keepdims=True)
    el2 = jnp.where(sub8 == i1, NEG, el)
    v2 = jnp.max(el2, axis=0, keepdims=True)
    i2 = jnp.min(jnp.where((el2 == v2) & (sub8 != i1), sub8, big), axis=0, keepdims=True)
    ex = jnp.exp(v2 - v1)
    den = 1.0 / (1.0 + ex)
    g1 = w_g * den
    g2 = w_g * ex * den
    e1 = gsel * EXPERTS_PER_GROUP + i1
    e2 = gsel * EXPERTS_PER_GROUP + i2
    sub_e = lax.broadcasted_iota(I32, (N_EXPERTS, tm), 0).astype(F32)
    oh1 = sub_e == e1
    oh2 = sub_e == e2
    onehot = (oh1 | oh2).astype(BF16)
    before = jnp.dot(onehot, tri_ref[...], preferred_element_type=F32)
    count = jnp.sum(onehot.astype(F32), axis=1, keepdims=True)
    units = jnp.floor((count + (RUN_ALIGN - 1)) * (1.0 / RUN_ALIGN)) * live
    units = jnp.broadcast_to(units, (N_EXPERTS, LANES))
    rows = units * RUN_ALIGN
    off = RUN_ALIGN * jnp.dot(lower_ref[...], units.astype(BF16), preferred_element_type=F32)
    place = off[:, 0:1] + before
    slot1 = jnp.sum(jnp.where(oh1, place, 0.0), axis=0, keepdims=True)
    slot2 = jnp.sum(jnp.where(oh2, place, 0.0), axis=0, keepdims=True)
    total = jnp.sum(rows, axis=0, keepdims=True)
    eye = (lax.broadcasted_iota(I32, (N_EXPERTS, LANES), 0) == lax.broadcasted_iota(I32, (N_EXPERTS, LANES), 1))
    lane1 = lax.broadcasted_iota(I32, (1, LANES), 1)
    as_row = lambda col: jnp.where(lane1 == TOTAL_LANE, total,
                                   jnp.sum(jnp.where(eye, col, 0.0), axis=0, keepdims=True))
    sub = lax.broadcasted_iota(I32, (8, LANES), 0)
    tab_ref[0] = jnp.where(sub == T_CARRY, as_row(carry_ref[...]),
                           jnp.where(sub == T_ROWS, as_row(rows), jnp.where(sub == T_OFF, as_row(off), 0.0)))
    carry_ref[...] += rows
    rec = jnp.zeros((8, tm), F32)
    for slot, val in ((R_E1, e1), (R_E2, e2), (R_SLOT1, slot1),
                      (R_SLOT2, slot2), (R_G1, g1), (R_G2, g2)):
        rec = jnp.where(sub8 == slot, val, rec)
    rec_ref[...] = rec


def _route_constants():
    r = jnp.arange(ROUTE_TM)
    tri = (r[:, None] < r[None, :]).astype(BF16)
    e = jnp.arange(N_EXPERTS)
    lower = (e[:, None] > e[None, :]).astype(BF16)
    return tri, lower


P_START, P_ROWS, P_OFF = 0, 1, 2


def _plan_kernel(carry_ref, rows_ref, off_ref, upper_ref, plan_ref, seg_ref):
    n_tiles = carry_ref.shape[0]
    lane = lax.broadcasted_iota(I32, (1, LANES), 1)
    last = pl.ds(n_tiles - 1, 1)
    totals = jnp.where(lane < N_EXPERTS, carry_ref[last, :] + rows_ref[last, :], 0.0)
    blocks = jnp.floor((totals + (FFN_BLOCK - 1)) * (1.0 / FFN_BLOCK))
    starts = FFN_BLOCK * jnp.dot(jnp.broadcast_to(blocks, (8, LANES)).astype(BF16), upper_ref[...],
                                 preferred_element_type=F32)[0:1]
    seg_ref[...] = jnp.broadcast_to(starts + blocks * FFN_BLOCK, seg_ref.shape)
    plan_ref[P_START] = (carry_ref[...] + starts).astype(I32)
    plan_ref[P_ROWS] = rows_ref[...].astype(I32)
    plan_ref[P_OFF] = off_ref[...].astype(I32)


def run_plan(tab):
    n_tiles = tab.shape[0]
    ln = jnp.arange(LANES)
    upper = (ln[:, None] < ln[None, :]).astype(BF16)
    return pl.pallas_call(
        _plan_kernel,
        out_shape=[jax.ShapeDtypeStruct((3, n_tiles, LANES), I32),
                   jax.ShapeDtypeStruct((8, LANES), F32)],
        compiler_params=pltpu.CompilerParams(vmem_limit_bytes=VMEM_LIMIT),
        name="run_plan",
    )(tab[:, T_CARRY], tab[:, T_ROWS], tab[:, T_OFF], upper)


RUN_PIECE = 64
SMALL_PIECES = (32, 16, 8)
TOTAL_PIECES = (1024, 512, 256, 128, 64, 32, 16, 8)
SORT_CHUNK = 256


def _for_each_run_piece(plan_ref, tile, n_tiles, fn):
    plane = n_tiles * LANES

    def per_expert(e, c):
        idx = tile * LANES + e
        start = plan_ref[P_START * plane + idx]
        rows = plan_ref[P_ROWS * plane + idx]
        off = plan_ref[P_OFF * plane + idx]

        def big(j, c2):
            fn(pl.multiple_of(off + j * RUN_PIECE, RUN_ALIGN), pl.multiple_of(start + j * RUN_PIECE, RUN_ALIGN),
               RUN_PIECE)
            return c2
        lax.fori_loop(0, lax.shift_right_logical(rows, 6), big, 0)
        for size in SMALL_PIECES:
            done = jnp.bitwise_and(rows, -2 * size)

            @pl.when(jnp.bitwise_and(rows, size) != 0)
            def _():
                fn(pl.multiple_of(off + done, RUN_ALIGN), pl.multiple_of(start + done, RUN_ALIGN), size)
        return c
    lax.fori_loop(0, N_EXPERTS, per_expert, 0)


def _for_each_total_piece(plan_ref, tile, n_tiles, fn):
    total = plan_ref[P_OFF * n_tiles * LANES + tile * LANES + TOTAL_LANE]
    for size in TOTAL_PIECES:
        @pl.when(jnp.bitwise_and(total, size) != 0)
        def _():
            fn(size)


def _dispatch_kernel(plan_ref, ends_ref, nv_ref, h2_ref, rec_ref, xs_ref, zero_buf, stage, sem, zsem):
    i = pl.program_id(0)
    n_steps = pl.num_programs(0)
    n_blocks = xs_ref.shape[0] // FFN_BLOCK

    @pl.when(i == 0)
    def _():
        zero_buf[...] = jnp.zeros_like(zero_buf)

        def zero_copy(row0):
            return pltpu.make_async_copy(zero_buf, xs_ref.at[pl.ds(row0, FFN_BLOCK)], zsem)

        def seg_end(e):
            return ends_ref[e], ends_ref[e] > jnp.where(e > 0, ends_ref[jnp.maximum(e - 1, 0)], 0)

        def start_e(e, c):
            end, nonempty = seg_end(e)

            @pl.when(nonempty)
            def _():
                zero_copy(pl.multiple_of(end - FFN_BLOCK, FFN_BLOCK)).start()
            return c
        lax.fori_loop(0, N_EXPERTS, start_e, 0)

        def start_b(blk, c):
            zero_copy(pl.multiple_of(blk * FFN_BLOCK, FFN_BLOCK)).start()
            return c
        lax.fori_loop(nv_ref[0], n_blocks, start_b, 0)

        def wait_e(e, c):
            _, nonempty = seg_end(e)

            @pl.when(nonempty)
            def _():
                zero_copy(0).wait()
            return c
        lax.fori_loop(0, N_EXPERTS, wait_e, 0)

        def wait_b(blk, c):
            zero_copy(0).wait()
            return c
        lax.fori_loop(nv_ref[0], n_blocks, wait_b, 0)

    slot = i % 2
    s1 = rec_ref[R_SLOT1:R_SLOT1 + 1, :]
    s2 = rec_ref[R_SLOT2:R_SLOT2 + 1, :]
    half = PACKED_W
    for c in range(LOCAL_ROWS // SORT_CHUNK):
        rowf = (lax.broadcasted_iota(I32, (SORT_CHUNK, ROUTE_TM), 0) + c * SORT_CHUNK).astype(F32)
        sel = ((rowf == s1) | (rowf == s2)).astype(BF16)
        pick = lambda cols: jnp.dot(sel, h2_ref[:, cols], preferred_element_type=F32)
        stage[slot, c * SORT_CHUNK:(c + 1) * SORT_CHUNK, :] = _pack_pair(pick(slice(0, half)),
                                                                         pick(slice(half, 2 * half)))

    def start_piece(lrow, grow, size):
        pltpu.make_async_copy(stage.at[slot, pl.ds(lrow, size)], xs_ref.at[pl.ds(grow, size)],
                              sem.at[slot]).start()
    _for_each_run_piece(plan_ref, i, n_steps, start_piece)

    def wait_pieces(tile, s):
        def wait_piece(size):
            pltpu.make_async_copy(stage.at[s, pl.ds(0, size)], xs_ref.at[pl.ds(0, size)], sem.at[s]).wait()
        _for_each_total_piece(plan_ref, tile, n_steps, wait_piece)

    @pl.when(i > 0)
    def _():
        wait_pieces(i - 1, 1 - slot)

    @pl.when(i + 1 == n_steps)
    def _():
        wait_pieces(i, slot)


def dispatch(h2, rec, plan, seg_ends, n_valid, n_rows):
    n = h2.shape[0]
    return pl.pallas_call(
        _dispatch_kernel,
        grid_spec=pltpu.PrefetchScalarGridSpec(
            num_scalar_prefetch=3,
            grid=(n // ROUTE_TM,),
            in_specs=[pl.BlockSpec((ROUTE_TM, D_MODEL), lambda i, *_: (i, 0)),
                      pl.BlockSpec((8, ROUTE_TM), lambda i, *_: (0, i))],
            out_specs=pl.BlockSpec(memory_space=pl.ANY),
            scratch_shapes=[pltpu.VMEM((FFN_BLOCK, PACKED_W), U32),
                            pltpu.VMEM((2, LOCAL_ROWS, PACKED_W), U32),
                            pltpu.SemaphoreType.DMA((2,)),
                            pltpu.SemaphoreType.DMA(())],
        ),
        out_shape=jax.ShapeDtypeStruct((n_rows, PACKED_W), U32),
        compiler_params=_cparams("arbitrary"),
        name="dispatch",
    )(plan, seg_ends, n_valid, h2, rec)


ROW_BLOCK_DMA_PRIORITY = 1


def _ffn_kernel(ends_ref, xs_ref, wg_ref, wu_ref, wd_ref, ys_ref,
                wg_sc, wu_sc, wd_sc, xbuf, ybuf, in_sem, out_sem, pend):
    e = pl.program_id(0)
    last = pl.num_programs(0) - 1
    blk = FFN_BLOCK
    base = jnp.where(e > 0, ends_ref[jnp.maximum(e - 1, 0)], 0)
    n_blk = (ends_ref[e] - base) // blk

    def load(row, s):
        return pltpu.make_async_copy(xs_ref.at[pl.ds(row, blk)], xbuf.at[s], in_sem.at[s])

    def store(row, s):
        return pltpu.make_async_copy(ybuf.at[s], ys_ref.at[pl.ds(row, blk)], out_sem.at[s])

    @pl.when(e == 0)
    def _():
        pend[0] = 0
        pend[1] = 0

        @pl.when(n_blk > 0)
        def _():
            load(pl.multiple_of(base, blk), 0).start(priority=ROW_BLOCK_DMA_PRIORITY)

    wg_sc[...] = wg_ref[0].astype(BF16)
    wu_sc[...] = wu_ref[0].astype(BF16)
    wd_sc[...] = wd_ref[0].astype(BF16)

    def body(j, c):
        s = j % 2
        load(0, s).wait()

        @pl.when(j + 1 < n_blk)
        def _():
            load(pl.multiple_of(base + (j + 1) * blk, blk), 1 - s).start(priority=ROW_BLOCK_DMA_PRIORITY)

        @pl.when(pend[s] == 1)
        def _():
            store(0, s).wait()
        xa, xb = (t.astype(BF16) for t in _unpack_halves(xbuf[s]))
        half = PACKED_W
        gate = (jnp.dot(xa, wg_sc[:half, :], preferred_element_type=F32)
                + jnp.dot(xb, wg_sc[half:, :], preferred_element_type=F32))
        up = (jnp.dot(xa, wu_sc[:half, :], preferred_element_type=F32)
              + jnp.dot(xb, wu_sc[half:, :], preferred_element_type=F32))
        hid = (gate * (1.0 / (1.0 + jnp.exp(-gate))) * up).astype(BF16)
        ybuf[s] = _pack_halves(jnp.dot(hid, wd_sc[...], preferred_element_type=F32))
        store(pl.multiple_of(base + j * blk, blk), s).start(priority=ROW_BLOCK_DMA_PRIORITY)
        pend[s] = 1
        return c
    lax.fori_loop(0, n_blk, body, 0)

    @pl.when(e < last)
    def _():
        nxt_base = ends_ref[e]

        @pl.when(ends_ref[jnp.minimum(e + 1, last)] > nxt_base)
        def _():
            load(pl.multiple_of(nxt_base, blk), 0).start(priority=ROW_BLOCK_DMA_PRIORITY)

    @pl.when(e == last)
    def _():
        for s in range(2):
            @pl.when(pend[s] == 1)
            def _():
                store(0, s).wait()
        ybuf[0] = jnp.zeros((blk, PACKED_W), U32)

        def zero_block(b, c):
            cp = store(pl.multiple_of(b * blk, blk), 0)
            cp.start()
            cp.wait()
            return c
        lax.fori_loop(ends_ref[last] // blk, ys_ref.shape[0] // blk, zero_block, 0)


def expert_ffn(xs, seg_ends, w_gate, w_up, w_down):
    n_rows = xs.shape[0]
    wspec = lambda r, c: pl.BlockSpec((1, r, c), lambda e, ends: (e, 0, 0))
    return pl.pallas_call(
        _ffn_kernel,
        grid_spec=pltpu.PrefetchScalarGridSpec(
            num_scalar_prefetch=1,
            grid=(N_EXPERTS,),
            in_specs=[pl.BlockSpec(memory_space=pl.ANY),
                      wspec(D_MODEL, D_EXPERT), wspec(D_MODEL, D_EXPERT), wspec(D_EXPERT, D_MODEL)],
            out_specs=pl.BlockSpec(memory_space=pl.ANY),
            scratch_shapes=[pltpu.VMEM((D_MODEL, D_EXPERT), BF16),
                            pltpu.VMEM((D_MODEL, D_EXPERT), BF16),
                            pltpu.VMEM((D_EXPERT, D_MODEL), BF16),
                            pltpu.VMEM((2, FFN_BLOCK, PACKED_W), U32),
                            pltpu.VMEM((2, FFN_BLOCK, PACKED_W), U32),
                            pltpu.SemaphoreType.DMA((2,)),
                            pltpu.SemaphoreType.DMA((2,)),
                            pltpu.SMEM((2,), I32)],
        ),
        out_shape=jax.ShapeDtypeStruct((n_rows, PACKED_W), U32),
        compiler_params=_cparams("arbitrary"),
        name="expert_ffn",
    )(seg_ends, xs, w_gate, w_up, w_down)


def _combine_kernel(plan_ref, x1_ref, rec_ref, gf_ref, ys_ref, out_ref, buf, sem):
    i = pl.program_id(0)
    n_steps = pl.num_programs(0)
    slot = i % 2

    def fetch_tile(tile, s):
        def start_piece(lrow, grow, size):
            pltpu.make_async_copy(ys_ref.at[pl.ds(grow, size)], buf.at[s, pl.ds(lrow, size)], sem.at[s]).start()
        _for_each_run_piece(plan_ref, tile, n_steps, start_piece)

    @pl.when(i == 0)
    def _():
        buf[...] = jnp.zeros_like(buf)
        fetch_tile(0, 0)

    @pl.when(i + 1 < n_steps)
    def _():
        fetch_tile(i + 1, 1 - slot)

    def wait_piece(size):
        pltpu.make_async_copy(ys_ref.at[pl.ds(0, size)], buf.at[slot, pl.ds(0, size)], sem.at[slot]).wait()
    _for_each_total_piece(plan_ref, i, n_steps, wait_piece)

    rowf = lax.broadcasted_iota(I32, (LOCAL_ROWS, ROUTE_TM), 0).astype(F32)
    sel = (jnp.where(rowf == rec_ref[R_SLOT1:R_SLOT1 + 1, :], rec_ref[R_G1:R_G1 + 1, :], 0.0)
           + jnp.where(rowf == rec_ref[R_SLOT2:R_SLOT2 + 1, :], rec_ref[R_G2:R_G2 + 1, :], 0.0)).astype(BF16)
    ya, yb = (t.astype(BF16) for t in _unpack_halves(buf[slot]))
    half = PACKED_W
    tn = (((0,), (0,)), ((), ()))
    xa = x1_ref[:, :half] + lax.dot_general(sel, ya, tn, preferred_element_type=F32)
    xb = x1_ref[:, half:] + lax.dot_general(sel, yb, tn, preferred_element_type=F32)
    ms = (jnp.sum(xa * xa, axis=-1, keepdims=True) + jnp.sum(xb * xb, axis=-1, keepdims=True)) * (1.0 / D_MODEL)
    scale = lax.rsqrt(ms + EPS)
    out_ref[:, :half] = xa * scale * gf_ref[:, :half]
    out_ref[:, half:] = xb * scale * gf_ref[:, half:]


def combine(x1, rec, norm_f_g, ys, plan):
    n = x1.shape[0]
    tm = ROUTE_TM
    return pl.pallas_call(
        _combine_kernel,
        grid_spec=pltpu.PrefetchScalarGridSpec(
            num_scalar_prefetch=1,
            grid=(n // tm,),
            in_specs=[pl.BlockSpec((tm, D_MODEL), lambda i, p: (i, 0)),
                      pl.BlockSpec((8, tm), lambda i, p: (0, i)),
                      pl.BlockSpec((1, D_MODEL), lambda i, p: (0, 0)),
                      pl.BlockSpec(memory_space=pl.ANY)],
            out_specs=pl.BlockSpec((tm, D_MODEL), lambda i, p: (i, 0)),
            scratch_shapes=[pltpu.VMEM((2, LOCAL_ROWS, PACKED_W), U32),
                            pltpu.SemaphoreType.DMA((2,))],
        ),
        out_shape=jax.ShapeDtypeStruct((n, D_MODEL), F32),
        compiler_params=_cparams("arbitrary"),
        name="combine",
    )(plan, x1, rec, norm_f_g.reshape(1, D_MODEL), ys)


def _router_weights(w_group, b_group, w_router, b_router):
    w = jnp.zeros((LANES, D_MODEL), F32)
    w = w.at[:N_GROUPS].set(w_group.T).at[EXPERT_ROW0:EXPERT_ROW0 + N_EXPERTS].set(w_router.T)
    b = jnp.zeros((LANES, 1), F32)
    b = b.at[:N_GROUPS, 0].set(b_group).at[EXPERT_ROW0:EXPERT_ROW0 + N_EXPERTS, 0].set(b_router)
    hi = w.astype(BF16)
    lo = (w - hi.astype(F32)).astype(BF16)
    return hi, lo, b


def _sorted_rows_bound(n_tokens):
    worst = (2 * n_tokens + (n_tokens // ROUTE_TM) * N_EXPERTS * (RUN_ALIGN - 1)
             + N_EXPERTS * (FFN_BLOCK - 1))
    return -(-worst // FFN_BLOCK) * FFN_BLOCK


def _block_plan(seg):
    seg_ends = seg[SEG_END, :N_EXPERTS].astype(I32)
    return seg_ends, seg_ends[-1:] // FFN_BLOCK


def kernel(x, norm1_g, w_in, attn_norm_g, hgrn_gamma, hgrn_norm_g, w_out, norm2_g, w_group, b_group,
           w_router, b_router, w_gate, w_up, w_down, norm_f_g):
    batch, seq, d = x.shape
    assert seq == SEQ and d == D_MODEL and norm1_g.shape[0] == 1
    n = batch * seq
    x2 = x.reshape(n, d)
    qkv, hg = in_proj(x2, norm1_g[0], w_in[0].astype(BF16), batch)
    oa = attention(qkv)
    yh = hgrn(hg.reshape(batch, seq, HG_W), hgrn_gamma, hgrn_norm_g[0]).reshape(n, WIDTH)
    wr_hi, wr_lo, br = _router_weights(w_group[0], b_group[0], w_router[0], b_router[0])
    x1, h2, rec, tab = post_mix(oa, yh, x2, attn_norm_g[0], w_out[0].astype(BF16), norm2_g[0],
                                wr_hi, wr_lo, br)
    plan, seg = run_plan(tab)
    plan = plan.reshape(-1)
    n_rows = _sorted_rows_bound(n)
    seg_ends, n_valid = _block_plan(seg)
    xs = dispatch(h2, rec, plan, seg_ends, n_valid, n_rows)
    ys = expert_ffn(xs, seg_ends, w_gate[0], w_up[0], w_down[0])
    return combine(x1, rec, norm_f_g, ys, plan).reshape(batch, seq, d)
```

```python
import jax
import jax.numpy as jnp
from jax import lax
from jax.experimental import pallas as pl
from jax.experimental.pallas import tpu as pltpu

F32 = jnp.float32
BF16 = jnp.bfloat16
I32 = jnp.int32
U32 = jnp.uint32

D_MODEL = 1024
HEAD_DIM = 64
N_HEADS = 8
WIDTH = N_HEADS * HEAD_DIM
QKV_W = 3 * WIDTH
HG_W = 4 * WIDTH
SEQ = 2048
ATTN_BLOCK = 128
DILATIONS = (1, 4, 16)
N_RES = 16
RES_LEN = SEQ // N_RES
TILE_TOKENS = 512
U_PER_TILE = TILE_TOKENS // N_RES
HGRN_CHUNK = 32
HGRN_SUPER = 256
HGRN_GROUP = 8
N_GROUPS = 4
EXPERTS_PER_GROUP = 8
N_EXPERTS = 32
D_EXPERT = 512
FFN_BLOCK = 512
PACKED_W = D_MODEL // 2
EPS = 1e-6
NEG = -1e30
LOG2E = 1.4426950408889634
LANES = 128
VMEM_LIMIT = 56 * 1024 * 1024


def _cparams(*sem):
    return pltpu.CompilerParams(dimension_semantics=sem, vmem_limit_bytes=VMEM_LIMIT)


def _pack_pair(a, b):
    hi = pltpu.bitcast(a.astype(BF16).astype(F32), U32)
    lo = pltpu.bitcast(b.astype(BF16).astype(F32), U32)
    return hi | (lo >> 16)


def _pack_halves(x):
    w = x.shape[1] // 2
    return _pack_pair(x[:, :w], x[:, w:])


def _unpack_halves(p):
    return pltpu.bitcast(p & jnp.uint32(0xFFFF0000), F32), pltpu.bitcast(p << 16, F32)


N_LANE_TILES = WIDTH // LANES


def _tile_permutation():
    i = jnp.arange(TILE_TOKENS)
    src = N_RES * (i % U_PER_TILE) + i // U_PER_TILE
    return (src[:, None] == jnp.arange(TILE_TOKENS)[None, :]).astype(BF16)


def _in_proj_kernel(x_ref, g_ref, w_ref, perm_ref, qkv_ref, hg_ref):
    x = x_ref[...]
    ms = jnp.mean(x * x, axis=-1, keepdims=True)
    h = (x * lax.rsqrt(ms + EPS) * g_ref[...]).astype(BF16)
    hp = jnp.dot(perm_ref[...], h, preferred_element_type=F32).astype(BF16)
    for c in range(QKV_W // WIDTH):
        res = jnp.dot(hp, w_ref[:, c * WIDTH:(c + 1) * WIDTH], preferred_element_type=F32)
        for r in range(N_RES):
            for g in range(N_LANE_TILES):
                qkv_ref[r, g, c] = res[r * U_PER_TILE:(r + 1) * U_PER_TILE, g * LANES:(g + 1) * LANES]
    for j in range(HG_W // WIDTH):
        wsl = slice(QKV_W + j * WIDTH, QKV_W + (j + 1) * WIDTH)
        hg_ref[:, j * WIDTH:(j + 1) * WIDTH] = jnp.dot(h, w_ref[:, wsl], preferred_element_type=F32).astype(BF16)


def in_proj(x2, g, w_bf16, batch):
    n = x2.shape[0]
    tiles_per_b = SEQ // TILE_TOKENS
    return pl.pallas_call(
        _in_proj_kernel,
        grid=(n // TILE_TOKENS,),
        in_specs=[
            pl.BlockSpec((TILE_TOKENS, D_MODEL), lambda i: (i, 0)),
            pl.BlockSpec((1, D_MODEL), lambda i: (0, 0)),
            pl.BlockSpec((D_MODEL, QKV_W + HG_W), lambda i: (0, 0)),
            pl.BlockSpec((TILE_TOKENS, TILE_TOKENS), lambda i: (0, 0)),
        ],
        out_specs=[
            pl.BlockSpec((None, N_RES, N_LANE_TILES, 3, U_PER_TILE, LANES),
                         lambda i: (i // tiles_per_b, 0, 0, 0, i % tiles_per_b, 0)),
            pl.BlockSpec((TILE_TOKENS, HG_W), lambda i: (i, 0)),
        ],
        out_shape=[
            jax.ShapeDtypeStruct((batch, N_RES, N_LANE_TILES, 3, RES_LEN, LANES), F32),
            jax.ShapeDtypeStruct((n, HG_W), BF16),
        ],
        compiler_params=_cparams("arbitrary"),
        name="in_proj",
    )(x2, g.reshape(1, D_MODEL), w_bf16, _tile_permutation())


ATTN_GROUP16 = 16
ATTN_GROUP1 = 5
HEADS_PER_TILE = LANES // HEAD_DIM


def _attn_bias_tables():
    slopes = 2.0 ** (-8.0 * jnp.arange(1, N_HEADS + 1, dtype=F32) / N_HEADS)

    def table(qpos, kpos, dil):
        sd = qpos[:, None] - kpos[None, :]
        b = -slopes[:, None, None] * (sd * dil).astype(F32)[None] * LOG2E
        return jnp.where(((sd >= 0) & (sd <= ATTN_BLOCK))[None], b, NEG)

    q = jnp.arange(ATTN_BLOCK)
    k = jnp.arange(2 * ATTN_BLOCK)
    q1 = 16 * (q % 8) + q // 8 + ATTN_BLOCK
    k1 = 16 * (k % 16) + k // 16
    q4 = 4 * (q % 32) + q // 32 + ATTN_BLOCK
    k4 = 4 * (k % 64) + k // 64
    q16 = q + ATTN_BLOCK
    none = jnp.full((N_HEADS, ATTN_BLOCK, 2 * ATTN_BLOCK), NEG, F32)
    tabs = [jnp.concatenate([table(q1, k1, 1), table(q1, q1, 1)], axis=-1),
            jnp.concatenate([table(q4, k4, 4), table(q4, q4, 4)], axis=-1),
            jnp.concatenate([none, table(q16, q16, 16)], axis=-1)]
    return jnp.stack(tabs, axis=0)


def _attn_bias(bias_ref, d_idx, with_prev):
    ksl = slice(0, 2 * ATTN_BLOCK) if with_prev else slice(2 * ATTN_BLOCK, 3 * ATTN_BLOCK)
    return jnp.concatenate([bias_ref[d_idx, h, :, ksl] for h in range(HEADS_PER_TILE)], axis=0)


def _attn_blocks(blocks):
    nb = ATTN_BLOCK
    lane = lax.broadcasted_iota(I32, (nb, LANES), 1)
    head0 = lane < HEAD_DIM
    biases = [blk[3] for blk in blocks]
    blocks = [blk[:3] for blk in blocks]
    qs = [q * (HEAD_DIM ** -0.5 * LOG2E) for q, _, _ in blocks]
    qq = [jnp.concatenate([jnp.where(head0, q, 0.0), jnp.where(head0, 0.0, q)], axis=0).astype(BF16)
          for q in qs]
    ks = [k.astype(BF16) for _, k, _ in blocks]
    vs = [v.astype(BF16) for _, _, v in blocks]
    v_ones = [jnp.concatenate([v, jnp.ones_like(v)], axis=-1) for v in vs]
    s = [lax.dot_general(q2, k, (((1,), (1,)), ((), ())), preferred_element_type=F32) + bias
         for q2, k, bias in zip(qq, ks, biases)]
    m = [jnp.max(x, axis=-1, keepdims=True) for x in s]
    p = [jnp.exp2(x - mx).astype(BF16) for x, mx in zip(s, m)]
    acc = [jnp.dot(px, vo, preferred_element_type=F32) for px, vo in zip(p, v_ones)]
    head0_wide = jnp.concatenate([head0, head0], axis=-1)
    outs = []
    for a, mx in zip(acc, m):
        ol = jnp.where(head0_wide, a[:nb], a[nb:])
        l = ol[:, LANES:]
        outs.append((ol[:, :LANES] * (1.0 / l), jnp.where(head0, mx[:nb], mx[nb:]) + jnp.log2(l)))
    return outs


def _attn_kernel(qkv_ref, bias_ref, o_ref, o4_ref, l4_ref, o16_ref, l16_ref, fin_ref):
    grp = ATTN_GROUP16

    def body16(j, c):
        rs = [j * grp + a for a in range(grp)]
        bias = _attn_bias(bias_ref, 2, False)
        res = _attn_blocks([(qkv_ref[r, 0], qkv_ref[r, 1], qkv_ref[r, 2], bias) for r in rs])
        for r, (o, l) in zip(rs, res):
            o16_ref[r] = o
            l16_ref[r] = l
        return c
    for j in range(N_RES // grp):
        body16(j, 0)

    def gather4(c, rho, u0, nu):
        return jnp.concatenate([qkv_ref[rho + 4 * a, c, pl.ds(u0, nu), :] for a in range(4)], axis=0)

    def store4(rho, u0, o, l):
        for a in range(4):
            o4_ref[rho + 4 * a, pl.ds(u0, 32), :] = o[a * 32:(a + 1) * 32]
            l4_ref[rho + 4 * a, pl.ds(u0, 32), :] = l[a * 32:(a + 1) * 32]

    def body4(j, c):
        bias = _attn_bias(bias_ref, 1, True)
        todo = [(2 * j + a, 32 * n) for a in range(2) for n in range(1, 4)]
        res = _attn_blocks([(gather4(0, rho, u0, 32), gather4(1, rho, u0 - 32, 64),
                             gather4(2, rho, u0 - 32, 64), bias) for rho, u0 in todo])
        for (rho, u0), (o, l) in zip(todo, res):
            store4(rho, u0, o, l)
        return c
    for j in range(2):
        body4(j, 0)

    def gather1(ref, u0, nu, *lead):
        return jnp.concatenate([ref[(r,) + lead + (pl.ds(u0, nu), slice(None))] for r in range(N_RES)],
                               axis=0)

    def merge_and_store(u0, o1, l1):
        l4 = gather1(l4_ref, u0, 8)
        l16 = gather1(l16_ref, u0, 8)
        mx = jnp.maximum(jnp.maximum(l1, l4), l16)
        e1 = jnp.exp2(l1 - mx)
        e4 = jnp.exp2(l4 - mx)
        e16 = jnp.exp2(l16 - mx)
        inv = 1.0 / (e1 + e4 + e16)
        o = (e1 * inv) * o1 + (e4 * inv) * gather1(o4_ref, u0, 8) + (e16 * inv) * gather1(o16_ref, u0, 8)
        for r in range(N_RES):
            fin_ref[r, pl.ds(u0, 8), :] = o[r * 8:(r + 1) * 8]

    bias4 = _attn_bias(bias_ref, 1, False)
    first = [tuple(gather4(c, rho, 0, 32) for c in range(3)) + (bias4,) for rho in range(4)]
    first.append(tuple(gather1(qkv_ref, 0, 8, c) for c in range(3)) + (_attn_bias(bias_ref, 0, False),))
    res = _attn_blocks(first)
    for rho in range(4):
        store4(rho, 0, *res[rho])
    merge_and_store(0, *res[4])

    n_blocks1 = RES_LEN // 8 - 1

    def body1(j, c):
        u0s = [(1 + j * ATTN_GROUP1 + a) * 8 for a in range(ATTN_GROUP1)]
        bias = _attn_bias(bias_ref, 0, True)
        res = _attn_blocks([(gather1(qkv_ref, u0, 8, 0), gather1(qkv_ref, u0 - 8, 16, 1),
                             gather1(qkv_ref, u0 - 8, 16, 2), bias) for u0 in u0s])
        for u0, (o, l) in zip(u0s, res):
            merge_and_store(u0, o, l)
        return c
    for j in range(n_blocks1 // ATTN_GROUP1):
        body1(j, 0)

    def emit(r, c):
        o_ref[r] = fin_ref[r].astype(o_ref.dtype)
        return c
    lax.fori_loop(0, N_RES, emit, 0)


def attention(qkv):
    batch = qkv.shape[0]
    scratch = pltpu.VMEM((N_RES, RES_LEN, LANES), F32)
    return pl.pallas_call(
        _attn_kernel,
        grid=(batch, N_LANE_TILES),
        in_specs=[
            pl.BlockSpec((None, N_RES, None, 3, RES_LEN, LANES), lambda b, g: (b, 0, g, 0, 0, 0)),
            pl.BlockSpec((len(DILATIONS), HEADS_PER_TILE, ATTN_BLOCK, 3 * ATTN_BLOCK),
                         lambda b, g: (0, g, 0, 0)),
        ],
        out_specs=pl.BlockSpec((None, N_RES, None, RES_LEN, LANES), lambda b, g: (b, 0, g, 0, 0)),
        out_shape=jax.ShapeDtypeStruct((batch, N_RES, N_LANE_TILES, RES_LEN, LANES), BF16),
        scratch_shapes=[scratch] * 5,
        compiler_params=_cparams("arbitrary", "arbitrary"),
        name="dilated_attention",
    )(qkv, _attn_bias_tables())


def _split2(x):
    a = x.astype(BF16)
    b = (x - a.astype(F32)).astype(BF16)
    return a, b


def _hgrn_kernel(q_ref, f_ref, i_ref, g_ref, gamma_ref, ng_ref, tri_ref, y_ref):
    sup, ch = HGRN_SUPER, HGRN_CHUNK
    n_ch = sup // ch
    gam = gamma_ref[...]
    gmx = jnp.max(gam, axis=0, keepdims=True)
    ge = jnp.exp(gam - gmx)
    lb = ge[0:1] / jnp.sum(ge, axis=0, keepdims=True)
    ng = ng_ref[...]
    tri = tri_ref[...]
    ri = lax.broadcasted_iota(I32, (2 * sup, sup), 0) % sup
    ci = lax.broadcasted_iota(I32, (2 * sup, sup), 1)
    causal_bd2 = (ri // ch == ci // ch) & (ri >= ci)
    head0 = lax.broadcasted_iota(I32, (sup, LANES), 1) < HEAD_DIM
    row_chunk = lax.broadcasted_iota(I32, (sup, LANES), 0) // ch
    same_head = (lax.broadcasted_iota(I32, (LANES, LANES), 0) // HEAD_DIM
                 == lax.broadcasted_iota(I32, (LANES, LANES), 1) // HEAD_DIM)

    def sigmoid(x):
        return 1.0 / (1.0 + jnp.exp(-x))

    def dot_nt(a, b):
        return lax.dot_general(a, b, (((1,), (1,)), ((), ())), preferred_element_type=F32)

    def dot_tn(a, b):
        return lax.dot_general(a, b, (((0,), (0,)), ((), ())), preferred_element_type=F32)

    def expand(x):
        zero = jnp.zeros_like(x)
        return jnp.concatenate([jnp.where(row_chunk == c, x, zero) for c in range(n_ch)], axis=1)

    def body(j, state):
        grp = range(HGRN_GROUP)
        rows = [pl.ds((j * HGRN_GROUP + a) * sup, sup) for a in grp]
        fr = [f_ref[r, :].astype(F32) for r in rows]
        t = [jnp.exp(-jnp.abs(x)) for x in fr]
        rcp = [1.0 / (1.0 + x) for x in t]
        sig = [jnp.where(f >= 0, r, x * r) for f, r, x in zip(fr, rcp, t)]
        nsig = [jnp.where(f >= 0, x * r, r) for f, r, x in zip(fr, rcp, t)]
        logf = [jnp.log(lb + (1.0 - lb) * x) for x in sig]
        key = [(1.0 - lb) * x for x in nsig]
        parts = [_split2(x) for x in logf]
        b = [sum(jnp.dot(tri, p, preferred_element_type=F32) for p in ps) for ps in parts]
        totals = [[x[(c + 1) * ch - 1:(c + 1) * ch, :] for c in range(n_ch)] for x in b]
        b_last = [jnp.concatenate([jnp.broadcast_to(t_, (ch, LANES)) for t_ in ts], axis=0) for ts in totals]
        pad = jnp.zeros((LANES - n_ch, LANES), F32)
        decay_t = [jnp.exp(jnp.transpose(jnp.concatenate(ts + [pad], axis=0))) for ts in totals]
        qv = [q_ref[r, :].astype(F32) for r in rows]
        q_in = [(x * sigmoid(x) * jnp.exp(bb)).astype(BF16) for x, bb in zip(qv, b)]
        k_in = [(k * jnp.exp(-bb)).astype(BF16) for k, bb in zip(key, b)]
        k_end = [(k * jnp.exp(bl - bb)).astype(BF16) for k, bl, bb in zip(key, b_last, b)]
        vv = [i_ref[r, :] for r in rows]
        q2 = [jnp.concatenate([jnp.where(head0, x, jnp.zeros_like(x)), jnp.where(head0, jnp.zeros_like(x), x)],
                              axis=0) for x in q_in]
        att = [jnp.where(causal_bd2, dot_nt(x, k), 0.0).astype(BF16) for x, k in zip(q2, k_in)]
        intra = [jnp.dot(a_, v, preferred_element_type=F32) for a_, v in zip(att, vv)]
        o_intra = [jnp.where(head0, x[:sup], x[sup:]) for x in intra]
        ds_all = [dot_tn(k, expand(v)) for k, v in zip(k_end, vv)]
        s_prev = []
        for a in grp:
            for c in range(n_ch):
                s_prev.append(state.astype(BF16))
                ds = jnp.where(same_head, ds_all[a][:, c * LANES:(c + 1) * LANES], 0.0)
                state = decay_t[a][:, c:c + 1] * state + ds
        o_inter = [jnp.concatenate(
            [jnp.dot(q_in[a][c * ch:(c + 1) * ch], s_prev[a * n_ch + c], preferred_element_type=F32)
             for c in range(n_ch)], axis=0) for a in grp]
        o = [x + y for x, y in zip(o_intra, o_inter)]
        sq = [x * x for x in o]
        ss0 = [jnp.sum(jnp.where(head0, x, 0.0), axis=-1, keepdims=True) for x in sq]
        ss1 = [jnp.sum(jnp.where(head0, 0.0, x), axis=-1, keepdims=True) for x in sq]
        ms = [jnp.where(head0, x, y) * (1.0 / HEAD_DIM) for x, y in zip(ss0, ss1)]
        gv = [g_ref[r, :].astype(F32) for r in rows]
        for r, x, m_, g_ in zip(rows, o, ms, gv):
            y_ref[r, :] = (x * lax.rsqrt(m_ + EPS) * ng * (g_ * sigmoid(g_))).astype(y_ref.dtype)
        return state

    state = jnp.zeros((LANES, LANES), F32)
    for j in range(SEQ // (sup * HGRN_GROUP)):
        state = body(j, state)


def hgrn(hg3, gamma, norm_g):
    batch = hg3.shape[0]
    r = jnp.arange(HGRN_SUPER)
    same = (r[:, None] // HGRN_CHUNK) == (r[None, :] // HGRN_CHUNK)
    tri = (same & (r[:, None] >= r[None, :])).astype(BF16)
    col = lambda which: (lambda b, g: (b, 0, which * N_LANE_TILES + g))
    const2 = lambda b, g: (0, 0)
    return pl.pallas_call(
        _hgrn_kernel,
        grid=(batch, N_LANE_TILES),
        in_specs=[
            pl.BlockSpec((None, SEQ, LANES), col(0)),
            pl.BlockSpec((None, SEQ, LANES), col(1)),
            pl.BlockSpec((None, SEQ, LANES), col(2)),
            pl.BlockSpec((None, SEQ, LANES), col(3)),
            pl.BlockSpec((2, LANES), lambda b, g: (0, g)),
            pl.BlockSpec((1, LANES), lambda b, g: (0, g)),
            pl.BlockSpec((HGRN_SUPER, HGRN_SUPER), const2),
        ],
        out_specs=pl.BlockSpec((None, SEQ, LANES), lambda b, g: (b, 0, g)),
        out_shape=jax.ShapeDtypeStruct((batch, SEQ, WIDTH), BF16),
        compiler_params=_cparams("arbitrary", "arbitrary"),
        name="hgrn2",
    )(hg3, hg3, hg3, hg3, gamma.astype(F32), norm_g.reshape(1, WIDTH).astype(F32), tri)


def _post_mix_kernel(oa_ref, yh_ref, x_ref, ag_ref, wo_ref, g2_ref, wrh_ref, wrl_ref, br_ref, perm_t_ref,
                     tri_ref, lower_ref,
                     x1_ref, h2_ref, rec_ref, tab_ref, carry_ref, lg_sc):
    i = pl.program_id(0)

    @pl.when(i == 0)
    def _():
        lg_sc[...] = jnp.zeros_like(lg_sc)
    _route_tile(lg_sc[...], (i > 0).astype(F32), tri_ref, lower_ref, rec_ref, tab_ref, carry_ref)

    oa = jnp.concatenate(
        [jnp.concatenate([oa_ref[r, g] for g in range(N_LANE_TILES)], axis=1)
         for r in range(N_RES)], axis=0).astype(F32)
    ms = jnp.mean(oa * oa, axis=-1, keepdims=True)
    ya = (oa * lax.rsqrt(ms + EPS) * ag_ref[...]).astype(BF16)
    ya = jnp.dot(perm_t_ref[...], ya, preferred_element_type=F32).astype(BF16)
    mix = (jnp.dot(ya, wo_ref[:WIDTH, :], preferred_element_type=F32)
           + jnp.dot(yh_ref[...], wo_ref[WIDTH:, :], preferred_element_type=F32))
    x1 = x_ref[...] + mix
    x1_ref[...] = x1
    ms2 = jnp.mean(x1 * x1, axis=-1, keepdims=True)
    h2 = x1 * lax.rsqrt(ms2 + EPS) * g2_ref[...]
    hi = h2.astype(BF16)
    h2_ref[...] = hi
    lo = (h2 - hi.astype(F32)).astype(BF16)
    nt = (((1,), (1,)), ((), ()))
    wrh = wrh_ref[...]
    lg_sc[...] = (lax.dot_general(wrh, hi, nt, preferred_element_type=F32)
                  + lax.dot_general(wrh, lo, nt, preferred_element_type=F32)
                  + lax.dot_general(wrl_ref[...], hi, nt, preferred_element_type=F32)
                  + br_ref[...])


def post_mix(oa, yh, x2, attn_g, w_out_bf16, g2, wr_hi, wr_lo, br):
    n = x2.shape[0]
    tiles_per_b = SEQ // TILE_TOKENS
    assert ROUTE_TM == TILE_TOKENS
    n_tiles = n // TILE_TOKENS
    cur = lambda i: jnp.minimum(i, n_tiles - 1)
    prev = lambda i: jnp.maximum(i - 1, 0)
    row = lambda w: pl.BlockSpec((TILE_TOKENS, w), lambda i: (cur(i), 0))
    const = lambda r, c: pl.BlockSpec((r, c), lambda i: (0, 0))
    return pl.pallas_call(
        _post_mix_kernel,
        grid=(n_tiles + 1,),
        in_specs=[pl.BlockSpec((None, N_RES, N_LANE_TILES, U_PER_TILE, LANES),
                               lambda i: (cur(i) // tiles_per_b, 0, 0, cur(i) % tiles_per_b, 0)),
                  row(WIDTH), row(D_MODEL),
                  const(1, WIDTH), const(2 * WIDTH, D_MODEL), const(1, D_MODEL),
                  const(LANES, D_MODEL), const(LANES, D_MODEL), const(LANES, 1),
                  const(TILE_TOKENS, TILE_TOKENS), const(ROUTE_TM, ROUTE_TM), const(N_EXPERTS, N_EXPERTS)],
        out_specs=[row(D_MODEL), row(D_MODEL),
                   pl.BlockSpec((8, ROUTE_TM), lambda i: (0, prev(i))),
                   pl.BlockSpec((1, 8, LANES), lambda i: (prev(i), 0, 0))],
        out_shape=[jax.ShapeDtypeStruct((n, D_MODEL), F32),
                   jax.ShapeDtypeStruct((n, D_MODEL), BF16),
                   jax.ShapeDtypeStruct((8, n), F32),
                   jax.ShapeDtypeStruct((n_tiles, 8, LANES), F32)],
        scratch_shapes=[pltpu.VMEM((N_EXPERTS, LANES), F32), pltpu.VMEM((LANES, TILE_TOKENS), F32)],
        compiler_params=_cparams("arbitrary"),
        name="post_mix",
    )(oa, yh, x2, attn_g.reshape(1, WIDTH), w_out_bf16, g2.reshape(1, D_MODEL), wr_hi, wr_lo, br,
      _tile_permutation().T, *_route_constants())


ROUTE_TM = 512
EXPERT_ROW0 = 32
R_E1, R_E2, R_SLOT1, R_SLOT2, R_G1, R_G2 = 0, 1, 2, 3, 4, 5
T_CARRY, T_ROWS, T_OFF = 0, 1, 2
SEG_END = 0
TOTAL_LANE = N_EXPERTS
RUN_ALIGN = 8
LOCAL_ROWS = 2 * ROUTE_TM + N_EXPERTS * RUN_ALIGN


def _route_tile(lg, live, tri_ref, lower_ref, rec_ref, tab_ref, carry_ref):
    i = pl.program_id(0)

    @pl.when(i == 0)
    def _():
        carry_ref[...] = jnp.zeros_like(carry_ref)

    tm = lg.shape[1]
    sub8 = lax.broadcasted_iota(I32, (8, tm), 0).astype(F32)
    big = 8.0
    gmask = sub8 < N_GROUPS
    gl = jnp.where(gmask, lg[0:8], NEG)
    gmax = jnp.max(gl, axis=0, keepdims=True)
    gsel = jnp.min(jnp.where(gmask & (gl == gmax), sub8, big), axis=0, keepdims=True)
    gsum = jnp.sum(jnp.where(gmask, jnp.exp(gl - gmax), 0.0), axis=0, keepdims=True)
    w_g = 1.0 / gsum
    el = jnp.zeros((EXPERTS_PER_GROUP, tm), F32)
    for g in range(N_GROUPS):
        lo = EXPERT_ROW0 + g * EXPERTS_PER_GROUP
        el = jnp.where(gsel == g, lg[lo:lo + EXPERTS_PER_GROUP], el)
    v1 = jnp.max(el, axis=0, keepdims=True)
    i1 = jnp.min(jnp.where(el == v1, sub8, big), axis=0, keepdims=True)
    el2 = jnp.where(sub8 == i1, NEG, el)
    v2 = jnp.max(el2, axis=0, keepdims=True)
    i2 = jnp.min(jnp.where((el2 == v2) & (sub8 != i1), sub8, big), axis=0, keepdims=True)
    ex = jnp.exp(v2 - v1)
    den = 1.0 / (1.0 + ex)
    g1 = w_g * den
    g2 = w_g * ex * den
    e1 = gsel * EXPERTS_PER_GROUP + i1
    e2 = gsel * EXPERTS_PER_GROUP + i2
    sub_e = lax.broadcasted_iota(I32, (N_EXPERTS, tm), 0).astype(F32)
    oh1 = sub_e == e1
    oh2 = sub_e == e2
    onehot = (oh1 | oh2).astype(BF16)
    before = jnp.dot(onehot, tri_ref[...], preferred_element_type=F32)
    count = jnp.sum(onehot.astype(F32), axis=1, keepdims=True)
    units = jnp.floor((count + (RUN_ALIGN - 1)) * (1.0 / RUN_ALIGN)) * live
    units = jnp.broadcast_to(units, (N_EXPERTS, LANES))
    rows = units * RUN_ALIGN
    off = RUN_ALIGN * jnp.dot(lower_ref[...], units.astype(BF16), preferred_element_type=F32)
    place = off[:, 0:1] + before
    slot1 = jnp.sum(jnp.where(oh1, place, 0.0), axis=0, keepdims=True)
    slot2 = jnp.sum(jnp.where(oh2, place, 0.0), axis=0, keepdims=True)
    total = jnp.sum(rows, axis=0, keepdims=True)
    eye = (lax.broadcasted_iota(I32, (N_EXPERTS, LANES), 0) == lax.broadcasted_iota(I32, (N_EXPERTS, LANES), 1))
    lane1 = lax.broadcasted_iota(I32, (1, LANES), 1)
    as_row = lambda col: jnp.where(lane1 == TOTAL_LANE, total,
                                   jnp.sum(jnp.where(eye, col, 0.0), axis=0, keepdims=True))
    sub = lax.broadcasted_iota(I32, (8, LANES), 0)
    tab_ref[0] = jnp.where(sub == T_CARRY, as_row(carry_ref[...]),
                           jnp.where(sub == T_ROWS, as_row(rows), jnp.where(sub == T_OFF, as_row(off), 0.0)))
    carry_ref[...] += rows
    rec = jnp.zeros((8, tm), F32)
    for slot, val in ((R_E1, e1), (R_E2, e2), (R_SLOT1, slot1),
                      (R_SLOT2, slot2), (R_G1, g1), (R_G2, g2)):
        rec = jnp.where(sub8 == slot, val, rec)
    rec_ref[...] = rec


def _route_constants():
    r = jnp.arange(ROUTE_TM)
    tri = (r[:, None] < r[None, :]).astype(BF16)
    e = jnp.arange(N_EXPERTS)
    lower = (e[:, None] > e[None, :]).astype(BF16)
    return tri, lower


P_START, P_ROWS, P_OFF = 0, 1, 2


def _plan_kernel(carry_ref, rows_ref, off_ref, upper_ref, plan_ref, seg_ref):
    n_tiles = carry_ref.shape[0]
    lane = lax.broadcasted_iota(I32, (1, LANES), 1)
    last = pl.ds(n_tiles - 1, 1)
    totals = jnp.where(lane < N_EXPERTS, carry_ref[last, :] + rows_ref[last, :], 0.0)
    blocks = jnp.floor((totals + (FFN_BLOCK - 1)) * (1.0 / FFN_BLOCK))
    starts = FFN_BLOCK * jnp.dot(jnp.broadcast_to(blocks, (8, LANES)).astype(BF16), upper_ref[...],
                                 preferred_element_type=F32)[0:1]
    seg_ref[...] = jnp.broadcast_to(starts + blocks * FFN_BLOCK, seg_ref.shape)
    plan_ref[P_START] = (carry_ref[...] + starts).astype(I32)
    plan_ref[P_ROWS] = rows_ref[...].astype(I32)
    plan_ref[P_OFF] = off_ref[...].astype(I32)


def run_plan(tab):
    n_tiles = tab.shape[0]
    ln = jnp.arange(LANES)
    upper = (ln[:, None] < ln[None, :]).astype(BF16)
    return pl.pallas_call(
        _plan_kernel,
        out_shape=[jax.ShapeDtypeStruct((3, n_tiles, LANES), I32),
                   jax.ShapeDtypeStruct((8, LANES), F32)],
        compiler_params=pltpu.CompilerParams(vmem_limit_bytes=VMEM_LIMIT),
        name="run_plan",
    )(tab[:, T_CARRY], tab[:, T_ROWS], tab[:, T_OFF], upper)


RUN_PIECE = 64
SMALL_PIECES = (32, 16, 8)
TOTAL_PIECES = (1024, 512, 256, 128, 64, 32, 16, 8)
SORT_CHUNK = 256


def _for_each_run_piece(plan_ref, tile, n_tiles, fn):
    plane = n_tiles * LANES

    def per_expert(e, c):
        idx = tile * LANES + e
        start = plan_ref[P_START * plane + idx]
        rows = plan_ref[P_ROWS * plane + idx]
        off = plan_ref[P_OFF * plane + idx]

        def big(j, c2):
            fn(pl.multiple_of(off + j * RUN_PIECE, RUN_ALIGN), pl.multiple_of(start + j * RUN_PIECE, RUN_ALIGN),
               RUN_PIECE)
            return c2
        lax.fori_loop(0, lax.shift_right_logical(rows, 6), big, 0)
        for size in SMALL_PIECES:
            done = jnp.bitwise_and(rows, -2 * size)

            @pl.when(jnp.bitwise_and(rows, size) != 0)
            def _():
                fn(pl.multiple_of(off + done, RUN_ALIGN), pl.multiple_of(start + done, RUN_ALIGN), size)
        return c
    lax.fori_loop(0, N_EXPERTS, per_expert, 0)


def _for_each_total_piece(plan_ref, tile, n_tiles, fn):
    total = plan_ref[P_OFF * n_tiles * LANES + tile * LANES + TOTAL_LANE]
    for size in TOTAL_PIECES:
        @pl.when(jnp.bitwise_and(total, size) != 0)
        def _():
            fn(size)


def _dispatch_kernel(plan_ref, ends_ref, nv_ref, h2_ref, rec_ref, xs_ref, zero_buf, stage, sem, zsem):
    i = pl.program_id(0)
    n_steps = pl.num_programs(0)
    n_blocks = xs_ref.shape[0] // FFN_BLOCK

    @pl.when(i == 0)
    def _():
        zero_buf[...] = jnp.zeros_like(zero_buf)

        def zero_copy(row0):
            return pltpu.make_async_copy(zero_buf, xs_ref.at[pl.ds(row0, FFN_BLOCK)], zsem)

        def seg_end(e):
            return ends_ref[e], ends_ref[e] > jnp.where(e > 0, ends_ref[jnp.maximum(e - 1, 0)], 0)

        def start_e(e, c):
            end, nonempty = seg_end(e)

            @pl.when(nonempty)
            def _():
                zero_copy(pl.multiple_of(end - FFN_BLOCK, FFN_BLOCK)).start()
            return c
        lax.fori_loop(0, N_EXPERTS, start_e, 0)

        def start_b(blk, c):
            zero_copy(pl.multiple_of(blk * FFN_BLOCK, FFN_BLOCK)).start()
            return c
        lax.fori_loop(nv_ref[0], n_blocks, start_b, 0)

        def wait_e(e, c):
            _, nonempty = seg_end(e)

            @pl.when(nonempty)
            def _():
                zero_copy(0).wait()
            return c
        lax.fori_loop(0, N_EXPERTS, wait_e, 0)

        def wait_b(blk, c):
            zero_copy(0).wait()
            return c
        lax.fori_loop(nv_ref[0], n_blocks, wait_b, 0)

    slot = i % 2
    s1 = rec_ref[R_SLOT1:R_SLOT1 + 1, :]
    s2 = rec_ref[R_SLOT2:R_SLOT2 + 1, :]
    half = PACKED_W
    for c in range(LOCAL_ROWS // SORT_CHUNK):
        rowf = (lax.broadcasted_iota(I32, (SORT_CHUNK, ROUTE_TM), 0) + c * SORT_CHUNK).astype(F32)
        sel = ((rowf == s1) | (rowf == s2)).astype(BF16)
        pick = lambda cols: jnp.dot(sel, h2_ref[:, cols], preferred_element_type=F32)
        stage[slot, c * SORT_CHUNK:(c + 1) * SORT_CHUNK, :] = _pack_pair(pick(slice(0, half)),
                                                                         pick(slice(half, 2 * half)))

    def start_piece(lrow, grow, size):
        pltpu.make_async_copy(stage.at[slot, pl.ds(lrow, size)], xs_ref.at[pl.ds(grow, size)],
                              sem.at[slot]).start()
    _for_each_run_piece(plan_ref, i, n_steps, start_piece)

    def wait_pieces(tile, s):
        def wait_piece(size):
            pltpu.make_async_copy(stage.at[s, pl.ds(0, size)], xs_ref.at[pl.ds(0, size)], sem.at[s]).wait()
        _for_each_total_piece(plan_ref, tile, n_steps, wait_piece)

    @pl.when(i > 0)
    def _():
        wait_pieces(i - 1, 1 - slot)

    @pl.when(i + 1 == n_steps)
    def _():
        wait_pieces(i, slot)


def dispatch(h2, rec, plan, seg_ends, n_valid, n_rows):
    n = h2.shape[0]
    return pl.pallas_call(
        _dispatch_kernel,
        grid_spec=pltpu.PrefetchScalarGridSpec(
            num_scalar_prefetch=3,
            grid=(n // ROUTE_TM,),
            in_specs=[pl.BlockSpec((ROUTE_TM, D_MODEL), lambda i, *_: (i, 0)),
                      pl.BlockSpec((8, ROUTE_TM), lambda i, *_: (0, i))],
            out_specs=pl.BlockSpec(memory_space=pl.ANY),
            scratch_shapes=[pltpu.VMEM((FFN_BLOCK, PACKED_W), U32),
                            pltpu.VMEM((2, LOCAL_ROWS, PACKED_W), U32),
                            pltpu.SemaphoreType.DMA((2,)),
                            pltpu.SemaphoreType.DMA(())],
        ),
        out_shape=jax.ShapeDtypeStruct((n_rows, PACKED_W), U32),
        compiler_params=_cparams("arbitrary"),
        name="dispatch",
    )(plan, seg_ends, n_valid, h2, rec)


def _ffn_kernel(be_ref, nv_ref, xs_ref, wg_ref, wu_ref, wd_ref, ys_ref, wg_sc, wu_sc, wd_sc):
    i = pl.program_id(0)
    prev = be_ref[jnp.maximum(i - 1, 0)]
    fresh = (i == 0) | (be_ref[i] != prev)

    @pl.when(fresh)
    def _():
        wg_sc[...] = wg_ref[0].astype(BF16)
        wu_sc[...] = wu_ref[0].astype(BF16)
        wd_sc[...] = wd_ref[0].astype(BF16)

    @pl.when(i < nv_ref[0])
    def _():
        xa, xb = (t.astype(BF16) for t in _unpack_halves(xs_ref[...]))
        half = PACKED_W
        gate = (jnp.dot(xa, wg_sc[:half, :], preferred_element_type=F32)
                + jnp.dot(xb, wg_sc[half:, :], preferred_element_type=F32))
        up = (jnp.dot(xa, wu_sc[:half, :], preferred_element_type=F32)
              + jnp.dot(xb, wu_sc[half:, :], preferred_element_type=F32))
        hid = (gate * (1.0 / (1.0 + jnp.exp(-gate))) * up).astype(BF16)
        ys_ref[...] = _pack_halves(jnp.dot(hid, wd_sc[...], preferred_element_type=F32))

    @pl.when(i >= nv_ref[0])
    def _():
        ys_ref[...] = jnp.zeros_like(ys_ref)


def expert_ffn(xs, block_expert, n_valid, w_gate, w_up, w_down):
    n_rows = xs.shape[0]
    n_blocks = n_rows // FFN_BLOCK
    wmap = lambda i, be, nv: (be[i], 0, 0)
    return pl.pallas_call(
        _ffn_kernel,
        grid_spec=pltpu.PrefetchScalarGridSpec(
            num_scalar_prefetch=2,
            grid=(n_blocks,),
            in_specs=[pl.BlockSpec((FFN_BLOCK, PACKED_W), lambda i, be, nv: (jnp.minimum(i, nv[0] - 1), 0)),
                      pl.BlockSpec((1, D_MODEL, D_EXPERT), wmap),
                      pl.BlockSpec((1, D_MODEL, D_EXPERT), wmap),
                      pl.BlockSpec((1, D_EXPERT, D_MODEL), wmap)],
            out_specs=pl.BlockSpec((FFN_BLOCK, PACKED_W), lambda i, be, nv: (i, 0)),
            scratch_shapes=[pltpu.VMEM((D_MODEL, D_EXPERT), BF16),
                            pltpu.VMEM((D_MODEL, D_EXPERT), BF16),
                            pltpu.VMEM((D_EXPERT, D_MODEL), BF16)],
        ),
        out_shape=jax.ShapeDtypeStruct((n_rows, PACKED_W), U32),
        compiler_params=_cparams("arbitrary"),
        name="expert_ffn",
    )(block_expert, n_valid, xs, w_gate, w_up, w_down)


def _combine_kernel(plan_ref, x1_ref, rec_ref, gf_ref, ys_ref, out_ref, buf, sem):
    i = pl.program_id(0)
    n_steps = pl.num_programs(0)
    slot = i % 2

    def fetch_tile(tile, s):
        def start_piece(lrow, grow, size):
            pltpu.make_async_copy(ys_ref.at[pl.ds(grow, size)], buf.at[s, pl.ds(lrow, size)], sem.at[s]).start()
        _for_each_run_piece(plan_ref, tile, n_steps, start_piece)

    @pl.when(i == 0)
    def _():
        buf[...] = jnp.zeros_like(buf)
        fetch_tile(0, 0)

    @pl.when(i + 1 < n_steps)
    def _():
        fetch_tile(i + 1, 1 - slot)

    def wait_piece(size):
        pltpu.make_async_copy(ys_ref.at[pl.ds(0, size)], buf.at[slot, pl.ds(0, size)], sem.at[slot]).wait()
    _for_each_total_piece(plan_ref, i, n_steps, wait_piece)

    rowf = lax.broadcasted_iota(I32, (LOCAL_ROWS, ROUTE_TM), 0).astype(F32)
    sel = (jnp.where(rowf == rec_ref[R_SLOT1:R_SLOT1 + 1, :], rec_ref[R_G1:R_G1 + 1, :], 0.0)
           + jnp.where(rowf == rec_ref[R_SLOT2:R_SLOT2 + 1, :], rec_ref[R_G2:R_G2 + 1, :], 0.0)).astype(BF16)
    ya, yb = (t.astype(BF16) for t in _unpack_halves(buf[slot]))
    half = PACKED_W
    tn = (((0,), (0,)), ((), ()))
    xa = x1_ref[:, :half] + lax.dot_general(sel, ya, tn, preferred_element_type=F32)
    xb = x1_ref[:, half:] + lax.dot_general(sel, yb, tn, preferred_element_type=F32)
    ms = (jnp.sum(xa * xa, axis=-1, keepdims=True) + jnp.sum(xb * xb, axis=-1, keepdims=True)) * (1.0 / D_MODEL)
    scale = lax.rsqrt(ms + EPS)
    out_ref[:, :half] = xa * scale * gf_ref[:, :half]
    out_ref[:, half:] = xb * scale * gf_ref[:, half:]


def combine(x1, rec, norm_f_g, ys, plan):
    n = x1.shape[0]
    tm = ROUTE_TM
    return pl.pallas_call(
        _combine_kernel,
        grid_spec=pltpu.PrefetchScalarGridSpec(
            num_scalar_prefetch=1,
            grid=(n // tm,),
            in_specs=[pl.BlockSpec((tm, D_MODEL), lambda i, p: (i, 0)),
                      pl.BlockSpec((8, tm), lambda i, p: (0, i)),
                      pl.BlockSpec((1, D_MODEL), lambda i, p: (0, 0)),
                      pl.BlockSpec(memory_space=pl.ANY)],
            out_specs=pl.BlockSpec((tm, D_MODEL), lambda i, p: (i, 0)),
            scratch_shapes=[pltpu.VMEM((2, LOCAL_ROWS, PACKED_W), U32),
                            pltpu.SemaphoreType.DMA((2,))],
        ),
        out_shape=jax.ShapeDtypeStruct((n, D_MODEL), F32),
        compiler_params=_cparams("arbitrary"),
        name="combine",
    )(plan, x1, rec, norm_f_g.reshape(1, D_MODEL), ys)


def _router_weights(w_group, b_group, w_router, b_router):
    w = jnp.zeros((LANES, D_MODEL), F32)
    w = w.at[:N_GROUPS].set(w_group.T).at[EXPERT_ROW0:EXPERT_ROW0 + N_EXPERTS].set(w_router.T)
    b = jnp.zeros((LANES, 1), F32)
    b = b.at[:N_GROUPS, 0].set(b_group).at[EXPERT_ROW0:EXPERT_ROW0 + N_EXPERTS, 0].set(b_router)
    hi = w.astype(BF16)
    lo = (w - hi.astype(F32)).astype(BF16)
    return hi, lo, b


def _sorted_rows_bound(n_tokens):
    worst = (2 * n_tokens + (n_tokens // ROUTE_TM) * N_EXPERTS * (RUN_ALIGN - 1)
             + N_EXPERTS * (FFN_BLOCK - 1))
    return -(-worst // FFN_BLOCK) * FFN_BLOCK


def _block_plan(seg, n_rows):
    seg_ends = seg[SEG_END, :N_EXPERTS].astype(I32)
    n_blocks = n_rows // FFN_BLOCK
    blk_start = jnp.arange(n_blocks, dtype=I32) * FFN_BLOCK
    block_expert = jnp.minimum(jnp.sum((seg_ends[None, :] <= blk_start[:, None]).astype(I32), axis=1),
                               N_EXPERTS - 1)
    n_valid = seg_ends[-1:] // FFN_BLOCK
    return seg_ends, block_expert, n_valid


def kernel(x, norm1_g, w_in, attn_norm_g, hgrn_gamma, hgrn_norm_g, w_out, norm2_g, w_group, b_group,
           w_router, b_router, w_gate, w_up, w_down, norm_f_g):
    batch, seq, d = x.shape
    assert seq == SEQ and d == D_MODEL and norm1_g.shape[0] == 1
    n = batch * seq
    x2 = x.reshape(n, d)
    qkv, hg = in_proj(x2, norm1_g[0], w_in[0].astype(BF16), batch)
    oa = attention(qkv)
    yh = hgrn(hg.reshape(batch, seq, HG_W), hgrn_gamma, hgrn_norm_g[0]).reshape(n, WIDTH)
    wr_hi, wr_lo, br = _router_weights(w_group[0], b_group[0], w_router[0], b_router[0])
    x1, h2, rec, tab = post_mix(oa, yh, x2, attn_norm_g[0], w_out[0].astype(BF16), norm2_g[0],
                                wr_hi, wr_lo, br)
    plan, seg = run_plan(tab)
    plan = plan.reshape(-1)
    n_rows = _sorted_rows_bound(n)
    seg_ends, block_expert, n_valid = _block_plan(seg, n_rows)
    xs = dispatch(h2, rec, plan, seg_ends, n_valid, n_rows)
    ys = expert_ffn(xs, block_expert, n_valid, w_gate[0], w_up[0], w_down[0])
    return combine(x1, rec, norm_f_g, ys, plan).reshape(batch, seq, d)
```

```python
import jax
import jax.numpy as jnp
from jax import lax
from jax.experimental import pallas as pl
from jax.experimental.pallas import tpu as pltpu

F32 = jnp.float32
BF16 = jnp.bfloat16
I32 = jnp.int32
U32 = jnp.uint32

D_MODEL = 1024
HEAD_DIM = 64
N_HEADS = 8
WIDTH = N_HEADS * HEAD_DIM
QKV_W = 3 * WIDTH
HG_W = 4 * WIDTH
SEQ = 2048
ATTN_BLOCK = 128
DILATIONS = (1, 4, 16)
N_RES = 16
RES_LEN = SEQ // N_RES
TILE_TOKENS = 512
U_PER_TILE = TILE_TOKENS // N_RES
HGRN_CHUNK = 32
HGRN_SUPER = 256
HGRN_GROUP = 8
N_GROUPS = 4
EXPERTS_PER_GROUP = 8
N_EXPERTS = 32
D_EXPERT = 512
FFN_BLOCK = 512
PACKED_W = D_MODEL // 2
EPS = 1e-6
NEG = -1e30
LOG2E = 1.4426950408889634
LANES = 128
VMEM_LIMIT = 56 * 1024 * 1024


def _cparams(*sem):
    return pltpu.CompilerParams(dimension_semantics=sem, vmem_limit_bytes=VMEM_LIMIT)


def _pack_pair(a, b):
    hi = pltpu.bitcast(a.astype(BF16).astype(F32), U32)
    lo = pltpu.bitcast(b.astype(BF16).astype(F32), U32)
    return hi | (lo >> 16)


def _pack_halves(x):
    w = x.shape[1] // 2
    return _pack_pair(x[:, :w], x[:, w:])


def _unpack_halves(p):
    return pltpu.bitcast(p & jnp.uint32(0xFFFF0000), F32), pltpu.bitcast(p << 16, F32)


N_LANE_TILES = WIDTH // LANES


def _tile_permutation():
    i = jnp.arange(TILE_TOKENS)
    src = N_RES * (i % U_PER_TILE) + i // U_PER_TILE
    return (src[:, None] == jnp.arange(TILE_TOKENS)[None, :]).astype(BF16)


def _in_proj_kernel(x_ref, g_ref, w_ref, perm_ref, qkv_ref, hg_ref):
    x = x_ref[...]
    ms = jnp.mean(x * x, axis=-1, keepdims=True)
    h = (x * lax.rsqrt(ms + EPS) * g_ref[...]).astype(BF16)
    hp = jnp.dot(perm_ref[...], h, preferred_element_type=F32).astype(BF16)
    for c in range(QKV_W // WIDTH):
        res = jnp.dot(hp, w_ref[:, c * WIDTH:(c + 1) * WIDTH], preferred_element_type=F32)
        for r in range(N_RES):
            for g in range(N_LANE_TILES):
                qkv_ref[r, g, c] = res[r * U_PER_TILE:(r + 1) * U_PER_TILE, g * LANES:(g + 1) * LANES]
    for j in range(HG_W // WIDTH):
        wsl = slice(QKV_W + j * WIDTH, QKV_W + (j + 1) * WIDTH)
        hg_ref[:, j * WIDTH:(j + 1) * WIDTH] = jnp.dot(h, w_ref[:, wsl], preferred_element_type=F32).astype(BF16)


def in_proj(x2, g, w_bf16, batch):
    n = x2.shape[0]
    tiles_per_b = SEQ // TILE_TOKENS
    return pl.pallas_call(
        _in_proj_kernel,
        grid=(n // TILE_TOKENS,),
        in_specs=[
            pl.BlockSpec((TILE_TOKENS, D_MODEL), lambda i: (i, 0)),
            pl.BlockSpec((1, D_MODEL), lambda i: (0, 0)),
            pl.BlockSpec((D_MODEL, QKV_W + HG_W), lambda i: (0, 0)),
            pl.BlockSpec((TILE_TOKENS, TILE_TOKENS), lambda i: (0, 0)),
        ],
        out_specs=[
            pl.BlockSpec((None, N_RES, N_LANE_TILES, 3, U_PER_TILE, LANES),
                         lambda i: (i // tiles_per_b, 0, 0, 0, i % tiles_per_b, 0)),
            pl.BlockSpec((TILE_TOKENS, HG_W), lambda i: (i, 0)),
        ],
        out_shape=[
            jax.ShapeDtypeStruct((batch, N_RES, N_LANE_TILES, 3, RES_LEN, LANES), F32),
            jax.ShapeDtypeStruct((n, HG_W), BF16),
        ],
        compiler_params=_cparams("arbitrary"),
        name="in_proj",
    )(x2, g.reshape(1, D_MODEL), w_bf16, _tile_permutation())


ATTN_GROUP16 = 16
ATTN_GROUP1 = 5
HEADS_PER_TILE = LANES // HEAD_DIM


def _attn_bias_tables():
    slopes = 2.0 ** (-8.0 * jnp.arange(1, N_HEADS + 1, dtype=F32) / N_HEADS)

    def table(qpos, kpos, dil):
        sd = qpos[:, None] - kpos[None, :]
        b = -slopes[:, None, None] * (sd * dil).astype(F32)[None] * LOG2E
        return jnp.where(((sd >= 0) & (sd <= ATTN_BLOCK))[None], b, NEG)

    q = jnp.arange(ATTN_BLOCK)
    k = jnp.arange(2 * ATTN_BLOCK)
    q1 = 16 * (q % 8) + q // 8 + ATTN_BLOCK
    k1 = 16 * (k % 16) + k // 16
    q4 = 4 * (q % 32) + q // 32 + ATTN_BLOCK
    k4 = 4 * (k % 64) + k // 64
    q16 = q + ATTN_BLOCK
    none = jnp.full((N_HEADS, ATTN_BLOCK, 2 * ATTN_BLOCK), NEG, F32)
    tabs = [jnp.concatenate([table(q1, k1, 1), table(q1, q1, 1)], axis=-1),
            jnp.concatenate([table(q4, k4, 4), table(q4, q4, 4)], axis=-1),
            jnp.concatenate([none, table(q16, q16, 16)], axis=-1)]
    return jnp.stack(tabs, axis=0)


def _attn_bias(bias_ref, d_idx, with_prev):
    ksl = slice(0, 2 * ATTN_BLOCK) if with_prev else slice(2 * ATTN_BLOCK, 3 * ATTN_BLOCK)
    return jnp.concatenate([bias_ref[d_idx, h, :, ksl] for h in range(HEADS_PER_TILE)], axis=0)


def _attn_blocks(blocks):
    nb = ATTN_BLOCK
    lane = lax.broadcasted_iota(I32, (nb, LANES), 1)
    head0 = lane < HEAD_DIM
    biases = [blk[3] for blk in blocks]
    blocks = [blk[:3] for blk in blocks]
    qs = [q * (HEAD_DIM ** -0.5 * LOG2E) for q, _, _ in blocks]
    qq = [jnp.concatenate([jnp.where(head0, q, 0.0), jnp.where(head0, 0.0, q)], axis=0).astype(BF16)
          for q in qs]
    ks = [k.astype(BF16) for _, k, _ in blocks]
    vs = [v.astype(BF16) for _, _, v in blocks]
    v_ones = [jnp.concatenate([v, jnp.ones_like(v)], axis=-1) for v in vs]
    s = [lax.dot_general(q2, k, (((1,), (1,)), ((), ())), preferred_element_type=F32) + bias
         for q2, k, bias in zip(qq, ks, biases)]
    m = [jnp.max(x, axis=-1, keepdims=True) for x in s]
    p = [jnp.exp2(x - mx).astype(BF16) for x, mx in zip(s, m)]
    acc = [jnp.dot(px, vo, preferred_element_type=F32) for px, vo in zip(p, v_ones)]
    head0_wide = jnp.concatenate([head0, head0], axis=-1)
    outs = []
    for a, mx in zip(acc, m):
        ol = jnp.where(head0_wide, a[:nb], a[nb:])
        l = ol[:, LANES:]
        outs.append((ol[:, :LANES] * (1.0 / l), jnp.where(head0, mx[:nb], mx[nb:]) + jnp.log2(l)))
    return outs


def _attn_kernel(qkv_ref, bias_ref, o_ref, o4_ref, l4_ref, o16_ref, l16_ref, fin_ref):
    grp = ATTN_GROUP16

    def body16(j, c):
        rs = [j * grp + a for a in range(grp)]
        bias = _attn_bias(bias_ref, 2, False)
        res = _attn_blocks([(qkv_ref[r, 0], qkv_ref[r, 1], qkv_ref[r, 2], bias) for r in rs])
        for r, (o, l) in zip(rs, res):
            o16_ref[r] = o
            l16_ref[r] = l
        return c
    for j in range(N_RES // grp):
        body16(j, 0)

    def gather4(c, rho, u0, nu):
        return jnp.concatenate([qkv_ref[rho + 4 * a, c, pl.ds(u0, nu), :] for a in range(4)], axis=0)

    def store4(rho, u0, o, l):
        for a in range(4):
            o4_ref[rho + 4 * a, pl.ds(u0, 32), :] = o[a * 32:(a + 1) * 32]
            l4_ref[rho + 4 * a, pl.ds(u0, 32), :] = l[a * 32:(a + 1) * 32]

    def body4(j, c):
        bias = _attn_bias(bias_ref, 1, True)
        todo = [(2 * j + a, 32 * n) for a in range(2) for n in range(1, 4)]
        res = _attn_blocks([(gather4(0, rho, u0, 32), gather4(1, rho, u0 - 32, 64),
                             gather4(2, rho, u0 - 32, 64), bias) for rho, u0 in todo])
        for (rho, u0), (o, l) in zip(todo, res):
            store4(rho, u0, o, l)
        return c
    for j in range(2):
        body4(j, 0)

    def gather1(ref, u0, nu, *lead):
        return jnp.concatenate([ref[(r,) + lead + (pl.ds(u0, nu), slice(None))] for r in range(N_RES)],
                               axis=0)

    def merge_and_store(u0, o1, l1):
        l4 = gather1(l4_ref, u0, 8)
        l16 = gather1(l16_ref, u0, 8)
        mx = jnp.maximum(jnp.maximum(l1, l4), l16)
        e1 = jnp.exp2(l1 - mx)
        e4 = jnp.exp2(l4 - mx)
        e16 = jnp.exp2(l16 - mx)
        inv = 1.0 / (e1 + e4 + e16)
        o = (e1 * inv) * o1 + (e4 * inv) * gather1(o4_ref, u0, 8) + (e16 * inv) * gather1(o16_ref, u0, 8)
        for r in range(N_RES):
            fin_ref[r, pl.ds(u0, 8), :] = o[r * 8:(r + 1) * 8]

    bias4 = _attn_bias(bias_ref, 1, False)
    first = [tuple(gather4(c, rho, 0, 32) for c in range(3)) + (bias4,) for rho in range(4)]
    first.append(tuple(gather1(qkv_ref, 0, 8, c) for c in range(3)) + (_attn_bias(bias_ref, 0, False),))
    res = _attn_blocks(first)
    for rho in range(4):
        store4(rho, 0, *res[rho])
    merge_and_store(0, *res[4])

    n_blocks1 = RES_LEN // 8 - 1

    def body1(j, c):
        u0s = [(1 + j * ATTN_GROUP1 + a) * 8 for a in range(ATTN_GROUP1)]
        bias = _attn_bias(bias_ref, 0, True)
        res = _attn_blocks([(gather1(qkv_ref, u0, 8, 0), gather1(qkv_ref, u0 - 8, 16, 1),
                             gather1(qkv_ref, u0 - 8, 16, 2), bias) for u0 in u0s])
        for u0, (o, l) in zip(u0s, res):
            merge_and_store(u0, o, l)
        return c
    for j in range(n_blocks1 // ATTN_GROUP1):
        body1(j, 0)

    def emit(r, c):
        o_ref[r] = fin_ref[r].astype(o_ref.dtype)
        return c
    lax.fori_loop(0, N_RES, emit, 0)


def attention(qkv):
    batch = qkv.shape[0]
    scratch = pltpu.VMEM((N_RES, RES_LEN, LANES), F32)
    return pl.pallas_call(
        _attn_kernel,
        grid=(batch, N_LANE_TILES),
        in_specs=[
            pl.BlockSpec((None, N_RES, None, 3, RES_LEN, LANES), lambda b, g: (b, 0, g, 0, 0, 0)),
            pl.BlockSpec((len(DILATIONS), HEADS_PER_TILE, ATTN_BLOCK, 3 * ATTN_BLOCK),
                         lambda b, g: (0, g, 0, 0)),
        ],
        out_specs=pl.BlockSpec((None, N_RES, None, RES_LEN, LANES), lambda b, g: (b, 0, g, 0, 0)),
        out_shape=jax.ShapeDtypeStruct((batch, N_RES, N_LANE_TILES, RES_LEN, LANES), BF16),
        scratch_shapes=[scratch] * 5,
        compiler_params=_cparams("arbitrary", "arbitrary"),
        name="dilated_attention",
    )(qkv, _attn_bias_tables())


def _split2(x):
    a = x.astype(BF16)
    b = (x - a.astype(F32)).astype(BF16)
    return a, b


def _hgrn_kernel(q_ref, f_ref, i_ref, g_ref, gamma_ref, ng_ref, tri_ref, y_ref):
    sup, ch = HGRN_SUPER, HGRN_CHUNK
    n_ch = sup // ch
    gam = gamma_ref[...]
    gmx = jnp.max(gam, axis=0, keepdims=True)
    ge = jnp.exp(gam - gmx)
    lb = ge[0:1] / jnp.sum(ge, axis=0, keepdims=True)
    ng = ng_ref[...]
    tri = tri_ref[...]
    ri = lax.broadcasted_iota(I32, (2 * sup, sup), 0) % sup
    ci = lax.broadcasted_iota(I32, (2 * sup, sup), 1)
    causal_bd2 = (ri // ch == ci // ch) & (ri >= ci)
    head0 = lax.broadcasted_iota(I32, (sup, LANES), 1) < HEAD_DIM
    row_chunk = lax.broadcasted_iota(I32, (sup, LANES), 0) // ch
    same_head = (lax.broadcasted_iota(I32, (LANES, LANES), 0) // HEAD_DIM
                 == lax.broadcasted_iota(I32, (LANES, LANES), 1) // HEAD_DIM)

    def sigmoid(x):
        return 1.0 / (1.0 + jnp.exp(-x))

    def dot_nt(a, b):
        return lax.dot_general(a, b, (((1,), (1,)), ((), ())), preferred_element_type=F32)

    def dot_tn(a, b):
        return lax.dot_general(a, b, (((0,), (0,)), ((), ())), preferred_element_type=F32)

    def expand(x):
        zero = jnp.zeros_like(x)
        return jnp.concatenate([jnp.where(row_chunk == c, x, zero) for c in range(n_ch)], axis=1)

    def body(j, state):
        grp = range(HGRN_GROUP)
        rows = [pl.ds((j * HGRN_GROUP + a) * sup, sup) for a in grp]
        fr = [f_ref[r, :].astype(F32) for r in rows]
        t = [jnp.exp(-jnp.abs(x)) for x in fr]
        rcp = [1.0 / (1.0 + x) for x in t]
        sig = [jnp.where(f >= 0, r, x * r) for f, r, x in zip(fr, rcp, t)]
        nsig = [jnp.where(f >= 0, x * r, r) for f, r, x in zip(fr, rcp, t)]
        logf = [jnp.log(lb + (1.0 - lb) * x) for x in sig]
        key = [(1.0 - lb) * x for x in nsig]
        parts = [_split2(x) for x in logf]
        b = [sum(jnp.dot(tri, p, preferred_element_type=F32) for p in ps) for ps in parts]
        totals = [[x[(c + 1) * ch - 1:(c + 1) * ch, :] for c in range(n_ch)] for x in b]
        b_last = [jnp.concatenate([jnp.broadcast_to(t_, (ch, LANES)) for t_ in ts], axis=0) for ts in totals]
        pad = jnp.zeros((LANES - n_ch, LANES), F32)
        decay_t = [jnp.exp(jnp.transpose(jnp.concatenate(ts + [pad], axis=0))) for ts in totals]
        qv = [q_ref[r, :].astype(F32) for r in rows]
        q_in = [(x * sigmoid(x) * jnp.exp(bb)).astype(BF16) for x, bb in zip(qv, b)]
        k_in = [(k * jnp.exp(-bb)).astype(BF16) for k, bb in zip(key, b)]
        k_end = [(k * jnp.exp(bl - bb)).astype(BF16) for k, bl, bb in zip(key, b_last, b)]
        vv = [i_ref[r, :] for r in rows]
        q2 = [jnp.concatenate([jnp.where(head0, x, jnp.zeros_like(x)), jnp.where(head0, jnp.zeros_like(x), x)],
                              axis=0) for x in q_in]
        att = [jnp.where(causal_bd2, dot_nt(x, k), 0.0).astype(BF16) for x, k in zip(q2, k_in)]
        intra = [jnp.dot(a_, v, preferred_element_type=F32) for a_, v in zip(att, vv)]
        o_intra = [jnp.where(head0, x[:sup], x[sup:]) for x in intra]
        ds_all = [dot_tn(k, expand(v)) for k, v in zip(k_end, vv)]
        s_prev = []
        for a in grp:
            for c in range(n_ch):
                s_prev.append(state.astype(BF16))
                ds = jnp.where(same_head, ds_all[a][:, c * LANES:(c + 1) * LANES], 0.0)
                state = decay_t[a][:, c:c + 1] * state + ds
        o_inter = [jnp.concatenate(
            [jnp.dot(q_in[a][c * ch:(c + 1) * ch], s_prev[a * n_ch + c], preferred_element_type=F32)
             for c in range(n_ch)], axis=0) for a in grp]
        o = [x + y for x, y in zip(o_intra, o_inter)]
        sq = [x * x for x in o]
        ss0 = [jnp.sum(jnp.where(head0, x, 0.0), axis=-1, keepdims=True) for x in sq]
        ss1 = [jnp.sum(jnp.where(head0, 0.0, x), axis=-1, keepdims=True) for x in sq]
        ms = [jnp.where(head0, x, y) * (1.0 / HEAD_DIM) for x, y in zip(ss0, ss1)]
        gv = [g_ref[r, :].astype(F32) for r in rows]
        for r, x, m_, g_ in zip(rows, o, ms, gv):
            y_ref[r, :] = (x * lax.rsqrt(m_ + EPS) * ng * (g_ * sigmoid(g_))).astype(y_ref.dtype)
        return state

    state = jnp.zeros((LANES, LANES), F32)
    for j in range(SEQ // (sup * HGRN_GROUP)):
        state = body(j, state)


def hgrn(hg3, gamma, norm_g):
    batch = hg3.shape[0]
    r = jnp.arange(HGRN_SUPER)
    same = (r[:, None] // HGRN_CHUNK) == (r[None, :] // HGRN_CHUNK)
    tri = (same & (r[:, None] >= r[None, :])).astype(BF16)
    col = lambda which: (lambda b, g: (b, 0, which * N_LANE_TILES + g))
    const2 = lambda b, g: (0, 0)
    return pl.pallas_call(
        _hgrn_kernel,
        grid=(batch, N_LANE_TILES),
        in_specs=[
            pl.BlockSpec((None, SEQ, LANES), col(0)),
            pl.BlockSpec((None, SEQ, LANES), col(1)),
            pl.BlockSpec((None, SEQ, LANES), col(2)),
            pl.BlockSpec((None, SEQ, LANES), col(3)),
            pl.BlockSpec((2, LANES), lambda b, g: (0, g)),
            pl.BlockSpec((1, LANES), lambda b, g: (0, g)),
            pl.BlockSpec((HGRN_SUPER, HGRN_SUPER), const2),
        ],
        out_specs=pl.BlockSpec((None, SEQ, LANES), lambda b, g: (b, 0, g)),
        out_shape=jax.ShapeDtypeStruct((batch, SEQ, WIDTH), BF16),
        compiler_params=_cparams("arbitrary", "arbitrary"),
        name="hgrn2",
    )(hg3, hg3, hg3, hg3, gamma.astype(F32), norm_g.reshape(1, WIDTH).astype(F32), tri)


def _post_mix_kernel(oa_ref, yh_ref, x_ref, ag_ref, wo_ref, g2_ref, wrh_ref, wrl_ref, br_ref, perm_t_ref,
                     tri_ref, lower_ref,
                     x1_ref, h2_ref, rec_ref, tab_ref, carry_ref, lg_sc):
    i = pl.program_id(0)

    @pl.when(i == 0)
    def _():
        lg_sc[...] = jnp.zeros_like(lg_sc)
    _route_tile(lg_sc[...], (i > 0).astype(F32), tri_ref, lower_ref, rec_ref, tab_ref, carry_ref)

    oa = jnp.concatenate(
        [jnp.concatenate([oa_ref[r, g] for g in range(N_LANE_TILES)], axis=1)
         for r in range(N_RES)], axis=0).astype(F32)
    ms = jnp.mean(oa * oa, axis=-1, keepdims=True)
    ya = (oa * lax.rsqrt(ms + EPS) * ag_ref[...]).astype(BF16)
    ya = jnp.dot(perm_t_ref[...], ya, preferred_element_type=F32).astype(BF16)
    mix = (jnp.dot(ya, wo_ref[:WIDTH, :], preferred_element_type=F32)
           + jnp.dot(yh_ref[...], wo_ref[WIDTH:, :], preferred_element_type=F32))
    x1 = x_ref[...] + mix
    x1_ref[...] = x1
    ms2 = jnp.mean(x1 * x1, axis=-1, keepdims=True)
    h2 = x1 * lax.rsqrt(ms2 + EPS) * g2_ref[...]
    hi = h2.astype(BF16)
    h2_ref[...] = hi
    lo = (h2 - hi.astype(F32)).astype(BF16)
    nt = (((1,), (1,)), ((), ()))
    wrh = wrh_ref[...]
    lg_sc[...] = (lax.dot_general(wrh, hi, nt, preferred_element_type=F32)
                  + lax.dot_general(wrh, lo, nt, preferred_element_type=F32)
                  + lax.dot_general(wrl_ref[...], hi, nt, preferred_element_type=F32)
                  + br_ref[...])


def post_mix(oa, yh, x2, attn_g, w_out_bf16, g2, wr_hi, wr_lo, br):
    n = x2.shape[0]
    tiles_per_b = SEQ // TILE_TOKENS
    assert ROUTE_TM == TILE_TOKENS
    n_tiles = n // TILE_TOKENS
    cur = lambda i: jnp.minimum(i, n_tiles - 1)
    prev = lambda i: jnp.maximum(i - 1, 0)
    row = lambda w: pl.BlockSpec((TILE_TOKENS, w), lambda i: (cur(i), 0))
    const = lambda r, c: pl.BlockSpec((r, c), lambda i: (0, 0))
    return pl.pallas_call(
        _post_mix_kernel,
        grid=(n_tiles + 1,),
        in_specs=[pl.BlockSpec((None, N_RES, N_LANE_TILES, U_PER_TILE, LANES),
                               lambda i: (cur(i) // tiles_per_b, 0, 0, cur(i) % tiles_per_b, 0)),
                  row(WIDTH), row(D_MODEL),
                  const(1, WIDTH), const(2 * WIDTH, D_MODEL), const(1, D_MODEL),
                  const(LANES, D_MODEL), const(LANES, D_MODEL), const(LANES, 1),
                  const(TILE_TOKENS, TILE_TOKENS), const(ROUTE_TM, ROUTE_TM), const(N_EXPERTS, N_EXPERTS)],
        out_specs=[row(D_MODEL), row(D_MODEL),
                   pl.BlockSpec((8, ROUTE_TM), lambda i: (0, prev(i))),
                   pl.BlockSpec((1, 8, LANES), lambda i: (prev(i), 0, 0))],
        out_shape=[jax.ShapeDtypeStruct((n, D_MODEL), F32),
                   jax.ShapeDtypeStruct((n, D_MODEL), BF16),
                   jax.ShapeDtypeStruct((8, n), F32),
                   jax.ShapeDtypeStruct((n_tiles, 8, LANES), F32)],
        scratch_shapes=[pltpu.VMEM((N_EXPERTS, LANES), F32), pltpu.VMEM((LANES, TILE_TOKENS), F32)],
        compiler_params=_cparams("arbitrary"),
        name="post_mix",
    )(oa, yh, x2, attn_g.reshape(1, WIDTH), w_out_bf16, g2.reshape(1, D_MODEL), wr_hi, wr_lo, br,
      _tile_permutation().T, *_route_constants())


ROUTE_TM = 512
EXPERT_ROW0 = 32
R_E1, R_E2, R_SLOT1, R_SLOT2, R_G1, R_G2 = 0, 1, 2, 3, 4, 5
T_CARRY, T_ROWS, T_OFF = 0, 1, 2
SEG_END = 0
SENT_LANE = N_EXPERTS + 1
RUN_PIECE = 64
RUN_GUARD = RUN_PIECE - 8
TOTAL_LANE = N_EXPERTS
RUN_ALIGN = 8
LOCAL_ROWS = 2 * ROUTE_TM + N_EXPERTS * RUN_ALIGN


def _route_tile(lg, live, tri_ref, lower_ref, rec_ref, tab_ref, carry_ref):
    i = pl.program_id(0)

    @pl.when(i == 0)
    def _():
        carry_ref[...] = jnp.zeros_like(carry_ref)

    tm = lg.shape[1]
    sub8 = lax.broadcasted_iota(I32, (8, tm), 0).astype(F32)
    big = 8.0
    gmask = sub8 < N_GROUPS
    gl = jnp.where(gmask, lg[0:8], NEG)
    gmax = jnp.max(gl, axis=0, keepdims=True)
    gsel = jnp.min(jnp.where(gmask & (gl == gmax), sub8, big), axis=0, keepdims=True)
    gsum = jnp.sum(jnp.where(gmask, jnp.exp(gl - gmax), 0.0), axis=0, keepdims=True)
    w_g = 1.0 / gsum
    el = jnp.zeros((EXPERTS_PER_GROUP, tm), F32)
    for g in range(N_GROUPS):
        lo = EXPERT_ROW0 + g * EXPERTS_PER_GROUP
        el = jnp.where(gsel == g, lg[lo:lo + EXPERTS_PER_GROUP], el)
    v1 = jnp.max(el, axis=0, keepdims=True)
    i1 = jnp.min(jnp.where(el == v1, sub8, big), axis=0, keepdims=True)
    el2 = jnp.where(sub8 == i1, NEG, el)
    v2 = jnp.max(el2, axis=0, keepdims=True)
    i2 = jnp.min(jnp.where((el2 == v2) & (sub8 != i1), sub8, big), axis=0, keepdims=True)
    ex = jnp.exp(v2 - v1)
    den = 1.0 / (1.0 + ex)
    g1 = w_g * den
    g2 = w_g * ex * den
    e1 = gsel * EXPERTS_PER_GROUP + i1
    e2 = gsel * EXPERTS_PER_GROUP + i2
    sub_e = lax.broadcasted_iota(I32, (N_EXPERTS, tm), 0).astype(F32)
    oh1 = sub_e == e1
    oh2 = sub_e == e2
    onehot = (oh1 | oh2).astype(BF16)
    before = jnp.dot(onehot, tri_ref[...], preferred_element_type=F32)
    count = jnp.sum(onehot.astype(F32), axis=1, keepdims=True)
    units = jnp.floor((count + (RUN_ALIGN - 1)) * (1.0 / RUN_ALIGN)) * live
    units = jnp.broadcast_to(units, (N_EXPERTS, LANES))
    rows = units * RUN_ALIGN
    off = RUN_ALIGN * jnp.dot(lower_ref[...], units.astype(BF16), preferred_element_type=F32)
    place = off[:, 0:1] + before
    slot1 = jnp.sum(jnp.where(oh1, place, 0.0), axis=0, keepdims=True)
    slot2 = jnp.sum(jnp.where(oh2, place, 0.0), axis=0, keepdims=True)
    total = jnp.sum(rows, axis=0, keepdims=True)
    eye = (lax.broadcasted_iota(I32, (N_EXPERTS, LANES), 0) == lax.broadcasted_iota(I32, (N_EXPERTS, LANES), 1))
    lane1 = lax.broadcasted_iota(I32, (1, LANES), 1)
    sent = RUN_PIECE * jnp.sum(jnp.floor((rows + (RUN_PIECE - 1)) * (1.0 / RUN_PIECE)), axis=0, keepdims=True)
    as_row = lambda col: jnp.where(lane1 == TOTAL_LANE, total, jnp.where(
        lane1 == SENT_LANE, sent, jnp.sum(jnp.where(eye, col, 0.0), axis=0, keepdims=True)))
    sub = lax.broadcasted_iota(I32, (8, LANES), 0)
    tab_ref[0] = jnp.where(sub == T_CARRY, as_row(carry_ref[...]),
                           jnp.where(sub == T_ROWS, as_row(rows), jnp.where(sub == T_OFF, as_row(off), 0.0)))
    carry_ref[...] += rows
    rec = jnp.zeros((8, tm), F32)
    for slot, val in ((R_E1, e1), (R_E2, e2), (R_SLOT1, slot1),
                      (R_SLOT2, slot2), (R_G1, g1), (R_G2, g2)):
        rec = jnp.where(sub8 == slot, val, rec)
    rec_ref[...] = rec


def _route_constants():
    r = jnp.arange(ROUTE_TM)
    tri = (r[:, None] < r[None, :]).astype(BF16)
    e = jnp.arange(N_EXPERTS)
    lower = (e[:, None] > e[None, :]).astype(BF16)
    return tri, lower


P_START, P_ROWS, P_OFF = 0, 1, 2


def _plan_kernel(carry_ref, rows_ref, off_ref, upper_ref, plan_ref, seg_ref):
    n_tiles = carry_ref.shape[0]
    lane = lax.broadcasted_iota(I32, (1, LANES), 1)
    last = pl.ds(n_tiles - 1, 1)
    totals = jnp.where(lane < N_EXPERTS, carry_ref[last, :] + rows_ref[last, :], 0.0)
    guard = jnp.where(totals > 0, float(RUN_GUARD), 0.0)
    blocks = jnp.floor((totals + guard + (FFN_BLOCK - 1)) * (1.0 / FFN_BLOCK))
    starts = FFN_BLOCK * jnp.dot(jnp.broadcast_to(blocks, (8, LANES)).astype(BF16), upper_ref[...],
                                 preferred_element_type=F32)[0:1]
    seg_ref[...] = jnp.broadcast_to(starts + blocks * FFN_BLOCK, seg_ref.shape)
    plan_ref[P_START] = (carry_ref[...] + starts).astype(I32)
    plan_ref[P_ROWS] = rows_ref[...].astype(I32)
    plan_ref[P_OFF] = off_ref[...].astype(I32)


def run_plan(tab):
    n_tiles = tab.shape[0]
    ln = jnp.arange(LANES)
    upper = (ln[:, None] < ln[None, :]).astype(BF16)
    return pl.pallas_call(
        _plan_kernel,
        out_shape=[jax.ShapeDtypeStruct((3, n_tiles, LANES), I32),
                   jax.ShapeDtypeStruct((8, LANES), F32)],
        compiler_params=pltpu.CompilerParams(vmem_limit_bytes=VMEM_LIMIT),
        name="run_plan",
    )(tab[:, T_CARRY], tab[:, T_ROWS], tab[:, T_OFF], upper)


SMALL_PIECES = (32, 16, 8)
SENT_PIECES = (1024, 512, 256, 128, 64)
TOTAL_PIECES = (1024, 512, 256, 128, 64, 32, 16, 8)
SORT_CHUNK = 256


def _for_each_run_piece(plan_ref, tile, n_tiles, fn):
    plane = n_tiles * LANES

    def per_expert(e, c):
        idx = tile * LANES + e
        start = plan_ref[P_START * plane + idx]
        rows = plan_ref[P_ROWS * plane + idx]
        off = plan_ref[P_OFF * plane + idx]

        def big(j, c2):
            fn(pl.multiple_of(off + j * RUN_PIECE, RUN_ALIGN), pl.multiple_of(start + j * RUN_PIECE, RUN_ALIGN),
               RUN_PIECE)
            return c2
        lax.fori_loop(0, lax.shift_right_logical(rows, 6), big, 0)
        for size in SMALL_PIECES:
            done = jnp.bitwise_and(rows, -2 * size)

            @pl.when(jnp.bitwise_and(rows, size) != 0)
            def _():
                fn(pl.multiple_of(off + done, RUN_ALIGN), pl.multiple_of(start + done, RUN_ALIGN), size)
        return c
    lax.fori_loop(0, N_EXPERTS, per_expert, 0)


def _for_each_total_piece(plan_ref, tile, n_tiles, fn):
    total = plan_ref[P_OFF * n_tiles * LANES + tile * LANES + TOTAL_LANE]
    for size in TOTAL_PIECES:
        @pl.when(jnp.bitwise_and(total, size) != 0)
        def _():
            fn(size)


def _dispatch_kernel(plan_ref, ends_ref, nv_ref, h2_ref, rec_ref, xs_ref, zero_buf, stage, sem, zsem):
    i = pl.program_id(0)
    n_steps = pl.num_programs(0)
    n_blocks = xs_ref.shape[0] // FFN_BLOCK

    @pl.when(i == 0)
    def _():
        zero_buf[...] = jnp.zeros_like(zero_buf)
        for s in range(2):
            stage[s, LOCAL_ROWS:, :] = jnp.zeros((RUN_PIECE, PACKED_W), U32)

        def zero_copy(row0):
            return pltpu.make_async_copy(zero_buf, xs_ref.at[pl.ds(row0, FFN_BLOCK)], zsem)

        def seg_end(e):
            return ends_ref[e], ends_ref[e] > jnp.where(e > 0, ends_ref[jnp.maximum(e - 1, 0)], 0)

        def start_e(e, c):
            end, nonempty = seg_end(e)

            @pl.when(nonempty)
            def _():
                zero_copy(pl.multiple_of(end - FFN_BLOCK, FFN_BLOCK)).start()
            return c
        lax.fori_loop(0, N_EXPERTS, start_e, 0)

        def start_b(blk, c):
            zero_copy(pl.multiple_of(blk * FFN_BLOCK, FFN_BLOCK)).start()
            return c
        lax.fori_loop(nv_ref[0], n_blocks, start_b, 0)

        def wait_e(e, c):
            _, nonempty = seg_end(e)

            @pl.when(nonempty)
            def _():
                zero_copy(0).wait()
            return c
        lax.fori_loop(0, N_EXPERTS, wait_e, 0)

        def wait_b(blk, c):
            zero_copy(0).wait()
            return c
        lax.fori_loop(nv_ref[0], n_blocks, wait_b, 0)

    slot = i % 2
    s1 = rec_ref[R_SLOT1:R_SLOT1 + 1, :]
    s2 = rec_ref[R_SLOT2:R_SLOT2 + 1, :]
    half = PACKED_W
    for c in range(LOCAL_ROWS // SORT_CHUNK):
        rowf = (lax.broadcasted_iota(I32, (SORT_CHUNK, ROUTE_TM), 0) + c * SORT_CHUNK).astype(F32)
        sel = ((rowf == s1) | (rowf == s2)).astype(BF16)
        pick = lambda cols: jnp.dot(sel, h2_ref[:, cols], preferred_element_type=F32)
        stage[slot, c * SORT_CHUNK:(c + 1) * SORT_CHUNK, :] = _pack_pair(pick(slice(0, half)),
                                                                         pick(slice(half, 2 * half)))

    plane = n_steps * LANES

    def wait_tile(tile, s):
        def wait_rows(size):
            pltpu.make_async_copy(stage.at[s, pl.ds(0, size)], xs_ref.at[pl.ds(0, size)], sem.at[s]).wait()
        sent = plan_ref[P_OFF * plane + tile * LANES + SENT_LANE]

        def wait_big(k, c):
            wait_rows(SENT_PIECES[0])
            return c
        lax.fori_loop(0, lax.shift_right_logical(sent, 10), wait_big, 0)
        for size in SENT_PIECES[1:]:
            @pl.when(jnp.bitwise_and(sent, size) != 0)
            def _():
                wait_rows(size)

    @pl.when(i > 0)
    def _():
        wait_tile(i - 1, 1 - slot)

    def per_expert(e, c):
        idx = i * LANES + e
        start = plan_ref[P_START * plane + idx]
        rows = plan_ref[P_ROWS * plane + idx]
        off = plan_ref[P_OFF * plane + idx]

        def piece(j, c2):
            pltpu.make_async_copy(stage.at[slot, pl.ds(pl.multiple_of(off + j * RUN_PIECE, RUN_ALIGN), RUN_PIECE)],
                                  xs_ref.at[pl.ds(pl.multiple_of(start + j * RUN_PIECE, RUN_ALIGN), RUN_PIECE)],
                                  sem.at[slot]).start()
            return c2
        lax.fori_loop(0, lax.shift_right_logical(rows + (RUN_PIECE - 1), 6), piece, 0)
        return c
    lax.fori_loop(0, N_EXPERTS, per_expert, 0)

    @pl.when(i + 1 == n_steps)
    def _():
        wait_tile(i, slot)


def dispatch(h2, rec, plan, seg_ends, n_valid, n_rows):
    n = h2.shape[0]
    return pl.pallas_call(
        _dispatch_kernel,
        grid_spec=pltpu.PrefetchScalarGridSpec(
            num_scalar_prefetch=3,
            grid=(n // ROUTE_TM,),
            in_specs=[pl.BlockSpec((ROUTE_TM, D_MODEL), lambda i, *_: (i, 0)),
                      pl.BlockSpec((8, ROUTE_TM), lambda i, *_: (0, i))],
            out_specs=pl.BlockSpec(memory_space=pl.ANY),
            scratch_shapes=[pltpu.VMEM((FFN_BLOCK, PACKED_W), U32),
                            pltpu.VMEM((2, LOCAL_ROWS + RUN_PIECE, PACKED_W), U32),
                            pltpu.SemaphoreType.DMA((2,)),
                            pltpu.SemaphoreType.DMA(())],
        ),
        out_shape=jax.ShapeDtypeStruct((n_rows, PACKED_W), U32),
        compiler_params=_cparams("arbitrary"),
        name="dispatch",
    )(plan, seg_ends, n_valid, h2, rec)


def _ffn_kernel(be_ref, nv_ref, xs_ref, wg_ref, wu_ref, wd_ref, ys_ref, wg_sc, wu_sc, wd_sc):
    i = pl.program_id(0)
    prev = be_ref[jnp.maximum(i - 1, 0)]
    fresh = (i == 0) | (be_ref[i] != prev)

    @pl.when(fresh)
    def _():
        wg_sc[...] = wg_ref[0].astype(BF16)
        wu_sc[...] = wu_ref[0].astype(BF16)
        wd_sc[...] = wd_ref[0].astype(BF16)

    @pl.when(i < nv_ref[0])
    def _():
        xa, xb = (t.astype(BF16) for t in _unpack_halves(xs_ref[...]))
        half = PACKED_W
        gate = (jnp.dot(xa, wg_sc[:half, :], preferred_element_type=F32)
                + jnp.dot(xb, wg_sc[half:, :], preferred_element_type=F32))
        up = (jnp.dot(xa, wu_sc[:half, :], preferred_element_type=F32)
              + jnp.dot(xb, wu_sc[half:, :], preferred_element_type=F32))
        hid = (gate * (1.0 / (1.0 + jnp.exp(-gate))) * up).astype(BF16)
        ys_ref[...] = _pack_halves(jnp.dot(hid, wd_sc[...], preferred_element_type=F32))

    @pl.when(i >= nv_ref[0])
    def _():
        ys_ref[...] = jnp.zeros_like(ys_ref)


def expert_ffn(xs, block_expert, n_valid, w_gate, w_up, w_down):
    n_rows = xs.shape[0]
    n_blocks = n_rows // FFN_BLOCK
    wmap = lambda i, be, nv: (be[i], 0, 0)
    return pl.pallas_call(
        _ffn_kernel,
        grid_spec=pltpu.PrefetchScalarGridSpec(
            num_scalar_prefetch=2,
            grid=(n_blocks,),
            in_specs=[pl.BlockSpec((FFN_BLOCK, PACKED_W), lambda i, be, nv: (jnp.minimum(i, nv[0] - 1), 0)),
                      pl.BlockSpec((1, D_MODEL, D_EXPERT), wmap),
                      pl.BlockSpec((1, D_MODEL, D_EXPERT), wmap),
                      pl.BlockSpec((1, D_EXPERT, D_MODEL), wmap)],
            out_specs=pl.BlockSpec((FFN_BLOCK, PACKED_W), lambda i, be, nv: (i, 0)),
            scratch_shapes=[pltpu.VMEM((D_MODEL, D_EXPERT), BF16),
                            pltpu.VMEM((D_MODEL, D_EXPERT), BF16),
                            pltpu.VMEM((D_EXPERT, D_MODEL), BF16)],
        ),
        out_shape=jax.ShapeDtypeStruct((n_rows, PACKED_W), U32),
        compiler_params=_cparams("arbitrary"),
        name="expert_ffn",
    )(block_expert, n_valid, xs, w_gate, w_up, w_down)


def _combine_kernel(plan_ref, x1_ref, rec_ref, gf_ref, ys_ref, out_ref, buf, sem):
    i = pl.program_id(0)
    n_steps = pl.num_programs(0)
    slot = i % 2

    def fetch_tile(tile, s):
        def start_piece(lrow, grow, size):
            pltpu.make_async_copy(ys_ref.at[pl.ds(grow, size)], buf.at[s, pl.ds(lrow, size)], sem.at[s]).start()
        _for_each_run_piece(plan_ref, tile, n_steps, start_piece)

    @pl.when(i == 0)
    def _():
        buf[...] = jnp.zeros_like(buf)
        fetch_tile(0, 0)

    @pl.when(i + 1 < n_steps)
    def _():
        fetch_tile(i + 1, 1 - slot)

    def wait_piece(size):
        pltpu.make_async_copy(ys_ref.at[pl.ds(0, size)], buf.at[slot, pl.ds(0, size)], sem.at[slot]).wait()
    _for_each_total_piece(plan_ref, i, n_steps, wait_piece)

    rowf = lax.broadcasted_iota(I32, (LOCAL_ROWS, ROUTE_TM), 0).astype(F32)
    sel = (jnp.where(rowf == rec_ref[R_SLOT1:R_SLOT1 + 1, :], rec_ref[R_G1:R_G1 + 1, :], 0.0)
           + jnp.where(rowf == rec_ref[R_SLOT2:R_SLOT2 + 1, :], rec_ref[R_G2:R_G2 + 1, :], 0.0)).astype(BF16)
    ya, yb = (t.astype(BF16) for t in _unpack_halves(buf[slot]))
    half = PACKED_W
    tn = (((0,), (0,)), ((), ()))
    xa = x1_ref[:, :half] + lax.dot_general(sel, ya, tn, preferred_element_type=F32)
    xb = x1_ref[:, half:] + lax.dot_general(sel, yb, tn, preferred_element_type=F32)
    ms = (jnp.sum(xa * xa, axis=-1, keepdims=True) + jnp.sum(xb * xb, axis=-1, keepdims=True)) * (1.0 / D_MODEL)
    scale = lax.rsqrt(ms + EPS)
    out_ref[:, :half] = xa * scale * gf_ref[:, :half]
    out_ref[:, half:] = xb * scale * gf_ref[:, half:]


def combine(x1, rec, norm_f_g, ys, plan):
    n = x1.shape[0]
    tm = ROUTE_TM
    return pl.pallas_call(
        _combine_kernel,
        grid_spec=pltpu.PrefetchScalarGridSpec(
            num_scalar_prefetch=1,
            grid=(n // tm,),
            in_specs=[pl.BlockSpec((tm, D_MODEL), lambda i, p: (i, 0)),
                      pl.BlockSpec((8, tm), lambda i, p: (0, i)),
                      pl.BlockSpec((1, D_MODEL), lambda i, p: (0, 0)),
                      pl.BlockSpec(memory_space=pl.ANY)],
            out_specs=pl.BlockSpec((tm, D_MODEL), lambda i, p: (i, 0)),
            scratch_shapes=[pltpu.VMEM((2, LOCAL_ROWS, PACKED_W), U32),
                            pltpu.SemaphoreType.DMA((2,))],
        ),
        out_shape=jax.ShapeDtypeStruct((n, D_MODEL), F32),
        compiler_params=_cparams("arbitrary"),
        name="combine",
    )(plan, x1, rec, norm_f_g.reshape(1, D_MODEL), ys)


def _router_weights(w_group, b_group, w_router, b_router):
    w = jnp.zeros((LANES, D_MODEL), F32)
    w = w.at[:N_GROUPS].set(w_group.T).at[EXPERT_ROW0:EXPERT_ROW0 + N_EXPERTS].set(w_router.T)
    b = jnp.zeros((LANES, 1), F32)
    b = b.at[:N_GROUPS, 0].set(b_group).at[EXPERT_ROW0:EXPERT_ROW0 + N_EXPERTS, 0].set(b_router)
    hi = w.astype(BF16)
    lo = (w - hi.astype(F32)).astype(BF16)
    return hi, lo, b


def _sorted_rows_bound(n_tokens):
    worst = (2 * n_tokens + (n_tokens // ROUTE_TM) * N_EXPERTS * (RUN_ALIGN - 1)
             + N_EXPERTS * (RUN_GUARD + FFN_BLOCK - 1))
    return -(-worst // FFN_BLOCK) * FFN_BLOCK


def _block_plan(seg, n_rows):
    seg_ends = seg[SEG_END, :N_EXPERTS].astype(I32)
    n_blocks = n_rows // FFN_BLOCK
    blk_start = jnp.arange(n_blocks, dtype=I32) * FFN_BLOCK
    block_expert = jnp.minimum(jnp.sum((seg_ends[None, :] <= blk_start[:, None]).astype(I32), axis=1),
                               N_EXPERTS - 1)
    n_valid = seg_ends[-1:] // FFN_BLOCK
    return seg_ends, block_expert, n_valid


def kernel(x, norm1_g, w_in, attn_norm_g, hgrn_gamma, hgrn_norm_g, w_out, norm2_g, w_group, b_group,
           w_router, b_router, w_gate, w_up, w_down, norm_f_g):
    batch, seq, d = x.shape
    assert seq == SEQ and d == D_MODEL and norm1_g.shape[0] == 1
    n = batch * seq
    x2 = x.reshape(n, d)
    qkv, hg = in_proj(x2, norm1_g[0], w_in[0].astype(BF16), batch)
    oa = attention(qkv)
    yh = hgrn(hg.reshape(batch, seq, HG_W), hgrn_gamma, hgrn_norm_g[0]).reshape(n, WIDTH)
    wr_hi, wr_lo, br = _router_weights(w_group[0], b_group[0], w_router[0], b_router[0])
    x1, h2, rec, tab = post_mix(oa, yh, x2, attn_norm_g[0], w_out[0].astype(BF16), norm2_g[0],
                                wr_hi, wr_lo, br)
    plan, seg = run_plan(tab)
    plan = plan.reshape(-1)
    n_rows = _sorted_rows_bound(n)
    seg_ends, block_expert, n_valid = _block_plan(seg, n_rows)
    xs = dispatch(h2, rec, plan, seg_ends, n_valid, n_rows)
    ys = expert_ffn(xs, block_expert, n_valid, w_gate[0], w_up[0], w_down[0])
    return combine(x1, rec, norm_f_g, ys, plan).reshape(batch, seq, d)
```

```python
import jax
import jax.numpy as jnp
from jax import lax
from jax.experimental import pallas as pl
from jax.experimental.pallas import tpu as pltpu

F32 = jnp.float32
BF16 = jnp.bfloat16
I32 = jnp.int32
U32 = jnp.uint32

D_MODEL = 1024
HEAD_DIM = 64
N_HEADS = 8
WIDTH = N_HEADS * HEAD_DIM
QKV_W = 3 * WIDTH
HG_W = 4 * WIDTH
SEQ = 2048
ATTN_BLOCK = 128
DILATIONS = (1, 4, 16)
N_RES = 16
RES_LEN = SEQ // N_RES
TILE_TOKENS = 512
U_PER_TILE = TILE_TOKENS // N_RES
HGRN_CHUNK = 32
HGRN_SUPER = 256
HGRN_GROUP = 8
N_GROUPS = 4
EXPERTS_PER_GROUP = 8
N_EXPERTS = 32
D_EXPERT = 512
FFN_BLOCK = 512
PACKED_W = D_MODEL // 2
EPS = 1e-6
NEG = -1e30
LOG2E = 1.4426950408889634
LANES = 128
VMEM_LIMIT = 56 * 1024 * 1024


def _cparams(*sem):
    return pltpu.CompilerParams(dimension_semantics=sem, vmem_limit_bytes=VMEM_LIMIT)


def _pack_pair(a, b):
    hi = pltpu.bitcast(a.astype(BF16).astype(F32), U32)
    lo = pltpu.bitcast(b.astype(BF16).astype(F32), U32)
    return hi | (lo >> 16)


def _pack_halves(x):
    w = x.shape[1] // 2
    return _pack_pair(x[:, :w], x[:, w:])


def _unpack_halves(p):
    return pltpu.bitcast(p & jnp.uint32(0xFFFF0000), F32), pltpu.bitcast(p << 16, F32)


N_LANE_TILES = WIDTH // LANES


def _tile_permutation():
    i = jnp.arange(TILE_TOKENS)
    src = N_RES * (i % U_PER_TILE) + i // U_PER_TILE
    return (src[:, None] == jnp.arange(TILE_TOKENS)[None, :]).astype(BF16)


def _in_proj_kernel(x_ref, g_ref, w_ref, x4_ref, qkv_ref, hg_ref, xp, sem):
    i = pl.program_id(0)
    n_steps = pl.num_programs(0)
    tiles_per_b = SEQ // TILE_TOKENS
    slot = i % 2

    def fetch(tile, s):
        b = tile // tiles_per_b
        u0 = (tile % tiles_per_b) * U_PER_TILE
        return [pltpu.make_async_copy(x4_ref.at[b, pl.ds(u0, U_PER_TILE), r, :],
                                      xp.at[s, pl.ds(r * U_PER_TILE, U_PER_TILE), :], sem.at[s])
                for r in range(N_RES)]

    @pl.when(i == 0)
    def _():
        for cp in fetch(0, 0):
            cp.start()

    @pl.when(i + 1 < n_steps)
    def _():
        for cp in fetch(i + 1, 1 - slot):
            cp.start()

    def normed(x):
        ms = jnp.mean(x * x, axis=-1, keepdims=True)
        return (x * lax.rsqrt(ms + EPS) * g_ref[...]).astype(BF16)

    h = normed(x_ref[...])
    for j in range(HG_W // WIDTH):
        wsl = slice(QKV_W + j * WIDTH, QKV_W + (j + 1) * WIDTH)
        hg_ref[:, j * WIDTH:(j + 1) * WIDTH] = jnp.dot(h, w_ref[:, wsl], preferred_element_type=F32).astype(BF16)
    for cp in fetch(i, slot):
        cp.wait()
    hp = normed(xp[slot])
    for c in range(QKV_W // WIDTH):
        res = jnp.dot(hp, w_ref[:, c * WIDTH:(c + 1) * WIDTH], preferred_element_type=F32)
        for r in range(N_RES):
            for g in range(N_LANE_TILES):
                qkv_ref[r, g, c] = res[r * U_PER_TILE:(r + 1) * U_PER_TILE, g * LANES:(g + 1) * LANES]


def in_proj(x2, g, w_bf16, batch):
    n = x2.shape[0]
    tiles_per_b = SEQ // TILE_TOKENS
    x4 = x2.reshape(batch, RES_LEN, N_RES, D_MODEL)
    return pl.pallas_call(
        _in_proj_kernel,
        grid=(n // TILE_TOKENS,),
        in_specs=[
            pl.BlockSpec((TILE_TOKENS, D_MODEL), lambda i: (i, 0)),
            pl.BlockSpec((1, D_MODEL), lambda i: (0, 0)),
            pl.BlockSpec((D_MODEL, QKV_W + HG_W), lambda i: (0, 0)),
            pl.BlockSpec(memory_space=pl.ANY),
        ],
        out_specs=[
            pl.BlockSpec((None, N_RES, N_LANE_TILES, 3, U_PER_TILE, LANES),
                         lambda i: (i // tiles_per_b, 0, 0, 0, i % tiles_per_b, 0)),
            pl.BlockSpec((TILE_TOKENS, HG_W), lambda i: (i, 0)),
        ],
        out_shape=[
            jax.ShapeDtypeStruct((batch, N_RES, N_LANE_TILES, 3, RES_LEN, LANES), F32),
            jax.ShapeDtypeStruct((n, HG_W), BF16),
        ],
        scratch_shapes=[pltpu.VMEM((2, TILE_TOKENS, D_MODEL), F32), pltpu.SemaphoreType.DMA((2,))],
        compiler_params=_cparams("arbitrary"),
        name="in_proj",
    )(x2, g.reshape(1, D_MODEL), w_bf16, x4)


ATTN_GROUP16 = 16
ATTN_GROUP1 = 5
HEADS_PER_TILE = LANES // HEAD_DIM


def _attn_bias_tables():
    slopes = 2.0 ** (-8.0 * jnp.arange(1, N_HEADS + 1, dtype=F32) / N_HEADS)

    def table(qpos, kpos, dil):
        sd = qpos[:, None] - kpos[None, :]
        b = -slopes[:, None, None] * (sd * dil).astype(F32)[None] * LOG2E
        return jnp.where(((sd >= 0) & (sd <= ATTN_BLOCK))[None], b, NEG)

    q = jnp.arange(ATTN_BLOCK)
    k = jnp.arange(2 * ATTN_BLOCK)
    q1 = 16 * (q % 8) + q // 8 + ATTN_BLOCK
    k1 = 16 * (k % 16) + k // 16
    q4 = 4 * (q % 32) + q // 32 + ATTN_BLOCK
    k4 = 4 * (k % 64) + k // 64
    q16 = q + ATTN_BLOCK
    none = jnp.full((N_HEADS, ATTN_BLOCK, 2 * ATTN_BLOCK), NEG, F32)
    tabs = [jnp.concatenate([table(q1, k1, 1), table(q1, q1, 1)], axis=-1),
            jnp.concatenate([table(q4, k4, 4), table(q4, q4, 4)], axis=-1),
            jnp.concatenate([none, table(q16, q16, 16)], axis=-1)]
    return jnp.stack(tabs, axis=0)


def _attn_bias(bias_ref, d_idx, with_prev):
    ksl = slice(0, 2 * ATTN_BLOCK) if with_prev else slice(2 * ATTN_BLOCK, 3 * ATTN_BLOCK)
    return jnp.concatenate([bias_ref[d_idx, h, :, ksl] for h in range(HEADS_PER_TILE)], axis=0)


def _attn_blocks(blocks):
    nb = ATTN_BLOCK
    lane = lax.broadcasted_iota(I32, (nb, LANES), 1)
    head0 = lane < HEAD_DIM
    biases = [blk[3] for blk in blocks]
    blocks = [blk[:3] for blk in blocks]
    qs = [q * (HEAD_DIM ** -0.5 * LOG2E) for q, _, _ in blocks]
    qq = [jnp.concatenate([jnp.where(head0, q, 0.0), jnp.where(head0, 0.0, q)], axis=0).astype(BF16)
          for q in qs]
    ks = [k.astype(BF16) for _, k, _ in blocks]
    vs = [v.astype(BF16) for _, _, v in blocks]
    v_ones = [jnp.concatenate([v, jnp.ones_like(v)], axis=-1) for v in vs]
    s = [lax.dot_general(q2, k, (((1,), (1,)), ((), ())), preferred_element_type=F32) + bias
         for q2, k, bias in zip(qq, ks, biases)]
    m = [jnp.max(x, axis=-1, keepdims=True) for x in s]
    p = [jnp.exp2(x - mx).astype(BF16) for x, mx in zip(s, m)]
    acc = [jnp.dot(px, vo, preferred_element_type=F32) for px, vo in zip(p, v_ones)]
    head0_wide = jnp.concatenate([head0, head0], axis=-1)
    outs = []
    for a, mx in zip(acc, m):
        ol = jnp.where(head0_wide, a[:nb], a[nb:])
        l = ol[:, LANES:]
        outs.append((ol[:, :LANES] * (1.0 / l), jnp.where(head0, mx[:nb], mx[nb:]) + jnp.log2(l)))
    return outs


def _attn_kernel(qkv_ref, bias_ref, o_ref, o4_ref, l4_ref, o16_ref, l16_ref, fin_ref):
    grp = ATTN_GROUP16

    def body16(j, c):
        rs = [j * grp + a for a in range(grp)]
        bias = _attn_bias(bias_ref, 2, False)
        res = _attn_blocks([(qkv_ref[r, 0], qkv_ref[r, 1], qkv_ref[r, 2], bias) for r in rs])
        for r, (o, l) in zip(rs, res):
            o16_ref[r] = o
            l16_ref[r] = l
        return c
    for j in range(N_RES // grp):
        body16(j, 0)

    def gather4(c, rho, u0, nu):
        return jnp.concatenate([qkv_ref[rho + 4 * a, c, pl.ds(u0, nu), :] for a in range(4)], axis=0)

    def store4(rho, u0, o, l):
        for a in range(4):
            o4_ref[rho + 4 * a, pl.ds(u0, 32), :] = o[a * 32:(a + 1) * 32]
            l4_ref[rho + 4 * a, pl.ds(u0, 32), :] = l[a * 32:(a + 1) * 32]

    def body4(j, c):
        bias = _attn_bias(bias_ref, 1, True)
        todo = [(2 * j + a, 32 * n) for a in range(2) for n in range(1, 4)]
        res = _attn_blocks([(gather4(0, rho, u0, 32), gather4(1, rho, u0 - 32, 64),
                             gather4(2, rho, u0 - 32, 64), bias) for rho, u0 in todo])
        for (rho, u0), (o, l) in zip(todo, res):
            store4(rho, u0, o, l)
        return c
    for j in range(2):
        body4(j, 0)

    def gather1(ref, u0, nu, *lead):
        return jnp.concatenate([ref[(r,) + lead + (pl.ds(u0, nu), slice(None))] for r in range(N_RES)],
                               axis=0)

    def merge_and_store(u0, o1, l1):
        l4 = gather1(l4_ref, u0, 8)
        l16 = gather1(l16_ref, u0, 8)
        mx = jnp.maximum(jnp.maximum(l1, l4), l16)
        e1 = jnp.exp2(l1 - mx)
        e4 = jnp.exp2(l4 - mx)
        e16 = jnp.exp2(l16 - mx)
        inv = 1.0 / (e1 + e4 + e16)
        o = (e1 * inv) * o1 + (e4 * inv) * gather1(o4_ref, u0, 8) + (e16 * inv) * gather1(o16_ref, u0, 8)
        for r in range(N_RES):
            fin_ref[r, pl.ds(u0, 8), :] = o[r * 8:(r + 1) * 8]

    bias4 = _attn_bias(bias_ref, 1, False)
    first = [tuple(gather4(c, rho, 0, 32) for c in range(3)) + (bias4,) for rho in range(4)]
    first.append(tuple(gather1(qkv_ref, 0, 8, c) for c in range(3)) + (_attn_bias(bias_ref, 0, False),))
    res = _attn_blocks(first)
    for rho in range(4):
        store4(rho, 0, *res[rho])
    merge_and_store(0, *res[4])

    n_blocks1 = RES_LEN // 8 - 1

    def body1(j, c):
        u0s = [(1 + j * ATTN_GROUP1 + a) * 8 for a in range(ATTN_GROUP1)]
        bias = _attn_bias(bias_ref, 0, True)
        res = _attn_blocks([(gather1(qkv_ref, u0, 8, 0), gather1(qkv_ref, u0 - 8, 16, 1),
                             gather1(qkv_ref, u0 - 8, 16, 2), bias) for u0 in u0s])
        for u0, (o, l) in zip(u0s, res):
            merge_and_store(u0, o, l)
        return c
    for j in range(n_blocks1 // ATTN_GROUP1):
        body1(j, 0)

    def emit(r, c):
        o_ref[r] = fin_ref[r].astype(o_ref.dtype)
        return c
    lax.fori_loop(0, N_RES, emit, 0)


def attention(qkv):
    batch = qkv.shape[0]
    scratch = pltpu.VMEM((N_RES, RES_LEN, LANES), F32)
    return pl.pallas_call(
        _attn_kernel,
        grid=(batch, N_LANE_TILES),
        in_specs=[
            pl.BlockSpec((None, N_RES, None, 3, RES_LEN, LANES), lambda b, g: (b, 0, g, 0, 0, 0)),
            pl.BlockSpec((len(DILATIONS), HEADS_PER_TILE, ATTN_BLOCK, 3 * ATTN_BLOCK),
                         lambda b, g: (0, g, 0, 0)),
        ],
        out_specs=pl.BlockSpec((None, N_RES, None, RES_LEN, LANES), lambda b, g: (b, 0, g, 0, 0)),
        out_shape=jax.ShapeDtypeStruct((batch, N_RES, N_LANE_TILES, RES_LEN, LANES), BF16),
        scratch_shapes=[scratch] * 5,
        compiler_params=_cparams("arbitrary", "arbitrary"),
        name="dilated_attention",
    )(qkv, _attn_bias_tables())


def _split2(x):
    a = x.astype(BF16)
    b = (x - a.astype(F32)).astype(BF16)
    return a, b


def _hgrn_kernel(q_ref, f_ref, i_ref, g_ref, gamma_ref, ng_ref, tri_ref, y_ref):
    sup, ch = HGRN_SUPER, HGRN_CHUNK
    n_ch = sup // ch
    gam = gamma_ref[...]
    gmx = jnp.max(gam, axis=0, keepdims=True)
    ge = jnp.exp(gam - gmx)
    lb = ge[0:1] / jnp.sum(ge, axis=0, keepdims=True)
    ng = ng_ref[...]
    tri = tri_ref[...]
    ri = lax.broadcasted_iota(I32, (2 * sup, sup), 0) % sup
    ci = lax.broadcasted_iota(I32, (2 * sup, sup), 1)
    causal_bd2 = (ri // ch == ci // ch) & (ri >= ci)
    head0 = lax.broadcasted_iota(I32, (sup, LANES), 1) < HEAD_DIM
    row_chunk = lax.broadcasted_iota(I32, (sup, LANES), 0) // ch
    same_head = (lax.broadcasted_iota(I32, (LANES, LANES), 0) // HEAD_DIM
                 == lax.broadcasted_iota(I32, (LANES, LANES), 1) // HEAD_DIM)

    def sigmoid(x):
        return 1.0 / (1.0 + jnp.exp(-x))

    def dot_nt(a, b):
        return lax.dot_general(a, b, (((1,), (1,)), ((), ())), preferred_element_type=F32)

    def dot_tn(a, b):
        return lax.dot_general(a, b, (((0,), (0,)), ((), ())), preferred_element_type=F32)

    def expand(x):
        zero = jnp.zeros_like(x)
        return jnp.concatenate([jnp.where(row_chunk == c, x, zero) for c in range(n_ch)], axis=1)

    def body(j, state):
        grp = range(HGRN_GROUP)
        rows = [pl.ds((j * HGRN_GROUP + a) * sup, sup) for a in grp]
        fr = [f_ref[r, :].astype(F32) for r in rows]
        t = [jnp.exp(-jnp.abs(x)) for x in fr]
        rcp = [1.0 / (1.0 + x) for x in t]
        sig = [jnp.where(f >= 0, r, x * r) for f, r, x in zip(fr, rcp, t)]
        nsig = [jnp.where(f >= 0, x * r, r) for f, r, x in zip(fr, rcp, t)]
        logf = [jnp.log(lb + (1.0 - lb) * x) for x in sig]
        key = [(1.0 - lb) * x for x in nsig]
        parts = [_split2(x) for x in logf]
        b = [sum(jnp.dot(tri, p, preferred_element_type=F32) for p in ps) for ps in parts]
        totals = [[x[(c + 1) * ch - 1:(c + 1) * ch, :] for c in range(n_ch)] for x in b]
        b_last = [jnp.concatenate([jnp.broadcast_to(t_, (ch, LANES)) for t_ in ts], axis=0) for ts in totals]
        pad = jnp.zeros((LANES - n_ch, LANES), F32)
        decay_t = [jnp.exp(jnp.transpose(jnp.concatenate(ts + [pad], axis=0))) for ts in totals]
        qv = [q_ref[r, :].astype(F32) for r in rows]
        q_in = [(x * sigmoid(x) * jnp.exp(bb)).astype(BF16) for x, bb in zip(qv, b)]
        k_in = [(k * jnp.exp(-bb)).astype(BF16) for k, bb in zip(key, b)]
        k_end = [(k * jnp.exp(bl - bb)).astype(BF16) for k, bl, bb in zip(key, b_last, b)]
        vv = [i_ref[r, :] for r in rows]
        q2 = [jnp.concatenate([jnp.where(head0, x, jnp.zeros_like(x)), jnp.where(head0, jnp.zeros_like(x), x)],
                              axis=0) for x in q_in]
        att = [jnp.where(causal_bd2, dot_nt(x, k), 0.0).astype(BF16) for x, k in zip(q2, k_in)]
        intra = [jnp.dot(a_, v, preferred_element_type=F32) for a_, v in zip(att, vv)]
        o_intra = [jnp.where(head0, x[:sup], x[sup:]) for x in intra]
        ds_all = [dot_tn(k, expand(v)) for k, v in zip(k_end, vv)]
        s_prev = []
        for a in grp:
            for c in range(n_ch):
                s_prev.append(state.astype(BF16))
                ds = jnp.where(same_head, ds_all[a][:, c * LANES:(c + 1) * LANES], 0.0)
                state = decay_t[a][:, c:c + 1] * state + ds
        o_inter = [jnp.concatenate(
            [jnp.dot(q_in[a][c * ch:(c + 1) * ch], s_prev[a * n_ch + c], preferred_element_type=F32)
             for c in range(n_ch)], axis=0) for a in grp]
        o = [x + y for x, y in zip(o_intra, o_inter)]
        sq = [x * x for x in o]
        ss0 = [jnp.sum(jnp.where(head0, x, 0.0), axis=-1, keepdims=True) for x in sq]
        ss1 = [jnp.sum(jnp.where(head0, 0.0, x), axis=-1, keepdims=True) for x in sq]
        ms = [jnp.where(head0, x, y) * (1.0 / HEAD_DIM) for x, y in zip(ss0, ss1)]
        gv = [g_ref[r, :].astype(F32) for r in rows]
        for r, x, m_, g_ in zip(rows, o, ms, gv):
            y_ref[r, :] = (x * lax.rsqrt(m_ + EPS) * ng * (g_ * sigmoid(g_))).astype(y_ref.dtype)
        return state

    state = jnp.zeros((LANES, LANES), F32)
    for j in range(SEQ // (sup * HGRN_GROUP)):
        state = body(j, state)


def hgrn(hg3, gamma, norm_g):
    batch = hg3.shape[0]
    r = jnp.arange(HGRN_SUPER)
    same = (r[:, None] // HGRN_CHUNK) == (r[None, :] // HGRN_CHUNK)
    tri = (same & (r[:, None] >= r[None, :])).astype(BF16)
    col = lambda which: (lambda b, g: (b, 0, which * N_LANE_TILES + g))
    const2 = lambda b, g: (0, 0)
    return pl.pallas_call(
        _hgrn_kernel,
        grid=(batch, N_LANE_TILES),
        in_specs=[
            pl.BlockSpec((None, SEQ, LANES), col(0)),
            pl.BlockSpec((None, SEQ, LANES), col(1)),
            pl.BlockSpec((None, SEQ, LANES), col(2)),
            pl.BlockSpec((None, SEQ, LANES), col(3)),
            pl.BlockSpec((2, LANES), lambda b, g: (0, g)),
            pl.BlockSpec((1, LANES), lambda b, g: (0, g)),
            pl.BlockSpec((HGRN_SUPER, HGRN_SUPER), const2),
        ],
        out_specs=pl.BlockSpec((None, SEQ, LANES), lambda b, g: (b, 0, g)),
        out_shape=jax.ShapeDtypeStruct((batch, SEQ, WIDTH), BF16),
        compiler_params=_cparams("arbitrary", "arbitrary"),
        name="hgrn2",
    )(hg3, hg3, hg3, hg3, gamma.astype(F32), norm_g.reshape(1, WIDTH).astype(F32), tri)


def _post_mix_kernel(oa_ref, yh_ref, x_ref, ag_ref, wo_ref, g2_ref, wrh_ref, wrl_ref, br_ref, perm_t_ref,
                     tri_ref, lower_ref,
                     x1_ref, h2_ref, rec_ref, tab_ref, carry_ref, lg_sc):
    i = pl.program_id(0)

    @pl.when(i == 0)
    def _():
        lg_sc[...] = jnp.zeros_like(lg_sc)
    _route_tile(lg_sc[...], (i > 0).astype(F32), tri_ref, lower_ref, rec_ref, tab_ref, carry_ref)

    oa = jnp.concatenate(
        [jnp.concatenate([oa_ref[r, g] for g in range(N_LANE_TILES)], axis=1)
         for r in range(N_RES)], axis=0).astype(F32)
    ms = jnp.mean(oa * oa, axis=-1, keepdims=True)
    ya = (oa * lax.rsqrt(ms + EPS) * ag_ref[...]).astype(BF16)
    ya = jnp.dot(perm_t_ref[...], ya, preferred_element_type=F32).astype(BF16)
    mix = (jnp.dot(ya, wo_ref[:WIDTH, :], preferred_element_type=F32)
           + jnp.dot(yh_ref[...], wo_ref[WIDTH:, :], preferred_element_type=F32))
    x1 = x_ref[...] + mix
    x1_ref[...] = x1
    ms2 = jnp.mean(x1 * x1, axis=-1, keepdims=True)
    h2 = x1 * lax.rsqrt(ms2 + EPS) * g2_ref[...]
    hi = h2.astype(BF16)
    h2_ref[...] = hi
    lo = (h2 - hi.astype(F32)).astype(BF16)
    nt = (((1,), (1,)), ((), ()))
    wrh = wrh_ref[...]
    lg_sc[...] = (lax.dot_general(wrh, hi, nt, preferred_element_type=F32)
                  + lax.dot_general(wrh, lo, nt, preferred_element_type=F32)
                  + lax.dot_general(wrl_ref[...], hi, nt, preferred_element_type=F32)
                  + br_ref[...])


def post_mix(oa, yh, x2, attn_g, w_out_bf16, g2, wr_hi, wr_lo, br):
    n = x2.shape[0]
    tiles_per_b = SEQ // TILE_TOKENS
    assert ROUTE_TM == TILE_TOKENS
    n_tiles = n // TILE_TOKENS
    cur = lambda i: jnp.minimum(i, n_tiles - 1)
    prev = lambda i: jnp.maximum(i - 1, 0)
    row = lambda w: pl.BlockSpec((TILE_TOKENS, w), lambda i: (cur(i), 0))
    const = lambda r, c: pl.BlockSpec((r, c), lambda i: (0, 0))
    return pl.pallas_call(
        _post_mix_kernel,
        grid=(n_tiles + 1,),
        in_specs=[pl.BlockSpec((None, N_RES, N_LANE_TILES, U_PER_TILE, LANES),
                               lambda i: (cur(i) // tiles_per_b, 0, 0, cur(i) % tiles_per_b, 0)),
                  row(WIDTH), row(D_MODEL),
                  const(1, WIDTH), const(2 * WIDTH, D_MODEL), const(1, D_MODEL),
                  const(LANES, D_MODEL), const(LANES, D_MODEL), const(LANES, 1),
                  const(TILE_TOKENS, TILE_TOKENS), const(ROUTE_TM, ROUTE_TM), const(N_EXPERTS, N_EXPERTS)],
        out_specs=[row(D_MODEL), row(D_MODEL),
                   pl.BlockSpec((8, ROUTE_TM), lambda i: (0, prev(i))),
                   pl.BlockSpec((1, 8, LANES), lambda i: (prev(i), 0, 0))],
        out_shape=[jax.ShapeDtypeStruct((n, D_MODEL), F32),
                   jax.ShapeDtypeStruct((n, D_MODEL), BF16),
                   jax.ShapeDtypeStruct((8, n), F32),
                   jax.ShapeDtypeStruct((n_tiles, 8, LANES), F32)],
        scratch_shapes=[pltpu.VMEM((N_EXPERTS, LANES), F32), pltpu.VMEM((LANES, TILE_TOKENS), F32)],
        compiler_params=_cparams("arbitrary"),
        name="post_mix",
    )(oa, yh, x2, attn_g.reshape(1, WIDTH), w_out_bf16, g2.reshape(1, D_MODEL), wr_hi, wr_lo, br,
      _tile_permutation().T, *_route_constants())


ROUTE_TM = 512
EXPERT_ROW0 = 32
R_E1, R_E2, R_SLOT1, R_SLOT2, R_G1, R_G2 = 0, 1, 2, 3, 4, 5
T_CARRY, T_ROWS, T_OFF = 0, 1, 2
SEG_END = 0
TOTAL_LANE = N_EXPERTS
RUN_ALIGN = 8
LOCAL_ROWS = 2 * ROUTE_TM + N_EXPERTS * RUN_ALIGN


def _route_tile(lg, live, tri_ref, lower_ref, rec_ref, tab_ref, carry_ref):
    i = pl.program_id(0)

    @pl.when(i == 0)
    def _():
        carry_ref[...] = jnp.zeros_like(carry_ref)

    tm = lg.shape[1]
    sub8 = lax.broadcasted_iota(I32, (8, tm), 0).astype(F32)
    big = 8.0
    gmask = sub8 < N_GROUPS
    gl = jnp.where(gmask, lg[0:8], NEG)
    gmax = jnp.max(gl, axis=0, keepdims=True)
    gsel = jnp.min(jnp.where(gmask & (gl == gmax), sub8, big), axis=0, keepdims=True)
    gsum = jnp.sum(jnp.where(gmask, jnp.exp(gl - gmax), 0.0), axis=0, keepdims=True)
    w_g = 1.0 / gsum
    el = jnp.zeros((EXPERTS_PER_GROUP, tm), F32)
    for g in range(N_GROUPS):
        lo = EXPERT_ROW0 + g * EXPERTS_PER_GROUP
        el = jnp.where(gsel == g, lg[lo:lo + EXPERTS_PER_GROUP], el)
    v1 = jnp.max(el, axis=0, keepdims=True)
    i1 = jnp.min(jnp.where(el == v1, sub8, big), axis=0, keepdims=True)
    el2 = jnp.where(sub8 == i1, NEG, el)
    v2 = jnp.max(el2, axis=0, keepdims=True)
    i2 = jnp.min(jnp.where((el2 == v2) & (sub8 != i1), sub8, big), axis=0, keepdims=True)
    ex = jnp.exp(v2 - v1)
    den = 1.0 / (1.0 + ex)
    g1 = w_g * den
    g2 = w_g * ex * den
    e1 = gsel * EXPERTS_PER_GROUP + i1
    e2 = gsel * EXPERTS_PER_GROUP + i2
    sub_e = lax.broadcasted_iota(I32, (N_EXPERTS, tm), 0).astype(F32)
    oh1 = sub_e == e1
    oh2 = sub_e == e2
    onehot = (oh1 | oh2).astype(BF16)
    before = jnp.dot(onehot, tri_ref[...], preferred_element_type=F32)
    count = jnp.sum(onehot.astype(F32), axis=1, keepdims=True)
    units = jnp.floor((count + (RUN_ALIGN - 1)) * (1.0 / RUN_ALIGN)) * live
    units = jnp.broadcast_to(units, (N_EXPERTS, LANES))
    rows = units * RUN_ALIGN
    off = RUN_ALIGN * jnp.dot(lower_ref[...], units.astype(BF16), preferred_element_type=F32)
    place = off[:, 0:1] + before
    slot1 = jnp.sum(jnp.where(oh1, place, 0.0), axis=0, keepdims=True)
    slot2 = jnp.sum(jnp.where(oh2, place, 0.0), axis=0, keepdims=True)
    total = jnp.sum(rows, axis=0, keepdims=True)
    eye = (lax.broadcasted_iota(I32, (N_EXPERTS, LANES), 0) == lax.broadcasted_iota(I32, (N_EXPERTS, LANES), 1))
    lane1 = lax.broadcasted_iota(I32, (1, LANES), 1)
    as_row = lambda col: jnp.where(lane1 == TOTAL_LANE, total,
                                   jnp.sum(jnp.where(eye, col, 0.0), axis=0, keepdims=True))
    sub = lax.broadcasted_iota(I32, (8, LANES), 0)
    tab_ref[0] = jnp.where(sub == T_CARRY, as_row(carry_ref[...]),
                           jnp.where(sub == T_ROWS, as_row(rows), jnp.where(sub == T_OFF, as_row(off), 0.0)))
    carry_ref[...] += rows
    rec = jnp.zeros((8, tm), F32)
    for slot, val in ((R_E1, e1), (R_E2, e2), (R_SLOT1, slot1),
                      (R_SLOT2, slot2), (R_G1, g1), (R_G2, g2)):
        rec = jnp.where(sub8 == slot, val, rec)
    rec_ref[...] = rec


def _route_constants():
    r = jnp.arange(ROUTE_TM)
    tri = (r[:, None] < r[None, :]).astype(BF16)
    e = jnp.arange(N_EXPERTS)
    lower = (e[:, None] > e[None, :]).astype(BF16)
    return tri, lower


P_START, P_ROWS, P_OFF = 0, 1, 2


def _plan_kernel(carry_ref, rows_ref, off_ref, upper_ref, plan_ref, seg_ref):
    n_tiles = carry_ref.shape[0]
    lane = lax.broadcasted_iota(I32, (1, LANES), 1)
    last = pl.ds(n_tiles - 1, 1)
    totals = jnp.where(lane < N_EXPERTS, carry_ref[last, :] + rows_ref[last, :], 0.0)
    blocks = jnp.floor((totals + (FFN_BLOCK - 1)) * (1.0 / FFN_BLOCK))
    starts = FFN_BLOCK * jnp.dot(jnp.broadcast_to(blocks, (8, LANES)).astype(BF16), upper_ref[...],
                                 preferred_element_type=F32)[0:1]
    seg_ref[...] = jnp.broadcast_to(starts + blocks * FFN_BLOCK, seg_ref.shape)
    plan_ref[P_START] = (carry_ref[...] + starts).astype(I32)
    plan_ref[P_ROWS] = rows_ref[...].astype(I32)
    plan_ref[P_OFF] = off_ref[...].astype(I32)


def run_plan(tab):
    n_tiles = tab.shape[0]
    ln = jnp.arange(LANES)
    upper = (ln[:, None] < ln[None, :]).astype(BF16)
    return pl.pallas_call(
        _plan_kernel,
        out_shape=[jax.ShapeDtypeStruct((3, n_tiles, LANES), I32),
                   jax.ShapeDtypeStruct((8, LANES), F32)],
        compiler_params=pltpu.CompilerParams(vmem_limit_bytes=VMEM_LIMIT),
        name="run_plan",
    )(tab[:, T_CARRY], tab[:, T_ROWS], tab[:, T_OFF], upper)


RUN_PIECE = 64
SMALL_PIECES = (32, 16, 8)
TOTAL_PIECES = (1024, 512, 256, 128, 64, 32, 16, 8)
SORT_CHUNK = 256


def _for_each_run_piece(plan_ref, tile, n_tiles, fn):
    plane = n_tiles * LANES

    def per_expert(e, c):
        idx = tile * LANES + e
        start = plan_ref[P_START * plane + idx]
        rows = plan_ref[P_ROWS * plane + idx]
        off = plan_ref[P_OFF * plane + idx]

        def big(j, c2):
            fn(pl.multiple_of(off + j * RUN_PIECE, RUN_ALIGN), pl.multiple_of(start + j * RUN_PIECE, RUN_ALIGN),
               RUN_PIECE)
            return c2
        lax.fori_loop(0, lax.shift_right_logical(rows, 6), big, 0)
        for size in SMALL_PIECES:
            done = jnp.bitwise_and(rows, -2 * size)

            @pl.when(jnp.bitwise_and(rows, size) != 0)
            def _():
                fn(pl.multiple_of(off + done, RUN_ALIGN), pl.multiple_of(start + done, RUN_ALIGN), size)
        return c
    lax.fori_loop(0, N_EXPERTS, per_expert, 0)


def _for_each_total_piece(plan_ref, tile, n_tiles, fn):
    total = plan_ref[P_OFF * n_tiles * LANES + tile * LANES + TOTAL_LANE]
    for size in TOTAL_PIECES:
        @pl.when(jnp.bitwise_and(total, size) != 0)
        def _():
            fn(size)


def _dispatch_kernel(plan_ref, ends_ref, nv_ref, h2_ref, rec_ref, xs_ref, zero_buf, stage, sem, zsem):
    i = pl.program_id(0)
    n_steps = pl.num_programs(0)
    n_blocks = xs_ref.shape[0] // FFN_BLOCK

    @pl.when(i == 0)
    def _():
        zero_buf[...] = jnp.zeros_like(zero_buf)

        def zero_copy(row0):
            return pltpu.make_async_copy(zero_buf, xs_ref.at[pl.ds(row0, FFN_BLOCK)], zsem)

        def seg_end(e):
            return ends_ref[e], ends_ref[e] > jnp.where(e > 0, ends_ref[jnp.maximum(e - 1, 0)], 0)

        def start_e(e, c):
            end, nonempty = seg_end(e)

            @pl.when(nonempty)
            def _():
                zero_copy(pl.multiple_of(end - FFN_BLOCK, FFN_BLOCK)).start()
            return c
        lax.fori_loop(0, N_EXPERTS, start_e, 0)

        def start_b(blk, c):
            zero_copy(pl.multiple_of(blk * FFN_BLOCK, FFN_BLOCK)).start()
            return c
        lax.fori_loop(nv_ref[0], n_blocks, start_b, 0)

        def wait_e(e, c):
            _, nonempty = seg_end(e)

            @pl.when(nonempty)
            def _():
                zero_copy(0).wait()
            return c
        lax.fori_loop(0, N_EXPERTS, wait_e, 0)

        def wait_b(blk, c):
            zero_copy(0).wait()
            return c
        lax.fori_loop(nv_ref[0], n_blocks, wait_b, 0)

    slot = i % 2
    s1 = rec_ref[R_SLOT1:R_SLOT1 + 1, :]
    s2 = rec_ref[R_SLOT2:R_SLOT2 + 1, :]
    half = PACKED_W
    for c in range(LOCAL_ROWS // SORT_CHUNK):
        rowf = (lax.broadcasted_iota(I32, (SORT_CHUNK, ROUTE_TM), 0) + c * SORT_CHUNK).astype(F32)
        sel = ((rowf == s1) | (rowf == s2)).astype(BF16)
        pick = lambda cols: jnp.dot(sel, h2_ref[:, cols], preferred_element_type=F32)
        stage[slot, c * SORT_CHUNK:(c + 1) * SORT_CHUNK, :] = _pack_pair(pick(slice(0, half)),
                                                                         pick(slice(half, 2 * half)))

    def start_piece(lrow, grow, size):
        pltpu.make_async_copy(stage.at[slot, pl.ds(lrow, size)], xs_ref.at[pl.ds(grow, size)],
                              sem.at[slot]).start()
    _for_each_run_piece(plan_ref, i, n_steps, start_piece)

    def wait_pieces(tile, s):
        def wait_piece(size):
            pltpu.make_async_copy(stage.at[s, pl.ds(0, size)], xs_ref.at[pl.ds(0, size)], sem.at[s]).wait()
        _for_each_total_piece(plan_ref, tile, n_steps, wait_piece)

    @pl.when(i > 0)
    def _():
        wait_pieces(i - 1, 1 - slot)

    @pl.when(i + 1 == n_steps)
    def _():
        wait_pieces(i, slot)


def dispatch(h2, rec, plan, seg_ends, n_valid, n_rows):
    n = h2.shape[0]
    return pl.pallas_call(
        _dispatch_kernel,
        grid_spec=pltpu.PrefetchScalarGridSpec(
            num_scalar_prefetch=3,
            grid=(n // ROUTE_TM,),
            in_specs=[pl.BlockSpec((ROUTE_TM, D_MODEL), lambda i, *_: (i, 0)),
                      pl.BlockSpec((8, ROUTE_TM), lambda i, *_: (0, i))],
            out_specs=pl.BlockSpec(memory_space=pl.ANY),
            scratch_shapes=[pltpu.VMEM((FFN_BLOCK, PACKED_W), U32),
                            pltpu.VMEM((2, LOCAL_ROWS, PACKED_W), U32),
                            pltpu.SemaphoreType.DMA((2,)),
                            pltpu.SemaphoreType.DMA(())],
        ),
        out_shape=jax.ShapeDtypeStruct((n_rows, PACKED_W), U32),
        compiler_params=_cparams("arbitrary"),
        name="dispatch",
    )(plan, seg_ends, n_valid, h2, rec)


def _ffn_kernel(be_ref, nv_ref, xs_ref, wg_ref, wu_ref, wd_ref, ys_ref, wg_sc, wu_sc, wd_sc):
    i = pl.program_id(0)
    prev = be_ref[jnp.maximum(i - 1, 0)]
    fresh = (i == 0) | (be_ref[i] != prev)

    @pl.when(fresh)
    def _():
        wg_sc[...] = wg_ref[0].astype(BF16)
        wu_sc[...] = wu_ref[0].astype(BF16)
        wd_sc[...] = wd_ref[0].astype(BF16)

    @pl.when(i < nv_ref[0])
    def _():
        xa, xb = (t.astype(BF16) for t in _unpack_halves(xs_ref[...]))
        half = PACKED_W
        gate = (jnp.dot(xa, wg_sc[:half, :], preferred_element_type=F32)
                + jnp.dot(xb, wg_sc[half:, :], preferred_element_type=F32))
        up = (jnp.dot(xa, wu_sc[:half, :], preferred_element_type=F32)
              + jnp.dot(xb, wu_sc[half:, :], preferred_element_type=F32))
        hid = (gate * (1.0 / (1.0 + jnp.exp(-gate))) * up).astype(BF16)
        ys_ref[...] = _pack_halves(jnp.dot(hid, wd_sc[...], preferred_element_type=F32))

    @pl.when(i >= nv_ref[0])
    def _():
        ys_ref[...] = jnp.zeros_like(ys_ref)


def expert_ffn(xs, block_expert, n_valid, w_gate, w_up, w_down):
    n_rows = xs.shape[0]
    n_blocks = n_rows // FFN_BLOCK
    wmap = lambda i, be, nv: (be[i], 0, 0)
    return pl.pallas_call(
        _ffn_kernel,
        grid_spec=pltpu.PrefetchScalarGridSpec(
            num_scalar_prefetch=2,
            grid=(n_blocks,),
            in_specs=[pl.BlockSpec((FFN_BLOCK, PACKED_W), lambda i, be, nv: (jnp.minimum(i, nv[0] - 1), 0)),
                      pl.BlockSpec((1, D_MODEL, D_EXPERT), wmap),
                      pl.BlockSpec((1, D_MODEL, D_EXPERT), wmap),
                      pl.BlockSpec((1, D_EXPERT, D_MODEL), wmap)],
            out_specs=pl.BlockSpec((FFN_BLOCK, PACKED_W), lambda i, be, nv: (i, 0)),
            scratch_shapes=[pltpu.VMEM((D_MODEL, D_EXPERT), BF16),
                            pltpu.VMEM((D_MODEL, D_EXPERT), BF16),
                            pltpu.VMEM((D_EXPERT, D_MODEL), BF16)],
        ),
        out_shape=jax.ShapeDtypeStruct((n_rows, PACKED_W), U32),
        compiler_params=_cparams("arbitrary"),
        name="expert_ffn",
    )(block_expert, n_valid, xs, w_gate, w_up, w_down)


def _combine_kernel(plan_ref, x1_ref, rec_ref, gf_ref, ys_ref, out_ref, buf, sem):
    i = pl.program_id(0)
    n_steps = pl.num_programs(0)
    slot = i % 2

    def fetch_tile(tile, s):
        def start_piece(lrow, grow, size):
            pltpu.make_async_copy(ys_ref.at[pl.ds(grow, size)], buf.at[s, pl.ds(lrow, size)], sem.at[s]).start()
        _for_each_run_piece(plan_ref, tile, n_steps, start_piece)

    @pl.when(i == 0)
    def _():
        buf[...] = jnp.zeros_like(buf)
        fetch_tile(0, 0)

    @pl.when(i + 1 < n_steps)
    def _():
        fetch_tile(i + 1, 1 - slot)

    def wait_piece(size):
        pltpu.make_async_copy(ys_ref.at[pl.ds(0, size)], buf.at[slot, pl.ds(0, size)], sem.at[slot]).wait()
    _for_each_total_piece(plan_ref, i, n_steps, wait_piece)

    rowf = lax.broadcasted_iota(I32, (LOCAL_ROWS, ROUTE_TM), 0).astype(F32)
    sel = (jnp.where(rowf == rec_ref[R_SLOT1:R_SLOT1 + 1, :], rec_ref[R_G1:R_G1 + 1, :], 0.0)
           + jnp.where(rowf == rec_ref[R_SLOT2:R_SLOT2 + 1, :], rec_ref[R_G2:R_G2 + 1, :], 0.0)).astype(BF16)
    ya, yb = (t.astype(BF16) for t in _unpack_halves(buf[slot]))
    half = PACKED_W
    tn = (((0,), (0,)), ((), ()))
    xa = x1_ref[:, :half] + lax.dot_general(sel, ya, tn, preferred_element_type=F32)
    xb = x1_ref[:, half:] + lax.dot_general(sel, yb, tn, preferred_element_type=F32)
    ms = (jnp.sum(xa * xa, axis=-1, keepdims=True) + jnp.sum(xb * xb, axis=-1, keepdims=True)) * (1.0 / D_MODEL)
    scale = lax.rsqrt(ms + EPS)
    out_ref[:, :half] = xa * scale * gf_ref[:, :half]
    out_ref[:, half:] = xb * scale * gf_ref[:, half:]


def combine(x1, rec, norm_f_g, ys, plan):
    n = x1.shape[0]
    tm = ROUTE_TM
    return pl.pallas_call(
        _combine_kernel,
        grid_spec=pltpu.PrefetchScalarGridSpec(
            num_scalar_prefetch=1,
            grid=(n // tm,),
            in_specs=[pl.BlockSpec((tm, D_MODEL), lambda i, p: (i, 0)),
                      pl.BlockSpec((8, tm), lambda i, p: (0, i)),
                      pl.BlockSpec((1, D_MODEL), lambda i, p: (0, 0)),
                      pl.BlockSpec(memory_space=pl.ANY)],
            out_specs=pl.BlockSpec((tm, D_MODEL), lambda i, p: (i, 0)),
            scratch_shapes=[pltpu.VMEM((2, LOCAL_ROWS, PACKED_W), U32),
                            pltpu.SemaphoreType.DMA((2,))],
        ),
        out_shape=jax.ShapeDtypeStruct((n, D_MODEL), F32),
        compiler_params=_cparams("arbitrary"),
        name="combine",
    )(plan, x1, rec, norm_f_g.reshape(1, D_MODEL), ys)


def _router_weights(w_group, b_group, w_router, b_router):
    w = jnp.zeros((LANES, D_MODEL), F32)
    w = w.at[:N_GROUPS].set(w_group.T).at[EXPERT_ROW0:EXPERT_ROW0 + N_EXPERTS].set(w_router.T)
    b = jnp.zeros((LANES, 1), F32)
    b = b.at[:N_GROUPS, 0].set(b_group).at[EXPERT_ROW0:EXPERT_ROW0 + N_EXPERTS, 0].set(b_router)
    hi = w.astype(BF16)
    lo = (w - hi.astype(F32)).astype(BF16)
    return hi, lo, b


def _sorted_rows_bound(n_tokens):
    worst = (2 * n_tokens + (n_tokens // ROUTE_TM) * N_EXPERTS * (RUN_ALIGN - 1)
             + N_EXPERTS * (FFN_BLOCK - 1))
    return -(-worst // FFN_BLOCK) * FFN_BLOCK


def _block_plan(seg, n_rows):
    seg_ends = seg[SEG_END, :N_EXPERTS].astype(I32)
    n_blocks = n_rows // FFN_BLOCK
    blk_start = jnp.arange(n_blocks, dtype=I32) * FFN_BLOCK
    block_expert = jnp.minimum(jnp.sum((seg_ends[None, :] <= blk_start[:, None]).astype(I32), axis=1),
                               N_EXPERTS - 1)
    n_valid = seg_ends[-1:] // FFN_BLOCK
    return seg_ends, block_expert, n_valid


def kernel(x, norm1_g, w_in, attn_norm_g, hgrn_gamma, hgrn_norm_g, w_out, norm2_g, w_group, b_group,
           w_router, b_router, w_gate, w_up, w_down, norm_f_g):
    batch, seq, d = x.shape
    assert seq == SEQ and d == D_MODEL and norm1_g.shape[0] == 1
    n = batch * seq
    x2 = x.reshape(n, d)
    qkv, hg = in_proj(x2, norm1_g[0], w_in[0].astype(BF16), batch)
    oa = attention(qkv)
    yh = hgrn(hg.reshape(batch, seq, HG_W), hgrn_gamma, hgrn_norm_g[0]).reshape(n, WIDTH)
    wr_hi, wr_lo, br = _router_weights(w_group[0], b_group[0], w_router[0], b_router[0])
    x1, h2, rec, tab = post_mix(oa, yh, x2, attn_norm_g[0], w_out[0].astype(BF16), norm2_g[0],
                                wr_hi, wr_lo, br)
    plan, seg = run_plan(tab)
    plan = plan.reshape(-1)
    n_rows = _sorted_rows_bound(n)
    seg_ends, block_expert, n_valid = _block_plan(seg, n_rows)
    xs = dispatch(h2, rec, plan, seg_ends, n_valid, n_rows)
    ys = expert_ffn(xs, block_expert, n_valid, w_gate[0], w_up[0], w_down[0])
    return combine(x1, rec, norm_f_g, ys, plan).reshape(batch, seq, d)
```

```python
import jax
import jax.numpy as jnp
from jax import lax
from jax.experimental import pallas as pl
from jax.experimental.pallas import tpu as pltpu

F32 = jnp.float32
BF16 = jnp.bfloat16
I32 = jnp.int32
U32 = jnp.uint32

D_MODEL = 1024
HEAD_DIM = 64
N_HEADS = 8
WIDTH = N_HEADS * HEAD_DIM
QKV_W = 3 * WIDTH
HG_W = 4 * WIDTH
SEQ = 2048
ATTN_BLOCK = 128
DILATIONS = (1, 4, 16)
N_RES = 16
RES_LEN = SEQ // N_RES
TILE_TOKENS = 512
U_PER_TILE = TILE_TOKENS // N_RES
HGRN_CHUNK = 32
HGRN_SUPER = 256
HGRN_GROUP = 8
N_GROUPS = 4
EXPERTS_PER_GROUP = 8
N_EXPERTS = 32
D_EXPERT = 512
FFN_BLOCK = 512
PACKED_W = D_MODEL // 2
EPS = 1e-6
NEG = -1e30
LOG2E = 1.4426950408889634
LANES = 128
VMEM_LIMIT = 56 * 1024 * 1024


def _cparams(*sem):
    return pltpu.CompilerParams(dimension_semantics=sem, vmem_limit_bytes=VMEM_LIMIT)


def _pack_pair(a, b):
    hi = pltpu.bitcast(a.astype(BF16).astype(F32), U32)
    lo = pltpu.bitcast(b.astype(BF16).astype(F32), U32)
    return hi | (lo >> 16)


def _pack_halves(x):
    w = x.shape[1] // 2
    return _pack_pair(x[:, :w], x[:, w:])


def _unpack_halves(p):
    return pltpu.bitcast(p & jnp.uint32(0xFFFF0000), F32), pltpu.bitcast(p << 16, F32)


N_LANE_TILES = WIDTH // LANES


def _tile_permutation():
    i = jnp.arange(TILE_TOKENS)
    src = N_RES * (i % U_PER_TILE) + i // U_PER_TILE
    return (src[:, None] == jnp.arange(TILE_TOKENS)[None, :]).astype(BF16)


def _in_proj_kernel(x_ref, g_ref, w_ref, x4_ref, qkv_ref, hg_ref, xp, sem):
    i = pl.program_id(0)
    n_steps = pl.num_programs(0)
    tiles_per_b = SEQ // TILE_TOKENS
    slot = i % 2

    def fetch(tile, s):
        b = tile // tiles_per_b
        u0 = (tile % tiles_per_b) * U_PER_TILE
        return [pltpu.make_async_copy(x4_ref.at[b, pl.ds(u0, U_PER_TILE), r, :],
                                      xp.at[s, pl.ds(r * U_PER_TILE, U_PER_TILE), :], sem.at[s])
                for r in range(N_RES)]

    @pl.when(i == 0)
    def _():
        for cp in fetch(0, 0):
            cp.start()

    @pl.when(i + 1 < n_steps)
    def _():
        for cp in fetch(i + 1, 1 - slot):
            cp.start()

    def normed(x):
        ms = jnp.mean(x * x, axis=-1, keepdims=True)
        return (x * lax.rsqrt(ms + EPS) * g_ref[...]).astype(BF16)

    h = normed(x_ref[...])
    for j in range(HG_W // WIDTH):
        wsl = slice(QKV_W + j * WIDTH, QKV_W + (j + 1) * WIDTH)
        hg_ref[:, j * WIDTH:(j + 1) * WIDTH] = jnp.dot(h, w_ref[:, wsl], preferred_element_type=F32).astype(BF16)
    for cp in fetch(i, slot):
        cp.wait()
    hp = normed(xp[slot])
    for c in range(QKV_W // WIDTH):
        res = jnp.dot(hp, w_ref[:, c * WIDTH:(c + 1) * WIDTH], preferred_element_type=F32)
        for r in range(N_RES):
            for g in range(N_LANE_TILES):
                qkv_ref[r, g, c] = res[r * U_PER_TILE:(r + 1) * U_PER_TILE, g * LANES:(g + 1) * LANES]


def in_proj(x2, g, w_bf16, batch):
    n = x2.shape[0]
    tiles_per_b = SEQ // TILE_TOKENS
    x4 = x2.reshape(batch, RES_LEN, N_RES, D_MODEL)
    return pl.pallas_call(
        _in_proj_kernel,
        grid=(n // TILE_TOKENS,),
        in_specs=[
            pl.BlockSpec((TILE_TOKENS, D_MODEL), lambda i: (i, 0)),
            pl.BlockSpec((1, D_MODEL), lambda i: (0, 0)),
            pl.BlockSpec((D_MODEL, QKV_W + HG_W), lambda i: (0, 0)),
            pl.BlockSpec(memory_space=pl.ANY),
        ],
        out_specs=[
            pl.BlockSpec((None, N_RES, N_LANE_TILES, 3, U_PER_TILE, LANES),
                         lambda i: (i // tiles_per_b, 0, 0, 0, i % tiles_per_b, 0)),
            pl.BlockSpec((TILE_TOKENS, HG_W), lambda i: (i, 0)),
        ],
        out_shape=[
            jax.ShapeDtypeStruct((batch, N_RES, N_LANE_TILES, 3, RES_LEN, LANES), F32),
            jax.ShapeDtypeStruct((n, HG_W), BF16),
        ],
        scratch_shapes=[pltpu.VMEM((2, TILE_TOKENS, D_MODEL), F32), pltpu.SemaphoreType.DMA((2,))],
        compiler_params=_cparams("arbitrary"),
        name="in_proj",
    )(x2, g.reshape(1, D_MODEL), w_bf16, x4)


ATTN_GROUP16 = 16
ATTN_GROUP1 = 5
HEADS_PER_TILE = LANES // HEAD_DIM


def _attn_bias_tables():
    slopes = 2.0 ** (-8.0 * jnp.arange(1, N_HEADS + 1, dtype=F32) / N_HEADS)

    def table(qpos, kpos, dil):
        sd = qpos[:, None] - kpos[None, :]
        b = -slopes[:, None, None] * (sd * dil).astype(F32)[None] * LOG2E
        return jnp.where(((sd >= 0) & (sd <= ATTN_BLOCK))[None], b, NEG)

    q = jnp.arange(ATTN_BLOCK)
    k = jnp.arange(2 * ATTN_BLOCK)
    q1 = 16 * (q % 8) + q // 8 + ATTN_BLOCK
    k1 = 16 * (k % 16) + k // 16
    q4 = 4 * (q % 32) + q // 32 + ATTN_BLOCK
    k4 = 4 * (k % 64) + k // 64
    q16 = q + ATTN_BLOCK
    none = jnp.full((N_HEADS, ATTN_BLOCK, 2 * ATTN_BLOCK), NEG, F32)
    tabs = [jnp.concatenate([table(q1, k1, 1), table(q1, q1, 1)], axis=-1),
            jnp.concatenate([table(q4, k4, 4), table(q4, q4, 4)], axis=-1),
            jnp.concatenate([none, table(q16, q16, 16)], axis=-1)]
    return jnp.stack(tabs, axis=0)


def _attn_bias(bias_ref, d_idx, with_prev):
    ksl = slice(0, 2 * ATTN_BLOCK) if with_prev else slice(2 * ATTN_BLOCK, 3 * ATTN_BLOCK)
    return jnp.concatenate([bias_ref[d_idx, h, :, ksl] for h in range(HEADS_PER_TILE)], axis=0)


def _attn_blocks(blocks):
    nb = ATTN_BLOCK
    lane = lax.broadcasted_iota(I32, (nb, LANES), 1)
    head0 = lane < HEAD_DIM
    biases = [blk[3] for blk in blocks]
    blocks = [blk[:3] for blk in blocks]
    qs = [q * (HEAD_DIM ** -0.5 * LOG2E) for q, _, _ in blocks]
    qq = [jnp.concatenate([jnp.where(head0, q, 0.0), jnp.where(head0, 0.0, q)], axis=0).astype(BF16)
          for q in qs]
    ks = [k.astype(BF16) for _, k, _ in blocks]
    vs = [v.astype(BF16) for _, _, v in blocks]
    v_ones = [jnp.concatenate([v, jnp.ones_like(v)], axis=-1) for v in vs]
    s = [lax.dot_general(q2, k, (((1,), (1,)), ((), ())), preferred_element_type=F32) + bias
         for q2, k, bias in zip(qq, ks, biases)]
    m = [jnp.max(x, axis=-1, keepdims=True) for x in s]
    p = [jnp.exp2(x - mx).astype(BF16) for x, mx in zip(s, m)]
    acc = [jnp.dot(px, vo, preferred_element_type=F32) for px, vo in zip(p, v_ones)]
    head0_wide = jnp.concatenate([head0, head0], axis=-1)
    outs = []
    for a, mx in zip(acc, m):
        ol = jnp.where(head0_wide, a[:nb], a[nb:])
        l = ol[:, LANES:]
        outs.append((ol[:, :LANES] * (1.0 / l), jnp.where(head0, mx[:nb], mx[nb:]) + jnp.log2(l)))
    return outs


def _attn_kernel(qkv_ref, bias_ref, o_ref, o4_ref, l4_ref, o16_ref, l16_ref, fin_ref):
    grp = ATTN_GROUP16

    def body16(j, c):
        rs = [j * grp + a for a in range(grp)]
        bias = _attn_bias(bias_ref, 2, False)
        res = _attn_blocks([(qkv_ref[r, 0], qkv_ref[r, 1], qkv_ref[r, 2], bias) for r in rs])
        for r, (o, l) in zip(rs, res):
            o16_ref[r] = o
            l16_ref[r] = l
        return c
    for j in range(N_RES // grp):
        body16(j, 0)

    def gather4(c, rho, u0, nu):
        return jnp.concatenate([qkv_ref[rho + 4 * a, c, pl.ds(u0, nu), :] for a in range(4)], axis=0)

    def store4(rho, u0, o, l):
        for a in range(4):
            o4_ref[rho + 4 * a, pl.ds(u0, 32), :] = o[a * 32:(a + 1) * 32]
            l4_ref[rho + 4 * a, pl.ds(u0, 32), :] = l[a * 32:(a + 1) * 32]

    def body4(j, c):
        bias = _attn_bias(bias_ref, 1, True)
        todo = [(2 * j + a, 32 * n) for a in range(2) for n in range(1, 4)]
        res = _attn_blocks([(gather4(0, rho, u0, 32), gather4(1, rho, u0 - 32, 64),
                             gather4(2, rho, u0 - 32, 64), bias) for rho, u0 in todo])
        for (rho, u0), (o, l) in zip(todo, res):
            store4(rho, u0, o, l)
        return c
    for j in range(2):
        body4(j, 0)

    def gather1(ref, u0, nu, *lead):
        return jnp.concatenate([ref[(r,) + lead + (pl.ds(u0, nu), slice(None))] for r in range(N_RES)],
                               axis=0)

    def merge_and_store(u0, o1, l1):
        l4 = gather1(l4_ref, u0, 8)
        l16 = gather1(l16_ref, u0, 8)
        mx = jnp.maximum(jnp.maximum(l1, l4), l16)
        e1 = jnp.exp2(l1 - mx)
        e4 = jnp.exp2(l4 - mx)
        e16 = jnp.exp2(l16 - mx)
        inv = 1.0 / (e1 + e4 + e16)
        o = (e1 * inv) * o1 + (e4 * inv) * gather1(o4_ref, u0, 8) + (e16 * inv) * gather1(o16_ref, u0, 8)
        for r in range(N_RES):
            fin_ref[r, pl.ds(u0, 8), :] = o[r * 8:(r + 1) * 8]

    bias4 = _attn_bias(bias_ref, 1, False)
    first = [tuple(gather4(c, rho, 0, 32) for c in range(3)) + (bias4,) for rho in range(4)]
    first.append(tuple(gather1(qkv_ref, 0, 8, c) for c in range(3)) + (_attn_bias(bias_ref, 0, False),))
    res = _attn_blocks(first)
    for rho in range(4):
        store4(rho, 0, *res[rho])
    merge_and_store(0, *res[4])

    n_blocks1 = RES_LEN // 8 - 1

    def body1(j, c):
        u0s = [(1 + j * ATTN_GROUP1 + a) * 8 for a in range(ATTN_GROUP1)]
        bias = _attn_bias(bias_ref, 0, True)
        res = _attn_blocks([(gather1(qkv_ref, u0, 8, 0), gather1(qkv_ref, u0 - 8, 16, 1),
                             gather1(qkv_ref, u0 - 8, 16, 2), bias) for u0 in u0s])
        for u0, (o, l) in zip(u0s, res):
            merge_and_store(u0, o, l)
        return c
    for j in range(n_blocks1 // ATTN_GROUP1):
        body1(j, 0)

    def emit(r, c):
        o_ref[r] = fin_ref[r].astype(o_ref.dtype)
        return c
    lax.fori_loop(0, N_RES, emit, 0)


def attention(qkv):
    batch = qkv.shape[0]
    scratch = pltpu.VMEM((N_RES, RES_LEN, LANES), F32)
    return pl.pallas_call(
        _attn_kernel,
        grid=(batch, N_LANE_TILES),
        in_specs=[
            pl.BlockSpec((None, N_RES, None, 3, RES_LEN, LANES), lambda b, g: (b, 0, g, 0, 0, 0)),
            pl.BlockSpec((len(DILATIONS), HEADS_PER_TILE, ATTN_BLOCK, 3 * ATTN_BLOCK),
                         lambda b, g: (0, g, 0, 0)),
        ],
        out_specs=pl.BlockSpec((None, N_RES, None, RES_LEN, LANES), lambda b, g: (b, 0, g, 0, 0)),
        out_shape=jax.ShapeDtypeStruct((batch, N_RES, N_LANE_TILES, RES_LEN, LANES), BF16),
        scratch_shapes=[scratch] * 5,
        compiler_params=_cparams("arbitrary", "arbitrary"),
        name="dilated_attention",
    )(qkv, _attn_bias_tables())


def _split2(x):
    a = x.astype(BF16)
    b = (x - a.astype(F32)).astype(BF16)
    return a, b


def _hgrn_kernel(q_ref, f_ref, i_ref, g_ref, gamma_ref, ng_ref, tri_ref, y_ref):
    sup, ch = HGRN_SUPER, HGRN_CHUNK
    n_ch = sup // ch
    gam = gamma_ref[...]
    gmx = jnp.max(gam, axis=0, keepdims=True)
    ge = jnp.exp(gam - gmx)
    lb = ge[0:1] / jnp.sum(ge, axis=0, keepdims=True)
    ng = ng_ref[...]
    tri = tri_ref[...]
    ri = lax.broadcasted_iota(I32, (2 * sup, sup), 0) % sup
    ci = lax.broadcasted_iota(I32, (2 * sup, sup), 1)
    causal_bd2 = (ri // ch == ci // ch) & (ri >= ci)
    head0 = lax.broadcasted_iota(I32, (sup, LANES), 1) < HEAD_DIM
    same_head = (lax.broadcasted_iota(I32, (LANES, LANES), 0) // HEAD_DIM
                 == lax.broadcasted_iota(I32, (LANES, LANES), 1) // HEAD_DIM)

    def sigmoid(x):
        return 1.0 / (1.0 + jnp.exp(-x))

    def dot_nt(a, b):
        return lax.dot_general(a, b, (((1,), (1,)), ((), ())), preferred_element_type=F32)

    def dot_tn(a, b):
        return lax.dot_general(a, b, (((0,), (0,)), ((), ())), preferred_element_type=F32)

    def body(j, state):
        grp = range(HGRN_GROUP)
        rows = [pl.ds((j * HGRN_GROUP + a) * sup, sup) for a in grp]
        fr = [f_ref[r, :].astype(F32) for r in rows]
        t = [jnp.exp(-jnp.abs(x)) for x in fr]
        rcp = [1.0 / (1.0 + x) for x in t]
        sig = [jnp.where(f >= 0, r, x * r) for f, r, x in zip(fr, rcp, t)]
        nsig = [jnp.where(f >= 0, x * r, r) for f, r, x in zip(fr, rcp, t)]
        logf = [jnp.log(lb + (1.0 - lb) * x) for x in sig]
        key = [(1.0 - lb) * x for x in nsig]
        parts = [_split2(x) for x in logf]
        b = [sum(jnp.dot(tri, p, preferred_element_type=F32) for p in ps) for ps in parts]
        totals = [[x[(c + 1) * ch - 1:(c + 1) * ch, :] for c in range(n_ch)] for x in b]
        b_last = [jnp.concatenate([jnp.broadcast_to(t_, (ch, LANES)) for t_ in ts], axis=0) for ts in totals]
        pad = jnp.zeros((LANES - n_ch, LANES), F32)
        decay_t = [jnp.exp(jnp.transpose(jnp.concatenate(ts + [pad], axis=0))) for ts in totals]
        qv = [q_ref[r, :].astype(F32) for r in rows]
        q_in = [(x * sigmoid(x) * jnp.exp(bb)).astype(BF16) for x, bb in zip(qv, b)]
        k_in = [(k * jnp.exp(-bb)).astype(BF16) for k, bb in zip(key, b)]
        k_end = [(k * jnp.exp(bl - bb)).astype(BF16) for k, bl, bb in zip(key, b_last, b)]
        vv = [i_ref[r, :] for r in rows]
        q2 = [jnp.concatenate([jnp.where(head0, x, jnp.zeros_like(x)), jnp.where(head0, jnp.zeros_like(x), x)],
                              axis=0) for x in q_in]
        att = [jnp.where(causal_bd2, dot_nt(x, k), 0.0).astype(BF16) for x, k in zip(q2, k_in)]
        intra = [jnp.dot(a_, v, preferred_element_type=F32) for a_, v in zip(att, vv)]
        o_intra = [jnp.where(head0, x[:sup], x[sup:]) for x in intra]
        ds_all = [jnp.concatenate([dot_tn(k[c * ch:(c + 1) * ch], v[c * ch:(c + 1) * ch])
                                   for c in range(n_ch)], axis=1) for k, v in zip(k_end, vv)]
        s_prev = []
        for a in grp:
            for c in range(n_ch):
                s_prev.append(state.astype(BF16))
                ds = jnp.where(same_head, ds_all[a][:, c * LANES:(c + 1) * LANES], 0.0)
                state = decay_t[a][:, c:c + 1] * state + ds
        o_inter = [jnp.concatenate(
            [jnp.dot(q_in[a][c * ch:(c + 1) * ch], s_prev[a * n_ch + c], preferred_element_type=F32)
             for c in range(n_ch)], axis=0) for a in grp]
        o = [x + y for x, y in zip(o_intra, o_inter)]
        sq = [x * x for x in o]
        ss0 = [jnp.sum(jnp.where(head0, x, 0.0), axis=-1, keepdims=True) for x in sq]
        ss1 = [jnp.sum(jnp.where(head0, 0.0, x), axis=-1, keepdims=True) for x in sq]
        ms = [jnp.where(head0, x, y) * (1.0 / HEAD_DIM) for x, y in zip(ss0, ss1)]
        gv = [g_ref[r, :].astype(F32) for r in rows]
        for r, x, m_, g_ in zip(rows, o, ms, gv):
            y_ref[r, :] = (x * lax.rsqrt(m_ + EPS) * ng * (g_ * sigmoid(g_))).astype(y_ref.dtype)
        return state

    state = jnp.zeros((LANES, LANES), F32)
    for j in range(SEQ // (sup * HGRN_GROUP)):
        state = body(j, state)


def hgrn(hg3, gamma, norm_g):
    batch = hg3.shape[0]
    r = jnp.arange(HGRN_SUPER)
    same = (r[:, None] // HGRN_CHUNK) == (r[None, :] // HGRN_CHUNK)
    tri = (same & (r[:, None] >= r[None, :])).astype(BF16)
    col = lambda which: (lambda b, g: (b, 0, which * N_LANE_TILES + g))
    const2 = lambda b, g: (0, 0)
    return pl.pallas_call(
        _hgrn_kernel,
        grid=(batch, N_LANE_TILES),
        in_specs=[
            pl.BlockSpec((None, SEQ, LANES), col(0)),
            pl.BlockSpec((None, SEQ, LANES), col(1)),
            pl.BlockSpec((None, SEQ, LANES), col(2)),
            pl.BlockSpec((None, SEQ, LANES), col(3)),
            pl.BlockSpec((2, LANES), lambda b, g: (0, g)),
            pl.BlockSpec((1, LANES), lambda b, g: (0, g)),
            pl.BlockSpec((HGRN_SUPER, HGRN_SUPER), const2),
        ],
        out_specs=pl.BlockSpec((None, SEQ, LANES), lambda b, g: (b, 0, g)),
        out_shape=jax.ShapeDtypeStruct((batch, SEQ, WIDTH), BF16),
        compiler_params=_cparams("arbitrary", "arbitrary"),
        name="hgrn2",
    )(hg3, hg3, hg3, hg3, gamma.astype(F32), norm_g.reshape(1, WIDTH).astype(F32), tri)


def _post_mix_kernel(oa_ref, yh_ref, x_ref, ag_ref, wo_ref, g2_ref, wrh_ref, wrl_ref, br_ref, perm_t_ref,
                     tri_ref, lower_ref,
                     x1_ref, h2_ref, rec_ref, tab_ref, carry_ref, lg_sc):
    i = pl.program_id(0)

    @pl.when(i == 0)
    def _():
        lg_sc[...] = jnp.zeros_like(lg_sc)
    _route_tile(lg_sc[...], (i > 0).astype(F32), tri_ref, lower_ref, rec_ref, tab_ref, carry_ref)

    oa = jnp.concatenate(
        [jnp.concatenate([oa_ref[r, g] for g in range(N_LANE_TILES)], axis=1)
         for r in range(N_RES)], axis=0).astype(F32)
    ms = jnp.mean(oa * oa, axis=-1, keepdims=True)
    ya = (oa * lax.rsqrt(ms + EPS) * ag_ref[...]).astype(BF16)
    ya = jnp.dot(perm_t_ref[...], ya, preferred_element_type=F32).astype(BF16)
    mix = (jnp.dot(ya, wo_ref[:WIDTH, :], preferred_element_type=F32)
           + jnp.dot(yh_ref[...], wo_ref[WIDTH:, :], preferred_element_type=F32))
    x1 = x_ref[...] + mix
    x1_ref[...] = x1
    ms2 = jnp.mean(x1 * x1, axis=-1, keepdims=True)
    h2 = x1 * lax.rsqrt(ms2 + EPS) * g2_ref[...]
    hi = h2.astype(BF16)
    h2_ref[...] = hi
    lo = (h2 - hi.astype(F32)).astype(BF16)
    nt = (((1,), (1,)), ((), ()))
    wrh = wrh_ref[...]
    lg_sc[...] = (lax.dot_general(wrh, hi, nt, preferred_element_type=F32)
                  + lax.dot_general(wrh, lo, nt, preferred_element_type=F32)
                  + lax.dot_general(wrl_ref[...], hi, nt, preferred_element_type=F32)
                  + br_ref[...])


def post_mix(oa, yh, x2, attn_g, w_out_bf16, g2, wr_hi, wr_lo, br):
    n = x2.shape[0]
    tiles_per_b = SEQ // TILE_TOKENS
    assert ROUTE_TM == TILE_TOKENS
    n_tiles = n // TILE_TOKENS
    cur = lambda i: jnp.minimum(i, n_tiles - 1)
    prev = lambda i: jnp.maximum(i - 1, 0)
    row = lambda w: pl.BlockSpec((TILE_TOKENS, w), lambda i: (cur(i), 0))
    const = lambda r, c: pl.BlockSpec((r, c), lambda i: (0, 0))
    return pl.pallas_call(
        _post_mix_kernel,
        grid=(n_tiles + 1,),
        in_specs=[pl.BlockSpec((None, N_RES, N_LANE_TILES, U_PER_TILE, LANES),
                               lambda i: (cur(i) // tiles_per_b, 0, 0, cur(i) % tiles_per_b, 0)),
                  row(WIDTH), row(D_MODEL),
                  const(1, WIDTH), const(2 * WIDTH, D_MODEL), const(1, D_MODEL),
                  const(LANES, D_MODEL), const(LANES, D_MODEL), const(LANES, 1),
                  const(TILE_TOKENS, TILE_TOKENS), const(ROUTE_TM, ROUTE_TM), const(N_EXPERTS, N_EXPERTS)],
        out_specs=[row(D_MODEL), row(D_MODEL),
                   pl.BlockSpec((8, ROUTE_TM), lambda i: (0, prev(i))),
                   pl.BlockSpec((1, 8, LANES), lambda i: (prev(i), 0, 0))],
        out_shape=[jax.ShapeDtypeStruct((n, D_MODEL), F32),
                   jax.ShapeDtypeStruct((n, D_MODEL), BF16),
                   jax.ShapeDtypeStruct((8, n), F32),
                   jax.ShapeDtypeStruct((n_tiles, 8, LANES), F32)],
        scratch_shapes=[pltpu.VMEM((N_EXPERTS, LANES), F32), pltpu.VMEM((LANES, TILE_TOKENS), F32)],
        compiler_params=_cparams("arbitrary"),
        name="post_mix",
    )(oa, yh, x2, attn_g.reshape(1, WIDTH), w_out_bf16, g2.reshape(1, D_MODEL), wr_hi, wr_lo, br,
      _tile_permutation().T, *_route_constants())


ROUTE_TM = 512
EXPERT_ROW0 = 32
R_E1, R_E2, R_SLOT1, R_SLOT2, R_G1, R_G2 = 0, 1, 2, 3, 4, 5
T_CARRY, T_ROWS, T_OFF = 0, 1, 2
SEG_END = 0
TOTAL_LANE = N_EXPERTS
RUN_ALIGN = 8
LOCAL_ROWS = 2 * ROUTE_TM + N_EXPERTS * RUN_ALIGN


def _route_tile(lg, live, tri_ref, lower_ref, rec_ref, tab_ref, carry_ref):
    i = pl.program_id(0)

    @pl.when(i == 0)
    def _():
        carry_ref[...] = jnp.zeros_like(carry_ref)

    tm = lg.shape[1]
    sub8 = lax.broadcasted_iota(I32, (8, tm), 0).astype(F32)
    big = 8.0
    gmask = sub8 < N_GROUPS
    gl = jnp.where(gmask, lg[0:8], NEG)
    gmax = jnp.max(gl, axis=0, keepdims=True)
    gsel = jnp.min(jnp.where(gmask & (gl == gmax), sub8, big), axis=0, keepdims=True)
    gsum = jnp.sum(jnp.where(gmask, jnp.exp(gl - gmax), 0.0), axis=0, keepdims=True)
    w_g = 1.0 / gsum
    el = jnp.zeros((EXPERTS_PER_GROUP, tm), F32)
    for g in range(N_GROUPS):
        lo = EXPERT_ROW0 + g * EXPERTS_PER_GROUP
        el = jnp.where(gsel == g, lg[lo:lo + EXPERTS_PER_GROUP], el)
    v1 = jnp.max(el, axis=0, keepdims=True)
    i1 = jnp.min(jnp.where(el == v1, sub8, big), axis=0, keepdims=True)
    el2 = jnp.where(sub8 == i1, NEG, el)
    v2 = jnp.max(el2, axis=0, keepdims=True)
    i2 = jnp.min(jnp.where((el2 == v2) & (sub8 != i1), sub8, big), axis=0, keepdims=True)
    ex = jnp.exp(v2 - v1)
    den = 1.0 / (1.0 + ex)
    g1 = w_g * den
    g2 = w_g * ex * den
    e1 = gsel * EXPERTS_PER_GROUP + i1
    e2 = gsel * EXPERTS_PER_GROUP + i2
    sub_e = lax.broadcasted_iota(I32, (N_EXPERTS, tm), 0).astype(F32)
    oh1 = sub_e == e1
    oh2 = sub_e == e2
    onehot = (oh1 | oh2).astype(BF16)
    before = jnp.dot(onehot, tri_ref[...], preferred_element_type=F32)
    count = jnp.sum(onehot.astype(F32), axis=1, keepdims=True)
    units = jnp.floor((count + (RUN_ALIGN - 1)) * (1.0 / RUN_ALIGN)) * live
    units = jnp.broadcast_to(units, (N_EXPERTS, LANES))
    rows = units * RUN_ALIGN
    off = RUN_ALIGN * jnp.dot(lower_ref[...], units.astype(BF16), preferred_element_type=F32)
    place = off[:, 0:1] + before
    slot1 = jnp.sum(jnp.where(oh1, place, 0.0), axis=0, keepdims=True)
    slot2 = jnp.sum(jnp.where(oh2, place, 0.0), axis=0, keepdims=True)
    total = jnp.sum(rows, axis=0, keepdims=True)
    eye = (lax.broadcasted_iota(I32, (N_EXPERTS, LANES), 0) == lax.broadcasted_iota(I32, (N_EXPERTS, LANES), 1))
    lane1 = lax.broadcasted_iota(I32, (1, LANES), 1)
    as_row = lambda col: jnp.where(lane1 == TOTAL_LANE, total,
                                   jnp.sum(jnp.where(eye, col, 0.0), axis=0, keepdims=True))
    sub = lax.broadcasted_iota(I32, (8, LANES), 0)
    tab_ref[0] = jnp.where(sub == T_CARRY, as_row(carry_ref[...]),
                           jnp.where(sub == T_ROWS, as_row(rows), jnp.where(sub == T_OFF, as_row(off), 0.0)))
    carry_ref[...] += rows
    rec = jnp.zeros((8, tm), F32)
    for slot, val in ((R_E1, e1), (R_E2, e2), (R_SLOT1, slot1),
                      (R_SLOT2, slot2), (R_G1, g1), (R_G2, g2)):
        rec = jnp.where(sub8 == slot, val, rec)
    rec_ref[...] = rec


def _route_constants():
    r = jnp.arange(ROUTE_TM)
    tri = (r[:, None] < r[None, :]).astype(BF16)
    e = jnp.arange(N_EXPERTS)
    lower = (e[:, None] > e[None, :]).astype(BF16)
    return tri, lower


P_START, P_ROWS, P_OFF = 0, 1, 2


def _plan_kernel(carry_ref, rows_ref, off_ref, upper_ref, plan_ref, seg_ref):
    n_tiles = carry_ref.shape[0]
    lane = lax.broadcasted_iota(I32, (1, LANES), 1)
    last = pl.ds(n_tiles - 1, 1)
    totals = jnp.where(lane < N_EXPERTS, carry_ref[last, :] + rows_ref[last, :], 0.0)
    blocks = jnp.floor((totals + (FFN_BLOCK - 1)) * (1.0 / FFN_BLOCK))
    starts = FFN_BLOCK * jnp.dot(jnp.broadcast_to(blocks, (8, LANES)).astype(BF16), upper_ref[...],
                                 preferred_element_type=F32)[0:1]
    seg_ref[...] = jnp.broadcast_to(starts + blocks * FFN_BLOCK, seg_ref.shape)
    plan_ref[P_START] = (carry_ref[...] + starts).astype(I32)
    plan_ref[P_ROWS] = rows_ref[...].astype(I32)
    plan_ref[P_OFF] = off_ref[...].astype(I32)


def run_plan(tab):
    n_tiles = tab.shape[0]
    ln = jnp.arange(LANES)
    upper = (ln[:, None] < ln[None, :]).astype(BF16)
    return pl.pallas_call(
        _plan_kernel,
        out_shape=[jax.ShapeDtypeStruct((3, n_tiles, LANES), I32),
                   jax.ShapeDtypeStruct((8, LANES), F32)],
        compiler_params=pltpu.CompilerParams(vmem_limit_bytes=VMEM_LIMIT),
        name="run_plan",
    )(tab[:, T_CARRY], tab[:, T_ROWS], tab[:, T_OFF], upper)


RUN_PIECE = 64
SMALL_PIECES = (32, 16, 8)
TOTAL_PIECES = (1024, 512, 256, 128, 64, 32, 16, 8)
SORT_CHUNK = 256


def _for_each_run_piece(plan_ref, tile, n_tiles, fn):
    plane = n_tiles * LANES

    def per_expert(e, c):
        idx = tile * LANES + e
        start = plan_ref[P_START * plane + idx]
        rows = plan_ref[P_ROWS * plane + idx]
        off = plan_ref[P_OFF * plane + idx]

        def big(j, c2):
            fn(pl.multiple_of(off + j * RUN_PIECE, RUN_ALIGN), pl.multiple_of(start + j * RUN_PIECE, RUN_ALIGN),
               RUN_PIECE)
            return c2
        lax.fori_loop(0, lax.shift_right_logical(rows, 6), big, 0)
        for size in SMALL_PIECES:
            done = jnp.bitwise_and(rows, -2 * size)

            @pl.when(jnp.bitwise_and(rows, size) != 0)
            def _():
                fn(pl.multiple_of(off + done, RUN_ALIGN), pl.multiple_of(start + done, RUN_ALIGN), size)
        return c
    lax.fori_loop(0, N_EXPERTS, per_expert, 0)


def _for_each_total_piece(plan_ref, tile, n_tiles, fn):
    total = plan_ref[P_OFF * n_tiles * LANES + tile * LANES + TOTAL_LANE]
    for size in TOTAL_PIECES:
        @pl.when(jnp.bitwise_and(total, size) != 0)
        def _():
            fn(size)


def _dispatch_kernel(plan_ref, ends_ref, nv_ref, h2_ref, rec_ref, xs_ref, zero_buf, stage, sem, zsem):
    i = pl.program_id(0)
    n_steps = pl.num_programs(0)
    n_blocks = xs_ref.shape[0] // FFN_BLOCK

    @pl.when(i == 0)
    def _():
        zero_buf[...] = jnp.zeros_like(zero_buf)

        def zero_copy(row0):
            return pltpu.make_async_copy(zero_buf, xs_ref.at[pl.ds(row0, FFN_BLOCK)], zsem)

        def seg_end(e):
            return ends_ref[e], ends_ref[e] > jnp.where(e > 0, ends_ref[jnp.maximum(e - 1, 0)], 0)

        def start_e(e, c):
            end, nonempty = seg_end(e)

            @pl.when(nonempty)
            def _():
                zero_copy(pl.multiple_of(end - FFN_BLOCK, FFN_BLOCK)).start()
            return c
        lax.fori_loop(0, N_EXPERTS, start_e, 0)

        def start_b(blk, c):
            zero_copy(pl.multiple_of(blk * FFN_BLOCK, FFN_BLOCK)).start()
            return c
        lax.fori_loop(nv_ref[0], n_blocks, start_b, 0)

        def wait_e(e, c):
            _, nonempty = seg_end(e)

            @pl.when(nonempty)
            def _():
                zero_copy(0).wait()
            return c
        lax.fori_loop(0, N_EXPERTS, wait_e, 0)

        def wait_b(blk, c):
            zero_copy(0).wait()
            return c
        lax.fori_loop(nv_ref[0], n_blocks, wait_b, 0)

    slot = i % 2
    s1 = rec_ref[R_SLOT1:R_SLOT1 + 1, :]
    s2 = rec_ref[R_SLOT2:R_SLOT2 + 1, :]
    half = PACKED_W
    for c in range(LOCAL_ROWS // SORT_CHUNK):
        rowf = (lax.broadcasted_iota(I32, (SORT_CHUNK, ROUTE_TM), 0) + c * SORT_CHUNK).astype(F32)
        sel = ((rowf == s1) | (rowf == s2)).astype(BF16)
        pick = lambda cols: jnp.dot(sel, h2_ref[:, cols], preferred_element_type=F32)
        stage[slot, c * SORT_CHUNK:(c + 1) * SORT_CHUNK, :] = _pack_pair(pick(slice(0, half)),
                                                                         pick(slice(half, 2 * half)))

    def start_piece(lrow, grow, size):
        pltpu.make_async_copy(stage.at[slot, pl.ds(lrow, size)], xs_ref.at[pl.ds(grow, size)],
                              sem.at[slot]).start()
    _for_each_run_piece(plan_ref, i, n_steps, start_piece)

    def wait_pieces(tile, s):
        def wait_piece(size):
            pltpu.make_async_copy(stage.at[s, pl.ds(0, size)], xs_ref.at[pl.ds(0, size)], sem.at[s]).wait()
        _for_each_total_piece(plan_ref, tile, n_steps, wait_piece)

    @pl.when(i > 0)
    def _():
        wait_pieces(i - 1, 1 - slot)

    @pl.when(i + 1 == n_steps)
    def _():
        wait_pieces(i, slot)


def dispatch(h2, rec, plan, seg_ends, n_valid, n_rows):
    n = h2.shape[0]
    return pl.pallas_call(
        _dispatch_kernel,
        grid_spec=pltpu.PrefetchScalarGridSpec(
            num_scalar_prefetch=3,
            grid=(n // ROUTE_TM,),
            in_specs=[pl.BlockSpec((ROUTE_TM, D_MODEL), lambda i, *_: (i, 0)),
                      pl.BlockSpec((8, ROUTE_TM), lambda i, *_: (0, i))],
            out_specs=pl.BlockSpec(memory_space=pl.ANY),
            scratch_shapes=[pltpu.VMEM((FFN_BLOCK, PACKED_W), U32),
                            pltpu.VMEM((2, LOCAL_ROWS, PACKED_W), U32),
                            pltpu.SemaphoreType.DMA((2,)),
                            pltpu.SemaphoreType.DMA(())],
        ),
        out_shape=jax.ShapeDtypeStruct((n_rows, PACKED_W), U32),
        compiler_params=_cparams("arbitrary"),
        name="dispatch",
    )(plan, seg_ends, n_valid, h2, rec)


def _ffn_kernel(be_ref, nv_ref, xs_ref, wg_ref, wu_ref, wd_ref, ys_ref, wg_sc, wu_sc, wd_sc):
    i = pl.program_id(0)
    prev = be_ref[jnp.maximum(i - 1, 0)]
    fresh = (i == 0) | (be_ref[i] != prev)

    @pl.when(fresh)
    def _():
        wg_sc[...] = wg_ref[0].astype(BF16)
        wu_sc[...] = wu_ref[0].astype(BF16)
        wd_sc[...] = wd_ref[0].astype(BF16)

    @pl.when(i < nv_ref[0])
    def _():
        xa, xb = (t.astype(BF16) for t in _unpack_halves(xs_ref[...]))
        half = PACKED_W
        gate = (jnp.dot(xa, wg_sc[:half, :], preferred_element_type=F32)
                + jnp.dot(xb, wg_sc[half:, :], preferred_element_type=F32))
        up = (jnp.dot(xa, wu_sc[:half, :], preferred_element_type=F32)
              + jnp.dot(xb, wu_sc[half:, :], preferred_element_type=F32))
        hid = (gate * (1.0 / (1.0 + jnp.exp(-gate))) * up).astype(BF16)
        ys_ref[...] = _pack_halves(jnp.dot(hid, wd_sc[...], preferred_element_type=F32))

    @pl.when(i >= nv_ref[0])
    def _():
        ys_ref[...] = jnp.zeros_like(ys_ref)


def expert_ffn(xs, block_expert, n_valid, w_gate, w_up, w_down):
    n_rows = xs.shape[0]
    n_blocks = n_rows // FFN_BLOCK
    wmap = lambda i, be, nv: (be[i], 0, 0)
    return pl.pallas_call(
        _ffn_kernel,
        grid_spec=pltpu.PrefetchScalarGridSpec(
            num_scalar_prefetch=2,
            grid=(n_blocks,),
            in_specs=[pl.BlockSpec((FFN_BLOCK, PACKED_W), lambda i, be, nv: (jnp.minimum(i, nv[0] - 1), 0)),
                      pl.BlockSpec((1, D_MODEL, D_EXPERT), wmap),
                      pl.BlockSpec((1, D_MODEL, D_EXPERT), wmap),
                      pl.BlockSpec((1, D_EXPERT, D_MODEL), wmap)],
            out_specs=pl.BlockSpec((FFN_BLOCK, PACKED_W), lambda i, be, nv: (i, 0)),
            scratch_shapes=[pltpu.VMEM((D_MODEL, D_EXPERT), BF16),
                            pltpu.VMEM((D_MODEL, D_EXPERT), BF16),
                            pltpu.VMEM((D_EXPERT, D_MODEL), BF16)],
        ),
        out_shape=jax.ShapeDtypeStruct((n_rows, PACKED_W), U32),
        compiler_params=_cparams("arbitrary"),
        name="expert_ffn",
    )(block_expert, n_valid, xs, w_gate, w_up, w_down)


def _combine_kernel(plan_ref, x1_ref, rec_ref, gf_ref, ys_ref, out_ref, buf, sem):
    i = pl.program_id(0)
    n_steps = pl.num_programs(0)
    slot = i % 2

    def fetch_tile(tile, s):
        def start_piece(lrow, grow, size):
            pltpu.make_async_copy(ys_ref.at[pl.ds(grow, size)], buf.at[s, pl.ds(lrow, size)], sem.at[s]).start()
        _for_each_run_piece(plan_ref, tile, n_steps, start_piece)

    @pl.when(i == 0)
    def _():
        buf[...] = jnp.zeros_like(buf)
        fetch_tile(0, 0)

    @pl.when(i + 1 < n_steps)
    def _():
        fetch_tile(i + 1, 1 - slot)

    def wait_piece(size):
        pltpu.make_async_copy(ys_ref.at[pl.ds(0, size)], buf.at[slot, pl.ds(0, size)], sem.at[slot]).wait()
    _for_each_total_piece(plan_ref, i, n_steps, wait_piece)

    rowf = lax.broadcasted_iota(I32, (LOCAL_ROWS, ROUTE_TM), 0).astype(F32)
    sel = (jnp.where(rowf == rec_ref[R_SLOT1:R_SLOT1 + 1, :], rec_ref[R_G1:R_G1 + 1, :], 0.0)
           + jnp.where(rowf == rec_ref[R_SLOT2:R_SLOT2 + 1, :], rec_ref[R_G2:R_G2 + 1, :], 0.0)).astype(BF16)
    ya, yb = (t.astype(BF16) for t in _unpack_halves(buf[slot]))
    half = PACKED_W
    tn = (((0,), (0,)), ((), ()))
    xa = x1_ref[:, :half] + lax.dot_general(sel, ya, tn, preferred_element_type=F32)
    xb = x1_ref[:, half:] + lax.dot_general(sel, yb, tn, preferred_element_type=F32)
    ms = (jnp.sum(xa * xa, axis=-1, keepdims=True) + jnp.sum(xb * xb, axis=-1, keepdims=True)) * (1.0 / D_MODEL)
    scale = lax.rsqrt(ms + EPS)
    out_ref[:, :half] = xa * scale * gf_ref[:, :half]
    out_ref[:, half:] = xb * scale * gf_ref[:, half:]


def combine(x1, rec, norm_f_g, ys, plan):
    n = x1.shape[0]
    tm = ROUTE_TM
    return pl.pallas_call(
        _combine_kernel,
        grid_spec=pltpu.PrefetchScalarGridSpec(
            num_scalar_prefetch=1,
            grid=(n // tm,),
            in_specs=[pl.BlockSpec((tm, D_MODEL), lambda i, p: (i, 0)),
                      pl.BlockSpec((8, tm), lambda i, p: (0, i)),
                      pl.BlockSpec((1, D_MODEL), lambda i, p: (0, 0)),
                      pl.BlockSpec(memory_space=pl.ANY)],
            out_specs=pl.BlockSpec((tm, D_MODEL), lambda i, p: (i, 0)),
            scratch_shapes=[pltpu.VMEM((2, LOCAL_ROWS, PACKED_W), U32),
                            pltpu.SemaphoreType.DMA((2,))],
        ),
        out_shape=jax.ShapeDtypeStruct((n, D_MODEL), F32),
        compiler_params=_cparams("arbitrary"),
        name="combine",
    )(plan, x1, rec, norm_f_g.reshape(1, D_MODEL), ys)


def _router_weights(w_group, b_group, w_router, b_router):
    w = jnp.zeros((LANES, D_MODEL), F32)
    w = w.at[:N_GROUPS].set(w_group.T).at[EXPERT_ROW0:EXPERT_ROW0 + N_EXPERTS].set(w_router.T)
    b = jnp.zeros((LANES, 1), F32)
    b = b.at[:N_GROUPS, 0].set(b_group).at[EXPERT_ROW0:EXPERT_ROW0 + N_EXPERTS, 0].set(b_router)
    hi = w.astype(BF16)
    lo = (w - hi.astype(F32)).astype(BF16)
    return hi, lo, b


def _sorted_rows_bound(n_tokens):
    worst = (2 * n_tokens + (n_tokens // ROUTE_TM) * N_EXPERTS * (RUN_ALIGN - 1)
             + N_EXPERTS * (FFN_BLOCK - 1))
    return -(-worst // FFN_BLOCK) * FFN_BLOCK


def _block_plan(seg, n_rows):
    seg_ends = seg[SEG_END, :N_EXPERTS].astype(I32)
    n_blocks = n_rows // FFN_BLOCK
    blk_start = jnp.arange(n_blocks, dtype=I32) * FFN_BLOCK
    block_expert = jnp.minimum(jnp.sum((seg_ends[None, :] <= blk_start[:, None]).astype(I32), axis=1),
                               N_EXPERTS - 1)
    n_valid = seg_ends[-1:] // FFN_BLOCK
    return seg_ends, block_expert, n_valid


def kernel(x, norm1_g, w_in, attn_norm_g, hgrn_gamma, hgrn_norm_g, w_out, norm2_g, w_group, b_group,
           w_router, b_router, w_gate, w_up, w_down, norm_f_g):
    batch, seq, d = x.shape
    assert seq == SEQ and d == D_MODEL and norm1_g.shape[0] == 1
    n = batch * seq
    x2 = x.reshape(n, d)
    qkv, hg = in_proj(x2, norm1_g[0], w_in[0].astype(BF16), batch)
    oa = attention(qkv)
    yh = hgrn(hg.reshape(batch, seq, HG_W), hgrn_gamma, hgrn_norm_g[0]).reshape(n, WIDTH)
    wr_hi, wr_lo, br = _router_weights(w_group[0], b_group[0], w_router[0], b_router[0])
    x1, h2, rec, tab = post_mix(oa, yh, x2, attn_norm_g[0], w_out[0].astype(BF16), norm2_g[0],
                                wr_hi, wr_lo, br)
    plan, seg = run_plan(tab)
    plan = plan.reshape(-1)
    n_rows = _sorted_rows_bound(n)
    seg_ends, block_expert, n_valid = _block_plan(seg, n_rows)
    xs = dispatch(h2, rec, plan, seg_ends, n_valid, n_rows)
    ys = expert_ffn(xs, block_expert, n_valid, w_gate[0], w_up[0], w_down[0])
    return combine(x1, rec, norm_f_g, ys, plan).reshape(batch, seq, d)
```

```python
import jax
import jax.numpy as jnp
from jax import lax
from jax.experimental import pallas as pl
from jax.experimental.pallas import tpu as pltpu

F32 = jnp.float32
BF16 = jnp.bfloat16
I32 = jnp.int32
U32 = jnp.uint32

D_MODEL = 1024
HEAD_DIM = 64
N_HEADS = 8
WIDTH = N_HEADS * HEAD_DIM
QKV_W = 3 * WIDTH
HG_W = 4 * WIDTH
SEQ = 2048
ATTN_BLOCK = 128
DILATIONS = (1, 4, 16)
N_RES = 16
RES_LEN = SEQ // N_RES
TILE_TOKENS = 512
U_PER_TILE = TILE_TOKENS // N_RES
HGRN_CHUNK = 32
HGRN_SUPER = 256
HGRN_GROUP = 8
N_GROUPS = 4
EXPERTS_PER_GROUP = 8
N_EXPERTS = 32
D_EXPERT = 512
FFN_BLOCK = 512
PACKED_W = D_MODEL // 2
EPS = 1e-6
NEG = -1e30
LOG2E = 1.4426950408889634
LANES = 128
VMEM_LIMIT = 56 * 1024 * 1024


def _cparams(*sem):
    return pltpu.CompilerParams(dimension_semantics=sem, vmem_limit_bytes=VMEM_LIMIT)


def _pack_pair(a, b):
    hi = pltpu.bitcast(a.astype(BF16).astype(F32), U32)
    lo = pltpu.bitcast(b.astype(BF16).astype(F32), U32)
    return hi | (lo >> 16)


def _pack_halves(x):
    w = x.shape[1] // 2
    return _pack_pair(x[:, :w], x[:, w:])


def _unpack_halves(p):
    return pltpu.bitcast(p & jnp.uint32(0xFFFF0000), F32), pltpu.bitcast(p << 16, F32)


N_LANE_TILES = WIDTH // LANES


def _tile_permutation():
    i = jnp.arange(TILE_TOKENS)
    src = N_RES * (i % U_PER_TILE) + i // U_PER_TILE
    return (src[:, None] == jnp.arange(TILE_TOKENS)[None, :]).astype(BF16)


def _in_proj_kernel(x_ref, g_ref, w_ref, x4_ref, qkv_ref, hg_ref, xp, sem):
    i = pl.program_id(0)
    n_steps = pl.num_programs(0)
    tiles_per_b = SEQ // TILE_TOKENS
    slot = i % 2

    def fetch(tile, s):
        b = tile // tiles_per_b
        u0 = (tile % tiles_per_b) * U_PER_TILE
        return [pltpu.make_async_copy(x4_ref.at[b, pl.ds(u0, U_PER_TILE), r, :],
                                      xp.at[s, pl.ds(r * U_PER_TILE, U_PER_TILE), :], sem.at[s])
                for r in range(N_RES)]

    @pl.when(i == 0)
    def _():
        for cp in fetch(0, 0):
            cp.start()

    @pl.when(i + 1 < n_steps)
    def _():
        for cp in fetch(i + 1, 1 - slot):
            cp.start()

    def normed(x):
        ms = jnp.mean(x * x, axis=-1, keepdims=True)
        return (x * lax.rsqrt(ms + EPS) * g_ref[...]).astype(BF16)

    h = normed(x_ref[...])
    for j in range(HG_W // WIDTH):
        wsl = slice(QKV_W + j * WIDTH, QKV_W + (j + 1) * WIDTH)
        hg_ref[:, j * WIDTH:(j + 1) * WIDTH] = jnp.dot(h, w_ref[:, wsl], preferred_element_type=F32).astype(BF16)
    for cp in fetch(i, slot):
        cp.wait()
    hp = normed(xp[slot])
    for c in range(QKV_W // WIDTH):
        res = jnp.dot(hp, w_ref[:, c * WIDTH:(c + 1) * WIDTH], preferred_element_type=F32)
        for r in range(N_RES):
            for g in range(N_LANE_TILES):
                qkv_ref[r, g, c] = res[r * U_PER_TILE:(r + 1) * U_PER_TILE, g * LANES:(g + 1) * LANES]


def in_proj(x2, g, w_bf16, batch):
    n = x2.shape[0]
    tiles_per_b = SEQ // TILE_TOKENS
    x4 = x2.reshape(batch, RES_LEN, N_RES, D_MODEL)
    return pl.pallas_call(
        _in_proj_kernel,
        grid=(n // TILE_TOKENS,),
        in_specs=[
            pl.BlockSpec((TILE_TOKENS, D_MODEL), lambda i: (i, 0)),
            pl.BlockSpec((1, D_MODEL), lambda i: (0, 0)),
            pl.BlockSpec((D_MODEL, QKV_W + HG_W), lambda i: (0, 0)),
            pl.BlockSpec(memory_space=pl.ANY),
        ],
        out_specs=[
            pl.BlockSpec((None, N_RES, N_LANE_TILES, 3, U_PER_TILE, LANES),
                         lambda i: (i // tiles_per_b, 0, 0, 0, i % tiles_per_b, 0)),
            pl.BlockSpec((TILE_TOKENS, HG_W), lambda i: (i, 0)),
        ],
        out_shape=[
            jax.ShapeDtypeStruct((batch, N_RES, N_LANE_TILES, 3, RES_LEN, LANES), F32),
            jax.ShapeDtypeStruct((n, HG_W), BF16),
        ],
        scratch_shapes=[pltpu.VMEM((2, TILE_TOKENS, D_MODEL), F32), pltpu.SemaphoreType.DMA((2,))],
        compiler_params=_cparams("arbitrary"),
        name="in_proj",
    )(x2, g.reshape(1, D_MODEL), w_bf16, x4)


ATTN_GROUP16 = 16
ATTN_GROUP1 = 5
HEADS_PER_TILE = LANES // HEAD_DIM


def _attn_bias_tables():
    slopes = 2.0 ** (-8.0 * jnp.arange(1, N_HEADS + 1, dtype=F32) / N_HEADS)

    def table(qpos, kpos, dil):
        sd = qpos[:, None] - kpos[None, :]
        b = -slopes[:, None, None] * (sd * dil).astype(F32)[None] * LOG2E
        return jnp.where(((sd >= 0) & (sd <= ATTN_BLOCK))[None], b, NEG)

    q = jnp.arange(ATTN_BLOCK)
    k = jnp.arange(2 * ATTN_BLOCK)
    q1 = 16 * (q % 8) + q // 8 + ATTN_BLOCK
    k1 = 16 * (k % 16) + k // 16
    q4 = 4 * (q % 32) + q // 32 + ATTN_BLOCK
    k4 = 4 * (k % 64) + k // 64
    q16 = q + ATTN_BLOCK
    none = jnp.full((N_HEADS, ATTN_BLOCK, 2 * ATTN_BLOCK), NEG, F32)
    tabs = [jnp.concatenate([table(q1, k1, 1), table(q1, q1, 1)], axis=-1),
            jnp.concatenate([table(q4, k4, 4), table(q4, q4, 4)], axis=-1),
            jnp.concatenate([none, table(q16, q16, 16)], axis=-1)]
    return jnp.stack(tabs, axis=0)


def _attn_bias(bias_ref, d_idx, with_prev):
    ksl = slice(0, 2 * ATTN_BLOCK) if with_prev else slice(2 * ATTN_BLOCK, 3 * ATTN_BLOCK)
    return jnp.concatenate([bias_ref[d_idx, h, :, ksl] for h in range(HEADS_PER_TILE)], axis=0)


def _attn_blocks(blocks):
    nb = ATTN_BLOCK
    lane = lax.broadcasted_iota(I32, (nb, LANES), 1)
    head0 = lane < HEAD_DIM
    biases = [blk[3] for blk in blocks]
    blocks = [blk[:3] for blk in blocks]
    qs = [q * (HEAD_DIM ** -0.5 * LOG2E) for q, _, _ in blocks]
    qq = [jnp.concatenate([jnp.where(head0, q, 0.0), jnp.where(head0, 0.0, q)], axis=0).astype(BF16)
          for q in qs]
    ks = [k.astype(BF16) for _, k, _ in blocks]
    vs = [v.astype(BF16) for _, _, v in blocks]
    v_ones = [jnp.concatenate([v, jnp.ones_like(v)], axis=-1) for v in vs]
    s = [lax.dot_general(q2, k, (((1,), (1,)), ((), ())), preferred_element_type=F32) + bias
         for q2, k, bias in zip(qq, ks, biases)]
    m = [jnp.max(x, axis=-1, keepdims=True) for x in s]
    p = [jnp.exp2(x - mx).astype(BF16) for x, mx in zip(s, m)]
    acc = [jnp.dot(px, vo, preferred_element_type=F32) for px, vo in zip(p, v_ones)]
    head0_wide = jnp.concatenate([head0, head0], axis=-1)
    outs = []
    for a, mx in zip(acc, m):
        ol = jnp.where(head0_wide, a[:nb], a[nb:])
        l = ol[:, LANES:]
        outs.append((ol[:, :LANES] * (1.0 / l), jnp.where(head0, mx[:nb], mx[nb:]) + jnp.log2(l)))
    return outs


def _attn_kernel(qkv_ref, bias_ref, o_ref, o4_ref, l4_ref, o16_ref, l16_ref, fin_ref):
    grp = ATTN_GROUP16

    def body16(j, c):
        rs = [j * grp + a for a in range(grp)]
        bias = _attn_bias(bias_ref, 2, False)
        res = _attn_blocks([(qkv_ref[r, 0], qkv_ref[r, 1], qkv_ref[r, 2], bias) for r in rs])
        for r, (o, l) in zip(rs, res):
            o16_ref[r] = o
            l16_ref[r] = l
        return c
    for j in range(N_RES // grp):
        body16(j, 0)

    def gather4(c, rho, u0, nu):
        return jnp.concatenate([qkv_ref[rho + 4 * a, c, pl.ds(u0, nu), :] for a in range(4)], axis=0)

    def store4(rho, u0, o, l):
        for a in range(4):
            o4_ref[rho + 4 * a, pl.ds(u0, 32), :] = o[a * 32:(a + 1) * 32]
            l4_ref[rho + 4 * a, pl.ds(u0, 32), :] = l[a * 32:(a + 1) * 32]

    def body4(j, c):
        bias = _attn_bias(bias_ref, 1, True)
        todo = [(2 * j + a, 32 * n) for a in range(2) for n in range(1, 4)]
        res = _attn_blocks([(gather4(0, rho, u0, 32), gather4(1, rho, u0 - 32, 64),
                             gather4(2, rho, u0 - 32, 64), bias) for rho, u0 in todo])
        for (rho, u0), (o, l) in zip(todo, res):
            store4(rho, u0, o, l)
        return c
    for j in range(2):
        body4(j, 0)

    def gather1(ref, u0, nu, *lead):
        return jnp.concatenate([ref[(r,) + lead + (pl.ds(u0, nu), slice(None))] for r in range(N_RES)],
                               axis=0)

    def merge_and_store(u0, o1, l1):
        l4 = gather1(l4_ref, u0, 8)
        l16 = gather1(l16_ref, u0, 8)
        mx = jnp.maximum(jnp.maximum(l1, l4), l16)
        e1 = jnp.exp2(l1 - mx)
        e4 = jnp.exp2(l4 - mx)
        e16 = jnp.exp2(l16 - mx)
        inv = 1.0 / (e1 + e4 + e16)
        o = (e1 * inv) * o1 + (e4 * inv) * gather1(o4_ref, u0, 8) + (e16 * inv) * gather1(o16_ref, u0, 8)
        for r in range(N_RES):
            fin_ref[r, pl.ds(u0, 8), :] = o[r * 8:(r + 1) * 8]

    bias4 = _attn_bias(bias_ref, 1, False)
    first = [tuple(gather4(c, rho, 0, 32) for c in range(3)) + (bias4,) for rho in range(4)]
    first.append(tuple(gather1(qkv_ref, 0, 8, c) for c in range(3)) + (_attn_bias(bias_ref, 0, False),))
    res = _attn_blocks(first)
    for rho in range(4):
        store4(rho, 0, *res[rho])
    merge_and_store(0, *res[4])

    n_blocks1 = RES_LEN // 8 - 1

    def body1(j, c):
        u0s = [(1 + j * ATTN_GROUP1 + a) * 8 for a in range(ATTN_GROUP1)]
        bias = _attn_bias(bias_ref, 0, True)
        res = _attn_blocks([(gather1(qkv_ref, u0, 8, 0), gather1(qkv_ref, u0 - 8, 16, 1),
                             gather1(qkv_ref, u0 - 8, 16, 2), bias) for u0 in u0s])
        for u0, (o, l) in zip(u0s, res):
            merge_and_store(u0, o, l)
        return c
    for j in range(n_blocks1 // ATTN_GROUP1):
        body1(j, 0)

    def emit(r, c):
        o_ref[r] = fin_ref[r].astype(o_ref.dtype)
        return c
    lax.fori_loop(0, N_RES, emit, 0)


def attention(qkv):
    batch = qkv.shape[0]
    scratch = pltpu.VMEM((N_RES, RES_LEN, LANES), F32)
    return pl.pallas_call(
        _attn_kernel,
        grid=(batch, N_LANE_TILES),
        in_specs=[
            pl.BlockSpec((None, N_RES, None, 3, RES_LEN, LANES), lambda b, g: (b, 0, g, 0, 0, 0)),
            pl.BlockSpec((len(DILATIONS), HEADS_PER_TILE, ATTN_BLOCK, 3 * ATTN_BLOCK),
                         lambda b, g: (0, g, 0, 0)),
        ],
        out_specs=pl.BlockSpec((None, N_RES, None, RES_LEN, LANES), lambda b, g: (b, 0, g, 0, 0)),
        out_shape=jax.ShapeDtypeStruct((batch, N_RES, N_LANE_TILES, RES_LEN, LANES), BF16),
        scratch_shapes=[scratch] * 5,
        compiler_params=_cparams("arbitrary", "arbitrary"),
        name="dilated_attention",
    )(qkv, _attn_bias_tables())


def _split2(x):
    a = x.astype(BF16)
    b = (x - a.astype(F32)).astype(BF16)
    return a, b


def _hgrn_kernel(q_ref, f_ref, i_ref, g_ref, gamma_ref, ng_ref, tri_ref, y_ref):
    sup, ch = HGRN_SUPER, HGRN_CHUNK
    n_ch = sup // ch
    gam = gamma_ref[...]
    gmx = jnp.max(gam, axis=0, keepdims=True)
    ge = jnp.exp(gam - gmx)
    lb = ge[0:1] / jnp.sum(ge, axis=0, keepdims=True)
    ng = ng_ref[...]
    tri = tri_ref[...]
    ri = lax.broadcasted_iota(I32, (2 * sup, sup), 0) % sup
    ci = lax.broadcasted_iota(I32, (2 * sup, sup), 1)
    causal_bd2 = (ri // ch == ci // ch) & (ri >= ci)
    head0 = lax.broadcasted_iota(I32, (sup, LANES), 1) < HEAD_DIM
    same_head = (lax.broadcasted_iota(I32, (LANES, LANES), 0) // HEAD_DIM
                 == lax.broadcasted_iota(I32, (LANES, LANES), 1) // HEAD_DIM)

    def sigmoid(x):
        return 1.0 / (1.0 + jnp.exp(-x))

    def dot_nt(a, b):
        return lax.dot_general(a, b, (((1,), (1,)), ((), ())), preferred_element_type=F32)

    def dot_tn(a, b):
        return lax.dot_general(a, b, (((0,), (0,)), ((), ())), preferred_element_type=F32)

    def body(j, state):
        grp = range(HGRN_GROUP)
        rows = [pl.ds((j * HGRN_GROUP + a) * sup, sup) for a in grp]
        fr = [f_ref[r, :].astype(F32) for r in rows]
        t = [jnp.exp(-jnp.abs(x)) for x in fr]
        rcp = [1.0 / (1.0 + x) for x in t]
        sig = [jnp.where(f >= 0, r, x * r) for f, r, x in zip(fr, rcp, t)]
        nsig = [jnp.where(f >= 0, x * r, r) for f, r, x in zip(fr, rcp, t)]
        logf = [jnp.log(lb + (1.0 - lb) * x) for x in sig]
        key = [(1.0 - lb) * x for x in nsig]
        parts = [_split2(x) for x in logf]
        b = [sum(jnp.dot(tri, p, preferred_element_type=F32) for p in ps) for ps in parts]
        totals = [[x[(c + 1) * ch - 1:(c + 1) * ch, :] for c in range(n_ch)] for x in b]
        b_last = [jnp.concatenate([jnp.broadcast_to(t_, (ch, LANES)) for t_ in ts], axis=0) for ts in totals]
        pad = jnp.zeros((LANES - n_ch, LANES), F32)
        decay_t = [jnp.exp(jnp.transpose(jnp.concatenate(ts + [pad], axis=0))) for ts in totals]
        qv = [q_ref[r, :].astype(F32) for r in rows]
        q_in = [(x * sigmoid(x) * jnp.exp(bb)).astype(BF16) for x, bb in zip(qv, b)]
        k_in = [(k * jnp.exp(-bb)).astype(BF16) for k, bb in zip(key, b)]
        k_end = [(k * jnp.exp(bl - bb)).astype(BF16) for k, bl, bb in zip(key, b_last, b)]
        vv = [i_ref[r, :] for r in rows]
        q2 = [jnp.concatenate([jnp.where(head0, x, jnp.zeros_like(x)), jnp.where(head0, jnp.zeros_like(x), x)],
                              axis=0) for x in q_in]
        att = [jnp.where(causal_bd2, dot_nt(x, k), 0.0).astype(BF16) for x, k in zip(q2, k_in)]
        intra = [jnp.dot(a_, v, preferred_element_type=F32) for a_, v in zip(att, vv)]
        o_intra = [jnp.where(head0, x[:sup], x[sup:]) for x in intra]
        ds_all = [jnp.concatenate([dot_tn(k[c * ch:(c + 1) * ch], v[c * ch:(c + 1) * ch])
                                   for c in range(n_ch)], axis=1) for k, v in zip(k_end, vv)]
        s_prev = []
        for a in grp:
            for c in range(n_ch):
                s_prev.append(state.astype(BF16))
                ds = jnp.where(same_head, ds_all[a][:, c * LANES:(c + 1) * LANES], 0.0)
                state = decay_t[a][:, c:c + 1] * state + ds
        o_inter = [jnp.concatenate(
            [jnp.dot(q_in[a][c * ch:(c + 1) * ch], s_prev[a * n_ch + c], preferred_element_type=F32)
             for c in range(n_ch)], axis=0) for a in grp]
        o = [x + y for x, y in zip(o_intra, o_inter)]
        sq = [x * x for x in o]
        ss0 = [jnp.sum(jnp.where(head0, x, 0.0), axis=-1, keepdims=True) for x in sq]
        ss1 = [jnp.sum(jnp.where(head0, 0.0, x), axis=-1, keepdims=True) for x in sq]
        ms = [jnp.where(head0, x, y) * (1.0 / HEAD_DIM) for x, y in zip(ss0, ss1)]
        gv = [g_ref[r, :].astype(F32) for r in rows]
        for r, x, m_, g_ in zip(rows, o, ms, gv):
            y_ref[r, :] = (x * lax.rsqrt(m_ + EPS) * ng * (g_ * sigmoid(g_))).astype(y_ref.dtype)
        return state

    state = jnp.zeros((LANES, LANES), F32)
    for j in range(SEQ // (sup * HGRN_GROUP)):
        state = body(j, state)


def hgrn(hg3, gamma, norm_g):
    batch = hg3.shape[0]
    r = jnp.arange(HGRN_SUPER)
    same = (r[:, None] // HGRN_CHUNK) == (r[None, :] // HGRN_CHUNK)
    tri = (same & (r[:, None] >= r[None, :])).astype(BF16)
    col = lambda which: (lambda b, g: (b, 0, which * N_LANE_TILES + g))
    const2 = lambda b, g: (0, 0)
    return pl.pallas_call(
        _hgrn_kernel,
        grid=(batch, N_LANE_TILES),
        in_specs=[
            pl.BlockSpec((None, SEQ, LANES), col(0)),
            pl.BlockSpec((None, SEQ, LANES), col(1)),
            pl.BlockSpec((None, SEQ, LANES), col(2)),
            pl.BlockSpec((None, SEQ, LANES), col(3)),
            pl.BlockSpec((2, LANES), lambda b, g: (0, g)),
            pl.BlockSpec((1, LANES), lambda b, g: (0, g)),
            pl.BlockSpec((HGRN_SUPER, HGRN_SUPER), const2),
        ],
        out_specs=pl.BlockSpec((None, SEQ, LANES), lambda b, g: (b, 0, g)),
        out_shape=jax.ShapeDtypeStruct((batch, SEQ, WIDTH), BF16),
        compiler_params=_cparams("arbitrary", "arbitrary"),
        name="hgrn2",
    )(hg3, hg3, hg3, hg3, gamma.astype(F32), norm_g.reshape(1, WIDTH).astype(F32), tri)


def _post_mix_kernel(oa_ref, yh_ref, x_ref, ag_ref, wo_ref, g2_ref, wrh_ref, wrl_ref, br_ref, perm_t_ref,
                     tri_ref, lower_ref,
                     x1_ref, h2_ref, rec_ref, tab_ref, carry_ref, lg_sc):
    i = pl.program_id(0)

    @pl.when(i == 0)
    def _():
        lg_sc[...] = jnp.zeros_like(lg_sc)
    _route_tile(lg_sc[...], (i > 0).astype(F32), tri_ref, lower_ref, rec_ref, tab_ref, carry_ref)

    oa = jnp.concatenate(
        [jnp.concatenate([oa_ref[r, g] for g in range(N_LANE_TILES)], axis=1)
         for r in range(N_RES)], axis=0).astype(F32)
    ms = jnp.mean(oa * oa, axis=-1, keepdims=True)
    ya = (oa * lax.rsqrt(ms + EPS) * ag_ref[...]).astype(BF16)
    ya = jnp.dot(perm_t_ref[...], ya, preferred_element_type=F32).astype(BF16)
    mix = (jnp.dot(ya, wo_ref[:WIDTH, :], preferred_element_type=F32)
           + jnp.dot(yh_ref[...], wo_ref[WIDTH:, :], preferred_element_type=F32))
    x1 = x_ref[...] + mix
    x1_ref[...] = x1
    ms2 = jnp.mean(x1 * x1, axis=-1, keepdims=True)
    h2 = x1 * lax.rsqrt(ms2 + EPS) * g2_ref[...]
    hi = h2.astype(BF16)
    h2_ref[...] = hi
    lo = (h2 - hi.astype(F32)).astype(BF16)
    nt = (((1,), (1,)), ((), ()))
    wrh = wrh_ref[...]
    lg_sc[...] = (lax.dot_general(wrh, hi, nt, preferred_element_type=F32)
                  + lax.dot_general(wrh, lo, nt, preferred_element_type=F32)
                  + lax.dot_general(wrl_ref[...], hi, nt, preferred_element_type=F32)
                  + br_ref[...])


def post_mix(oa, yh, x2, attn_g, w_out_bf16, g2, wr_hi, wr_lo, br):
    n = x2.shape[0]
    tiles_per_b = SEQ // TILE_TOKENS
    assert ROUTE_TM == TILE_TOKENS
    n_tiles = n // TILE_TOKENS
    cur = lambda i: jnp.minimum(i, n_tiles - 1)
    prev = lambda i: jnp.maximum(i - 1, 0)
    row = lambda w: pl.BlockSpec((TILE_TOKENS, w), lambda i: (cur(i), 0))
    const = lambda r, c: pl.BlockSpec((r, c), lambda i: (0, 0))
    return pl.pallas_call(
        _post_mix_kernel,
        grid=(n_tiles + 1,),
        in_specs=[pl.BlockSpec((None, N_RES, N_LANE_TILES, U_PER_TILE, LANES),
                               lambda i: (cur(i) // tiles_per_b, 0, 0, cur(i) % tiles_per_b, 0)),
                  row(WIDTH), row(D_MODEL),
                  const(1, WIDTH), const(2 * WIDTH, D_MODEL), const(1, D_MODEL),
                  const(LANES, D_MODEL), const(LANES, D_MODEL), const(LANES, 1),
                  const(TILE_TOKENS, TILE_TOKENS), const(ROUTE_TM, ROUTE_TM), const(N_EXPERTS, N_EXPERTS)],
        out_specs=[row(D_MODEL), row(D_MODEL),
                   pl.BlockSpec((8, ROUTE_TM), lambda i: (0, prev(i))),
                   pl.BlockSpec((1, 8, LANES), lambda i: (prev(i), 0, 0))],
        out_shape=[jax.ShapeDtypeStruct((n, D_MODEL), F32),
                   jax.ShapeDtypeStruct((n, D_MODEL), BF16),
                   jax.ShapeDtypeStruct((8, n), F32),
                   jax.ShapeDtypeStruct((n_tiles, 8, LANES), F32)],
        scratch_shapes=[pltpu.VMEM((N_EXPERTS, LANES), F32), pltpu.VMEM((LANES, TILE_TOKENS), F32)],
        compiler_params=_cparams("arbitrary"),
        name="post_mix",
    )(oa, yh, x2, attn_g.reshape(1, WIDTH), w_out_bf16, g2.reshape(1, D_MODEL), wr_hi, wr_lo, br,
      _tile_permutation().T, *_route_constants())


ROUTE_TM = 512
EXPERT_ROW0 = 32
R_E1, R_E2, R_SLOT1, R_SLOT2, R_G1, R_G2 = 0, 1, 2, 3, 4, 5
T_CARRY, T_ROWS, T_OFF = 0, 1, 2
SEG_END = 0
TOTAL_LANE = N_EXPERTS
RUN_ALIGN = 8
LOCAL_ROWS = 2 * ROUTE_TM + N_EXPERTS * RUN_ALIGN


def _route_tile(lg, live, tri_ref, lower_ref, rec_ref, tab_ref, carry_ref):
    i = pl.program_id(0)

    @pl.when(i == 0)
    def _():
        carry_ref[...] = jnp.zeros_like(carry_ref)

    tm = lg.shape[1]
    sub8 = lax.broadcasted_iota(I32, (8, tm), 0).astype(F32)
    big = 8.0
    gmask = sub8 < N_GROUPS
    gl = jnp.where(gmask, lg[0:8], NEG)
    gmax = jnp.max(gl, axis=0, keepdims=True)
    gsel = jnp.min(jnp.where(gmask & (gl == gmax), sub8, big), axis=0, keepdims=True)
    gsum = jnp.sum(jnp.where(gmask, jnp.exp(gl - gmax), 0.0), axis=0, keepdims=True)
    w_g = 1.0 / gsum
    el = jnp.zeros((EXPERTS_PER_GROUP, tm), F32)
    for g in range(N_GROUPS):
        lo = EXPERT_ROW0 + g * EXPERTS_PER_GROUP
        el = jnp.where(gsel == g, lg[lo:lo + EXPERTS_PER_GROUP], el)
    v1 = jnp.max(el, axis=0, keepdims=True)
    i1 = jnp.min(jnp.where(el == v1, sub8, big), axis=0, keepdims=True)
    el2 = jnp.where(sub8 == i1, NEG, el)
    v2 = jnp.max(el2, axis=0, keepdims=True)
    i2 = jnp.min(jnp.where((el2 == v2) & (sub8 != i1), sub8, big), axis=0, keepdims=True)
    ex = jnp.exp(v2 - v1)
    den = 1.0 / (1.0 + ex)
    g1 = w_g * den
    g2 = w_g * ex * den
    e1 = gsel * EXPERTS_PER_GROUP + i1
    e2 = gsel * EXPERTS_PER_GROUP + i2
    sub_e = lax.broadcasted_iota(I32, (N_EXPERTS, tm), 0).astype(F32)
    oh1 = sub_e == e1
    oh2 = sub_e == e2
    onehot = (oh1 | oh2).astype(BF16)
    before = jnp.dot(onehot, tri_ref[...], preferred_element_type=F32)
    count = jnp.sum(onehot.astype(F32), axis=1, keepdims=True)
    units = jnp.floor((count + (RUN_ALIGN - 1)) * (1.0 / RUN_ALIGN)) * live
    units = jnp.broadcast_to(units, (N_EXPERTS, LANES))
    rows = units * RUN_ALIGN
    off = RUN_ALIGN * jnp.dot(lower_ref[...], units.astype(BF16), preferred_element_type=F32)
    place = off[:, 0:1] + before
    slot1 = jnp.sum(jnp.where(oh1, place, 0.0), axis=0, keepdims=True)
    slot2 = jnp.sum(jnp.where(oh2, place, 0.0), axis=0, keepdims=True)
    total = jnp.sum(rows, axis=0, keepdims=True)
    eye = (lax.broadcasted_iota(I32, (N_EXPERTS, LANES), 0) == lax.broadcasted_iota(I32, (N_EXPERTS, LANES), 1))
    lane1 = lax.broadcasted_iota(I32, (1, LANES), 1)
    as_row = lambda col: jnp.where(lane1 == TOTAL_LANE, total,
                                   jnp.sum(jnp.where(eye, col, 0.0), axis=0, keepdims=True))
    sub = lax.broadcasted_iota(I32, (8, LANES), 0)
    tab_ref[0] = jnp.where(sub == T_CARRY, as_row(carry_ref[...]),
                           jnp.where(sub == T_ROWS, as_row(rows), jnp.where(sub == T_OFF, as_row(off), 0.0)))
    carry_ref[...] += rows
    rec = jnp.zeros((8, tm), F32)
    for slot, val in ((R_E1, e1), (R_E2, e2), (R_SLOT1, slot1),
                      (R_SLOT2, slot2), (R_G1, g1), (R_G2, g2)):
        rec = jnp.where(sub8 == slot, val, rec)
    rec_ref[...] = rec


def _route_constants():
    r = jnp.arange(ROUTE_TM)
    tri = (r[:, None] < r[None, :]).astype(BF16)
    e = jnp.arange(N_EXPERTS)
    lower = (e[:, None] > e[None, :]).astype(BF16)
    return tri, lower


P_START, P_ROWS, P_OFF = 0, 1, 2


def _plan_kernel(carry_ref, rows_ref, off_ref, upper_ref, plan_ref, seg_ref):
    n_tiles = carry_ref.shape[0]
    lane = lax.broadcasted_iota(I32, (1, LANES), 1)
    last = pl.ds(n_tiles - 1, 1)
    totals = jnp.where(lane < N_EXPERTS, carry_ref[last, :] + rows_ref[last, :], 0.0)
    blocks = jnp.floor((totals + (FFN_BLOCK - 1)) * (1.0 / FFN_BLOCK))
    starts = FFN_BLOCK * jnp.dot(jnp.broadcast_to(blocks, (8, LANES)).astype(BF16), upper_ref[...],
                                 preferred_element_type=F32)[0:1]
    seg_ref[...] = jnp.broadcast_to(starts + blocks * FFN_BLOCK, seg_ref.shape)
    plan_ref[P_START] = (carry_ref[...] + starts).astype(I32)
    plan_ref[P_ROWS] = rows_ref[...].astype(I32)
    plan_ref[P_OFF] = off_ref[...].astype(I32)


def run_plan(tab):
    n_tiles = tab.shape[0]
    ln = jnp.arange(LANES)
    upper = (ln[:, None] < ln[None, :]).astype(BF16)
    return pl.pallas_call(
        _plan_kernel,
        out_shape=[jax.ShapeDtypeStruct((3, n_tiles, LANES), I32),
                   jax.ShapeDtypeStruct((8, LANES), F32)],
        compiler_params=pltpu.CompilerParams(vmem_limit_bytes=VMEM_LIMIT),
        name="run_plan",
    )(tab[:, T_CARRY], tab[:, T_ROWS], tab[:, T_OFF], upper)


RUN_PIECE = 64
SMALL_PIECES = (32, 16, 8)
TOTAL_PIECES = (1024, 512, 256, 128, 64, 32, 16, 8)
SORT_CHUNK = 256


def _for_each_run_piece(plan_ref, tile, n_tiles, fn):
    plane = n_tiles * LANES

    def per_expert(e, c):
        idx = tile * LANES + e
        start = plan_ref[P_START * plane + idx]
        rows = plan_ref[P_ROWS * plane + idx]
        off = plan_ref[P_OFF * plane + idx]

        def big(j, c2):
            fn(pl.multiple_of(off + j * RUN_PIECE, RUN_ALIGN), pl.multiple_of(start + j * RUN_PIECE, RUN_ALIGN),
               RUN_PIECE)
            return c2
        lax.fori_loop(0, lax.shift_right_logical(rows, 6), big, 0)
        for size in SMALL_PIECES:
            done = jnp.bitwise_and(rows, -2 * size)

            @pl.when(jnp.bitwise_and(rows, size) != 0)
            def _():
                fn(pl.multiple_of(off + done, RUN_ALIGN), pl.multiple_of(start + done, RUN_ALIGN), size)
        return c
    lax.fori_loop(0, N_EXPERTS, per_expert, 0, unroll=4)


def _for_each_total_piece(plan_ref, tile, n_tiles, fn):
    total = plan_ref[P_OFF * n_tiles * LANES + tile * LANES + TOTAL_LANE]
    for size in TOTAL_PIECES:
        @pl.when(jnp.bitwise_and(total, size) != 0)
        def _():
            fn(size)


def _dispatch_kernel(plan_ref, ends_ref, nv_ref, h2_ref, rec_ref, xs_ref, zero_buf, stage, sem, zsem):
    i = pl.program_id(0)
    n_steps = pl.num_programs(0)
    n_blocks = xs_ref.shape[0] // FFN_BLOCK

    @pl.when(i == 0)
    def _():
        zero_buf[...] = jnp.zeros_like(zero_buf)

        def zero_copy(row0):
            return pltpu.make_async_copy(zero_buf, xs_ref.at[pl.ds(row0, FFN_BLOCK)], zsem)

        def seg_end(e):
            return ends_ref[e], ends_ref[e] > jnp.where(e > 0, ends_ref[jnp.maximum(e - 1, 0)], 0)

        def start_e(e, c):
            end, nonempty = seg_end(e)

            @pl.when(nonempty)
            def _():
                zero_copy(pl.multiple_of(end - FFN_BLOCK, FFN_BLOCK)).start()
            return c
        lax.fori_loop(0, N_EXPERTS, start_e, 0)

        def start_b(blk, c):
            zero_copy(pl.multiple_of(blk * FFN_BLOCK, FFN_BLOCK)).start()
            return c
        lax.fori_loop(nv_ref[0], n_blocks, start_b, 0)

        def wait_e(e, c):
            _, nonempty = seg_end(e)

            @pl.when(nonempty)
            def _():
                zero_copy(0).wait()
            return c
        lax.fori_loop(0, N_EXPERTS, wait_e, 0)

        def wait_b(blk, c):
            zero_copy(0).wait()
            return c
        lax.fori_loop(nv_ref[0], n_blocks, wait_b, 0)

    slot = i % 2
    s1 = rec_ref[R_SLOT1:R_SLOT1 + 1, :]
    s2 = rec_ref[R_SLOT2:R_SLOT2 + 1, :]
    half = PACKED_W
    for c in range(LOCAL_ROWS // SORT_CHUNK):
        rowf = (lax.broadcasted_iota(I32, (SORT_CHUNK, ROUTE_TM), 0) + c * SORT_CHUNK).astype(F32)
        sel = ((rowf == s1) | (rowf == s2)).astype(BF16)
        pick = lambda cols: jnp.dot(sel, h2_ref[:, cols], preferred_element_type=F32)
        stage[slot, c * SORT_CHUNK:(c + 1) * SORT_CHUNK, :] = _pack_pair(pick(slice(0, half)),
                                                                         pick(slice(half, 2 * half)))

    def start_piece(lrow, grow, size):
        pltpu.make_async_copy(stage.at[slot, pl.ds(lrow, size)], xs_ref.at[pl.ds(grow, size)],
                              sem.at[slot]).start()
    _for_each_run_piece(plan_ref, i, n_steps, start_piece)

    def wait_pieces(tile, s):
        def wait_piece(size):
            pltpu.make_async_copy(stage.at[s, pl.ds(0, size)], xs_ref.at[pl.ds(0, size)], sem.at[s]).wait()
        _for_each_total_piece(plan_ref, tile, n_steps, wait_piece)

    @pl.when(i > 0)
    def _():
        wait_pieces(i - 1, 1 - slot)

    @pl.when(i + 1 == n_steps)
    def _():
        wait_pieces(i, slot)


def dispatch(h2, rec, plan, seg_ends, n_valid, n_rows):
    n = h2.shape[0]
    return pl.pallas_call(
        _dispatch_kernel,
        grid_spec=pltpu.PrefetchScalarGridSpec(
            num_scalar_prefetch=3,
            grid=(n // ROUTE_TM,),
            in_specs=[pl.BlockSpec((ROUTE_TM, D_MODEL), lambda i, *_: (i, 0)),
                      pl.BlockSpec((8, ROUTE_TM), lambda i, *_: (0, i))],
            out_specs=pl.BlockSpec(memory_space=pl.ANY),
            scratch_shapes=[pltpu.VMEM((FFN_BLOCK, PACKED_W), U32),
                            pltpu.VMEM((2, LOCAL_ROWS, PACKED_W), U32),
                            pltpu.SemaphoreType.DMA((2,)),
                            pltpu.SemaphoreType.DMA(())],
        ),
        out_shape=jax.ShapeDtypeStruct((n_rows, PACKED_W), U32),
        compiler_params=_cparams("arbitrary"),
        name="dispatch",
    )(plan, seg_ends, n_valid, h2, rec)


def _ffn_kernel(be_ref, nv_ref, xs_ref, wg_ref, wu_ref, wd_ref, ys_ref, wg_sc, wu_sc, wd_sc):
    i = pl.program_id(0)
    prev = be_ref[jnp.maximum(i - 1, 0)]
    fresh = (i == 0) | (be_ref[i] != prev)

    @pl.when(fresh)
    def _():
        wg_sc[...] = wg_ref[0].astype(BF16)
        wu_sc[...] = wu_ref[0].astype(BF16)
        wd_sc[...] = wd_ref[0].astype(BF16)

    @pl.when(i < nv_ref[0])
    def _():
        xa, xb = (t.astype(BF16) for t in _unpack_halves(xs_ref[...]))
        half = PACKED_W
        gate = (jnp.dot(xa, wg_sc[:half, :], preferred_element_type=F32)
                + jnp.dot(xb, wg_sc[half:, :], preferred_element_type=F32))
        up = (jnp.dot(xa, wu_sc[:half, :], preferred_element_type=F32)
              + jnp.dot(xb, wu_sc[half:, :], preferred_element_type=F32))
        hid = (gate * (1.0 / (1.0 + jnp.exp(-gate))) * up).astype(BF16)
        ys_ref[...] = _pack_halves(jnp.dot(hid, wd_sc[...], preferred_element_type=F32))

    @pl.when(i >= nv_ref[0])
    def _():
        ys_ref[...] = jnp.zeros_like(ys_ref)


def expert_ffn(xs, block_expert, n_valid, w_gate, w_up, w_down):
    n_rows = xs.shape[0]
    n_blocks = n_rows // FFN_BLOCK
    wmap = lambda i, be, nv: (be[i], 0, 0)
    return pl.pallas_call(
        _ffn_kernel,
        grid_spec=pltpu.PrefetchScalarGridSpec(
            num_scalar_prefetch=2,
            grid=(n_blocks,),
            in_specs=[pl.BlockSpec((FFN_BLOCK, PACKED_W), lambda i, be, nv: (jnp.minimum(i, nv[0] - 1), 0)),
                      pl.BlockSpec((1, D_MODEL, D_EXPERT), wmap),
                      pl.BlockSpec((1, D_MODEL, D_EXPERT), wmap),
                      pl.BlockSpec((1, D_EXPERT, D_MODEL), wmap)],
            out_specs=pl.BlockSpec((FFN_BLOCK, PACKED_W), lambda i, be, nv: (i, 0)),
            scratch_shapes=[pltpu.VMEM((D_MODEL, D_EXPERT), BF16),
                            pltpu.VMEM((D_MODEL, D_EXPERT), BF16),
                            pltpu.VMEM((D_EXPERT, D_MODEL), BF16)],
        ),
        out_shape=jax.ShapeDtypeStruct((n_rows, PACKED_W), U32),
        compiler_params=_cparams("arbitrary"),
        name="expert_ffn",
    )(block_expert, n_valid, xs, w_gate, w_up, w_down)


def _combine_kernel(plan_ref, x1_ref, rec_ref, gf_ref, ys_ref, out_ref, buf, sem):
    i = pl.program_id(0)
    n_steps = pl.num_programs(0)
    slot = i % 2

    def fetch_tile(tile, s):
        def start_piece(lrow, grow, size):
            pltpu.make_async_copy(ys_ref.at[pl.ds(grow, size)], buf.at[s, pl.ds(lrow, size)], sem.at[s]).start()
        _for_each_run_piece(plan_ref, tile, n_steps, start_piece)

    @pl.when(i == 0)
    def _():
        buf[...] = jnp.zeros_like(buf)
        fetch_tile(0, 0)

    @pl.when(i + 1 < n_steps)
    def _():
        fetch_tile(i + 1, 1 - slot)

    def wait_piece(size):
        pltpu.make_async_copy(ys_ref.at[pl.ds(0, size)], buf.at[slot, pl.ds(0, size)], sem.at[slot]).wait()
    _for_each_total_piece(plan_ref, i, n_steps, wait_piece)

    rowf = lax.broadcasted_iota(I32, (LOCAL_ROWS, ROUTE_TM), 0).astype(F32)
    sel = (jnp.where(rowf == rec_ref[R_SLOT1:R_SLOT1 + 1, :], rec_ref[R_G1:R_G1 + 1, :], 0.0)
           + jnp.where(rowf == rec_ref[R_SLOT2:R_SLOT2 + 1, :], rec_ref[R_G2:R_G2 + 1, :], 0.0)).astype(BF16)
    ya, yb = (t.astype(BF16) for t in _unpack_halves(buf[slot]))
    half = PACKED_W
    tn = (((0,), (0,)), ((), ()))
    xa = x1_ref[:, :half] + lax.dot_general(sel, ya, tn, preferred_element_type=F32)
    xb = x1_ref[:, half:] + lax.dot_general(sel, yb, tn, preferred_element_type=F32)
    ms = (jnp.sum(xa * xa, axis=-1, keepdims=True) + jnp.sum(xb * xb, axis=-1, keepdims=True)) * (1.0 / D_MODEL)
    scale = lax.rsqrt(ms + EPS)
    out_ref[:, :half] = xa * scale * gf_ref[:, :half]
    out_ref[:, half:] = xb * scale * gf_ref[:, half:]


def combine(x1, rec, norm_f_g, ys, plan):
    n = x1.shape[0]
    tm = ROUTE_TM
    return pl.pallas_call(
        _combine_kernel,
        grid_spec=pltpu.PrefetchScalarGridSpec(
            num_scalar_prefetch=1,
            grid=(n // tm,),
            in_specs=[pl.BlockSpec((tm, D_MODEL), lambda i, p: (i, 0)),
                      pl.BlockSpec((8, tm), lambda i, p: (0, i)),
                      pl.BlockSpec((1, D_MODEL), lambda i, p: (0, 0)),
                      pl.BlockSpec(memory_space=pl.ANY)],
            out_specs=pl.BlockSpec((tm, D_MODEL), lambda i, p: (i, 0)),
            scratch_shapes=[pltpu.VMEM((2, LOCAL_ROWS, PACKED_W), U32),
                            pltpu.SemaphoreType.DMA((2,))],
        ),
        out_shape=jax.ShapeDtypeStruct((n, D_MODEL), F32),
        compiler_params=_cparams("arbitrary"),
        name="combine",
    )(plan, x1, rec, norm_f_g.reshape(1, D_MODEL), ys)


def _router_weights(w_group, b_group, w_router, b_router):
    w = jnp.zeros((LANES, D_MODEL), F32)
    w = w.at[:N_GROUPS].set(w_group.T).at[EXPERT_ROW0:EXPERT_ROW0 + N_EXPERTS].set(w_router.T)
    b = jnp.zeros((LANES, 1), F32)
    b = b.at[:N_GROUPS, 0].set(b_group).at[EXPERT_ROW0:EXPERT_ROW0 + N_EXPERTS, 0].set(b_router)
    hi = w.astype(BF16)
    lo = (w - hi.astype(F32)).astype(BF16)
    return hi, lo, b


def _sorted_rows_bound(n_tokens):
    worst = (2 * n_tokens + (n_tokens // ROUTE_TM) * N_EXPERTS * (RUN_ALIGN - 1)
             + N_EXPERTS * (FFN_BLOCK - 1))
    return -(-worst // FFN_BLOCK) * FFN_BLOCK


def _block_plan(seg, n_rows):
    seg_ends = seg[SEG_END, :N_EXPERTS].astype(I32)
    n_blocks = n_rows // FFN_BLOCK
    blk_start = jnp.arange(n_blocks, dtype=I32) * FFN_BLOCK
    block_expert = jnp.minimum(jnp.sum((seg_ends[None, :] <= blk_start[:, None]).astype(I32), axis=1),
                               N_EXPERTS - 1)
    n_valid = seg_ends[-1:] // FFN_BLOCK
    return seg_ends, block_expert, n_valid


def kernel(x, norm1_g, w_in, attn_norm_g, hgrn_gamma, hgrn_norm_g, w_out, norm2_g, w_group, b_group,
           w_router, b_router, w_gate, w_up, w_down, norm_f_g):
    batch, seq, d = x.shape
    assert seq == SEQ and d == D_MODEL and norm1_g.shape[0] == 1
    n = batch * seq
    x2 = x.reshape(n, d)
    qkv, hg = in_proj(x2, norm1_g[0], w_in[0].astype(BF16), batch)
    oa = attention(qkv)
    yh = hgrn(hg.reshape(batch, seq, HG_W), hgrn_gamma, hgrn_norm_g[0]).reshape(n, WIDTH)
    wr_hi, wr_lo, br = _router_weights(w_group[0], b_group[0], w_router[0], b_router[0])
    x1, h2, rec, tab = post_mix(oa, yh, x2, attn_norm_g[0], w_out[0].astype(BF16), norm2_g[0],
                                wr_hi, wr_lo, br)
    plan, seg = run_plan(tab)
    plan = plan.reshape(-1)
    n_rows = _sorted_rows_bound(n)
    seg_ends, block_expert, n_valid = _block_plan(seg, n_rows)
    xs = dispatch(h2, rec, plan, seg_ends, n_valid, n_rows)
    ys = expert_ffn(xs, block_expert, n_valid, w_gate[0], w_up[0], w_down[0])
    return combine(x1, rec, norm_f_g, ys, plan).reshape(batch, seq, d)
```

```python
import jax
import jax.numpy as jnp
from jax import lax
from jax.experimental import pallas as pl
from jax.experimental.pallas import tpu as pltpu

F32 = jnp.float32
BF16 = jnp.bfloat16
I32 = jnp.int32
U32 = jnp.uint32

D_MODEL = 1024
HEAD_DIM = 64
N_HEADS = 8
WIDTH = N_HEADS * HEAD_DIM
QKV_W = 3 * WIDTH
HG_W = 4 * WIDTH
SEQ = 2048
ATTN_BLOCK = 128
DILATIONS = (1, 4, 16)
N_RES = 16
RES_LEN = SEQ // N_RES
TILE_TOKENS = 512
U_PER_TILE = TILE_TOKENS // N_RES
HGRN_CHUNK = 32
HGRN_SUPER = 256
HGRN_GROUP = 8
N_GROUPS = 4
EXPERTS_PER_GROUP = 8
N_EXPERTS = 32
D_EXPERT = 512
FFN_BLOCK = 512
PACKED_W = D_MODEL // 2
EPS = 1e-6
NEG = -1e30
LOG2E = 1.4426950408889634
LANES = 128
VMEM_LIMIT = 56 * 1024 * 1024


def _cparams(*sem):
    return pltpu.CompilerParams(dimension_semantics=sem, vmem_limit_bytes=VMEM_LIMIT)


def _pack_pair(a, b):
    hi = pltpu.bitcast(a.astype(BF16).astype(F32), U32)
    lo = pltpu.bitcast(b.astype(BF16).astype(F32), U32)
    return hi | (lo >> 16)


def _pack_halves(x):
    w = x.shape[1] // 2
    return _pack_pair(x[:, :w], x[:, w:])


def _unpack_halves(p):
    return pltpu.bitcast(p & jnp.uint32(0xFFFF0000), F32), pltpu.bitcast(p << 16, F32)


N_LANE_TILES = WIDTH // LANES


def _tile_permutation():
    i = jnp.arange(TILE_TOKENS)
    src = N_RES * (i % U_PER_TILE) + i // U_PER_TILE
    return (src[:, None] == jnp.arange(TILE_TOKENS)[None, :]).astype(BF16)


def _in_proj_kernel(x_ref, g_ref, w_ref, x4_ref, qkv_ref, hg_ref, xp, sem):
    i = pl.program_id(0)
    n_steps = pl.num_programs(0)
    tiles_per_b = SEQ // TILE_TOKENS
    slot = i % 2

    def fetch(tile, s):
        b = tile // tiles_per_b
        u0 = (tile % tiles_per_b) * U_PER_TILE
        return [pltpu.make_async_copy(x4_ref.at[b, pl.ds(u0, U_PER_TILE), r, :],
                                      xp.at[s, pl.ds(r * U_PER_TILE, U_PER_TILE), :], sem.at[s])
                for r in range(N_RES)]

    @pl.when(i == 0)
    def _():
        for cp in fetch(0, 0):
            cp.start()

    @pl.when(i + 1 < n_steps)
    def _():
        for cp in fetch(i + 1, 1 - slot):
            cp.start()

    def normed(x):
        ms = jnp.mean(x * x, axis=-1, keepdims=True)
        return (x * lax.rsqrt(ms + EPS) * g_ref[...]).astype(BF16)

    h = normed(x_ref[...])
    for j in range(HG_W // WIDTH):
        wsl = slice(QKV_W + j * WIDTH, QKV_W + (j + 1) * WIDTH)
        hg_ref[:, j * WIDTH:(j + 1) * WIDTH] = jnp.dot(h, w_ref[:, wsl], preferred_element_type=F32).astype(BF16)
    for cp in fetch(i, slot):
        cp.wait()
    hp = normed(xp[slot])
    for c in range(QKV_W // WIDTH):
        res = jnp.dot(hp, w_ref[:, c * WIDTH:(c + 1) * WIDTH], preferred_element_type=F32)
        for r in range(N_RES):
            for g in range(N_LANE_TILES):
                qkv_ref[r, g, c] = res[r * U_PER_TILE:(r + 1) * U_PER_TILE, g * LANES:(g + 1) * LANES]


def in_proj(x2, g, w_bf16, batch):
    n = x2.shape[0]
    tiles_per_b = SEQ // TILE_TOKENS
    x4 = x2.reshape(batch, RES_LEN, N_RES, D_MODEL)
    return pl.pallas_call(
        _in_proj_kernel,
        grid=(n // TILE_TOKENS,),
        in_specs=[
            pl.BlockSpec((TILE_TOKENS, D_MODEL), lambda i: (i, 0)),
            pl.BlockSpec((1, D_MODEL), lambda i: (0, 0)),
            pl.BlockSpec((D_MODEL, QKV_W + HG_W), lambda i: (0, 0)),
            pl.BlockSpec(memory_space=pl.ANY),
        ],
        out_specs=[
            pl.BlockSpec((None, N_RES, N_LANE_TILES, 3, U_PER_TILE, LANES),
                         lambda i: (i // tiles_per_b, 0, 0, 0, i % tiles_per_b, 0)),
            pl.BlockSpec((TILE_TOKENS, HG_W), lambda i: (i, 0)),
        ],
        out_shape=[
            jax.ShapeDtypeStruct((batch, N_RES, N_LANE_TILES, 3, RES_LEN, LANES), F32),
            jax.ShapeDtypeStruct((n, HG_W), BF16),
        ],
        scratch_shapes=[pltpu.VMEM((2, TILE_TOKENS, D_MODEL), F32), pltpu.SemaphoreType.DMA((2,))],
        compiler_params=_cparams("arbitrary"),
        name="in_proj",
    )(x2, g.reshape(1, D_MODEL), w_bf16, x4)


ATTN_GROUP16 = 16
ATTN_GROUP1 = 5
HEADS_PER_TILE = LANES // HEAD_DIM


def _attn_bias_tables():
    slopes = 2.0 ** (-8.0 * jnp.arange(1, N_HEADS + 1, dtype=F32) / N_HEADS)

    def table(qpos, kpos, dil):
        sd = qpos[:, None] - kpos[None, :]
        b = -slopes[:, None, None] * (sd * dil).astype(F32)[None] * LOG2E
        return jnp.where(((sd >= 0) & (sd <= ATTN_BLOCK))[None], b, NEG)

    q = jnp.arange(ATTN_BLOCK)
    k = jnp.arange(2 * ATTN_BLOCK)
    q1 = 16 * (q % 8) + q // 8 + ATTN_BLOCK
    k1 = 16 * (k % 16) + k // 16
    q4 = 4 * (q % 32) + q // 32 + ATTN_BLOCK
    k4 = 4 * (k % 64) + k // 64
    q16 = q + ATTN_BLOCK
    none = jnp.full((N_HEADS, ATTN_BLOCK, 2 * ATTN_BLOCK), NEG, F32)
    tabs = [jnp.concatenate([table(q1, k1, 1), table(q1, q1, 1)], axis=-1),
            jnp.concatenate([table(q4, k4, 4), table(q4, q4, 4)], axis=-1),
            jnp.concatenate([none, table(q16, q16, 16)], axis=-1)]
    return jnp.stack(tabs, axis=0)


def _attn_bias(bias_ref, d_idx, with_prev):
    ksl = slice(0, 2 * ATTN_BLOCK) if with_prev else slice(2 * ATTN_BLOCK, 3 * ATTN_BLOCK)
    return jnp.concatenate([bias_ref[d_idx, h, :, ksl] for h in range(HEADS_PER_TILE)], axis=0)


def _attn_blocks(blocks):
    nb = ATTN_BLOCK
    lane = lax.broadcasted_iota(I32, (nb, LANES), 1)
    head0 = lane < HEAD_DIM
    biases = [blk[3] for blk in blocks]
    blocks = [blk[:3] for blk in blocks]
    qs = [q * (HEAD_DIM ** -0.5 * LOG2E) for q, _, _ in blocks]
    qq = [jnp.concatenate([jnp.where(head0, q, 0.0), jnp.where(head0, 0.0, q)], axis=0).astype(BF16)
          for q in qs]
    ks = [k.astype(BF16) for _, k, _ in blocks]
    vs = [v.astype(BF16) for _, _, v in blocks]
    v_ones = [jnp.concatenate([v, jnp.ones_like(v)], axis=-1) for v in vs]
    s = [lax.dot_general(q2, k, (((1,), (1,)), ((), ())), preferred_element_type=F32) + bias
         for q2, k, bias in zip(qq, ks, biases)]
    m = [jnp.max(x, axis=-1, keepdims=True) for x in s]
    p = [jnp.exp2(x - mx).astype(BF16) for x, mx in zip(s, m)]
    acc = [jnp.dot(px, vo, preferred_element_type=F32) for px, vo in zip(p, v_ones)]
    head0_wide = jnp.concatenate([head0, head0], axis=-1)
    outs = []
    for a, mx in zip(acc, m):
        ol = jnp.where(head0_wide, a[:nb], a[nb:])
        l = ol[:, LANES:]
        outs.append((ol[:, :LANES] * (1.0 / l), jnp.where(head0, mx[:nb], mx[nb:]) + jnp.log2(l)))
    return outs


def _attn_kernel(qkv_ref, bias_ref, o_ref, o4_ref, l4_ref, o16_ref, l16_ref, fin_ref):
    grp = ATTN_GROUP16

    def body16(j, c):
        rs = [j * grp + a for a in range(grp)]
        bias = _attn_bias(bias_ref, 2, False)
        res = _attn_blocks([(qkv_ref[r, 0], qkv_ref[r, 1], qkv_ref[r, 2], bias) for r in rs])
        for r, (o, l) in zip(rs, res):
            o16_ref[r] = o
            l16_ref[r] = l
        return c
    for j in range(N_RES // grp):
        body16(j, 0)

    def gather4(c, rho, u0, nu):
        return jnp.concatenate([qkv_ref[rho + 4 * a, c, pl.ds(u0, nu), :] for a in range(4)], axis=0)

    def store4(rho, u0, o, l):
        for a in range(4):
            o4_ref[rho + 4 * a, pl.ds(u0, 32), :] = o[a * 32:(a + 1) * 32]
            l4_ref[rho + 4 * a, pl.ds(u0, 32), :] = l[a * 32:(a + 1) * 32]

    def body4(j, c):
        bias = _attn_bias(bias_ref, 1, True)
        todo = [(2 * j + a, 32 * n) for a in range(2) for n in range(1, 4)]
        res = _attn_blocks([(gather4(0, rho, u0, 32), gather4(1, rho, u0 - 32, 64),
                             gather4(2, rho, u0 - 32, 64), bias) for rho, u0 in todo])
        for (rho, u0), (o, l) in zip(todo, res):
            store4(rho, u0, o, l)
        return c
    for j in range(2):
        body4(j, 0)

    def gather1(ref, u0, nu, *lead):
        return jnp.concatenate([ref[(r,) + lead + (pl.ds(u0, nu), slice(None))] for r in range(N_RES)],
                               axis=0)

    def merge_and_store(u0, o1, l1):
        l4 = gather1(l4_ref, u0, 8)
        l16 = gather1(l16_ref, u0, 8)
        mx = jnp.maximum(jnp.maximum(l1, l4), l16)
        e1 = jnp.exp2(l1 - mx)
        e4 = jnp.exp2(l4 - mx)
        e16 = jnp.exp2(l16 - mx)
        inv = 1.0 / (e1 + e4 + e16)
        o = (e1 * inv) * o1 + (e4 * inv) * gather1(o4_ref, u0, 8) + (e16 * inv) * gather1(o16_ref, u0, 8)
        for r in range(N_RES):
            fin_ref[r, pl.ds(u0, 8), :] = o[r * 8:(r + 1) * 8]

    bias4 = _attn_bias(bias_ref, 1, False)
    first = [tuple(gather4(c, rho, 0, 32) for c in range(3)) + (bias4,) for rho in range(4)]
    first.append(tuple(gather1(qkv_ref, 0, 8, c) for c in range(3)) + (_attn_bias(bias_ref, 0, False),))
    res = _attn_blocks(first)
    for rho in range(4):
        store4(rho, 0, *res[rho])
    merge_and_store(0, *res[4])

    n_blocks1 = RES_LEN // 8 - 1

    def body1(j, c):
        u0s = [(1 + j * ATTN_GROUP1 + a) * 8 for a in range(ATTN_GROUP1)]
        bias = _attn_bias(bias_ref, 0, True)
        res = _attn_blocks([(gather1(qkv_ref, u0, 8, 0), gather1(qkv_ref, u0 - 8, 16, 1),
                             gather1(qkv_ref, u0 - 8, 16, 2), bias) for u0 in u0s])
        for u0, (o, l) in zip(u0s, res):
            merge_and_store(u0, o, l)
        return c
    for j in range(n_blocks1 // ATTN_GROUP1):
        body1(j, 0)

    def emit(r, c):
        o_ref[r] = fin_ref[r].astype(o_ref.dtype)
        return c
    lax.fori_loop(0, N_RES, emit, 0)


def attention(qkv):
    batch = qkv.shape[0]
    scratch = pltpu.VMEM((N_RES, RES_LEN, LANES), F32)
    return pl.pallas_call(
        _attn_kernel,
        grid=(batch, N_LANE_TILES),
        in_specs=[
            pl.BlockSpec((None, N_RES, None, 3, RES_LEN, LANES), lambda b, g: (b, 0, g, 0, 0, 0)),
            pl.BlockSpec((len(DILATIONS), HEADS_PER_TILE, ATTN_BLOCK, 3 * ATTN_BLOCK),
                         lambda b, g: (0, g, 0, 0)),
        ],
        out_specs=pl.BlockSpec((None, N_RES, None, RES_LEN, LANES), lambda b, g: (b, 0, g, 0, 0)),
        out_shape=jax.ShapeDtypeStruct((batch, N_RES, N_LANE_TILES, RES_LEN, LANES), BF16),
        scratch_shapes=[scratch] * 5,
        compiler_params=_cparams("arbitrary", "arbitrary"),
        name="dilated_attention",
    )(qkv, _attn_bias_tables())


def _split2(x):
    a = x.astype(BF16)
    b = (x - a.astype(F32)).astype(BF16)
    return a, b


def _hgrn_kernel(q_ref, f_ref, i_ref, g_ref, gamma_ref, ng_ref, tri_ref, y_ref):
    sup, ch = HGRN_SUPER, HGRN_CHUNK
    n_ch = sup // ch
    gam = gamma_ref[...]
    gmx = jnp.max(gam, axis=0, keepdims=True)
    ge = jnp.exp(gam - gmx)
    lb = ge[0:1] / jnp.sum(ge, axis=0, keepdims=True)
    ng = ng_ref[...]
    tri = tri_ref[...]
    ri = lax.broadcasted_iota(I32, (2 * sup, sup), 0) % sup
    ci = lax.broadcasted_iota(I32, (2 * sup, sup), 1)
    causal_bd2 = (ri // ch == ci // ch) & (ri >= ci)
    head0 = lax.broadcasted_iota(I32, (sup, LANES), 1) < HEAD_DIM
    same_head = (lax.broadcasted_iota(I32, (LANES, LANES), 0) // HEAD_DIM
                 == lax.broadcasted_iota(I32, (LANES, LANES), 1) // HEAD_DIM)

    def sigmoid(x):
        return 1.0 / (1.0 + jnp.exp(-x))

    def dot_nt(a, b):
        return lax.dot_general(a, b, (((1,), (1,)), ((), ())), preferred_element_type=F32)

    def dot_tn(a, b):
        return lax.dot_general(a, b, (((0,), (0,)), ((), ())), preferred_element_type=F32)

    def body(j, state):
        grp = range(HGRN_GROUP)
        rows = [pl.ds((j * HGRN_GROUP + a) * sup, sup) for a in grp]
        fr = [f_ref[r, :].astype(F32) for r in rows]
        t = [jnp.exp(-jnp.abs(x)) for x in fr]
        rcp = [1.0 / (1.0 + x) for x in t]
        sig = [jnp.where(f >= 0, r, x * r) for f, r, x in zip(fr, rcp, t)]
        nsig = [jnp.where(f >= 0, x * r, r) for f, r, x in zip(fr, rcp, t)]
        logf = [jnp.log(lb + (1.0 - lb) * x) for x in sig]
        key = [(1.0 - lb) * x for x in nsig]
        parts = [_split2(x) for x in logf]
        b = [sum(jnp.dot(tri, p, preferred_element_type=F32) for p in ps) for ps in parts]
        totals = [[x[(c + 1) * ch - 1:(c + 1) * ch, :] for c in range(n_ch)] for x in b]
        b_last = [jnp.concatenate([jnp.broadcast_to(t_, (ch, LANES)) for t_ in ts], axis=0) for ts in totals]
        pad = jnp.zeros((LANES - n_ch, LANES), F32)
        decay_t = [jnp.exp(jnp.transpose(jnp.concatenate(ts + [pad], axis=0))) for ts in totals]
        qv = [q_ref[r, :].astype(F32) for r in rows]
        q_in = [(x * sigmoid(x) * jnp.exp(bb)).astype(BF16) for x, bb in zip(qv, b)]
        k_in = [(k * jnp.exp(-bb)).astype(BF16) for k, bb in zip(key, b)]
        k_end = [(k * jnp.exp(bl - bb)).astype(BF16) for k, bl, bb in zip(key, b_last, b)]
        vv = [i_ref[r, :] for r in rows]
        q2 = [jnp.concatenate([jnp.where(head0, x, jnp.zeros_like(x)), jnp.where(head0, jnp.zeros_like(x), x)],
                              axis=0) for x in q_in]
        att = [jnp.where(causal_bd2, dot_nt(x, k), 0.0).astype(BF16) for x, k in zip(q2, k_in)]
        intra = [jnp.dot(a_, v, preferred_element_type=F32) for a_, v in zip(att, vv)]
        o_intra = [jnp.where(head0, x[:sup], x[sup:]) for x in intra]
        ds_all = [jnp.concatenate([dot_tn(k[c * ch:(c + 1) * ch], v[c * ch:(c + 1) * ch])
                                   for c in range(n_ch)], axis=1) for k, v in zip(k_end, vv)]
        s_prev = []
        for a in grp:
            for c in range(n_ch):
                s_prev.append(state.astype(BF16))
                ds = jnp.where(same_head, ds_all[a][:, c * LANES:(c + 1) * LANES], 0.0)
                state = decay_t[a][:, c:c + 1] * state + ds
        o_inter = [jnp.concatenate(
            [jnp.dot(q_in[a][c * ch:(c + 1) * ch], s_prev[a * n_ch + c], preferred_element_type=F32)
             for c in range(n_ch)], axis=0) for a in grp]
        o = [x + y for x, y in zip(o_intra, o_inter)]
        sq = [x * x for x in o]
        ss0 = [jnp.sum(jnp.where(head0, x, 0.0), axis=-1, keepdims=True) for x in sq]
        ss1 = [jnp.sum(jnp.where(head0, 0.0, x), axis=-1, keepdims=True) for x in sq]
        ms = [jnp.where(head0, x, y) * (1.0 / HEAD_DIM) for x, y in zip(ss0, ss1)]
        gv = [g_ref[r, :].astype(F32) for r in rows]
        for r, x, m_, g_ in zip(rows, o, ms, gv):
            y_ref[r, :] = (x * lax.rsqrt(m_ + EPS) * ng * (g_ * sigmoid(g_))).astype(y_ref.dtype)
        return state

    state = jnp.zeros((LANES, LANES), F32)
    for j in range(SEQ // (sup * HGRN_GROUP)):
        state = body(j, state)


def hgrn(hg3, gamma, norm_g):
    batch = hg3.shape[0]
    r = jnp.arange(HGRN_SUPER)
    same = (r[:, None] // HGRN_CHUNK) == (r[None, :] // HGRN_CHUNK)
    tri = (same & (r[:, None] >= r[None, :])).astype(BF16)
    col = lambda which: (lambda b, g: (b, 0, which * N_LANE_TILES + g))
    const2 = lambda b, g: (0, 0)
    return pl.pallas_call(
        _hgrn_kernel,
        grid=(batch, N_LANE_TILES),
        in_specs=[
            pl.BlockSpec((None, SEQ, LANES), col(0)),
            pl.BlockSpec((None, SEQ, LANES), col(1)),
            pl.BlockSpec((None, SEQ, LANES), col(2)),
            pl.BlockSpec((None, SEQ, LANES), col(3)),
            pl.BlockSpec((2, LANES), lambda b, g: (0, g)),
            pl.BlockSpec((1, LANES), lambda b, g: (0, g)),
            pl.BlockSpec((HGRN_SUPER, HGRN_SUPER), const2),
        ],
        out_specs=pl.BlockSpec((None, SEQ, LANES), lambda b, g: (b, 0, g)),
        out_shape=jax.ShapeDtypeStruct((batch, SEQ, WIDTH), BF16),
        compiler_params=_cparams("arbitrary", "arbitrary"),
        name="hgrn2",
    )(hg3, hg3, hg3, hg3, gamma.astype(F32), norm_g.reshape(1, WIDTH).astype(F32), tri)


POST_MIX_PARTS = 8


def _post_mix_kernel(oa_ref, yh_ref, x_ref, ag_ref, wo_ref, g2_ref, wrh_ref, wrl_ref, br_ref, perm_t_ref,
                     tri_ref, lower_ref,
                     x1_ref, h2_ref, rec_ref, tab_ref, carry_ref, lg_sc):
    i = pl.program_id(0)

    @pl.when(i == 0)
    def _():
        lg_sc[...] = jnp.zeros_like(lg_sc)
    _route_tile(lg_sc[...], (i > 0).astype(F32), tri_ref, lower_ref, rec_ref, tab_ref, carry_ref)

    oa = jnp.concatenate(
        [jnp.concatenate([oa_ref[r, g] for g in range(N_LANE_TILES)], axis=1)
         for r in range(N_RES)], axis=0).astype(F32)
    ms = jnp.mean(oa * oa, axis=-1, keepdims=True)
    ya = (oa * lax.rsqrt(ms + EPS) * ag_ref[...]).astype(BF16)
    ya = jnp.dot(perm_t_ref[...], ya, preferred_element_type=F32).astype(BF16)
    parts = POST_MIX_PARTS
    pr = TILE_TOKENS // parts
    rows = [slice(p * pr, (p + 1) * pr) for p in range(parts)]
    mix = [jnp.dot(ya[r], wo_ref[:WIDTH, :], preferred_element_type=F32)
           + jnp.dot(yh_ref[r, :], wo_ref[WIDTH:, :], preferred_element_type=F32) for r in rows]
    x1 = [x_ref[r, :] + m for r, m in zip(rows, mix)]
    for r, v in zip(rows, x1):
        x1_ref[r, :] = v
    ms2 = [jnp.mean(v * v, axis=-1, keepdims=True) for v in x1]
    h2 = [v * lax.rsqrt(m + EPS) * g2_ref[...] for v, m in zip(x1, ms2)]
    hi = [v.astype(BF16) for v in h2]
    for r, v in zip(rows, hi):
        h2_ref[r, :] = v
    lo = [(v - h.astype(F32)).astype(BF16) for v, h in zip(h2, hi)]
    nt = (((1,), (1,)), ((), ()))
    wrh = wrh_ref[...]
    for r, h, l in zip(rows, hi, lo):
        lg_sc[:, r] = (lax.dot_general(wrh, h, nt, preferred_element_type=F32)
                       + lax.dot_general(wrh, l, nt, preferred_element_type=F32)
                       + lax.dot_general(wrl_ref[...], h, nt, preferred_element_type=F32)
                       + br_ref[...])


def post_mix(oa, yh, x2, attn_g, w_out_bf16, g2, wr_hi, wr_lo, br):
    n = x2.shape[0]
    tiles_per_b = SEQ // TILE_TOKENS
    assert ROUTE_TM == TILE_TOKENS
    n_tiles = n // TILE_TOKENS
    cur = lambda i: jnp.minimum(i, n_tiles - 1)
    prev = lambda i: jnp.maximum(i - 1, 0)
    row = lambda w: pl.BlockSpec((TILE_TOKENS, w), lambda i: (cur(i), 0))
    const = lambda r, c: pl.BlockSpec((r, c), lambda i: (0, 0))
    return pl.pallas_call(
        _post_mix_kernel,
        grid=(n_tiles + 1,),
        in_specs=[pl.BlockSpec((None, N_RES, N_LANE_TILES, U_PER_TILE, LANES),
                               lambda i: (cur(i) // tiles_per_b, 0, 0, cur(i) % tiles_per_b, 0)),
                  row(WIDTH), row(D_MODEL),
                  const(1, WIDTH), const(2 * WIDTH, D_MODEL), const(1, D_MODEL),
                  const(LANES, D_MODEL), const(LANES, D_MODEL), const(LANES, 1),
                  const(TILE_TOKENS, TILE_TOKENS), const(ROUTE_TM, ROUTE_TM), const(N_EXPERTS, N_EXPERTS)],
        out_specs=[row(D_MODEL), row(D_MODEL),
                   pl.BlockSpec((8, ROUTE_TM), lambda i: (0, prev(i))),
                   pl.BlockSpec((1, 8, LANES), lambda i: (prev(i), 0, 0))],
        out_shape=[jax.ShapeDtypeStruct((n, D_MODEL), F32),
                   jax.ShapeDtypeStruct((n, D_MODEL), BF16),
                   jax.ShapeDtypeStruct((8, n), F32),
                   jax.ShapeDtypeStruct((n_tiles, 8, LANES), F32)],
        scratch_shapes=[pltpu.VMEM((N_EXPERTS, LANES), F32), pltpu.VMEM((LANES, TILE_TOKENS), F32)],
        compiler_params=_cparams("arbitrary"),
        name="post_mix",
    )(oa, yh, x2, attn_g.reshape(1, WIDTH), w_out_bf16, g2.reshape(1, D_MODEL), wr_hi, wr_lo, br,
      _tile_permutation().T, *_route_constants())


ROUTE_TM = 512
EXPERT_ROW0 = 32
R_E1, R_E2, R_SLOT1, R_SLOT2, R_G1, R_G2 = 0, 1, 2, 3, 4, 5
T_CARRY, T_ROWS, T_OFF = 0, 1, 2
SEG_END = 0
TOTAL_LANE = N_EXPERTS
RUN_ALIGN = 8
LOCAL_ROWS = 2 * ROUTE_TM + N_EXPERTS * RUN_ALIGN


def _route_tile(lg, live, tri_ref, lower_ref, rec_ref, tab_ref, carry_ref):
    i = pl.program_id(0)

    @pl.when(i == 0)
    def _():
        carry_ref[...] = jnp.zeros_like(carry_ref)

    tm = lg.shape[1]
    sub8 = lax.broadcasted_iota(I32, (8, tm), 0).astype(F32)
    big = 8.0
    gmask = sub8 < N_GROUPS
    gl = jnp.where(gmask, lg[0:8], NEG)
    gmax = jnp.max(gl, axis=0, keepdims=True)
    gsel = jnp.min(jnp.where(gmask & (gl == gmax), sub8, big), axis=0, keepdims=True)
    gsum = jnp.sum(jnp.where(gmask, jnp.exp(gl - gmax), 0.0), axis=0, keepdims=True)
    w_g = 1.0 / gsum
    el = jnp.zeros((EXPERTS_PER_GROUP, tm), F32)
    for g in range(N_GROUPS):
        lo = EXPERT_ROW0 + g * EXPERTS_PER_GROUP
        el = jnp.where(gsel == g, lg[lo:lo + EXPERTS_PER_GROUP], el)
    v1 = jnp.max(el, axis=0, keepdims=True)
    i1 = jnp.min(jnp.where(el == v1, sub8, big), axis=0, keepdims=True)
    el2 = jnp.where(sub8 == i1, NEG, el)
    v2 = jnp.max(el2, axis=0, keepdims=True)
    i2 = jnp.min(jnp.where((el2 == v2) & (sub8 != i1), sub8, big), axis=0, keepdims=True)
    ex = jnp.exp(v2 - v1)
    den = 1.0 / (1.0 + ex)
    g1 = w_g * den
    g2 = w_g * ex * den
    e1 = gsel * EXPERTS_PER_GROUP + i1
    e2 = gsel * EXPERTS_PER_GROUP + i2
    sub_e = lax.broadcasted_iota(I32, (N_EXPERTS, tm), 0).astype(F32)
    oh1 = sub_e == e1
    oh2 = sub_e == e2
    onehot = (oh1 | oh2).astype(BF16)
    before = jnp.dot(onehot, tri_ref[...], preferred_element_type=F32)
    count = jnp.sum(onehot.astype(F32), axis=1, keepdims=True)
    units = jnp.floor((count + (RUN_ALIGN - 1)) * (1.0 / RUN_ALIGN)) * live
    units = jnp.broadcast_to(units, (N_EXPERTS, LANES))
    rows = units * RUN_ALIGN
    off = RUN_ALIGN * jnp.dot(lower_ref[...], units.astype(BF16), preferred_element_type=F32)
    place = off[:, 0:1] + before
    slot1 = jnp.sum(jnp.where(oh1, place, 0.0), axis=0, keepdims=True)
    slot2 = jnp.sum(jnp.where(oh2, place, 0.0), axis=0, keepdims=True)
    total = jnp.sum(rows, axis=0, keepdims=True)
    eye = (lax.broadcasted_iota(I32, (N_EXPERTS, LANES), 0) == lax.broadcasted_iota(I32, (N_EXPERTS, LANES), 1))
    lane1 = lax.broadcasted_iota(I32, (1, LANES), 1)
    as_row = lambda col: jnp.where(lane1 == TOTAL_LANE, total,
                                   jnp.sum(jnp.where(eye, col, 0.0), axis=0, keepdims=True))
    sub = lax.broadcasted_iota(I32, (8, LANES), 0)
    tab_ref[0] = jnp.where(sub == T_CARRY, as_row(carry_ref[...]),
                           jnp.where(sub == T_ROWS, as_row(rows), jnp.where(sub == T_OFF, as_row(off), 0.0)))
    carry_ref[...] += rows
    rec = jnp.zeros((8, tm), F32)
    for slot, val in ((R_E1, e1), (R_E2, e2), (R_SLOT1, slot1),
                      (R_SLOT2, slot2), (R_G1, g1), (R_G2, g2)):
        rec = jnp.where(sub8 == slot, val, rec)
    rec_ref[...] = rec


def _route_constants():
    r = jnp.arange(ROUTE_TM)
    tri = (r[:, None] < r[None, :]).astype(BF16)
    e = jnp.arange(N_EXPERTS)
    lower = (e[:, None] > e[None, :]).astype(BF16)
    return tri, lower


P_START, P_ROWS, P_OFF = 0, 1, 2


def _plan_kernel(carry_ref, rows_ref, off_ref, upper_ref, plan_ref, seg_ref):
    n_tiles = carry_ref.shape[0]
    lane = lax.broadcasted_iota(I32, (1, LANES), 1)
    last = pl.ds(n_tiles - 1, 1)
    totals = jnp.where(lane < N_EXPERTS, carry_ref[last, :] + rows_ref[last, :], 0.0)
    blocks = jnp.floor((totals + (FFN_BLOCK - 1)) * (1.0 / FFN_BLOCK))
    starts = FFN_BLOCK * jnp.dot(jnp.broadcast_to(blocks, (8, LANES)).astype(BF16), upper_ref[...],
                                 preferred_element_type=F32)[0:1]
    seg_ref[...] = jnp.broadcast_to(starts + blocks * FFN_BLOCK, seg_ref.shape)
    plan_ref[P_START] = (carry_ref[...] + starts).astype(I32)
    plan_ref[P_ROWS] = rows_ref[...].astype(I32)
    plan_ref[P_OFF] = off_ref[...].astype(I32)


def run_plan(tab):
    n_tiles = tab.shape[0]
    ln = jnp.arange(LANES)
    upper = (ln[:, None] < ln[None, :]).astype(BF16)
    return pl.pallas_call(
        _plan_kernel,
        out_shape=[jax.ShapeDtypeStruct((3, n_tiles, LANES), I32),
                   jax.ShapeDtypeStruct((8, LANES), F32)],
        compiler_params=pltpu.CompilerParams(vmem_limit_bytes=VMEM_LIMIT),
        name="run_plan",
    )(tab[:, T_CARRY], tab[:, T_ROWS], tab[:, T_OFF], upper)


RUN_PIECE = 64
SMALL_PIECES = (32, 16, 8)
TOTAL_PIECES = (1024, 512, 256, 128, 64, 32, 16, 8)
SORT_CHUNK = 256


def _for_each_run_piece(plan_ref, tile, n_tiles, fn):
    plane = n_tiles * LANES

    def per_expert(e, c):
        idx = tile * LANES + e
        start = plan_ref[P_START * plane + idx]
        rows = plan_ref[P_ROWS * plane + idx]
        off = plan_ref[P_OFF * plane + idx]

        def big(j, c2):
            fn(pl.multiple_of(off + j * RUN_PIECE, RUN_ALIGN), pl.multiple_of(start + j * RUN_PIECE, RUN_ALIGN),
               RUN_PIECE)
            return c2
        lax.fori_loop(0, lax.shift_right_logical(rows, 6), big, 0)
        for size in SMALL_PIECES:
            done = jnp.bitwise_and(rows, -2 * size)

            @pl.when(jnp.bitwise_and(rows, size) != 0)
            def _():
                fn(pl.multiple_of(off + done, RUN_ALIGN), pl.multiple_of(start + done, RUN_ALIGN), size)
        return c
    lax.fori_loop(0, N_EXPERTS, per_expert, 0, unroll=4)


def _for_each_total_piece(plan_ref, tile, n_tiles, fn):
    total = plan_ref[P_OFF * n_tiles * LANES + tile * LANES + TOTAL_LANE]
    for size in TOTAL_PIECES:
        @pl.when(jnp.bitwise_and(total, size) != 0)
        def _():
            fn(size)


def _dispatch_kernel(plan_ref, ends_ref, nv_ref, h2_ref, rec_ref, xs_ref, zero_buf, stage, sem, zsem):
    i = pl.program_id(0)
    n_steps = pl.num_programs(0)
    n_blocks = xs_ref.shape[0] // FFN_BLOCK

    @pl.when(i == 0)
    def _():
        zero_buf[...] = jnp.zeros_like(zero_buf)

        def zero_copy(row0):
            return pltpu.make_async_copy(zero_buf, xs_ref.at[pl.ds(row0, FFN_BLOCK)], zsem)

        def seg_end(e):
            return ends_ref[e], ends_ref[e] > jnp.where(e > 0, ends_ref[jnp.maximum(e - 1, 0)], 0)

        def start_e(e, c):
            end, nonempty = seg_end(e)

            @pl.when(nonempty)
            def _():
                zero_copy(pl.multiple_of(end - FFN_BLOCK, FFN_BLOCK)).start()
            return c
        lax.fori_loop(0, N_EXPERTS, start_e, 0)

        def start_b(blk, c):
            zero_copy(pl.multiple_of(blk * FFN_BLOCK, FFN_BLOCK)).start()
            return c
        lax.fori_loop(nv_ref[0], n_blocks, start_b, 0)

        def wait_e(e, c):
            _, nonempty = seg_end(e)

            @pl.when(nonempty)
            def _():
                zero_copy(0).wait()
            return c
        lax.fori_loop(0, N_EXPERTS, wait_e, 0)

        def wait_b(blk, c):
            zero_copy(0).wait()
            return c
        lax.fori_loop(nv_ref[0], n_blocks, wait_b, 0)

    slot = i % 2
    s1 = rec_ref[R_SLOT1:R_SLOT1 + 1, :]
    s2 = rec_ref[R_SLOT2:R_SLOT2 + 1, :]
    half = PACKED_W
    for c in range(LOCAL_ROWS // SORT_CHUNK):
        rowf = (lax.broadcasted_iota(I32, (SORT_CHUNK, ROUTE_TM), 0) + c * SORT_CHUNK).astype(F32)
        sel = ((rowf == s1) | (rowf == s2)).astype(BF16)
        pick = lambda cols: jnp.dot(sel, h2_ref[:, cols], preferred_element_type=F32)
        stage[slot, c * SORT_CHUNK:(c + 1) * SORT_CHUNK, :] = _pack_pair(pick(slice(0, half)),
                                                                         pick(slice(half, 2 * half)))

    def start_piece(lrow, grow, size):
        pltpu.make_async_copy(stage.at[slot, pl.ds(lrow, size)], xs_ref.at[pl.ds(grow, size)],
                              sem.at[slot]).start()
    _for_each_run_piece(plan_ref, i, n_steps, start_piece)

    def wait_pieces(tile, s):
        def wait_piece(size):
            pltpu.make_async_copy(stage.at[s, pl.ds(0, size)], xs_ref.at[pl.ds(0, size)], sem.at[s]).wait()
        _for_each_total_piece(plan_ref, tile, n_steps, wait_piece)

    @pl.when(i > 0)
    def _():
        wait_pieces(i - 1, 1 - slot)

    @pl.when(i + 1 == n_steps)
    def _():
        wait_pieces(i, slot)


def dispatch(h2, rec, plan, seg_ends, n_valid, n_rows):
    n = h2.shape[0]
    return pl.pallas_call(
        _dispatch_kernel,
        grid_spec=pltpu.PrefetchScalarGridSpec(
            num_scalar_prefetch=3,
            grid=(n // ROUTE_TM,),
            in_specs=[pl.BlockSpec((ROUTE_TM, D_MODEL), lambda i, *_: (i, 0)),
                      pl.BlockSpec((8, ROUTE_TM), lambda i, *_: (0, i))],
            out_specs=pl.BlockSpec(memory_space=pl.ANY),
            scratch_shapes=[pltpu.VMEM((FFN_BLOCK, PACKED_W), U32),
                            pltpu.VMEM((2, LOCAL_ROWS, PACKED_W), U32),
                            pltpu.SemaphoreType.DMA((2,)),
                            pltpu.SemaphoreType.DMA(())],
        ),
        out_shape=jax.ShapeDtypeStruct((n_rows, PACKED_W), U32),
        compiler_params=_cparams("arbitrary"),
        name="dispatch",
    )(plan, seg_ends, n_valid, h2, rec)


def _ffn_kernel(be_ref, nv_ref, xs_ref, wg_ref, wu_ref, wd_ref, ys_ref, wg_sc, wu_sc, wd_sc):
    i = pl.program_id(0)
    prev = be_ref[jnp.maximum(i - 1, 0)]
    fresh = (i == 0) | (be_ref[i] != prev)

    @pl.when(fresh)
    def _():
        wg_sc[...] = wg_ref[0].astype(BF16)
        wu_sc[...] = wu_ref[0].astype(BF16)
        wd_sc[...] = wd_ref[0].astype(BF16)

    @pl.when(i < nv_ref[0])
    def _():
        xa, xb = (t.astype(BF16) for t in _unpack_halves(xs_ref[...]))
        half = PACKED_W
        gate = (jnp.dot(xa, wg_sc[:half, :], preferred_element_type=F32)
                + jnp.dot(xb, wg_sc[half:, :], preferred_element_type=F32))
        up = (jnp.dot(xa, wu_sc[:half, :], preferred_element_type=F32)
              + jnp.dot(xb, wu_sc[half:, :], preferred_element_type=F32))
        hid = (gate * (1.0 / (1.0 + jnp.exp(-gate))) * up).astype(BF16)
        ys_ref[...] = _pack_halves(jnp.dot(hid, wd_sc[...], preferred_element_type=F32))

    @pl.when(i >= nv_ref[0])
    def _():
        ys_ref[...] = jnp.zeros_like(ys_ref)


def expert_ffn(xs, block_expert, n_valid, w_gate, w_up, w_down):
    n_rows = xs.shape[0]
    n_blocks = n_rows // FFN_BLOCK
    wmap = lambda i, be, nv: (be[i], 0, 0)
    return pl.pallas_call(
        _ffn_kernel,
        grid_spec=pltpu.PrefetchScalarGridSpec(
            num_scalar_prefetch=2,
            grid=(n_blocks,),
            in_specs=[pl.BlockSpec((FFN_BLOCK, PACKED_W), lambda i, be, nv: (jnp.minimum(i, nv[0] - 1), 0)),
                      pl.BlockSpec((1, D_MODEL, D_EXPERT), wmap),
                      pl.BlockSpec((1, D_MODEL, D_EXPERT), wmap),
                      pl.BlockSpec((1, D_EXPERT, D_MODEL), wmap)],
            out_specs=pl.BlockSpec((FFN_BLOCK, PACKED_W), lambda i, be, nv: (i, 0)),
            scratch_shapes=[pltpu.VMEM((D_MODEL, D_EXPERT), BF16),
                            pltpu.VMEM((D_MODEL, D_EXPERT), BF16),
                            pltpu.VMEM((D_EXPERT, D_MODEL), BF16)],
        ),
        out_shape=jax.ShapeDtypeStruct((n_rows, PACKED_W), U32),
        compiler_params=_cparams("arbitrary"),
        name="expert_ffn",
    )(block_expert, n_valid, xs, w_gate, w_up, w_down)


def _combine_kernel(plan_ref, x1_ref, rec_ref, gf_ref, ys_ref, out_ref, buf, sem):
    i = pl.program_id(0)
    n_steps = pl.num_programs(0)
    slot = i % 2

    def fetch_tile(tile, s):
        def start_piece(lrow, grow, size):
            pltpu.make_async_copy(ys_ref.at[pl.ds(grow, size)], buf.at[s, pl.ds(lrow, size)], sem.at[s]).start()
        _for_each_run_piece(plan_ref, tile, n_steps, start_piece)

    @pl.when(i == 0)
    def _():
        buf[...] = jnp.zeros_like(buf)
        fetch_tile(0, 0)

    @pl.when(i + 1 < n_steps)
    def _():
        fetch_tile(i + 1, 1 - slot)

    def wait_piece(size):
        pltpu.make_async_copy(ys_ref.at[pl.ds(0, size)], buf.at[slot, pl.ds(0, size)], sem.at[slot]).wait()
    _for_each_total_piece(plan_ref, i, n_steps, wait_piece)

    rowf = lax.broadcasted_iota(I32, (LOCAL_ROWS, ROUTE_TM), 0).astype(F32)
    sel = (jnp.where(rowf == rec_ref[R_SLOT1:R_SLOT1 + 1, :], rec_ref[R_G1:R_G1 + 1, :], 0.0)
           + jnp.where(rowf == rec_ref[R_SLOT2:R_SLOT2 + 1, :], rec_ref[R_G2:R_G2 + 1, :], 0.0)).astype(BF16)
    ya, yb = (t.astype(BF16) for t in _unpack_halves(buf[slot]))
    half = PACKED_W
    tn = (((0,), (0,)), ((), ()))
    xa = x1_ref[:, :half] + lax.dot_general(sel, ya, tn, preferred_element_type=F32)
    xb = x1_ref[:, half:] + lax.dot_general(sel, yb, tn, preferred_element_type=F32)
    ms = (jnp.sum(xa * xa, axis=-1, keepdims=True) + jnp.sum(xb * xb, axis=-1, keepdims=True)) * (1.0 / D_MODEL)
    scale = lax.rsqrt(ms + EPS)
    out_ref[:, :half] = xa * scale * gf_ref[:, :half]
    out_ref[:, half:] = xb * scale * gf_ref[:, half:]


def combine(x1, rec, norm_f_g, ys, plan):
    n = x1.shape[0]
    tm = ROUTE_TM
    return pl.pallas_call(
        _combine_kernel,
        grid_spec=pltpu.PrefetchScalarGridSpec(
            num_scalar_prefetch=1,
            grid=(n // tm,),
            in_specs=[pl.BlockSpec((tm, D_MODEL), lambda i, p: (i, 0)),
                      pl.BlockSpec((8, tm), lambda i, p: (0, i)),
                      pl.BlockSpec((1, D_MODEL), lambda i, p: (0, 0)),
                      pl.BlockSpec(memory_space=pl.ANY)],
            out_specs=pl.BlockSpec((tm, D_MODEL), lambda i, p: (i, 0)),
            scratch_shapes=[pltpu.VMEM((2, LOCAL_ROWS, PACKED_W), U32),
                            pltpu.SemaphoreType.DMA((2,))],
        ),
        out_shape=jax.ShapeDtypeStruct((n, D_MODEL), F32),
        compiler_params=_cparams("arbitrary"),
        name="combine",
    )(plan, x1, rec, norm_f_g.reshape(1, D_MODEL), ys)


def _router_weights(w_group, b_group, w_router, b_router):
    w = jnp.zeros((LANES, D_MODEL), F32)
    w = w.at[:N_GROUPS].set(w_group.T).at[EXPERT_ROW0:EXPERT_ROW0 + N_EXPERTS].set(w_router.T)
    b = jnp.zeros((LANES, 1), F32)
    b = b.at[:N_GROUPS, 0].set(b_group).at[EXPERT_ROW0:EXPERT_ROW0 + N_EXPERTS, 0].set(b_router)
    hi = w.astype(BF16)
    lo = (w - hi.astype(F32)).astype(BF16)
    return hi, lo, b


def _sorted_rows_bound(n_tokens):
    worst = (2 * n_tokens + (n_tokens // ROUTE_TM) * N_EXPERTS * (RUN_ALIGN - 1)
             + N_EXPERTS * (FFN_BLOCK - 1))
    return -(-worst // FFN_BLOCK) * FFN_BLOCK


def _block_plan(seg, n_rows):
    seg_ends = seg[SEG_END, :N_EXPERTS].astype(I32)
    n_blocks = n_rows // FFN_BLOCK
    blk_start = jnp.arange(n_blocks, dtype=I32) * FFN_BLOCK
    block_expert = jnp.minimum(jnp.sum((seg_ends[None, :] <= blk_start[:, None]).astype(I32), axis=1),
                               N_EXPERTS - 1)
    n_valid = seg_ends[-1:] // FFN_BLOCK
    return seg_ends, block_expert, n_valid


def kernel(x, norm1_g, w_in, attn_norm_g, hgrn_gamma, hgrn_norm_g, w_out, norm2_g, w_group, b_group,
           w_router, b_router, w_gate, w_up, w_down, norm_f_g):
    batch, seq, d = x.shape
    assert seq == SEQ and d == D_MODEL and norm1_g.shape[0] == 1
    n = batch * seq
    x2 = x.reshape(n, d)
    qkv, hg = in_proj(x2, norm1_g[0], w_in[0].astype(BF16), batch)
    oa = attention(qkv)
    yh = hgrn(hg.reshape(batch, seq, HG_W), hgrn_gamma, hgrn_norm_g[0]).reshape(n, WIDTH)
    wr_hi, wr_lo, br = _router_weights(w_group[0], b_group[0], w_router[0], b_router[0])
    x1, h2, rec, tab = post_mix(oa, yh, x2, attn_norm_g[0], w_out[0].astype(BF16), norm2_g[0],
                                wr_hi, wr_lo, br)
    plan, seg = run_plan(tab)
    plan = plan.reshape(-1)
    n_rows = _sorted_rows_bound(n)
    seg_ends, block_expert, n_valid = _block_plan(seg, n_rows)
    xs = dispatch(h2, rec, plan, seg_ends, n_valid, n_rows)
    ys = expert_ffn(xs, block_expert, n_valid, w_gate[0], w_up[0], w_down[0])
    return combine(x1, rec, norm_f_g, ys, plan).reshape(batch, seq, d)
```
